```python
import math
import jax, jax.numpy as jnp
from jax import lax
import numpy as np

D_MODEL = 2048
BATCH = 8
SEQ = 8192
DEPTH = 2

MIX_WIDTH = D_MODEL
S5_WIDTH = MIX_WIDTH // 4
SGU_WIDTH = MIX_WIDTH // 2
POOL_WIDTH = MIX_WIDTH - S5_WIDTH - SGU_WIDTH
S5_GROUP_CH = 16
S5_GROUPS = S5_WIDTH // S5_GROUP_CH
S5_STATE = 64
DT_MIN = 0.001
DT_MAX = 0.1
CHUNK = 128
SGU_HEAD_DIM = 128
SGU_HEADS = SGU_WIDTH // SGU_HEAD_DIM
POOL_WINDOWS = (2, 4, 8, 16)
POOL_GROUPS = len(POOL_WINDOWS)
POOL_GROUP_CH = POOL_WIDTH // POOL_GROUPS
SPLIT_SIZES = (S5_WIDTH, SGU_WIDTH, SGU_WIDTH, POOL_WIDTH, S5_WIDTH, SGU_WIDTH, POOL_WIDTH)
IN_COLS = sum(SPLIT_SIZES)
SPLIT_POINTS = tuple(int(s) for s in np.cumsum(SPLIT_SIZES)[:-1])
RMS_EPS = 1e-6
LN_EPS = 1e-5

kernel_name = 'hymba_style_s5_gmlp_pool_hybrid'


def rms_norm(x, g):
    xf = x.astype(jnp.float32)
    y = xf * lax.rsqrt(jnp.mean(xf * xf, axis=-1, keepdims=True) + RMS_EPS)
    return (y * g.astype(jnp.float32)).astype(x.dtype)


def s5_mixer(xa, lam_re, lam_im, b_re, b_im, c_re, c_im, d_skip, log_dt, w_glu, b_glu):
    bsz, seq, _ = xa.shape
    f32 = jnp.float32
    xg = xa.astype(f32).reshape(bsz, seq, S5_GROUPS, S5_GROUP_CH)
    lam = lax.complex(lam_re.astype(f32), lam_im.astype(f32))
    dt = jnp.exp(log_dt.astype(f32))[:, None]
    lam_bar = jnp.exp(lam * dt)
    b = lax.complex(b_re.astype(f32), b_im.astype(f32))
    b_bar = ((lam_bar - 1.0) / lam)[..., None] * b
    c = lax.complex(c_re.astype(f32), c_im.astype(f32))
    bu = jnp.einsum('blgh,gph->blgp', xg.astype(jnp.complex64), b_bar)
    a = jnp.broadcast_to(lam_bar, (1, seq) + lam_bar.shape)

    def combine(left, right):
        a_l, b_l = left
        a_r, b_r = right
        return a_r * a_l, a_r * b_l + b_r

    _, states = lax.associative_scan(combine, (a, bu), axis=1)
    y = jnp.einsum('blgp,ghp->blgh', states, c).real + d_skip.astype(f32) * xg
    y = jax.nn.gelu(y.reshape(bsz, seq, S5_WIDTH)).astype(xa.dtype)
    return y * jax.nn.sigmoid(y @ w_glu + b_glu)


def sgu_mixer(u, v, ln_g, ln_b, w_s, b_s):
    bsz, seq, _ = v.shape
    u = jax.nn.gelu(u)
    vf = jax.nn.gelu(v).astype(jnp.float32)
    mu = jnp.mean(vf, axis=-1, keepdims=True)
    var = jnp.mean(jnp.square(vf - mu), axis=-1, keepdims=True)
    vn = ((vf - mu) * lax.rsqrt(var + LN_EPS) * ln_g.astype(jnp.float32)
          + ln_b.astype(jnp.float32)).astype(v.dtype)
    vn = vn.reshape(bsz, seq // CHUNK, CHUNK, SGU_HEADS, SGU_HEAD_DIM)
    causal = jnp.tril(jnp.ones((CHUNK, CHUNK), dtype=bool))
    ws = jnp.where(causal[None], w_s, jnp.zeros_like(w_s))
    s = jnp.einsum('hts,bcshd->bcthd', ws, vn) + jnp.transpose(b_s)[:, :, None]
    return u * s.reshape(bsz, seq, SGU_WIDTH)


def pool_mixer(xc, w_pool, pool_scale):
    bsz, seq, _ = xc.shape
    xg = xc.astype(jnp.float32).reshape(bsz, seq, POOL_GROUPS, POOL_GROUP_CH)
    cs = jnp.cumsum(xg, axis=1)
    pos = jnp.arange(1, seq + 1, dtype=jnp.float32)[None, :, None]
    outs = []
    for g, w in enumerate(POOL_WINDOWS):
        c = cs[:, :, g]
        lagged = jnp.pad(c[:, :seq - w], ((0, 0), (w, 0), (0, 0)))
        mean = (c - lagged) / jnp.minimum(pos, float(w))
        outs.append(mean - xg[:, :, g])
    p = jnp.stack(outs, axis=2).astype(xc.dtype)
    y = jnp.einsum('blgc,gcd->blgd', p, w_pool).reshape(bsz, seq, POOL_WIDTH)
    return y * pool_scale


def _fwd_setup_inputs(seed: int = 0) -> dict:
    key = jax.random.key(seed)
    ks = jax.random.split(key, 24)
    f32 = jnp.float32
    nrm = lambda k, shape: jax.random.normal(k, shape, dtype=f32)
    L, D = DEPTH, D_MODEL
    G, P, H = S5_GROUPS, S5_STATE, S5_GROUP_CH
    x = nrm(ks[0], (BATCH, SEQ, D))
    norm_g = 1.0 + 0.02 * nrm(ks[1], (L, D))
    w_in = nrm(ks[2], (L, D, IN_COLS)) * D ** -0.5
    lam_re = -0.5 + 0.01 * nrm(ks[3], (L, G, P))
    lam_im = math.pi * jnp.arange(P, dtype=f32)[None, None, :] + 0.01 * nrm(ks[4], (L, G, P))
    b_re = nrm(ks[5], (L, G, P, H)) * (2.0 * H) ** -0.5
    b_im = nrm(ks[6], (L, G, P, H)) * (2.0 * H) ** -0.5
    c_re = nrm(ks[7], (L, G, H, P)) * P ** -0.5
    c_im = nrm(ks[8], (L, G, H, P)) * P ** -0.5
    d_skip = nrm(ks[9], (L, G, H))
    log_dt = jax.random.uniform(ks[10], (L, G), dtype=f32,
                                minval=math.log(DT_MIN), maxval=math.log(DT_MAX))
    w_glu = nrm(ks[11], (L, S5_WIDTH, S5_WIDTH)) * S5_WIDTH ** -0.5
    b_glu = 0.01 * nrm(ks[12], (L, S5_WIDTH))
    ln_g = 1.0 + 0.02 * nrm(ks[13], (L, SGU_WIDTH))
    ln_b = 0.02 * nrm(ks[14], (L, SGU_WIDTH))
    w_s = nrm(ks[15], (L, SGU_HEADS, CHUNK, CHUNK)) * CHUNK ** -0.5
    b_s = 1.0 + 0.02 * nrm(ks[16], (L, SGU_HEADS, CHUNK))
    w_pool = nrm(ks[17], (L, POOL_GROUPS, POOL_GROUP_CH, POOL_GROUP_CH)) * POOL_GROUP_CH ** -0.5
    pool_scale = 1.0 + 0.1 * nrm(ks[18], (L, POOL_WIDTH))
    w_out = nrm(ks[19], (L, MIX_WIDTH, D)) * MIX_WIDTH ** -0.5
    final_g = 1.0 + 0.02 * nrm(ks[20], (D,))
    return {'x': x, 'norm_g': norm_g, 'w_in': w_in, 'lam_re': lam_re, 'lam_im': lam_im,
            'b_re': b_re, 'b_im': b_im, 'c_re': c_re, 'c_im': c_im, 'd_skip': d_skip,
            'log_dt': log_dt, 'w_glu': w_glu, 'b_glu': b_glu, 'ln_g': ln_g, 'ln_b': ln_b,
            'w_s': w_s, 'b_s': b_s, 'w_pool': w_pool, 'pool_scale': pool_scale,
            'w_out': w_out, 'final_g': final_g}


def _fwd_reference(x, norm_g, w_in, lam_re, lam_im, b_re, b_im, c_re, c_im, d_skip, log_dt,
              w_glu, b_glu, ln_g, ln_b, w_s, b_s, w_pool, pool_scale, w_out, final_g):
    for i in range(DEPTH):
        h = rms_norm(x, norm_g[i])
        z = h @ w_in[i]
        xa, u, v, xc, ga, gb, gc = jnp.split(z, SPLIT_POINTS, axis=-1)
        ya = s5_mixer(xa, lam_re[i], lam_im[i], b_re[i], b_im[i], c_re[i], c_im[i],
                      d_skip[i], log_dt[i], w_glu[i], b_glu[i]) * jax.nn.silu(ga)
        yb = sgu_mixer(u, v, ln_g[i], ln_b[i], w_s[i], b_s[i]) * jax.nn.silu(gb)
        yc = pool_mixer(xc, w_pool[i], pool_scale[i]) * jax.nn.silu(gc)
        y = jnp.concatenate([ya.astype(x.dtype), yb.astype(x.dtype), yc.astype(x.dtype)], axis=-1)
        x = x + y @ w_out[i]
    return rms_norm(x, final_g)


import jax as _jax
import jax.numpy as _jnp

TWIN_FORMAT = 'train_step'
FWD_PARAMS = ['x', 'norm_g', 'w_in', 'lam_re', 'lam_im', 'b_re', 'b_im', 'c_re', 'c_im', 'd_skip', 'log_dt', 'w_glu', 'b_glu', 'ln_g', 'ln_b', 'w_s', 'b_s', 'w_pool', 'pool_scale', 'w_out', 'final_g']
TWIN_WEIGHTS = ['norm_g', 'w_in', 'lam_re', 'lam_im', 'b_re', 'b_im', 'c_re', 'c_im', 'd_skip', 'log_dt', 'w_glu', 'b_glu', 'ln_g', 'ln_b', 'w_s', 'b_s', 'w_pool', 'pool_scale', 'w_out', 'final_g']
TWIN_DIFF_INPUT = 'x'
TWIN_INPUTS = ['x', 'norm_g', 'w_in', 'lam_re', 'lam_im', 'b_re', 'b_im', 'c_re', 'c_im', 'd_skip', 'log_dt', 'w_glu', 'b_glu', 'ln_g', 'ln_b', 'w_s', 'b_s', 'w_pool', 'pool_scale', 'w_out', 'final_g', 'loss_target', 'm_norm_g', 'm_w_in', 'm_lam_re', 'm_lam_im', 'm_b_re', 'm_b_im', 'm_c_re', 'm_c_im', 'm_d_skip', 'm_log_dt', 'm_w_glu', 'm_b_glu', 'm_ln_g', 'm_ln_b', 'm_w_s', 'm_b_s', 'm_w_pool', 'm_pool_scale', 'm_w_out', 'm_final_g', 'v_norm_g', 'v_w_in', 'v_lam_re', 'v_lam_im', 'v_b_re', 'v_b_im', 'v_c_re', 'v_c_im', 'v_d_skip', 'v_log_dt', 'v_w_glu', 'v_b_glu', 'v_ln_g', 'v_ln_b', 'v_w_s', 'v_b_s', 'v_w_pool', 'v_pool_scale', 'v_w_out', 'v_final_g']
TWIN_OUTPUTS = ['loss', 'grad_x', 'grad_norm_g', 'grad_w_in', 'grad_lam_re', 'grad_lam_im', 'grad_b_re', 'grad_b_im', 'grad_c_re', 'grad_c_im', 'grad_d_skip', 'grad_log_dt', 'grad_w_glu', 'grad_b_glu', 'grad_ln_g', 'grad_ln_b', 'grad_w_s', 'grad_b_s', 'grad_w_pool', 'grad_pool_scale', 'grad_w_out', 'grad_final_g', 'delta_norm_g', 'delta_w_in', 'delta_lam_re', 'delta_lam_im', 'delta_b_re', 'delta_b_im', 'delta_c_re', 'delta_c_im', 'delta_d_skip', 'delta_log_dt', 'delta_w_glu', 'delta_b_glu', 'delta_ln_g', 'delta_ln_b', 'delta_w_s', 'delta_b_s', 'delta_w_pool', 'delta_pool_scale', 'delta_w_out', 'delta_final_g', 'new_m_norm_g', 'new_m_w_in', 'new_m_lam_re', 'new_m_lam_im', 'new_m_b_re', 'new_m_b_im', 'new_m_c_re', 'new_m_c_im', 'new_m_d_skip', 'new_m_log_dt', 'new_m_w_glu', 'new_m_b_glu', 'new_m_ln_g', 'new_m_ln_b', 'new_m_w_s', 'new_m_b_s', 'new_m_w_pool', 'new_m_pool_scale', 'new_m_w_out', 'new_m_final_g', 'new_v_norm_g', 'new_v_w_in', 'new_v_lam_re', 'new_v_lam_im', 'new_v_b_re', 'new_v_b_im', 'new_v_c_re', 'new_v_c_im', 'new_v_d_skip', 'new_v_log_dt', 'new_v_w_glu', 'new_v_b_glu', 'new_v_ln_g', 'new_v_ln_b', 'new_v_w_s', 'new_v_b_s', 'new_v_w_pool', 'new_v_pool_scale', 'new_v_w_out', 'new_v_final_g']
TWIN_LEAF_KINDS = {'loss': 'loss', 'grad_x': 'grad_x', 'grad_norm_g': 'grad_w', 'grad_w_in': 'grad_w', 'grad_lam_re': 'grad_w', 'grad_lam_im': 'grad_w', 'grad_b_re': 'grad_w', 'grad_b_im': 'grad_w', 'grad_c_re': 'grad_w', 'grad_c_im': 'grad_w', 'grad_d_skip': 'grad_w', 'grad_log_dt': 'grad_w', 'grad_w_glu': 'grad_w', 'grad_b_glu': 'grad_w', 'grad_ln_g': 'grad_w', 'grad_ln_b': 'grad_w', 'grad_w_s': 'grad_w', 'grad_b_s': 'grad_w', 'grad_w_pool': 'grad_w', 'grad_pool_scale': 'grad_w', 'grad_w_out': 'grad_w', 'grad_final_g': 'grad_w', 'delta_norm_g': 'delta_w', 'delta_w_in': 'delta_w', 'delta_lam_re': 'delta_w', 'delta_lam_im': 'delta_w', 'delta_b_re': 'delta_w', 'delta_b_im': 'delta_w', 'delta_c_re': 'delta_w', 'delta_c_im': 'delta_w', 'delta_d_skip': 'delta_w', 'delta_log_dt': 'delta_w', 'delta_w_glu': 'delta_w', 'delta_b_glu': 'delta_w', 'delta_ln_g': 'delta_w', 'delta_ln_b': 'delta_w', 'delta_w_s': 'delta_w', 'delta_b_s': 'delta_w', 'delta_w_pool': 'delta_w', 'delta_pool_scale': 'delta_w', 'delta_w_out': 'delta_w', 'delta_final_g': 'delta_w', 'new_m_norm_g': 'new_m', 'new_m_w_in': 'new_m', 'new_m_lam_re': 'new_m', 'new_m_lam_im': 'new_m', 'new_m_b_re': 'new_m', 'new_m_b_im': 'new_m', 'new_m_c_re': 'new_m', 'new_m_c_im': 'new_m', 'new_m_d_skip': 'new_m', 'new_m_log_dt': 'new_m', 'new_m_w_glu': 'new_m', 'new_m_b_glu': 'new_m', 'new_m_ln_g': 'new_m', 'new_m_ln_b': 'new_m', 'new_m_w_s': 'new_m', 'new_m_b_s': 'new_m', 'new_m_w_pool': 'new_m', 'new_m_pool_scale': 'new_m', 'new_m_w_out': 'new_m', 'new_m_final_g': 'new_m', 'new_v_norm_g': 'new_v', 'new_v_w_in': 'new_v', 'new_v_lam_re': 'new_v', 'new_v_lam_im': 'new_v', 'new_v_b_re': 'new_v', 'new_v_b_im': 'new_v', 'new_v_c_re': 'new_v', 'new_v_c_im': 'new_v', 'new_v_d_skip': 'new_v', 'new_v_log_dt': 'new_v', 'new_v_w_glu': 'new_v', 'new_v_b_glu': 'new_v', 'new_v_ln_g': 'new_v', 'new_v_ln_b': 'new_v', 'new_v_w_s': 'new_v', 'new_v_b_s': 'new_v', 'new_v_w_pool': 'new_v', 'new_v_pool_scale': 'new_v', 'new_v_w_out': 'new_v', 'new_v_final_g': 'new_v'}


def _forward(args):
    return _fwd_reference(*[args[k] for k in FWD_PARAMS])


def _output_shape():
    def fwd():
        inp = _fwd_setup_inputs(0)
        return _fwd_reference(*[inp[k] for k in FWD_PARAMS])
    out = _jax.eval_shape(fwd)
    return out.shape, out.dtype

N_MICROBATCH = 1
ADAM_LR = 0.001
ADAM_B1 = 0.9
ADAM_B2 = 0.999
ADAM_EPS = 1e-08
ADAM_WD = 0.01
ADAM_STEP = 10
PER_EXAMPLE_BATCH_AXIS = {'x': 0, 'loss_target': 0}
SHARED_INPUTS = []
_WEIGHT_DTYPES = {'norm_g': _jnp.float32, 'w_in': _jnp.float32, 'lam_re': _jnp.float32, 'lam_im': _jnp.float32, 'b_re': _jnp.float32, 'b_im': _jnp.float32, 'c_re': _jnp.float32, 'c_im': _jnp.float32, 'd_skip': _jnp.float32, 'log_dt': _jnp.float32, 'w_glu': _jnp.float32, 'b_glu': _jnp.float32, 'ln_g': _jnp.float32, 'ln_b': _jnp.float32, 'w_s': _jnp.float32, 'b_s': _jnp.float32, 'w_pool': _jnp.float32, 'pool_scale': _jnp.float32, 'w_out': _jnp.float32, 'final_g': _jnp.float32}
MOMENT_SCALE = {'norm_g': 7.837555e-02, 'w_in': 4.966553e-02, 'lam_re': 1.950973e-03, 'lam_im': 1.961753e-03, 'b_re': 1.177395e-03, 'b_im': 1.197166e-03, 'c_re': 1.672849e-03, 'c_im': 1.687138e-03, 'd_skip': 2.846054e-02, 'log_dt': 8.270896e-01, 'w_glu': 7.217579e-03, 'b_glu': 1.116787e-02, 'ln_g': 3.094527e-02, 'ln_b': 3.212510e-02, 'w_s': 3.185233e-02, 'b_s': 4.514229e-02, 'w_pool': 6.155381e-02, 'pool_scale': 6.091467e-02, 'w_out': 5.099370e-02, 'final_g': 3.200711e+01}


def _to_microbatches(a, axis):
    t = _jnp.moveaxis(a, axis, 0)
    t = t.reshape((N_MICROBATCH, t.shape[0] // N_MICROBATCH) + t.shape[1:])
    return _jnp.moveaxis(t, 1, axis + 1)


def setup_inputs(seed: int = 0) -> dict:
    inp = _fwd_setup_inputs(seed)
    key = _jax.random.fold_in(_jax.random.key(seed), 7919)
    shape, _ = _output_shape()
    out = dict(inp)
    out["loss_target"] = _jax.random.normal(_jax.random.fold_in(key, 0), shape, _jnp.float32)
    for i, name in enumerate(TWIN_WEIGHTS):
        w = inp[name].astype(_jnp.float32)
        if MOMENT_SCALE is None:
            s = _jnp.sqrt(_jnp.mean(_jnp.square(w)) + 1e-30)
        else:
            s = MOMENT_SCALE[name]
        km, kv = _jax.random.split(_jax.random.fold_in(key, i + 1))
        out[name] = w
        out["m_" + name] = s * _jax.random.normal(km, w.shape, _jnp.float32)
        out["v_" + name] = (s * s) * _jax.random.uniform(kv, w.shape, _jnp.float32, 0.5, 1.5)
    if N_MICROBATCH > 1:
        for name, axis in PER_EXAMPLE_BATCH_AXIS.items():
            out[name] = _to_microbatches(out[name], axis)
    return {'x': out['x'], 'norm_g': out['norm_g'], 'w_in': out['w_in'], 'lam_re': out['lam_re'], 'lam_im': out['lam_im'], 'b_re': out['b_re'], 'b_im': out['b_im'], 'c_re': out['c_re'], 'c_im': out['c_im'], 'd_skip': out['d_skip'], 'log_dt': out['log_dt'], 'w_glu': out['w_glu'], 'b_glu': out['b_glu'], 'ln_g': out['ln_g'], 'ln_b': out['ln_b'], 'w_s': out['w_s'], 'b_s': out['b_s'], 'w_pool': out['w_pool'], 'pool_scale': out['pool_scale'], 'w_out': out['w_out'], 'final_g': out['final_g'], 'loss_target': out['loss_target'], 'm_norm_g': out['m_norm_g'], 'm_w_in': out['m_w_in'], 'm_lam_re': out['m_lam_re'], 'm_lam_im': out['m_lam_im'], 'm_b_re': out['m_b_re'], 'm_b_im': out['m_b_im'], 'm_c_re': out['m_c_re'], 'm_c_im': out['m_c_im'], 'm_d_skip': out['m_d_skip'], 'm_log_dt': out['m_log_dt'], 'm_w_glu': out['m_w_glu'], 'm_b_glu': out['m_b_glu'], 'm_ln_g': out['m_ln_g'], 'm_ln_b': out['m_ln_b'], 'm_w_s': out['m_w_s'], 'm_b_s': out['m_b_s'], 'm_w_pool': out['m_w_pool'], 'm_pool_scale': out['m_pool_scale'], 'm_w_out': out['m_w_out'], 'm_final_g': out['m_final_g'], 'v_norm_g': out['v_norm_g'], 'v_w_in': out['v_w_in'], 'v_lam_re': out['v_lam_re'], 'v_lam_im': out['v_lam_im'], 'v_b_re': out['v_b_re'], 'v_b_im': out['v_b_im'], 'v_c_re': out['v_c_re'], 'v_c_im': out['v_c_im'], 'v_d_skip': out['v_d_skip'], 'v_log_dt': out['v_log_dt'], 'v_w_glu': out['v_w_glu'], 'v_b_glu': out['v_b_glu'], 'v_ln_g': out['v_ln_g'], 'v_ln_b': out['v_ln_b'], 'v_w_s': out['v_w_s'], 'v_b_s': out['v_b_s'], 'v_w_pool': out['v_w_pool'], 'v_pool_scale': out['v_pool_scale'], 'v_w_out': out['v_w_out'], 'v_final_g': out['v_final_g']}


def _loss(weights, diff, rest, loss_target):
    with _jax.named_scope("forward"):
        args = {**rest, TWIN_DIFF_INPUT: diff, **{k: w.astype(_WEIGHT_DTYPES[k]) for k, w in weights.items()}}
        y = _forward(args)
    with _jax.named_scope("loss_head"):
        err = _jnp.square(y.astype(_jnp.float32) - loss_target)
        return 0.5 * _jnp.sum(_jnp.mean(err, axis=-1)) if err.ndim else 0.5 * err


def _adamw(w, g, m, v):
    m = ADAM_B1 * m + (1.0 - ADAM_B1) * g
    v = ADAM_B2 * v + (1.0 - ADAM_B2) * _jnp.square(g)
    m_hat = m / (1.0 - ADAM_B1 ** ADAM_STEP)
    v_hat = v / (1.0 - ADAM_B2 ** ADAM_STEP)
    delta = -ADAM_LR * (m_hat / (_jnp.sqrt(v_hat) + ADAM_EPS) + ADAM_WD * w)
    return delta, m, v


def reference(x, norm_g, w_in, lam_re, lam_im, b_re, b_im, c_re, c_im, d_skip, log_dt, w_glu, b_glu, ln_g, ln_b, w_s, b_s, w_pool, pool_scale, w_out, final_g, loss_target, m_norm_g, m_w_in, m_lam_re, m_lam_im, m_b_re, m_b_im, m_c_re, m_c_im, m_d_skip, m_log_dt, m_w_glu, m_b_glu, m_ln_g, m_ln_b, m_w_s, m_b_s, m_w_pool, m_pool_scale, m_w_out, m_final_g, v_norm_g, v_w_in, v_lam_re, v_lam_im, v_b_re, v_b_im, v_c_re, v_c_im, v_d_skip, v_log_dt, v_w_glu, v_b_glu, v_ln_g, v_ln_b, v_w_s, v_b_s, v_w_pool, v_pool_scale, v_w_out, v_final_g):
    given = dict(x=x, norm_g=norm_g, w_in=w_in, lam_re=lam_re, lam_im=lam_im, b_re=b_re, b_im=b_im, c_re=c_re, c_im=c_im, d_skip=d_skip, log_dt=log_dt, w_glu=w_glu, b_glu=b_glu, ln_g=ln_g, ln_b=ln_b, w_s=w_s, b_s=b_s, w_pool=w_pool, pool_scale=pool_scale, w_out=w_out, final_g=final_g, loss_target=loss_target, m_norm_g=m_norm_g, m_w_in=m_w_in, m_lam_re=m_lam_re, m_lam_im=m_lam_im, m_b_re=m_b_re, m_b_im=m_b_im, m_c_re=m_c_re, m_c_im=m_c_im, m_d_skip=m_d_skip, m_log_dt=m_log_dt, m_w_glu=m_w_glu, m_b_glu=m_b_glu, m_ln_g=m_ln_g, m_ln_b=m_ln_b, m_w_s=m_w_s, m_b_s=m_b_s, m_w_pool=m_w_pool, m_pool_scale=m_pool_scale, m_w_out=m_w_out, m_final_g=m_final_g, v_norm_g=v_norm_g, v_w_in=v_w_in, v_lam_re=v_lam_re, v_lam_im=v_lam_im, v_b_re=v_b_re, v_b_im=v_b_im, v_c_re=v_c_re, v_c_im=v_c_im, v_d_skip=v_d_skip, v_log_dt=v_log_dt, v_w_glu=v_w_glu, v_b_glu=v_b_glu, v_ln_g=v_ln_g, v_ln_b=v_ln_b, v_w_s=v_w_s, v_b_s=v_b_s, v_w_pool=v_w_pool, v_pool_scale=v_pool_scale, v_w_out=v_w_out, v_final_g=v_final_g)
    weights = {n: given[n] for n in TWIN_WEIGHTS}
    shared = {n: given[n] for n in SHARED_INPUTS}
    per_example = {n: given[n] for n in ['x']}
    grad_fn = _jax.value_and_grad(_loss, argnums=(0, 1))

    def one_microbatch(ex, loss_target):
        ex = dict(ex)
        diff = ex.pop(TWIN_DIFF_INPUT)
        return grad_fn(weights, diff, {**shared, **ex}, loss_target)

    if N_MICROBATCH == 1:
        loss, (grad_w, grad_x) = one_microbatch(per_example, given["loss_target"])
    else:
        def body(carry, xs):
            loss_sum, grad_sum = carry
            l_k, (gw_k, gx_k) = one_microbatch(xs[0], xs[1])
            with _jax.named_scope("update"):
                return (loss_sum + l_k, _jax.tree.map(_jnp.add, grad_sum, gw_k)), gx_k

        init = (_jnp.zeros((), _jnp.float32), _jax.tree.map(_jnp.zeros_like, weights))
        (loss, grad_w), grad_x = _jax.lax.scan(body, init, (per_example, given["loss_target"]))
    with _jax.named_scope("update"):
        delta_w, new_m, new_v = {}, {}, {}
        for n in TWIN_WEIGHTS:
            delta_w[n], new_m[n], new_v[n] = _adamw(weights[n], grad_w[n], given["m_" + n], given["v_" + n])
    return (loss, grad_x, *[grad_w[n] for n in TWIN_WEIGHTS], *[delta_w[n] for n in TWIN_WEIGHTS],
            *[new_m[n] for n in TWIN_WEIGHTS], *[new_v[n] for n in TWIN_WEIGHTS])
```

```python
import functools
import math

import jax
import jax.numpy as jnp
from jax import lax
from jax.experimental import pallas as pl
from jax.experimental.pallas import tpu as pltpu

F32 = jnp.float32
BF16 = jnp.bfloat16

D_MODEL = 2048
DEPTH = 2
S5_W = 512
SGU_W = 1024
POOL_W = 512
IN_COLS = 5120
N_CHIPS = 4
SHARD_COLS = IN_COLS // N_CHIPS
S5_GROUPS = 32
S5_STATE = 64
S5_CH = 16
STATE_W = S5_GROUPS * S5_STATE
SUPER = 4
CHUNK = 128
SGU_HEADS = 8
POOL_WINDOWS = (2, 4, 8, 16)
POOL_HALO = 16
RMS_EPS = 1e-6
LN_EPS = 1e-5
SCAN_COLS = 512

ADAM_LR = 0.001
ADAM_B1 = 0.9
ADAM_B2 = 0.999
ADAM_EPS = 1e-08
ADAM_WD = 0.01
ADAM_STEP = 10

VMEM_LIMIT = 56 * 1024 * 1024
MESH_ID = pl.DeviceIdType.MESH

_GELU_K0 = math.sqrt(2.0 / math.pi)
_GELU_K1 = 0.044715


def _cparams(n_axes):
    return pltpu.CompilerParams(dimension_semantics=("arbitrary",) * n_axes, vmem_limit_bytes=VMEM_LIMIT)


def _gelu(x):
    t = jnp.tanh(_GELU_K0 * (x + _GELU_K1 * (x * x * x)))
    return 0.5 * x * (1.0 + t)


def _gelu_and_grad(x):
    x2 = x * x
    t = jnp.tanh(_GELU_K0 * (x + _GELU_K1 * (x * x2)))
    g = 0.5 * x * (1.0 + t)
    dg = 0.5 * (1.0 + t) + 0.5 * x * (1.0 - t * t) * (_GELU_K0 * (1.0 + 3.0 * _GELU_K1 * x2))
    return g, dg


def _silu_and_grad(x):
    s = jax.nn.sigmoid(x)
    return x * s, s * (1.0 + x * (1.0 - s))


def _dot(a, b):
    return jnp.dot(a.astype(BF16), b.astype(BF16), preferred_element_type=F32)


def _dot_nt(a, b):
    return lax.dot_general(a.astype(BF16), b.astype(BF16), (((1,), (1,)), ((), ())), preferred_element_type=F32)


def _dot_tn(a, b):
    return lax.dot_general(a.astype(BF16), b.astype(BF16), (((0,), (0,)), ((), ())), preferred_element_type=F32)


def _full(shape):
    nd = len(shape)
    return pl.BlockSpec(shape, lambda *_: (0,) * nd)


def _inproj(x, g, w4):
    T = x.shape[0]
    tm = min(512, T)
    tn = SHARD_COLS // 2
    per = SHARD_COLS // tn

    def body(x_ref, g_ref, w_ref, z_ref, h_ref, hs_ref):
        @pl.when(pl.program_id(1) == 0)
        def _():
            xv = x_ref[...]
            r = lax.rsqrt(jnp.mean(xv * xv, axis=-1, keepdims=True) + RMS_EPS)
            hv = (xv * r * g_ref[...]).astype(BF16)
            hs_ref[...] = hv
            h_ref[...] = hv

        z_ref[...] = jnp.dot(hs_ref[...], w_ref[...], preferred_element_type=F32)

    return pl.pallas_call(
        body,
        name="inproj",
        grid=(T // tm, IN_COLS // tn),
        in_specs=[
            pl.BlockSpec((tm, D_MODEL), lambda i, j: (i, 0)),
            pl.BlockSpec((1, D_MODEL), lambda i, j: (0, 0)),
            pl.BlockSpec((None, D_MODEL, tn), lambda i, j: (j // per, 0, j % per)),
        ],
        out_specs=[
            pl.BlockSpec((tm, tn), lambda i, j: (i, j)),
            pl.BlockSpec((tm, D_MODEL), lambda i, j: (i, 0)),
        ],
        out_shape=[jax.ShapeDtypeStruct((T, IN_COLS), F32), jax.ShapeDtypeStruct((T, D_MODEL), BF16)],
        scratch_shapes=[pltpu.VMEM((tm, D_MODEL), BF16)],
        compiler_params=_cparams(2),
    )(x, g, w4)


def _outproj(ya, yb, yc, w, x):
    T = x.shape[0]
    tm = min(512, T)
    tn = 1024

    def body(ya_ref, yb_ref, yc_ref, w_ref, x_ref, o_ref, y_ref):
        acc = jnp.dot(ya_ref[...], w_ref[0:S5_W, :], preferred_element_type=F32)
        acc += jnp.dot(yb_ref[...], w_ref[S5_W:S5_W + SGU_W, :], preferred_element_type=F32)
        acc += jnp.dot(yc_ref[...], w_ref[S5_W + SGU_W:D_MODEL, :], preferred_element_type=F32)
        o_ref[...] = x_ref[...] + acc

        @pl.when(pl.program_id(1) == 0)
        def _():
            y_ref[:, 0:S5_W] = ya_ref[...]
            y_ref[:, S5_W:S5_W + SGU_W] = yb_ref[...]
            y_ref[:, S5_W + SGU_W:D_MODEL] = yc_ref[...]

    return pl.pallas_call(
        body,
        name="outproj",
        grid=(T // tm, D_MODEL // tn),
        in_specs=[
            pl.BlockSpec((tm, S5_W), lambda i, j: (i, 0)),
            pl.BlockSpec((tm, SGU_W), lambda i, j: (i, 0)),
            pl.BlockSpec((tm, POOL_W), lambda i, j: (i, 0)),
            pl.BlockSpec((D_MODEL, tn), lambda i, j: (0, j)),
            pl.BlockSpec((tm, tn), lambda i, j: (i, j)),
        ],
        out_specs=[
            pl.BlockSpec((tm, tn), lambda i, j: (i, j)),
            pl.BlockSpec((tm, D_MODEL), lambda i, j: (i, 0)),
        ],
        out_shape=[jax.ShapeDtypeStruct((T, D_MODEL), F32), jax.ShapeDtypeStruct((T, D_MODEL), BF16)],
        compiler_params=_cparams(2),
    )(ya, yb, yc, w, x)


def _outproj_bwd_dy(dxo, w):
    T = dxo.shape[0]
    tm = min(512, T)
    tn = 1024

    def body(d_ref, w_ref, o_ref, ds_ref):
        @pl.when(pl.program_id(1) == 0)
        def _():
            ds_ref[...] = d_ref[...].astype(BF16)

        o_ref[...] = lax.dot_general(ds_ref[...], w_ref[...], (((1,), (1,)), ((), ())), preferred_element_type=F32)

    return pl.pallas_call(
        body,
        name="outproj_bwd_dy",
        grid=(T // tm, D_MODEL // tn),
        in_specs=[
            pl.BlockSpec((tm, D_MODEL), lambda i, j: (i, 0)),
            pl.BlockSpec((tn, D_MODEL), lambda i, j: (j, 0)),
        ],
        out_specs=pl.BlockSpec((tm, tn), lambda i, j: (i, j)),
        out_shape=jax.ShapeDtypeStruct((T, D_MODEL), F32),
        scratch_shapes=[pltpu.VMEM((tm, D_MODEL), BF16)],
        compiler_params=_cparams(2),
    )(dxo, w)


def _outproj_bwd_dw(y, dxo):
    T = y.shape[0]
    tm = min(512, T)
    tr = 1024

    def body(y_ref, d_ref, o_ref):
        @pl.when(pl.program_id(1) == 0)
        def _():
            o_ref[...] = jnp.zeros_like(o_ref)

        o_ref[...] += _dot_tn(y_ref[...], d_ref[...])

    return pl.pallas_call(
        body,
        name="outproj_bwd_dw",
        grid=(D_MODEL // tr, T // tm),
        in_specs=[
            pl.BlockSpec((tm, tr), lambda p, t: (t, p)),
            pl.BlockSpec((tm, D_MODEL), lambda p, t: (t, 0)),
        ],
        out_specs=pl.BlockSpec((tr, D_MODEL), lambda p, t: (p, 0)),
        out_shape=jax.ShapeDtypeStruct((D_MODEL, D_MODEL), F32),
        compiler_params=_cparams(2),
    )(y, dxo)


def _inproj_bwd_dw(h, dz):
    T = h.shape[0]
    tm = min(512, T)

    def body(h_ref, dz_ref, o_ref):
        @pl.when(pl.program_id(1) == 0)
        def _():
            o_ref[...] = jnp.zeros_like(o_ref)

        o_ref[...] += _dot_tn(h_ref[...], dz_ref[...])

    return pl.pallas_call(
        body,
        name="inproj_bwd_dw",
        grid=(N_CHIPS, T // tm),
        in_specs=[
            pl.BlockSpec((tm, D_MODEL), lambda j, t: (t, 0)),
            pl.BlockSpec((tm, SHARD_COLS), lambda j, t: (t, j)),
        ],
        out_specs=pl.BlockSpec((None, D_MODEL, SHARD_COLS), lambda j, t: (j, 0, 0)),
        out_shape=jax.ShapeDtypeStruct((N_CHIPS, D_MODEL, SHARD_COLS), F32),
        compiler_params=_cparams(2),
    )(h, dz)


def _inproj_bwd_dx(dz, w4, x, g, dxo):
    T = x.shape[0]
    tm = min(512, T)
    tk = SHARD_COLS // 2
    per = SHARD_COLS // tk
    nk = IN_COLS // tk

    def body(dz_ref, w_ref, x_ref, g_ref, dxo_ref, dx_ref, dg_ref, acc_ref):
        i, j = pl.program_id(0), pl.program_id(1)

        @pl.when(j == 0)
        def _():
            acc_ref[...] = jnp.zeros_like(acc_ref)

        acc_ref[...] += lax.dot_general(dz_ref[...], w_ref[...], (((1,), (1,)), ((), ())), preferred_element_type=F32)

        @pl.when(j == nk - 1)
        def _():
            dh = acc_ref[...]
            xv = x_ref[...]
            r = lax.rsqrt(jnp.mean(xv * xv, axis=-1, keepdims=True) + RMS_EPS)
            xh = xv * r
            w = dh * g_ref[...]
            dx_ref[...] = dxo_ref[...] + r * (w - xh * jnp.mean(w * xh, axis=-1, keepdims=True))
            part = jnp.sum(dh * xh, axis=0, keepdims=True)

            @pl.when(i == 0)
            def _():
                dg_ref[...] = part

            @pl.when(i > 0)
            def _():
                dg_ref[...] += part

    return pl.pallas_call(
        body,
        name="inproj_bwd_dx",
        grid=(T // tm, nk),
        in_specs=[
            pl.BlockSpec((tm, tk), lambda i, j: (i, j)),
            pl.BlockSpec((None, D_MODEL, tk), lambda i, j: (j // per, 0, j % per)),
            pl.BlockSpec((tm, D_MODEL), lambda i, j: (i, 0)),
            pl.BlockSpec((1, D_MODEL), lambda i, j: (0, 0)),
            pl.BlockSpec((tm, D_MODEL), lambda i, j: (i, 0)),
        ],
        out_specs=[
            pl.BlockSpec((tm, D_MODEL), lambda i, j: (i, 0)),
            pl.BlockSpec((1, D_MODEL), lambda i, j: (0, 0)),
        ],
        out_shape=[jax.ShapeDtypeStruct((T, D_MODEL), F32), jax.ShapeDtypeStruct((1, D_MODEL), F32)],
        scratch_shapes=[pltpu.VMEM((tm, D_MODEL), F32)],
        compiler_params=_cparams(2),
    )(dz, w4, x, g, dxo)


def _loss_head(x, g, tgt):
    T = x.shape[0]
    tm = min(512, T)

    def body(x_ref, g_ref, t_ref, dx_ref, l_ref, dg_ref):
        i = pl.program_id(0)
        xv = x_ref[...]
        r = lax.rsqrt(jnp.mean(xv * xv, axis=-1, keepdims=True) + RMS_EPS)
        xh = xv * r
        err = xh * g_ref[...] - t_ref[...]
        lpart = 0.5 * jnp.sum(jnp.mean(err * err, axis=-1, keepdims=True), axis=0, keepdims=True)
        dout = err * (1.0 / D_MODEL)
        w = dout * g_ref[...]
        dx_ref[...] = r * (w - xh * jnp.mean(w * xh, axis=-1, keepdims=True))
        gpart = jnp.sum(dout * xh, axis=0, keepdims=True)

        @pl.when(i == 0)
        def _():
            l_ref[...] = jnp.broadcast_to(lpart, l_ref.shape)
            dg_ref[...] = gpart

        @pl.when(i > 0)
        def _():
            l_ref[...] += jnp.broadcast_to(lpart, l_ref.shape)
            dg_ref[...] += gpart

    return pl.pallas_call(
        body,
        name="loss_head",
        grid=(T // tm,),
        in_specs=[
            pl.BlockSpec((tm, D_MODEL), lambda i: (i, 0)),
            pl.BlockSpec((1, D_MODEL), lambda i: (0, 0)),
            pl.BlockSpec((tm, D_MODEL), lambda i: (i, 0)),
        ],
        out_specs=[
            pl.BlockSpec((tm, D_MODEL), lambda i: (i, 0)),
            pl.BlockSpec((1, 128), lambda i: (0, 0)),
            pl.BlockSpec((1, D_MODEL), lambda i: (0, 0)),
        ],
        out_shape=[
            jax.ShapeDtypeStruct((T, D_MODEL), F32),
            jax.ShapeDtypeStruct((1, 128), F32),
            jax.ShapeDtypeStruct((1, D_MODEL), F32),
        ],
        compiler_params=_cparams(1),
    )(x, g, tgt)


def _s5_prep(lam_re, lam_im, b_re, b_im, log_dt):
    lam = lax.complex(lam_re, lam_im)
    dt = jnp.exp(log_dt)[:, None]
    a = jnp.exp(lam * dt)
    bbar = ((a - 1.0) / lam)[..., None] * lax.complex(b_re, b_im)
    return jnp.real(a), jnp.imag(a), jnp.real(bbar), jnp.imag(bbar)


def _block_diag_in(m):
    m4 = m.reshape(SUPER, 8, S5_STATE, S5_CH)
    eye = jnp.eye(8, dtype=m.dtype)
    out = jnp.einsum("jgph,gk->jghkp", m4, eye)
    return out.reshape(SUPER, 8 * S5_CH, 8 * S5_STATE)


def _block_diag_in_grad(d):
    d6 = d.reshape(SUPER, 8, S5_CH, 8, S5_STATE)
    diag = jnp.einsum("jghgp->jgph", d6)
    return diag.reshape(S5_GROUPS, S5_STATE, S5_CH)


def _block_diag_out(m):
    m4 = m.reshape(SUPER, 8, S5_CH, S5_STATE)
    eye = jnp.eye(8, dtype=m.dtype)
    out = jnp.einsum("jghp,gk->jgpkh", m4, eye)
    return out.reshape(SUPER, 8 * S5_STATE, 8 * S5_CH)


def _block_diag_out_grad(d):
    d6 = d.reshape(SUPER, 8, S5_STATE, 8, S5_CH)
    diag = jnp.einsum("jgpgh->jghp", d6)
    return diag.reshape(S5_GROUPS, S5_CH, S5_STATE)


def _scan_coefs(a_re, a_im, reverse):
    a = lax.complex(a_re.reshape(-1), a_im.reshape(-1))
    if reverse:
        a = jnp.conj(a)
    pw = [a]
    for _ in range(7):
        pw.append(pw[-1] * a)
    rows = jnp.arange(8)

    def masked(k):
        m = (rows + k <= 7) if reverse else (rows >= k)
        return jnp.where(m[:, None], pw[k - 1][None, :], 0.0)

    a1, a2, a4 = masked(1), masked(2), masked(4)
    carry = jnp.stack([pw[7 - r] for r in range(8)]) if reverse else jnp.stack(pw)
    parts = []
    for c in (a1, a2, a4, carry):
        parts += [jnp.real(c), jnp.imag(c)]
    return jnp.stack(parts).astype(F32)


def _scan_block(r, im, coef_ref, cs, reverse):
    for k, idx in ((1, 0), (2, 2), (4, 4)):
        ar = coef_ref[idx, :, cs]
        ai = coef_ref[idx + 1, :, cs]
        sh = 8 - k if reverse else k
        rr = pltpu.roll(r, sh, 0)
        ri = pltpu.roll(im, sh, 0)
        r, im = r + ar * rr - ai * ri, im + ar * ri + ai * rr
    return r, im


def _s5_fwd(z, p):
    T = z.shape[0]
    tm = min(256, T)
    nblk = tm // 8
    W = STATE_W

    def body(xa_ref, ga_ref, bre_ref, bim_ref, cre_ref, cim_ref, dv_ref, wg_ref, bg_ref, coef_ref,
             ya_ref, yraw_ref, sre_ref, sim_ref, wre, wim):
        @pl.when(pl.program_id(0) == 0)
        def _():
            wre[0:8, :] = jnp.zeros((8, W), F32)
            wim[0:8, :] = jnp.zeros((8, W), F32)

        xa = xa_ref[...]
        xab = xa.astype(BF16)
        for j in range(SUPER):
            xj = xab[:, j * 128:(j + 1) * 128]
            wre[8:8 + tm, j * 512:(j + 1) * 512] = jnp.dot(xj, bre_ref[j], preferred_element_type=F32)
            wim[8:8 + tm, j * 512:(j + 1) * 512] = jnp.dot(xj, bim_ref[j], preferred_element_type=F32)

        def blk(b, carry):
            base = pl.multiple_of(8 + b * 8, 8)
            for cc in range(W // SCAN_COLS):
                cs = pl.ds(cc * SCAN_COLS, SCAN_COLS)
                r, im = _scan_block(wre[pl.ds(base, 8), cs], wim[pl.ds(base, 8), cs], coef_ref, cs, False)
                cr = wre[pl.ds(base - 1, 1), cs]
                ci = wim[pl.ds(base - 1, 1), cs]
                pr = coef_ref[6, :, cs]
                pi = coef_ref[7, :, cs]
                wre[pl.ds(base, 8), cs] = r + pr * cr - pi * ci
                wim[pl.ds(base, 8), cs] = im + pr * ci + pi * cr
            return carry

        lax.fori_loop(0, nblk, blk, 0)
        wre[0:8, :] = wre[tm:tm + 8, :]
        wim[0:8, :] = wim[tm:tm + 8, :]
        sre_ref[...] = wre[8:8 + tm, :]
        sim_ref[...] = wim[8:8 + tm, :]

        for j in range(SUPER):
            yr = jnp.dot(wre[8:8 + tm, j * 512:(j + 1) * 512].astype(BF16), cre_ref[j], preferred_element_type=F32)
            yr += jnp.dot(wim[8:8 + tm, j * 512:(j + 1) * 512].astype(BF16), cim_ref[j], preferred_element_type=F32)
            yraw_ref[:, j * 128:(j + 1) * 128] = yr
        yraw = yraw_ref[...] + dv_ref[...] * xa
        yraw_ref[...] = yraw
        yg = _gelu(yraw)
        q = jnp.dot(yg.astype(BF16), wg_ref[...], preferred_element_type=F32) + bg_ref[...]
        sga, _ = _silu_and_grad(ga_ref[...])
        ya_ref[...] = (yg * jax.nn.sigmoid(q) * sga).astype(BF16)

    return pl.pallas_call(
        body,
        name="s5_fwd",
        grid=(T // tm,),
        in_specs=[
            pl.BlockSpec((tm, S5_W), lambda i: (i, 0)),
            pl.BlockSpec((tm, S5_W), lambda i: (i, 6)),
            _full((SUPER, 128, 512)), _full((SUPER, 128, 512)),
            _full((SUPER, 512, 128)), _full((SUPER, 512, 128)),
            _full((1, S5_W)), _full((S5_W, S5_W)), _full((1, S5_W)),
            _full((8, 8, W)),
        ],
        out_specs=[
            pl.BlockSpec((tm, S5_W), lambda i: (i, 0)),
            pl.BlockSpec((tm, S5_W), lambda i: (i, 0)),
            pl.BlockSpec((tm, W), lambda i: (i, 0)),
            pl.BlockSpec((tm, W), lambda i: (i, 0)),
        ],
        out_shape=[
            jax.ShapeDtypeStruct((T, S5_W), BF16),
            jax.ShapeDtypeStruct((T, S5_W), F32),
            jax.ShapeDtypeStruct((T, W), F32),
            jax.ShapeDtypeStruct((T, W), F32),
        ],
        scratch_shapes=[pltpu.VMEM((tm + 8, W), F32), pltpu.VMEM((tm + 8, W), F32)],
        compiler_params=_cparams(1),
    )(z, z, p["b4re"], p["b4im"], p["c4re"], p["c4im"], p["dvec"], p["wglu"], p["bglu"], p["coef_f"])


def _s5_bwd(dy, z, yraw, sre, sim, p):
    T = z.shape[0]
    tm = min(256, T)
    nt = T // tm
    nblk = tm // 8
    W = STATE_W
    rev = lambda i: nt - 1 - i

    def body(dya_ref, xa_ref, ga_ref, yraw_ref, sre_ref, sim_ref, hre_ref, him_ref,
             bre_t_ref, bim_t_ref, cre_t_ref, cim_t_ref, dv_ref, wg_ref, wgt_ref, bg_ref, coef_ref,
             dxa_ref, dga_ref, dbre_ref, dbim_ref, dcre_ref, dcim_ref, dd_ref, dwg_ref, dbg_ref, da_ref,
             wre, wim, dyr_ref):
        i = pl.program_id(0)

        @pl.when(i == 0)
        def _():
            wre[tm:tm + 8, :] = jnp.zeros((8, W), F32)
            wim[tm:tm + 8, :] = jnp.zeros((8, W), F32)
            for ref in (dbre_ref, dbim_ref, dcre_ref, dcim_ref, dd_ref, dwg_ref, dbg_ref, da_ref):
                ref[...] = jnp.zeros_like(ref)

        xa = xa_ref[...]
        dya = dya_ref[...]
        yg, dgelu = _gelu_and_grad(yraw_ref[...])
        ygb = yg.astype(BF16)
        q = jnp.dot(ygb, wg_ref[...], preferred_element_type=F32) + bg_ref[...]
        sq = jax.nn.sigmoid(q)
        sga, dsga = _silu_and_grad(ga_ref[...])
        dga_ref[...] = (dya * (yg * sq) * dsga).astype(BF16)
        dya0 = dya * sga
        dq = dya0 * yg * sq * (1.0 - sq)
        dqb = dq.astype(BF16)
        dyg = dya0 * sq + jnp.dot(dqb, wgt_ref[...], preferred_element_type=F32)
        dwg_ref[...] += _dot_tn(ygb, dqb)
        dbg_ref[...] += jnp.sum(dq, axis=0, keepdims=True)
        dyraw = dyg * dgelu
        dd_ref[...] += jnp.sum(dyraw * xa, axis=0, keepdims=True)
        dyr_ref[...] = dyraw.astype(BF16)

        for j in range(SUPER):
            dj = dyr_ref[:, j * 128:(j + 1) * 128]
            wre[0:tm, j * 512:(j + 1) * 512] = jnp.dot(dj, cre_t_ref[j], preferred_element_type=F32)
            wim[0:tm, j * 512:(j + 1) * 512] = jnp.dot(dj, cim_t_ref[j], preferred_element_type=F32)

        row0 = lax.broadcasted_iota(jnp.int32, (8, SCAN_COLS), 0) == 0
        head_on = (i < nt - 1).astype(F32)

        def one_block(base, first):
            for cc in range(W // SCAN_COLS):
                cs = pl.ds(cc * SCAN_COLS, SCAN_COLS)
                r, im = _scan_block(wre[pl.ds(base, 8), cs], wim[pl.ds(base, 8), cs], coef_ref, cs, True)
                cr = wre[pl.ds(base + 8, 1), cs]
                ci = wim[pl.ds(base + 8, 1), cs]
                pr = coef_ref[6, :, cs]
                pi = coef_ref[7, :, cs]
                r, im = r + pr * cr - pi * ci, im + pr * ci + pi * cr
                wre[pl.ds(base, 8), cs] = r
                wim[pl.ds(base, 8), cs] = im
                if first:
                    pre = hre_ref[7:8, cs] * head_on
                    pim = him_ref[7:8, cs] * head_on
                else:
                    pre = sre_ref[pl.ds(base - 1, 1), cs]
                    pim = sim_ref[pl.ds(base - 1, 1), cs]
                spr = jnp.where(row0, pre, pltpu.roll(sre_ref[pl.ds(base, 8), cs], 1, 0))
                spi = jnp.where(row0, pim, pltpu.roll(sim_ref[pl.ds(base, 8), cs], 1, 0))
                da_ref[0, :, cs] += r * spr + im * spi
                da_ref[1, :, cs] += im * spr - r * spi

        def blk(b, carry):
            one_block(pl.multiple_of((nblk - 1 - b) * 8, 8), False)
            return carry

        lax.fori_loop(0, nblk - 1, blk, 0)
        one_block(0, True)
        wre[tm:tm + 8, :] = wre[0:8, :]
        wim[tm:tm + 8, :] = wim[0:8, :]

        xab = xa.astype(BF16)
        for j in range(SUPER):
            cols = slice(j * 512, (j + 1) * 512)
            gre = wre[0:tm, cols].astype(BF16)
            gim = wim[0:tm, cols].astype(BF16)
            xj = xab[:, j * 128:(j + 1) * 128]
            dj = dyr_ref[:, j * 128:(j + 1) * 128]
            dbre_ref[j] += _dot_tn(xj, gre)
            dbim_ref[j] += _dot_tn(xj, gim)
            dcre_ref[j] += _dot_tn(sre_ref[:, cols], dj)
            dcim_ref[j] += _dot_tn(sim_ref[:, cols], dj)
            dxj = jnp.dot(gre, bre_t_ref[j], preferred_element_type=F32)
            dxj += jnp.dot(gim, bim_t_ref[j], preferred_element_type=F32)
            dxj += dyraw[:, j * 128:(j + 1) * 128] * dv_ref[:, j * 128:(j + 1) * 128]
            dxa_ref[:, j * 128:(j + 1) * 128] = dxj.astype(BF16)

    acc = lambda shape: _full(shape)
    hb = tm // 8
    return pl.pallas_call(
        body,
        name="s5_bwd",
        grid=(nt,),
        in_specs=[
            pl.BlockSpec((tm, S5_W), lambda i: (rev(i), 0)),
            pl.BlockSpec((tm, S5_W), lambda i: (rev(i), 0)),
            pl.BlockSpec((tm, S5_W), lambda i: (rev(i), 6)),
            pl.BlockSpec((tm, S5_W), lambda i: (rev(i), 0)),
            pl.BlockSpec((tm, W), lambda i: (rev(i), 0)),
            pl.BlockSpec((tm, W), lambda i: (rev(i), 0)),
            pl.BlockSpec((8, W), lambda i: (jnp.maximum(rev(i) * hb - 1, 0), 0)),
            pl.BlockSpec((8, W), lambda i: (jnp.maximum(rev(i) * hb - 1, 0), 0)),
            _full((SUPER, 512, 128)), _full((SUPER, 512, 128)),
            _full((SUPER, 128, 512)), _full((SUPER, 128, 512)),
            _full((1, S5_W)), _full((S5_W, S5_W)), _full((S5_W, S5_W)), _full((1, S5_W)),
            _full((8, 8, W)),
        ],
        out_specs=[
            pl.BlockSpec((tm, S5_W), lambda i: (rev(i), 0)),
            pl.BlockSpec((tm, S5_W), lambda i: (rev(i), 0)),
            acc((SUPER, 128, 512)), acc((SUPER, 128, 512)),
            acc((SUPER, 512, 128)), acc((SUPER, 512, 128)),
            acc((1, S5_W)), acc((S5_W, S5_W)), acc((1, S5_W)), acc((2, 8, W)),
        ],
        out_shape=[
            jax.ShapeDtypeStruct((T, S5_W), BF16),
            jax.ShapeDtypeStruct((T, S5_W), BF16),
            jax.ShapeDtypeStruct((SUPER, 128, 512), F32), jax.ShapeDtypeStruct((SUPER, 128, 512), F32),
            jax.ShapeDtypeStruct((SUPER, 512, 128), F32), jax.ShapeDtypeStruct((SUPER, 512, 128), F32),
            jax.ShapeDtypeStruct((1, S5_W), F32), jax.ShapeDtypeStruct((S5_W, S5_W), F32),
            jax.ShapeDtypeStruct((1, S5_W), F32), jax.ShapeDtypeStruct((2, 8, W), F32),
        ],
        scratch_shapes=[pltpu.VMEM((tm + 8, W), F32), pltpu.VMEM((tm + 8, W), F32), pltpu.VMEM((tm, S5_W), BF16)],
        compiler_params=_cparams(1),
    )(dy, z, z, yraw, sre, sim, sre, sim,
      p["b4re_t"], p["b4im_t"], p["c4re_t"], p["c4im_t"], p["dvec"], p["wglu"], p["wglu_t"], p["bglu"], p["coef_r"])


def _ln_fwd(vf, lng, lnb):
    mu = jnp.mean(vf, axis=-1, keepdims=True)
    d = vf - mu
    rstd = lax.rsqrt(jnp.mean(d * d, axis=-1, keepdims=True) + LN_EPS)
    xh = d * rstd
    return xh, rstd, xh * lng + lnb


def _col_block(tm, b):
    return pl.BlockSpec((tm, 512), lambda i: (i, b))


def _ln_halves(vf0, vf1):
    mu = (jnp.sum(vf0, axis=-1, keepdims=True) + jnp.sum(vf1, axis=-1, keepdims=True)) * (1.0 / SGU_W)
    d0, d1 = vf0 - mu, vf1 - mu
    var = (jnp.sum(d0 * d0, axis=-1, keepdims=True) + jnp.sum(d1 * d1, axis=-1, keepdims=True)) * (1.0 / SGU_W)
    rstd = lax.rsqrt(var + LN_EPS)
    return d0 * rstd, d1 * rstd, rstd


def _sgu_fwd(z, ws, bsf, lng, lnb):
    T = z.shape[0]
    tm = min(512, T)

    def body(u0, u1, v0, v1, g0, g1, ws_ref, bs_ref, lng_ref, lnb_ref, yb_ref, vn_ref):
        for c in range(tm // CHUNK):
            rows = slice(c * CHUNK, (c + 1) * CHUNK)
            xh0, xh1, _ = _ln_halves(_gelu(v0[rows, :]), _gelu(v1[rows, :]))
            vn_ref[:, 0:512] = (xh0 * lng_ref[:, 0:512] + lnb_ref[:, 0:512]).astype(BF16)
            vn_ref[:, 512:1024] = (xh1 * lng_ref[:, 512:1024] + lnb_ref[:, 512:1024]).astype(BF16)
            for half, (u_ref, g_ref) in enumerate(((u0, g0), (u1, g1))):
                sg, _ = _silu_and_grad(g_ref[rows, :])
                m = _gelu(u_ref[rows, :]) * sg
                for hh in range(SGU_HEADS // 2):
                    h = half * (SGU_HEADS // 2) + hh
                    cols = slice(h * 128, (h + 1) * 128)
                    s = jnp.dot(ws_ref[h], vn_ref[:, cols], preferred_element_type=F32) + bs_ref[:, cols]
                    yb_ref[rows, cols] = (m[:, hh * 128:(hh + 1) * 128] * s).astype(BF16)

    return pl.pallas_call(
        body,
        name="sgu_fwd",
        grid=(T // tm,),
        in_specs=[_col_block(tm, b) for b in (1, 2, 3, 4, 7, 8)] + [
            _full((SGU_HEADS, CHUNK, CHUNK)), _full((CHUNK, SGU_W)), _full((1, SGU_W)), _full((1, SGU_W)),
        ],
        out_specs=pl.BlockSpec((tm, SGU_W), lambda i: (i, 0)),
        out_shape=jax.ShapeDtypeStruct((T, SGU_W), BF16),
        scratch_shapes=[pltpu.VMEM((CHUNK, SGU_W), BF16)],
        compiler_params=_cparams(1),
    )(z, z, z, z, z, z, ws, bsf, lng, lnb)


def _sgu_bwd(dy, z, ws, ws_t, bsf, lng, lnb):
    T = z.shape[0]
    tm = min(512, T)
    HH = SGU_HEADS // 2

    def body(u0, u1, v0, v1, g0, g1, dy0, dy1, ws_ref, wst_ref, bs_ref, lng_ref, lnb_ref,
             du_ref, dv_ref, dgb_ref, dws_ref, dbs_ref, dlng_ref, dlnb_ref, vn_ref, dvn_ref):
        @pl.when(pl.program_id(0) == 0)
        def _():
            for ref in (dws_ref, dbs_ref, dlng_ref, dlnb_ref):
                ref[...] = jnp.zeros_like(ref)

        for c in range(tm // CHUNK):
            rows = slice(c * CHUNK, (c + 1) * CHUNK)
            vf0, dgv0 = _gelu_and_grad(v0[rows, :])
            vf1, dgv1 = _gelu_and_grad(v1[rows, :])
            xh0, xh1, rstd = _ln_halves(vf0, vf1)
            vn_ref[:, 0:512] = (xh0 * lng_ref[:, 0:512] + lnb_ref[:, 0:512]).astype(BF16)
            vn_ref[:, 512:1024] = (xh1 * lng_ref[:, 512:1024] + lnb_ref[:, 512:1024]).astype(BF16)
            for half, (u_ref, g_ref, dy_ref) in enumerate(((u0, g0, dy0), (u1, g1, dy1))):
                ug, dgu = _gelu_and_grad(u_ref[rows, :])
                sg, dsg = _silu_and_grad(g_ref[rows, :])
                dyb = dy_ref[rows, :]
                dyb0 = dyb * sg
                ds_half = dyb0 * ug
                du_scale = dyb0 * dgu
                dg_scale = dyb * ug * dsg
                for hh in range(HH):
                    h = half * HH + hh
                    cols = slice(h * 128, (h + 1) * 128)
                    lc = slice(hh * 128, (hh + 1) * 128)
                    s = jnp.dot(ws_ref[h], vn_ref[:, cols], preferred_element_type=F32) + bs_ref[:, cols]
                    du_ref[rows, cols] = (du_scale[:, lc] * s).astype(BF16)
                    dgb_ref[rows, cols] = (dg_scale[:, lc] * s).astype(BF16)
                    ds = ds_half[:, lc]
                    dbs_ref[:, cols] += ds
                    dsb = ds.astype(BF16)
                    dws_ref[h] += _dot_nt(dsb, vn_ref[:, cols])
                    dvn_ref[:, cols] = jnp.dot(wst_ref[h], dsb, preferred_element_type=F32)
            dvn0 = dvn_ref[:, 0:512]
            dvn1 = dvn_ref[:, 512:1024]
            dlnb_ref[:, 0:512] += jnp.sum(dvn0, axis=0, keepdims=True)
            dlnb_ref[:, 512:1024] += jnp.sum(dvn1, axis=0, keepdims=True)
            dlng_ref[:, 0:512] += jnp.sum(dvn0 * xh0, axis=0, keepdims=True)
            dlng_ref[:, 512:1024] += jnp.sum(dvn1 * xh1, axis=0, keepdims=True)
            dxh0 = dvn0 * lng_ref[:, 0:512]
            dxh1 = dvn1 * lng_ref[:, 512:1024]
            m1 = (jnp.sum(dxh0, axis=-1, keepdims=True) + jnp.sum(dxh1, axis=-1, keepdims=True)) * (1.0 / SGU_W)
            m2 = (jnp.sum(dxh0 * xh0, axis=-1, keepdims=True) + jnp.sum(dxh1 * xh1, axis=-1, keepdims=True)) * (1.0 / SGU_W)
            dv_ref[rows, 0:512] = (rstd * (dxh0 - m1 - xh0 * m2) * dgv0).astype(BF16)
            dv_ref[rows, 512:1024] = (rstd * (dxh1 - m1 - xh1 * m2) * dgv1).astype(BF16)

    row_out = pl.BlockSpec((tm, SGU_W), lambda i: (i, 0))
    return pl.pallas_call(
        body,
        name="sgu_bwd",
        grid=(T // tm,),
        in_specs=[_col_block(tm, b) for b in (1, 2, 3, 4, 7, 8)] + [_col_block(tm, 1), _col_block(tm, 2)] + [
            _full((SGU_HEADS, CHUNK, CHUNK)), _full((SGU_HEADS, CHUNK, CHUNK)),
            _full((CHUNK, SGU_W)), _full((1, SGU_W)), _full((1, SGU_W)),
        ],
        out_specs=[row_out, row_out, row_out,
                   _full((SGU_HEADS, CHUNK, CHUNK)), _full((CHUNK, SGU_W)), _full((1, SGU_W)), _full((1, SGU_W))],
        out_shape=[
            jax.ShapeDtypeStruct((T, SGU_W), BF16), jax.ShapeDtypeStruct((T, SGU_W), BF16),
            jax.ShapeDtypeStruct((T, SGU_W), BF16),
            jax.ShapeDtypeStruct((SGU_HEADS, CHUNK, CHUNK), F32), jax.ShapeDtypeStruct((CHUNK, SGU_W), F32),
            jax.ShapeDtypeStruct((1, SGU_W), F32), jax.ShapeDtypeStruct((1, SGU_W), F32),
        ],
        scratch_shapes=[pltpu.VMEM((CHUNK, SGU_W), BF16), pltpu.VMEM((CHUNK, SGU_W), F32)],
        compiler_params=_cparams(1),
    )(z, z, z, z, z, z, dy, dy, ws, ws_t, bsf, lng, lnb)


def _pool_den(first_row, n):
    return (lax.broadcasted_iota(jnp.int32, (n, 1), 0) + first_row + 1).astype(F32)


def _pool_p(ext, xc, pos, tm):
    w2 = ext + pltpu.roll(ext, 1, 0)
    w4 = w2 + pltpu.roll(w2, 2, 0)
    w8 = w4 + pltpu.roll(w4, 4, 0)
    w16 = w8 + pltpu.roll(w8, 8, 0)
    out = []
    for g, (w, ws) in enumerate(zip(POOL_WINDOWS, (w2, w4, w8, w16))):
        cols = slice(g * 128, (g + 1) * 128)
        mean = ws[POOL_HALO:POOL_HALO + tm, cols] / jnp.minimum(pos, float(w))
        out.append(mean - xc[:, cols])
    return out


def _pool_fwd(z, wp, scale):
    T = z.shape[0]
    tm = min(512, T)
    hb = tm // POOL_HALO

    def body(xc_ref, hx_ref, gc_ref, wp_ref, sc_ref, yc_ref):
        i = pl.program_id(0)
        xc = xc_ref[...]
        halo = hx_ref[...] * (i > 0).astype(F32)
        ext = jnp.concatenate([halo, xc], axis=0)
        ps = _pool_p(ext, xc, _pool_den(i * tm, tm), tm)
        sg, _ = _silu_and_grad(gc_ref[...])
        for g in range(4):
            cols = slice(g * 128, (g + 1) * 128)
            pw = _dot(ps[g], wp_ref[g])
            yc_ref[:, cols] = (pw * sc_ref[:, cols] * sg[:, cols]).astype(BF16)

    return pl.pallas_call(
        body,
        name="pool_fwd",
        grid=(T // tm,),
        in_specs=[
            _col_block(tm, 5),
            pl.BlockSpec((POOL_HALO, 512), lambda i: (jnp.maximum(i * hb - 1, 0), 5)),
            _col_block(tm, 9),
            _full((4, 128, 128)), _full((1, POOL_W)),
        ],
        out_specs=pl.BlockSpec((tm, POOL_W), lambda i: (i, 0)),
        out_shape=jax.ShapeDtypeStruct((T, POOL_W), BF16),
        compiler_params=_cparams(1),
    )(z, z, z, wp, scale)


def _pool_bwd(dy, z, wp, wp_t, scale):
    T = z.shape[0]
    tm = min(512, T)
    nt = T // tm
    hb = tm // POOL_HALO
    last_hb = T // POOL_HALO - 1
    L = tm + POOL_HALO

    def body(xc_ref, hx_ref, gc_ref, gn_ref, dyc_ref, dyn_ref, wp_ref, wpt_ref, sc_ref,
             dxc_ref, dgc_ref, dwp_ref, dsc_ref):
        i = pl.program_id(0)

        @pl.when(i == 0)
        def _():
            dwp_ref[...] = jnp.zeros_like(dwp_ref)
            dsc_ref[...] = jnp.zeros_like(dsc_ref)

        xc = xc_ref[...]
        halo = hx_ref[...] * (i > 0).astype(F32)
        pos = _pool_den(i * tm, tm)
        ps = _pool_p(jnp.concatenate([halo, xc], axis=0), xc, pos, tm)
        sg, dsg = _silu_and_grad(gc_ref[...])
        dyc = dyc_ref[...]
        dyc0 = dyc * sg
        dpw = dyc0 * sc_ref[...]
        sgn, _ = _silu_and_grad(gn_ref[...])
        dpwn = dyn_ref[...] * sgn * sc_ref[...] * (i < nt - 1).astype(F32)
        posn = _pool_den((i + 1) * tm, POOL_HALO)
        dps, qs = [], []
        for g, w in enumerate(POOL_WINDOWS):
            cols = slice(g * 128, (g + 1) * 128)
            pw = _dot(ps[g], wp_ref[g])
            dgc_ref[:, cols] = (dyc[:, cols] * pw * sc_ref[:, cols] * dsg[:, cols]).astype(BF16)
            dsc_ref[:, cols] += jnp.sum(dyc0[:, cols] * pw, axis=0, keepdims=True)
            dwp_ref[g] += _dot_tn(ps[g], dpw[:, cols])
            dp = _dot(dpw[:, cols], wpt_ref[g])
            dpn = _dot(dpwn[:, cols], wpt_ref[g])
            dps.append(dp)
            qs.append(jnp.concatenate([dp / jnp.minimum(pos, float(w)), dpn / jnp.minimum(posn, float(w))], axis=0))
        ext = jnp.concatenate(qs, axis=1)
        f2 = ext + pltpu.roll(ext, L - 1, 0)
        f4 = f2 + pltpu.roll(f2, L - 2, 0)
        f8 = f4 + pltpu.roll(f4, L - 4, 0)
        f16 = f8 + pltpu.roll(f8, L - 8, 0)
        for g, f in enumerate((f2, f4, f8, f16)):
            cols = slice(g * 128, (g + 1) * 128)
            dxc_ref[:, cols] = (f[0:tm, cols] - dps[g]).astype(BF16)

    nxt = lambda i: jnp.minimum((i + 1) * hb, last_hb)
    return pl.pallas_call(
        body,
        name="pool_bwd",
        grid=(nt,),
        in_specs=[
            _col_block(tm, 5),
            pl.BlockSpec((POOL_HALO, 512), lambda i: (jnp.maximum(i * hb - 1, 0), 5)),
            _col_block(tm, 9),
            pl.BlockSpec((POOL_HALO, 512), lambda i: (nxt(i), 9)),
            _col_block(tm, 3),
            pl.BlockSpec((POOL_HALO, 512), lambda i: (nxt(i), 3)),
            _full((4, 128, 128)), _full((4, 128, 128)), _full((1, POOL_W)),
        ],
        out_specs=[
            pl.BlockSpec((tm, POOL_W), lambda i: (i, 0)), pl.BlockSpec((tm, POOL_W), lambda i: (i, 0)),
            _full((4, 128, 128)), _full((1, POOL_W)),
        ],
        out_shape=[
            jax.ShapeDtypeStruct((T, POOL_W), BF16), jax.ShapeDtypeStruct((T, POOL_W), BF16),
            jax.ShapeDtypeStruct((4, 128, 128), F32), jax.ShapeDtypeStruct((1, POOL_W), F32),
        ],
        compiler_params=_cparams(1),
    )(z, z, z, z, dy, dy, wp, wp_t, scale)


def _row_tile(rows, cols):
    tr = 8
    while tr * 2 * cols * 4 <= 2 * 1024 * 1024 and rows % (tr * 2) == 0:
        tr *= 2
    return tr


def _add_own_layer(own, recv, cidx):
    _, R, C = own.shape
    tr = _row_tile(R, C)

    def body(c_ref, a_ref, b_ref, o_ref):
        o_ref[...] = a_ref[...] + b_ref[...]

    return pl.pallas_call(
        body,
        name="add_own_layer",
        grid_spec=pltpu.PrefetchScalarGridSpec(
            num_scalar_prefetch=1,
            grid=(R // tr,),
            in_specs=[
                pl.BlockSpec((None, tr, C), lambda i, c: (c[0], i, 0)),
                pl.BlockSpec((tr, C), lambda i, c: (i, 0)),
            ],
            out_specs=pl.BlockSpec((tr, C), lambda i, c: (i, 0)),
        ),
        out_shape=jax.ShapeDtypeStruct((R, C), F32),
        compiler_params=_cparams(1),
    )(cidx, own, recv)


def _add2(a, b):
    R, C = a.shape
    tr = _row_tile(R, C)

    def body(a_ref, b_ref, o_ref):
        o_ref[...] = a_ref[...] + b_ref[...]

    spec = pl.BlockSpec((tr, C), lambda i: (i, 0))
    return pl.pallas_call(
        body, name="add2", grid=(R // tr,), in_specs=[spec, spec], out_specs=spec,
        out_shape=jax.ShapeDtypeStruct((R, C), F32), compiler_params=_cparams(1),
    )(a, b)


def _sum_chips(parts):
    _, R, C = parts.shape
    tr = _row_tile(R, N_CHIPS * C)

    def body(p_ref, o_ref):
        o_ref[...] = ((p_ref[0] + p_ref[1]) + p_ref[2]) + p_ref[3]

    return pl.pallas_call(
        body, name="sum_chips", grid=(R // tr,),
        in_specs=[pl.BlockSpec((N_CHIPS, tr, C), lambda i: (0, i, 0))],
        out_specs=pl.BlockSpec((tr, C), lambda i: (i, 0)),
        out_shape=jax.ShapeDtypeStruct((R, C), F32), compiler_params=_cparams(1),
    )(parts)


def _adamw_math(w, g, m, v):
    m = ADAM_B1 * m + (1.0 - ADAM_B1) * g
    v = ADAM_B2 * v + (1.0 - ADAM_B2) * (g * g)
    m_hat = m / (1.0 - ADAM_B1 ** ADAM_STEP)
    v_hat = v / (1.0 - ADAM_B2 ** ADAM_STEP)
    delta = -ADAM_LR * (m_hat / (jnp.sqrt(v_hat) + ADAM_EPS) + ADAM_WD * w)
    return delta, m, v


def _adamw(w, g, m, v):
    R, C = w.shape
    tr = _row_tile(R, C)

    def body(w_ref, g_ref, m_ref, v_ref, d_ref, mo_ref, vo_ref):
        d_ref[...], mo_ref[...], vo_ref[...] = _adamw_math(w_ref[...], g_ref[...], m_ref[...], v_ref[...])

    spec = pl.BlockSpec((tr, C), lambda i: (i, 0))
    shp = jax.ShapeDtypeStruct((R, C), F32)
    return pl.pallas_call(
        body, name="adamw", grid=(R // tr,), in_specs=[spec] * 4, out_specs=[spec] * 3,
        out_shape=[shp] * 3, compiler_params=_cparams(1),
    )(w, g, m, v)


def _adamw_layers(w, mine, theirs, m, v, cidx):
    _, R, C = w.shape
    tr = _row_tile(R, C)

    def body(c_ref, w_ref, a_ref, b_ref, m_ref, v_ref, g_ref, d_ref, mo_ref, vo_ref):
        g = jnp.where(pl.program_id(0) == c_ref[0], a_ref[...], b_ref[...])
        g_ref[...] = g
        d_ref[...], mo_ref[...], vo_ref[...] = _adamw_math(w_ref[...], g, m_ref[...], v_ref[...])

    lay = pl.BlockSpec((None, tr, C), lambda l, i, c: (l, i, 0))
    flat = pl.BlockSpec((tr, C), lambda l, i, c: (i, 0))
    shp = jax.ShapeDtypeStruct((2, R, C), F32)
    return pl.pallas_call(
        body,
        name="adamw_layers",
        grid_spec=pltpu.PrefetchScalarGridSpec(
            num_scalar_prefetch=1, grid=(2, R // tr),
            in_specs=[lay, flat, flat, lay, lay], out_specs=[lay] * 4,
        ),
        out_shape=[shp] * 4,
        compiler_params=_cparams(2),
    )(cidx, w, mine, theirs, m, v)


_ANY = pl.BlockSpec(memory_space=pl.ANY)


def _mesh_pos():
    return lax.axis_index("x"), lax.axis_index("y"), lax.axis_index("c")


def _other_chips(x, y):
    return [(2 * x + (1 - y), x, 1 - y), (2 * (1 - x) + y, 1 - x, y), (2 * (1 - x) + (1 - y), 1 - x, 1 - y)]


def _gather_weights(shards):
    n = len(shards)

    def body(*refs):
        ins, outs = refs[:n], refs[n:2 * n]
        lsem, ssem, rsem = refs[2 * n:]
        x, y, c = _mesh_pos()
        me = 2 * x + y
        sib = (x, y, 1 - c)
        chips = _other_chips(x, y)

        def ici(k, d):
            return pltpu.make_async_remote_copy(
                ins[k].at[c], outs[k].at[c, me], ssem.at[6 * k + d], rsem.at[6 * k + d],
                device_id=(chips[d][1], chips[d][2], c), device_id_type=MESH_ID)

        def landed(k, d):
            return pltpu.make_async_remote_copy(
                ins[k].at[c], outs[k].at[c, chips[d][0]], ssem.at[6 * k + d], rsem.at[6 * k + d],
                device_id=sib, device_id_type=MESH_ID)

        def fwd(k, d, layer):
            return pltpu.make_async_remote_copy(
                outs[k].at[layer, chips[d][0]], outs[k].at[layer, chips[d][0]], ssem.at[6 * k + 3 + d],
                rsem.at[6 * k + 3 + d], device_id=sib, device_id_type=MESH_ID)

        local = [pltpu.make_async_copy(ins[k].at[l], outs[k].at[l, me], lsem.at[2 * k + l])
                 for k in range(n) for l in range(2)]
        for cp in local:
            cp.start()
        for k in range(n):
            for d in range(3):
                ici(k, d).start()
        for d in range(3):
            for k in range(n):
                landed(k, d).wait_recv()
                fwd(k, d, c).start()
        for d in range(3):
            for k in range(n):
                fwd(k, d, 1 - c).wait_recv()
        for k in range(n):
            for d in range(3):
                ici(k, d).wait_send()
                fwd(k, d, c).wait_send()
        for cp in local:
            cp.wait()

    return pl.pallas_call(
        body,
        name="gather_weights",
        in_specs=[_ANY] * n,
        out_specs=[_ANY] * n,
        out_shape=[jax.ShapeDtypeStruct((2, N_CHIPS) + s.shape[1:], s.dtype) for s in shards],
        scratch_shapes=[pltpu.SemaphoreType.DMA((2 * n,)), pltpu.SemaphoreType.DMA((6 * n,)),
                        pltpu.SemaphoreType.DMA((6 * n,))],
    )(*shards)


def _pair_exchange(name, arrs, other_layer):
    n = len(arrs)

    def body(*refs):
        ins, outs = refs[:n], refs[n:2 * n]
        ssem, rsem = refs[2 * n:]
        x, y, c = _mesh_pos()
        copies = [
            pltpu.make_async_remote_copy(
                ins[k].at[1 - c] if other_layer else ins[k], outs[k], ssem.at[k], rsem.at[k],
                device_id=(x, y, 1 - c), device_id_type=MESH_ID)
            for k in range(n)
        ]
        for cp in copies:
            cp.start()
        for cp in copies:
            cp.wait()

    return pl.pallas_call(
        body,
        name=name,
        in_specs=[_ANY] * n,
        out_specs=[_ANY] * n,
        out_shape=[jax.ShapeDtypeStruct(a.shape[1:] if other_layer else a.shape, a.dtype) for a in arrs],
        scratch_shapes=[pltpu.SemaphoreType.DMA((n,)), pltpu.SemaphoreType.DMA((n,))],
    )(*arrs)


def _chip_exchange(name, arrs, broadcast):
    n = len(arrs)

    def body(*refs):
        ins, outs = refs[:n], refs[n:2 * n]
        lsem, ssem, rsem = refs[2 * n:]
        x, y, c = _mesh_pos()
        me = 2 * x + y
        copies = []
        for k in range(n):
            copies.append(pltpu.make_async_copy(ins[k] if broadcast else ins[k].at[me], outs[k].at[me], lsem.at[k]))
        for k in range(n):
            for d, (j, tx, ty) in enumerate(_other_chips(x, y)):
                copies.append(pltpu.make_async_remote_copy(
                    ins[k] if broadcast else ins[k].at[j], outs[k].at[me], ssem.at[3 * k + d], rsem.at[3 * k + d],
                    device_id=(tx, ty, c), device_id_type=MESH_ID))
        for cp in copies:
            cp.start()
        for cp in copies:
            cp.wait()

    return pl.pallas_call(
        body,
        name=name,
        in_specs=[_ANY] * n,
        out_specs=[_ANY] * n,
        out_shape=[jax.ShapeDtypeStruct(((N_CHIPS,) + a.shape) if broadcast else a.shape, a.dtype) for a in arrs],
        scratch_shapes=[pltpu.SemaphoreType.DMA((n,)), pltpu.SemaphoreType.DMA((3 * n,)),
                        pltpu.SemaphoreType.DMA((3 * n,))],
    )(*arrs)


SMALL = ("norm_g", "lam_re", "lam_im", "b_re", "b_im", "c_re", "c_im", "d_skip", "log_dt", "b_glu", "ln_g", "ln_b",
         "w_s", "b_s", "w_pool", "pool_scale", "final_g")
BIG = ("w_in", "w_glu", "w_out")
WEIGHTS = ("norm_g", "w_in", "lam_re", "lam_im", "b_re", "b_im", "c_re", "c_im", "d_skip", "log_dt", "w_glu", "b_glu",
           "ln_g", "ln_b", "w_s", "b_s", "w_pool", "pool_scale", "w_out", "final_g")
PACK_UNIT = 8 * 128


def _pack(arrs):
    parts = []
    for a in arrs:
        f = a.reshape(-1).astype(F32)
        pad = (-f.shape[0]) % PACK_UNIT
        parts.append(jnp.pad(f, (0, pad)) if pad else f)
    return jnp.concatenate(parts).reshape(-1, 128)


def _unpack(buf, like):
    flat = buf.reshape(-1)
    out, off = [], 0
    for a in like:
        n = math.prod(a.shape)
        out.append(flat[off:off + n].reshape(a.shape))
        off += n + ((-n) % PACK_UNIT)
    return out


def _layer_params(l, wt, g_glu):
    a_re, a_im, bb_re, bb_im = _s5_prep(wt["lam_re"][l], wt["lam_im"][l], wt["b_re"][l], wt["b_im"][l], wt["log_dt"][l])
    b4re, b4im = _block_diag_in(bb_re), _block_diag_in(bb_im)
    c4re, c4im = _block_diag_out(wt["c_re"][l]), _block_diag_out(-wt["c_im"][l])
    tr = lambda m: jnp.swapaxes(m, 1, 2).astype(BF16)
    causal = jnp.tril(jnp.ones((CHUNK, CHUNK), dtype=bool))
    ws = jnp.where(causal[None], wt["w_s"][l], 0.0)
    wglu = g_glu[l].reshape(S5_W, S5_W)
    return dict(
        b4re=b4re.astype(BF16), b4im=b4im.astype(BF16), c4re=c4re.astype(BF16), c4im=c4im.astype(BF16),
        b4re_t=tr(b4re), b4im_t=tr(b4im), c4re_t=tr(c4re), c4im_t=tr(c4im),
        dvec=wt["d_skip"][l].reshape(1, S5_W), wglu=wglu, wglu_t=wglu.T, bglu=wt["b_glu"][l].reshape(1, S5_W),
        coef_f=_scan_coefs(a_re, a_im, False), coef_r=_scan_coefs(a_re, a_im, True),
        ws=ws.astype(BF16), ws_t=tr(ws),
        bsf=jnp.broadcast_to(wt["b_s"][l][:, None, :], (SGU_HEADS, CHUNK, CHUNK)).transpose(2, 0, 1).reshape(CHUNK, SGU_W),
        lng=wt["ln_g"][l].reshape(1, SGU_W), lnb=wt["ln_b"][l].reshape(1, SGU_W),
        wp=wt["w_pool"][l].astype(BF16), wp_t=tr(wt["w_pool"][l]), scale=wt["pool_scale"][l].reshape(1, POOL_W),
        norm_g=wt["norm_g"][l].reshape(1, D_MODEL),
    )


def _local_step(x0, tgt, wt, g_in, g_glu, g_out):
    params = [_layer_params(l, wt, g_glu) for l in range(DEPTH)]
    xs, saved = [x0], []
    for l in range(DEPTH):
        p = params[l]
        z, h = _inproj(xs[-1], p["norm_g"], g_in[l])
        ya, yraw, sre, sim = _s5_fwd(z, p)
        yb = _sgu_fwd(z, p["ws"], p["bsf"], p["lng"], p["lnb"])
        yc = _pool_fwd(z, p["wp"], p["scale"])
        xn, y = _outproj(ya, yb, yc, g_out[l].reshape(D_MODEL, D_MODEL), xs[-1])
        xs.append(xn)
        saved.append((z, h, yraw, sre, sim, y))

    dx, loss, dfg = _loss_head(xs[-1], wt["final_g"].reshape(1, D_MODEL), tgt)

    gr = {k: [None] * DEPTH for k in WEIGHTS if k != "final_g"}
    for l in reversed(range(DEPTH)):
        p = params[l]
        z, h, yraw, sre, sim, y = saved[l]
        w_out = g_out[l].reshape(D_MODEL, D_MODEL)
        dy = _outproj_bwd_dy(dx, w_out)
        gr["w_out"][l] = _outproj_bwd_dw(y, dx)
        dxa, dga, dbre, dbim, dcre, dcim, dd, dwg, dbg, da = _s5_bwd(dy, z, yraw, sre, sim, p)
        du, dv, dgb, dws, dbsf, dlng, dlnb = _sgu_bwd(dy, z, p["ws"], p["ws_t"], p["bsf"], p["lng"], p["lnb"])
        dxc, dgc, dwp, dsc = _pool_bwd(dy, z, p["wp"], p["wp_t"], p["scale"])
        dz = jnp.concatenate([dxa, du, dv, dxc, dga, dgb, dgc], axis=1)
        gr["w_in"][l] = _inproj_bwd_dw(h, dz)
        dx, dng = _inproj_bwd_dx(dz, g_in[l], xs[l], p["norm_g"], dx)

        raw = (wt["lam_re"][l], wt["lam_im"][l], wt["b_re"][l], wt["b_im"][l], wt["log_dt"][l])
        _, vjp = jax.vjp(_s5_prep, *raw)
        da = jnp.sum(da, axis=1)
        cot = (da[0].reshape(S5_GROUPS, S5_STATE), da[1].reshape(S5_GROUPS, S5_STATE),
               _block_diag_in_grad(dbre), _block_diag_in_grad(dbim))
        gr["lam_re"][l], gr["lam_im"][l], gr["b_re"][l], gr["b_im"][l], gr["log_dt"][l] = vjp(cot)
        gr["c_re"][l] = _block_diag_out_grad(dcre)
        gr["c_im"][l] = -_block_diag_out_grad(dcim)
        gr["d_skip"][l] = dd.reshape(S5_GROUPS, S5_CH)
        gr["w_glu"][l] = dwg
        gr["b_glu"][l] = dbg.reshape(S5_W)
        causal = jnp.tril(jnp.ones((CHUNK, CHUNK), dtype=bool))
        gr["w_s"][l] = jnp.where(causal[None], dws, 0.0)
        gr["b_s"][l] = dbsf.reshape(CHUNK, SGU_HEADS, CHUNK).sum(-1).T
        gr["ln_g"][l] = dlng.reshape(SGU_W)
        gr["ln_b"][l] = dlnb.reshape(SGU_W)
        gr["w_pool"][l] = dwp
        gr["pool_scale"][l] = dsc.reshape(POOL_W)
        gr["norm_g"][l] = dng.reshape(D_MODEL)

    grads = {k: jnp.stack(v) for k, v in gr.items()}
    grads["final_g"] = dfg.reshape(D_MODEL)
    return loss, dx, grads


def kernel(x, norm_g, w_in, lam_re, lam_im, b_re, b_im, c_re, c_im, d_skip, log_dt, w_glu, b_glu, ln_g, ln_b, w_s, b_s, w_pool, pool_scale, w_out, final_g, loss_target, m_norm_g, m_w_in, m_lam_re, m_lam_im, m_b_re, m_b_im, m_c_re, m_c_im, m_d_skip, m_log_dt, m_w_glu, m_b_glu, m_ln_g, m_ln_b, m_w_s, m_b_s, m_w_pool, m_pool_scale, m_w_out, m_final_g, v_norm_g, v_w_in, v_lam_re, v_lam_im, v_b_re, v_b_im, v_c_re, v_c_im, v_d_skip, v_log_dt, v_w_glu, v_b_glu, v_ln_g, v_ln_b, v_w_s, v_b_s, v_w_pool, v_pool_scale, v_w_out, v_final_g):
    wt = dict(norm_g=norm_g, w_in=w_in, lam_re=lam_re, lam_im=lam_im, b_re=b_re, b_im=b_im, c_re=c_re, c_im=c_im,
              d_skip=d_skip, log_dt=log_dt, w_glu=w_glu, b_glu=b_glu, ln_g=ln_g, ln_b=ln_b, w_s=w_s, b_s=b_s,
              w_pool=w_pool, pool_scale=pool_scale, w_out=w_out, final_g=final_g)
    mom = dict(norm_g=m_norm_g, w_in=m_w_in, lam_re=m_lam_re, lam_im=m_lam_im, b_re=m_b_re, b_im=m_b_im, c_re=m_c_re,
               c_im=m_c_im, d_skip=m_d_skip, log_dt=m_log_dt, w_glu=m_w_glu, b_glu=m_b_glu, ln_g=m_ln_g, ln_b=m_ln_b,
               w_s=m_w_s, b_s=m_b_s, w_pool=m_w_pool, pool_scale=m_pool_scale, w_out=m_w_out, final_g=m_final_g)
    vel = dict(norm_g=v_norm_g, w_in=v_w_in, lam_re=v_lam_re, lam_im=v_lam_im, b_re=v_b_re, b_im=v_b_im, c_re=v_c_re,
               c_im=v_c_im, d_skip=v_d_skip, log_dt=v_log_dt, w_glu=v_w_glu, b_glu=v_b_glu, ln_g=v_ln_g, ln_b=v_ln_b,
               w_s=v_w_s, b_s=v_b_s, w_pool=v_w_pool, pool_scale=v_pool_scale, w_out=v_w_out, final_g=v_final_g)
    T = x.shape[1]
    cidx = lax.axis_index("c").astype(jnp.int32).reshape(1)

    g_in, g_glu, g_out = _gather_weights([wt[k].astype(BF16) for k in BIG])

    loss, grad_x, grads = _local_step(x.reshape(T, D_MODEL), loss_target.reshape(T, D_MODEL), wt, g_in, g_glu, g_out)

    part = dict(w_in=grads["w_in"], w_glu=grads["w_glu"].reshape(2, N_CHIPS, S5_W // N_CHIPS, S5_W),
                w_out=grads["w_out"].reshape(2, N_CHIPS, D_MODEL // N_CHIPS, D_MODEL))
    from_sib = _pair_exchange("grad_pair_sum_exchange", [part[k] for k in BIG], True)
    chip_sum = []
    for k, r in zip(BIG, from_sib):
        rows, cols = math.prod(r.shape[:-1]), r.shape[-1]
        chip_sum.append(_add_own_layer(part[k].reshape(2, rows, cols), r.reshape(rows, cols), cidx).reshape(r.shape))
    from_chips = _chip_exchange("grad_chip_exchange", chip_sum, False)
    mine = []
    for r in from_chips:
        rows, cols = math.prod(r.shape[1:-1]), r.shape[-1]
        mine.append(_sum_chips(r.reshape(N_CHIPS, rows, cols)))
    theirs = _pair_exchange("grad_pair_result_exchange", mine, False)

    out_g, out_d, out_m, out_v = {}, {}, {}, {}
    for k, a, b in zip(BIG, mine, theirs):
        shape = wt[k].shape
        rows, cols = a.shape
        flat = lambda t: t.reshape(2, rows, cols)
        g, d, m, v = _adamw_layers(flat(wt[k]), a, b, flat(mom[k]), flat(vel[k]), cidx)
        out_g[k], out_d[k], out_m[k], out_v[k] = (t.reshape(shape) for t in (g, d, m, v))

    packed = _pack([grads[k] for k in SMALL] + [loss[0, 0:1]])
    (sib_packed,) = _pair_exchange("small_pair_exchange", [packed], False)
    chip_packed = _add2(packed, sib_packed)
    (all_packed,) = _chip_exchange("small_chip_broadcast", [chip_packed], True)
    total = _sum_chips(all_packed)
    like = [wt[k] for k in SMALL]
    small_g = _unpack(total, like + [loss[0, 0:1]])
    loss_out = small_g[-1].reshape(())
    w_p, m_p, v_p = _pack(like), _pack([mom[k] for k in SMALL]), _pack([vel[k] for k in SMALL])
    g_p = total[: w_p.shape[0]]
    d_p, mo_p, vo_p = _adamw(w_p, g_p, m_p, v_p)
    for k, g, d, m, v in zip(SMALL, small_g[:-1], _unpack(d_p, like), _unpack(mo_p, like), _unpack(vo_p, like)):
        out_g[k], out_d[k], out_m[k], out_v[k] = g, d, m, v

    return (loss_out, grad_x.reshape(x.shape), *[out_g[k] for k in WEIGHTS], *[out_d[k] for k in WEIGHTS],
            *[out_m[k] for k in WEIGHTS], *[out_v[k] for k in WEIGHTS])
```

```python
import functools
import math

import jax
import jax.numpy as jnp
from jax import lax
from jax.experimental import pallas as pl
from jax.experimental.pallas import tpu as pltpu

F32 = jnp.float32
BF16 = jnp.bfloat16

D_MODEL = 2048
DEPTH = 2
S5_W = 512
SGU_W = 1024
POOL_W = 512
IN_COLS = 5120
N_CHIPS = 4
SHARD_COLS = IN_COLS // N_CHIPS
S5_GROUPS = 32
S5_STATE = 64
S5_CH = 16
STATE_W = S5_GROUPS * S5_STATE
SUPER = 4
CHUNK = 128
SGU_HEADS = 8
POOL_WINDOWS = (2, 4, 8, 16)
POOL_HALO = 16
RMS_EPS = 1e-6
LN_EPS = 1e-5
SCAN_COLS = 512

ADAM_LR = 0.001
ADAM_B1 = 0.9
ADAM_B2 = 0.999
ADAM_EPS = 1e-08
ADAM_WD = 0.01
ADAM_STEP = 10

VMEM_LIMIT = 56 * 1024 * 1024
MESH_ID = pl.DeviceIdType.MESH

_GELU_K0 = math.sqrt(2.0 / math.pi)
_GELU_K1 = 0.044715


def _cparams(n_axes):
    return pltpu.CompilerParams(dimension_semantics=("arbitrary",) * n_axes, vmem_limit_bytes=VMEM_LIMIT)


def _gelu(x):
    t = jnp.tanh(_GELU_K0 * (x + _GELU_K1 * (x * x * x)))
    return 0.5 * x * (1.0 + t)


def _gelu_and_grad(x):
    x2 = x * x
    t = jnp.tanh(_GELU_K0 * (x + _GELU_K1 * (x * x2)))
    g = 0.5 * x * (1.0 + t)
    dg = 0.5 * (1.0 + t) + 0.5 * x * (1.0 - t * t) * (_GELU_K0 * (1.0 + 3.0 * _GELU_K1 * x2))
    return g, dg


def _silu_and_grad(x):
    s = jax.nn.sigmoid(x)
    return x * s, s * (1.0 + x * (1.0 - s))


def _dot(a, b):
    return jnp.dot(a.astype(BF16), b.astype(BF16), preferred_element_type=F32)


def _dot_nt(a, b):
    return lax.dot_general(a.astype(BF16), b.astype(BF16), (((1,), (1,)), ((), ())), preferred_element_type=F32)


def _dot_tn(a, b):
    return lax.dot_general(a.astype(BF16), b.astype(BF16), (((0,), (0,)), ((), ())), preferred_element_type=F32)


def _full(shape):
    nd = len(shape)
    return pl.BlockSpec(shape, lambda *_: (0,) * nd)


HALF_D = D_MODEL // 2


def _inproj(x, g, w, riders=None):
    T = x.shape[0]
    tm = min(512, T)
    ni = T // tm
    n = len(riders) if riders else 0

    def body(*refs):
        x_ref, g_ref, w_ref = refs[:3]
        rin = refs[3:3 + n]
        z_ref, h_ref = refs[3 + n:5 + n]
        rout = refs[5 + n:5 + 2 * n]
        hs_ref = refs[5 + 2 * n]
        i, j = pl.program_id(0), pl.program_id(1)
        if n:
            start, mid, end = _gather_steps(rin, rout, *refs[6 + 2 * n:])
            pl.when((i == 0) & (j == 0))(start)
            pl.when((i == ni // 2) & (j == 0))(mid)

        @pl.when(j == 0)
        def _():
            xv = x_ref[...]
            r = lax.rsqrt(jnp.mean(xv * xv, axis=-1, keepdims=True) + RMS_EPS)
            hv = (xv * r * g_ref[...]).astype(BF16)
            hs_ref[...] = hv
            h_ref[...] = hv

        z_ref[...] = (jnp.dot(hs_ref[:, 0:HALF_D], w_ref[0], preferred_element_type=F32)
                      + jnp.dot(hs_ref[:, HALF_D:D_MODEL], w_ref[1], preferred_element_type=F32))
        if n:
            pl.when((i == ni - 1) & (j == N_CHIPS - 1))(end)

    return pl.pallas_call(
        body,
        name="inproj_gather" if n else "inproj",
        grid=(ni, N_CHIPS),
        in_specs=[
            pl.BlockSpec((tm, D_MODEL), lambda i, j: (i, 0)),
            pl.BlockSpec((1, D_MODEL), lambda i, j: (0, 0)),
            pl.BlockSpec((2, None, HALF_D, SHARD_COLS), lambda i, j: (0, j, 0, 0)),
        ] + [_ANY] * n,
        out_specs=[
            pl.BlockSpec((tm, SHARD_COLS), lambda i, j: (i, j)),
            pl.BlockSpec((tm, D_MODEL), lambda i, j: (i, 0)),
        ] + [_ANY] * n,
        out_shape=[jax.ShapeDtypeStruct((T, IN_COLS), F32), jax.ShapeDtypeStruct((T, D_MODEL), BF16)]
        + _gathered_shapes(riders or []),
        scratch_shapes=[pltpu.VMEM((tm, D_MODEL), BF16)] + (_gather_sems(n) if n else []),
        compiler_params=_cparams(2),
    )(x, g, w, *(riders or []))


def _outproj(ya, yb, yc, w, x):
    T = x.shape[0]
    tm = min(512, T)
    tn = 1024

    def body(ya_ref, yb_ref, yc_ref, w_ref, x_ref, o_ref, y_ref):
        acc = jnp.dot(ya_ref[...], w_ref[0:S5_W, :], preferred_element_type=F32)
        acc += jnp.dot(yb_ref[...], w_ref[S5_W:S5_W + SGU_W, :], preferred_element_type=F32)
        acc += jnp.dot(yc_ref[...], w_ref[S5_W + SGU_W:D_MODEL, :], preferred_element_type=F32)
        o_ref[...] = x_ref[...] + acc

        @pl.when(pl.program_id(1) == 0)
        def _():
            y_ref[:, 0:S5_W] = ya_ref[...]
            y_ref[:, S5_W:S5_W + SGU_W] = yb_ref[...]
            y_ref[:, S5_W + SGU_W:D_MODEL] = yc_ref[...]

    return pl.pallas_call(
        body,
        name="outproj",
        grid=(T // tm, D_MODEL // tn),
        in_specs=[
            pl.BlockSpec((tm, S5_W), lambda i, j: (i, 0)),
            pl.BlockSpec((tm, SGU_W), lambda i, j: (i, 0)),
            pl.BlockSpec((tm, POOL_W), lambda i, j: (i, 0)),
            pl.BlockSpec((D_MODEL, tn), lambda i, j: (0, j)),
            pl.BlockSpec((tm, tn), lambda i, j: (i, j)),
        ],
        out_specs=[
            pl.BlockSpec((tm, tn), lambda i, j: (i, j)),
            pl.BlockSpec((tm, D_MODEL), lambda i, j: (i, 0)),
        ],
        out_shape=[jax.ShapeDtypeStruct((T, D_MODEL), F32), jax.ShapeDtypeStruct((T, D_MODEL), BF16)],
        compiler_params=_cparams(2),
    )(ya, yb, yc, w, x)


def _outproj_bwd_dy(dxo, w):
    T = dxo.shape[0]
    tm = min(512, T)
    tn = 1024

    def body(d_ref, w_ref, o_ref, ds_ref):
        @pl.when(pl.program_id(1) == 0)
        def _():
            ds_ref[...] = d_ref[...].astype(BF16)

        o_ref[...] = lax.dot_general(ds_ref[...], w_ref[...], (((1,), (1,)), ((), ())), preferred_element_type=F32)

    return pl.pallas_call(
        body,
        name="outproj_bwd_dy",
        grid=(T // tm, D_MODEL // tn),
        in_specs=[
            pl.BlockSpec((tm, D_MODEL), lambda i, j: (i, 0)),
            pl.BlockSpec((tn, D_MODEL), lambda i, j: (j, 0)),
        ],
        out_specs=pl.BlockSpec((tm, tn), lambda i, j: (i, j)),
        out_shape=jax.ShapeDtypeStruct((T, D_MODEL), F32),
        scratch_shapes=[pltpu.VMEM((tm, D_MODEL), BF16)],
        compiler_params=_cparams(2),
    )(dxo, w)


def _outproj_bwd_dw(y, dxo):
    T = y.shape[0]
    tm = min(512, T)
    tr = 1024

    def body(y_ref, d_ref, o_ref):
        @pl.when(pl.program_id(1) == 0)
        def _():
            o_ref[...] = jnp.zeros_like(o_ref)

        o_ref[...] += _dot_tn(y_ref[...], d_ref[...])

    return pl.pallas_call(
        body,
        name="outproj_bwd_dw",
        grid=(D_MODEL // tr, T // tm),
        in_specs=[
            pl.BlockSpec((tm, tr), lambda p, t: (t, p)),
            pl.BlockSpec((tm, D_MODEL), lambda p, t: (t, 0)),
        ],
        out_specs=pl.BlockSpec((tr, D_MODEL), lambda p, t: (p, 0)),
        out_shape=jax.ShapeDtypeStruct((D_MODEL, D_MODEL), F32),
        compiler_params=_cparams(2),
    )(y, dxo)


def _inproj_bwd_dw(h, dz):
    T = h.shape[0]
    tm = min(512, T)

    def body(h_ref, dz_ref, o_ref):
        @pl.when(pl.program_id(1) == 0)
        def _():
            o_ref[...] = jnp.zeros_like(o_ref)

        o_ref[...] += _dot_tn(h_ref[...], dz_ref[...])

    return pl.pallas_call(
        body,
        name="inproj_bwd_dw",
        grid=(N_CHIPS, T // tm),
        in_specs=[
            pl.BlockSpec((tm, D_MODEL), lambda j, t: (t, 0)),
            pl.BlockSpec((tm, SHARD_COLS), lambda j, t: (t, j)),
        ],
        out_specs=pl.BlockSpec((None, D_MODEL, SHARD_COLS), lambda j, t: (j, 0, 0)),
        out_shape=jax.ShapeDtypeStruct((N_CHIPS, D_MODEL, SHARD_COLS), F32),
        compiler_params=_cparams(2),
    )(h, dz)


def _inproj_bwd_dx(dz, w4, x, g, dxo):
    T = x.shape[0]
    tm = min(512, T)
    nk = N_CHIPS
    nt = (((1,), (1,)), ((), ()))

    def body(dz_ref, w_ref, x_ref, g_ref, dxo_ref, dx_ref, dg_ref, acc_ref):
        i, j = pl.program_id(0), pl.program_id(1)
        lo = lax.dot_general(dz_ref[...], w_ref[0], nt, preferred_element_type=F32)
        hi = lax.dot_general(dz_ref[...], w_ref[1], nt, preferred_element_type=F32)

        @pl.when(j == 0)
        def _():
            acc_ref[:, 0:HALF_D] = lo
            acc_ref[:, HALF_D:D_MODEL] = hi

        @pl.when(j > 0)
        def _():
            acc_ref[:, 0:HALF_D] += lo
            acc_ref[:, HALF_D:D_MODEL] += hi

        @pl.when(j == nk - 1)
        def _():
            @pl.when(i == 0)
            def _():
                dg_ref[...] = jnp.zeros_like(dg_ref)

            rc = min(128, tm)
            for c in range(tm // rc):
                rows = slice(c * rc, (c + 1) * rc)
                dh = acc_ref[rows, :]
                xv = x_ref[rows, :]
                r = lax.rsqrt(jnp.mean(xv * xv, axis=-1, keepdims=True) + RMS_EPS)
                xh = xv * r
                w = dh * g_ref[...]
                dx_ref[rows, :] = dxo_ref[rows, :] + r * (w - xh * jnp.mean(w * xh, axis=-1, keepdims=True))
                dg_ref[...] += jnp.sum(dh * xh, axis=0, keepdims=True)

    return pl.pallas_call(
        body,
        name="inproj_bwd_dx",
        grid=(T // tm, nk),
        in_specs=[
            pl.BlockSpec((tm, SHARD_COLS), lambda i, j: (i, j)),
            pl.BlockSpec((2, None, HALF_D, SHARD_COLS), lambda i, j: (0, j, 0, 0)),
            pl.BlockSpec((tm, D_MODEL), lambda i, j: (i, 0)),
            pl.BlockSpec((1, D_MODEL), lambda i, j: (0, 0)),
            pl.BlockSpec((tm, D_MODEL), lambda i, j: (i, 0)),
        ],
        out_specs=[
            pl.BlockSpec((tm, D_MODEL), lambda i, j: (i, 0)),
            pl.BlockSpec((1, D_MODEL), lambda i, j: (0, 0)),
        ],
        out_shape=[jax.ShapeDtypeStruct((T, D_MODEL), F32), jax.ShapeDtypeStruct((1, D_MODEL), F32)],
        scratch_shapes=[pltpu.VMEM((tm, D_MODEL), F32)],
        compiler_params=_cparams(2),
    )(dz, w4, x, g, dxo)


def _loss_head(x, g, tgt):
    T = x.shape[0]
    tm = min(512, T)

    def body(x_ref, g_ref, t_ref, dx_ref, l_ref, dg_ref):
        i = pl.program_id(0)
        xv = x_ref[...]
        r = lax.rsqrt(jnp.mean(xv * xv, axis=-1, keepdims=True) + RMS_EPS)
        xh = xv * r
        err = xh * g_ref[...] - t_ref[...]
        lpart = 0.5 * jnp.sum(jnp.mean(err * err, axis=-1, keepdims=True), axis=0, keepdims=True)
        dout = err * (1.0 / D_MODEL)
        w = dout * g_ref[...]
        dx_ref[...] = r * (w - xh * jnp.mean(w * xh, axis=-1, keepdims=True))
        gpart = jnp.sum(dout * xh, axis=0, keepdims=True)

        @pl.when(i == 0)
        def _():
            l_ref[...] = jnp.broadcast_to(lpart, l_ref.shape)
            dg_ref[...] = gpart

        @pl.when(i > 0)
        def _():
            l_ref[...] += jnp.broadcast_to(lpart, l_ref.shape)
            dg_ref[...] += gpart

    return pl.pallas_call(
        body,
        name="loss_head",
        grid=(T // tm,),
        in_specs=[
            pl.BlockSpec((tm, D_MODEL), lambda i: (i, 0)),
            pl.BlockSpec((1, D_MODEL), lambda i: (0, 0)),
            pl.BlockSpec((tm, D_MODEL), lambda i: (i, 0)),
        ],
        out_specs=[
            pl.BlockSpec((tm, D_MODEL), lambda i: (i, 0)),
            pl.BlockSpec((1, 128), lambda i: (0, 0)),
            pl.BlockSpec((1, D_MODEL), lambda i: (0, 0)),
        ],
        out_shape=[
            jax.ShapeDtypeStruct((T, D_MODEL), F32),
            jax.ShapeDtypeStruct((1, 128), F32),
            jax.ShapeDtypeStruct((1, D_MODEL), F32),
        ],
        compiler_params=_cparams(1),
    )(x, g, tgt)


def _s5_prep(lam_re, lam_im, b_re, b_im, log_dt):
    lam = lax.complex(lam_re, lam_im)
    dt = jnp.exp(log_dt)[:, None]
    a = jnp.exp(lam * dt)
    bbar = ((a - 1.0) / lam)[..., None] * lax.complex(b_re, b_im)
    return jnp.real(a), jnp.imag(a), jnp.real(bbar), jnp.imag(bbar)


def _block_diag_in(m):
    m4 = m.reshape(SUPER, 8, S5_STATE, S5_CH)
    eye = jnp.eye(8, dtype=m.dtype)
    out = jnp.einsum("jgph,gk->jghkp", m4, eye)
    return out.reshape(SUPER, 8 * S5_CH, 8 * S5_STATE)


def _block_diag_in_grad(d):
    d6 = d.reshape(SUPER, 8, S5_CH, 8, S5_STATE)
    diag = jnp.einsum("jghgp->jgph", d6)
    return diag.reshape(S5_GROUPS, S5_STATE, S5_CH)


def _block_diag_out(m):
    m4 = m.reshape(SUPER, 8, S5_CH, S5_STATE)
    eye = jnp.eye(8, dtype=m.dtype)
    out = jnp.einsum("jghp,gk->jgpkh", m4, eye)
    return out.reshape(SUPER, 8 * S5_STATE, 8 * S5_CH)


def _block_diag_out_grad(d):
    d6 = d.reshape(SUPER, 8, S5_STATE, 8, S5_CH)
    diag = jnp.einsum("jgpgh->jghp", d6)
    return diag.reshape(S5_GROUPS, S5_CH, S5_STATE)


def _scan_coefs(a_re, a_im, reverse):
    a = lax.complex(a_re.reshape(-1), a_im.reshape(-1))
    if reverse:
        a = jnp.conj(a)
    pw = [a]
    for _ in range(7):
        pw.append(pw[-1] * a)
    rows = jnp.arange(8)

    def masked(k):
        m = (rows + k <= 7) if reverse else (rows >= k)
        return jnp.where(m[:, None], pw[k - 1][None, :], 0.0)

    a1, a2, a4 = masked(1), masked(2), masked(4)
    carry = jnp.stack([pw[7 - r] for r in range(8)]) if reverse else jnp.stack(pw)
    parts = []
    for c in (a1, a2, a4, carry):
        parts += [jnp.real(c), jnp.imag(c)]
    return jnp.stack(parts).astype(F32)


def _scan_block(r, im, coef_ref, cs, reverse):
    for k, idx in ((1, 0), (2, 2), (4, 4)):
        ar = coef_ref[idx, :, cs]
        ai = coef_ref[idx + 1, :, cs]
        sh = 8 - k if reverse else k
        rr = pltpu.roll(r, sh, 0)
        ri = pltpu.roll(im, sh, 0)
        r, im = r + ar * rr - ai * ri, im + ar * ri + ai * rr
    return r, im


def _s5_fwd(z, p):
    T = z.shape[0]
    tm = min(256, T)
    nblk = tm // 8
    W = STATE_W

    def body(xa_ref, ga_ref, bre_ref, bim_ref, cre_ref, cim_ref, dv_ref, wg_ref, bg_ref, coef_ref,
             ya_ref, yraw_ref, sre_ref, sim_ref, wre, wim):
        @pl.when(pl.program_id(0) == 0)
        def _():
            wre[0:8, :] = jnp.zeros((8, W), F32)
            wim[0:8, :] = jnp.zeros((8, W), F32)

        xa = xa_ref[...]
        xab = xa.astype(BF16)
        for j in range(SUPER):
            xj = xab[:, j * 128:(j + 1) * 128]
            wre[8:8 + tm, j * 512:(j + 1) * 512] = jnp.dot(xj, bre_ref[j], preferred_element_type=F32)
            wim[8:8 + tm, j * 512:(j + 1) * 512] = jnp.dot(xj, bim_ref[j], preferred_element_type=F32)

        def blk(b, carry):
            base = pl.multiple_of(8 + b * 8, 8)
            for cc in range(W // SCAN_COLS):
                cs = pl.ds(cc * SCAN_COLS, SCAN_COLS)
                r, im = _scan_block(wre[pl.ds(base, 8), cs], wim[pl.ds(base, 8), cs], coef_ref, cs, False)
                cr = wre[pl.ds(base - 1, 1), cs]
                ci = wim[pl.ds(base - 1, 1), cs]
                pr = coef_ref[6, :, cs]
                pi = coef_ref[7, :, cs]
                wre[pl.ds(base, 8), cs] = r + pr * cr - pi * ci
                wim[pl.ds(base, 8), cs] = im + pr * ci + pi * cr
            return carry

        lax.fori_loop(0, nblk, blk, 0)
        wre[0:8, :] = wre[tm:tm + 8, :]
        wim[0:8, :] = wim[tm:tm + 8, :]
        sre_ref[...] = wre[8:8 + tm, :]
        sim_ref[...] = wim[8:8 + tm, :]

        for j in range(SUPER):
            yr = jnp.dot(wre[8:8 + tm, j * 512:(j + 1) * 512].astype(BF16), cre_ref[j], preferred_element_type=F32)
            yr += jnp.dot(wim[8:8 + tm, j * 512:(j + 1) * 512].astype(BF16), cim_ref[j], preferred_element_type=F32)
            yraw_ref[:, j * 128:(j + 1) * 128] = yr
        yraw = yraw_ref[...] + dv_ref[...] * xa
        yraw_ref[...] = yraw
        yg = _gelu(yraw)
        q = jnp.dot(yg.astype(BF16), wg_ref[...], preferred_element_type=F32) + bg_ref[...]
        sga, _ = _silu_and_grad(ga_ref[...])
        ya_ref[...] = (yg * jax.nn.sigmoid(q) * sga).astype(BF16)

    return pl.pallas_call(
        body,
        name="s5_fwd",
        grid=(T // tm,),
        in_specs=[
            pl.BlockSpec((tm, S5_W), lambda i: (i, 0)),
            pl.BlockSpec((tm, S5_W), lambda i: (i, 6)),
            _full((SUPER, 128, 512)), _full((SUPER, 128, 512)),
            _full((SUPER, 512, 128)), _full((SUPER, 512, 128)),
            _full((1, S5_W)), _full((S5_W, S5_W)), _full((1, S5_W)),
            _full((8, 8, W)),
        ],
        out_specs=[
            pl.BlockSpec((tm, S5_W), lambda i: (i, 0)),
            pl.BlockSpec((tm, S5_W), lambda i: (i, 0)),
            pl.BlockSpec((tm, W), lambda i: (i, 0)),
            pl.BlockSpec((tm, W), lambda i: (i, 0)),
        ],
        out_shape=[
            jax.ShapeDtypeStruct((T, S5_W), BF16),
            jax.ShapeDtypeStruct((T, S5_W), F32),
            jax.ShapeDtypeStruct((T, W), F32),
            jax.ShapeDtypeStruct((T, W), F32),
        ],
        scratch_shapes=[pltpu.VMEM((tm + 8, W), F32), pltpu.VMEM((tm + 8, W), F32)],
        compiler_params=_cparams(1),
    )(z, z, p["b4re"], p["b4im"], p["c4re"], p["c4im"], p["dvec"], p["wglu"], p["bglu"], p["coef_f"])


def _s5_bwd(dy, z, yraw, sre, sim, p):
    T = z.shape[0]
    tm = min(256, T)
    nt = T // tm
    nblk = tm // 8
    W = STATE_W
    rev = lambda i: nt - 1 - i

    def body(dya_ref, xa_ref, ga_ref, yraw_ref, sre_ref, sim_ref, hre_ref, him_ref,
             bre_t_ref, bim_t_ref, cre_t_ref, cim_t_ref, dv_ref, wg_ref, wgt_ref, bg_ref, coef_ref,
             dxa_ref, dga_ref, dbre_ref, dbim_ref, dcre_ref, dcim_ref, dd_ref, dwg_ref, dbg_ref, da_ref,
             wre, wim, dyr_ref):
        i = pl.program_id(0)

        @pl.when(i == 0)
        def _():
            wre[tm:tm + 8, :] = jnp.zeros((8, W), F32)
            wim[tm:tm + 8, :] = jnp.zeros((8, W), F32)
            for ref in (dbre_ref, dbim_ref, dcre_ref, dcim_ref, dd_ref, dwg_ref, dbg_ref, da_ref):
                ref[...] = jnp.zeros_like(ref)

        xa = xa_ref[...]
        dya = dya_ref[...]
        yg, dgelu = _gelu_and_grad(yraw_ref[...])
        ygb = yg.astype(BF16)
        q = jnp.dot(ygb, wg_ref[...], preferred_element_type=F32) + bg_ref[...]
        sq = jax.nn.sigmoid(q)
        sga, dsga = _silu_and_grad(ga_ref[...])
        dga_ref[...] = (dya * (yg * sq) * dsga).astype(BF16)
        dya0 = dya * sga
        dq = dya0 * yg * sq * (1.0 - sq)
        dqb = dq.astype(BF16)
        dyg = dya0 * sq + jnp.dot(dqb, wgt_ref[...], preferred_element_type=F32)
        dwg_ref[...] += _dot_tn(ygb, dqb)
        dbg_ref[...] += jnp.sum(dq, axis=0, keepdims=True)
        dyraw = dyg * dgelu
        dd_ref[...] += jnp.sum(dyraw * xa, axis=0, keepdims=True)
        dyr_ref[...] = dyraw.astype(BF16)

        for j in range(SUPER):
            dj = dyr_ref[:, j * 128:(j + 1) * 128]
            wre[0:tm, j * 512:(j + 1) * 512] = jnp.dot(dj, cre_t_ref[j], preferred_element_type=F32)
            wim[0:tm, j * 512:(j + 1) * 512] = jnp.dot(dj, cim_t_ref[j], preferred_element_type=F32)

        row0 = lax.broadcasted_iota(jnp.int32, (8, SCAN_COLS), 0) == 0
        head_on = (i < nt - 1).astype(F32)

        def one_block(base, first):
            for cc in range(W // SCAN_COLS):
                cs = pl.ds(cc * SCAN_COLS, SCAN_COLS)
                r, im = _scan_block(wre[pl.ds(base, 8), cs], wim[pl.ds(base, 8), cs], coef_ref, cs, True)
                cr = wre[pl.ds(base + 8, 1), cs]
                ci = wim[pl.ds(base + 8, 1), cs]
                pr = coef_ref[6, :, cs]
                pi = coef_ref[7, :, cs]
                r, im = r + pr * cr - pi * ci, im + pr * ci + pi * cr
                wre[pl.ds(base, 8), cs] = r
                wim[pl.ds(base, 8), cs] = im
                if first:
                    pre = hre_ref[7:8, cs] * head_on
                    pim = him_ref[7:8, cs] * head_on
                else:
                    pre = sre_ref[pl.ds(base - 1, 1), cs]
                    pim = sim_ref[pl.ds(base - 1, 1), cs]
                spr = jnp.where(row0, pre, pltpu.roll(sre_ref[pl.ds(base, 8), cs], 1, 0))
                spi = jnp.where(row0, pim, pltpu.roll(sim_ref[pl.ds(base, 8), cs], 1, 0))
                da_ref[0, :, cs] += r * spr + im * spi
                da_ref[1, :, cs] += im * spr - r * spi

        def blk(b, carry):
            one_block(pl.multiple_of((nblk - 1 - b) * 8, 8), False)
            return carry

        lax.fori_loop(0, nblk - 1, blk, 0)
        one_block(0, True)
        wre[tm:tm + 8, :] = wre[0:8, :]
        wim[tm:tm + 8, :] = wim[0:8, :]

        xab = xa.astype(BF16)
        for j in range(SUPER):
            cols = slice(j * 512, (j + 1) * 512)
            gre = wre[0:tm, cols].astype(BF16)
            gim = wim[0:tm, cols].astype(BF16)
            xj = xab[:, j * 128:(j + 1) * 128]
            dj = dyr_ref[:, j * 128:(j + 1) * 128]
            dbre_ref[j] += _dot_tn(xj, gre)
            dbim_ref[j] += _dot_tn(xj, gim)
            dcre_ref[j] += _dot_tn(sre_ref[:, cols], dj)
            dcim_ref[j] += _dot_tn(sim_ref[:, cols], dj)
            dxj = jnp.dot(gre, bre_t_ref[j], preferred_element_type=F32)
            dxj += jnp.dot(gim, bim_t_ref[j], preferred_element_type=F32)
            dxj += dyraw[:, j * 128:(j + 1) * 128] * dv_ref[:, j * 128:(j + 1) * 128]
            dxa_ref[:, j * 128:(j + 1) * 128] = dxj.astype(BF16)

    acc = lambda shape: _full(shape)
    hb = tm // 8
    return pl.pallas_call(
        body,
        name="s5_bwd",
        grid=(nt,),
        in_specs=[
            pl.BlockSpec((tm, S5_W), lambda i: (rev(i), 0)),
            pl.BlockSpec((tm, S5_W), lambda i: (rev(i), 0)),
            pl.BlockSpec((tm, S5_W), lambda i: (rev(i), 6)),
            pl.BlockSpec((tm, S5_W), lambda i: (rev(i), 0)),
            pl.BlockSpec((tm, W), lambda i: (rev(i), 0)),
            pl.BlockSpec((tm, W), lambda i: (rev(i), 0)),
            pl.BlockSpec((8, W), lambda i: (jnp.maximum(rev(i) * hb - 1, 0), 0)),
            pl.BlockSpec((8, W), lambda i: (jnp.maximum(rev(i) * hb - 1, 0), 0)),
            _full((SUPER, 512, 128)), _full((SUPER, 512, 128)),
            _full((SUPER, 128, 512)), _full((SUPER, 128, 512)),
            _full((1, S5_W)), _full((S5_W, S5_W)), _full((S5_W, S5_W)), _full((1, S5_W)),
            _full((8, 8, W)),
        ],
        out_specs=[
            pl.BlockSpec((tm, S5_W), lambda i: (rev(i), 0)),
            pl.BlockSpec((tm, S5_W), lambda i: (rev(i), 0)),
            acc((SUPER, 128, 512)), acc((SUPER, 128, 512)),
            acc((SUPER, 512, 128)), acc((SUPER, 512, 128)),
            acc((1, S5_W)), acc((S5_W, S5_W)), acc((1, S5_W)), acc((2, 8, W)),
        ],
        out_shape=[
            jax.ShapeDtypeStruct((T, S5_W), BF16),
            jax.ShapeDtypeStruct((T, S5_W), BF16),
            jax.ShapeDtypeStruct((SUPER, 128, 512), F32), jax.ShapeDtypeStruct((SUPER, 128, 512), F32),
            jax.ShapeDtypeStruct((SUPER, 512, 128), F32), jax.ShapeDtypeStruct((SUPER, 512, 128), F32),
            jax.ShapeDtypeStruct((1, S5_W), F32), jax.ShapeDtypeStruct((S5_W, S5_W), F32),
            jax.ShapeDtypeStruct((1, S5_W), F32), jax.ShapeDtypeStruct((2, 8, W), F32),
        ],
        scratch_shapes=[pltpu.VMEM((tm + 8, W), F32), pltpu.VMEM((tm + 8, W), F32), pltpu.VMEM((tm, S5_W), BF16)],
        compiler_params=_cparams(1),
    )(dy, z, z, yraw, sre, sim, sre, sim,
      p["b4re_t"], p["b4im_t"], p["c4re_t"], p["c4im_t"], p["dvec"], p["wglu"], p["wglu_t"], p["bglu"], p["coef_r"])


def _ln_fwd(vf, lng, lnb):
    mu = jnp.mean(vf, axis=-1, keepdims=True)
    d = vf - mu
    rstd = lax.rsqrt(jnp.mean(d * d, axis=-1, keepdims=True) + LN_EPS)
    xh = d * rstd
    return xh, rstd, xh * lng + lnb


def _col_block(tm, b):
    return pl.BlockSpec((tm, 512), lambda i: (i, b))


def _ln_halves(vf0, vf1):
    mu = (jnp.sum(vf0, axis=-1, keepdims=True) + jnp.sum(vf1, axis=-1, keepdims=True)) * (1.0 / SGU_W)
    d0, d1 = vf0 - mu, vf1 - mu
    var = (jnp.sum(d0 * d0, axis=-1, keepdims=True) + jnp.sum(d1 * d1, axis=-1, keepdims=True)) * (1.0 / SGU_W)
    rstd = lax.rsqrt(var + LN_EPS)
    return d0 * rstd, d1 * rstd, rstd


def _sgu_fwd(z, ws, bsf, lng, lnb):
    T = z.shape[0]
    tm = min(512, T)

    def body(u0, u1, v0, v1, g0, g1, ws_ref, bs_ref, lng_ref, lnb_ref, yb_ref, vn_ref):
        for c in range(tm // CHUNK):
            rows = slice(c * CHUNK, (c + 1) * CHUNK)
            xh0, xh1, _ = _ln_halves(_gelu(v0[rows, :]), _gelu(v1[rows, :]))
            vn_ref[:, 0:512] = (xh0 * lng_ref[:, 0:512] + lnb_ref[:, 0:512]).astype(BF16)
            vn_ref[:, 512:1024] = (xh1 * lng_ref[:, 512:1024] + lnb_ref[:, 512:1024]).astype(BF16)
            for half, (u_ref, g_ref) in enumerate(((u0, g0), (u1, g1))):
                sg, _ = _silu_and_grad(g_ref[rows, :])
                m = _gelu(u_ref[rows, :]) * sg
                for hh in range(SGU_HEADS // 2):
                    h = half * (SGU_HEADS // 2) + hh
                    cols = slice(h * 128, (h + 1) * 128)
                    s = jnp.dot(ws_ref[h], vn_ref[:, cols], preferred_element_type=F32) + bs_ref[:, cols]
                    yb_ref[rows, cols] = (m[:, hh * 128:(hh + 1) * 128] * s).astype(BF16)

    return pl.pallas_call(
        body,
        name="sgu_fwd",
        grid=(T // tm,),
        in_specs=[_col_block(tm, b) for b in (1, 2, 3, 4, 7, 8)] + [
            _full((SGU_HEADS, CHUNK, CHUNK)), _full((CHUNK, SGU_W)), _full((1, SGU_W)), _full((1, SGU_W)),
        ],
        out_specs=pl.BlockSpec((tm, SGU_W), lambda i: (i, 0)),
        out_shape=jax.ShapeDtypeStruct((T, SGU_W), BF16),
        scratch_shapes=[pltpu.VMEM((CHUNK, SGU_W), BF16)],
        compiler_params=_cparams(1),
    )(z, z, z, z, z, z, ws, bsf, lng, lnb)


def _sgu_bwd(dy, z, ws, ws_t, bsf, lng, lnb):
    T = z.shape[0]
    tm = min(512, T)
    HH = SGU_HEADS // 2

    def body(u0, u1, v0, v1, g0, g1, dy0, dy1, ws_ref, wst_ref, bs_ref, lng_ref, lnb_ref,
             du_ref, dv_ref, dgb_ref, dws_ref, dbs_ref, dlng_ref, dlnb_ref, vn_ref, dvn_ref):
        @pl.when(pl.program_id(0) == 0)
        def _():
            for ref in (dws_ref, dbs_ref, dlng_ref, dlnb_ref):
                ref[...] = jnp.zeros_like(ref)

        for c in range(tm // CHUNK):
            rows = slice(c * CHUNK, (c + 1) * CHUNK)
            vf0, dgv0 = _gelu_and_grad(v0[rows, :])
            vf1, dgv1 = _gelu_and_grad(v1[rows, :])
            xh0, xh1, rstd = _ln_halves(vf0, vf1)
            vn_ref[:, 0:512] = (xh0 * lng_ref[:, 0:512] + lnb_ref[:, 0:512]).astype(BF16)
            vn_ref[:, 512:1024] = (xh1 * lng_ref[:, 512:1024] + lnb_ref[:, 512:1024]).astype(BF16)
            for half, (u_ref, g_ref, dy_ref) in enumerate(((u0, g0, dy0), (u1, g1, dy1))):
                ug, dgu = _gelu_and_grad(u_ref[rows, :])
                sg, dsg = _silu_and_grad(g_ref[rows, :])
                dyb = dy_ref[rows, :]
                dyb0 = dyb * sg
                ds_half = dyb0 * ug
                du_scale = dyb0 * dgu
                dg_scale = dyb * ug * dsg
                for hh in range(HH):
                    h = half * HH + hh
                    cols = slice(h * 128, (h + 1) * 128)
                    lc = slice(hh * 128, (hh + 1) * 128)
                    s = jnp.dot(ws_ref[h], vn_ref[:, cols], preferred_element_type=F32) + bs_ref[:, cols]
                    du_ref[rows, cols] = (du_scale[:, lc] * s).astype(BF16)
                    dgb_ref[rows, cols] = (dg_scale[:, lc] * s).astype(BF16)
                    ds = ds_half[:, lc]
                    dbs_ref[:, cols] += ds
                    dsb = ds.astype(BF16)
                    dws_ref[h] += _dot_nt(dsb, vn_ref[:, cols])
                    dvn_ref[:, cols] = jnp.dot(wst_ref[h], dsb, preferred_element_type=F32)
            dvn0 = dvn_ref[:, 0:512]
            dvn1 = dvn_ref[:, 512:1024]
            dlnb_ref[:, 0:512] += jnp.sum(dvn0, axis=0, keepdims=True)
            dlnb_ref[:, 512:1024] += jnp.sum(dvn1, axis=0, keepdims=True)
            dlng_ref[:, 0:512] += jnp.sum(dvn0 * xh0, axis=0, keepdims=True)
            dlng_ref[:, 512:1024] += jnp.sum(dvn1 * xh1, axis=0, keepdims=True)
            dxh0 = dvn0 * lng_ref[:, 0:512]
            dxh1 = dvn1 * lng_ref[:, 512:1024]
            m1 = (jnp.sum(dxh0, axis=-1, keepdims=True) + jnp.sum(dxh1, axis=-1, keepdims=True)) * (1.0 / SGU_W)
            m2 = (jnp.sum(dxh0 * xh0, axis=-1, keepdims=True) + jnp.sum(dxh1 * xh1, axis=-1, keepdims=True)) * (1.0 / SGU_W)
            dv_ref[rows, 0:512] = (rstd * (dxh0 - m1 - xh0 * m2) * dgv0).astype(BF16)
            dv_ref[rows, 512:1024] = (rstd * (dxh1 - m1 - xh1 * m2) * dgv1).astype(BF16)

    row_out = pl.BlockSpec((tm, SGU_W), lambda i: (i, 0))
    return pl.pallas_call(
        body,
        name="sgu_bwd",
        grid=(T // tm,),
        in_specs=[_col_block(tm, b) for b in (1, 2, 3, 4, 7, 8)] + [_col_block(tm, 1), _col_block(tm, 2)] + [
            _full((SGU_HEADS, CHUNK, CHUNK)), _full((SGU_HEADS, CHUNK, CHUNK)),
            _full((CHUNK, SGU_W)), _full((1, SGU_W)), _full((1, SGU_W)),
        ],
        out_specs=[row_out, row_out, row_out,
                   _full((SGU_HEADS, CHUNK, CHUNK)), _full((CHUNK, SGU_W)), _full((1, SGU_W)), _full((1, SGU_W))],
        out_shape=[
            jax.ShapeDtypeStruct((T, SGU_W), BF16), jax.ShapeDtypeStruct((T, SGU_W), BF16),
            jax.ShapeDtypeStruct((T, SGU_W), BF16),
            jax.ShapeDtypeStruct((SGU_HEADS, CHUNK, CHUNK), F32), jax.ShapeDtypeStruct((CHUNK, SGU_W), F32),
            jax.ShapeDtypeStruct((1, SGU_W), F32), jax.ShapeDtypeStruct((1, SGU_W), F32),
        ],
        scratch_shapes=[pltpu.VMEM((CHUNK, SGU_W), BF16), pltpu.VMEM((CHUNK, SGU_W), F32)],
        compiler_params=_cparams(1),
    )(z, z, z, z, z, z, dy, dy, ws, ws_t, bsf, lng, lnb)


def _pool_den(first_row, n):
    return (lax.broadcasted_iota(jnp.int32, (n, 1), 0) + first_row + 1).astype(F32)


def _pool_p(ext, xc, pos, tm):
    w2 = ext + pltpu.roll(ext, 1, 0)
    w4 = w2 + pltpu.roll(w2, 2, 0)
    w8 = w4 + pltpu.roll(w4, 4, 0)
    w16 = w8 + pltpu.roll(w8, 8, 0)
    out = []
    for g, (w, ws) in enumerate(zip(POOL_WINDOWS, (w2, w4, w8, w16))):
        cols = slice(g * 128, (g + 1) * 128)
        mean = ws[POOL_HALO:POOL_HALO + tm, cols] / jnp.minimum(pos, float(w))
        out.append(mean - xc[:, cols])
    return out


def _pool_fwd(z, wp, scale):
    T = z.shape[0]
    tm = min(512, T)
    hb = tm // POOL_HALO

    def body(xc_ref, hx_ref, gc_ref, wp_ref, sc_ref, yc_ref):
        i = pl.program_id(0)
        xc = xc_ref[...]
        halo = hx_ref[...] * (i > 0).astype(F32)
        ext = jnp.concatenate([halo, xc], axis=0)
        ps = _pool_p(ext, xc, _pool_den(i * tm, tm), tm)
        sg, _ = _silu_and_grad(gc_ref[...])
        for g in range(4):
            cols = slice(g * 128, (g + 1) * 128)
            pw = _dot(ps[g], wp_ref[g])
            yc_ref[:, cols] = (pw * sc_ref[:, cols] * sg[:, cols]).astype(BF16)

    return pl.pallas_call(
        body,
        name="pool_fwd",
        grid=(T // tm,),
        in_specs=[
            _col_block(tm, 5),
            pl.BlockSpec((POOL_HALO, 512), lambda i: (jnp.maximum(i * hb - 1, 0), 5)),
            _col_block(tm, 9),
            _full((4, 128, 128)), _full((1, POOL_W)),
        ],
        out_specs=pl.BlockSpec((tm, POOL_W), lambda i: (i, 0)),
        out_shape=jax.ShapeDtypeStruct((T, POOL_W), BF16),
        compiler_params=_cparams(1),
    )(z, z, z, wp, scale)


def _pool_bwd(dy, z, wp, wp_t, scale):
    T = z.shape[0]
    tm = min(512, T)
    nt = T // tm
    hb = tm // POOL_HALO
    last_hb = T // POOL_HALO - 1
    L = tm + POOL_HALO

    def body(xc_ref, hx_ref, gc_ref, gn_ref, dyc_ref, dyn_ref, wp_ref, wpt_ref, sc_ref,
             dxc_ref, dgc_ref, dwp_ref, dsc_ref):
        i = pl.program_id(0)

        @pl.when(i == 0)
        def _():
            dwp_ref[...] = jnp.zeros_like(dwp_ref)
            dsc_ref[...] = jnp.zeros_like(dsc_ref)

        xc = xc_ref[...]
        halo = hx_ref[...] * (i > 0).astype(F32)
        pos = _pool_den(i * tm, tm)
        ps = _pool_p(jnp.concatenate([halo, xc], axis=0), xc, pos, tm)
        sg, dsg = _silu_and_grad(gc_ref[...])
        dyc = dyc_ref[...]
        dyc0 = dyc * sg
        dpw = dyc0 * sc_ref[...]
        sgn, _ = _silu_and_grad(gn_ref[...])
        dpwn = dyn_ref[...] * sgn * sc_ref[...] * (i < nt - 1).astype(F32)
        posn = _pool_den((i + 1) * tm, POOL_HALO)
        dps, qs = [], []
        for g, w in enumerate(POOL_WINDOWS):
            cols = slice(g * 128, (g + 1) * 128)
            pw = _dot(ps[g], wp_ref[g])
            dgc_ref[:, cols] = (dyc[:, cols] * pw * sc_ref[:, cols] * dsg[:, cols]).astype(BF16)
            dsc_ref[:, cols] += jnp.sum(dyc0[:, cols] * pw, axis=0, keepdims=True)
            dwp_ref[g] += _dot_tn(ps[g], dpw[:, cols])
            dp = _dot(dpw[:, cols], wpt_ref[g])
            dpn = _dot(dpwn[:, cols], wpt_ref[g])
            dps.append(dp)
            qs.append(jnp.concatenate([dp / jnp.minimum(pos, float(w)), dpn / jnp.minimum(posn, float(w))], axis=0))
        ext = jnp.concatenate(qs, axis=1)
        f2 = ext + pltpu.roll(ext, L - 1, 0)
        f4 = f2 + pltpu.roll(f2, L - 2, 0)
        f8 = f4 + pltpu.roll(f4, L - 4, 0)
        f16 = f8 + pltpu.roll(f8, L - 8, 0)
        for g, f in enumerate((f2, f4, f8, f16)):
            cols = slice(g * 128, (g + 1) * 128)
            dxc_ref[:, cols] = (f[0:tm, cols] - dps[g]).astype(BF16)

    nxt = lambda i: jnp.minimum((i + 1) * hb, last_hb)
    return pl.pallas_call(
        body,
        name="pool_bwd",
        grid=(nt,),
        in_specs=[
            _col_block(tm, 5),
            pl.BlockSpec((POOL_HALO, 512), lambda i: (jnp.maximum(i * hb - 1, 0), 5)),
            _col_block(tm, 9),
            pl.BlockSpec((POOL_HALO, 512), lambda i: (nxt(i), 9)),
            _col_block(tm, 3),
            pl.BlockSpec((POOL_HALO, 512), lambda i: (nxt(i), 3)),
            _full((4, 128, 128)), _full((4, 128, 128)), _full((1, POOL_W)),
        ],
        out_specs=[
            pl.BlockSpec((tm, POOL_W), lambda i: (i, 0)), pl.BlockSpec((tm, POOL_W), lambda i: (i, 0)),
            _full((4, 128, 128)), _full((1, POOL_W)),
        ],
        out_shape=[
            jax.ShapeDtypeStruct((T, POOL_W), BF16), jax.ShapeDtypeStruct((T, POOL_W), BF16),
            jax.ShapeDtypeStruct((4, 128, 128), F32), jax.ShapeDtypeStruct((1, POOL_W), F32),
        ],
        compiler_params=_cparams(1),
    )(z, z, z, z, dy, dy, wp, wp_t, scale)


def _row_tile(rows, cols):
    tr = 8
    while tr * 2 * cols * 4 <= 2 * 1024 * 1024 and rows % (tr * 2) == 0:
        tr *= 2
    return tr


def _add_own_layer(p0, p1, recv, cidx):
    R, C = recv.shape
    tr = _row_tile(R, C)

    def body(c_ref, a_ref, b_ref, r_ref, o_ref):
        own = jnp.where(c_ref[0] == 0, a_ref[...], b_ref[...])
        o_ref[...] = (own + r_ref[...]).astype(BF16)

    return pl.pallas_call(
        body,
        name="add_own_layer",
        grid_spec=pltpu.PrefetchScalarGridSpec(
            num_scalar_prefetch=1,
            grid=(R // tr,),
            in_specs=[
                pl.BlockSpec((tr, C), lambda i, c: (i * (1 - c[0]), 0)),
                pl.BlockSpec((tr, C), lambda i, c: (i * c[0], 0)),
                pl.BlockSpec((tr, C), lambda i, c: (i, 0)),
            ],
            out_specs=pl.BlockSpec((tr, C), lambda i, c: (i, 0)),
        ),
        out_shape=jax.ShapeDtypeStruct((R, C), BF16),
        compiler_params=_cparams(1),
    )(cidx, p0, p1, recv)


def _add2(a, b):
    R, C = a.shape
    tr = _row_tile(R, C)

    def body(a_ref, b_ref, o_ref):
        o_ref[...] = a_ref[...] + b_ref[...]

    spec = pl.BlockSpec((tr, C), lambda i: (i, 0))
    return pl.pallas_call(
        body, name="add2", grid=(R // tr,), in_specs=[spec, spec], out_specs=spec,
        out_shape=jax.ShapeDtypeStruct((R, C), F32), compiler_params=_cparams(1),
    )(a, b)


def _sum_chips(parts):
    _, R, C = parts.shape
    tr = _row_tile(R, N_CHIPS * C)

    def body(p_ref, o_ref):
        p = [p_ref[j].astype(F32) for j in range(N_CHIPS)]
        o_ref[...] = ((p[0] + p[1]) + p[2]) + p[3]

    return pl.pallas_call(
        body, name="sum_chips", grid=(R // tr,),
        in_specs=[pl.BlockSpec((N_CHIPS, tr, C), lambda i: (0, i, 0))],
        out_specs=pl.BlockSpec((tr, C), lambda i: (i, 0)),
        out_shape=jax.ShapeDtypeStruct((R, C), F32), compiler_params=_cparams(1),
    )(parts)


def _adamw_math(w, g, m, v):
    m = ADAM_B1 * m + (1.0 - ADAM_B1) * g
    v = ADAM_B2 * v + (1.0 - ADAM_B2) * (g * g)
    m_hat = m / (1.0 - ADAM_B1 ** ADAM_STEP)
    v_hat = v / (1.0 - ADAM_B2 ** ADAM_STEP)
    delta = -ADAM_LR * (m_hat / (jnp.sqrt(v_hat) + ADAM_EPS) + ADAM_WD * w)
    return delta, m, v


def _adamw(w, g, m, v):
    R, C = w.shape
    tr = _row_tile(R, C)

    def body(w_ref, g_ref, m_ref, v_ref, d_ref, mo_ref, vo_ref):
        d_ref[...], mo_ref[...], vo_ref[...] = _adamw_math(w_ref[...], g_ref[...], m_ref[...], v_ref[...])

    spec = pl.BlockSpec((tr, C), lambda i: (i, 0))
    shp = jax.ShapeDtypeStruct((R, C), F32)
    return pl.pallas_call(
        body, name="adamw", grid=(R // tr,), in_specs=[spec] * 4, out_specs=[spec] * 3,
        out_shape=[shp] * 3, compiler_params=_cparams(1),
    )(w, g, m, v)


def _adamw_layers(w, mine, theirs, m, v, cidx):
    _, R, C = w.shape
    tr = _row_tile(R, C)

    def body(c_ref, w_ref, a_ref, b_ref, m_ref, v_ref, g_ref, d_ref, mo_ref, vo_ref):
        g = jnp.where(pl.program_id(0) == c_ref[0], a_ref[...], b_ref[...])
        g_ref[...] = g
        d_ref[...], mo_ref[...], vo_ref[...] = _adamw_math(w_ref[...], g, m_ref[...], v_ref[...])

    lay = pl.BlockSpec((None, tr, C), lambda l, i, c: (l, i, 0))
    flat = pl.BlockSpec((tr, C), lambda l, i, c: (i, 0))
    shp = jax.ShapeDtypeStruct((2, R, C), F32)
    return pl.pallas_call(
        body,
        name="adamw_layers",
        grid_spec=pltpu.PrefetchScalarGridSpec(
            num_scalar_prefetch=1, grid=(2, R // tr),
            in_specs=[lay, flat, flat, lay, lay], out_specs=[lay] * 4,
        ),
        out_shape=[shp] * 4,
        compiler_params=_cparams(2),
    )(cidx, w, mine, theirs, m, v)


_ANY = pl.BlockSpec(memory_space=pl.ANY)


def _mesh_pos():
    return lax.axis_index("x"), lax.axis_index("y"), lax.axis_index("c")


def _other_chips(x, y):
    return [(2 * x + (1 - y), x, 1 - y), (2 * (1 - x) + y, 1 - x, y), (2 * (1 - x) + (1 - y), 1 - x, 1 - y)]


def _gathered_shapes(shards):
    return [jax.ShapeDtypeStruct((2, N_CHIPS) + s.shape[1:], s.dtype) for s in shards]


def _gather_sems(n):
    return [pltpu.SemaphoreType.DMA((2 * n,)), pltpu.SemaphoreType.DMA((6 * n,)), pltpu.SemaphoreType.DMA((6 * n,))]


def _gather_steps(ins, outs, lsem, ssem, rsem):
    n = len(ins)
    x, y, c = _mesh_pos()
    me = 2 * x + y
    sib = (x, y, 1 - c)
    chips = _other_chips(x, y)

    def ici(k, d):
        return pltpu.make_async_remote_copy(
            ins[k].at[c], outs[k].at[c, me], ssem.at[6 * k + d], rsem.at[6 * k + d],
            device_id=(chips[d][1], chips[d][2], c), device_id_type=MESH_ID)

    def landed(k, d):
        return pltpu.make_async_remote_copy(
            ins[k].at[c], outs[k].at[c, chips[d][0]], ssem.at[6 * k + d], rsem.at[6 * k + d],
            device_id=sib, device_id_type=MESH_ID)

    def fwd(k, d, half):
        return pltpu.make_async_remote_copy(
            outs[k].at[half, chips[d][0]], outs[k].at[half, chips[d][0]], ssem.at[6 * k + 3 + d],
            rsem.at[6 * k + 3 + d], device_id=sib, device_id_type=MESH_ID)

    def local(k, h):
        return pltpu.make_async_copy(ins[k].at[h], outs[k].at[h, me], lsem.at[2 * k + h])

    def start():
        for k in range(n):
            for h in range(2):
                local(k, h).start()
            for d in range(3):
                ici(k, d).start()

    def mid():
        for d in range(3):
            for k in range(n):
                landed(k, d).wait_recv()
                fwd(k, d, c).start()

    def end():
        for d in range(3):
            for k in range(n):
                fwd(k, d, 1 - c).wait_recv()
        for k in range(n):
            for d in range(3):
                ici(k, d).wait_send()
                fwd(k, d, c).wait_send()
            for h in range(2):
                local(k, h).wait()

    return start, mid, end


def _gather_weights(shards):
    n = len(shards)

    def body(*refs):
        for step in _gather_steps(refs[:n], refs[n:2 * n], *refs[2 * n:]):
            step()

    return pl.pallas_call(
        body,
        name="gather_weights",
        in_specs=[_ANY] * n,
        out_specs=[_ANY] * n,
        out_shape=_gathered_shapes(shards),
        scratch_shapes=_gather_sems(n),
    )(*shards)


def _pair_exchange(name, arrs, arrs_core1=None):
    n = len(arrs)
    two = arrs_core1 is not None

    def body(*refs):
        ins = refs[:n]
        ins1 = refs[n:2 * n] if two else ins
        outs = refs[-2 - n:-2]
        ssem, rsem = refs[-2:]
        x, y, c = _mesh_pos()

        def copy(src, k):
            return pltpu.make_async_remote_copy(src[k], outs[k], ssem.at[k], rsem.at[k],
                                                device_id=(x, y, 1 - c), device_id_type=MESH_ID)

        if two:
            @pl.when(c == 0)
            def _():
                for k in range(n):
                    copy(ins, k).start()

            @pl.when(c == 1)
            def _():
                for k in range(n):
                    copy(ins1, k).start()
        else:
            for k in range(n):
                copy(ins, k).start()
        for k in range(n):
            copy(ins, k).wait()

    return pl.pallas_call(
        body,
        name=name,
        in_specs=[_ANY] * (2 * n if two else n),
        out_specs=[_ANY] * n,
        out_shape=[jax.ShapeDtypeStruct(a.shape, a.dtype) for a in arrs],
        scratch_shapes=[pltpu.SemaphoreType.DMA((n,)), pltpu.SemaphoreType.DMA((n,))],
    )(*arrs, *(arrs_core1 or []))


def _chip_exchange(name, arrs, broadcast):
    n = len(arrs)

    def body(*refs):
        ins, outs = refs[:n], refs[n:2 * n]
        lsem, ssem, rsem = refs[2 * n:]
        x, y, c = _mesh_pos()
        me = 2 * x + y
        copies = []
        for k in range(n):
            copies.append(pltpu.make_async_copy(ins[k] if broadcast else ins[k].at[me], outs[k].at[me], lsem.at[k]))
        for k in range(n):
            for d, (j, tx, ty) in enumerate(_other_chips(x, y)):
                copies.append(pltpu.make_async_remote_copy(
                    ins[k] if broadcast else ins[k].at[j], outs[k].at[me], ssem.at[3 * k + d], rsem.at[3 * k + d],
                    device_id=(tx, ty, c), device_id_type=MESH_ID))
        for cp in copies:
            cp.start()
        for cp in copies:
            cp.wait()

    return pl.pallas_call(
        body,
        name=name,
        in_specs=[_ANY] * n,
        out_specs=[_ANY] * n,
        out_shape=[jax.ShapeDtypeStruct(((N_CHIPS,) + a.shape) if broadcast else a.shape, a.dtype) for a in arrs],
        scratch_shapes=[pltpu.SemaphoreType.DMA((n,)), pltpu.SemaphoreType.DMA((3 * n,)),
                        pltpu.SemaphoreType.DMA((3 * n,))],
    )(*arrs)


SMALL = ("norm_g", "lam_re", "lam_im", "b_re", "b_im", "c_re", "c_im", "d_skip", "log_dt", "b_glu", "ln_g", "ln_b",
         "w_s", "b_s", "w_pool", "pool_scale", "final_g")
BIG = ("w_in", "w_glu", "w_out")
WEIGHTS = ("norm_g", "w_in", "lam_re", "lam_im", "b_re", "b_im", "c_re", "c_im", "d_skip", "log_dt", "w_glu", "b_glu",
           "ln_g", "ln_b", "w_s", "b_s", "w_pool", "pool_scale", "w_out", "final_g")
PACK_UNIT = 8 * 128
PACK_ROWS = 1024


def _pack(arrs):
    parts, total = [], 0
    for a in arrs:
        f = a.reshape(-1).astype(F32)
        pad = (-f.shape[0]) % PACK_UNIT
        parts.append(jnp.pad(f, (0, pad)) if pad else f)
        total += f.shape[0] + pad
    tail = (-total) % (PACK_ROWS * 128)
    if tail:
        parts.append(jnp.zeros((tail,), F32))
    return jnp.concatenate(parts).reshape(-1, 128)


def _unpack(buf, like):
    flat = buf.reshape(-1)
    out, off = [], 0
    for a in like:
        n = math.prod(a.shape)
        out.append(flat[off:off + n].reshape(a.shape))
        off += n + ((-n) % PACK_UNIT)
    return out


def _layer_params(l, wt, g_glu):
    a_re, a_im, bb_re, bb_im = _s5_prep(wt["lam_re"][l], wt["lam_im"][l], wt["b_re"][l], wt["b_im"][l], wt["log_dt"][l])
    b4re, b4im = _block_diag_in(bb_re), _block_diag_in(bb_im)
    c4re, c4im = _block_diag_out(wt["c_re"][l]), _block_diag_out(-wt["c_im"][l])
    tr = lambda m: jnp.swapaxes(m, 1, 2).astype(BF16)
    causal = jnp.tril(jnp.ones((CHUNK, CHUNK), dtype=bool))
    ws = jnp.where(causal[None], wt["w_s"][l], 0.0)
    wglu = g_glu[l].reshape(S5_W, S5_W)
    return dict(
        b4re=b4re.astype(BF16), b4im=b4im.astype(BF16), c4re=c4re.astype(BF16), c4im=c4im.astype(BF16),
        b4re_t=tr(b4re), b4im_t=tr(b4im), c4re_t=tr(c4re), c4im_t=tr(c4im),
        dvec=wt["d_skip"][l].reshape(1, S5_W), wglu=wglu, wglu_t=wglu.T, bglu=wt["b_glu"][l].reshape(1, S5_W),
        coef_f=_scan_coefs(a_re, a_im, False), coef_r=_scan_coefs(a_re, a_im, True),
        ws=ws.astype(BF16), ws_t=tr(ws),
        bsf=jnp.broadcast_to(wt["b_s"][l][:, None, :], (SGU_HEADS, CHUNK, CHUNK)).transpose(2, 0, 1).reshape(CHUNK, SGU_W),
        lng=wt["ln_g"][l].reshape(1, SGU_W), lnb=wt["ln_b"][l].reshape(1, SGU_W),
        wp=wt["w_pool"][l].astype(BF16), wp_t=tr(wt["w_pool"][l]), scale=wt["pool_scale"][l].reshape(1, POOL_W),
        norm_g=wt["norm_g"][l].reshape(1, D_MODEL),
    )


def _local_step(x0, tgt, wt, g_in0, rest, rest_gathered):
    xs, saved, params = [x0], [], []
    for l in range(DEPTH):
        norm_g = wt["norm_g"][l].reshape(1, D_MODEL)
        if l == 0 and not rest_gathered:
            z, h, g_in1, g_glu, g_out = _inproj(xs[-1], norm_g, g_in0, list(rest))
        elif l == 0:
            g_in1, g_glu, g_out = rest
            z, h = _inproj(xs[-1], norm_g, g_in0)
        else:
            z, h = _inproj(xs[-1], norm_g, g_in1)
        g_in = (g_in0, g_in1)
        p = _layer_params(l, wt, g_glu)
        params.append(p)
        ya, yraw, sre, sim = _s5_fwd(z, p)
        yb = _sgu_fwd(z, p["ws"], p["bsf"], p["lng"], p["lnb"])
        yc = _pool_fwd(z, p["wp"], p["scale"])
        xn, y = _outproj(ya, yb, yc, g_out[l].reshape(D_MODEL, D_MODEL), xs[-1])
        xs.append(xn)
        saved.append((z, h, yraw, sre, sim, y))

    dx, loss, dfg = _loss_head(xs[-1], wt["final_g"].reshape(1, D_MODEL), tgt)

    gr = {k: [None] * DEPTH for k in WEIGHTS if k != "final_g"}
    for l in reversed(range(DEPTH)):
        p = params[l]
        z, h, yraw, sre, sim, y = saved[l]
        w_out = g_out[l].reshape(D_MODEL, D_MODEL)
        dy = _outproj_bwd_dy(dx, w_out)
        gr["w_out"][l] = _outproj_bwd_dw(y, dx)
        dxa, dga, dbre, dbim, dcre, dcim, dd, dwg, dbg, da = _s5_bwd(dy, z, yraw, sre, sim, p)
        du, dv, dgb, dws, dbsf, dlng, dlnb = _sgu_bwd(dy, z, p["ws"], p["ws_t"], p["bsf"], p["lng"], p["lnb"])
        dxc, dgc, dwp, dsc = _pool_bwd(dy, z, p["wp"], p["wp_t"], p["scale"])
        dz = jnp.concatenate([dxa, du, dv, dxc, dga, dgb, dgc], axis=1)
        gr["w_in"][l] = _inproj_bwd_dw(h, dz)
        dx, dng = _inproj_bwd_dx(dz, g_in[l], xs[l], p["norm_g"], dx)

        raw = (wt["lam_re"][l], wt["lam_im"][l], wt["b_re"][l], wt["b_im"][l], wt["log_dt"][l])
        _, vjp = jax.vjp(_s5_prep, *raw)
        da = jnp.sum(da, axis=1)
        cot = (da[0].reshape(S5_GROUPS, S5_STATE), da[1].reshape(S5_GROUPS, S5_STATE),
               _block_diag_in_grad(dbre), _block_diag_in_grad(dbim))
        gr["lam_re"][l], gr["lam_im"][l], gr["b_re"][l], gr["b_im"][l], gr["log_dt"][l] = vjp(cot)
        gr["c_re"][l] = _block_diag_out_grad(dcre)
        gr["c_im"][l] = -_block_diag_out_grad(dcim)
        gr["d_skip"][l] = dd.reshape(S5_GROUPS, S5_CH)
        gr["w_glu"][l] = dwg
        gr["b_glu"][l] = dbg.reshape(S5_W)
        causal = jnp.tril(jnp.ones((CHUNK, CHUNK), dtype=bool))
        gr["w_s"][l] = jnp.where(causal[None], dws, 0.0)
        gr["b_s"][l] = dbsf.reshape(CHUNK, SGU_HEADS, CHUNK).sum(-1).T
        gr["ln_g"][l] = dlng.reshape(SGU_W)
        gr["ln_b"][l] = dlnb.reshape(SGU_W)
        gr["w_pool"][l] = dwp
        gr["pool_scale"][l] = dsc.reshape(POOL_W)
        gr["norm_g"][l] = dng.reshape(D_MODEL)

    grads = {k: (v if k in BIG else jnp.stack(v)) for k, v in gr.items()}
    grads["final_g"] = dfg.reshape(D_MODEL)
    return loss, dx, grads


def kernel(x, norm_g, w_in, lam_re, lam_im, b_re, b_im, c_re, c_im, d_skip, log_dt, w_glu, b_glu, ln_g, ln_b, w_s, b_s, w_pool, pool_scale, w_out, final_g, loss_target, m_norm_g, m_w_in, m_lam_re, m_lam_im, m_b_re, m_b_im, m_c_re, m_c_im, m_d_skip, m_log_dt, m_w_glu, m_b_glu, m_ln_g, m_ln_b, m_w_s, m_b_s, m_w_pool, m_pool_scale, m_w_out, m_final_g, v_norm_g, v_w_in, v_lam_re, v_lam_im, v_b_re, v_b_im, v_c_re, v_c_im, v_d_skip, v_log_dt, v_w_glu, v_b_glu, v_ln_g, v_ln_b, v_w_s, v_b_s, v_w_pool, v_pool_scale, v_w_out, v_final_g):
    wt = dict(norm_g=norm_g, w_in=w_in, lam_re=lam_re, lam_im=lam_im, b_re=b_re, b_im=b_im, c_re=c_re, c_im=c_im,
              d_skip=d_skip, log_dt=log_dt, w_glu=w_glu, b_glu=b_glu, ln_g=ln_g, ln_b=ln_b, w_s=w_s, b_s=b_s,
              w_pool=w_pool, pool_scale=pool_scale, w_out=w_out, final_g=final_g)
    mom = dict(norm_g=m_norm_g, w_in=m_w_in, lam_re=m_lam_re, lam_im=m_lam_im, b_re=m_b_re, b_im=m_b_im, c_re=m_c_re,
               c_im=m_c_im, d_skip=m_d_skip, log_dt=m_log_dt, w_glu=m_w_glu, b_glu=m_b_glu, ln_g=m_ln_g, ln_b=m_ln_b,
               w_s=m_w_s, b_s=m_b_s, w_pool=m_w_pool, pool_scale=m_pool_scale, w_out=m_w_out, final_g=m_final_g)
    vel = dict(norm_g=v_norm_g, w_in=v_w_in, lam_re=v_lam_re, lam_im=v_lam_im, b_re=v_b_re, b_im=v_b_im, c_re=v_c_re,
               c_im=v_c_im, d_skip=v_d_skip, log_dt=v_log_dt, w_glu=v_w_glu, b_glu=v_b_glu, ln_g=v_ln_g, ln_b=v_ln_b,
               w_s=v_w_s, b_s=v_b_s, w_pool=v_w_pool, pool_scale=v_pool_scale, w_out=v_w_out, final_g=v_final_g)
    T = x.shape[1]
    cidx = lax.axis_index("c").astype(jnp.int32).reshape(1)

    w_in_b = w_in.astype(BF16)
    (g_in0,) = _gather_weights([w_in_b[0].reshape(2, HALF_D, SHARD_COLS)])
    rest = (w_in_b[1].reshape(2, HALF_D, SHARD_COLS), w_glu.astype(BF16), w_out.astype(BF16))
    loss, grad_x, grads = _local_step(x.reshape(T, D_MODEL), loss_target.reshape(T, D_MODEL), wt, g_in0, rest, False)

    shard = dict(w_in=(D_MODEL, SHARD_COLS), w_glu=(S5_W // N_CHIPS, S5_W), w_out=(D_MODEL // N_CHIPS, D_MODEL))
    part = [[grads[k][l].reshape(N_CHIPS * shard[k][0], shard[k][1]) for k in BIG] for l in range(DEPTH)]
    from_sib = _pair_exchange("grad_pair_sum_exchange", part[1], part[0])
    chip_sum = [_add_own_layer(p0, p1, r, cidx).reshape((N_CHIPS,) + shard[k])
                for k, p0, p1, r in zip(BIG, part[0], part[1], from_sib)]
    from_chips = _chip_exchange("grad_chip_exchange", chip_sum, False)
    mine = [_sum_chips(r) for r in from_chips]
    theirs = _pair_exchange("grad_pair_result_exchange", mine)

    out_g, out_d, out_m, out_v = {}, {}, {}, {}
    for k, a, b in zip(BIG, mine, theirs):
        shape = wt[k].shape
        rows, cols = a.shape
        flat = lambda t: t.reshape(2, rows, cols)
        g, d, m, v = _adamw_layers(flat(wt[k]), a, b, flat(mom[k]), flat(vel[k]), cidx)
        out_g[k], out_d[k], out_m[k], out_v[k] = (t.reshape(shape) for t in (g, d, m, v))

    packed = _pack([grads[k] for k in SMALL] + [loss[0, 0:1]])
    (sib_packed,) = _pair_exchange("small_pair_exchange", [packed])
    chip_packed = _add2(packed, sib_packed)
    (all_packed,) = _chip_exchange("small_chip_broadcast", [chip_packed], True)
    total = _sum_chips(all_packed)
    like = [wt[k] for k in SMALL]
    small_g = _unpack(total, like + [loss[0, 0:1]])
    loss_out = small_g[-1].reshape(())
    w_p, m_p, v_p = _pack(like), _pack([mom[k] for k in SMALL]), _pack([vel[k] for k in SMALL])
    d_p, mo_p, vo_p = _adamw(w_p, total, m_p, v_p)
    for k, g, d, m, v in zip(SMALL, small_g[:-1], _unpack(d_p, like), _unpack(mo_p, like), _unpack(vo_p, like)):
        out_g[k], out_d[k], out_m[k], out_v[k] = g, d, m, v

    return (loss_out, grad_x.reshape(x.shape), *[out_g[k] for k in WEIGHTS], *[out_d[k] for k in WEIGHTS],
            *[out_m[k] for k in WEIGHTS], *[out_v[k] for k in WEIGHTS])
```

```python
import functools
import math

import jax
import jax.numpy as jnp
from jax import lax
from jax.experimental import pallas as pl
from jax.experimental.pallas import tpu as pltpu

F32 = jnp.float32
BF16 = jnp.bfloat16

D_MODEL = 2048
DEPTH = 2
S5_W = 512
SGU_W = 1024
POOL_W = 512
IN_COLS = 5120
N_CHIPS = 4
SHARD_COLS = IN_COLS // N_CHIPS
S5_GROUPS = 32
S5_STATE = 64
S5_CH = 16
STATE_W = S5_GROUPS * S5_STATE
SUPER = 4
CHUNK = 128
SGU_HEADS = 8
POOL_WINDOWS = (2, 4, 8, 16)
POOL_HALO = 16
RMS_EPS = 1e-6
LN_EPS = 1e-5
SCAN_COLS = 512

ADAM_LR = 0.001
ADAM_B1 = 0.9
ADAM_B2 = 0.999
ADAM_EPS = 1e-08
ADAM_WD = 0.01
ADAM_STEP = 10

VMEM_LIMIT = 56 * 1024 * 1024
MESH_ID = pl.DeviceIdType.MESH

_GELU_K0 = math.sqrt(2.0 / math.pi)
_GELU_K1 = 0.044715


def _cparams(n_axes):
    return pltpu.CompilerParams(dimension_semantics=("arbitrary",) * n_axes, vmem_limit_bytes=VMEM_LIMIT)


def _gelu(x):
    t = jnp.tanh(_GELU_K0 * (x + _GELU_K1 * (x * x * x)))
    return 0.5 * x * (1.0 + t)


def _gelu_and_grad(x):
    x2 = x * x
    t = jnp.tanh(_GELU_K0 * (x + _GELU_K1 * (x * x2)))
    g = 0.5 * x * (1.0 + t)
    dg = 0.5 * (1.0 + t) + 0.5 * x * (1.0 - t * t) * (_GELU_K0 * (1.0 + 3.0 * _GELU_K1 * x2))
    return g, dg


def _silu_and_grad(x):
    s = jax.nn.sigmoid(x)
    return x * s, s * (1.0 + x * (1.0 - s))


def _dot(a, b):
    return jnp.dot(a.astype(BF16), b.astype(BF16), preferred_element_type=F32)


def _dot_nt(a, b):
    return lax.dot_general(a.astype(BF16), b.astype(BF16), (((1,), (1,)), ((), ())), preferred_element_type=F32)


def _dot_tn(a, b):
    return lax.dot_general(a.astype(BF16), b.astype(BF16), (((0,), (0,)), ((), ())), preferred_element_type=F32)


def _full(shape):
    nd = len(shape)
    return pl.BlockSpec(shape, lambda *_: (0,) * nd)


class _Rider:
    def __init__(self, arrs, out_shapes, sems, steps):
        self.arrs, self.out_shapes, self.sems, self.steps = list(arrs), list(out_shapes), list(sems), steps


def _ride(body, n_in, n_out, rider, first, last):
    if rider is None:
        return body
    ri, ro, ns = len(rider.arrs), len(rider.out_shapes), len(rider.sems)

    def wrapped(*refs):
        o0 = n_in + ri
        start, end = rider.steps(refs[n_in:o0], refs[o0 + n_out:o0 + n_out + ro], *refs[len(refs) - ns:])
        pl.when(first())(start)
        body(*refs[:n_in], *refs[o0:o0 + n_out], *refs[o0 + n_out + ro:len(refs) - ns])
        pl.when(last())(end)

    return wrapped


def _rider_specs(rider):
    if rider is None:
        return [], [], [], [], []
    anyspec = pl.BlockSpec(memory_space=pl.ANY)
    return ([anyspec] * len(rider.arrs), [anyspec] * len(rider.out_shapes), rider.out_shapes, rider.sems, rider.arrs)


HALF_D = D_MODEL // 2


def _inproj(x, g, w, riders=None):
    T = x.shape[0]
    tm = min(512, T)
    ni = T // tm
    n = len(riders) if riders else 0

    def body(*refs):
        x_ref, g_ref, w_ref = refs[:3]
        rin = refs[3:3 + n]
        z_ref, h_ref = refs[3 + n:5 + n]
        rout = refs[5 + n:5 + 2 * n]
        hs_ref = refs[5 + 2 * n]
        i, j = pl.program_id(0), pl.program_id(1)
        if n:
            start, mid, end = _gather_steps(rin, rout, *refs[6 + 2 * n:])
            pl.when((i == 0) & (j == 0))(start)
            pl.when((i == ni // 2) & (j == 0))(mid)

        @pl.when(j == 0)
        def _():
            xv = x_ref[...]
            r = lax.rsqrt(jnp.mean(xv * xv, axis=-1, keepdims=True) + RMS_EPS)
            hv = (xv * r * g_ref[...]).astype(BF16)
            hs_ref[...] = hv
            h_ref[...] = hv

        z_ref[...] = (jnp.dot(hs_ref[:, 0:HALF_D], w_ref[0], preferred_element_type=F32)
                      + jnp.dot(hs_ref[:, HALF_D:D_MODEL], w_ref[1], preferred_element_type=F32))
        if n:
            pl.when((i == ni - 1) & (j == N_CHIPS - 1))(end)

    return pl.pallas_call(
        body,
        name="inproj_gather" if n else "inproj",
        grid=(ni, N_CHIPS),
        in_specs=[
            pl.BlockSpec((tm, D_MODEL), lambda i, j: (i, 0)),
            pl.BlockSpec((1, D_MODEL), lambda i, j: (0, 0)),
            pl.BlockSpec((2, None, HALF_D, SHARD_COLS), lambda i, j: (0, j, 0, 0)),
        ] + [_ANY] * n,
        out_specs=[
            pl.BlockSpec((tm, SHARD_COLS), lambda i, j: (i, j)),
            pl.BlockSpec((tm, D_MODEL), lambda i, j: (i, 0)),
        ] + [_ANY] * n,
        out_shape=[jax.ShapeDtypeStruct((T, IN_COLS), F32), jax.ShapeDtypeStruct((T, D_MODEL), BF16)]
        + _gathered_shapes(riders or []),
        scratch_shapes=[pltpu.VMEM((tm, D_MODEL), BF16)] + (_gather_sems(n) if n else []),
        compiler_params=_cparams(2),
    )(x, g, w, *(riders or []))


def _outproj(ya, yb, yc, w, x):
    T = x.shape[0]
    tm = min(512, T)
    tn = 1024

    def body(ya_ref, yb_ref, yc_ref, w_ref, x_ref, o_ref, y_ref):
        acc = jnp.dot(ya_ref[...], w_ref[0:S5_W, :], preferred_element_type=F32)
        acc += jnp.dot(yb_ref[...], w_ref[S5_W:S5_W + SGU_W, :], preferred_element_type=F32)
        acc += jnp.dot(yc_ref[...], w_ref[S5_W + SGU_W:D_MODEL, :], preferred_element_type=F32)
        o_ref[...] = x_ref[...] + acc

        @pl.when(pl.program_id(1) == 0)
        def _():
            y_ref[:, 0:S5_W] = ya_ref[...]
            y_ref[:, S5_W:S5_W + SGU_W] = yb_ref[...]
            y_ref[:, S5_W + SGU_W:D_MODEL] = yc_ref[...]

    return pl.pallas_call(
        body,
        name="outproj",
        grid=(T // tm, D_MODEL // tn),
        in_specs=[
            pl.BlockSpec((tm, S5_W), lambda i, j: (i, 0)),
            pl.BlockSpec((tm, SGU_W), lambda i, j: (i, 0)),
            pl.BlockSpec((tm, POOL_W), lambda i, j: (i, 0)),
            pl.BlockSpec((D_MODEL, tn), lambda i, j: (0, j)),
            pl.BlockSpec((tm, tn), lambda i, j: (i, j)),
        ],
        out_specs=[
            pl.BlockSpec((tm, tn), lambda i, j: (i, j)),
            pl.BlockSpec((tm, D_MODEL), lambda i, j: (i, 0)),
        ],
        out_shape=[jax.ShapeDtypeStruct((T, D_MODEL), F32), jax.ShapeDtypeStruct((T, D_MODEL), BF16)],
        compiler_params=_cparams(2),
    )(ya, yb, yc, w, x)


def _outproj_bwd_dy(dxo, w):
    T = dxo.shape[0]
    tm = min(512, T)
    tn = 1024

    def body(d_ref, w_ref, o_ref, ds_ref):
        @pl.when(pl.program_id(1) == 0)
        def _():
            ds_ref[...] = d_ref[...].astype(BF16)

        o_ref[...] = lax.dot_general(ds_ref[...], w_ref[...], (((1,), (1,)), ((), ())), preferred_element_type=F32)

    return pl.pallas_call(
        body,
        name="outproj_bwd_dy",
        grid=(T // tm, D_MODEL // tn),
        in_specs=[
            pl.BlockSpec((tm, D_MODEL), lambda i, j: (i, 0)),
            pl.BlockSpec((tn, D_MODEL), lambda i, j: (j, 0)),
        ],
        out_specs=pl.BlockSpec((tm, tn), lambda i, j: (i, j)),
        out_shape=jax.ShapeDtypeStruct((T, D_MODEL), F32),
        scratch_shapes=[pltpu.VMEM((tm, D_MODEL), BF16)],
        compiler_params=_cparams(2),
    )(dxo, w)


def _outproj_bwd_dw(y, dxo):
    T = y.shape[0]
    tm = min(512, T)
    tr = 1024

    def body(y_ref, d_ref, o_ref):
        @pl.when(pl.program_id(1) == 0)
        def _():
            o_ref[...] = jnp.zeros_like(o_ref)

        o_ref[...] += _dot_tn(y_ref[...], d_ref[...])

    return pl.pallas_call(
        body,
        name="outproj_bwd_dw",
        grid=(D_MODEL // tr, T // tm),
        in_specs=[
            pl.BlockSpec((tm, tr), lambda p, t: (t, p)),
            pl.BlockSpec((tm, D_MODEL), lambda p, t: (t, 0)),
        ],
        out_specs=pl.BlockSpec((tr, D_MODEL), lambda p, t: (p, 0)),
        out_shape=jax.ShapeDtypeStruct((D_MODEL, D_MODEL), F32),
        compiler_params=_cparams(2),
    )(y, dxo)


def _inproj_bwd_dw(h, dz):
    T = h.shape[0]
    tm = min(512, T)

    def body(h_ref, dz_ref, o_ref):
        @pl.when(pl.program_id(1) == 0)
        def _():
            o_ref[...] = jnp.zeros_like(o_ref)

        o_ref[...] += _dot_tn(h_ref[...], dz_ref[...])

    return pl.pallas_call(
        body,
        name="inproj_bwd_dw",
        grid=(N_CHIPS, T // tm),
        in_specs=[
            pl.BlockSpec((tm, D_MODEL), lambda j, t: (t, 0)),
            pl.BlockSpec((tm, SHARD_COLS), lambda j, t: (t, j)),
        ],
        out_specs=pl.BlockSpec((None, D_MODEL, SHARD_COLS), lambda j, t: (j, 0, 0)),
        out_shape=jax.ShapeDtypeStruct((N_CHIPS, D_MODEL, SHARD_COLS), F32),
        compiler_params=_cparams(2),
    )(h, dz)


def _dx_tile(T):
    return min(512, max(T // 2, 8))


def _inproj_bwd_dx(dz, w4, x, g, dxo, rider=None, tiles=None, prev=None):
    T = x.shape[0]
    tm = _dx_tile(T)
    t0, ni = tiles if tiles else (0, T // tm)
    nk = N_CHIPS
    nt = (((1,), (1,)), ((), ()))
    n_in = 5 if prev is None else 6

    def body(dz_ref, w_ref, x_ref, g_ref, dxo_ref, *rest):
        dx_ref, dg_ref, acc_ref = rest[-3:]
        i, j = pl.program_id(0), pl.program_id(1)
        lo = lax.dot_general(dz_ref[...], w_ref[0], nt, preferred_element_type=F32)
        hi = lax.dot_general(dz_ref[...], w_ref[1], nt, preferred_element_type=F32)

        @pl.when(j == 0)
        def _():
            acc_ref[:, 0:HALF_D] = lo
            acc_ref[:, HALF_D:D_MODEL] = hi

        @pl.when(j > 0)
        def _():
            acc_ref[:, 0:HALF_D] += lo
            acc_ref[:, HALF_D:D_MODEL] += hi

        @pl.when(j == nk - 1)
        def _():
            @pl.when(i == 0)
            def _():
                dg_ref[...] = jnp.zeros_like(dg_ref)

            rc = min(128, tm)
            for c in range(tm // rc):
                rows = slice(c * rc, (c + 1) * rc)
                dh = acc_ref[rows, :]
                xv = x_ref[rows, :]
                r = lax.rsqrt(jnp.mean(xv * xv, axis=-1, keepdims=True) + RMS_EPS)
                xh = xv * r
                w = dh * g_ref[...]
                dx_ref[rows, :] = dxo_ref[rows, :] + r * (w - xh * jnp.mean(w * xh, axis=-1, keepdims=True))
                dg_ref[...] += jnp.sum(dh * xh, axis=0, keepdims=True)

    r_in, r_out, r_shapes, r_sems, r_args = _rider_specs(rider)
    return pl.pallas_call(
        _ride(body, n_in, 2, rider, lambda: (pl.program_id(0) == 0) & (pl.program_id(1) == 0),
              lambda: (pl.program_id(0) == ni - 1) & (pl.program_id(1) == nk - 1)),
        name="inproj_bwd_dx" + ("" if rider is None else "_ride") + ("" if prev is None else "_rest"),
        grid=(ni, nk),
        in_specs=[
            pl.BlockSpec((tm, SHARD_COLS), lambda i, j: (i + t0, j)),
            pl.BlockSpec((2, None, HALF_D, SHARD_COLS), lambda i, j: (0, j, 0, 0)),
            pl.BlockSpec((tm, D_MODEL), lambda i, j: (i + t0, 0)),
            pl.BlockSpec((1, D_MODEL), lambda i, j: (0, 0)),
            pl.BlockSpec((tm, D_MODEL), lambda i, j: (i + t0, 0)),
        ] + ([] if prev is None else [pl.BlockSpec(memory_space=pl.ANY)]) + r_in,
        out_specs=[
            pl.BlockSpec((tm, D_MODEL), lambda i, j: (i + t0, 0)),
            pl.BlockSpec((1, D_MODEL), lambda i, j: (0, 0)),
        ] + r_out,
        out_shape=[jax.ShapeDtypeStruct((T, D_MODEL), F32), jax.ShapeDtypeStruct((1, D_MODEL), F32)] + r_shapes,
        scratch_shapes=[pltpu.VMEM((tm, D_MODEL), F32)] + r_sems,
        input_output_aliases={} if prev is None else {5: 0},
        compiler_params=_cparams(2),
    )(dz, w4, x, g, dxo, *([] if prev is None else [prev]), *r_args)


def _loss_head(x, g, tgt):
    T = x.shape[0]
    tm = min(512, T)

    def body(x_ref, g_ref, t_ref, dx_ref, l_ref, dg_ref):
        i = pl.program_id(0)
        xv = x_ref[...]
        r = lax.rsqrt(jnp.mean(xv * xv, axis=-1, keepdims=True) + RMS_EPS)
        xh = xv * r
        err = xh * g_ref[...] - t_ref[...]
        lpart = 0.5 * jnp.sum(jnp.mean(err * err, axis=-1, keepdims=True), axis=0, keepdims=True)
        dout = err * (1.0 / D_MODEL)
        w = dout * g_ref[...]
        dx_ref[...] = r * (w - xh * jnp.mean(w * xh, axis=-1, keepdims=True))
        gpart = jnp.sum(dout * xh, axis=0, keepdims=True)

        @pl.when(i == 0)
        def _():
            l_ref[...] = jnp.broadcast_to(lpart, l_ref.shape)
            dg_ref[...] = gpart

        @pl.when(i > 0)
        def _():
            l_ref[...] += jnp.broadcast_to(lpart, l_ref.shape)
            dg_ref[...] += gpart

    return pl.pallas_call(
        body,
        name="loss_head",
        grid=(T // tm,),
        in_specs=[
            pl.BlockSpec((tm, D_MODEL), lambda i: (i, 0)),
            pl.BlockSpec((1, D_MODEL), lambda i: (0, 0)),
            pl.BlockSpec((tm, D_MODEL), lambda i: (i, 0)),
        ],
        out_specs=[
            pl.BlockSpec((tm, D_MODEL), lambda i: (i, 0)),
            pl.BlockSpec((1, 128), lambda i: (0, 0)),
            pl.BlockSpec((1, D_MODEL), lambda i: (0, 0)),
        ],
        out_shape=[
            jax.ShapeDtypeStruct((T, D_MODEL), F32),
            jax.ShapeDtypeStruct((1, 128), F32),
            jax.ShapeDtypeStruct((1, D_MODEL), F32),
        ],
        compiler_params=_cparams(1),
    )(x, g, tgt)


def _s5_prep(lam_re, lam_im, b_re, b_im, log_dt):
    lam = lax.complex(lam_re, lam_im)
    dt = jnp.exp(log_dt)[:, None]
    a = jnp.exp(lam * dt)
    bbar = ((a - 1.0) / lam)[..., None] * lax.complex(b_re, b_im)
    return jnp.real(a), jnp.imag(a), jnp.real(bbar), jnp.imag(bbar)


def _block_diag_in(m):
    m4 = m.reshape(SUPER, 8, S5_STATE, S5_CH)
    eye = jnp.eye(8, dtype=m.dtype)
    out = jnp.einsum("jgph,gk->jghkp", m4, eye)
    return out.reshape(SUPER, 8 * S5_CH, 8 * S5_STATE)


def _block_diag_in_grad(d):
    d6 = d.reshape(SUPER, 8, S5_CH, 8, S5_STATE)
    diag = jnp.einsum("jghgp->jgph", d6)
    return diag.reshape(S5_GROUPS, S5_STATE, S5_CH)


def _block_diag_out(m):
    m4 = m.reshape(SUPER, 8, S5_CH, S5_STATE)
    eye = jnp.eye(8, dtype=m.dtype)
    out = jnp.einsum("jghp,gk->jgpkh", m4, eye)
    return out.reshape(SUPER, 8 * S5_STATE, 8 * S5_CH)


def _block_diag_out_grad(d):
    d6 = d.reshape(SUPER, 8, S5_STATE, 8, S5_CH)
    diag = jnp.einsum("jgpgh->jghp", d6)
    return diag.reshape(S5_GROUPS, S5_CH, S5_STATE)


def _scan_coefs(a_re, a_im, reverse):
    a = lax.complex(a_re.reshape(-1), a_im.reshape(-1))
    if reverse:
        a = jnp.conj(a)
    pw = [a]
    for _ in range(7):
        pw.append(pw[-1] * a)
    rows = jnp.arange(8)

    def masked(k):
        m = (rows + k <= 7) if reverse else (rows >= k)
        return jnp.where(m[:, None], pw[k - 1][None, :], 0.0)

    a1, a2, a4 = masked(1), masked(2), masked(4)
    carry = jnp.stack([pw[7 - r] for r in range(8)]) if reverse else jnp.stack(pw)
    parts = []
    for c in (a1, a2, a4, carry):
        parts += [jnp.real(c), jnp.imag(c)]
    return jnp.stack(parts).astype(F32)


def _scan_block(r, im, coef_ref, cs, reverse):
    for k, idx in ((1, 0), (2, 2), (4, 4)):
        ar = coef_ref[idx, :, cs]
        ai = coef_ref[idx + 1, :, cs]
        sh = 8 - k if reverse else k
        rr = pltpu.roll(r, sh, 0)
        ri = pltpu.roll(im, sh, 0)
        r, im = r + ar * rr - ai * ri, im + ar * ri + ai * rr
    return r, im


def _s5_fwd(z, p):
    T = z.shape[0]
    tm = min(256, T)
    nblk = tm // 8
    W = STATE_W

    def body(xa_ref, ga_ref, bre_ref, bim_ref, cre_ref, cim_ref, dv_ref, wg_ref, bg_ref, coef_ref,
             ya_ref, yraw_ref, sre_ref, sim_ref, wre, wim):
        @pl.when(pl.program_id(0) == 0)
        def _():
            wre[0:8, :] = jnp.zeros((8, W), F32)
            wim[0:8, :] = jnp.zeros((8, W), F32)

        xa = xa_ref[...]
        xab = xa.astype(BF16)
        for j in range(SUPER):
            xj = xab[:, j * 128:(j + 1) * 128]
            wre[8:8 + tm, j * 512:(j + 1) * 512] = jnp.dot(xj, bre_ref[j], preferred_element_type=F32)
            wim[8:8 + tm, j * 512:(j + 1) * 512] = jnp.dot(xj, bim_ref[j], preferred_element_type=F32)

        def blk(b, carry):
            base = pl.multiple_of(8 + b * 8, 8)
            for cc in range(W // SCAN_COLS):
                cs = pl.ds(cc * SCAN_COLS, SCAN_COLS)
                r, im = _scan_block(wre[pl.ds(base, 8), cs], wim[pl.ds(base, 8), cs], coef_ref, cs, False)
                cr = wre[pl.ds(base - 1, 1), cs]
                ci = wim[pl.ds(base - 1, 1), cs]
                pr = coef_ref[6, :, cs]
                pi = coef_ref[7, :, cs]
                wre[pl.ds(base, 8), cs] = r + pr * cr - pi * ci
                wim[pl.ds(base, 8), cs] = im + pr * ci + pi * cr
            return carry

        lax.fori_loop(0, nblk, blk, 0)
        wre[0:8, :] = wre[tm:tm + 8, :]
        wim[0:8, :] = wim[tm:tm + 8, :]
        sre_ref[...] = wre[8:8 + tm, :]
        sim_ref[...] = wim[8:8 + tm, :]

        for j in range(SUPER):
            yr = jnp.dot(wre[8:8 + tm, j * 512:(j + 1) * 512].astype(BF16), cre_ref[j], preferred_element_type=F32)
            yr += jnp.dot(wim[8:8 + tm, j * 512:(j + 1) * 512].astype(BF16), cim_ref[j], preferred_element_type=F32)
            yraw_ref[:, j * 128:(j + 1) * 128] = yr
        yraw = yraw_ref[...] + dv_ref[...] * xa
        yraw_ref[...] = yraw
        yg = _gelu(yraw)
        q = jnp.dot(yg.astype(BF16), wg_ref[...], preferred_element_type=F32) + bg_ref[...]
        sga, _ = _silu_and_grad(ga_ref[...])
        ya_ref[...] = (yg * jax.nn.sigmoid(q) * sga).astype(BF16)

    return pl.pallas_call(
        body,
        name="s5_fwd",
        grid=(T // tm,),
        in_specs=[
            pl.BlockSpec((tm, S5_W), lambda i: (i, 0)),
            pl.BlockSpec((tm, S5_W), lambda i: (i, 6)),
            _full((SUPER, 128, 512)), _full((SUPER, 128, 512)),
            _full((SUPER, 512, 128)), _full((SUPER, 512, 128)),
            _full((1, S5_W)), _full((S5_W, S5_W)), _full((1, S5_W)),
            _full((8, 8, W)),
        ],
        out_specs=[
            pl.BlockSpec((tm, S5_W), lambda i: (i, 0)),
            pl.BlockSpec((tm, S5_W), lambda i: (i, 0)),
            pl.BlockSpec((tm, W), lambda i: (i, 0)),
            pl.BlockSpec((tm, W), lambda i: (i, 0)),
        ],
        out_shape=[
            jax.ShapeDtypeStruct((T, S5_W), BF16),
            jax.ShapeDtypeStruct((T, S5_W), F32),
            jax.ShapeDtypeStruct((T, W), F32),
            jax.ShapeDtypeStruct((T, W), F32),
        ],
        scratch_shapes=[pltpu.VMEM((tm + 8, W), F32), pltpu.VMEM((tm + 8, W), F32)],
        compiler_params=_cparams(1),
    )(z, z, p["b4re"], p["b4im"], p["c4re"], p["c4im"], p["dvec"], p["wglu"], p["bglu"], p["coef_f"])


def _s5_bwd(dy, z, yraw, sre, sim, p, rider=None):
    T = z.shape[0]
    tm = min(256, T)
    nt = T // tm
    nblk = tm // 8
    W = STATE_W
    rev = lambda i: nt - 1 - i

    def body(dya_ref, xa_ref, ga_ref, yraw_ref, sre_ref, sim_ref, hre_ref, him_ref,
             bre_t_ref, bim_t_ref, cre_t_ref, cim_t_ref, dv_ref, wg_ref, wgt_ref, bg_ref, coef_ref,
             dxa_ref, dga_ref, dbre_ref, dbim_ref, dcre_ref, dcim_ref, dd_ref, dwg_ref, dbg_ref, da_ref,
             wre, wim, dyr_ref):
        i = pl.program_id(0)

        @pl.when(i == 0)
        def _():
            wre[tm:tm + 8, :] = jnp.zeros((8, W), F32)
            wim[tm:tm + 8, :] = jnp.zeros((8, W), F32)
            for ref in (dbre_ref, dbim_ref, dcre_ref, dcim_ref, dd_ref, dwg_ref, dbg_ref, da_ref):
                ref[...] = jnp.zeros_like(ref)

        xa = xa_ref[...]
        dya = dya_ref[...]
        yg, dgelu = _gelu_and_grad(yraw_ref[...])
        ygb = yg.astype(BF16)
        q = jnp.dot(ygb, wg_ref[...], preferred_element_type=F32) + bg_ref[...]
        sq = jax.nn.sigmoid(q)
        sga, dsga = _silu_and_grad(ga_ref[...])
        dga_ref[...] = (dya * (yg * sq) * dsga).astype(BF16)
        dya0 = dya * sga
        dq = dya0 * yg * sq * (1.0 - sq)
        dqb = dq.astype(BF16)
        dyg = dya0 * sq + jnp.dot(dqb, wgt_ref[...], preferred_element_type=F32)
        dwg_ref[...] += _dot_tn(ygb, dqb)
        dbg_ref[...] += jnp.sum(dq, axis=0, keepdims=True)
        dyraw = dyg * dgelu
        dd_ref[...] += jnp.sum(dyraw * xa, axis=0, keepdims=True)
        dyr_ref[...] = dyraw.astype(BF16)

        for j in range(SUPER):
            dj = dyr_ref[:, j * 128:(j + 1) * 128]
            wre[0:tm, j * 512:(j + 1) * 512] = jnp.dot(dj, cre_t_ref[j], preferred_element_type=F32)
            wim[0:tm, j * 512:(j + 1) * 512] = jnp.dot(dj, cim_t_ref[j], preferred_element_type=F32)

        row0 = lax.broadcasted_iota(jnp.int32, (8, SCAN_COLS), 0) == 0
        head_on = (i < nt - 1).astype(F32)

        def one_block(base, first):
            for cc in range(W // SCAN_COLS):
                cs = pl.ds(cc * SCAN_COLS, SCAN_COLS)
                r, im = _scan_block(wre[pl.ds(base, 8), cs], wim[pl.ds(base, 8), cs], coef_ref, cs, True)
                cr = wre[pl.ds(base + 8, 1), cs]
                ci = wim[pl.ds(base + 8, 1), cs]
                pr = coef_ref[6, :, cs]
                pi = coef_ref[7, :, cs]
                r, im = r + pr * cr - pi * ci, im + pr * ci + pi * cr
                wre[pl.ds(base, 8), cs] = r
                wim[pl.ds(base, 8), cs] = im
                if first:
                    pre = hre_ref[7:8, cs] * head_on
                    pim = him_ref[7:8, cs] * head_on
                else:
                    pre = sre_ref[pl.ds(base - 1, 1), cs]
                    pim = sim_ref[pl.ds(base - 1, 1), cs]
                spr = jnp.where(row0, pre, pltpu.roll(sre_ref[pl.ds(base, 8), cs], 1, 0))
                spi = jnp.where(row0, pim, pltpu.roll(sim_ref[pl.ds(base, 8), cs], 1, 0))
                da_ref[0, :, cs] += r * spr + im * spi
                da_ref[1, :, cs] += im * spr - r * spi

        def blk(b, carry):
            one_block(pl.multiple_of((nblk - 1 - b) * 8, 8), False)
            return carry

        lax.fori_loop(0, nblk - 1, blk, 0)
        one_block(0, True)
        wre[tm:tm + 8, :] = wre[0:8, :]
        wim[tm:tm + 8, :] = wim[0:8, :]

        xab = xa.astype(BF16)
        for j in range(SUPER):
            cols = slice(j * 512, (j + 1) * 512)
            gre = wre[0:tm, cols].astype(BF16)
            gim = wim[0:tm, cols].astype(BF16)
            xj = xab[:, j * 128:(j + 1) * 128]
            dj = dyr_ref[:, j * 128:(j + 1) * 128]
            dbre_ref[j] += _dot_tn(xj, gre)
            dbim_ref[j] += _dot_tn(xj, gim)
            dcre_ref[j] += _dot_tn(sre_ref[:, cols], dj)
            dcim_ref[j] += _dot_tn(sim_ref[:, cols], dj)
            dxj = jnp.dot(gre, bre_t_ref[j], preferred_element_type=F32)
            dxj += jnp.dot(gim, bim_t_ref[j], preferred_element_type=F32)
            dxj += dyraw[:, j * 128:(j + 1) * 128] * dv_ref[:, j * 128:(j + 1) * 128]
            dxa_ref[:, j * 128:(j + 1) * 128] = dxj.astype(BF16)

    acc = lambda shape: _full(shape)
    hb = tm // 8
    r_in, r_out, r_shapes, r_sems, r_args = _rider_specs(rider)
    return pl.pallas_call(
        _ride(body, 17, 10, rider, lambda: pl.program_id(0) == 0, lambda: pl.program_id(0) == nt - 1),
        name="s5_bwd" + ("" if rider is None else "_ride"),
        grid=(nt,),
        in_specs=[
            pl.BlockSpec((tm, S5_W), lambda i: (rev(i), 0)),
            pl.BlockSpec((tm, S5_W), lambda i: (rev(i), 0)),
            pl.BlockSpec((tm, S5_W), lambda i: (rev(i), 6)),
            pl.BlockSpec((tm, S5_W), lambda i: (rev(i), 0)),
            pl.BlockSpec((tm, W), lambda i: (rev(i), 0)),
            pl.BlockSpec((tm, W), lambda i: (rev(i), 0)),
            pl.BlockSpec((8, W), lambda i: (jnp.maximum(rev(i) * hb - 1, 0), 0)),
            pl.BlockSpec((8, W), lambda i: (jnp.maximum(rev(i) * hb - 1, 0), 0)),
            _full((SUPER, 512, 128)), _full((SUPER, 512, 128)),
            _full((SUPER, 128, 512)), _full((SUPER, 128, 512)),
            _full((1, S5_W)), _full((S5_W, S5_W)), _full((S5_W, S5_W)), _full((1, S5_W)),
            _full((8, 8, W)),
        ] + r_in,
        out_specs=[
            pl.BlockSpec((tm, S5_W), lambda i: (rev(i), 0)),
            pl.BlockSpec((tm, S5_W), lambda i: (rev(i), 0)),
            acc((SUPER, 128, 512)), acc((SUPER, 128, 512)),
            acc((SUPER, 512, 128)), acc((SUPER, 512, 128)),
            acc((1, S5_W)), acc((S5_W, S5_W)), acc((1, S5_W)), acc((2, 8, W)),
        ] + r_out,
        out_shape=[
            jax.ShapeDtypeStruct((T, S5_W), BF16),
            jax.ShapeDtypeStruct((T, S5_W), BF16),
            jax.ShapeDtypeStruct((SUPER, 128, 512), F32), jax.ShapeDtypeStruct((SUPER, 128, 512), F32),
            jax.ShapeDtypeStruct((SUPER, 512, 128), F32), jax.ShapeDtypeStruct((SUPER, 512, 128), F32),
            jax.ShapeDtypeStruct((1, S5_W), F32), jax.ShapeDtypeStruct((S5_W, S5_W), F32),
            jax.ShapeDtypeStruct((1, S5_W), F32), jax.ShapeDtypeStruct((2, 8, W), F32),
        ] + r_shapes,
        scratch_shapes=[pltpu.VMEM((tm + 8, W), F32), pltpu.VMEM((tm + 8, W), F32), pltpu.VMEM((tm, S5_W), BF16)] + r_sems,
        compiler_params=_cparams(1),
    )(dy, z, z, yraw, sre, sim, sre, sim,
      p["b4re_t"], p["b4im_t"], p["c4re_t"], p["c4im_t"], p["dvec"], p["wglu"], p["wglu_t"], p["bglu"], p["coef_r"],
      *r_args)


def _ln_fwd(vf, lng, lnb):
    mu = jnp.mean(vf, axis=-1, keepdims=True)
    d = vf - mu
    rstd = lax.rsqrt(jnp.mean(d * d, axis=-1, keepdims=True) + LN_EPS)
    xh = d * rstd
    return xh, rstd, xh * lng + lnb


def _col_block(tm, b):
    return pl.BlockSpec((tm, 512), lambda i: (i, b))


def _ln_halves(vf0, vf1):
    mu = (jnp.sum(vf0, axis=-1, keepdims=True) + jnp.sum(vf1, axis=-1, keepdims=True)) * (1.0 / SGU_W)
    d0, d1 = vf0 - mu, vf1 - mu
    var = (jnp.sum(d0 * d0, axis=-1, keepdims=True) + jnp.sum(d1 * d1, axis=-1, keepdims=True)) * (1.0 / SGU_W)
    rstd = lax.rsqrt(var + LN_EPS)
    return d0 * rstd, d1 * rstd, rstd


def _sgu_fwd(z, ws, bsf, lng, lnb):
    T = z.shape[0]
    tm = min(512, T)

    def body(u0, u1, v0, v1, g0, g1, ws_ref, bs_ref, lng_ref, lnb_ref, yb_ref, vn_ref):
        for c in range(tm // CHUNK):
            rows = slice(c * CHUNK, (c + 1) * CHUNK)
            xh0, xh1, _ = _ln_halves(_gelu(v0[rows, :]), _gelu(v1[rows, :]))
            vn_ref[:, 0:512] = (xh0 * lng_ref[:, 0:512] + lnb_ref[:, 0:512]).astype(BF16)
            vn_ref[:, 512:1024] = (xh1 * lng_ref[:, 512:1024] + lnb_ref[:, 512:1024]).astype(BF16)
            for half, (u_ref, g_ref) in enumerate(((u0, g0), (u1, g1))):
                sg, _ = _silu_and_grad(g_ref[rows, :])
                m = _gelu(u_ref[rows, :]) * sg
                for hh in range(SGU_HEADS // 2):
                    h = half * (SGU_HEADS // 2) + hh
                    cols = slice(h * 128, (h + 1) * 128)
                    s = jnp.dot(ws_ref[h], vn_ref[:, cols], preferred_element_type=F32) + bs_ref[:, cols]
                    yb_ref[rows, cols] = (m[:, hh * 128:(hh + 1) * 128] * s).astype(BF16)

    return pl.pallas_call(
        body,
        name="sgu_fwd",
        grid=(T // tm,),
        in_specs=[_col_block(tm, b) for b in (1, 2, 3, 4, 7, 8)] + [
            _full((SGU_HEADS, CHUNK, CHUNK)), _full((CHUNK, SGU_W)), _full((1, SGU_W)), _full((1, SGU_W)),
        ],
        out_specs=pl.BlockSpec((tm, SGU_W), lambda i: (i, 0)),
        out_shape=jax.ShapeDtypeStruct((T, SGU_W), BF16),
        scratch_shapes=[pltpu.VMEM((CHUNK, SGU_W), BF16)],
        compiler_params=_cparams(1),
    )(z, z, z, z, z, z, ws, bsf, lng, lnb)


def _sgu_bwd(dy, z, ws, ws_t, bsf, lng, lnb, rider=None):
    T = z.shape[0]
    tm = min(512, T)
    HH = SGU_HEADS // 2

    def body(u0, u1, v0, v1, g0, g1, dy0, dy1, ws_ref, wst_ref, bs_ref, lng_ref, lnb_ref,
             du_ref, dv_ref, dgb_ref, dws_ref, dbs_ref, dlng_ref, dlnb_ref, vn_ref, dvn_ref):
        @pl.when(pl.program_id(0) == 0)
        def _():
            for ref in (dws_ref, dbs_ref, dlng_ref, dlnb_ref):
                ref[...] = jnp.zeros_like(ref)

        for c in range(tm // CHUNK):
            rows = slice(c * CHUNK, (c + 1) * CHUNK)
            vf0, dgv0 = _gelu_and_grad(v0[rows, :])
            vf1, dgv1 = _gelu_and_grad(v1[rows, :])
            xh0, xh1, rstd = _ln_halves(vf0, vf1)
            vn_ref[:, 0:512] = (xh0 * lng_ref[:, 0:512] + lnb_ref[:, 0:512]).astype(BF16)
            vn_ref[:, 512:1024] = (xh1 * lng_ref[:, 512:1024] + lnb_ref[:, 512:1024]).astype(BF16)
            for half, (u_ref, g_ref, dy_ref) in enumerate(((u0, g0, dy0), (u1, g1, dy1))):
                ug, dgu = _gelu_and_grad(u_ref[rows, :])
                sg, dsg = _silu_and_grad(g_ref[rows, :])
                dyb = dy_ref[rows, :]
                dyb0 = dyb * sg
                ds_half = dyb0 * ug
                du_scale = dyb0 * dgu
                dg_scale = dyb * ug * dsg
                for hh in range(HH):
                    h = half * HH + hh
                    cols = slice(h * 128, (h + 1) * 128)
                    lc = slice(hh * 128, (hh + 1) * 128)
                    s = jnp.dot(ws_ref[h], vn_ref[:, cols], preferred_element_type=F32) + bs_ref[:, cols]
                    du_ref[rows, cols] = (du_scale[:, lc] * s).astype(BF16)
                    dgb_ref[rows, cols] = (dg_scale[:, lc] * s).astype(BF16)
                    ds = ds_half[:, lc]
                    dbs_ref[:, cols] += ds
                    dsb = ds.astype(BF16)
                    dws_ref[h] += _dot_nt(dsb, vn_ref[:, cols])
                    dvn_ref[:, cols] = jnp.dot(wst_ref[h], dsb, preferred_element_type=F32)
            dvn0 = dvn_ref[:, 0:512]
            dvn1 = dvn_ref[:, 512:1024]
            dlnb_ref[:, 0:512] += jnp.sum(dvn0, axis=0, keepdims=True)
            dlnb_ref[:, 512:1024] += jnp.sum(dvn1, axis=0, keepdims=True)
            dlng_ref[:, 0:512] += jnp.sum(dvn0 * xh0, axis=0, keepdims=True)
            dlng_ref[:, 512:1024] += jnp.sum(dvn1 * xh1, axis=0, keepdims=True)
            dxh0 = dvn0 * lng_ref[:, 0:512]
            dxh1 = dvn1 * lng_ref[:, 512:1024]
            m1 = (jnp.sum(dxh0, axis=-1, keepdims=True) + jnp.sum(dxh1, axis=-1, keepdims=True)) * (1.0 / SGU_W)
            m2 = (jnp.sum(dxh0 * xh0, axis=-1, keepdims=True) + jnp.sum(dxh1 * xh1, axis=-1, keepdims=True)) * (1.0 / SGU_W)
            dv_ref[rows, 0:512] = (rstd * (dxh0 - m1 - xh0 * m2) * dgv0).astype(BF16)
            dv_ref[rows, 512:1024] = (rstd * (dxh1 - m1 - xh1 * m2) * dgv1).astype(BF16)

    row_out = pl.BlockSpec((tm, SGU_W), lambda i: (i, 0))
    r_in, r_out, r_shapes, r_sems, r_args = _rider_specs(rider)
    return pl.pallas_call(
        _ride(body, 13, 7, rider, lambda: pl.program_id(0) == 0, lambda: pl.program_id(0) == T // tm - 1),
        name="sgu_bwd" + ("" if rider is None else "_ride"),
        grid=(T // tm,),
        in_specs=[_col_block(tm, b) for b in (1, 2, 3, 4, 7, 8)] + [_col_block(tm, 1), _col_block(tm, 2)] + [
            _full((SGU_HEADS, CHUNK, CHUNK)), _full((SGU_HEADS, CHUNK, CHUNK)),
            _full((CHUNK, SGU_W)), _full((1, SGU_W)), _full((1, SGU_W)),
        ] + r_in,
        out_specs=[row_out, row_out, row_out,
                   _full((SGU_HEADS, CHUNK, CHUNK)), _full((CHUNK, SGU_W)), _full((1, SGU_W)), _full((1, SGU_W))] + r_out,
        out_shape=[
            jax.ShapeDtypeStruct((T, SGU_W), BF16), jax.ShapeDtypeStruct((T, SGU_W), BF16),
            jax.ShapeDtypeStruct((T, SGU_W), BF16),
            jax.ShapeDtypeStruct((SGU_HEADS, CHUNK, CHUNK), F32), jax.ShapeDtypeStruct((CHUNK, SGU_W), F32),
            jax.ShapeDtypeStruct((1, SGU_W), F32), jax.ShapeDtypeStruct((1, SGU_W), F32),
        ] + r_shapes,
        scratch_shapes=[pltpu.VMEM((CHUNK, SGU_W), BF16), pltpu.VMEM((CHUNK, SGU_W), F32)] + r_sems,
        compiler_params=_cparams(1),
    )(z, z, z, z, z, z, dy, dy, ws, ws_t, bsf, lng, lnb, *r_args)


def _pool_den(first_row, n):
    return (lax.broadcasted_iota(jnp.int32, (n, 1), 0) + first_row + 1).astype(F32)


def _pool_p(ext, xc, pos, tm):
    w2 = ext + pltpu.roll(ext, 1, 0)
    w4 = w2 + pltpu.roll(w2, 2, 0)
    w8 = w4 + pltpu.roll(w4, 4, 0)
    w16 = w8 + pltpu.roll(w8, 8, 0)
    out = []
    for g, (w, ws) in enumerate(zip(POOL_WINDOWS, (w2, w4, w8, w16))):
        cols = slice(g * 128, (g + 1) * 128)
        mean = ws[POOL_HALO:POOL_HALO + tm, cols] / jnp.minimum(pos, float(w))
        out.append(mean - xc[:, cols])
    return out


def _pool_fwd(z, wp, scale):
    T = z.shape[0]
    tm = min(512, T)
    hb = tm // POOL_HALO

    def body(xc_ref, hx_ref, gc_ref, wp_ref, sc_ref, yc_ref):
        i = pl.program_id(0)
        xc = xc_ref[...]
        halo = hx_ref[...] * (i > 0).astype(F32)
        ext = jnp.concatenate([halo, xc], axis=0)
        ps = _pool_p(ext, xc, _pool_den(i * tm, tm), tm)
        sg, _ = _silu_and_grad(gc_ref[...])
        for g in range(4):
            cols = slice(g * 128, (g + 1) * 128)
            pw = _dot(ps[g], wp_ref[g])
            yc_ref[:, cols] = (pw * sc_ref[:, cols] * sg[:, cols]).astype(BF16)

    return pl.pallas_call(
        body,
        name="pool_fwd",
        grid=(T // tm,),
        in_specs=[
            _col_block(tm, 5),
            pl.BlockSpec((POOL_HALO, 512), lambda i: (jnp.maximum(i * hb - 1, 0), 5)),
            _col_block(tm, 9),
            _full((4, 128, 128)), _full((1, POOL_W)),
        ],
        out_specs=pl.BlockSpec((tm, POOL_W), lambda i: (i, 0)),
        out_shape=jax.ShapeDtypeStruct((T, POOL_W), BF16),
        compiler_params=_cparams(1),
    )(z, z, z, wp, scale)


def _pool_bwd(dy, z, wp, wp_t, scale):
    T = z.shape[0]
    tm = min(512, T)
    nt = T // tm
    hb = tm // POOL_HALO
    last_hb = T // POOL_HALO - 1
    L = tm + POOL_HALO

    def body(xc_ref, hx_ref, gc_ref, gn_ref, dyc_ref, dyn_ref, wp_ref, wpt_ref, sc_ref,
             dxc_ref, dgc_ref, dwp_ref, dsc_ref):
        i = pl.program_id(0)

        @pl.when(i == 0)
        def _():
            dwp_ref[...] = jnp.zeros_like(dwp_ref)
            dsc_ref[...] = jnp.zeros_like(dsc_ref)

        xc = xc_ref[...]
        halo = hx_ref[...] * (i > 0).astype(F32)
        pos = _pool_den(i * tm, tm)
        ps = _pool_p(jnp.concatenate([halo, xc], axis=0), xc, pos, tm)
        sg, dsg = _silu_and_grad(gc_ref[...])
        dyc = dyc_ref[...]
        dyc0 = dyc * sg
        dpw = dyc0 * sc_ref[...]
        sgn, _ = _silu_and_grad(gn_ref[...])
        dpwn = dyn_ref[...] * sgn * sc_ref[...] * (i < nt - 1).astype(F32)
        posn = _pool_den((i + 1) * tm, POOL_HALO)
        dps, qs = [], []
        for g, w in enumerate(POOL_WINDOWS):
            cols = slice(g * 128, (g + 1) * 128)
            pw = _dot(ps[g], wp_ref[g])
            dgc_ref[:, cols] = (dyc[:, cols] * pw * sc_ref[:, cols] * dsg[:, cols]).astype(BF16)
            dsc_ref[:, cols] += jnp.sum(dyc0[:, cols] * pw, axis=0, keepdims=True)
            dwp_ref[g] += _dot_tn(ps[g], dpw[:, cols])
            dp = _dot(dpw[:, cols], wpt_ref[g])
            dpn = _dot(dpwn[:, cols], wpt_ref[g])
            dps.append(dp)
            qs.append(jnp.concatenate([dp / jnp.minimum(pos, float(w)), dpn / jnp.minimum(posn, float(w))], axis=0))
        ext = jnp.concatenate(qs, axis=1)
        f2 = ext + pltpu.roll(ext, L - 1, 0)
        f4 = f2 + pltpu.roll(f2, L - 2, 0)
        f8 = f4 + pltpu.roll(f4, L - 4, 0)
        f16 = f8 + pltpu.roll(f8, L - 8, 0)
        for g, f in enumerate((f2, f4, f8, f16)):
            cols = slice(g * 128, (g + 1) * 128)
            dxc_ref[:, cols] = (f[0:tm, cols] - dps[g]).astype(BF16)

    nxt = lambda i: jnp.minimum((i + 1) * hb, last_hb)
    return pl.pallas_call(
        body,
        name="pool_bwd",
        grid=(nt,),
        in_specs=[
            _col_block(tm, 5),
            pl.BlockSpec((POOL_HALO, 512), lambda i: (jnp.maximum(i * hb - 1, 0), 5)),
            _col_block(tm, 9),
            pl.BlockSpec((POOL_HALO, 512), lambda i: (nxt(i), 9)),
            _col_block(tm, 3),
            pl.BlockSpec((POOL_HALO, 512), lambda i: (nxt(i), 3)),
            _full((4, 128, 128)), _full((4, 128, 128)), _full((1, POOL_W)),
        ],
        out_specs=[
            pl.BlockSpec((tm, POOL_W), lambda i: (i, 0)), pl.BlockSpec((tm, POOL_W), lambda i: (i, 0)),
            _full((4, 128, 128)), _full((1, POOL_W)),
        ],
        out_shape=[
            jax.ShapeDtypeStruct((T, POOL_W), BF16), jax.ShapeDtypeStruct((T, POOL_W), BF16),
            jax.ShapeDtypeStruct((4, 128, 128), F32), jax.ShapeDtypeStruct((1, POOL_W), F32),
        ],
        compiler_params=_cparams(1),
    )(z, z, z, z, dy, dy, wp, wp_t, scale)


def _row_tile(rows, cols):
    tr = 8
    while tr * 2 * cols * 4 <= 2 * 1024 * 1024 and rows % (tr * 2) == 0:
        tr *= 2
    return tr


def _add_own_half(part, recv, cidx):
    _, _, R2, C = part.shape
    tr = _row_tile(R2, C)

    def body(c_ref, a_ref, r_ref, o_ref):
        o_ref[...] = (a_ref[...] + r_ref[...]).astype(BF16)

    return pl.pallas_call(
        body,
        name="add_own_half",
        grid_spec=pltpu.PrefetchScalarGridSpec(
            num_scalar_prefetch=1,
            grid=(N_CHIPS, R2 // tr),
            in_specs=[
                pl.BlockSpec((None, None, tr, C), lambda j, i, c: (j, c[0], i, 0)),
                pl.BlockSpec((None, tr, C), lambda j, i, c: (j, i, 0)),
            ],
            out_specs=pl.BlockSpec((None, tr, C), lambda j, i, c: (j, i, 0)),
        ),
        out_shape=jax.ShapeDtypeStruct((N_CHIPS, R2, C), BF16),
        compiler_params=_cparams(2),
    )(cidx, part, recv)


def _add2(a, b):
    R, C = a.shape
    tr = _row_tile(R, C)

    def body(a_ref, b_ref, o_ref):
        o_ref[...] = a_ref[...] + b_ref[...]

    spec = pl.BlockSpec((tr, C), lambda i: (i, 0))
    return pl.pallas_call(
        body, name="add2", grid=(R // tr,), in_specs=[spec, spec], out_specs=spec,
        out_shape=jax.ShapeDtypeStruct((R, C), F32), compiler_params=_cparams(1),
    )(a, b)


def _sum_chips(parts):
    _, R, C = parts.shape
    tr = _row_tile(R, N_CHIPS * C)

    def body(p_ref, o_ref):
        p = [p_ref[j].astype(F32) for j in range(N_CHIPS)]
        o_ref[...] = ((p[0] + p[1]) + p[2]) + p[3]

    return pl.pallas_call(
        body, name="sum_chips", grid=(R // tr,),
        in_specs=[pl.BlockSpec((N_CHIPS, tr, C), lambda i: (0, i, 0))],
        out_specs=pl.BlockSpec((tr, C), lambda i: (i, 0)),
        out_shape=jax.ShapeDtypeStruct((R, C), F32), compiler_params=_cparams(1),
    )(parts)


def _adamw_math(w, g, m, v):
    m = ADAM_B1 * m + (1.0 - ADAM_B1) * g
    v = ADAM_B2 * v + (1.0 - ADAM_B2) * (g * g)
    m_hat = m / (1.0 - ADAM_B1 ** ADAM_STEP)
    v_hat = v / (1.0 - ADAM_B2 ** ADAM_STEP)
    delta = -ADAM_LR * (m_hat / (jnp.sqrt(v_hat) + ADAM_EPS) + ADAM_WD * w)
    return delta, m, v


def _adamw(w, g, m, v):
    R, C = w.shape
    tr = _row_tile(R, C)

    def body(w_ref, g_ref, m_ref, v_ref, d_ref, mo_ref, vo_ref):
        d_ref[...], mo_ref[...], vo_ref[...] = _adamw_math(w_ref[...], g_ref[...], m_ref[...], v_ref[...])

    spec = pl.BlockSpec((tr, C), lambda i: (i, 0))
    shp = jax.ShapeDtypeStruct((R, C), F32)
    return pl.pallas_call(
        body, name="adamw", grid=(R // tr,), in_specs=[spec] * 4, out_specs=[spec] * 3,
        out_shape=[shp] * 3, compiler_params=_cparams(1),
    )(w, g, m, v)


def _adamw_halves(w, mine, theirs, m, v, cidx, rider=None):
    _, _, R2, C = w.shape
    tr = _row_tile(R2, C)
    nr = R2 // tr

    def body(c_ref, w_ref, a0_ref, b0_ref, a1_ref, b1_ref, m_ref, v_ref, g_ref, d_ref, mo_ref, vo_ref):
        own = pl.program_id(1) == c_ref[0]
        g0 = jnp.where(own, a0_ref[...], b0_ref[...])
        g1 = jnp.where(own, a1_ref[...], b1_ref[...])
        g = jnp.where(pl.program_id(0) == 0, g0, g1)
        g_ref[...] = g
        d_ref[...], mo_ref[...], vo_ref[...] = _adamw_math(w_ref[...], g, m_ref[...], v_ref[...])

    full = pl.BlockSpec((None, None, tr, C), lambda l, h, i, c: (l, h, i, 0))

    def pick(layer, mine_side):
        def index(l, h, i, c):
            used = (l == layer) & ((h == c[0]) == mine_side)
            return (jnp.where(used, i, 0), 0)
        return pl.BlockSpec((tr, C), index)

    shp = jax.ShapeDtypeStruct(w.shape, F32)
    r_in, r_out, r_shapes, r_sems, r_args = _rider_specs(rider)
    last = lambda: (pl.program_id(0) == 1) & (pl.program_id(1) == 1) & (pl.program_id(2) == nr - 1)
    first = lambda: (pl.program_id(0) == 0) & (pl.program_id(1) == 0) & (pl.program_id(2) == 0)
    return pl.pallas_call(
        _ride(body, 8, 4, rider, first, last),
        name="adamw_halves" + ("" if rider is None else "_ride"),
        grid_spec=pltpu.PrefetchScalarGridSpec(
            num_scalar_prefetch=1, grid=(2, 2, nr),
            in_specs=[full, pick(0, True), pick(0, False), pick(1, True), pick(1, False), full, full] + r_in,
            out_specs=[full] * 4 + r_out,
            scratch_shapes=r_sems,
        ),
        out_shape=[shp] * 4 + r_shapes,
        compiler_params=_cparams(3),
    )(cidx, w, mine[0], theirs[0], mine[1], theirs[1], m, v, *r_args)


_ANY = pl.BlockSpec(memory_space=pl.ANY)


def _mesh_pos():
    return lax.axis_index("x"), lax.axis_index("y"), lax.axis_index("c")


def _other_chips(x, y):
    return [(2 * x + (1 - y), x, 1 - y), (2 * (1 - x) + y, 1 - x, y), (2 * (1 - x) + (1 - y), 1 - x, 1 - y)]


def _gathered_shapes(shards):
    return [jax.ShapeDtypeStruct((2, N_CHIPS) + s.shape[1:], s.dtype) for s in shards]


def _gather_sems(n):
    return [pltpu.SemaphoreType.DMA((2 * n,)), pltpu.SemaphoreType.DMA((6 * n,)), pltpu.SemaphoreType.DMA((6 * n,))]


def _gather_steps(ins, outs, lsem, ssem, rsem):
    n = len(ins)
    x, y, c = _mesh_pos()
    me = 2 * x + y
    sib = (x, y, 1 - c)
    chips = _other_chips(x, y)

    def ici(k, d):
        return pltpu.make_async_remote_copy(
            ins[k].at[c], outs[k].at[c, me], ssem.at[6 * k + d], rsem.at[6 * k + d],
            device_id=(chips[d][1], chips[d][2], c), device_id_type=MESH_ID)

    def landed(k, d):
        return pltpu.make_async_remote_copy(
            ins[k].at[c], outs[k].at[c, chips[d][0]], ssem.at[6 * k + d], rsem.at[6 * k + d],
            device_id=sib, device_id_type=MESH_ID)

    def fwd(k, d, half):
        return pltpu.make_async_remote_copy(
            outs[k].at[half, chips[d][0]], outs[k].at[half, chips[d][0]], ssem.at[6 * k + 3 + d],
            rsem.at[6 * k + 3 + d], device_id=sib, device_id_type=MESH_ID)

    def local(k, h):
        return pltpu.make_async_copy(ins[k].at[h], outs[k].at[h, me], lsem.at[2 * k + h])

    def start():
        for k in range(n):
            for h in range(2):
                local(k, h).start()
            for d in range(3):
                ici(k, d).start()

    def mid():
        for d in range(3):
            for k in range(n):
                landed(k, d).wait_recv()
                fwd(k, d, c).start()

    def end():
        for d in range(3):
            for k in range(n):
                fwd(k, d, 1 - c).wait_recv()
        for k in range(n):
            for d in range(3):
                ici(k, d).wait_send()
                fwd(k, d, c).wait_send()
            for h in range(2):
                local(k, h).wait()

    return start, mid, end


def _gather_weights(shards):
    n = len(shards)

    def body(*refs):
        for step in _gather_steps(refs[:n], refs[n:2 * n], *refs[2 * n:]):
            step()

    return pl.pallas_call(
        body,
        name="gather_weights",
        in_specs=[_ANY] * n,
        out_specs=[_ANY] * n,
        out_shape=_gathered_shapes(shards),
        scratch_shapes=_gather_sems(n),
    )(*shards)


def _pair_rider(arrs, other_half):
    n = len(arrs)

    def steps(ins, outs, ssem, rsem):
        x, y, c = _mesh_pos()

        def copy(k):
            return pltpu.make_async_remote_copy(ins[k].at[:, 1 - c] if other_half else ins[k], outs[k], ssem.at[k],
                                                rsem.at[k], device_id=(x, y, 1 - c), device_id_type=MESH_ID)

        def start():
            for k in range(n):
                copy(k).start()

        def end():
            for k in range(n):
                copy(k).wait()

        return start, end

    shapes = [jax.ShapeDtypeStruct(a.shape[:1] + a.shape[2:] if other_half else a.shape, a.dtype) for a in arrs]
    return _Rider(arrs, shapes, [pltpu.SemaphoreType.DMA((n,)), pltpu.SemaphoreType.DMA((n,))], steps)


def _chip_rider(arrs, broadcast):
    n = len(arrs)

    def steps(ins, outs, lsem, ssem, rsem):
        x, y, c = _mesh_pos()
        me = 2 * x + y

        def copies():
            cps = [pltpu.make_async_copy(ins[k] if broadcast else ins[k].at[me], outs[k].at[me], lsem.at[k])
                   for k in range(n)]
            for k in range(n):
                for d, (j, tx, ty) in enumerate(_other_chips(x, y)):
                    cps.append(pltpu.make_async_remote_copy(
                        ins[k] if broadcast else ins[k].at[j], outs[k].at[me], ssem.at[3 * k + d], rsem.at[3 * k + d],
                        device_id=(tx, ty, c), device_id_type=MESH_ID))
            return cps

        def start():
            for cp in copies():
                cp.start()

        def end():
            for cp in copies():
                cp.wait()

        return start, end

    shapes = [jax.ShapeDtypeStruct(((N_CHIPS,) + a.shape) if broadcast else a.shape, a.dtype) for a in arrs]
    sems = [pltpu.SemaphoreType.DMA((n,)), pltpu.SemaphoreType.DMA((3 * n,)), pltpu.SemaphoreType.DMA((3 * n,))]
    return _Rider(arrs, shapes, sems, steps)


def _run_rider(name, rider):
    n, m = len(rider.arrs), len(rider.out_shapes)

    def body(*refs):
        start, end = rider.steps(refs[:n], refs[n:n + m], *refs[n + m:])
        start()
        end()

    return pl.pallas_call(
        body, name=name, in_specs=[_ANY] * n, out_specs=[_ANY] * m, out_shape=rider.out_shapes,
        scratch_shapes=rider.sems,
    )(*rider.arrs)


SMALL = ("norm_g", "lam_re", "lam_im", "b_re", "b_im", "c_re", "c_im", "d_skip", "log_dt", "b_glu", "ln_g", "ln_b",
         "w_s", "b_s", "w_pool", "pool_scale", "final_g")
BIG = ("w_in", "w_glu", "w_out")
WEIGHTS = ("norm_g", "w_in", "lam_re", "lam_im", "b_re", "b_im", "c_re", "c_im", "d_skip", "log_dt", "w_glu", "b_glu",
           "ln_g", "ln_b", "w_s", "b_s", "w_pool", "pool_scale", "w_out", "final_g")
PACK_UNIT = 8 * 128
PACK_ROWS = 1024


def _pack(arrs):
    parts, total = [], 0
    for a in arrs:
        f = a.reshape(-1).astype(F32)
        pad = (-f.shape[0]) % PACK_UNIT
        parts.append(jnp.pad(f, (0, pad)) if pad else f)
        total += f.shape[0] + pad
    tail = (-total) % (PACK_ROWS * 128)
    if tail:
        parts.append(jnp.zeros((tail,), F32))
    return jnp.concatenate(parts).reshape(-1, 128)


def _unpack(buf, like):
    flat = buf.reshape(-1)
    out, off = [], 0
    for a in like:
        n = math.prod(a.shape)
        out.append(flat[off:off + n].reshape(a.shape))
        off += n + ((-n) % PACK_UNIT)
    return out


def _layer_params(l, wt, g_glu):
    a_re, a_im, bb_re, bb_im = _s5_prep(wt["lam_re"][l], wt["lam_im"][l], wt["b_re"][l], wt["b_im"][l], wt["log_dt"][l])
    b4re, b4im = _block_diag_in(bb_re), _block_diag_in(bb_im)
    c4re, c4im = _block_diag_out(wt["c_re"][l]), _block_diag_out(-wt["c_im"][l])
    tr = lambda m: jnp.swapaxes(m, 1, 2).astype(BF16)
    causal = jnp.tril(jnp.ones((CHUNK, CHUNK), dtype=bool))
    ws = jnp.where(causal[None], wt["w_s"][l], 0.0)
    wglu = g_glu[l].reshape(S5_W, S5_W)
    return dict(
        b4re=b4re.astype(BF16), b4im=b4im.astype(BF16), c4re=c4re.astype(BF16), c4im=c4im.astype(BF16),
        b4re_t=tr(b4re), b4im_t=tr(b4im), c4re_t=tr(c4re), c4im_t=tr(c4im),
        dvec=wt["d_skip"][l].reshape(1, S5_W), wglu=wglu, wglu_t=wglu.T, bglu=wt["b_glu"][l].reshape(1, S5_W),
        coef_f=_scan_coefs(a_re, a_im, False), coef_r=_scan_coefs(a_re, a_im, True),
        ws=ws.astype(BF16), ws_t=tr(ws),
        bsf=jnp.broadcast_to(wt["b_s"][l][:, None, :], (SGU_HEADS, CHUNK, CHUNK)).transpose(2, 0, 1).reshape(CHUNK, SGU_W),
        lng=wt["ln_g"][l].reshape(1, SGU_W), lnb=wt["ln_b"][l].reshape(1, SGU_W),
        wp=wt["w_pool"][l].astype(BF16), wp_t=tr(wt["w_pool"][l]), scale=wt["pool_scale"][l].reshape(1, POOL_W),
        norm_g=wt["norm_g"][l].reshape(1, D_MODEL),
    )


def _local_step(x0, tgt, wt, g_in0, rest, rest_gathered, cidx=None):
    dist = cidx is not None
    xs, saved, params = [x0], [], []
    for l in range(DEPTH):
        norm_g = wt["norm_g"][l].reshape(1, D_MODEL)
        if l == 0 and not rest_gathered:
            z, h, g_in1, g_glu, g_out = _inproj(xs[-1], norm_g, g_in0, list(rest))
        elif l == 0:
            g_in1, g_glu, g_out = rest
            z, h = _inproj(xs[-1], norm_g, g_in0)
        else:
            z, h = _inproj(xs[-1], norm_g, g_in1)
        g_in = (g_in0, g_in1)
        p = _layer_params(l, wt, g_glu)
        params.append(p)
        ya, yraw, sre, sim = _s5_fwd(z, p)
        yb = _sgu_fwd(z, p["ws"], p["bsf"], p["lng"], p["lnb"])
        yc = _pool_fwd(z, p["wp"], p["scale"])
        xn, y = _outproj(ya, yb, yc, g_out[l].reshape(D_MODEL, D_MODEL), xs[-1])
        xs.append(xn)
        saved.append((z, h, yraw, sre, sim, y))

    dx, loss, dfg = _loss_head(xs[-1], wt["final_g"].reshape(1, D_MODEL), tgt)

    gr = {k: [None] * DEPTH for k in WEIGHTS if k != "final_g"}
    mine, theirs, chip_sum = [None] * DEPTH, [None] * DEPTH, None
    halves = lambda a, rows: a.reshape(N_CHIPS, 2, rows // 2, a.shape[-1])
    for l in reversed(range(DEPTH)):
        p = params[l]
        z, h, yraw, sre, sim, y = saved[l]
        w_out = g_out[l].reshape(D_MODEL, D_MODEL)
        dy = _outproj_bwd_dy(dx, w_out)
        gr["w_out"][l] = _outproj_bwd_dw(y, dx)
        ride_c = _chip_rider(chip_sum, False) if dist and l == 0 else None
        dxa, dga, dbre, dbim, dcre, dcim, dd, dwg, dbg, da, *landed = _s5_bwd(dy, z, yraw, sre, sim, p, ride_c)
        if ride_c:
            mine[1] = [_sum_chips(r) for r in landed]
        ride_e = _pair_rider(mine[1], False) if dist and l == 0 else None
        du, dv, dgb, dws, dbsf, dlng, dlnb, *got = _sgu_bwd(dy, z, p["ws"], p["ws_t"], p["bsf"], p["lng"], p["lnb"],
                                                           ride_e)
        if ride_e:
            theirs[1] = got
        dxc, dgc, dwp, dsc = _pool_bwd(dy, z, p["wp"], p["wp_t"], p["scale"])
        dz = jnp.concatenate([dxa, du, dv, dxc, dga, dgb, dgc], axis=1)
        gr["w_in"][l] = _inproj_bwd_dw(h, dz)
        if not dist:
            dx, dng = _inproj_bwd_dx(dz, g_in[l], xs[l], p["norm_g"], dx)
        else:
            part = [halves(gr["w_in"][l], D_MODEL), halves(dwg, S5_W // N_CHIPS),
                    halves(gr["w_out"][l], D_MODEL // N_CHIPS)]
            ride_a = _pair_rider(part, True)
            if l == 1:
                dx, dng, *from_sib = _inproj_bwd_dx(dz, g_in[l], xs[l], p["norm_g"], dx, ride_a)
                chip_sum = [_add_own_half(a, r, cidx) for a, r in zip(part, from_sib)]
            else:
                nt = x0.shape[0] // _dx_tile(x0.shape[0])
                dx_top, dng_top, *from_sib = _inproj_bwd_dx(dz, g_in[l], xs[l], p["norm_g"], dx, ride_a,
                                                            tiles=(0, nt // 2))
                chip_sum0 = [_add_own_half(a, r, cidx) for a, r in zip(part, from_sib)]
                dx, dng_rest, *landed = _inproj_bwd_dx(dz, g_in[l], xs[l], p["norm_g"], dx,
                                                       _chip_rider(chip_sum0, False), tiles=(nt // 2, nt - nt // 2),
                                                       prev=dx_top)
                dng = dng_top + dng_rest
                mine[0] = [_sum_chips(r) for r in landed]
                theirs[0] = _run_rider("grad_result_exchange", _pair_rider(mine[0], False))

        raw = (wt["lam_re"][l], wt["lam_im"][l], wt["b_re"][l], wt["b_im"][l], wt["log_dt"][l])
        _, vjp = jax.vjp(_s5_prep, *raw)
        da = jnp.sum(da, axis=1)
        cot = (da[0].reshape(S5_GROUPS, S5_STATE), da[1].reshape(S5_GROUPS, S5_STATE),
               _block_diag_in_grad(dbre), _block_diag_in_grad(dbim))
        gr["lam_re"][l], gr["lam_im"][l], gr["b_re"][l], gr["b_im"][l], gr["log_dt"][l] = vjp(cot)
        gr["c_re"][l] = _block_diag_out_grad(dcre)
        gr["c_im"][l] = -_block_diag_out_grad(dcim)
        gr["d_skip"][l] = dd.reshape(S5_GROUPS, S5_CH)
        gr["w_glu"][l] = dwg
        gr["b_glu"][l] = dbg.reshape(S5_W)
        causal = jnp.tril(jnp.ones((CHUNK, CHUNK), dtype=bool))
        gr["w_s"][l] = jnp.where(causal[None], dws, 0.0)
        gr["b_s"][l] = dbsf.reshape(CHUNK, SGU_HEADS, CHUNK).sum(-1).T
        gr["ln_g"][l] = dlng.reshape(SGU_W)
        gr["ln_b"][l] = dlnb.reshape(SGU_W)
        gr["w_pool"][l] = dwp
        gr["pool_scale"][l] = dsc.reshape(POOL_W)
        gr["norm_g"][l] = dng.reshape(D_MODEL)

    grads = {k: (v if k in BIG else jnp.stack(v)) for k, v in gr.items()}
    grads["final_g"] = dfg.reshape(D_MODEL)
    if dist:
        for i, k in enumerate(BIG):
            grads[k] = ([mine[l][i] for l in range(DEPTH)], [theirs[l][i] for l in range(DEPTH)])
    return loss, dx, grads


def kernel(x, norm_g, w_in, lam_re, lam_im, b_re, b_im, c_re, c_im, d_skip, log_dt, w_glu, b_glu, ln_g, ln_b, w_s, b_s, w_pool, pool_scale, w_out, final_g, loss_target, m_norm_g, m_w_in, m_lam_re, m_lam_im, m_b_re, m_b_im, m_c_re, m_c_im, m_d_skip, m_log_dt, m_w_glu, m_b_glu, m_ln_g, m_ln_b, m_w_s, m_b_s, m_w_pool, m_pool_scale, m_w_out, m_final_g, v_norm_g, v_w_in, v_lam_re, v_lam_im, v_b_re, v_b_im, v_c_re, v_c_im, v_d_skip, v_log_dt, v_w_glu, v_b_glu, v_ln_g, v_ln_b, v_w_s, v_b_s, v_w_pool, v_pool_scale, v_w_out, v_final_g):
    wt = dict(norm_g=norm_g, w_in=w_in, lam_re=lam_re, lam_im=lam_im, b_re=b_re, b_im=b_im, c_re=c_re, c_im=c_im,
              d_skip=d_skip, log_dt=log_dt, w_glu=w_glu, b_glu=b_glu, ln_g=ln_g, ln_b=ln_b, w_s=w_s, b_s=b_s,
              w_pool=w_pool, pool_scale=pool_scale, w_out=w_out, final_g=final_g)
    mom = dict(norm_g=m_norm_g, w_in=m_w_in, lam_re=m_lam_re, lam_im=m_lam_im, b_re=m_b_re, b_im=m_b_im, c_re=m_c_re,
               c_im=m_c_im, d_skip=m_d_skip, log_dt=m_log_dt, w_glu=m_w_glu, b_glu=m_b_glu, ln_g=m_ln_g, ln_b=m_ln_b,
               w_s=m_w_s, b_s=m_b_s, w_pool=m_w_pool, pool_scale=m_pool_scale, w_out=m_w_out, final_g=m_final_g)
    vel = dict(norm_g=v_norm_g, w_in=v_w_in, lam_re=v_lam_re, lam_im=v_lam_im, b_re=v_b_re, b_im=v_b_im, c_re=v_c_re,
               c_im=v_c_im, d_skip=v_d_skip, log_dt=v_log_dt, w_glu=v_w_glu, b_glu=v_b_glu, ln_g=v_ln_g, ln_b=v_ln_b,
               w_s=v_w_s, b_s=v_b_s, w_pool=v_w_pool, pool_scale=v_pool_scale, w_out=v_w_out, final_g=v_final_g)
    T = x.shape[1]
    cidx = lax.axis_index("c").astype(jnp.int32).reshape(1)

    w_in_b = w_in.astype(BF16)
    (g_in0,) = _gather_weights([w_in_b[0].reshape(2, HALF_D, SHARD_COLS)])
    rest = (w_in_b[1].reshape(2, HALF_D, SHARD_COLS), w_glu.astype(BF16), w_out.astype(BF16))
    loss, grad_x, grads = _local_step(x.reshape(T, D_MODEL), loss_target.reshape(T, D_MODEL), wt, g_in0, rest, False,
                                      cidx)

    packed = _pack([grads[k] for k in SMALL] + [loss[0, 0:1]])
    (sib_packed,) = _run_rider("small_pair_exchange", _pair_rider([packed], False))
    chip_packed = _add2(packed, sib_packed)
    half_rows = chip_packed.shape[0] // 2
    my_half = lax.dynamic_index_in_dim(chip_packed.reshape(2, half_rows, 128), cidx[0], 0, keepdims=False)
    small_ride = _chip_rider([my_half], True)

    out_g, out_d, out_m, out_v = {}, {}, {}, {}
    all_half = None
    for k in BIG:
        shape = wt[k].shape
        quad = lambda t: t.reshape(2, 2, shape[1] // 2, shape[2])
        g, d, m, v, *landed = _adamw_halves(quad(wt[k]), grads[k][0], grads[k][1], quad(mom[k]), quad(vel[k]), cidx,
                                            small_ride if k == BIG[0] else None)
        if landed:
            (all_half,) = landed
        out_g[k], out_d[k], out_m[k], out_v[k] = (t.reshape(shape) for t in (g, d, m, v))

    mine_half = _sum_chips(all_half)
    (their_half,) = _run_rider("small_result_exchange", _pair_rider([mine_half], False))
    total = jnp.where(cidx[0] == 0, jnp.concatenate([mine_half, their_half]), jnp.concatenate([their_half, mine_half]))
    like = [wt[k] for k in SMALL]
    small_g = _unpack(total, like + [loss[0, 0:1]])
    loss_out = small_g[-1].reshape(())
    w_p, m_p, v_p = _pack(like), _pack([mom[k] for k in SMALL]), _pack([vel[k] for k in SMALL])
    d_p, mo_p, vo_p = _adamw(w_p, total, m_p, v_p)
    for k, g, d, m, v in zip(SMALL, small_g[:-1], _unpack(d_p, like), _unpack(mo_p, like), _unpack(vo_p, like)):
        out_g[k], out_d[k], out_m[k], out_v[k] = g, d, m, v

    return (loss_out, grad_x.reshape(x.shape), *[out_g[k] for k in WEIGHTS], *[out_d[k] for k in WEIGHTS],
            *[out_m[k] for k in WEIGHTS], *[out_v[k] for k in WEIGHTS])
```

```python
import functools
import math

import jax
import jax.numpy as jnp
from jax import lax
from jax.experimental import pallas as pl
from jax.experimental.pallas import tpu as pltpu

F32 = jnp.float32
BF16 = jnp.bfloat16

D_MODEL = 2048
DEPTH = 2
S5_W = 512
SGU_W = 1024
POOL_W = 512
IN_COLS = 5120
N_CHIPS = 4
SHARD_COLS = IN_COLS // N_CHIPS
S5_GROUPS = 32
S5_STATE = 64
S5_CH = 16
STATE_W = S5_GROUPS * S5_STATE
SUPER = 4
CHUNK = 128
SGU_HEADS = 8
POOL_WINDOWS = (2, 4, 8, 16)
POOL_HALO = 16
RMS_EPS = 1e-6
LN_EPS = 1e-5
SCAN_COLS = 512

ADAM_LR = 0.001
ADAM_B1 = 0.9
ADAM_B2 = 0.999
ADAM_EPS = 1e-08
ADAM_WD = 0.01
ADAM_STEP = 10

VMEM_LIMIT = 56 * 1024 * 1024
MESH_ID = pl.DeviceIdType.MESH

_GELU_K0 = math.sqrt(2.0 / math.pi)
_GELU_K1 = 0.044715


def _cparams(n_axes):
    return pltpu.CompilerParams(dimension_semantics=("arbitrary",) * n_axes, vmem_limit_bytes=VMEM_LIMIT)


def _gelu(x):
    t = jnp.tanh(_GELU_K0 * (x + _GELU_K1 * (x * x * x)))
    return 0.5 * x * (1.0 + t)


def _gelu_and_grad(x):
    x2 = x * x
    t = jnp.tanh(_GELU_K0 * (x + _GELU_K1 * (x * x2)))
    g = 0.5 * x * (1.0 + t)
    dg = 0.5 * (1.0 + t) + 0.5 * x * (1.0 - t * t) * (_GELU_K0 * (1.0 + 3.0 * _GELU_K1 * x2))
    return g, dg


def _silu_and_grad(x):
    s = jax.nn.sigmoid(x)
    return x * s, s * (1.0 + x * (1.0 - s))


def _dot(a, b):
    return jnp.dot(a.astype(BF16), b.astype(BF16), preferred_element_type=F32)


def _dot_nt(a, b):
    return lax.dot_general(a.astype(BF16), b.astype(BF16), (((1,), (1,)), ((), ())), preferred_element_type=F32)


def _dot_tn(a, b):
    return lax.dot_general(a.astype(BF16), b.astype(BF16), (((0,), (0,)), ((), ())), preferred_element_type=F32)


def _full(shape):
    nd = len(shape)
    return pl.BlockSpec(shape, lambda *_: (0,) * nd)


class _Rider:
    def __init__(self, arrs, out_shapes, sems, steps):
        self.arrs, self.out_shapes, self.sems, self.steps = list(arrs), list(out_shapes), list(sems), steps


def _ride(body, n_in, n_out, rider, first, last):
    if rider is None:
        return body
    ri, ro, ns = len(rider.arrs), len(rider.out_shapes), len(rider.sems)

    def wrapped(*refs):
        o0 = n_in + ri
        start, end = rider.steps(refs[n_in:o0], refs[o0 + n_out:o0 + n_out + ro], *refs[len(refs) - ns:])
        pl.when(first())(start)
        body(*refs[:n_in], *refs[o0:o0 + n_out], *refs[o0 + n_out + ro:len(refs) - ns])
        pl.when(last())(end)

    return wrapped


class _ColumnWriter:
    def __init__(self, stage_ref, sem_ref, dst_ref, col0, step, n_steps):
        self.stage, self.sem, self.dst, self.col0, self.step, self.n = stage_ref, sem_ref, dst_ref, col0, step, n_steps
        self.tm, self.w = stage_ref.shape[1], stage_ref.shape[2]

    def _copy(self, slot, row0):
        return pltpu.make_async_copy(self.stage.at[slot],
                                     self.dst.at[pl.ds(row0, self.tm), pl.ds(self.col0, self.w)], self.sem.at[slot])

    def slot(self):
        s = self.step % 2

        @pl.when(self.step >= 2)
        def _():
            self._copy(s, 0).wait()

        return self.stage.at[s]

    def send(self, row0):
        s = self.step % 2
        self._copy(s, row0).start()

        @pl.when(self.step == self.n - 1)
        def _():
            self._copy(s, 0).wait()
            if self.n >= 2:
                self._copy(1 - s, 0).wait()


def _stage_scratch(tm, widths):
    return ([pltpu.VMEM((2, tm, w), BF16) for w in widths], [pltpu.SemaphoreType.DMA((2,)) for _ in widths])


def _rider_specs(rider):
    if rider is None:
        return [], [], [], [], []
    anyspec = pl.BlockSpec(memory_space=pl.ANY)
    return ([anyspec] * len(rider.arrs), [anyspec] * len(rider.out_shapes), rider.out_shapes, rider.sems, rider.arrs)


HALF_D = D_MODEL // 2


def _inproj(x, g, w, riders=None):
    T = x.shape[0]
    tm = min(512, T)
    ni = T // tm
    n = len(riders) if riders else 0

    def body(*refs):
        x_ref, g_ref, w_ref = refs[:3]
        rin = refs[3:3 + n]
        z_ref, h_ref = refs[3 + n:5 + n]
        rout = refs[5 + n:5 + 2 * n]
        hs_ref = refs[5 + 2 * n]
        i, j = pl.program_id(0), pl.program_id(1)
        if n:
            start, mid, end = _gather_steps(rin, rout, *refs[6 + 2 * n:])
            pl.when((i == 0) & (j == 0))(start)
            pl.when((i == ni // 2) & (j == 0))(mid)

        @pl.when(j == 0)
        def _():
            xv = x_ref[...]
            r = lax.rsqrt(jnp.mean(xv * xv, axis=-1, keepdims=True) + RMS_EPS)
            hv = (xv * r * g_ref[...]).astype(BF16)
            hs_ref[...] = hv
            h_ref[...] = hv

        z_ref[...] = (jnp.dot(hs_ref[:, 0:HALF_D], w_ref[0], preferred_element_type=F32)
                      + jnp.dot(hs_ref[:, HALF_D:D_MODEL], w_ref[1], preferred_element_type=F32))
        if n:
            pl.when((i == ni - 1) & (j == N_CHIPS - 1))(end)

    return pl.pallas_call(
        body,
        name="inproj_gather" if n else "inproj",
        grid=(ni, N_CHIPS),
        in_specs=[
            pl.BlockSpec((tm, D_MODEL), lambda i, j: (i, 0)),
            pl.BlockSpec((1, D_MODEL), lambda i, j: (0, 0)),
            pl.BlockSpec((2, None, HALF_D, SHARD_COLS), lambda i, j: (0, j, 0, 0)),
        ] + [_ANY] * n,
        out_specs=[
            pl.BlockSpec((tm, SHARD_COLS), lambda i, j: (i, j)),
            pl.BlockSpec((tm, D_MODEL), lambda i, j: (i, 0)),
        ] + [_ANY] * n,
        out_shape=[jax.ShapeDtypeStruct((T, IN_COLS), F32), jax.ShapeDtypeStruct((T, D_MODEL), BF16)]
        + _gathered_shapes(riders or []),
        scratch_shapes=[pltpu.VMEM((tm, D_MODEL), BF16)] + (_gather_sems(n) if n else []),
        compiler_params=_cparams(2),
    )(x, g, w, *(riders or []))


def _outproj(ya, yb, yc, w, x):
    T = x.shape[0]
    tm = min(512, T)
    tn = 1024

    def body(ya_ref, yb_ref, yc_ref, w_ref, x_ref, o_ref, y_ref):
        acc = jnp.dot(ya_ref[...], w_ref[0:S5_W, :], preferred_element_type=F32)
        acc += jnp.dot(yb_ref[...], w_ref[S5_W:S5_W + SGU_W, :], preferred_element_type=F32)
        acc += jnp.dot(yc_ref[...], w_ref[S5_W + SGU_W:D_MODEL, :], preferred_element_type=F32)
        o_ref[...] = x_ref[...] + acc

        @pl.when(pl.program_id(1) == 0)
        def _():
            y_ref[:, 0:S5_W] = ya_ref[...]
            y_ref[:, S5_W:S5_W + SGU_W] = yb_ref[...]
            y_ref[:, S5_W + SGU_W:D_MODEL] = yc_ref[...]

    return pl.pallas_call(
        body,
        name="outproj",
        grid=(T // tm, D_MODEL // tn),
        in_specs=[
            pl.BlockSpec((tm, S5_W), lambda i, j: (i, 0)),
            pl.BlockSpec((tm, SGU_W), lambda i, j: (i, 0)),
            pl.BlockSpec((tm, POOL_W), lambda i, j: (i, 0)),
            pl.BlockSpec((D_MODEL, tn), lambda i, j: (0, j)),
            pl.BlockSpec((tm, tn), lambda i, j: (i, j)),
        ],
        out_specs=[
            pl.BlockSpec((tm, tn), lambda i, j: (i, j)),
            pl.BlockSpec((tm, D_MODEL), lambda i, j: (i, 0)),
        ],
        out_shape=[jax.ShapeDtypeStruct((T, D_MODEL), F32), jax.ShapeDtypeStruct((T, D_MODEL), BF16)],
        compiler_params=_cparams(2),
    )(ya, yb, yc, w, x)


def _outproj_bwd_dy(dxo, w):
    T = dxo.shape[0]
    tm = min(512, T)
    tn = 1024

    def body(d_ref, w_ref, o_ref, ds_ref):
        @pl.when(pl.program_id(1) == 0)
        def _():
            ds_ref[...] = d_ref[...].astype(BF16)

        o_ref[...] = lax.dot_general(ds_ref[...], w_ref[...], (((1,), (1,)), ((), ())), preferred_element_type=F32)

    return pl.pallas_call(
        body,
        name="outproj_bwd_dy",
        grid=(T // tm, D_MODEL // tn),
        in_specs=[
            pl.BlockSpec((tm, D_MODEL), lambda i, j: (i, 0)),
            pl.BlockSpec((tn, D_MODEL), lambda i, j: (j, 0)),
        ],
        out_specs=pl.BlockSpec((tm, tn), lambda i, j: (i, j)),
        out_shape=jax.ShapeDtypeStruct((T, D_MODEL), F32),
        scratch_shapes=[pltpu.VMEM((tm, D_MODEL), BF16)],
        compiler_params=_cparams(2),
    )(dxo, w)


def _outproj_bwd_dw(y, dxo):
    T = y.shape[0]
    tm = min(512, T)
    tr = 1024

    def body(y_ref, d_ref, o_ref):
        @pl.when(pl.program_id(1) == 0)
        def _():
            o_ref[...] = jnp.zeros_like(o_ref)

        o_ref[...] += _dot_tn(y_ref[...], d_ref[...])

    return pl.pallas_call(
        body,
        name="outproj_bwd_dw",
        grid=(D_MODEL // tr, T // tm),
        in_specs=[
            pl.BlockSpec((tm, tr), lambda p, t: (t, p)),
            pl.BlockSpec((tm, D_MODEL), lambda p, t: (t, 0)),
        ],
        out_specs=pl.BlockSpec((tr, D_MODEL), lambda p, t: (p, 0)),
        out_shape=jax.ShapeDtypeStruct((D_MODEL, D_MODEL), F32),
        compiler_params=_cparams(2),
    )(y, dxo)


def _inproj_bwd_dw(h, dz):
    T = h.shape[0]
    tm = min(512, T)

    def body(h_ref, dz_ref, o_ref):
        @pl.when(pl.program_id(1) == 0)
        def _():
            o_ref[...] = jnp.zeros_like(o_ref)

        o_ref[...] += _dot_tn(h_ref[...], dz_ref[...])

    return pl.pallas_call(
        body,
        name="inproj_bwd_dw",
        grid=(N_CHIPS, T // tm),
        in_specs=[
            pl.BlockSpec((tm, D_MODEL), lambda j, t: (t, 0)),
            pl.BlockSpec((tm, SHARD_COLS), lambda j, t: (t, j)),
        ],
        out_specs=pl.BlockSpec((None, D_MODEL, SHARD_COLS), lambda j, t: (j, 0, 0)),
        out_shape=jax.ShapeDtypeStruct((N_CHIPS, D_MODEL, SHARD_COLS), F32),
        compiler_params=_cparams(2),
    )(h, dz)


def _dx_tile(T):
    return min(512, max(T // 4, 8))


def _inproj_bwd_dx(dz, w4, x, g, dxo, rider=None, tiles=None, prev=None):
    T = x.shape[0]
    tm = _dx_tile(T)
    t0, ni = tiles if tiles else (0, T // tm)
    nk = N_CHIPS
    nt = (((1,), (1,)), ((), ()))
    n_in = 5 if prev is None else 6

    def body(dz_ref, w_ref, x_ref, g_ref, dxo_ref, *rest):
        dx_ref, dg_ref, acc_ref = rest[-3:]
        i, j = pl.program_id(0), pl.program_id(1)
        lo = lax.dot_general(dz_ref[...], w_ref[0], nt, preferred_element_type=F32)
        hi = lax.dot_general(dz_ref[...], w_ref[1], nt, preferred_element_type=F32)

        @pl.when(j == 0)
        def _():
            acc_ref[:, 0:HALF_D] = lo
            acc_ref[:, HALF_D:D_MODEL] = hi

        @pl.when(j > 0)
        def _():
            acc_ref[:, 0:HALF_D] += lo
            acc_ref[:, HALF_D:D_MODEL] += hi

        @pl.when(j == nk - 1)
        def _():
            @pl.when(i == 0)
            def _():
                dg_ref[...] = jnp.zeros_like(dg_ref)

            rc = min(128, tm)
            for c in range(tm // rc):
                rows = slice(c * rc, (c + 1) * rc)
                dh = acc_ref[rows, :]
                xv = x_ref[rows, :]
                r = lax.rsqrt(jnp.mean(xv * xv, axis=-1, keepdims=True) + RMS_EPS)
                xh = xv * r
                w = dh * g_ref[...]
                dx_ref[rows, :] = dxo_ref[rows, :] + r * (w - xh * jnp.mean(w * xh, axis=-1, keepdims=True))
                dg_ref[...] += jnp.sum(dh * xh, axis=0, keepdims=True)

    r_in, r_out, r_shapes, r_sems, r_args = _rider_specs(rider)
    return pl.pallas_call(
        _ride(body, n_in, 2, rider, lambda: (pl.program_id(0) == 0) & (pl.program_id(1) == 0),
              lambda: (pl.program_id(0) == ni - 1) & (pl.program_id(1) == nk - 1)),
        name="inproj_bwd_dx" + ("" if rider is None else "_ride") + ("" if prev is None else "_rest"),
        grid=(ni, nk),
        in_specs=[
            pl.BlockSpec((tm, SHARD_COLS), lambda i, j: (i + t0, j)),
            pl.BlockSpec((2, None, HALF_D, SHARD_COLS), lambda i, j: (0, j, 0, 0)),
            pl.BlockSpec((tm, D_MODEL), lambda i, j: (i + t0, 0)),
            pl.BlockSpec((1, D_MODEL), lambda i, j: (0, 0)),
            pl.BlockSpec((tm, D_MODEL), lambda i, j: (i + t0, 0)),
        ] + ([] if prev is None else [pl.BlockSpec(memory_space=pl.ANY)]) + r_in,
        out_specs=[
            pl.BlockSpec((tm, D_MODEL), lambda i, j: (i + t0, 0)),
            pl.BlockSpec((1, D_MODEL), lambda i, j: (0, 0)),
        ] + r_out,
        out_shape=[jax.ShapeDtypeStruct((T, D_MODEL), F32), jax.ShapeDtypeStruct((1, D_MODEL), F32)] + r_shapes,
        scratch_shapes=[pltpu.VMEM((tm, D_MODEL), F32)] + r_sems,
        input_output_aliases={} if prev is None else {5: 0},
        compiler_params=_cparams(2),
    )(dz, w4, x, g, dxo, *([] if prev is None else [prev]), *r_args)


def _loss_head(x, g, tgt):
    T = x.shape[0]
    tm = min(512, T)

    def body(x_ref, g_ref, t_ref, dx_ref, l_ref, dg_ref):
        i = pl.program_id(0)
        xv = x_ref[...]
        r = lax.rsqrt(jnp.mean(xv * xv, axis=-1, keepdims=True) + RMS_EPS)
        xh = xv * r
        err = xh * g_ref[...] - t_ref[...]
        lpart = 0.5 * jnp.sum(jnp.mean(err * err, axis=-1, keepdims=True), axis=0, keepdims=True)
        dout = err * (1.0 / D_MODEL)
        w = dout * g_ref[...]
        dx_ref[...] = r * (w - xh * jnp.mean(w * xh, axis=-1, keepdims=True))
        gpart = jnp.sum(dout * xh, axis=0, keepdims=True)

        @pl.when(i == 0)
        def _():
            l_ref[...] = jnp.broadcast_to(lpart, l_ref.shape)
            dg_ref[...] = gpart

        @pl.when(i > 0)
        def _():
            l_ref[...] += jnp.broadcast_to(lpart, l_ref.shape)
            dg_ref[...] += gpart

    return pl.pallas_call(
        body,
        name="loss_head",
        grid=(T // tm,),
        in_specs=[
            pl.BlockSpec((tm, D_MODEL), lambda i: (i, 0)),
            pl.BlockSpec((1, D_MODEL), lambda i: (0, 0)),
            pl.BlockSpec((tm, D_MODEL), lambda i: (i, 0)),
        ],
        out_specs=[
            pl.BlockSpec((tm, D_MODEL), lambda i: (i, 0)),
            pl.BlockSpec((1, 128), lambda i: (0, 0)),
            pl.BlockSpec((1, D_MODEL), lambda i: (0, 0)),
        ],
        out_shape=[
            jax.ShapeDtypeStruct((T, D_MODEL), F32),
            jax.ShapeDtypeStruct((1, 128), F32),
            jax.ShapeDtypeStruct((1, D_MODEL), F32),
        ],
        compiler_params=_cparams(1),
    )(x, g, tgt)


def _s5_prep(lam_re, lam_im, b_re, b_im, log_dt):
    lam = lax.complex(lam_re, lam_im)
    dt = jnp.exp(log_dt)[:, None]
    a = jnp.exp(lam * dt)
    bbar = ((a - 1.0) / lam)[..., None] * lax.complex(b_re, b_im)
    return jnp.real(a), jnp.imag(a), jnp.real(bbar), jnp.imag(bbar)


def _block_diag_in(m):
    m4 = m.reshape(SUPER, 8, S5_STATE, S5_CH)
    eye = jnp.eye(8, dtype=m.dtype)
    out = jnp.einsum("jgph,gk->jghkp", m4, eye)
    return out.reshape(SUPER, 8 * S5_CH, 8 * S5_STATE)


def _block_diag_in_grad(d):
    d6 = d.reshape(SUPER, 8, S5_CH, 8, S5_STATE)
    diag = jnp.einsum("jghgp->jgph", d6)
    return diag.reshape(S5_GROUPS, S5_STATE, S5_CH)


def _block_diag_out(m):
    m4 = m.reshape(SUPER, 8, S5_CH, S5_STATE)
    eye = jnp.eye(8, dtype=m.dtype)
    out = jnp.einsum("jghp,gk->jgpkh", m4, eye)
    return out.reshape(SUPER, 8 * S5_STATE, 8 * S5_CH)


def _block_diag_out_grad(d):
    d6 = d.reshape(SUPER, 8, S5_STATE, 8, S5_CH)
    diag = jnp.einsum("jgpgh->jghp", d6)
    return diag.reshape(S5_GROUPS, S5_CH, S5_STATE)


def _scan_coefs(a_re, a_im, reverse):
    a = lax.complex(a_re.reshape(-1), a_im.reshape(-1))
    if reverse:
        a = jnp.conj(a)
    pw = [a]
    for _ in range(7):
        pw.append(pw[-1] * a)
    rows = jnp.arange(8)

    def masked(k):
        m = (rows + k <= 7) if reverse else (rows >= k)
        return jnp.where(m[:, None], pw[k - 1][None, :], 0.0)

    a1, a2, a4 = masked(1), masked(2), masked(4)
    carry = jnp.stack([pw[7 - r] for r in range(8)]) if reverse else jnp.stack(pw)
    parts = []
    for c in (a1, a2, a4, carry):
        parts += [jnp.real(c), jnp.imag(c)]
    return jnp.stack(parts).astype(F32)


def _scan_block(r, im, coef_ref, cs, reverse):
    for k, idx in ((1, 0), (2, 2), (4, 4)):
        ar = coef_ref[idx, :, cs]
        ai = coef_ref[idx + 1, :, cs]
        sh = 8 - k if reverse else k
        rr = pltpu.roll(r, sh, 0)
        ri = pltpu.roll(im, sh, 0)
        r, im = r + ar * rr - ai * ri, im + ar * ri + ai * rr
    return r, im


def _s5_fwd(z, p):
    T = z.shape[0]
    tm = min(256, T)
    nblk = tm // 8
    W = STATE_W

    def body(xa_ref, ga_ref, bre_ref, bim_ref, cre_ref, cim_ref, dv_ref, wg_ref, bg_ref, coef_ref,
             ya_ref, yraw_ref, sre_ref, sim_ref, wre, wim):
        @pl.when(pl.program_id(0) == 0)
        def _():
            wre[0:8, :] = jnp.zeros((8, W), F32)
            wim[0:8, :] = jnp.zeros((8, W), F32)

        xa = xa_ref[...]
        xab = xa.astype(BF16)
        for j in range(SUPER):
            xj = xab[:, j * 128:(j + 1) * 128]
            wre[8:8 + tm, j * 512:(j + 1) * 512] = jnp.dot(xj, bre_ref[j], preferred_element_type=F32)
            wim[8:8 + tm, j * 512:(j + 1) * 512] = jnp.dot(xj, bim_ref[j], preferred_element_type=F32)

        def blk(b, carry):
            base = pl.multiple_of(8 + b * 8, 8)
            for cc in range(W // SCAN_COLS):
                cs = pl.ds(cc * SCAN_COLS, SCAN_COLS)
                r, im = _scan_block(wre[pl.ds(base, 8), cs], wim[pl.ds(base, 8), cs], coef_ref, cs, False)
                cr = wre[pl.ds(base - 1, 1), cs]
                ci = wim[pl.ds(base - 1, 1), cs]
                pr = coef_ref[6, :, cs]
                pi = coef_ref[7, :, cs]
                wre[pl.ds(base, 8), cs] = r + pr * cr - pi * ci
                wim[pl.ds(base, 8), cs] = im + pr * ci + pi * cr
            return carry

        lax.fori_loop(0, nblk, blk, 0)
        wre[0:8, :] = wre[tm:tm + 8, :]
        wim[0:8, :] = wim[tm:tm + 8, :]
        sre_ref[...] = wre[8:8 + tm, :]
        sim_ref[...] = wim[8:8 + tm, :]

        for j in range(SUPER):
            yr = jnp.dot(wre[8:8 + tm, j * 512:(j + 1) * 512].astype(BF16), cre_ref[j], preferred_element_type=F32)
            yr += jnp.dot(wim[8:8 + tm, j * 512:(j + 1) * 512].astype(BF16), cim_ref[j], preferred_element_type=F32)
            yraw_ref[:, j * 128:(j + 1) * 128] = yr
        yraw = yraw_ref[...] + dv_ref[...] * xa
        yraw_ref[...] = yraw
        yg = _gelu(yraw)
        q = jnp.dot(yg.astype(BF16), wg_ref[...], preferred_element_type=F32) + bg_ref[...]
        sga, _ = _silu_and_grad(ga_ref[...])
        ya_ref[...] = (yg * jax.nn.sigmoid(q) * sga).astype(BF16)

    return pl.pallas_call(
        body,
        name="s5_fwd",
        grid=(T // tm,),
        in_specs=[
            pl.BlockSpec((tm, S5_W), lambda i: (i, 0)),
            pl.BlockSpec((tm, S5_W), lambda i: (i, 6)),
            _full((SUPER, 128, 512)), _full((SUPER, 128, 512)),
            _full((SUPER, 512, 128)), _full((SUPER, 512, 128)),
            _full((1, S5_W)), _full((S5_W, S5_W)), _full((1, S5_W)),
            _full((8, 8, W)),
        ],
        out_specs=[
            pl.BlockSpec((tm, S5_W), lambda i: (i, 0)),
            pl.BlockSpec((tm, S5_W), lambda i: (i, 0)),
            pl.BlockSpec((tm, W), lambda i: (i, 0)),
            pl.BlockSpec((tm, W), lambda i: (i, 0)),
        ],
        out_shape=[
            jax.ShapeDtypeStruct((T, S5_W), BF16),
            jax.ShapeDtypeStruct((T, S5_W), F32),
            jax.ShapeDtypeStruct((T, W), F32),
            jax.ShapeDtypeStruct((T, W), F32),
        ],
        scratch_shapes=[pltpu.VMEM((tm + 8, W), F32), pltpu.VMEM((tm + 8, W), F32)],
        compiler_params=_cparams(1),
    )(z, z, p["b4re"], p["b4im"], p["c4re"], p["c4im"], p["dvec"], p["wglu"], p["bglu"], p["coef_f"])


def _s5_bwd(dy, z, yraw, sre, sim, p, rider=None):
    T = z.shape[0]
    tm = min(256, T)
    nt = T // tm
    nblk = tm // 8
    W = STATE_W
    rev = lambda i: nt - 1 - i

    def body(dya_ref, xa_ref, ga_ref, yraw_ref, sre_ref, sim_ref, hre_ref, him_ref,
             bre_t_ref, bim_t_ref, cre_t_ref, cim_t_ref, dv_ref, wg_ref, wgt_ref, bg_ref, coef_ref,
             dz_ref, dbre_ref, dbim_ref, dcre_ref, dcim_ref, dd_ref, dwg_ref, dbg_ref, da_ref,
             wre, wim, dyr_ref, xa_stage, ga_stage, xa_sem, ga_sem):
        i = pl.program_id(0)
        xa_out = _ColumnWriter(xa_stage, xa_sem, dz_ref, 0, i, nt)
        ga_out = _ColumnWriter(ga_stage, ga_sem, dz_ref, 6 * 512, i, nt)
        dxa_ref, dga_ref = xa_out.slot(), ga_out.slot()

        @pl.when(i == 0)
        def _():
            wre[tm:tm + 8, :] = jnp.zeros((8, W), F32)
            wim[tm:tm + 8, :] = jnp.zeros((8, W), F32)
            for ref in (dbre_ref, dbim_ref, dcre_ref, dcim_ref, dd_ref, dwg_ref, dbg_ref, da_ref):
                ref[...] = jnp.zeros_like(ref)

        xa = xa_ref[...]
        dya = dya_ref[...]
        yg, dgelu = _gelu_and_grad(yraw_ref[...])
        ygb = yg.astype(BF16)
        q = jnp.dot(ygb, wg_ref[...], preferred_element_type=F32) + bg_ref[...]
        sq = jax.nn.sigmoid(q)
        sga, dsga = _silu_and_grad(ga_ref[...])
        dga_ref[...] = (dya * (yg * sq) * dsga).astype(BF16)
        dya0 = dya * sga
        dq = dya0 * yg * sq * (1.0 - sq)
        dqb = dq.astype(BF16)
        dyg = dya0 * sq + jnp.dot(dqb, wgt_ref[...], preferred_element_type=F32)
        dwg_ref[...] += _dot_tn(ygb, dqb)
        dbg_ref[...] += jnp.sum(dq, axis=0, keepdims=True)
        dyraw = dyg * dgelu
        dd_ref[...] += jnp.sum(dyraw * xa, axis=0, keepdims=True)
        dyr_ref[...] = dyraw.astype(BF16)

        for j in range(SUPER):
            dj = dyr_ref[:, j * 128:(j + 1) * 128]
            wre[0:tm, j * 512:(j + 1) * 512] = jnp.dot(dj, cre_t_ref[j], preferred_element_type=F32)
            wim[0:tm, j * 512:(j + 1) * 512] = jnp.dot(dj, cim_t_ref[j], preferred_element_type=F32)

        row0 = lax.broadcasted_iota(jnp.int32, (8, SCAN_COLS), 0) == 0
        head_on = (i < nt - 1).astype(F32)

        def one_block(base, first):
            for cc in range(W // SCAN_COLS):
                cs = pl.ds(cc * SCAN_COLS, SCAN_COLS)
                r, im = _scan_block(wre[pl.ds(base, 8), cs], wim[pl.ds(base, 8), cs], coef_ref, cs, True)
                cr = wre[pl.ds(base + 8, 1), cs]
                ci = wim[pl.ds(base + 8, 1), cs]
                pr = coef_ref[6, :, cs]
                pi = coef_ref[7, :, cs]
                r, im = r + pr * cr - pi * ci, im + pr * ci + pi * cr
                wre[pl.ds(base, 8), cs] = r
                wim[pl.ds(base, 8), cs] = im
                if first:
                    pre = hre_ref[7:8, cs] * head_on
                    pim = him_ref[7:8, cs] * head_on
                else:
                    pre = sre_ref[pl.ds(base - 1, 1), cs]
                    pim = sim_ref[pl.ds(base - 1, 1), cs]
                spr = jnp.where(row0, pre, pltpu.roll(sre_ref[pl.ds(base, 8), cs], 1, 0))
                spi = jnp.where(row0, pim, pltpu.roll(sim_ref[pl.ds(base, 8), cs], 1, 0))
                da_ref[0, :, cs] += r * spr + im * spi
                da_ref[1, :, cs] += im * spr - r * spi

        def blk(b, carry):
            one_block(pl.multiple_of((nblk - 1 - b) * 8, 8), False)
            return carry

        lax.fori_loop(0, nblk - 1, blk, 0)
        one_block(0, True)
        wre[tm:tm + 8, :] = wre[0:8, :]
        wim[tm:tm + 8, :] = wim[0:8, :]

        xab = xa.astype(BF16)
        for j in range(SUPER):
            cols = slice(j * 512, (j + 1) * 512)
            gre = wre[0:tm, cols].astype(BF16)
            gim = wim[0:tm, cols].astype(BF16)
            xj = xab[:, j * 128:(j + 1) * 128]
            dj = dyr_ref[:, j * 128:(j + 1) * 128]
            dbre_ref[j] += _dot_tn(xj, gre)
            dbim_ref[j] += _dot_tn(xj, gim)
            dcre_ref[j] += _dot_tn(sre_ref[:, cols], dj)
            dcim_ref[j] += _dot_tn(sim_ref[:, cols], dj)
            dxj = jnp.dot(gre, bre_t_ref[j], preferred_element_type=F32)
            dxj += jnp.dot(gim, bim_t_ref[j], preferred_element_type=F32)
            dxj += dyraw[:, j * 128:(j + 1) * 128] * dv_ref[:, j * 128:(j + 1) * 128]
            dxa_ref[:, j * 128:(j + 1) * 128] = dxj.astype(BF16)
        xa_out.send(rev(i) * tm)
        ga_out.send(rev(i) * tm)

    acc = lambda shape: _full(shape)
    hb = tm // 8
    r_in, r_out, r_shapes, r_sems, r_args = _rider_specs(rider)
    stages, stage_sems = _stage_scratch(tm, (S5_W, S5_W))
    return pl.pallas_call(
        _ride(body, 17, 9, rider, lambda: pl.program_id(0) == 0, lambda: pl.program_id(0) == nt - 1),
        name="s5_bwd" + ("" if rider is None else "_ride"),
        grid=(nt,),
        in_specs=[
            pl.BlockSpec((tm, S5_W), lambda i: (rev(i), 0)),
            pl.BlockSpec((tm, S5_W), lambda i: (rev(i), 0)),
            pl.BlockSpec((tm, S5_W), lambda i: (rev(i), 6)),
            pl.BlockSpec((tm, S5_W), lambda i: (rev(i), 0)),
            pl.BlockSpec((tm, W), lambda i: (rev(i), 0)),
            pl.BlockSpec((tm, W), lambda i: (rev(i), 0)),
            pl.BlockSpec((8, W), lambda i: (jnp.maximum(rev(i) * hb - 1, 0), 0)),
            pl.BlockSpec((8, W), lambda i: (jnp.maximum(rev(i) * hb - 1, 0), 0)),
            _full((SUPER, 512, 128)), _full((SUPER, 512, 128)),
            _full((SUPER, 128, 512)), _full((SUPER, 128, 512)),
            _full((1, S5_W)), _full((S5_W, S5_W)), _full((S5_W, S5_W)), _full((1, S5_W)),
            _full((8, 8, W)),
        ] + r_in,
        out_specs=[
            pl.BlockSpec(memory_space=pl.ANY),
            acc((SUPER, 128, 512)), acc((SUPER, 128, 512)),
            acc((SUPER, 512, 128)), acc((SUPER, 512, 128)),
            acc((1, S5_W)), acc((S5_W, S5_W)), acc((1, S5_W)), acc((2, 8, W)),
        ] + r_out,
        out_shape=[
            jax.ShapeDtypeStruct((T, IN_COLS), BF16),
            jax.ShapeDtypeStruct((SUPER, 128, 512), F32), jax.ShapeDtypeStruct((SUPER, 128, 512), F32),
            jax.ShapeDtypeStruct((SUPER, 512, 128), F32), jax.ShapeDtypeStruct((SUPER, 512, 128), F32),
            jax.ShapeDtypeStruct((1, S5_W), F32), jax.ShapeDtypeStruct((S5_W, S5_W), F32),
            jax.ShapeDtypeStruct((1, S5_W), F32), jax.ShapeDtypeStruct((2, 8, W), F32),
        ] + r_shapes,
        scratch_shapes=[pltpu.VMEM((tm + 8, W), F32), pltpu.VMEM((tm + 8, W), F32), pltpu.VMEM((tm, S5_W), BF16)]
        + stages + stage_sems + r_sems,
        compiler_params=_cparams(1),
    )(dy, z, z, yraw, sre, sim, sre, sim,
      p["b4re_t"], p["b4im_t"], p["c4re_t"], p["c4im_t"], p["dvec"], p["wglu"], p["wglu_t"], p["bglu"], p["coef_r"],
      *r_args)


def _ln_fwd(vf, lng, lnb):
    mu = jnp.mean(vf, axis=-1, keepdims=True)
    d = vf - mu
    rstd = lax.rsqrt(jnp.mean(d * d, axis=-1, keepdims=True) + LN_EPS)
    xh = d * rstd
    return xh, rstd, xh * lng + lnb


def _col_block(tm, b):
    return pl.BlockSpec((tm, 512), lambda i: (i, b))


def _ln_halves(vf0, vf1):
    mu = (jnp.sum(vf0, axis=-1, keepdims=True) + jnp.sum(vf1, axis=-1, keepdims=True)) * (1.0 / SGU_W)
    d0, d1 = vf0 - mu, vf1 - mu
    var = (jnp.sum(d0 * d0, axis=-1, keepdims=True) + jnp.sum(d1 * d1, axis=-1, keepdims=True)) * (1.0 / SGU_W)
    rstd = lax.rsqrt(var + LN_EPS)
    return d0 * rstd, d1 * rstd, rstd


def _sgu_fwd(z, ws, bsf, lng, lnb):
    T = z.shape[0]
    tm = min(512, T)

    def body(u0, u1, v0, v1, g0, g1, ws_ref, bs_ref, lng_ref, lnb_ref, yb_ref, vn_ref):
        for c in range(tm // CHUNK):
            rows = slice(c * CHUNK, (c + 1) * CHUNK)
            xh0, xh1, _ = _ln_halves(_gelu(v0[rows, :]), _gelu(v1[rows, :]))
            vn_ref[:, 0:512] = (xh0 * lng_ref[:, 0:512] + lnb_ref[:, 0:512]).astype(BF16)
            vn_ref[:, 512:1024] = (xh1 * lng_ref[:, 512:1024] + lnb_ref[:, 512:1024]).astype(BF16)
            for half, (u_ref, g_ref) in enumerate(((u0, g0), (u1, g1))):
                sg, _ = _silu_and_grad(g_ref[rows, :])
                m = _gelu(u_ref[rows, :]) * sg
                for hh in range(SGU_HEADS // 2):
                    h = half * (SGU_HEADS // 2) + hh
                    cols = slice(h * 128, (h + 1) * 128)
                    s = jnp.dot(ws_ref[h], vn_ref[:, cols], preferred_element_type=F32) + bs_ref[:, cols]
                    yb_ref[rows, cols] = (m[:, hh * 128:(hh + 1) * 128] * s).astype(BF16)

    return pl.pallas_call(
        body,
        name="sgu_fwd",
        grid=(T // tm,),
        in_specs=[_col_block(tm, b) for b in (1, 2, 3, 4, 7, 8)] + [
            _full((SGU_HEADS, CHUNK, CHUNK)), _full((CHUNK, SGU_W)), _full((1, SGU_W)), _full((1, SGU_W)),
        ],
        out_specs=pl.BlockSpec((tm, SGU_W), lambda i: (i, 0)),
        out_shape=jax.ShapeDtypeStruct((T, SGU_W), BF16),
        scratch_shapes=[pltpu.VMEM((CHUNK, SGU_W), BF16)],
        compiler_params=_cparams(1),
    )(z, z, z, z, z, z, ws, bsf, lng, lnb)


def _sgu_bwd(dy, z, dz, ws, ws_t, bsf, lng, lnb, rider=None):
    T = z.shape[0]
    tm = min(512, T)
    HH = SGU_HEADS // 2

    def body(u0, u1, v0, v1, g0, g1, dy0, dy1, ws_ref, wst_ref, bs_ref, lng_ref, lnb_ref, dz_in,
             dz_ref, dws_ref, dbs_ref, dlng_ref, dlnb_ref, vn_ref, dvn_ref, *stage):
        step = pl.program_id(0)
        outs = [_ColumnWriter(stage[k], stage[3 + k], dz_ref, col, step, T // tm)
                for k, col in enumerate((512, 1536, 3584))]
        du_ref, dv_ref, dgb_ref = (o.slot() for o in outs)

        @pl.when(step == 0)
        def _():
            for ref in (dws_ref, dbs_ref, dlng_ref, dlnb_ref):
                ref[...] = jnp.zeros_like(ref)

        for c in range(tm // CHUNK):
            rows = slice(c * CHUNK, (c + 1) * CHUNK)
            vf0, dgv0 = _gelu_and_grad(v0[rows, :])
            vf1, dgv1 = _gelu_and_grad(v1[rows, :])
            xh0, xh1, rstd = _ln_halves(vf0, vf1)
            vn_ref[:, 0:512] = (xh0 * lng_ref[:, 0:512] + lnb_ref[:, 0:512]).astype(BF16)
            vn_ref[:, 512:1024] = (xh1 * lng_ref[:, 512:1024] + lnb_ref[:, 512:1024]).astype(BF16)
            for half, (u_ref, g_ref, dy_ref) in enumerate(((u0, g0, dy0), (u1, g1, dy1))):
                ug, dgu = _gelu_and_grad(u_ref[rows, :])
                sg, dsg = _silu_and_grad(g_ref[rows, :])
                dyb = dy_ref[rows, :]
                dyb0 = dyb * sg
                ds_half = dyb0 * ug
                du_scale = dyb0 * dgu
                dg_scale = dyb * ug * dsg
                for hh in range(HH):
                    h = half * HH + hh
                    cols = slice(h * 128, (h + 1) * 128)
                    lc = slice(hh * 128, (hh + 1) * 128)
                    s = jnp.dot(ws_ref[h], vn_ref[:, cols], preferred_element_type=F32) + bs_ref[:, cols]
                    du_ref[rows, cols] = (du_scale[:, lc] * s).astype(BF16)
                    dgb_ref[rows, cols] = (dg_scale[:, lc] * s).astype(BF16)
                    ds = ds_half[:, lc]
                    dbs_ref[:, cols] += ds
                    dsb = ds.astype(BF16)
                    dws_ref[h] += _dot_nt(dsb, vn_ref[:, cols])
                    dvn_ref[:, cols] = jnp.dot(wst_ref[h], dsb, preferred_element_type=F32)
            dvn0 = dvn_ref[:, 0:512]
            dvn1 = dvn_ref[:, 512:1024]
            dlnb_ref[:, 0:512] += jnp.sum(dvn0, axis=0, keepdims=True)
            dlnb_ref[:, 512:1024] += jnp.sum(dvn1, axis=0, keepdims=True)
            dlng_ref[:, 0:512] += jnp.sum(dvn0 * xh0, axis=0, keepdims=True)
            dlng_ref[:, 512:1024] += jnp.sum(dvn1 * xh1, axis=0, keepdims=True)
            dxh0 = dvn0 * lng_ref[:, 0:512]
            dxh1 = dvn1 * lng_ref[:, 512:1024]
            m1 = (jnp.sum(dxh0, axis=-1, keepdims=True) + jnp.sum(dxh1, axis=-1, keepdims=True)) * (1.0 / SGU_W)
            m2 = (jnp.sum(dxh0 * xh0, axis=-1, keepdims=True) + jnp.sum(dxh1 * xh1, axis=-1, keepdims=True)) * (1.0 / SGU_W)
            dv_ref[rows, 0:512] = (rstd * (dxh0 - m1 - xh0 * m2) * dgv0).astype(BF16)
            dv_ref[rows, 512:1024] = (rstd * (dxh1 - m1 - xh1 * m2) * dgv1).astype(BF16)
        for o in outs:
            o.send(step * tm)

    anyspec = pl.BlockSpec(memory_space=pl.ANY)
    r_in, r_out, r_shapes, r_sems, r_args = _rider_specs(rider)
    stages, stage_sems = _stage_scratch(tm, (SGU_W, SGU_W, SGU_W))
    return pl.pallas_call(
        _ride(body, 14, 5, rider, lambda: pl.program_id(0) == 0, lambda: pl.program_id(0) == T // tm - 1),
        name="sgu_bwd" + ("" if rider is None else "_ride"),
        grid=(T // tm,),
        in_specs=[_col_block(tm, b) for b in (1, 2, 3, 4, 7, 8)] + [_col_block(tm, 1), _col_block(tm, 2)] + [
            _full((SGU_HEADS, CHUNK, CHUNK)), _full((SGU_HEADS, CHUNK, CHUNK)),
            _full((CHUNK, SGU_W)), _full((1, SGU_W)), _full((1, SGU_W)), anyspec,
        ] + r_in,
        out_specs=[anyspec,
                   _full((SGU_HEADS, CHUNK, CHUNK)), _full((CHUNK, SGU_W)), _full((1, SGU_W)), _full((1, SGU_W))] + r_out,
        input_output_aliases={13: 0},
        out_shape=[
            jax.ShapeDtypeStruct((T, IN_COLS), BF16),
            jax.ShapeDtypeStruct((SGU_HEADS, CHUNK, CHUNK), F32), jax.ShapeDtypeStruct((CHUNK, SGU_W), F32),
            jax.ShapeDtypeStruct((1, SGU_W), F32), jax.ShapeDtypeStruct((1, SGU_W), F32),
        ] + r_shapes,
        scratch_shapes=[pltpu.VMEM((CHUNK, SGU_W), BF16), pltpu.VMEM((CHUNK, SGU_W), F32)] + stages + stage_sems + r_sems,
        compiler_params=_cparams(1),
    )(z, z, z, z, z, z, dy, dy, ws, ws_t, bsf, lng, lnb, dz, *r_args)


def _pool_den(first_row, n):
    return (lax.broadcasted_iota(jnp.int32, (n, 1), 0) + first_row + 1).astype(F32)


def _pool_p(ext, xc, pos, tm):
    w2 = ext + pltpu.roll(ext, 1, 0)
    w4 = w2 + pltpu.roll(w2, 2, 0)
    w8 = w4 + pltpu.roll(w4, 4, 0)
    w16 = w8 + pltpu.roll(w8, 8, 0)
    out = []
    for g, (w, ws) in enumerate(zip(POOL_WINDOWS, (w2, w4, w8, w16))):
        cols = slice(g * 128, (g + 1) * 128)
        mean = ws[POOL_HALO:POOL_HALO + tm, cols] / jnp.minimum(pos, float(w))
        out.append(mean - xc[:, cols])
    return out


def _pool_fwd(z, wp, scale):
    T = z.shape[0]
    tm = min(512, T)
    hb = tm // POOL_HALO

    def body(xc_ref, hx_ref, gc_ref, wp_ref, sc_ref, yc_ref):
        i = pl.program_id(0)
        xc = xc_ref[...]
        halo = hx_ref[...] * (i > 0).astype(F32)
        ext = jnp.concatenate([halo, xc], axis=0)
        ps = _pool_p(ext, xc, _pool_den(i * tm, tm), tm)
        sg, _ = _silu_and_grad(gc_ref[...])
        for g in range(4):
            cols = slice(g * 128, (g + 1) * 128)
            pw = _dot(ps[g], wp_ref[g])
            yc_ref[:, cols] = (pw * sc_ref[:, cols] * sg[:, cols]).astype(BF16)

    return pl.pallas_call(
        body,
        name="pool_fwd",
        grid=(T // tm,),
        in_specs=[
            _col_block(tm, 5),
            pl.BlockSpec((POOL_HALO, 512), lambda i: (jnp.maximum(i * hb - 1, 0), 5)),
            _col_block(tm, 9),
            _full((4, 128, 128)), _full((1, POOL_W)),
        ],
        out_specs=pl.BlockSpec((tm, POOL_W), lambda i: (i, 0)),
        out_shape=jax.ShapeDtypeStruct((T, POOL_W), BF16),
        compiler_params=_cparams(1),
    )(z, z, z, wp, scale)


def _pool_bwd(dy, z, dz, wp, wp_t, scale):
    T = z.shape[0]
    tm = min(512, T)
    nt = T // tm
    hb = tm // POOL_HALO
    last_hb = T // POOL_HALO - 1
    L = tm + POOL_HALO

    def body(xc_ref, hx_ref, gc_ref, gn_ref, dyc_ref, dyn_ref, wp_ref, wpt_ref, sc_ref, dz_in,
             dz_ref, dwp_ref, dsc_ref, xc_stage, gc_stage, xc_sem, gc_sem):
        i = pl.program_id(0)
        xc_out = _ColumnWriter(xc_stage, xc_sem, dz_ref, 5 * 512, i, nt)
        gc_out = _ColumnWriter(gc_stage, gc_sem, dz_ref, 9 * 512, i, nt)
        dxc_ref, dgc_ref = xc_out.slot(), gc_out.slot()

        @pl.when(i == 0)
        def _():
            dwp_ref[...] = jnp.zeros_like(dwp_ref)
            dsc_ref[...] = jnp.zeros_like(dsc_ref)

        xc = xc_ref[...]
        halo = hx_ref[...] * (i > 0).astype(F32)
        pos = _pool_den(i * tm, tm)
        ps = _pool_p(jnp.concatenate([halo, xc], axis=0), xc, pos, tm)
        sg, dsg = _silu_and_grad(gc_ref[...])
        dyc = dyc_ref[...]
        dyc0 = dyc * sg
        dpw = dyc0 * sc_ref[...]
        sgn, _ = _silu_and_grad(gn_ref[...])
        dpwn = dyn_ref[...] * sgn * sc_ref[...] * (i < nt - 1).astype(F32)
        posn = _pool_den((i + 1) * tm, POOL_HALO)
        dps, qs = [], []
        for g, w in enumerate(POOL_WINDOWS):
            cols = slice(g * 128, (g + 1) * 128)
            pw = _dot(ps[g], wp_ref[g])
            dgc_ref[:, cols] = (dyc[:, cols] * pw * sc_ref[:, cols] * dsg[:, cols]).astype(BF16)
            dsc_ref[:, cols] += jnp.sum(dyc0[:, cols] * pw, axis=0, keepdims=True)
            dwp_ref[g] += _dot_tn(ps[g], dpw[:, cols])
            dp = _dot(dpw[:, cols], wpt_ref[g])
            dpn = _dot(dpwn[:, cols], wpt_ref[g])
            dps.append(dp)
            qs.append(jnp.concatenate([dp / jnp.minimum(pos, float(w)), dpn / jnp.minimum(posn, float(w))], axis=0))
        ext = jnp.concatenate(qs, axis=1)
        f2 = ext + pltpu.roll(ext, L - 1, 0)
        f4 = f2 + pltpu.roll(f2, L - 2, 0)
        f8 = f4 + pltpu.roll(f4, L - 4, 0)
        f16 = f8 + pltpu.roll(f8, L - 8, 0)
        for g, f in enumerate((f2, f4, f8, f16)):
            cols = slice(g * 128, (g + 1) * 128)
            dxc_ref[:, cols] = (f[0:tm, cols] - dps[g]).astype(BF16)
        xc_out.send(i * tm)
        gc_out.send(i * tm)

    nxt = lambda i: jnp.minimum((i + 1) * hb, last_hb)
    anyspec = pl.BlockSpec(memory_space=pl.ANY)
    stages, stage_sems = _stage_scratch(tm, (POOL_W, POOL_W))
    return pl.pallas_call(
        body,
        name="pool_bwd",
        grid=(nt,),
        in_specs=[
            _col_block(tm, 5),
            pl.BlockSpec((POOL_HALO, 512), lambda i: (jnp.maximum(i * hb - 1, 0), 5)),
            _col_block(tm, 9),
            pl.BlockSpec((POOL_HALO, 512), lambda i: (nxt(i), 9)),
            _col_block(tm, 3),
            pl.BlockSpec((POOL_HALO, 512), lambda i: (nxt(i), 3)),
            _full((4, 128, 128)), _full((4, 128, 128)), _full((1, POOL_W)), anyspec,
        ],
        out_specs=[anyspec, _full((4, 128, 128)), _full((1, POOL_W))],
        input_output_aliases={9: 0},
        out_shape=[
            jax.ShapeDtypeStruct((T, IN_COLS), BF16),
            jax.ShapeDtypeStruct((4, 128, 128), F32), jax.ShapeDtypeStruct((1, POOL_W), F32),
        ],
        scratch_shapes=stages + stage_sems,
        compiler_params=_cparams(1),
    )(z, z, z, z, dy, dy, wp, wp_t, scale, dz)


def _row_tile(rows, cols):
    tr = 8
    while tr * 2 * cols * 4 <= 2 * 1024 * 1024 and rows % (tr * 2) == 0:
        tr *= 2
    return tr


def _add_own_half(part, recv, cidx):
    _, _, R2, C = part.shape
    tr = _row_tile(R2, C)

    def body(c_ref, a_ref, r_ref, o_ref):
        o_ref[...] = (a_ref[...] + r_ref[...]).astype(BF16)

    return pl.pallas_call(
        body,
        name="add_own_half",
        grid_spec=pltpu.PrefetchScalarGridSpec(
            num_scalar_prefetch=1,
            grid=(N_CHIPS, R2 // tr),
            in_specs=[
                pl.BlockSpec((None, None, tr, C), lambda j, i, c: (j, c[0], i, 0)),
                pl.BlockSpec((None, tr, C), lambda j, i, c: (j, i, 0)),
            ],
            out_specs=pl.BlockSpec((None, tr, C), lambda j, i, c: (j, i, 0)),
        ),
        out_shape=jax.ShapeDtypeStruct((N_CHIPS, R2, C), BF16),
        compiler_params=_cparams(2),
    )(cidx, part, recv)


def _add2(a, b):
    R, C = a.shape
    tr = _row_tile(R, C)

    def body(a_ref, b_ref, o_ref):
        o_ref[...] = a_ref[...] + b_ref[...]

    spec = pl.BlockSpec((tr, C), lambda i: (i, 0))
    return pl.pallas_call(
        body, name="add2", grid=(R // tr,), in_specs=[spec, spec], out_specs=spec,
        out_shape=jax.ShapeDtypeStruct((R, C), F32), compiler_params=_cparams(1),
    )(a, b)


def _sum_chips(parts):
    _, R, C = parts.shape
    tr = _row_tile(R, N_CHIPS * C)

    def body(p_ref, o_ref):
        p = [p_ref[j].astype(F32) for j in range(N_CHIPS)]
        o_ref[...] = ((p[0] + p[1]) + p[2]) + p[3]

    return pl.pallas_call(
        body, name="sum_chips", grid=(R // tr,),
        in_specs=[pl.BlockSpec((N_CHIPS, tr, C), lambda i: (0, i, 0))],
        out_specs=pl.BlockSpec((tr, C), lambda i: (i, 0)),
        out_shape=jax.ShapeDtypeStruct((R, C), F32), compiler_params=_cparams(1),
    )(parts)


def _adamw_math(w, g, m, v):
    m = ADAM_B1 * m + (1.0 - ADAM_B1) * g
    v = ADAM_B2 * v + (1.0 - ADAM_B2) * (g * g)
    m_hat = m / (1.0 - ADAM_B1 ** ADAM_STEP)
    v_hat = v / (1.0 - ADAM_B2 ** ADAM_STEP)
    delta = -ADAM_LR * (m_hat / (jnp.sqrt(v_hat) + ADAM_EPS) + ADAM_WD * w)
    return delta, m, v


def _adamw(w, g, m, v):
    R, C = w.shape
    tr = _row_tile(R, C)

    def body(w_ref, g_ref, m_ref, v_ref, d_ref, mo_ref, vo_ref):
        d_ref[...], mo_ref[...], vo_ref[...] = _adamw_math(w_ref[...], g_ref[...], m_ref[...], v_ref[...])

    spec = pl.BlockSpec((tr, C), lambda i: (i, 0))
    shp = jax.ShapeDtypeStruct((R, C), F32)
    return pl.pallas_call(
        body, name="adamw", grid=(R // tr,), in_specs=[spec] * 4, out_specs=[spec] * 3,
        out_shape=[shp] * 3, compiler_params=_cparams(1),
    )(w, g, m, v)


def _adamw_halves(w, mine, theirs, m, v, cidx, rider=None):
    _, _, R2, C = w.shape
    tr = _row_tile(R2, C)
    nr = R2 // tr

    def body(c_ref, w_ref, a0_ref, b0_ref, a1_ref, b1_ref, m_ref, v_ref, g_ref, d_ref, mo_ref, vo_ref):
        own = pl.program_id(1) == c_ref[0]
        g0 = jnp.where(own, a0_ref[...], b0_ref[...])
        g1 = jnp.where(own, a1_ref[...], b1_ref[...])
        g = jnp.where(pl.program_id(0) == 0, g0, g1)
        g_ref[...] = g
        d_ref[...], mo_ref[...], vo_ref[...] = _adamw_math(w_ref[...], g, m_ref[...], v_ref[...])

    full = pl.BlockSpec((None, None, tr, C), lambda l, h, i, c: (l, h, i, 0))

    def pick(layer, mine_side):
        def index(l, h, i, c):
            used = (l == layer) & ((h == c[0]) == mine_side)
            return (jnp.where(used, i, 0), 0)
        return pl.BlockSpec((tr, C), index)

    shp = jax.ShapeDtypeStruct(w.shape, F32)
    r_in, r_out, r_shapes, r_sems, r_args = _rider_specs(rider)
    last = lambda: (pl.program_id(0) == 1) & (pl.program_id(1) == 1) & (pl.program_id(2) == nr - 1)
    first = lambda: (pl.program_id(0) == 0) & (pl.program_id(1) == 0) & (pl.program_id(2) == 0)
    return pl.pallas_call(
        _ride(body, 8, 4, rider, first, last),
        name="adamw_halves" + ("" if rider is None else "_ride"),
        grid_spec=pltpu.PrefetchScalarGridSpec(
            num_scalar_prefetch=1, grid=(2, 2, nr),
            in_specs=[full, pick(0, True), pick(0, False), pick(1, True), pick(1, False), full, full] + r_in,
            out_specs=[full] * 4 + r_out,
            scratch_shapes=r_sems,
        ),
        out_shape=[shp] * 4 + r_shapes,
        compiler_params=_cparams(3),
    )(cidx, w, mine[0], theirs[0], mine[1], theirs[1], m, v, *r_args)


_ANY = pl.BlockSpec(memory_space=pl.ANY)


def _mesh_pos():
    return lax.axis_index("x"), lax.axis_index("y"), lax.axis_index("c")


def _other_chips(x, y):
    return [(2 * x + (1 - y), x, 1 - y), (2 * (1 - x) + y, 1 - x, y), (2 * (1 - x) + (1 - y), 1 - x, 1 - y)]


def _gathered_shapes(shards):
    return [jax.ShapeDtypeStruct((2, N_CHIPS) + s.shape[1:], s.dtype) for s in shards]


def _gather_sems(n):
    return [pltpu.SemaphoreType.DMA((2 * n,)), pltpu.SemaphoreType.DMA((6 * n,)), pltpu.SemaphoreType.DMA((6 * n,))]


def _gather_steps(ins, outs, lsem, ssem, rsem):
    n = len(ins)
    x, y, c = _mesh_pos()
    me = 2 * x + y
    sib = (x, y, 1 - c)
    chips = _other_chips(x, y)

    def ici(k, d):
        return pltpu.make_async_remote_copy(
            ins[k].at[c], outs[k].at[c, me], ssem.at[6 * k + d], rsem.at[6 * k + d],
            device_id=(chips[d][1], chips[d][2], c), device_id_type=MESH_ID)

    def landed(k, d):
        return pltpu.make_async_remote_copy(
            ins[k].at[c], outs[k].at[c, chips[d][0]], ssem.at[6 * k + d], rsem.at[6 * k + d],
            device_id=sib, device_id_type=MESH_ID)

    def fwd(k, d, half):
        return pltpu.make_async_remote_copy(
            outs[k].at[half, chips[d][0]], outs[k].at[half, chips[d][0]], ssem.at[6 * k + 3 + d],
            rsem.at[6 * k + 3 + d], device_id=sib, device_id_type=MESH_ID)

    def local(k, h):
        return pltpu.make_async_copy(ins[k].at[h], outs[k].at[h, me], lsem.at[2 * k + h])

    def start():
        for k in range(n):
            for h in range(2):
                local(k, h).start()
            for d in range(3):
                ici(k, d).start()

    def mid():
        for d in range(3):
            for k in range(n):
                landed(k, d).wait_recv()
                fwd(k, d, c).start()

    def end():
        for d in range(3):
            for k in range(n):
                fwd(k, d, 1 - c).wait_recv()
        for k in range(n):
            for d in range(3):
                ici(k, d).wait_send()
                fwd(k, d, c).wait_send()
            for h in range(2):
                local(k, h).wait()

    return start, mid, end


def _gather_weights(shards):
    n = len(shards)

    def body(*refs):
        for step in _gather_steps(refs[:n], refs[n:2 * n], *refs[2 * n:]):
            step()

    return pl.pallas_call(
        body,
        name="gather_weights",
        in_specs=[_ANY] * n,
        out_specs=[_ANY] * n,
        out_shape=_gathered_shapes(shards),
        scratch_shapes=_gather_sems(n),
    )(*shards)


def _pair_rider(arrs, other_half):
    n = len(arrs)

    def steps(ins, outs, ssem, rsem):
        x, y, c = _mesh_pos()

        def copy(k):
            return pltpu.make_async_remote_copy(ins[k].at[:, 1 - c] if other_half else ins[k], outs[k], ssem.at[k],
                                                rsem.at[k], device_id=(x, y, 1 - c), device_id_type=MESH_ID)

        def start():
            for k in range(n):
                copy(k).start()

        def end():
            for k in range(n):
                copy(k).wait()

        return start, end

    shapes = [jax.ShapeDtypeStruct(a.shape[:1] + a.shape[2:] if other_half else a.shape, a.dtype) for a in arrs]
    return _Rider(arrs, shapes, [pltpu.SemaphoreType.DMA((n,)), pltpu.SemaphoreType.DMA((n,))], steps)


def _chip_rider(arrs, broadcast):
    n = len(arrs)

    def steps(ins, outs, lsem, ssem, rsem):
        x, y, c = _mesh_pos()
        me = 2 * x + y

        def copies():
            cps = [pltpu.make_async_copy(ins[k] if broadcast else ins[k].at[me], outs[k].at[me], lsem.at[k])
                   for k in range(n)]
            for k in range(n):
                for d, (j, tx, ty) in enumerate(_other_chips(x, y)):
                    cps.append(pltpu.make_async_remote_copy(
                        ins[k] if broadcast else ins[k].at[j], outs[k].at[me], ssem.at[3 * k + d], rsem.at[3 * k + d],
                        device_id=(tx, ty, c), device_id_type=MESH_ID))
            return cps

        def start():
            for cp in copies():
                cp.start()

        def end():
            for cp in copies():
                cp.wait()

        return start, end

    shapes = [jax.ShapeDtypeStruct(((N_CHIPS,) + a.shape) if broadcast else a.shape, a.dtype) for a in arrs]
    sems = [pltpu.SemaphoreType.DMA((n,)), pltpu.SemaphoreType.DMA((3 * n,)), pltpu.SemaphoreType.DMA((3 * n,))]
    return _Rider(arrs, shapes, sems, steps)


def _run_rider(name, rider):
    n, m = len(rider.arrs), len(rider.out_shapes)

    def body(*refs):
        start, end = rider.steps(refs[:n], refs[n:n + m], *refs[n + m:])
        start()
        end()

    return pl.pallas_call(
        body, name=name, in_specs=[_ANY] * n, out_specs=[_ANY] * m, out_shape=rider.out_shapes,
        scratch_shapes=rider.sems,
    )(*rider.arrs)


SMALL = ("norm_g", "lam_re", "lam_im", "b_re", "b_im", "c_re", "c_im", "d_skip", "log_dt", "b_glu", "ln_g", "ln_b",
         "w_s", "b_s", "w_pool", "pool_scale", "final_g")
BIG = ("w_in", "w_glu", "w_out")
WEIGHTS = ("norm_g", "w_in", "lam_re", "lam_im", "b_re", "b_im", "c_re", "c_im", "d_skip", "log_dt", "w_glu", "b_glu",
           "ln_g", "ln_b", "w_s", "b_s", "w_pool", "pool_scale", "w_out", "final_g")
PACK_UNIT = 8 * 128
PACK_ROWS = 1024


def _pack(arrs):
    parts, total = [], 0
    for a in arrs:
        f = a.reshape(-1).astype(F32)
        pad = (-f.shape[0]) % PACK_UNIT
        parts.append(jnp.pad(f, (0, pad)) if pad else f)
        total += f.shape[0] + pad
    tail = (-total) % (PACK_ROWS * 128)
    if tail:
        parts.append(jnp.zeros((tail,), F32))
    return jnp.concatenate(parts).reshape(-1, 128)


def _unpack(buf, like):
    flat = buf.reshape(-1)
    out, off = [], 0
    for a in like:
        n = math.prod(a.shape)
        out.append(flat[off:off + n].reshape(a.shape))
        off += n + ((-n) % PACK_UNIT)
    return out


def _layer_params(l, wt, g_glu):
    a_re, a_im, bb_re, bb_im = _s5_prep(wt["lam_re"][l], wt["lam_im"][l], wt["b_re"][l], wt["b_im"][l], wt["log_dt"][l])
    b4re, b4im = _block_diag_in(bb_re), _block_diag_in(bb_im)
    c4re, c4im = _block_diag_out(wt["c_re"][l]), _block_diag_out(-wt["c_im"][l])
    tr = lambda m: jnp.swapaxes(m, 1, 2).astype(BF16)
    causal = jnp.tril(jnp.ones((CHUNK, CHUNK), dtype=bool))
    ws = jnp.where(causal[None], wt["w_s"][l], 0.0)
    wglu = g_glu[l].reshape(S5_W, S5_W)
    return dict(
        b4re=b4re.astype(BF16), b4im=b4im.astype(BF16), c4re=c4re.astype(BF16), c4im=c4im.astype(BF16),
        b4re_t=tr(b4re), b4im_t=tr(b4im), c4re_t=tr(c4re), c4im_t=tr(c4im),
        dvec=wt["d_skip"][l].reshape(1, S5_W), wglu=wglu, wglu_t=wglu.T, bglu=wt["b_glu"][l].reshape(1, S5_W),
        coef_f=_scan_coefs(a_re, a_im, False), coef_r=_scan_coefs(a_re, a_im, True),
        ws=ws.astype(BF16), ws_t=tr(ws),
        bsf=jnp.broadcast_to(wt["b_s"][l][:, None, :], (SGU_HEADS, CHUNK, CHUNK)).transpose(2, 0, 1).reshape(CHUNK, SGU_W),
        lng=wt["ln_g"][l].reshape(1, SGU_W), lnb=wt["ln_b"][l].reshape(1, SGU_W),
        wp=wt["w_pool"][l].astype(BF16), wp_t=tr(wt["w_pool"][l]), scale=wt["pool_scale"][l].reshape(1, POOL_W),
        norm_g=wt["norm_g"][l].reshape(1, D_MODEL),
    )


def _local_step(x0, tgt, wt, g_in0, rest, rest_gathered, cidx=None):
    dist = cidx is not None
    xs, saved, params = [x0], [], []
    for l in range(DEPTH):
        norm_g = wt["norm_g"][l].reshape(1, D_MODEL)
        if l == 0 and not rest_gathered:
            z, h, g_in1, g_glu, g_out = _inproj(xs[-1], norm_g, g_in0, list(rest))
        elif l == 0:
            g_in1, g_glu, g_out = rest
            z, h = _inproj(xs[-1], norm_g, g_in0)
        else:
            z, h = _inproj(xs[-1], norm_g, g_in1)
        g_in = (g_in0, g_in1)
        p = _layer_params(l, wt, g_glu)
        params.append(p)
        ya, yraw, sre, sim = _s5_fwd(z, p)
        yb = _sgu_fwd(z, p["ws"], p["bsf"], p["lng"], p["lnb"])
        yc = _pool_fwd(z, p["wp"], p["scale"])
        xn, y = _outproj(ya, yb, yc, g_out[l].reshape(D_MODEL, D_MODEL), xs[-1])
        xs.append(xn)
        saved.append((z, h, yraw, sre, sim, y))

    dx, loss, dfg = _loss_head(xs[-1], wt["final_g"].reshape(1, D_MODEL), tgt)

    gr = {k: [None] * DEPTH for k in WEIGHTS if k != "final_g"}
    mine, theirs, chip_sum = [None] * DEPTH, [None] * DEPTH, None
    halves = lambda a, rows: a.reshape(N_CHIPS, 2, rows // 2, a.shape[-1])
    for l in reversed(range(DEPTH)):
        p = params[l]
        z, h, yraw, sre, sim, y = saved[l]
        w_out = g_out[l].reshape(D_MODEL, D_MODEL)
        dy = _outproj_bwd_dy(dx, w_out)
        gr["w_out"][l] = _outproj_bwd_dw(y, dx)
        ride_c = _chip_rider(chip_sum, False) if dist and l == 0 else None
        dz, dbre, dbim, dcre, dcim, dd, dwg, dbg, da, *landed = _s5_bwd(dy, z, yraw, sre, sim, p, ride_c)
        if ride_c:
            mine[1] = [_sum_chips(r) for r in landed]
        ride_e = _pair_rider(mine[1], False) if dist and l == 0 else None
        dz, dws, dbsf, dlng, dlnb, *got = _sgu_bwd(dy, z, dz, p["ws"], p["ws_t"], p["bsf"], p["lng"], p["lnb"], ride_e)
        if ride_e:
            theirs[1] = got
        dz, dwp, dsc = _pool_bwd(dy, z, dz, p["wp"], p["wp_t"], p["scale"])
        gr["w_in"][l] = _inproj_bwd_dw(h, dz)
        if not dist:
            dx, dng = _inproj_bwd_dx(dz, g_in[l], xs[l], p["norm_g"], dx)
        else:
            part = [halves(gr["w_in"][l], D_MODEL), halves(dwg, S5_W // N_CHIPS),
                    halves(gr["w_out"][l], D_MODEL // N_CHIPS)]
            ride_a = _pair_rider(part, True)
            if l == 1:
                dx, dng, *from_sib = _inproj_bwd_dx(dz, g_in[l], xs[l], p["norm_g"], dx, ride_a)
                chip_sum = [_add_own_half(a, r, cidx) for a, r in zip(part, from_sib)]
            else:
                nt = x0.shape[0] // _dx_tile(x0.shape[0])
                n_top = max(nt // 4, 1)
                n_last = 1 if nt < 8 else 2
                n_mid = nt - n_top - n_last
                dx_top, dng_top, *from_sib = _inproj_bwd_dx(dz, g_in[l], xs[l], p["norm_g"], dx, ride_a,
                                                            tiles=(0, n_top))
                chip_sum0 = [_add_own_half(a, r, cidx) for a, r in zip(part, from_sib)]
                if n_mid:
                    dx_mid, dng_mid, *landed = _inproj_bwd_dx(dz, g_in[l], xs[l], p["norm_g"], dx,
                                                              _chip_rider(chip_sum0, False), tiles=(n_top, n_mid),
                                                              prev=dx_top)
                else:
                    dx_mid, dng_mid = dx_top, 0.0
                    landed = _run_rider("grad_chip_exchange", _chip_rider(chip_sum0, False))
                dx, dng_last = _inproj_bwd_dx(dz, g_in[l], xs[l], p["norm_g"], dx, tiles=(n_top + n_mid, n_last),
                                              prev=dx_mid)
                dng = dng_top + dng_mid + dng_last
                mine[0] = [_sum_chips(r) for r in landed]
                theirs[0] = _run_rider("grad_result_exchange", _pair_rider(mine[0], False))

        raw = (wt["lam_re"][l], wt["lam_im"][l], wt["b_re"][l], wt["b_im"][l], wt["log_dt"][l])
        _, vjp = jax.vjp(_s5_prep, *raw)
        da = jnp.sum(da, axis=1)
        cot = (da[0].reshape(S5_GROUPS, S5_STATE), da[1].reshape(S5_GROUPS, S5_STATE),
               _block_diag_in_grad(dbre), _block_diag_in_grad(dbim))
        gr["lam_re"][l], gr["lam_im"][l], gr["b_re"][l], gr["b_im"][l], gr["log_dt"][l] = vjp(cot)
        gr["c_re"][l] = _block_diag_out_grad(dcre)
        gr["c_im"][l] = -_block_diag_out_grad(dcim)
        gr["d_skip"][l] = dd.reshape(S5_GROUPS, S5_CH)
        gr["w_glu"][l] = dwg
        gr["b_glu"][l] = dbg.reshape(S5_W)
        causal = jnp.tril(jnp.ones((CHUNK, CHUNK), dtype=bool))
        gr["w_s"][l] = jnp.where(causal[None], dws, 0.0)
        gr["b_s"][l] = dbsf.reshape(CHUNK, SGU_HEADS, CHUNK).sum(-1).T
        gr["ln_g"][l] = dlng.reshape(SGU_W)
        gr["ln_b"][l] = dlnb.reshape(SGU_W)
        gr["w_pool"][l] = dwp
        gr["pool_scale"][l] = dsc.reshape(POOL_W)
        gr["norm_g"][l] = dng.reshape(D_MODEL)

    grads = {k: (v if k in BIG else jnp.stack(v)) for k, v in gr.items()}
    grads["final_g"] = dfg.reshape(D_MODEL)
    if dist:
        for i, k in enumerate(BIG):
            grads[k] = ([mine[l][i] for l in range(DEPTH)], [theirs[l][i] for l in range(DEPTH)])
    return loss, dx, grads


def kernel(x, norm_g, w_in, lam_re, lam_im, b_re, b_im, c_re, c_im, d_skip, log_dt, w_glu, b_glu, ln_g, ln_b, w_s, b_s, w_pool, pool_scale, w_out, final_g, loss_target, m_norm_g, m_w_in, m_lam_re, m_lam_im, m_b_re, m_b_im, m_c_re, m_c_im, m_d_skip, m_log_dt, m_w_glu, m_b_glu, m_ln_g, m_ln_b, m_w_s, m_b_s, m_w_pool, m_pool_scale, m_w_out, m_final_g, v_norm_g, v_w_in, v_lam_re, v_lam_im, v_b_re, v_b_im, v_c_re, v_c_im, v_d_skip, v_log_dt, v_w_glu, v_b_glu, v_ln_g, v_ln_b, v_w_s, v_b_s, v_w_pool, v_pool_scale, v_w_out, v_final_g):
    wt = dict(norm_g=norm_g, w_in=w_in, lam_re=lam_re, lam_im=lam_im, b_re=b_re, b_im=b_im, c_re=c_re, c_im=c_im,
              d_skip=d_skip, log_dt=log_dt, w_glu=w_glu, b_glu=b_glu, ln_g=ln_g, ln_b=ln_b, w_s=w_s, b_s=b_s,
              w_pool=w_pool, pool_scale=pool_scale, w_out=w_out, final_g=final_g)
    mom = dict(norm_g=m_norm_g, w_in=m_w_in, lam_re=m_lam_re, lam_im=m_lam_im, b_re=m_b_re, b_im=m_b_im, c_re=m_c_re,
               c_im=m_c_im, d_skip=m_d_skip, log_dt=m_log_dt, w_glu=m_w_glu, b_glu=m_b_glu, ln_g=m_ln_g, ln_b=m_ln_b,
               w_s=m_w_s, b_s=m_b_s, w_pool=m_w_pool, pool_scale=m_pool_scale, w_out=m_w_out, final_g=m_final_g)
    vel = dict(norm_g=v_norm_g, w_in=v_w_in, lam_re=v_lam_re, lam_im=v_lam_im, b_re=v_b_re, b_im=v_b_im, c_re=v_c_re,
               c_im=v_c_im, d_skip=v_d_skip, log_dt=v_log_dt, w_glu=v_w_glu, b_glu=v_b_glu, ln_g=v_ln_g, ln_b=v_ln_b,
               w_s=v_w_s, b_s=v_b_s, w_pool=v_w_pool, pool_scale=v_pool_scale, w_out=v_w_out, final_g=v_final_g)
    T = x.shape[1]
    cidx = lax.axis_index("c").astype(jnp.int32).reshape(1)

    w_in_b = w_in.astype(BF16)
    (g_in0,) = _gather_weights([w_in_b[0].reshape(2, HALF_D, SHARD_COLS)])
    rest = (w_in_b[1].reshape(2, HALF_D, SHARD_COLS), w_glu.astype(BF16), w_out.astype(BF16))
    loss, grad_x, grads = _local_step(x.reshape(T, D_MODEL), loss_target.reshape(T, D_MODEL), wt, g_in0, rest, False,
                                      cidx)

    packed = _pack([grads[k] for k in SMALL] + [loss[0, 0:1]])
    (sib_packed,) = _run_rider("small_pair_exchange", _pair_rider([packed], False))
    chip_packed = _add2(packed, sib_packed)
    half_rows = chip_packed.shape[0] // 2
    my_half = lax.dynamic_index_in_dim(chip_packed.reshape(2, half_rows, 128), cidx[0], 0, keepdims=False)
    small_ride = _chip_rider([my_half], True)

    out_g, out_d, out_m, out_v = {}, {}, {}, {}
    all_half = None
    for k in BIG:
        shape = wt[k].shape
        quad = lambda t: t.reshape(2, 2, shape[1] // 2, shape[2])
        g, d, m, v, *landed = _adamw_halves(quad(wt[k]), grads[k][0], grads[k][1], quad(mom[k]), quad(vel[k]), cidx,
                                            small_ride if k == BIG[0] else None)
        if landed:
            (all_half,) = landed
        out_g[k], out_d[k], out_m[k], out_v[k] = (t.reshape(shape) for t in (g, d, m, v))

    mine_half = _sum_chips(all_half)
    (their_half,) = _run_rider("small_result_exchange", _pair_rider([mine_half], False))
    total = jnp.where(cidx[0] == 0, jnp.concatenate([mine_half, their_half]), jnp.concatenate([their_half, mine_half]))
    like = [wt[k] for k in SMALL]
    small_g = _unpack(total, like + [loss[0, 0:1]])
    loss_out = small_g[-1].reshape(())
    w_p, m_p, v_p = _pack(like), _pack([mom[k] for k in SMALL]), _pack([vel[k] for k in SMALL])
    d_p, mo_p, vo_p = _adamw(w_p, total, m_p, v_p)
    for k, g, d, m, v in zip(SMALL, small_g[:-1], _unpack(d_p, like), _unpack(mo_p, like), _unpack(vo_p, like)):
        out_g[k], out_d[k], out_m[k], out_v[k] = g, d, m, v

    return (loss_out, grad_x.reshape(x.shape), *[out_g[k] for k in WEIGHTS], *[out_d[k] for k in WEIGHTS],
            *[out_m[k] for k in WEIGHTS], *[out_v[k] for k in WEIGHTS])
```

```python
import functools
import math

import jax
import jax.numpy as jnp
from jax import lax
from jax.experimental import pallas as pl
from jax.experimental.pallas import tpu as pltpu

F32 = jnp.float32
BF16 = jnp.bfloat16

D_MODEL = 2048
DEPTH = 2
S5_W = 512
SGU_W = 1024
POOL_W = 512
IN_COLS = 5120
N_CHIPS = 4
SHARD_COLS = IN_COLS // N_CHIPS
S5_GROUPS = 32
S5_STATE = 64
S5_CH = 16
STATE_W = S5_GROUPS * S5_STATE
SUPER = 4
CHUNK = 128
SGU_HEADS = 8
POOL_WINDOWS = (2, 4, 8, 16)
POOL_HALO = 16
RMS_EPS = 1e-6
LN_EPS = 1e-5
SCAN_COLS = 512

ADAM_LR = 0.001
ADAM_B1 = 0.9
ADAM_B2 = 0.999
ADAM_EPS = 1e-08
ADAM_WD = 0.01
ADAM_STEP = 10

VMEM_LIMIT = 56 * 1024 * 1024
MESH_ID = pl.DeviceIdType.MESH

_GELU_K0 = math.sqrt(2.0 / math.pi)
_GELU_K1 = 0.044715


def _cparams(n_axes):
    return pltpu.CompilerParams(dimension_semantics=("arbitrary",) * n_axes, vmem_limit_bytes=VMEM_LIMIT)


def _gelu(x):
    t = jnp.tanh(_GELU_K0 * (x + _GELU_K1 * (x * x * x)))
    return 0.5 * x * (1.0 + t)


def _gelu_and_grad(x):
    x2 = x * x
    t = jnp.tanh(_GELU_K0 * (x + _GELU_K1 * (x * x2)))
    g = 0.5 * x * (1.0 + t)
    dg = 0.5 * (1.0 + t) + 0.5 * x * (1.0 - t * t) * (_GELU_K0 * (1.0 + 3.0 * _GELU_K1 * x2))
    return g, dg


def _silu_and_grad(x):
    s = jax.nn.sigmoid(x)
    return x * s, s * (1.0 + x * (1.0 - s))


def _dot(a, b):
    return jnp.dot(a.astype(BF16), b.astype(BF16), preferred_element_type=F32)


def _dot_nt(a, b):
    return lax.dot_general(a.astype(BF16), b.astype(BF16), (((1,), (1,)), ((), ())), preferred_element_type=F32)


def _dot_tn(a, b):
    return lax.dot_general(a.astype(BF16), b.astype(BF16), (((0,), (0,)), ((), ())), preferred_element_type=F32)


def _full(shape):
    nd = len(shape)
    return pl.BlockSpec(shape, lambda *_: (0,) * nd)


class _Rider:
    def __init__(self, arrs, out_shapes, sems, steps):
        self.arrs, self.out_shapes, self.sems, self.steps = list(arrs), list(out_shapes), list(sems), steps


def _ride(body, n_in, n_out, rider, first, last, middle=None):
    if rider is None:
        return body
    ri, ro, ns = len(rider.arrs), len(rider.out_shapes), len(rider.sems)

    def wrapped(*refs):
        o0 = n_in + ri
        steps = rider.steps(refs[n_in:o0], refs[o0 + n_out:o0 + n_out + ro], *refs[len(refs) - ns:])
        pl.when(first())(steps[0])
        if len(steps) == 3:
            pl.when(middle())(steps[1])
        body(*refs[:n_in], *refs[o0:o0 + n_out], *refs[o0 + n_out + ro:len(refs) - ns])
        pl.when(last())(steps[-1])

    return wrapped


class _ColumnWriter:
    def __init__(self, stage_ref, sem_ref, dst_ref, col0, step, n_steps):
        self.stage, self.sem, self.dst, self.col0, self.step, self.n = stage_ref, sem_ref, dst_ref, col0, step, n_steps
        self.tm, self.w = stage_ref.shape[1], stage_ref.shape[2]

    def _copy(self, slot, row0):
        return pltpu.make_async_copy(self.stage.at[slot],
                                     self.dst.at[pl.ds(row0, self.tm), pl.ds(self.col0, self.w)], self.sem.at[slot])

    def slot(self):
        s = self.step % 2

        @pl.when(self.step >= 2)
        def _():
            self._copy(s, 0).wait()

        return self.stage.at[s]

    def send(self, row0):
        s = self.step % 2
        self._copy(s, row0).start()

        @pl.when(self.step == self.n - 1)
        def _():
            self._copy(s, 0).wait()
            if self.n >= 2:
                self._copy(1 - s, 0).wait()


def _stage_scratch(tm, widths):
    return ([pltpu.VMEM((2, tm, w), BF16) for w in widths], [pltpu.SemaphoreType.DMA((2,)) for _ in widths])


def _rider_specs(rider):
    if rider is None:
        return [], [], [], [], []
    anyspec = pl.BlockSpec(memory_space=pl.ANY)
    return ([anyspec] * len(rider.arrs), [anyspec] * len(rider.out_shapes), rider.out_shapes, rider.sems, rider.arrs)


HALF_D = D_MODEL // 2


def _inproj(x, g, w):
    T = x.shape[0]
    tm = min(512, T)

    def body(x_ref, g_ref, w_ref, z_ref, h_ref, hs_ref):
        @pl.when(pl.program_id(1) == 0)
        def _():
            xv = x_ref[...]
            r = lax.rsqrt(jnp.mean(xv * xv, axis=-1, keepdims=True) + RMS_EPS)
            hv = (xv * r * g_ref[...]).astype(BF16)
            hs_ref[...] = hv
            h_ref[...] = hv

        z_ref[...] = (jnp.dot(hs_ref[:, 0:HALF_D], w_ref[0], preferred_element_type=F32)
                      + jnp.dot(hs_ref[:, HALF_D:D_MODEL], w_ref[1], preferred_element_type=F32))

    return pl.pallas_call(
        body,
        name="inproj",
        grid=(T // tm, N_CHIPS),
        in_specs=[
            pl.BlockSpec((tm, D_MODEL), lambda i, j: (i, 0)),
            pl.BlockSpec((1, D_MODEL), lambda i, j: (0, 0)),
            pl.BlockSpec((2, None, HALF_D, SHARD_COLS), lambda i, j: (0, j, 0, 0)),
        ],
        out_specs=[
            pl.BlockSpec((tm, SHARD_COLS), lambda i, j: (i, j)),
            pl.BlockSpec((tm, D_MODEL), lambda i, j: (i, 0)),
        ],
        out_shape=[jax.ShapeDtypeStruct((T, IN_COLS), F32), jax.ShapeDtypeStruct((T, D_MODEL), BF16)],
        scratch_shapes=[pltpu.VMEM((tm, D_MODEL), BF16)],
        compiler_params=_cparams(2),
    )(x, g, w)


def _rms_h(x, g):
    T = x.shape[0]
    tm = min(512, T)

    def body(x_ref, g_ref, h_ref):
        xv = x_ref[...]
        r = lax.rsqrt(jnp.mean(xv * xv, axis=-1, keepdims=True) + RMS_EPS)
        h_ref[...] = (xv * r * g_ref[...]).astype(BF16)

    return pl.pallas_call(
        body, name="rms_h", grid=(T // tm,),
        in_specs=[pl.BlockSpec((tm, D_MODEL), lambda i: (i, 0)), pl.BlockSpec((1, D_MODEL), lambda i: (0, 0))],
        out_specs=pl.BlockSpec((tm, D_MODEL), lambda i: (i, 0)),
        out_shape=jax.ShapeDtypeStruct((T, D_MODEL), BF16), compiler_params=_cparams(1),
    )(x, g)


def _inproj_first(h, w0, order, riders):
    T = h.shape[0]
    tm = min(512, T)
    ni = T // tm
    n = len(riders)

    def body(order_ref, h_ref, w0_ref, *refs):
        rin = refs[:n]
        z_ref, gin_ref = refs[n:n + 2]
        rout = refs[n + 2:2 * n + 2]
        wbuf, csem, lsem, ssem, rsem = refs[2 * n + 2:2 * n + 7]
        s, i = pl.program_id(0), pl.program_id(1)
        x, y, c = _mesh_pos()
        me = 2 * x + y
        sib = (x, y, 1 - c)
        chips = _other_chips(x, y)
        if n:
            r_start, r_mid, r_end = _gather_steps(rin, rout, *refs[2 * n + 7:])

        def ici(d):
            return pltpu.make_async_remote_copy(w0_ref.at[c], gin_ref.at[c, me], ssem.at[d], rsem.at[d],
                                                device_id=(chips[d][1], chips[d][2], c), device_id_type=MESH_ID)

        def landed(d):
            return pltpu.make_async_remote_copy(w0_ref.at[c], gin_ref.at[c, chips[d][0]], ssem.at[d], rsem.at[d],
                                                device_id=sib, device_id_type=MESH_ID)

        def fwd(d, half):
            blk = gin_ref.at[half, chips[d][0]]
            return pltpu.make_async_remote_copy(blk, blk, ssem.at[3 + d], rsem.at[3 + d], device_id=sib,
                                                device_id_type=MESH_ID)

        def local(hf):
            return pltpu.make_async_copy(w0_ref.at[hf], gin_ref.at[hf, me], lsem.at[hf])

        def load(src):
            cp = pltpu.make_async_copy(src, wbuf, csem.at[0])
            cp.start()
            cp.wait()

        @pl.when((s == 0) & (i == 0))
        def _():
            for d in range(3):
                ici(d).start()
            local(0).start()
            local(1).start()
            load(w0_ref)

        for d in range(3):
            @pl.when((s == d + 1) & (i == 0))
            def _(d=d):
                landed(d).wait_recv()
                fwd(d, c).start()
                fwd(d, 1 - c).wait_recv()
                load(gin_ref.at[:, chips[d][0]])
                if d == 2 and n:
                    r_start()

        z_ref[...] = (jnp.dot(h_ref[:, 0:HALF_D], wbuf[0], preferred_element_type=F32)
                      + jnp.dot(h_ref[:, HALF_D:D_MODEL], wbuf[1], preferred_element_type=F32))

        @pl.when((s == N_CHIPS - 1) & (i == ni - 1))
        def _():
            for d in range(3):
                ici(d).wait_send()
                fwd(d, c).wait_send()
            local(0).wait()
            local(1).wait()
            if n:
                r_mid()
                r_end()

    anyspec = pl.BlockSpec(memory_space=pl.ANY)
    return pl.pallas_call(
        body,
        name="inproj_first",
        grid_spec=pltpu.PrefetchScalarGridSpec(
            num_scalar_prefetch=1,
            grid=(N_CHIPS, ni),
            in_specs=[pl.BlockSpec((tm, D_MODEL), lambda s, i, o: (i, 0)), anyspec] + [anyspec] * n,
            out_specs=[pl.BlockSpec((tm, SHARD_COLS), lambda s, i, o: (i, o[s])), anyspec] + [anyspec] * n,
            scratch_shapes=[pltpu.VMEM((2, HALF_D, SHARD_COLS), BF16), pltpu.SemaphoreType.DMA((1,)),
                            pltpu.SemaphoreType.DMA((2,)), pltpu.SemaphoreType.DMA((6,)),
                            pltpu.SemaphoreType.DMA((6,))] + (_gather_sems(n) if n else []),
        ),
        out_shape=[jax.ShapeDtypeStruct((T, IN_COLS), F32),
                   jax.ShapeDtypeStruct((2, N_CHIPS, HALF_D, SHARD_COLS), BF16)] + _gathered_shapes(riders),
        compiler_params=_cparams(2),
    )(order, h, w0, *riders)


def _outproj(ya, yb, yc, w, x):
    T = x.shape[0]
    tm = min(512, T)
    tn = 1024

    def body(ya_ref, yb_ref, yc_ref, w_ref, x_ref, o_ref, y_ref):
        acc = jnp.dot(ya_ref[...], w_ref[0:S5_W, :], preferred_element_type=F32)
        acc += jnp.dot(yb_ref[...], w_ref[S5_W:S5_W + SGU_W, :], preferred_element_type=F32)
        acc += jnp.dot(yc_ref[...], w_ref[S5_W + SGU_W:D_MODEL, :], preferred_element_type=F32)
        o_ref[...] = x_ref[...] + acc

        @pl.when(pl.program_id(1) == 0)
        def _():
            y_ref[:, 0:S5_W] = ya_ref[...]
            y_ref[:, S5_W:S5_W + SGU_W] = yb_ref[...]
            y_ref[:, S5_W + SGU_W:D_MODEL] = yc_ref[...]

    return pl.pallas_call(
        body,
        name="outproj",
        grid=(T // tm, D_MODEL // tn),
        in_specs=[
            pl.BlockSpec((tm, S5_W), lambda i, j: (i, 0)),
            pl.BlockSpec((tm, SGU_W), lambda i, j: (i, 0)),
            pl.BlockSpec((tm, POOL_W), lambda i, j: (i, 0)),
            pl.BlockSpec((D_MODEL, tn), lambda i, j: (0, j)),
            pl.BlockSpec((tm, tn), lambda i, j: (i, j)),
        ],
        out_specs=[
            pl.BlockSpec((tm, tn), lambda i, j: (i, j)),
            pl.BlockSpec((tm, D_MODEL), lambda i, j: (i, 0)),
        ],
        out_shape=[jax.ShapeDtypeStruct((T, D_MODEL), F32), jax.ShapeDtypeStruct((T, D_MODEL), BF16)],
        compiler_params=_cparams(2),
    )(ya, yb, yc, w, x)


def _outproj_bwd_dy(dxo, w):
    T = dxo.shape[0]
    tm = min(512, T)
    tn = 1024

    def body(d_ref, w_ref, o_ref, ds_ref):
        @pl.when(pl.program_id(1) == 0)
        def _():
            ds_ref[...] = d_ref[...].astype(BF16)

        o_ref[...] = lax.dot_general(ds_ref[...], w_ref[...], (((1,), (1,)), ((), ())), preferred_element_type=F32)

    return pl.pallas_call(
        body,
        name="outproj_bwd_dy",
        grid=(T // tm, D_MODEL // tn),
        in_specs=[
            pl.BlockSpec((tm, D_MODEL), lambda i, j: (i, 0)),
            pl.BlockSpec((tn, D_MODEL), lambda i, j: (j, 0)),
        ],
        out_specs=pl.BlockSpec((tm, tn), lambda i, j: (i, j)),
        out_shape=jax.ShapeDtypeStruct((T, D_MODEL), F32),
        scratch_shapes=[pltpu.VMEM((tm, D_MODEL), BF16)],
        compiler_params=_cparams(2),
    )(dxo, w)


def _outproj_bwd_dw(y, dxo):
    T = y.shape[0]
    tm = min(512, T)
    tr = 1024

    def body(y_ref, d_ref, o_ref):
        @pl.when(pl.program_id(1) == 0)
        def _():
            o_ref[...] = jnp.zeros_like(o_ref)

        o_ref[...] += _dot_tn(y_ref[...], d_ref[...])

    return pl.pallas_call(
        body,
        name="outproj_bwd_dw",
        grid=(D_MODEL // tr, T // tm),
        in_specs=[
            pl.BlockSpec((tm, tr), lambda p, t: (t, p)),
            pl.BlockSpec((tm, D_MODEL), lambda p, t: (t, 0)),
        ],
        out_specs=pl.BlockSpec((tr, D_MODEL), lambda p, t: (p, 0)),
        out_shape=jax.ShapeDtypeStruct((D_MODEL, D_MODEL), F32),
        compiler_params=_cparams(2),
    )(y, dxo)


def _inproj_bwd_dw(h, dz):
    T = h.shape[0]
    tm = min(512, T)

    def body(h_ref, dz_ref, o_ref):
        @pl.when(pl.program_id(1) == 0)
        def _():
            o_ref[...] = jnp.zeros_like(o_ref)

        o_ref[...] += _dot_tn(h_ref[...], dz_ref[...])

    return pl.pallas_call(
        body,
        name="inproj_bwd_dw",
        grid=(N_CHIPS, T // tm),
        in_specs=[
            pl.BlockSpec((tm, D_MODEL), lambda j, t: (t, 0)),
            pl.BlockSpec((tm, SHARD_COLS), lambda j, t: (t, j)),
        ],
        out_specs=pl.BlockSpec((None, D_MODEL, SHARD_COLS), lambda j, t: (j, 0, 0)),
        out_shape=jax.ShapeDtypeStruct((N_CHIPS, D_MODEL, SHARD_COLS), F32),
        compiler_params=_cparams(2),
    )(h, dz)


def _dx_tile(T):
    return min(512, max(T // 4, 8))


def _inproj_bwd_dx(dz, w4, x, g, dxo, rider=None, tiles=None, prev=None):
    T = x.shape[0]
    tm = _dx_tile(T)
    t0, ni = tiles if tiles else (0, T // tm)
    nk = N_CHIPS
    nt = (((1,), (1,)), ((), ()))
    n_in = 5 if prev is None else 6

    def body(dz_ref, w_ref, x_ref, g_ref, dxo_ref, *rest):
        dx_ref, dg_ref, acc_ref = rest[-3:]
        i, j = pl.program_id(0), pl.program_id(1)
        lo = lax.dot_general(dz_ref[...], w_ref[0], nt, preferred_element_type=F32)
        hi = lax.dot_general(dz_ref[...], w_ref[1], nt, preferred_element_type=F32)

        @pl.when(j == 0)
        def _():
            acc_ref[:, 0:HALF_D] = lo
            acc_ref[:, HALF_D:D_MODEL] = hi

        @pl.when(j > 0)
        def _():
            acc_ref[:, 0:HALF_D] += lo
            acc_ref[:, HALF_D:D_MODEL] += hi

        @pl.when(j == nk - 1)
        def _():
            @pl.when(i == 0)
            def _():
                dg_ref[...] = jnp.zeros_like(dg_ref)

            rc = min(128, tm)
            for c in range(tm // rc):
                rows = slice(c * rc, (c + 1) * rc)
                dh = acc_ref[rows, :]
                xv = x_ref[rows, :]
                r = lax.rsqrt(jnp.mean(xv * xv, axis=-1, keepdims=True) + RMS_EPS)
                xh = xv * r
                w = dh * g_ref[...]
                dx_ref[rows, :] = dxo_ref[rows, :] + r * (w - xh * jnp.mean(w * xh, axis=-1, keepdims=True))
                dg_ref[...] += jnp.sum(dh * xh, axis=0, keepdims=True)

    r_in, r_out, r_shapes, r_sems, r_args = _rider_specs(rider)
    return pl.pallas_call(
        _ride(body, n_in, 2, rider, lambda: (pl.program_id(0) == 0) & (pl.program_id(1) == 0),
              lambda: (pl.program_id(0) == ni - 1) & (pl.program_id(1) == nk - 1)),
        name="inproj_bwd_dx" + ("" if rider is None else "_ride") + ("" if prev is None else "_rest"),
        grid=(ni, nk),
        in_specs=[
            pl.BlockSpec((tm, SHARD_COLS), lambda i, j: (i + t0, j)),
            pl.BlockSpec((2, None, HALF_D, SHARD_COLS), lambda i, j: (0, j, 0, 0)),
            pl.BlockSpec((tm, D_MODEL), lambda i, j: (i + t0, 0)),
            pl.BlockSpec((1, D_MODEL), lambda i, j: (0, 0)),
            pl.BlockSpec((tm, D_MODEL), lambda i, j: (i + t0, 0)),
        ] + ([] if prev is None else [pl.BlockSpec(memory_space=pl.ANY)]) + r_in,
        out_specs=[
            pl.BlockSpec((tm, D_MODEL), lambda i, j: (i + t0, 0)),
            pl.BlockSpec((1, D_MODEL), lambda i, j: (0, 0)),
        ] + r_out,
        out_shape=[jax.ShapeDtypeStruct((T, D_MODEL), F32), jax.ShapeDtypeStruct((1, D_MODEL), F32)] + r_shapes,
        scratch_shapes=[pltpu.VMEM((tm, D_MODEL), F32)] + r_sems,
        input_output_aliases={} if prev is None else {5: 0},
        compiler_params=_cparams(2),
    )(dz, w4, x, g, dxo, *([] if prev is None else [prev]), *r_args)


def _loss_head(x, g, tgt):
    T = x.shape[0]
    tm = min(512, T)

    def body(x_ref, g_ref, t_ref, dx_ref, l_ref, dg_ref):
        i = pl.program_id(0)
        xv = x_ref[...]
        r = lax.rsqrt(jnp.mean(xv * xv, axis=-1, keepdims=True) + RMS_EPS)
        xh = xv * r
        err = xh * g_ref[...] - t_ref[...]
        lpart = 0.5 * jnp.sum(jnp.mean(err * err, axis=-1, keepdims=True), axis=0, keepdims=True)
        dout = err * (1.0 / D_MODEL)
        w = dout * g_ref[...]
        dx_ref[...] = r * (w - xh * jnp.mean(w * xh, axis=-1, keepdims=True))
        gpart = jnp.sum(dout * xh, axis=0, keepdims=True)

        @pl.when(i == 0)
        def _():
            l_ref[...] = jnp.broadcast_to(lpart, l_ref.shape)
            dg_ref[...] = gpart

        @pl.when(i > 0)
        def _():
            l_ref[...] += jnp.broadcast_to(lpart, l_ref.shape)
            dg_ref[...] += gpart

    return pl.pallas_call(
        body,
        name="loss_head",
        grid=(T // tm,),
        in_specs=[
            pl.BlockSpec((tm, D_MODEL), lambda i: (i, 0)),
            pl.BlockSpec((1, D_MODEL), lambda i: (0, 0)),
            pl.BlockSpec((tm, D_MODEL), lambda i: (i, 0)),
        ],
        out_specs=[
            pl.BlockSpec((tm, D_MODEL), lambda i: (i, 0)),
            pl.BlockSpec((1, 128), lambda i: (0, 0)),
            pl.BlockSpec((1, D_MODEL), lambda i: (0, 0)),
        ],
        out_shape=[
            jax.ShapeDtypeStruct((T, D_MODEL), F32),
            jax.ShapeDtypeStruct((1, 128), F32),
            jax.ShapeDtypeStruct((1, D_MODEL), F32),
        ],
        compiler_params=_cparams(1),
    )(x, g, tgt)


def _s5_prep(lam_re, lam_im, b_re, b_im, log_dt):
    lam = lax.complex(lam_re, lam_im)
    dt = jnp.exp(log_dt)[:, None]
    a = jnp.exp(lam * dt)
    bbar = ((a - 1.0) / lam)[..., None] * lax.complex(b_re, b_im)
    return jnp.real(a), jnp.imag(a), jnp.real(bbar), jnp.imag(bbar)


def _block_diag_in(m):
    m4 = m.reshape(SUPER, 8, S5_STATE, S5_CH)
    eye = jnp.eye(8, dtype=m.dtype)
    out = jnp.einsum("jgph,gk->jghkp", m4, eye)
    return out.reshape(SUPER, 8 * S5_CH, 8 * S5_STATE)


def _block_diag_in_grad(d):
    d6 = d.reshape(SUPER, 8, S5_CH, 8, S5_STATE)
    diag = jnp.einsum("jghgp->jgph", d6)
    return diag.reshape(S5_GROUPS, S5_STATE, S5_CH)


def _block_diag_out(m):
    m4 = m.reshape(SUPER, 8, S5_CH, S5_STATE)
    eye = jnp.eye(8, dtype=m.dtype)
    out = jnp.einsum("jghp,gk->jgpkh", m4, eye)
    return out.reshape(SUPER, 8 * S5_STATE, 8 * S5_CH)


def _block_diag_out_grad(d):
    d6 = d.reshape(SUPER, 8, S5_STATE, 8, S5_CH)
    diag = jnp.einsum("jgpgh->jghp", d6)
    return diag.reshape(S5_GROUPS, S5_CH, S5_STATE)


def _scan_coefs(a_re, a_im, reverse):
    a = lax.complex(a_re.reshape(-1), a_im.reshape(-1))
    if reverse:
        a = jnp.conj(a)
    pw = [a]
    for _ in range(7):
        pw.append(pw[-1] * a)
    rows = jnp.arange(8)

    def masked(k):
        m = (rows + k <= 7) if reverse else (rows >= k)
        return jnp.where(m[:, None], pw[k - 1][None, :], 0.0)

    a1, a2, a4 = masked(1), masked(2), masked(4)
    carry = jnp.stack([pw[7 - r] for r in range(8)]) if reverse else jnp.stack(pw)
    parts = []
    for c in (a1, a2, a4, carry):
        parts += [jnp.real(c), jnp.imag(c)]
    return jnp.stack(parts).astype(F32)


def _scan_block(r, im, coef_ref, cs, reverse):
    for k, idx in ((1, 0), (2, 2), (4, 4)):
        ar = coef_ref[idx, :, cs]
        ai = coef_ref[idx + 1, :, cs]
        sh = 8 - k if reverse else k
        rr = pltpu.roll(r, sh, 0)
        ri = pltpu.roll(im, sh, 0)
        r, im = r + ar * rr - ai * ri, im + ar * ri + ai * rr
    return r, im


def _s5_fwd(z, p, rider=None):
    T = z.shape[0]
    tm = min(256, T)
    nblk = tm // 8
    W = STATE_W

    def body(xa_ref, ga_ref, bre_ref, bim_ref, cre_ref, cim_ref, dv_ref, wg_ref, bg_ref, coef_ref,
             ya_ref, yraw_ref, sre_ref, sim_ref, wre, wim):
        @pl.when(pl.program_id(0) == 0)
        def _():
            wre[0:8, :] = jnp.zeros((8, W), F32)
            wim[0:8, :] = jnp.zeros((8, W), F32)

        xa = xa_ref[...]
        xab = xa.astype(BF16)
        for j in range(SUPER):
            xj = xab[:, j * 128:(j + 1) * 128]
            wre[8:8 + tm, j * 512:(j + 1) * 512] = jnp.dot(xj, bre_ref[j], preferred_element_type=F32)
            wim[8:8 + tm, j * 512:(j + 1) * 512] = jnp.dot(xj, bim_ref[j], preferred_element_type=F32)

        def blk(b, carry):
            base = pl.multiple_of(8 + b * 8, 8)
            for cc in range(W // SCAN_COLS):
                cs = pl.ds(cc * SCAN_COLS, SCAN_COLS)
                r, im = _scan_block(wre[pl.ds(base, 8), cs], wim[pl.ds(base, 8), cs], coef_ref, cs, False)
                cr = wre[pl.ds(base - 1, 1), cs]
                ci = wim[pl.ds(base - 1, 1), cs]
                pr = coef_ref[6, :, cs]
                pi = coef_ref[7, :, cs]
                wre[pl.ds(base, 8), cs] = r + pr * cr - pi * ci
                wim[pl.ds(base, 8), cs] = im + pr * ci + pi * cr
            return carry

        lax.fori_loop(0, nblk, blk, 0)
        wre[0:8, :] = wre[tm:tm + 8, :]
        wim[0:8, :] = wim[tm:tm + 8, :]
        sre_ref[...] = wre[8:8 + tm, :]
        sim_ref[...] = wim[8:8 + tm, :]

        for j in range(SUPER):
            yr = jnp.dot(wre[8:8 + tm, j * 512:(j + 1) * 512].astype(BF16), cre_ref[j], preferred_element_type=F32)
            yr += jnp.dot(wim[8:8 + tm, j * 512:(j + 1) * 512].astype(BF16), cim_ref[j], preferred_element_type=F32)
            yraw_ref[:, j * 128:(j + 1) * 128] = yr
        yraw = yraw_ref[...] + dv_ref[...] * xa
        yraw_ref[...] = yraw
        yg = _gelu(yraw)
        q = jnp.dot(yg.astype(BF16), wg_ref[...], preferred_element_type=F32) + bg_ref[...]
        sga, _ = _silu_and_grad(ga_ref[...])
        ya_ref[...] = (yg * jax.nn.sigmoid(q) * sga).astype(BF16)

    nt = T // tm
    r_in, r_out, r_shapes, r_sems, r_args = _rider_specs(rider)
    return pl.pallas_call(
        _ride(body, 10, 4, rider, lambda: pl.program_id(0) == 0, lambda: pl.program_id(0) == nt - 1,
              lambda: pl.program_id(0) == nt // 2),
        name="s5_fwd" + ("" if rider is None else "_ride"),
        grid=(nt,),
        in_specs=[
            pl.BlockSpec((tm, S5_W), lambda i: (i, 0)),
            pl.BlockSpec((tm, S5_W), lambda i: (i, 6)),
            _full((SUPER, 128, 512)), _full((SUPER, 128, 512)),
            _full((SUPER, 512, 128)), _full((SUPER, 512, 128)),
            _full((1, S5_W)), _full((S5_W, S5_W)), _full((1, S5_W)),
            _full((8, 8, W)),
        ] + r_in,
        out_specs=[
            pl.BlockSpec((tm, S5_W), lambda i: (i, 0)),
            pl.BlockSpec((tm, S5_W), lambda i: (i, 0)),
            pl.BlockSpec((tm, W), lambda i: (i, 0)),
            pl.BlockSpec((tm, W), lambda i: (i, 0)),
        ] + r_out,
        out_shape=[
            jax.ShapeDtypeStruct((T, S5_W), BF16),
            jax.ShapeDtypeStruct((T, S5_W), F32),
            jax.ShapeDtypeStruct((T, W), F32),
            jax.ShapeDtypeStruct((T, W), F32),
        ] + r_shapes,
        scratch_shapes=[pltpu.VMEM((tm + 8, W), F32), pltpu.VMEM((tm + 8, W), F32)] + r_sems,
        compiler_params=_cparams(1),
    )(z, z, p["b4re"], p["b4im"], p["c4re"], p["c4im"], p["dvec"], p["wglu"], p["bglu"], p["coef_f"], *r_args)


def _s5_bwd(dy, z, yraw, sre, sim, p, rider=None):
    T = z.shape[0]
    tm = min(256, T)
    nt = T // tm
    nblk = tm // 8
    W = STATE_W
    rev = lambda i: nt - 1 - i

    def body(dya_ref, xa_ref, ga_ref, yraw_ref, sre_ref, sim_ref, hre_ref, him_ref,
             bre_t_ref, bim_t_ref, cre_t_ref, cim_t_ref, dv_ref, wg_ref, wgt_ref, bg_ref, coef_ref,
             dz_ref, dbre_ref, dbim_ref, dcre_ref, dcim_ref, dd_ref, dwg_ref, dbg_ref, da_ref,
             wre, wim, dyr_ref, xa_stage, ga_stage, xa_sem, ga_sem):
        i = pl.program_id(0)
        xa_out = _ColumnWriter(xa_stage, xa_sem, dz_ref, 0, i, nt)
        ga_out = _ColumnWriter(ga_stage, ga_sem, dz_ref, 6 * 512, i, nt)
        dxa_ref, dga_ref = xa_out.slot(), ga_out.slot()

        @pl.when(i == 0)
        def _():
            wre[tm:tm + 8, :] = jnp.zeros((8, W), F32)
            wim[tm:tm + 8, :] = jnp.zeros((8, W), F32)
            for ref in (dbre_ref, dbim_ref, dcre_ref, dcim_ref, dd_ref, dwg_ref, dbg_ref, da_ref):
                ref[...] = jnp.zeros_like(ref)

        xa = xa_ref[...]
        dya = dya_ref[...]
        yg, dgelu = _gelu_and_grad(yraw_ref[...])
        ygb = yg.astype(BF16)
        q = jnp.dot(ygb, wg_ref[...], preferred_element_type=F32) + bg_ref[...]
        sq = jax.nn.sigmoid(q)
        sga, dsga = _silu_and_grad(ga_ref[...])
        dga_ref[...] = (dya * (yg * sq) * dsga).astype(BF16)
        dya0 = dya * sga
        dq = dya0 * yg * sq * (1.0 - sq)
        dqb = dq.astype(BF16)
        dyg = dya0 * sq + jnp.dot(dqb, wgt_ref[...], preferred_element_type=F32)
        dwg_ref[...] += _dot_tn(ygb, dqb)
        dbg_ref[...] += jnp.sum(dq, axis=0, keepdims=True)
        dyraw = dyg * dgelu
        dd_ref[...] += jnp.sum(dyraw * xa, axis=0, keepdims=True)
        dyr_ref[...] = dyraw.astype(BF16)

        for j in range(SUPER):
            dj = dyr_ref[:, j * 128:(j + 1) * 128]
            wre[0:tm, j * 512:(j + 1) * 512] = jnp.dot(dj, cre_t_ref[j], preferred_element_type=F32)
            wim[0:tm, j * 512:(j + 1) * 512] = jnp.dot(dj, cim_t_ref[j], preferred_element_type=F32)

        row0 = lax.broadcasted_iota(jnp.int32, (8, SCAN_COLS), 0) == 0
        head_on = (i < nt - 1).astype(F32)

        def one_block(base, first):
            for cc in range(W // SCAN_COLS):
                cs = pl.ds(cc * SCAN_COLS, SCAN_COLS)
                r, im = _scan_block(wre[pl.ds(base, 8), cs], wim[pl.ds(base, 8), cs], coef_ref, cs, True)
                cr = wre[pl.ds(base + 8, 1), cs]
                ci = wim[pl.ds(base + 8, 1), cs]
                pr = coef_ref[6, :, cs]
                pi = coef_ref[7, :, cs]
                r, im = r + pr * cr - pi * ci, im + pr * ci + pi * cr
                wre[pl.ds(base, 8), cs] = r
                wim[pl.ds(base, 8), cs] = im
                if first:
                    pre = hre_ref[7:8, cs] * head_on
                    pim = him_ref[7:8, cs] * head_on
                else:
                    pre = sre_ref[pl.ds(base - 1, 1), cs]
                    pim = sim_ref[pl.ds(base - 1, 1), cs]
                spr = jnp.where(row0, pre, pltpu.roll(sre_ref[pl.ds(base, 8), cs], 1, 0))
                spi = jnp.where(row0, pim, pltpu.roll(sim_ref[pl.ds(base, 8), cs], 1, 0))
                da_ref[0, :, cs] += r * spr + im * spi
                da_ref[1, :, cs] += im * spr - r * spi

        def blk(b, carry):
            one_block(pl.multiple_of((nblk - 1 - b) * 8, 8), False)
            return carry

        lax.fori_loop(0, nblk - 1, blk, 0)
        one_block(0, True)
        wre[tm:tm + 8, :] = wre[0:8, :]
        wim[tm:tm + 8, :] = wim[0:8, :]

        xab = xa.astype(BF16)
        for j in range(SUPER):
            cols = slice(j * 512, (j + 1) * 512)
            gre = wre[0:tm, cols].astype(BF16)
            gim = wim[0:tm, cols].astype(BF16)
            xj = xab[:, j * 128:(j + 1) * 128]
            dj = dyr_ref[:, j * 128:(j + 1) * 128]
            dbre_ref[j] += _dot_tn(xj, gre)
            dbim_ref[j] += _dot_tn(xj, gim)
            dcre_ref[j] += _dot_tn(sre_ref[:, cols], dj)
            dcim_ref[j] += _dot_tn(sim_ref[:, cols], dj)
            dxj = jnp.dot(gre, bre_t_ref[j], preferred_element_type=F32)
            dxj += jnp.dot(gim, bim_t_ref[j], preferred_element_type=F32)
            dxj += dyraw[:, j * 128:(j + 1) * 128] * dv_ref[:, j * 128:(j + 1) * 128]
            dxa_ref[:, j * 128:(j + 1) * 128] = dxj.astype(BF16)
        xa_out.send(rev(i) * tm)
        ga_out.send(rev(i) * tm)

    acc = lambda shape: _full(shape)
    hb = tm // 8
    r_in, r_out, r_shapes, r_sems, r_args = _rider_specs(rider)
    stages, stage_sems = _stage_scratch(tm, (S5_W, S5_W))
    return pl.pallas_call(
        _ride(body, 17, 9, rider, lambda: pl.program_id(0) == 0, lambda: pl.program_id(0) == nt - 1),
        name="s5_bwd" + ("" if rider is None else "_ride"),
        grid=(nt,),
        in_specs=[
            pl.BlockSpec((tm, S5_W), lambda i: (rev(i), 0)),
            pl.BlockSpec((tm, S5_W), lambda i: (rev(i), 0)),
            pl.BlockSpec((tm, S5_W), lambda i: (rev(i), 6)),
            pl.BlockSpec((tm, S5_W), lambda i: (rev(i), 0)),
            pl.BlockSpec((tm, W), lambda i: (rev(i), 0)),
            pl.BlockSpec((tm, W), lambda i: (rev(i), 0)),
            pl.BlockSpec((8, W), lambda i: (jnp.maximum(rev(i) * hb - 1, 0), 0)),
            pl.BlockSpec((8, W), lambda i: (jnp.maximum(rev(i) * hb - 1, 0), 0)),
            _full((SUPER, 512, 128)), _full((SUPER, 512, 128)),
            _full((SUPER, 128, 512)), _full((SUPER, 128, 512)),
            _full((1, S5_W)), _full((S5_W, S5_W)), _full((S5_W, S5_W)), _full((1, S5_W)),
            _full((8, 8, W)),
        ] + r_in,
        out_specs=[
            pl.BlockSpec(memory_space=pl.ANY),
            acc((SUPER, 128, 512)), acc((SUPER, 128, 512)),
            acc((SUPER, 512, 128)), acc((SUPER, 512, 128)),
            acc((1, S5_W)), acc((S5_W, S5_W)), acc((1, S5_W)), acc((2, 8, W)),
        ] + r_out,
        out_shape=[
            jax.ShapeDtypeStruct((T, IN_COLS), BF16),
            jax.ShapeDtypeStruct((SUPER, 128, 512), F32), jax.ShapeDtypeStruct((SUPER, 128, 512), F32),
            jax.ShapeDtypeStruct((SUPER, 512, 128), F32), jax.ShapeDtypeStruct((SUPER, 512, 128), F32),
            jax.ShapeDtypeStruct((1, S5_W), F32), jax.ShapeDtypeStruct((S5_W, S5_W), F32),
            jax.ShapeDtypeStruct((1, S5_W), F32), jax.ShapeDtypeStruct((2, 8, W), F32),
        ] + r_shapes,
        scratch_shapes=[pltpu.VMEM((tm + 8, W), F32), pltpu.VMEM((tm + 8, W), F32), pltpu.VMEM((tm, S5_W), BF16)]
        + stages + stage_sems + r_sems,
        compiler_params=_cparams(1),
    )(dy, z, z, yraw, sre, sim, sre, sim,
      p["b4re_t"], p["b4im_t"], p["c4re_t"], p["c4im_t"], p["dvec"], p["wglu"], p["wglu_t"], p["bglu"], p["coef_r"],
      *r_args)


def _ln_fwd(vf, lng, lnb):
    mu = jnp.mean(vf, axis=-1, keepdims=True)
    d = vf - mu
    rstd = lax.rsqrt(jnp.mean(d * d, axis=-1, keepdims=True) + LN_EPS)
    xh = d * rstd
    return xh, rstd, xh * lng + lnb


def _col_block(tm, b):
    return pl.BlockSpec((tm, 512), lambda i: (i, b))


def _ln_halves(vf0, vf1):
    mu = (jnp.sum(vf0, axis=-1, keepdims=True) + jnp.sum(vf1, axis=-1, keepdims=True)) * (1.0 / SGU_W)
    d0, d1 = vf0 - mu, vf1 - mu
    var = (jnp.sum(d0 * d0, axis=-1, keepdims=True) + jnp.sum(d1 * d1, axis=-1, keepdims=True)) * (1.0 / SGU_W)
    rstd = lax.rsqrt(var + LN_EPS)
    return d0 * rstd, d1 * rstd, rstd


def _sgu_fwd(z, ws, bsf, lng, lnb):
    T = z.shape[0]
    tm = min(512, T)

    def body(u0, u1, v0, v1, g0, g1, ws_ref, bs_ref, lng_ref, lnb_ref, yb_ref, vn_ref):
        for c in range(tm // CHUNK):
            rows = slice(c * CHUNK, (c + 1) * CHUNK)
            xh0, xh1, _ = _ln_halves(_gelu(v0[rows, :]), _gelu(v1[rows, :]))
            vn_ref[:, 0:512] = (xh0 * lng_ref[:, 0:512] + lnb_ref[:, 0:512]).astype(BF16)
            vn_ref[:, 512:1024] = (xh1 * lng_ref[:, 512:1024] + lnb_ref[:, 512:1024]).astype(BF16)
            for half, (u_ref, g_ref) in enumerate(((u0, g0), (u1, g1))):
                sg, _ = _silu_and_grad(g_ref[rows, :])
                m = _gelu(u_ref[rows, :]) * sg
                for hh in range(SGU_HEADS // 2):
                    h = half * (SGU_HEADS // 2) + hh
                    cols = slice(h * 128, (h + 1) * 128)
                    s = jnp.dot(ws_ref[h], vn_ref[:, cols], preferred_element_type=F32) + bs_ref[:, cols]
                    yb_ref[rows, cols] = (m[:, hh * 128:(hh + 1) * 128] * s).astype(BF16)

    return pl.pallas_call(
        body,
        name="sgu_fwd",
        grid=(T // tm,),
        in_specs=[_col_block(tm, b) for b in (1, 2, 3, 4, 7, 8)] + [
            _full((SGU_HEADS, CHUNK, CHUNK)), _full((CHUNK, SGU_W)), _full((1, SGU_W)), _full((1, SGU_W)),
        ],
        out_specs=pl.BlockSpec((tm, SGU_W), lambda i: (i, 0)),
        out_shape=jax.ShapeDtypeStruct((T, SGU_W), BF16),
        scratch_shapes=[pltpu.VMEM((CHUNK, SGU_W), BF16)],
        compiler_params=_cparams(1),
    )(z, z, z, z, z, z, ws, bsf, lng, lnb)


def _sgu_bwd(dy, z, dz, ws, ws_t, bsf, lng, lnb, rider=None):
    T = z.shape[0]
    tm = min(512, T)
    HH = SGU_HEADS // 2

    def body(u0, u1, v0, v1, g0, g1, dy0, dy1, ws_ref, wst_ref, bs_ref, lng_ref, lnb_ref, dz_in,
             dz_ref, dws_ref, dbs_ref, dlng_ref, dlnb_ref, vn_ref, dvn_ref, *stage):
        step = pl.program_id(0)
        outs = [_ColumnWriter(stage[k], stage[3 + k], dz_ref, col, step, T // tm)
                for k, col in enumerate((512, 1536, 3584))]
        du_ref, dv_ref, dgb_ref = (o.slot() for o in outs)

        @pl.when(step == 0)
        def _():
            for ref in (dws_ref, dbs_ref, dlng_ref, dlnb_ref):
                ref[...] = jnp.zeros_like(ref)

        for c in range(tm // CHUNK):
            rows = slice(c * CHUNK, (c + 1) * CHUNK)
            vf0, dgv0 = _gelu_and_grad(v0[rows, :])
            vf1, dgv1 = _gelu_and_grad(v1[rows, :])
            xh0, xh1, rstd = _ln_halves(vf0, vf1)
            vn_ref[:, 0:512] = (xh0 * lng_ref[:, 0:512] + lnb_ref[:, 0:512]).astype(BF16)
            vn_ref[:, 512:1024] = (xh1 * lng_ref[:, 512:1024] + lnb_ref[:, 512:1024]).astype(BF16)
            for half, (u_ref, g_ref, dy_ref) in enumerate(((u0, g0, dy0), (u1, g1, dy1))):
                ug, dgu = _gelu_and_grad(u_ref[rows, :])
                sg, dsg = _silu_and_grad(g_ref[rows, :])
                dyb = dy_ref[rows, :]
                dyb0 = dyb * sg
                ds_half = dyb0 * ug
                du_scale = dyb0 * dgu
                dg_scale = dyb * ug * dsg
                for hh in range(HH):
                    h = half * HH + hh
                    cols = slice(h * 128, (h + 1) * 128)
                    lc = slice(hh * 128, (hh + 1) * 128)
                    s = jnp.dot(ws_ref[h], vn_ref[:, cols], preferred_element_type=F32) + bs_ref[:, cols]
                    du_ref[rows, cols] = (du_scale[:, lc] * s).astype(BF16)
                    dgb_ref[rows, cols] = (dg_scale[:, lc] * s).astype(BF16)
                    ds = ds_half[:, lc]
                    dbs_ref[:, cols] += ds
                    dsb = ds.astype(BF16)
                    dws_ref[h] += _dot_nt(dsb, vn_ref[:, cols])
                    dvn_ref[:, cols] = jnp.dot(wst_ref[h], dsb, preferred_element_type=F32)
            dvn0 = dvn_ref[:, 0:512]
            dvn1 = dvn_ref[:, 512:1024]
            dlnb_ref[:, 0:512] += jnp.sum(dvn0, axis=0, keepdims=True)
            dlnb_ref[:, 512:1024] += jnp.sum(dvn1, axis=0, keepdims=True)
            dlng_ref[:, 0:512] += jnp.sum(dvn0 * xh0, axis=0, keepdims=True)
            dlng_ref[:, 512:1024] += jnp.sum(dvn1 * xh1, axis=0, keepdims=True)
            dxh0 = dvn0 * lng_ref[:, 0:512]
            dxh1 = dvn1 * lng_ref[:, 512:1024]
            m1 = (jnp.sum(dxh0, axis=-1, keepdims=True) + jnp.sum(dxh1, axis=-1, keepdims=True)) * (1.0 / SGU_W)
            m2 = (jnp.sum(dxh0 * xh0, axis=-1, keepdims=True) + jnp.sum(dxh1 * xh1, axis=-1, keepdims=True)) * (1.0 / SGU_W)
            dv_ref[rows, 0:512] = (rstd * (dxh0 - m1 - xh0 * m2) * dgv0).astype(BF16)
            dv_ref[rows, 512:1024] = (rstd * (dxh1 - m1 - xh1 * m2) * dgv1).astype(BF16)
        for o in outs:
            o.send(step * tm)

    anyspec = pl.BlockSpec(memory_space=pl.ANY)
    r_in, r_out, r_shapes, r_sems, r_args = _rider_specs(rider)
    stages, stage_sems = _stage_scratch(tm, (SGU_W, SGU_W, SGU_W))
    return pl.pallas_call(
        _ride(body, 14, 5, rider, lambda: pl.program_id(0) == 0, lambda: pl.program_id(0) == T // tm - 1),
        name="sgu_bwd" + ("" if rider is None else "_ride"),
        grid=(T // tm,),
        in_specs=[_col_block(tm, b) for b in (1, 2, 3, 4, 7, 8)] + [_col_block(tm, 1), _col_block(tm, 2)] + [
            _full((SGU_HEADS, CHUNK, CHUNK)), _full((SGU_HEADS, CHUNK, CHUNK)),
            _full((CHUNK, SGU_W)), _full((1, SGU_W)), _full((1, SGU_W)), anyspec,
        ] + r_in,
        out_specs=[anyspec,
                   _full((SGU_HEADS, CHUNK, CHUNK)), _full((CHUNK, SGU_W)), _full((1, SGU_W)), _full((1, SGU_W))] + r_out,
        input_output_aliases={13: 0},
        out_shape=[
            jax.ShapeDtypeStruct((T, IN_COLS), BF16),
            jax.ShapeDtypeStruct((SGU_HEADS, CHUNK, CHUNK), F32), jax.ShapeDtypeStruct((CHUNK, SGU_W), F32),
            jax.ShapeDtypeStruct((1, SGU_W), F32), jax.ShapeDtypeStruct((1, SGU_W), F32),
        ] + r_shapes,
        scratch_shapes=[pltpu.VMEM((CHUNK, SGU_W), BF16), pltpu.VMEM((CHUNK, SGU_W), F32)] + stages + stage_sems + r_sems,
        compiler_params=_cparams(1),
    )(z, z, z, z, z, z, dy, dy, ws, ws_t, bsf, lng, lnb, dz, *r_args)


def _pool_den(first_row, n):
    return (lax.broadcasted_iota(jnp.int32, (n, 1), 0) + first_row + 1).astype(F32)


def _pool_p(ext, xc, pos, tm):
    w2 = ext + pltpu.roll(ext, 1, 0)
    w4 = w2 + pltpu.roll(w2, 2, 0)
    w8 = w4 + pltpu.roll(w4, 4, 0)
    w16 = w8 + pltpu.roll(w8, 8, 0)
    out = []
    for g, (w, ws) in enumerate(zip(POOL_WINDOWS, (w2, w4, w8, w16))):
        cols = slice(g * 128, (g + 1) * 128)
        mean = ws[POOL_HALO:POOL_HALO + tm, cols] / jnp.minimum(pos, float(w))
        out.append(mean - xc[:, cols])
    return out


def _pool_fwd(z, wp, scale):
    T = z.shape[0]
    tm = min(512, T)
    hb = tm // POOL_HALO

    def body(xc_ref, hx_ref, gc_ref, wp_ref, sc_ref, yc_ref):
        i = pl.program_id(0)
        xc = xc_ref[...]
        halo = hx_ref[...] * (i > 0).astype(F32)
        ext = jnp.concatenate([halo, xc], axis=0)
        ps = _pool_p(ext, xc, _pool_den(i * tm, tm), tm)
        sg, _ = _silu_and_grad(gc_ref[...])
        for g in range(4):
            cols = slice(g * 128, (g + 1) * 128)
            pw = _dot(ps[g], wp_ref[g])
            yc_ref[:, cols] = (pw * sc_ref[:, cols] * sg[:, cols]).astype(BF16)

    return pl.pallas_call(
        body,
        name="pool_fwd",
        grid=(T // tm,),
        in_specs=[
            _col_block(tm, 5),
            pl.BlockSpec((POOL_HALO, 512), lambda i: (jnp.maximum(i * hb - 1, 0), 5)),
            _col_block(tm, 9),
            _full((4, 128, 128)), _full((1, POOL_W)),
        ],
        out_specs=pl.BlockSpec((tm, POOL_W), lambda i: (i, 0)),
        out_shape=jax.ShapeDtypeStruct((T, POOL_W), BF16),
        compiler_params=_cparams(1),
    )(z, z, z, wp, scale)


def _pool_bwd(dy, z, dz, wp, wp_t, scale):
    T = z.shape[0]
    tm = min(512, T)
    nt = T // tm
    hb = tm // POOL_HALO
    last_hb = T // POOL_HALO - 1
    L = tm + POOL_HALO

    def body(xc_ref, hx_ref, gc_ref, gn_ref, dyc_ref, dyn_ref, wp_ref, wpt_ref, sc_ref, dz_in,
             dz_ref, dwp_ref, dsc_ref, xc_stage, gc_stage, xc_sem, gc_sem):
        i = pl.program_id(0)
        xc_out = _ColumnWriter(xc_stage, xc_sem, dz_ref, 5 * 512, i, nt)
        gc_out = _ColumnWriter(gc_stage, gc_sem, dz_ref, 9 * 512, i, nt)
        dxc_ref, dgc_ref = xc_out.slot(), gc_out.slot()

        @pl.when(i == 0)
        def _():
            dwp_ref[...] = jnp.zeros_like(dwp_ref)
            dsc_ref[...] = jnp.zeros_like(dsc_ref)

        xc = xc_ref[...]
        halo = hx_ref[...] * (i > 0).astype(F32)
        pos = _pool_den(i * tm, tm)
        ps = _pool_p(jnp.concatenate([halo, xc], axis=0), xc, pos, tm)
        sg, dsg = _silu_and_grad(gc_ref[...])
        dyc = dyc_ref[...]
        dyc0 = dyc * sg
        dpw = dyc0 * sc_ref[...]
        sgn, _ = _silu_and_grad(gn_ref[...])
        dpwn = dyn_ref[...] * sgn * sc_ref[...] * (i < nt - 1).astype(F32)
        posn = _pool_den((i + 1) * tm, POOL_HALO)
        dps, qs = [], []
        for g, w in enumerate(POOL_WINDOWS):
            cols = slice(g * 128, (g + 1) * 128)
            pw = _dot(ps[g], wp_ref[g])
            dgc_ref[:, cols] = (dyc[:, cols] * pw * sc_ref[:, cols] * dsg[:, cols]).astype(BF16)
            dsc_ref[:, cols] += jnp.sum(dyc0[:, cols] * pw, axis=0, keepdims=True)
            dwp_ref[g] += _dot_tn(ps[g], dpw[:, cols])
            dp = _dot(dpw[:, cols], wpt_ref[g])
            dpn = _dot(dpwn[:, cols], wpt_ref[g])
            dps.append(dp)
            qs.append(jnp.concatenate([dp / jnp.minimum(pos, float(w)), dpn / jnp.minimum(posn, float(w))], axis=0))
        ext = jnp.concatenate(qs, axis=1)
        f2 = ext + pltpu.roll(ext, L - 1, 0)
        f4 = f2 + pltpu.roll(f2, L - 2, 0)
        f8 = f4 + pltpu.roll(f4, L - 4, 0)
        f16 = f8 + pltpu.roll(f8, L - 8, 0)
        for g, f in enumerate((f2, f4, f8, f16)):
            cols = slice(g * 128, (g + 1) * 128)
            dxc_ref[:, cols] = (f[0:tm, cols] - dps[g]).astype(BF16)
        xc_out.send(i * tm)
        gc_out.send(i * tm)

    nxt = lambda i: jnp.minimum((i + 1) * hb, last_hb)
    anyspec = pl.BlockSpec(memory_space=pl.ANY)
    stages, stage_sems = _stage_scratch(tm, (POOL_W, POOL_W))
    return pl.pallas_call(
        body,
        name="pool_bwd",
        grid=(nt,),
        in_specs=[
            _col_block(tm, 5),
            pl.BlockSpec((POOL_HALO, 512), lambda i: (jnp.maximum(i * hb - 1, 0), 5)),
            _col_block(tm, 9),
            pl.BlockSpec((POOL_HALO, 512), lambda i: (nxt(i), 9)),
            _col_block(tm, 3),
            pl.BlockSpec((POOL_HALO, 512), lambda i: (nxt(i), 3)),
            _full((4, 128, 128)), _full((4, 128, 128)), _full((1, POOL_W)), anyspec,
        ],
        out_specs=[anyspec, _full((4, 128, 128)), _full((1, POOL_W))],
        input_output_aliases={9: 0},
        out_shape=[
            jax.ShapeDtypeStruct((T, IN_COLS), BF16),
            jax.ShapeDtypeStruct((4, 128, 128), F32), jax.ShapeDtypeStruct((1, POOL_W), F32),
        ],
        scratch_shapes=stages + stage_sems,
        compiler_params=_cparams(1),
    )(z, z, z, z, dy, dy, wp, wp_t, scale, dz)


def _row_tile(rows, cols):
    tr = 8
    while tr * 2 * cols * 4 <= 2 * 1024 * 1024 and rows % (tr * 2) == 0:
        tr *= 2
    return tr


def _add_own_half(part, recv, cidx):
    _, _, R2, C = part.shape
    tr = _row_tile(R2, C)

    def body(c_ref, a_ref, r_ref, o_ref):
        o_ref[...] = (a_ref[...] + r_ref[...]).astype(BF16)

    return pl.pallas_call(
        body,
        name="add_own_half",
        grid_spec=pltpu.PrefetchScalarGridSpec(
            num_scalar_prefetch=1,
            grid=(N_CHIPS, R2 // tr),
            in_specs=[
                pl.BlockSpec((None, None, tr, C), lambda j, i, c: (j, c[0], i, 0)),
                pl.BlockSpec((None, tr, C), lambda j, i, c: (j, i, 0)),
            ],
            out_specs=pl.BlockSpec((None, tr, C), lambda j, i, c: (j, i, 0)),
        ),
        out_shape=jax.ShapeDtypeStruct((N_CHIPS, R2, C), BF16),
        compiler_params=_cparams(2),
    )(cidx, part, recv)


def _add2(a, b):
    R, C = a.shape
    tr = _row_tile(R, C)

    def body(a_ref, b_ref, o_ref):
        o_ref[...] = a_ref[...] + b_ref[...]

    spec = pl.BlockSpec((tr, C), lambda i: (i, 0))
    return pl.pallas_call(
        body, name="add2", grid=(R // tr,), in_specs=[spec, spec], out_specs=spec,
        out_shape=jax.ShapeDtypeStruct((R, C), F32), compiler_params=_cparams(1),
    )(a, b)


def _sum_chips(parts):
    _, R, C = parts.shape
    tr = _row_tile(R, N_CHIPS * C)

    def body(p_ref, o_ref):
        p = [p_ref[j].astype(F32) for j in range(N_CHIPS)]
        o_ref[...] = ((p[0] + p[1]) + p[2]) + p[3]

    return pl.pallas_call(
        body, name="sum_chips", grid=(R // tr,),
        in_specs=[pl.BlockSpec((N_CHIPS, tr, C), lambda i: (0, i, 0))],
        out_specs=pl.BlockSpec((tr, C), lambda i: (i, 0)),
        out_shape=jax.ShapeDtypeStruct((R, C), F32), compiler_params=_cparams(1),
    )(parts)


def _adamw_math(w, g, m, v):
    m = ADAM_B1 * m + (1.0 - ADAM_B1) * g
    v = ADAM_B2 * v + (1.0 - ADAM_B2) * (g * g)
    m_hat = m / (1.0 - ADAM_B1 ** ADAM_STEP)
    v_hat = v / (1.0 - ADAM_B2 ** ADAM_STEP)
    delta = -ADAM_LR * (m_hat / (jnp.sqrt(v_hat) + ADAM_EPS) + ADAM_WD * w)
    return delta, m, v


def _adamw(w, g, m, v):
    R, C = w.shape
    tr = _row_tile(R, C)

    def body(w_ref, g_ref, m_ref, v_ref, d_ref, mo_ref, vo_ref):
        d_ref[...], mo_ref[...], vo_ref[...] = _adamw_math(w_ref[...], g_ref[...], m_ref[...], v_ref[...])

    spec = pl.BlockSpec((tr, C), lambda i: (i, 0))
    shp = jax.ShapeDtypeStruct((R, C), F32)
    return pl.pallas_call(
        body, name="adamw", grid=(R // tr,), in_specs=[spec] * 4, out_specs=[spec] * 3,
        out_shape=[shp] * 3, compiler_params=_cparams(1),
    )(w, g, m, v)


def _adamw_halves(w, mine, theirs, m, v, cidx, rider=None):
    _, _, R2, C = w.shape
    tr = _row_tile(R2, C)
    nr = R2 // tr

    def body(c_ref, w_ref, a0_ref, b0_ref, a1_ref, b1_ref, m_ref, v_ref, g_ref, d_ref, mo_ref, vo_ref):
        own = pl.program_id(1) == c_ref[0]
        g0 = jnp.where(own, a0_ref[...], b0_ref[...])
        g1 = jnp.where(own, a1_ref[...], b1_ref[...])
        g = jnp.where(pl.program_id(0) == 0, g0, g1)
        g_ref[...] = g
        d_ref[...], mo_ref[...], vo_ref[...] = _adamw_math(w_ref[...], g, m_ref[...], v_ref[...])

    full = pl.BlockSpec((None, None, tr, C), lambda l, h, i, c: (l, h, i, 0))

    def pick(layer, mine_side):
        def index(l, h, i, c):
            used = (l == layer) & ((h == c[0]) == mine_side)
            return (jnp.where(used, i, 0), 0)
        return pl.BlockSpec((tr, C), index)

    shp = jax.ShapeDtypeStruct(w.shape, F32)
    r_in, r_out, r_shapes, r_sems, r_args = _rider_specs(rider)
    last = lambda: (pl.program_id(0) == 1) & (pl.program_id(1) == 1) & (pl.program_id(2) == nr - 1)
    first = lambda: (pl.program_id(0) == 0) & (pl.program_id(1) == 0) & (pl.program_id(2) == 0)
    return pl.pallas_call(
        _ride(body, 8, 4, rider, first, last),
        name="adamw_halves" + ("" if rider is None else "_ride"),
        grid_spec=pltpu.PrefetchScalarGridSpec(
            num_scalar_prefetch=1, grid=(2, 2, nr),
            in_specs=[full, pick(0, True), pick(0, False), pick(1, True), pick(1, False), full, full] + r_in,
            out_specs=[full] * 4 + r_out,
            scratch_shapes=r_sems,
        ),
        out_shape=[shp] * 4 + r_shapes,
        compiler_params=_cparams(3),
    )(cidx, w, mine[0], theirs[0], mine[1], theirs[1], m, v, *r_args)


_ANY = pl.BlockSpec(memory_space=pl.ANY)


def _mesh_pos():
    return lax.axis_index("x"), lax.axis_index("y"), lax.axis_index("c")


def _other_chips(x, y):
    return [(2 * x + (1 - y), x, 1 - y), (2 * (1 - x) + y, 1 - x, y), (2 * (1 - x) + (1 - y), 1 - x, 1 - y)]


def _gathered_shapes(shards):
    return [jax.ShapeDtypeStruct((2, N_CHIPS) + s.shape[1:], s.dtype) for s in shards]


def _gather_sems(n):
    return [pltpu.SemaphoreType.DMA((2 * n,)), pltpu.SemaphoreType.DMA((6 * n,)), pltpu.SemaphoreType.DMA((6 * n,))]


def _gather_steps(ins, outs, lsem, ssem, rsem):
    n = len(ins)
    x, y, c = _mesh_pos()
    me = 2 * x + y
    sib = (x, y, 1 - c)
    chips = _other_chips(x, y)

    def ici(k, d):
        return pltpu.make_async_remote_copy(
            ins[k].at[c], outs[k].at[c, me], ssem.at[6 * k + d], rsem.at[6 * k + d],
            device_id=(chips[d][1], chips[d][2], c), device_id_type=MESH_ID)

    def landed(k, d):
        return pltpu.make_async_remote_copy(
            ins[k].at[c], outs[k].at[c, chips[d][0]], ssem.at[6 * k + d], rsem.at[6 * k + d],
            device_id=sib, device_id_type=MESH_ID)

    def fwd(k, d, half):
        return pltpu.make_async_remote_copy(
            outs[k].at[half, chips[d][0]], outs[k].at[half, chips[d][0]], ssem.at[6 * k + 3 + d],
            rsem.at[6 * k + 3 + d], device_id=sib, device_id_type=MESH_ID)

    def local(k, h):
        return pltpu.make_async_copy(ins[k].at[h], outs[k].at[h, me], lsem.at[2 * k + h])

    def start():
        for k in range(n):
            for h in range(2):
                local(k, h).start()
            for d in range(3):
                ici(k, d).start()

    def mid():
        for d in range(3):
            for k in range(n):
                landed(k, d).wait_recv()
                fwd(k, d, c).start()

    def end():
        for d in range(3):
            for k in range(n):
                fwd(k, d, 1 - c).wait_recv()
        for k in range(n):
            for d in range(3):
                ici(k, d).wait_send()
                fwd(k, d, c).wait_send()
            for h in range(2):
                local(k, h).wait()

    return start, mid, end


def _gather_rider(shards):
    return _Rider(shards, _gathered_shapes(shards), _gather_sems(len(shards)), _gather_steps)


def _pair_rider(arrs, other_half):
    n = len(arrs)

    def steps(ins, outs, ssem, rsem):
        x, y, c = _mesh_pos()

        def copy(k):
            return pltpu.make_async_remote_copy(ins[k].at[:, 1 - c] if other_half else ins[k], outs[k], ssem.at[k],
                                                rsem.at[k], device_id=(x, y, 1 - c), device_id_type=MESH_ID)

        def start():
            for k in range(n):
                copy(k).start()

        def end():
            for k in range(n):
                copy(k).wait()

        return start, end

    shapes = [jax.ShapeDtypeStruct(a.shape[:1] + a.shape[2:] if other_half else a.shape, a.dtype) for a in arrs]
    return _Rider(arrs, shapes, [pltpu.SemaphoreType.DMA((n,)), pltpu.SemaphoreType.DMA((n,))], steps)


def _chip_rider(arrs, broadcast):
    n = len(arrs)

    def steps(ins, outs, lsem, ssem, rsem):
        x, y, c = _mesh_pos()
        me = 2 * x + y

        def copies():
            cps = [pltpu.make_async_copy(ins[k] if broadcast else ins[k].at[me], outs[k].at[me], lsem.at[k])
                   for k in range(n)]
            for k in range(n):
                for d, (j, tx, ty) in enumerate(_other_chips(x, y)):
                    cps.append(pltpu.make_async_remote_copy(
                        ins[k] if broadcast else ins[k].at[j], outs[k].at[me], ssem.at[3 * k + d], rsem.at[3 * k + d],
                        device_id=(tx, ty, c), device_id_type=MESH_ID))
            return cps

        def start():
            for cp in copies():
                cp.start()

        def end():
            for cp in copies():
                cp.wait()

        return start, end

    shapes = [jax.ShapeDtypeStruct(((N_CHIPS,) + a.shape) if broadcast else a.shape, a.dtype) for a in arrs]
    sems = [pltpu.SemaphoreType.DMA((n,)), pltpu.SemaphoreType.DMA((3 * n,)), pltpu.SemaphoreType.DMA((3 * n,))]
    return _Rider(arrs, shapes, sems, steps)


def _run_rider(name, rider):
    n, m = len(rider.arrs), len(rider.out_shapes)

    def body(*refs):
        for step in rider.steps(refs[:n], refs[n:n + m], *refs[n + m:]):
            step()

    return pl.pallas_call(
        body, name=name, in_specs=[_ANY] * n, out_specs=[_ANY] * m, out_shape=rider.out_shapes,
        scratch_shapes=rider.sems,
    )(*rider.arrs)


SMALL = ("norm_g", "lam_re", "lam_im", "b_re", "b_im", "c_re", "c_im", "d_skip", "log_dt", "b_glu", "ln_g", "ln_b",
         "w_s", "b_s", "w_pool", "pool_scale", "final_g")
BIG = ("w_in", "w_glu", "w_out")
WEIGHTS = ("norm_g", "w_in", "lam_re", "lam_im", "b_re", "b_im", "c_re", "c_im", "d_skip", "log_dt", "w_glu", "b_glu",
           "ln_g", "ln_b", "w_s", "b_s", "w_pool", "pool_scale", "w_out", "final_g")
PACK_UNIT = 8 * 128
PACK_ROWS = 1024


def _pack(arrs):
    parts, total = [], 0
    for a in arrs:
        f = a.reshape(-1).astype(F32)
        pad = (-f.shape[0]) % PACK_UNIT
        parts.append(jnp.pad(f, (0, pad)) if pad else f)
        total += f.shape[0] + pad
    tail = (-total) % (PACK_ROWS * 128)
    if tail:
        parts.append(jnp.zeros((tail,), F32))
    return jnp.concatenate(parts).reshape(-1, 128)


def _unpack(buf, like):
    flat = buf.reshape(-1)
    out, off = [], 0
    for a in like:
        n = math.prod(a.shape)
        out.append(flat[off:off + n].reshape(a.shape))
        off += n + ((-n) % PACK_UNIT)
    return out


def _layer_params(l, wt, g_glu):
    a_re, a_im, bb_re, bb_im = _s5_prep(wt["lam_re"][l], wt["lam_im"][l], wt["b_re"][l], wt["b_im"][l], wt["log_dt"][l])
    b4re, b4im = _block_diag_in(bb_re), _block_diag_in(bb_im)
    c4re, c4im = _block_diag_out(wt["c_re"][l]), _block_diag_out(-wt["c_im"][l])
    tr = lambda m: jnp.swapaxes(m, 1, 2).astype(BF16)
    causal = jnp.tril(jnp.ones((CHUNK, CHUNK), dtype=bool))
    ws = jnp.where(causal[None], wt["w_s"][l], 0.0)
    wglu = g_glu[l].reshape(S5_W, S5_W)
    return dict(
        b4re=b4re.astype(BF16), b4im=b4im.astype(BF16), c4re=c4re.astype(BF16), c4im=c4im.astype(BF16),
        b4re_t=tr(b4re), b4im_t=tr(b4im), c4re_t=tr(c4re), c4im_t=tr(c4im),
        dvec=wt["d_skip"][l].reshape(1, S5_W), wglu=wglu, wglu_t=wglu.T, bglu=wt["b_glu"][l].reshape(1, S5_W),
        coef_f=_scan_coefs(a_re, a_im, False), coef_r=_scan_coefs(a_re, a_im, True),
        ws=ws.astype(BF16), ws_t=tr(ws),
        bsf=jnp.broadcast_to(wt["b_s"][l][:, None, :], (SGU_HEADS, CHUNK, CHUNK)).transpose(2, 0, 1).reshape(CHUNK, SGU_W),
        lng=wt["ln_g"][l].reshape(1, SGU_W), lnb=wt["ln_b"][l].reshape(1, SGU_W),
        wp=wt["w_pool"][l].astype(BF16), wp_t=tr(wt["w_pool"][l]), scale=wt["pool_scale"][l].reshape(1, POOL_W),
        norm_g=wt["norm_g"][l].reshape(1, D_MODEL),
    )


def _local_step(x0, tgt, wt, g_in0, rest, rest_gathered, cidx=None, order=None):
    dist = cidx is not None
    xs, saved, params = [x0], [], []
    for l in range(DEPTH):
        norm_g = wt["norm_g"][l].reshape(1, D_MODEL)
        out_rider = None
        if l == 0 and not rest_gathered:
            w_in1, w_glu_b, w_out_b = rest
            h = _rms_h(xs[-1], norm_g)
            z, g_in0, g_in1, g_glu = _inproj_first(h, g_in0, order, [w_in1, w_glu_b])
            out_rider = _gather_rider([w_out_b])
        elif l == 0:
            g_in1, g_glu, g_out = rest
            z, h = _inproj(xs[-1], norm_g, g_in0)
        else:
            z, h = _inproj(xs[-1], norm_g, g_in1)
        g_in = (g_in0, g_in1)
        p = _layer_params(l, wt, g_glu)
        params.append(p)
        ya, yraw, sre, sim, *gathered = _s5_fwd(z, p, out_rider)
        if gathered:
            (g_out,) = gathered
        yb = _sgu_fwd(z, p["ws"], p["bsf"], p["lng"], p["lnb"])
        yc = _pool_fwd(z, p["wp"], p["scale"])
        xn, y = _outproj(ya, yb, yc, g_out[l].reshape(D_MODEL, D_MODEL), xs[-1])
        xs.append(xn)
        saved.append((z, h, yraw, sre, sim, y))

    dx, loss, dfg = _loss_head(xs[-1], wt["final_g"].reshape(1, D_MODEL), tgt)

    gr = {k: [None] * DEPTH for k in WEIGHTS if k != "final_g"}
    mine, theirs, chip_sum = [None] * DEPTH, [None] * DEPTH, None
    halves = lambda a, rows: a.reshape(N_CHIPS, 2, rows // 2, a.shape[-1])
    for l in reversed(range(DEPTH)):
        p = params[l]
        z, h, yraw, sre, sim, y = saved[l]
        w_out = g_out[l].reshape(D_MODEL, D_MODEL)
        dy = _outproj_bwd_dy(dx, w_out)
        gr["w_out"][l] = _outproj_bwd_dw(y, dx)
        ride_c = _chip_rider(chip_sum, False) if dist and l == 0 else None
        dz, dbre, dbim, dcre, dcim, dd, dwg, dbg, da, *landed = _s5_bwd(dy, z, yraw, sre, sim, p, ride_c)
        if ride_c:
            mine[1] = [_sum_chips(r) for r in landed]
        ride_e = _pair_rider(mine[1], False) if dist and l == 0 else None
        dz, dws, dbsf, dlng, dlnb, *got = _sgu_bwd(dy, z, dz, p["ws"], p["ws_t"], p["bsf"], p["lng"], p["lnb"], ride_e)
        if ride_e:
            theirs[1] = got
        dz, dwp, dsc = _pool_bwd(dy, z, dz, p["wp"], p["wp_t"], p["scale"])
        gr["w_in"][l] = _inproj_bwd_dw(h, dz)
        if not dist:
            dx, dng = _inproj_bwd_dx(dz, g_in[l], xs[l], p["norm_g"], dx)
        else:
            part = [halves(gr["w_in"][l], D_MODEL), halves(dwg, S5_W // N_CHIPS),
                    halves(gr["w_out"][l], D_MODEL // N_CHIPS)]
            ride_a = _pair_rider(part, True)
            if l == 1:
                dx, dng, *from_sib = _inproj_bwd_dx(dz, g_in[l], xs[l], p["norm_g"], dx, ride_a)
                chip_sum = [_add_own_half(a, r, cidx) for a, r in zip(part, from_sib)]
            else:
                nt = x0.shape[0] // _dx_tile(x0.shape[0])
                n_top = max(nt // 4, 1)
                n_last = 1 if nt < 8 else 2
                n_mid = nt - n_top - n_last
                dx_top, dng_top, *from_sib = _inproj_bwd_dx(dz, g_in[l], xs[l], p["norm_g"], dx, ride_a,
                                                            tiles=(0, n_top))
                chip_sum0 = [_add_own_half(a, r, cidx) for a, r in zip(part, from_sib)]
                if n_mid:
                    dx_mid, dng_mid, *landed = _inproj_bwd_dx(dz, g_in[l], xs[l], p["norm_g"], dx,
                                                              _chip_rider(chip_sum0, False), tiles=(n_top, n_mid),
                                                              prev=dx_top)
                else:
                    dx_mid, dng_mid = dx_top, 0.0
                    landed = _run_rider("grad_chip_exchange", _chip_rider(chip_sum0, False))
                dx, dng_last = _inproj_bwd_dx(dz, g_in[l], xs[l], p["norm_g"], dx, tiles=(n_top + n_mid, n_last),
                                              prev=dx_mid)
                dng = dng_top + dng_mid + dng_last
                mine[0] = [_sum_chips(r) for r in landed]
                theirs[0] = _run_rider("grad_result_exchange", _pair_rider(mine[0], False))

        raw = (wt["lam_re"][l], wt["lam_im"][l], wt["b_re"][l], wt["b_im"][l], wt["log_dt"][l])
        _, vjp = jax.vjp(_s5_prep, *raw)
        da = jnp.sum(da, axis=1)
        cot = (da[0].reshape(S5_GROUPS, S5_STATE), da[1].reshape(S5_GROUPS, S5_STATE),
               _block_diag_in_grad(dbre), _block_diag_in_grad(dbim))
        gr["lam_re"][l], gr["lam_im"][l], gr["b_re"][l], gr["b_im"][l], gr["log_dt"][l] = vjp(cot)
        gr["c_re"][l] = _block_diag_out_grad(dcre)
        gr["c_im"][l] = -_block_diag_out_grad(dcim)
        gr["d_skip"][l] = dd.reshape(S5_GROUPS, S5_CH)
        gr["w_glu"][l] = dwg
        gr["b_glu"][l] = dbg.reshape(S5_W)
        causal = jnp.tril(jnp.ones((CHUNK, CHUNK), dtype=bool))
        gr["w_s"][l] = jnp.where(causal[None], dws, 0.0)
        gr["b_s"][l] = dbsf.reshape(CHUNK, SGU_HEADS, CHUNK).sum(-1).T
        gr["ln_g"][l] = dlng.reshape(SGU_W)
        gr["ln_b"][l] = dlnb.reshape(SGU_W)
        gr["w_pool"][l] = dwp
        gr["pool_scale"][l] = dsc.reshape(POOL_W)
        gr["norm_g"][l] = dng.reshape(D_MODEL)

    grads = {k: (v if k in BIG else jnp.stack(v)) for k, v in gr.items()}
    grads["final_g"] = dfg.reshape(D_MODEL)
    if dist:
        for i, k in enumerate(BIG):
            grads[k] = ([mine[l][i] for l in range(DEPTH)], [theirs[l][i] for l in range(DEPTH)])
    return loss, dx, grads


def kernel(x, norm_g, w_in, lam_re, lam_im, b_re, b_im, c_re, c_im, d_skip, log_dt, w_glu, b_glu, ln_g, ln_b, w_s, b_s, w_pool, pool_scale, w_out, final_g, loss_target, m_norm_g, m_w_in, m_lam_re, m_lam_im, m_b_re, m_b_im, m_c_re, m_c_im, m_d_skip, m_log_dt, m_w_glu, m_b_glu, m_ln_g, m_ln_b, m_w_s, m_b_s, m_w_pool, m_pool_scale, m_w_out, m_final_g, v_norm_g, v_w_in, v_lam_re, v_lam_im, v_b_re, v_b_im, v_c_re, v_c_im, v_d_skip, v_log_dt, v_w_glu, v_b_glu, v_ln_g, v_ln_b, v_w_s, v_b_s, v_w_pool, v_pool_scale, v_w_out, v_final_g):
    wt = dict(norm_g=norm_g, w_in=w_in, lam_re=lam_re, lam_im=lam_im, b_re=b_re, b_im=b_im, c_re=c_re, c_im=c_im,
              d_skip=d_skip, log_dt=log_dt, w_glu=w_glu, b_glu=b_glu, ln_g=ln_g, ln_b=ln_b, w_s=w_s, b_s=b_s,
              w_pool=w_pool, pool_scale=pool_scale, w_out=w_out, final_g=final_g)
    mom = dict(norm_g=m_norm_g, w_in=m_w_in, lam_re=m_lam_re, lam_im=m_lam_im, b_re=m_b_re, b_im=m_b_im, c_re=m_c_re,
               c_im=m_c_im, d_skip=m_d_skip, log_dt=m_log_dt, w_glu=m_w_glu, b_glu=m_b_glu, ln_g=m_ln_g, ln_b=m_ln_b,
               w_s=m_w_s, b_s=m_b_s, w_pool=m_w_pool, pool_scale=m_pool_scale, w_out=m_w_out, final_g=m_final_g)
    vel = dict(norm_g=v_norm_g, w_in=v_w_in, lam_re=v_lam_re, lam_im=v_lam_im, b_re=v_b_re, b_im=v_b_im, c_re=v_c_re,
               c_im=v_c_im, d_skip=v_d_skip, log_dt=v_log_dt, w_glu=v_w_glu, b_glu=v_b_glu, ln_g=v_ln_g, ln_b=v_ln_b,
               w_s=v_w_s, b_s=v_b_s, w_pool=v_w_pool, pool_scale=v_pool_scale, w_out=v_w_out, final_g=v_final_g)
    T = x.shape[1]
    cidx = lax.axis_index("c").astype(jnp.int32).reshape(1)

    w_in_b = w_in.astype(BF16)
    w0 = w_in_b[0].reshape(2, HALF_D, SHARD_COLS)
    rest = (w_in_b[1].reshape(2, HALF_D, SHARD_COLS), w_glu.astype(BF16), w_out.astype(BF16))
    mx, my = lax.axis_index("x"), lax.axis_index("y")
    order = jnp.stack([2 * mx + my] + [j for j, _, _ in _other_chips(mx, my)]).astype(jnp.int32)
    loss, grad_x, grads = _local_step(x.reshape(T, D_MODEL), loss_target.reshape(T, D_MODEL), wt, w0, rest, False,
                                      cidx, order)

    packed = _pack([grads[k] for k in SMALL] + [loss[0, 0:1]])
    (sib_packed,) = _run_rider("small_pair_exchange", _pair_rider([packed], False))
    chip_packed = _add2(packed, sib_packed)
    half_rows = chip_packed.shape[0] // 2
    my_half = lax.dynamic_index_in_dim(chip_packed.reshape(2, half_rows, 128), cidx[0], 0, keepdims=False)
    small_ride = _chip_rider([my_half], True)

    out_g, out_d, out_m, out_v = {}, {}, {}, {}
    all_half = None
    for k in BIG:
        shape = wt[k].shape
        quad = lambda t: t.reshape(2, 2, shape[1] // 2, shape[2])
        g, d, m, v, *landed = _adamw_halves(quad(wt[k]), grads[k][0], grads[k][1], quad(mom[k]), quad(vel[k]), cidx,
                                            small_ride if k == BIG[0] else None)
        if landed:
            (all_half,) = landed
        out_g[k], out_d[k], out_m[k], out_v[k] = (t.reshape(shape) for t in (g, d, m, v))

    mine_half = _sum_chips(all_half)
    (their_half,) = _run_rider("small_result_exchange", _pair_rider([mine_half], False))
    total = jnp.where(cidx[0] == 0, jnp.concatenate([mine_half, their_half]), jnp.concatenate([their_half, mine_half]))
    like = [wt[k] for k in SMALL]
    small_g = _unpack(total, like + [loss[0, 0:1]])
    loss_out = small_g[-1].reshape(())
    w_p, m_p, v_p = _pack(like), _pack([mom[k] for k in SMALL]), _pack([vel[k] for k in SMALL])
    d_p, mo_p, vo_p = _adamw(w_p, total, m_p, v_p)
    for k, g, d, m, v in zip(SMALL, small_g[:-1], _unpack(d_p, like), _unpack(mo_p, like), _unpack(vo_p, like)):
        out_g[k], out_d[k], out_m[k], out_v[k] = g, d, m, v

    return (loss_out, grad_x.reshape(x.shape), *[out_g[k] for k in WEIGHTS], *[out_d[k] for k in WEIGHTS],
            *[out_m[k] for k in WEIGHTS], *[out_v[k] for k in WEIGHTS])
```

```python
import functools
import math

import jax
import jax.numpy as jnp
from jax import lax
from jax.experimental import pallas as pl
from jax.experimental.pallas import tpu as pltpu

F32 = jnp.float32
BF16 = jnp.bfloat16

D_MODEL = 2048
DEPTH = 2
S5_W = 512
SGU_W = 1024
POOL_W = 512
IN_COLS = 5120
N_CHIPS = 4
SHARD_COLS = IN_COLS // N_CHIPS
S5_GROUPS = 32
S5_STATE = 64
S5_CH = 16
STATE_W = S5_GROUPS * S5_STATE
SUPER = 4
CHUNK = 128
SGU_HEADS = 8
POOL_WINDOWS = (2, 4, 8, 16)
POOL_HALO = 16
RMS_EPS = 1e-6
LN_EPS = 1e-5
SCAN_COLS = 512

ADAM_LR = 0.001
ADAM_B1 = 0.9
ADAM_B2 = 0.999
ADAM_EPS = 1e-08
ADAM_WD = 0.01
ADAM_STEP = 10

VMEM_LIMIT = 56 * 1024 * 1024
MESH_ID = pl.DeviceIdType.MESH

_GELU_K0 = math.sqrt(2.0 / math.pi)
_GELU_K1 = 0.044715


def _cparams(n_axes):
    return pltpu.CompilerParams(dimension_semantics=("arbitrary",) * n_axes, vmem_limit_bytes=VMEM_LIMIT)


def _gelu(x):
    t = jnp.tanh(_GELU_K0 * (x + _GELU_K1 * (x * x * x)))
    return 0.5 * x * (1.0 + t)


def _gelu_and_grad(x):
    x2 = x * x
    t = jnp.tanh(_GELU_K0 * (x + _GELU_K1 * (x * x2)))
    g = 0.5 * x * (1.0 + t)
    dg = 0.5 * (1.0 + t) + 0.5 * x * (1.0 - t * t) * (_GELU_K0 * (1.0 + 3.0 * _GELU_K1 * x2))
    return g, dg


def _silu_and_grad(x):
    s = jax.nn.sigmoid(x)
    return x * s, s * (1.0 + x * (1.0 - s))


def _dot(a, b):
    return jnp.dot(a.astype(BF16), b.astype(BF16), preferred_element_type=F32)


def _dot_nt(a, b):
    return lax.dot_general(a.astype(BF16), b.astype(BF16), (((1,), (1,)), ((), ())), preferred_element_type=F32)


def _dot_tn(a, b):
    return lax.dot_general(a.astype(BF16), b.astype(BF16), (((0,), (0,)), ((), ())), preferred_element_type=F32)


def _full(shape):
    nd = len(shape)
    return pl.BlockSpec(shape, lambda *_: (0,) * nd)


class _Rider:
    def __init__(self, arrs, out_shapes, sems, steps):
        self.arrs, self.out_shapes, self.sems, self.steps = list(arrs), list(out_shapes), list(sems), steps


def _ride(body, n_in, n_out, rider, first, last, middle=None):
    if rider is None:
        return body
    ri, ro, ns = len(rider.arrs), len(rider.out_shapes), len(rider.sems)

    def wrapped(*refs):
        o0 = n_in + ri
        steps = rider.steps(refs[n_in:o0], refs[o0 + n_out:o0 + n_out + ro], *refs[len(refs) - ns:])
        pl.when(first())(steps[0])
        if len(steps) == 3:
            pl.when(middle())(steps[1])
        body(*refs[:n_in], *refs[o0:o0 + n_out], *refs[o0 + n_out + ro:len(refs) - ns])
        pl.when(last())(steps[-1])

    return wrapped


class _ColumnWriter:
    def __init__(self, stage_ref, sem_ref, dst_ref, col0, step, n_steps):
        self.stage, self.sem, self.dst, self.col0, self.step, self.n = stage_ref, sem_ref, dst_ref, col0, step, n_steps
        self.tm, self.w = stage_ref.shape[1], stage_ref.shape[2]

    def _copy(self, slot, row0):
        return pltpu.make_async_copy(self.stage.at[slot],
                                     self.dst.at[pl.ds(row0, self.tm), pl.ds(self.col0, self.w)], self.sem.at[slot])

    def slot(self):
        s = self.step % 2

        @pl.when(self.step >= 2)
        def _():
            self._copy(s, 0).wait()

        return self.stage.at[s]

    def send(self, row0):
        s = self.step % 2
        self._copy(s, row0).start()

        @pl.when(self.step == self.n - 1)
        def _():
            self._copy(s, 0).wait()
            if self.n >= 2:
                self._copy(1 - s, 0).wait()


def _stage_scratch(tm, widths):
    return ([pltpu.VMEM((2, tm, w), BF16) for w in widths], [pltpu.SemaphoreType.DMA((2,)) for _ in widths])


def _rider_specs(rider):
    if rider is None:
        return [], [], [], [], []
    anyspec = pl.BlockSpec(memory_space=pl.ANY)
    return ([anyspec] * len(rider.arrs), [anyspec] * len(rider.out_shapes), rider.out_shapes, rider.sems, rider.arrs)


HALF_D = D_MODEL // 2


def _inproj(x, g, w):
    T = x.shape[0]
    tm = min(512, T)

    def body(x_ref, g_ref, w_ref, z_ref, h_ref, hs_ref):
        @pl.when(pl.program_id(1) == 0)
        def _():
            xv = x_ref[...]
            r = lax.rsqrt(jnp.mean(xv * xv, axis=-1, keepdims=True) + RMS_EPS)
            hv = (xv * r * g_ref[...]).astype(BF16)
            hs_ref[...] = hv
            h_ref[...] = hv

        z_ref[...] = (jnp.dot(hs_ref[:, 0:HALF_D], w_ref[0], preferred_element_type=F32)
                      + jnp.dot(hs_ref[:, HALF_D:D_MODEL], w_ref[1], preferred_element_type=F32))

    return pl.pallas_call(
        body,
        name="inproj",
        grid=(T // tm, N_CHIPS),
        in_specs=[
            pl.BlockSpec((tm, D_MODEL), lambda i, j: (i, 0)),
            pl.BlockSpec((1, D_MODEL), lambda i, j: (0, 0)),
            pl.BlockSpec((2, None, HALF_D, SHARD_COLS), lambda i, j: (0, j, 0, 0)),
        ],
        out_specs=[
            pl.BlockSpec((tm, SHARD_COLS), lambda i, j: (i, j)),
            pl.BlockSpec((tm, D_MODEL), lambda i, j: (i, 0)),
        ],
        out_shape=[jax.ShapeDtypeStruct((T, IN_COLS), F32), jax.ShapeDtypeStruct((T, D_MODEL), BF16)],
        scratch_shapes=[pltpu.VMEM((tm, D_MODEL), BF16)],
        compiler_params=_cparams(2),
    )(x, g, w)


def _rms_h(x, g):
    T = x.shape[0]
    tm = min(512, T)

    def body(x_ref, g_ref, h_ref):
        xv = x_ref[...]
        r = lax.rsqrt(jnp.mean(xv * xv, axis=-1, keepdims=True) + RMS_EPS)
        h_ref[...] = (xv * r * g_ref[...]).astype(BF16)

    return pl.pallas_call(
        body, name="rms_h", grid=(T // tm,),
        in_specs=[pl.BlockSpec((tm, D_MODEL), lambda i: (i, 0)), pl.BlockSpec((1, D_MODEL), lambda i: (0, 0))],
        out_specs=pl.BlockSpec((tm, D_MODEL), lambda i: (i, 0)),
        out_shape=jax.ShapeDtypeStruct((T, D_MODEL), BF16), compiler_params=_cparams(1),
    )(x, g)


def _inproj_first(h, w0, order, riders):
    T = h.shape[0]
    tm = min(512, T)
    ni = T // tm
    n = len(riders)

    def body(order_ref, h_ref, w0_ref, *refs):
        rin = refs[:n]
        z_ref, gin_ref = refs[n:n + 2]
        rout = refs[n + 2:2 * n + 2]
        wbuf, csem, lsem, ssem, rsem = refs[2 * n + 2:2 * n + 7]
        s, i = pl.program_id(0), pl.program_id(1)
        x, y, c = _mesh_pos()
        me = 2 * x + y
        sib = (x, y, 1 - c)
        chips = _other_chips(x, y)
        if n:
            r_start, r_mid, r_end = _gather_steps(rin, rout, *refs[2 * n + 7:])

        def ici(d):
            return pltpu.make_async_remote_copy(w0_ref.at[c], gin_ref.at[c, me], ssem.at[d], rsem.at[d],
                                                device_id=(chips[d][1], chips[d][2], c), device_id_type=MESH_ID)

        def landed(d):
            return pltpu.make_async_remote_copy(w0_ref.at[c], gin_ref.at[c, chips[d][0]], ssem.at[d], rsem.at[d],
                                                device_id=sib, device_id_type=MESH_ID)

        def fwd(d, half):
            blk = gin_ref.at[half, chips[d][0]]
            return pltpu.make_async_remote_copy(blk, blk, ssem.at[3 + d], rsem.at[3 + d], device_id=sib,
                                                device_id_type=MESH_ID)

        def local(hf):
            return pltpu.make_async_copy(w0_ref.at[hf], gin_ref.at[hf, me], lsem.at[hf])

        def load(src):
            cp = pltpu.make_async_copy(src, wbuf, csem.at[0])
            cp.start()
            cp.wait()

        @pl.when((s == 0) & (i == 0))
        def _():
            for d in range(3):
                ici(d).start()
            local(0).start()
            local(1).start()
            load(w0_ref)

        for d in range(3):
            @pl.when((s == d + 1) & (i == 0))
            def _(d=d):
                landed(d).wait_recv()
                fwd(d, c).start()
                fwd(d, 1 - c).wait_recv()
                load(gin_ref.at[:, chips[d][0]])
                if d == 2 and n:
                    r_start()

        z_ref[...] = (jnp.dot(h_ref[:, 0:HALF_D], wbuf[0], preferred_element_type=F32)
                      + jnp.dot(h_ref[:, HALF_D:D_MODEL], wbuf[1], preferred_element_type=F32))

        @pl.when((s == N_CHIPS - 1) & (i == ni - 1))
        def _():
            for d in range(3):
                ici(d).wait_send()
                fwd(d, c).wait_send()
            local(0).wait()
            local(1).wait()
            if n:
                r_mid()
                r_end()

    anyspec = pl.BlockSpec(memory_space=pl.ANY)
    return pl.pallas_call(
        body,
        name="inproj_first",
        grid_spec=pltpu.PrefetchScalarGridSpec(
            num_scalar_prefetch=1,
            grid=(N_CHIPS, ni),
            in_specs=[pl.BlockSpec((tm, D_MODEL), lambda s, i, o: (i, 0)), anyspec] + [anyspec] * n,
            out_specs=[pl.BlockSpec((tm, SHARD_COLS), lambda s, i, o: (i, o[s])), anyspec] + [anyspec] * n,
            scratch_shapes=[pltpu.VMEM((2, HALF_D, SHARD_COLS), BF16), pltpu.SemaphoreType.DMA((1,)),
                            pltpu.SemaphoreType.DMA((2,)), pltpu.SemaphoreType.DMA((6,)),
                            pltpu.SemaphoreType.DMA((6,))] + (_gather_sems(n) if n else []),
        ),
        out_shape=[jax.ShapeDtypeStruct((T, IN_COLS), F32),
                   jax.ShapeDtypeStruct((2, N_CHIPS, HALF_D, SHARD_COLS), BF16)] + _gathered_shapes(riders),
        compiler_params=_cparams(2),
    )(order, h, w0, *riders)


def _outproj(ya, yb, yc, w, x):
    T = x.shape[0]
    tm = min(512, T)
    tn = 1024

    def body(ya_ref, yb_ref, yc_ref, w_ref, x_ref, o_ref, y_ref):
        acc = jnp.dot(ya_ref[...], w_ref[0:S5_W, :], preferred_element_type=F32)
        acc += jnp.dot(yb_ref[...], w_ref[S5_W:S5_W + SGU_W, :], preferred_element_type=F32)
        acc += jnp.dot(yc_ref[...], w_ref[S5_W + SGU_W:D_MODEL, :], preferred_element_type=F32)
        o_ref[...] = x_ref[...] + acc

        @pl.when(pl.program_id(1) == 0)
        def _():
            y_ref[:, 0:S5_W] = ya_ref[...]
            y_ref[:, S5_W:S5_W + SGU_W] = yb_ref[...]
            y_ref[:, S5_W + SGU_W:D_MODEL] = yc_ref[...]

    return pl.pallas_call(
        body,
        name="outproj",
        grid=(T // tm, D_MODEL // tn),
        in_specs=[
            pl.BlockSpec((tm, S5_W), lambda i, j: (i, 0)),
            pl.BlockSpec((tm, SGU_W), lambda i, j: (i, 0)),
            pl.BlockSpec((tm, POOL_W), lambda i, j: (i, 0)),
            pl.BlockSpec((D_MODEL, tn), lambda i, j: (0, j)),
            pl.BlockSpec((tm, tn), lambda i, j: (i, j)),
        ],
        out_specs=[
            pl.BlockSpec((tm, tn), lambda i, j: (i, j)),
            pl.BlockSpec((tm, D_MODEL), lambda i, j: (i, 0)),
        ],
        out_shape=[jax.ShapeDtypeStruct((T, D_MODEL), F32), jax.ShapeDtypeStruct((T, D_MODEL), BF16)],
        compiler_params=_cparams(2),
    )(ya, yb, yc, w, x)


def _outproj_bwd_dy(dxo, w):
    T = dxo.shape[0]
    tm = min(512, T)
    tn = 1024

    def body(d_ref, w_ref, o_ref, ds_ref):
        @pl.when(pl.program_id(1) == 0)
        def _():
            ds_ref[...] = d_ref[...].astype(BF16)

        o_ref[...] = lax.dot_general(ds_ref[...], w_ref[...], (((1,), (1,)), ((), ())), preferred_element_type=F32)

    return pl.pallas_call(
        body,
        name="outproj_bwd_dy",
        grid=(T // tm, D_MODEL // tn),
        in_specs=[
            pl.BlockSpec((tm, D_MODEL), lambda i, j: (i, 0)),
            pl.BlockSpec((tn, D_MODEL), lambda i, j: (j, 0)),
        ],
        out_specs=pl.BlockSpec((tm, tn), lambda i, j: (i, j)),
        out_shape=jax.ShapeDtypeStruct((T, D_MODEL), F32),
        scratch_shapes=[pltpu.VMEM((tm, D_MODEL), BF16)],
        compiler_params=_cparams(2),
    )(dxo, w)


def _outproj_bwd_dw(y, dxo):
    T = y.shape[0]
    tm = min(512, T)
    tr = 1024

    def body(y_ref, d_ref, o_ref):
        @pl.when(pl.program_id(1) == 0)
        def _():
            o_ref[...] = jnp.zeros_like(o_ref)

        o_ref[...] += _dot_tn(y_ref[...], d_ref[...])

    return pl.pallas_call(
        body,
        name="outproj_bwd_dw",
        grid=(D_MODEL // tr, T // tm),
        in_specs=[
            pl.BlockSpec((tm, tr), lambda p, t: (t, p)),
            pl.BlockSpec((tm, D_MODEL), lambda p, t: (t, 0)),
        ],
        out_specs=pl.BlockSpec((tr, D_MODEL), lambda p, t: (p, 0)),
        out_shape=jax.ShapeDtypeStruct((D_MODEL, D_MODEL), F32),
        compiler_params=_cparams(2),
    )(y, dxo)


def _inproj_bwd_dw(h, dz):
    T = h.shape[0]
    tm = min(512, T)

    def body(h_ref, dz_ref, o_ref):
        @pl.when(pl.program_id(1) == 0)
        def _():
            o_ref[...] = jnp.zeros_like(o_ref)

        o_ref[...] += _dot_tn(h_ref[...], dz_ref[...])

    return pl.pallas_call(
        body,
        name="inproj_bwd_dw",
        grid=(N_CHIPS, T // tm),
        in_specs=[
            pl.BlockSpec((tm, D_MODEL), lambda j, t: (t, 0)),
            pl.BlockSpec((tm, SHARD_COLS), lambda j, t: (t, j)),
        ],
        out_specs=pl.BlockSpec((None, D_MODEL, SHARD_COLS), lambda j, t: (j, 0, 0)),
        out_shape=jax.ShapeDtypeStruct((N_CHIPS, D_MODEL, SHARD_COLS), F32),
        compiler_params=_cparams(2),
    )(h, dz)


def _dx_tile(T):
    return min(512, max(T // 4, 8))


def _inproj_bwd_dx(dz, w4, x, g, dxo, rider=None, tiles=None, prev=None):
    T = x.shape[0]
    tm = _dx_tile(T)
    t0, ni = tiles if tiles else (0, T // tm)
    nk = N_CHIPS
    nt = (((1,), (1,)), ((), ()))
    n_in = 5 if prev is None else 6

    def body(dz_ref, w_ref, x_ref, g_ref, dxo_ref, *rest):
        dx_ref, dg_ref, acc_ref = rest[-3:]
        i, j = pl.program_id(0), pl.program_id(1)
        lo = lax.dot_general(dz_ref[...], w_ref[0], nt, preferred_element_type=F32)
        hi = lax.dot_general(dz_ref[...], w_ref[1], nt, preferred_element_type=F32)

        @pl.when(j == 0)
        def _():
            acc_ref[:, 0:HALF_D] = lo
            acc_ref[:, HALF_D:D_MODEL] = hi

        @pl.when(j > 0)
        def _():
            acc_ref[:, 0:HALF_D] += lo
            acc_ref[:, HALF_D:D_MODEL] += hi

        @pl.when(j == nk - 1)
        def _():
            @pl.when(i == 0)
            def _():
                dg_ref[...] = jnp.zeros_like(dg_ref)

            rc = min(128, tm)
            for c in range(tm // rc):
                rows = slice(c * rc, (c + 1) * rc)
                dh = acc_ref[rows, :]
                xv = x_ref[rows, :]
                r = lax.rsqrt(jnp.mean(xv * xv, axis=-1, keepdims=True) + RMS_EPS)
                xh = xv * r
                w = dh * g_ref[...]
                dx_ref[rows, :] = dxo_ref[rows, :] + r * (w - xh * jnp.mean(w * xh, axis=-1, keepdims=True))
                dg_ref[...] += jnp.sum(dh * xh, axis=0, keepdims=True)

    r_in, r_out, r_shapes, r_sems, r_args = _rider_specs(rider)
    return pl.pallas_call(
        _ride(body, n_in, 2, rider, lambda: (pl.program_id(0) == 0) & (pl.program_id(1) == 0),
              lambda: (pl.program_id(0) == ni - 1) & (pl.program_id(1) == nk - 1)),
        name="inproj_bwd_dx" + ("" if rider is None else "_ride") + ("" if prev is None else "_rest"),
        grid=(ni, nk),
        in_specs=[
            pl.BlockSpec((tm, SHARD_COLS), lambda i, j: (i + t0, j)),
            pl.BlockSpec((2, None, HALF_D, SHARD_COLS), lambda i, j: (0, j, 0, 0)),
            pl.BlockSpec((tm, D_MODEL), lambda i, j: (i + t0, 0)),
            pl.BlockSpec((1, D_MODEL), lambda i, j: (0, 0)),
            pl.BlockSpec((tm, D_MODEL), lambda i, j: (i + t0, 0)),
        ] + ([] if prev is None else [pl.BlockSpec(memory_space=pl.ANY)]) + r_in,
        out_specs=[
            pl.BlockSpec((tm, D_MODEL), lambda i, j: (i + t0, 0)),
            pl.BlockSpec((1, D_MODEL), lambda i, j: (0, 0)),
        ] + r_out,
        out_shape=[jax.ShapeDtypeStruct((T, D_MODEL), F32), jax.ShapeDtypeStruct((1, D_MODEL), F32)] + r_shapes,
        scratch_shapes=[pltpu.VMEM((tm, D_MODEL), F32)] + r_sems,
        input_output_aliases={} if prev is None else {5: 0},
        compiler_params=_cparams(2),
    )(dz, w4, x, g, dxo, *([] if prev is None else [prev]), *r_args)


def _outproj_loss(ya, yb, yc, w, x, g, tgt):
    T = x.shape[0]
    tm = min(256, T)

    def body(ya_ref, yb_ref, yc_ref, w_ref, x_ref, g_ref, t_ref, dx_ref, l_ref, dg_ref, y_ref):
        i = pl.program_id(0)
        acc = jnp.dot(ya_ref[...], w_ref[0:S5_W, :], preferred_element_type=F32)
        acc += jnp.dot(yb_ref[...], w_ref[S5_W:S5_W + SGU_W, :], preferred_element_type=F32)
        acc += jnp.dot(yc_ref[...], w_ref[S5_W + SGU_W:D_MODEL, :], preferred_element_type=F32)
        y_ref[:, 0:S5_W] = ya_ref[...]
        y_ref[:, S5_W:S5_W + SGU_W] = yb_ref[...]
        y_ref[:, S5_W + SGU_W:D_MODEL] = yc_ref[...]
        xv = x_ref[...] + acc
        r = lax.rsqrt(jnp.mean(xv * xv, axis=-1, keepdims=True) + RMS_EPS)
        xh = xv * r
        err = xh * g_ref[...] - t_ref[...]
        lpart = 0.5 * jnp.sum(jnp.mean(err * err, axis=-1, keepdims=True), axis=0, keepdims=True)
        dout = err * (1.0 / D_MODEL)
        w = dout * g_ref[...]
        dx_ref[...] = r * (w - xh * jnp.mean(w * xh, axis=-1, keepdims=True))
        gpart = jnp.sum(dout * xh, axis=0, keepdims=True)

        @pl.when(i == 0)
        def _():
            l_ref[...] = jnp.broadcast_to(lpart, l_ref.shape)
            dg_ref[...] = gpart

        @pl.when(i > 0)
        def _():
            l_ref[...] += jnp.broadcast_to(lpart, l_ref.shape)
            dg_ref[...] += gpart

    row = lambda w: pl.BlockSpec((tm, w), lambda i: (i, 0))
    return pl.pallas_call(
        body,
        name="outproj_loss",
        grid=(T // tm,),
        in_specs=[row(S5_W), row(SGU_W), row(POOL_W), _full((D_MODEL, D_MODEL)), row(D_MODEL), _full((1, D_MODEL)),
                  row(D_MODEL)],
        out_specs=[row(D_MODEL), _full((1, 128)), _full((1, D_MODEL)), row(D_MODEL)],
        out_shape=[
            jax.ShapeDtypeStruct((T, D_MODEL), F32),
            jax.ShapeDtypeStruct((1, 128), F32),
            jax.ShapeDtypeStruct((1, D_MODEL), F32),
            jax.ShapeDtypeStruct((T, D_MODEL), BF16),
        ],
        compiler_params=_cparams(1),
    )(ya, yb, yc, w, x, g, tgt)


def _s5_prep(lam_re, lam_im, b_re, b_im, log_dt):
    lam = lax.complex(lam_re, lam_im)
    dt = jnp.exp(log_dt)[:, None]
    a = jnp.exp(lam * dt)
    bbar = ((a - 1.0) / lam)[..., None] * lax.complex(b_re, b_im)
    return jnp.real(a), jnp.imag(a), jnp.real(bbar), jnp.imag(bbar)


def _block_diag_in(m):
    m4 = m.reshape(SUPER, 8, S5_STATE, S5_CH)
    eye = jnp.eye(8, dtype=m.dtype)
    out = jnp.einsum("jgph,gk->jghkp", m4, eye)
    return out.reshape(SUPER, 8 * S5_CH, 8 * S5_STATE)


def _block_diag_in_grad(d):
    d6 = d.reshape(SUPER, 8, S5_CH, 8, S5_STATE)
    diag = jnp.einsum("jghgp->jgph", d6)
    return diag.reshape(S5_GROUPS, S5_STATE, S5_CH)


def _block_diag_out(m):
    m4 = m.reshape(SUPER, 8, S5_CH, S5_STATE)
    eye = jnp.eye(8, dtype=m.dtype)
    out = jnp.einsum("jghp,gk->jgpkh", m4, eye)
    return out.reshape(SUPER, 8 * S5_STATE, 8 * S5_CH)


def _block_diag_out_grad(d):
    d6 = d.reshape(SUPER, 8, S5_STATE, 8, S5_CH)
    diag = jnp.einsum("jgpgh->jghp", d6)
    return diag.reshape(S5_GROUPS, S5_CH, S5_STATE)


def _scan_coefs(a_re, a_im, reverse):
    a = lax.complex(a_re.reshape(-1), a_im.reshape(-1))
    if reverse:
        a = jnp.conj(a)
    pw = [a]
    for _ in range(7):
        pw.append(pw[-1] * a)
    rows = jnp.arange(8)

    def masked(k):
        m = (rows + k <= 7) if reverse else (rows >= k)
        return jnp.where(m[:, None], pw[k - 1][None, :], 0.0)

    a1, a2, a4 = masked(1), masked(2), masked(4)
    carry = jnp.stack([pw[7 - r] for r in range(8)]) if reverse else jnp.stack(pw)
    parts = []
    for c in (a1, a2, a4, carry):
        parts += [jnp.real(c), jnp.imag(c)]
    return jnp.stack(parts).astype(F32)


def _scan_block(r, im, coef_ref, cs, reverse):
    for k, idx in ((1, 0), (2, 2), (4, 4)):
        ar = coef_ref[idx, :, cs]
        ai = coef_ref[idx + 1, :, cs]
        sh = 8 - k if reverse else k
        rr = pltpu.roll(r, sh, 0)
        ri = pltpu.roll(im, sh, 0)
        r, im = r + ar * rr - ai * ri, im + ar * ri + ai * rr
    return r, im


def _s5_fwd(z, p, rider=None):
    T = z.shape[0]
    tm = min(256, T)
    nblk = tm // 8
    W = STATE_W

    def body(xa_ref, ga_ref, bre_ref, bim_ref, cre_ref, cim_ref, dv_ref, wg_ref, bg_ref, coef_ref,
             ya_ref, yraw_ref, sre_ref, sim_ref, wre, wim):
        @pl.when(pl.program_id(0) == 0)
        def _():
            wre[0:8, :] = jnp.zeros((8, W), F32)
            wim[0:8, :] = jnp.zeros((8, W), F32)

        xa = xa_ref[...]
        xab = xa.astype(BF16)
        for j in range(SUPER):
            xj = xab[:, j * 128:(j + 1) * 128]
            wre[8:8 + tm, j * 512:(j + 1) * 512] = jnp.dot(xj, bre_ref[j], preferred_element_type=F32)
            wim[8:8 + tm, j * 512:(j + 1) * 512] = jnp.dot(xj, bim_ref[j], preferred_element_type=F32)

        def blk(b, carry):
            base = pl.multiple_of(8 + b * 8, 8)
            for cc in range(W // SCAN_COLS):
                cs = pl.ds(cc * SCAN_COLS, SCAN_COLS)
                r, im = _scan_block(wre[pl.ds(base, 8), cs], wim[pl.ds(base, 8), cs], coef_ref, cs, False)
                cr = wre[pl.ds(base - 1, 1), cs]
                ci = wim[pl.ds(base - 1, 1), cs]
                pr = coef_ref[6, :, cs]
                pi = coef_ref[7, :, cs]
                wre[pl.ds(base, 8), cs] = r + pr * cr - pi * ci
                wim[pl.ds(base, 8), cs] = im + pr * ci + pi * cr
            return carry

        lax.fori_loop(0, nblk, blk, 0)
        wre[0:8, :] = wre[tm:tm + 8, :]
        wim[0:8, :] = wim[tm:tm + 8, :]
        sre_ref[...] = wre[8:8 + tm, :]
        sim_ref[...] = wim[8:8 + tm, :]

        for j in range(SUPER):
            yr = jnp.dot(wre[8:8 + tm, j * 512:(j + 1) * 512].astype(BF16), cre_ref[j], preferred_element_type=F32)
            yr += jnp.dot(wim[8:8 + tm, j * 512:(j + 1) * 512].astype(BF16), cim_ref[j], preferred_element_type=F32)
            yraw_ref[:, j * 128:(j + 1) * 128] = yr
        yraw = yraw_ref[...] + dv_ref[...] * xa
        yraw_ref[...] = yraw
        yg = _gelu(yraw)
        q = jnp.dot(yg.astype(BF16), wg_ref[...], preferred_element_type=F32) + bg_ref[...]
        sga, _ = _silu_and_grad(ga_ref[...])
        ya_ref[...] = (yg * jax.nn.sigmoid(q) * sga).astype(BF16)

    nt = T // tm
    r_in, r_out, r_shapes, r_sems, r_args = _rider_specs(rider)
    return pl.pallas_call(
        _ride(body, 10, 4, rider, lambda: pl.program_id(0) == 0, lambda: pl.program_id(0) == nt - 1,
              lambda: pl.program_id(0) == nt // 2),
        name="s5_fwd" + ("" if rider is None else "_ride"),
        grid=(nt,),
        in_specs=[
            pl.BlockSpec((tm, S5_W), lambda i: (i, 0)),
            pl.BlockSpec((tm, S5_W), lambda i: (i, 6)),
            _full((SUPER, 128, 512)), _full((SUPER, 128, 512)),
            _full((SUPER, 512, 128)), _full((SUPER, 512, 128)),
            _full((1, S5_W)), _full((S5_W, S5_W)), _full((1, S5_W)),
            _full((8, 8, W)),
        ] + r_in,
        out_specs=[
            pl.BlockSpec((tm, S5_W), lambda i: (i, 0)),
            pl.BlockSpec((tm, S5_W), lambda i: (i, 0)),
            pl.BlockSpec((tm, W), lambda i: (i, 0)),
            pl.BlockSpec((tm, W), lambda i: (i, 0)),
        ] + r_out,
        out_shape=[
            jax.ShapeDtypeStruct((T, S5_W), BF16),
            jax.ShapeDtypeStruct((T, S5_W), F32),
            jax.ShapeDtypeStruct((T, W), F32),
            jax.ShapeDtypeStruct((T, W), F32),
        ] + r_shapes,
        scratch_shapes=[pltpu.VMEM((tm + 8, W), F32), pltpu.VMEM((tm + 8, W), F32)] + r_sems,
        compiler_params=_cparams(1),
    )(z, z, p["b4re"], p["b4im"], p["c4re"], p["c4im"], p["dvec"], p["wglu"], p["bglu"], p["coef_f"], *r_args)


def _s5_bwd(dy, z, yraw, sre, sim, p, rider=None):
    T = z.shape[0]
    tm = min(256, T)
    nt = T // tm
    nblk = tm // 8
    W = STATE_W
    rev = lambda i: nt - 1 - i

    def body(dya_ref, xa_ref, ga_ref, yraw_ref, sre_ref, sim_ref, hre_ref, him_ref,
             bre_t_ref, bim_t_ref, cre_t_ref, cim_t_ref, dv_ref, wg_ref, wgt_ref, bg_ref, coef_ref,
             dz_ref, dbre_ref, dbim_ref, dcre_ref, dcim_ref, dd_ref, dwg_ref, dbg_ref, da_ref,
             wre, wim, dyr_ref, xa_stage, ga_stage, xa_sem, ga_sem):
        i = pl.program_id(0)
        xa_out = _ColumnWriter(xa_stage, xa_sem, dz_ref, 0, i, nt)
        ga_out = _ColumnWriter(ga_stage, ga_sem, dz_ref, 6 * 512, i, nt)
        dxa_ref, dga_ref = xa_out.slot(), ga_out.slot()

        @pl.when(i == 0)
        def _():
            wre[tm:tm + 8, :] = jnp.zeros((8, W), F32)
            wim[tm:tm + 8, :] = jnp.zeros((8, W), F32)
            for ref in (dbre_ref, dbim_ref, dcre_ref, dcim_ref, dd_ref, dwg_ref, dbg_ref, da_ref):
                ref[...] = jnp.zeros_like(ref)

        xa = xa_ref[...]
        dya = dya_ref[...]
        yg, dgelu = _gelu_and_grad(yraw_ref[...])
        ygb = yg.astype(BF16)
        q = jnp.dot(ygb, wg_ref[...], preferred_element_type=F32) + bg_ref[...]
        sq = jax.nn.sigmoid(q)
        sga, dsga = _silu_and_grad(ga_ref[...])
        dga_ref[...] = (dya * (yg * sq) * dsga).astype(BF16)
        dya0 = dya * sga
        dq = dya0 * yg * sq * (1.0 - sq)
        dqb = dq.astype(BF16)
        dyg = dya0 * sq + jnp.dot(dqb, wgt_ref[...], preferred_element_type=F32)
        dwg_ref[...] += _dot_tn(ygb, dqb)
        dbg_ref[...] += jnp.sum(dq, axis=0, keepdims=True)
        dyraw = dyg * dgelu
        dd_ref[...] += jnp.sum(dyraw * xa, axis=0, keepdims=True)
        dyr_ref[...] = dyraw.astype(BF16)

        for j in range(SUPER):
            dj = dyr_ref[:, j * 128:(j + 1) * 128]
            wre[0:tm, j * 512:(j + 1) * 512] = jnp.dot(dj, cre_t_ref[j], preferred_element_type=F32)
            wim[0:tm, j * 512:(j + 1) * 512] = jnp.dot(dj, cim_t_ref[j], preferred_element_type=F32)

        row0 = lax.broadcasted_iota(jnp.int32, (8, SCAN_COLS), 0) == 0
        head_on = (i < nt - 1).astype(F32)

        def one_block(base, first):
            for cc in range(W // SCAN_COLS):
                cs = pl.ds(cc * SCAN_COLS, SCAN_COLS)
                r, im = _scan_block(wre[pl.ds(base, 8), cs], wim[pl.ds(base, 8), cs], coef_ref, cs, True)
                cr = wre[pl.ds(base + 8, 1), cs]
                ci = wim[pl.ds(base + 8, 1), cs]
                pr = coef_ref[6, :, cs]
                pi = coef_ref[7, :, cs]
                r, im = r + pr * cr - pi * ci, im + pr * ci + pi * cr
                wre[pl.ds(base, 8), cs] = r
                wim[pl.ds(base, 8), cs] = im
                if first:
                    pre = hre_ref[7:8, cs] * head_on
                    pim = him_ref[7:8, cs] * head_on
                else:
                    pre = sre_ref[pl.ds(base - 1, 1), cs]
                    pim = sim_ref[pl.ds(base - 1, 1), cs]
                spr = jnp.where(row0, pre, pltpu.roll(sre_ref[pl.ds(base, 8), cs], 1, 0))
                spi = jnp.where(row0, pim, pltpu.roll(sim_ref[pl.ds(base, 8), cs], 1, 0))
                da_ref[0, :, cs] += r * spr + im * spi
                da_ref[1, :, cs] += im * spr - r * spi

        def blk(b, carry):
            one_block(pl.multiple_of((nblk - 1 - b) * 8, 8), False)
            return carry

        lax.fori_loop(0, nblk - 1, blk, 0)
        one_block(0, True)
        wre[tm:tm + 8, :] = wre[0:8, :]
        wim[tm:tm + 8, :] = wim[0:8, :]

        xab = xa.astype(BF16)
        for j in range(SUPER):
            cols = slice(j * 512, (j + 1) * 512)
            gre = wre[0:tm, cols].astype(BF16)
            gim = wim[0:tm, cols].astype(BF16)
            xj = xab[:, j * 128:(j + 1) * 128]
            dj = dyr_ref[:, j * 128:(j + 1) * 128]
            dbre_ref[j] += _dot_tn(xj, gre)
            dbim_ref[j] += _dot_tn(xj, gim)
            dcre_ref[j] += _dot_tn(sre_ref[:, cols], dj)
            dcim_ref[j] += _dot_tn(sim_ref[:, cols], dj)
            dxj = jnp.dot(gre, bre_t_ref[j], preferred_element_type=F32)
            dxj += jnp.dot(gim, bim_t_ref[j], preferred_element_type=F32)
            dxj += dyraw[:, j * 128:(j + 1) * 128] * dv_ref[:, j * 128:(j + 1) * 128]
            dxa_ref[:, j * 128:(j + 1) * 128] = dxj.astype(BF16)
        xa_out.send(rev(i) * tm)
        ga_out.send(rev(i) * tm)

    acc = lambda shape: _full(shape)
    hb = tm // 8
    r_in, r_out, r_shapes, r_sems, r_args = _rider_specs(rider)
    stages, stage_sems = _stage_scratch(tm, (S5_W, S5_W))
    return pl.pallas_call(
        _ride(body, 17, 9, rider, lambda: pl.program_id(0) == 0, lambda: pl.program_id(0) == nt - 1),
        name="s5_bwd" + ("" if rider is None else "_ride"),
        grid=(nt,),
        in_specs=[
            pl.BlockSpec((tm, S5_W), lambda i: (rev(i), 0)),
            pl.BlockSpec((tm, S5_W), lambda i: (rev(i), 0)),
            pl.BlockSpec((tm, S5_W), lambda i: (rev(i), 6)),
            pl.BlockSpec((tm, S5_W), lambda i: (rev(i), 0)),
            pl.BlockSpec((tm, W), lambda i: (rev(i), 0)),
            pl.BlockSpec((tm, W), lambda i: (rev(i), 0)),
            pl.BlockSpec((8, W), lambda i: (jnp.maximum(rev(i) * hb - 1, 0), 0)),
            pl.BlockSpec((8, W), lambda i: (jnp.maximum(rev(i) * hb - 1, 0), 0)),
            _full((SUPER, 512, 128)), _full((SUPER, 512, 128)),
            _full((SUPER, 128, 512)), _full((SUPER, 128, 512)),
            _full((1, S5_W)), _full((S5_W, S5_W)), _full((S5_W, S5_W)), _full((1, S5_W)),
            _full((8, 8, W)),
        ] + r_in,
        out_specs=[
            pl.BlockSpec(memory_space=pl.ANY),
            acc((SUPER, 128, 512)), acc((SUPER, 128, 512)),
            acc((SUPER, 512, 128)), acc((SUPER, 512, 128)),
            acc((1, S5_W)), acc((S5_W, S5_W)), acc((1, S5_W)), acc((2, 8, W)),
        ] + r_out,
        out_shape=[
            jax.ShapeDtypeStruct((T, IN_COLS), BF16),
            jax.ShapeDtypeStruct((SUPER, 128, 512), F32), jax.ShapeDtypeStruct((SUPER, 128, 512), F32),
            jax.ShapeDtypeStruct((SUPER, 512, 128), F32), jax.ShapeDtypeStruct((SUPER, 512, 128), F32),
            jax.ShapeDtypeStruct((1, S5_W), F32), jax.ShapeDtypeStruct((S5_W, S5_W), F32),
            jax.ShapeDtypeStruct((1, S5_W), F32), jax.ShapeDtypeStruct((2, 8, W), F32),
        ] + r_shapes,
        scratch_shapes=[pltpu.VMEM((tm + 8, W), F32), pltpu.VMEM((tm + 8, W), F32), pltpu.VMEM((tm, S5_W), BF16)]
        + stages + stage_sems + r_sems,
        compiler_params=_cparams(1),
    )(dy, z, z, yraw, sre, sim, sre, sim,
      p["b4re_t"], p["b4im_t"], p["c4re_t"], p["c4im_t"], p["dvec"], p["wglu"], p["wglu_t"], p["bglu"], p["coef_r"],
      *r_args)


def _ln_fwd(vf, lng, lnb):
    mu = jnp.mean(vf, axis=-1, keepdims=True)
    d = vf - mu
    rstd = lax.rsqrt(jnp.mean(d * d, axis=-1, keepdims=True) + LN_EPS)
    xh = d * rstd
    return xh, rstd, xh * lng + lnb


def _col_block(tm, b):
    return pl.BlockSpec((tm, 512), lambda i: (i, b))


def _ln_halves(vf0, vf1):
    mu = (jnp.sum(vf0, axis=-1, keepdims=True) + jnp.sum(vf1, axis=-1, keepdims=True)) * (1.0 / SGU_W)
    d0, d1 = vf0 - mu, vf1 - mu
    var = (jnp.sum(d0 * d0, axis=-1, keepdims=True) + jnp.sum(d1 * d1, axis=-1, keepdims=True)) * (1.0 / SGU_W)
    rstd = lax.rsqrt(var + LN_EPS)
    return d0 * rstd, d1 * rstd, rstd


def _sgu_fwd(z, ws, bsf, lng, lnb):
    T = z.shape[0]
    tm = min(512, T)

    def body(u0, u1, v0, v1, g0, g1, ws_ref, bs_ref, lng_ref, lnb_ref, yb_ref, vn_ref):
        for c in range(tm // CHUNK):
            rows = slice(c * CHUNK, (c + 1) * CHUNK)
            xh0, xh1, _ = _ln_halves(_gelu(v0[rows, :]), _gelu(v1[rows, :]))
            vn_ref[:, 0:512] = (xh0 * lng_ref[:, 0:512] + lnb_ref[:, 0:512]).astype(BF16)
            vn_ref[:, 512:1024] = (xh1 * lng_ref[:, 512:1024] + lnb_ref[:, 512:1024]).astype(BF16)
            for half, (u_ref, g_ref) in enumerate(((u0, g0), (u1, g1))):
                sg, _ = _silu_and_grad(g_ref[rows, :])
                m = _gelu(u_ref[rows, :]) * sg
                for hh in range(SGU_HEADS // 2):
                    h = half * (SGU_HEADS // 2) + hh
                    cols = slice(h * 128, (h + 1) * 128)
                    s = jnp.dot(ws_ref[h], vn_ref[:, cols], preferred_element_type=F32) + bs_ref[:, cols]
                    yb_ref[rows, cols] = (m[:, hh * 128:(hh + 1) * 128] * s).astype(BF16)

    return pl.pallas_call(
        body,
        name="sgu_fwd",
        grid=(T // tm,),
        in_specs=[_col_block(tm, b) for b in (1, 2, 3, 4, 7, 8)] + [
            _full((SGU_HEADS, CHUNK, CHUNK)), _full((CHUNK, SGU_W)), _full((1, SGU_W)), _full((1, SGU_W)),
        ],
        out_specs=pl.BlockSpec((tm, SGU_W), lambda i: (i, 0)),
        out_shape=jax.ShapeDtypeStruct((T, SGU_W), BF16),
        scratch_shapes=[pltpu.VMEM((CHUNK, SGU_W), BF16)],
        compiler_params=_cparams(1),
    )(z, z, z, z, z, z, ws, bsf, lng, lnb)


def _sgu_bwd(dy, z, dz, ws, ws_t, bsf, lng, lnb, rider=None):
    T = z.shape[0]
    tm = min(512, T)
    HH = SGU_HEADS // 2

    def body(u0, u1, v0, v1, g0, g1, dy0, dy1, ws_ref, wst_ref, bs_ref, lng_ref, lnb_ref, dz_in,
             dz_ref, dws_ref, dbs_ref, dlng_ref, dlnb_ref, vn_ref, dvn_ref, *stage):
        step = pl.program_id(0)
        outs = [_ColumnWriter(stage[k], stage[3 + k], dz_ref, col, step, T // tm)
                for k, col in enumerate((512, 1536, 3584))]
        du_ref, dv_ref, dgb_ref = (o.slot() for o in outs)

        @pl.when(step == 0)
        def _():
            for ref in (dws_ref, dbs_ref, dlng_ref, dlnb_ref):
                ref[...] = jnp.zeros_like(ref)

        for c in range(tm // CHUNK):
            rows = slice(c * CHUNK, (c + 1) * CHUNK)
            vf0, dgv0 = _gelu_and_grad(v0[rows, :])
            vf1, dgv1 = _gelu_and_grad(v1[rows, :])
            xh0, xh1, rstd = _ln_halves(vf0, vf1)
            vn_ref[:, 0:512] = (xh0 * lng_ref[:, 0:512] + lnb_ref[:, 0:512]).astype(BF16)
            vn_ref[:, 512:1024] = (xh1 * lng_ref[:, 512:1024] + lnb_ref[:, 512:1024]).astype(BF16)
            for half, (u_ref, g_ref, dy_ref) in enumerate(((u0, g0, dy0), (u1, g1, dy1))):
                ug, dgu = _gelu_and_grad(u_ref[rows, :])
                sg, dsg = _silu_and_grad(g_ref[rows, :])
                dyb = dy_ref[rows, :]
                dyb0 = dyb * sg
                ds_half = dyb0 * ug
                du_scale = dyb0 * dgu
                dg_scale = dyb * ug * dsg
                for hh in range(HH):
                    h = half * HH + hh
                    cols = slice(h * 128, (h + 1) * 128)
                    lc = slice(hh * 128, (hh + 1) * 128)
                    s = jnp.dot(ws_ref[h], vn_ref[:, cols], preferred_element_type=F32) + bs_ref[:, cols]
                    du_ref[rows, cols] = (du_scale[:, lc] * s).astype(BF16)
                    dgb_ref[rows, cols] = (dg_scale[:, lc] * s).astype(BF16)
                    ds = ds_half[:, lc]
                    dbs_ref[:, cols] += ds
                    dsb = ds.astype(BF16)
                    dws_ref[h] += _dot_nt(dsb, vn_ref[:, cols])
                    dvn_ref[:, cols] = jnp.dot(wst_ref[h], dsb, preferred_element_type=F32)
            dvn0 = dvn_ref[:, 0:512]
            dvn1 = dvn_ref[:, 512:1024]
            dlnb_ref[:, 0:512] += jnp.sum(dvn0, axis=0, keepdims=True)
            dlnb_ref[:, 512:1024] += jnp.sum(dvn1, axis=0, keepdims=True)
            dlng_ref[:, 0:512] += jnp.sum(dvn0 * xh0, axis=0, keepdims=True)
            dlng_ref[:, 512:1024] += jnp.sum(dvn1 * xh1, axis=0, keepdims=True)
            dxh0 = dvn0 * lng_ref[:, 0:512]
            dxh1 = dvn1 * lng_ref[:, 512:1024]
            m1 = (jnp.sum(dxh0, axis=-1, keepdims=True) + jnp.sum(dxh1, axis=-1, keepdims=True)) * (1.0 / SGU_W)
            m2 = (jnp.sum(dxh0 * xh0, axis=-1, keepdims=True) + jnp.sum(dxh1 * xh1, axis=-1, keepdims=True)) * (1.0 / SGU_W)
            dv_ref[rows, 0:512] = (rstd * (dxh0 - m1 - xh0 * m2) * dgv0).astype(BF16)
            dv_ref[rows, 512:1024] = (rstd * (dxh1 - m1 - xh1 * m2) * dgv1).astype(BF16)
        for o in outs:
            o.send(step * tm)

    anyspec = pl.BlockSpec(memory_space=pl.ANY)
    r_in, r_out, r_shapes, r_sems, r_args = _rider_specs(rider)
    stages, stage_sems = _stage_scratch(tm, (SGU_W, SGU_W, SGU_W))
    return pl.pallas_call(
        _ride(body, 14, 5, rider, lambda: pl.program_id(0) == 0, lambda: pl.program_id(0) == T // tm - 1),
        name="sgu_bwd" + ("" if rider is None else "_ride"),
        grid=(T // tm,),
        in_specs=[_col_block(tm, b) for b in (1, 2, 3, 4, 7, 8)] + [_col_block(tm, 1), _col_block(tm, 2)] + [
            _full((SGU_HEADS, CHUNK, CHUNK)), _full((SGU_HEADS, CHUNK, CHUNK)),
            _full((CHUNK, SGU_W)), _full((1, SGU_W)), _full((1, SGU_W)), anyspec,
        ] + r_in,
        out_specs=[anyspec,
                   _full((SGU_HEADS, CHUNK, CHUNK)), _full((CHUNK, SGU_W)), _full((1, SGU_W)), _full((1, SGU_W))] + r_out,
        input_output_aliases={13: 0},
        out_shape=[
            jax.ShapeDtypeStruct((T, IN_COLS), BF16),
            jax.ShapeDtypeStruct((SGU_HEADS, CHUNK, CHUNK), F32), jax.ShapeDtypeStruct((CHUNK, SGU_W), F32),
            jax.ShapeDtypeStruct((1, SGU_W), F32), jax.ShapeDtypeStruct((1, SGU_W), F32),
        ] + r_shapes,
        scratch_shapes=[pltpu.VMEM((CHUNK, SGU_W), BF16), pltpu.VMEM((CHUNK, SGU_W), F32)] + stages + stage_sems + r_sems,
        compiler_params=_cparams(1),
    )(z, z, z, z, z, z, dy, dy, ws, ws_t, bsf, lng, lnb, dz, *r_args)


def _pool_den(first_row, n):
    return (lax.broadcasted_iota(jnp.int32, (n, 1), 0) + first_row + 1).astype(F32)


def _pool_p(ext, xc, pos, tm):
    w2 = ext + pltpu.roll(ext, 1, 0)
    w4 = w2 + pltpu.roll(w2, 2, 0)
    w8 = w4 + pltpu.roll(w4, 4, 0)
    w16 = w8 + pltpu.roll(w8, 8, 0)
    out = []
    for g, (w, ws) in enumerate(zip(POOL_WINDOWS, (w2, w4, w8, w16))):
        cols = slice(g * 128, (g + 1) * 128)
        mean = ws[POOL_HALO:POOL_HALO + tm, cols] / jnp.minimum(pos, float(w))
        out.append(mean - xc[:, cols])
    return out


def _pool_fwd(z, wp, scale):
    T = z.shape[0]
    tm = min(512, T)
    hb = tm // POOL_HALO

    def body(xc_ref, hx_ref, gc_ref, wp_ref, sc_ref, yc_ref):
        i = pl.program_id(0)
        xc = xc_ref[...]
        halo = hx_ref[...] * (i > 0).astype(F32)
        ext = jnp.concatenate([halo, xc], axis=0)
        ps = _pool_p(ext, xc, _pool_den(i * tm, tm), tm)
        sg, _ = _silu_and_grad(gc_ref[...])
        for g in range(4):
            cols = slice(g * 128, (g + 1) * 128)
            pw = _dot(ps[g], wp_ref[g])
            yc_ref[:, cols] = (pw * sc_ref[:, cols] * sg[:, cols]).astype(BF16)

    return pl.pallas_call(
        body,
        name="pool_fwd",
        grid=(T // tm,),
        in_specs=[
            _col_block(tm, 5),
            pl.BlockSpec((POOL_HALO, 512), lambda i: (jnp.maximum(i * hb - 1, 0), 5)),
            _col_block(tm, 9),
            _full((4, 128, 128)), _full((1, POOL_W)),
        ],
        out_specs=pl.BlockSpec((tm, POOL_W), lambda i: (i, 0)),
        out_shape=jax.ShapeDtypeStruct((T, POOL_W), BF16),
        compiler_params=_cparams(1),
    )(z, z, z, wp, scale)


def _pool_bwd(dy, z, dz, wp, wp_t, scale):
    T = z.shape[0]
    tm = min(512, T)
    nt = T // tm
    hb = tm // POOL_HALO
    last_hb = T // POOL_HALO - 1
    L = tm + POOL_HALO

    def body(xc_ref, hx_ref, gc_ref, gn_ref, dyc_ref, dyn_ref, wp_ref, wpt_ref, sc_ref, dz_in,
             dz_ref, dwp_ref, dsc_ref, xc_stage, gc_stage, xc_sem, gc_sem):
        i = pl.program_id(0)
        xc_out = _ColumnWriter(xc_stage, xc_sem, dz_ref, 5 * 512, i, nt)
        gc_out = _ColumnWriter(gc_stage, gc_sem, dz_ref, 9 * 512, i, nt)
        dxc_ref, dgc_ref = xc_out.slot(), gc_out.slot()

        @pl.when(i == 0)
        def _():
            dwp_ref[...] = jnp.zeros_like(dwp_ref)
            dsc_ref[...] = jnp.zeros_like(dsc_ref)

        xc = xc_ref[...]
        halo = hx_ref[...] * (i > 0).astype(F32)
        pos = _pool_den(i * tm, tm)
        ps = _pool_p(jnp.concatenate([halo, xc], axis=0), xc, pos, tm)
        sg, dsg = _silu_and_grad(gc_ref[...])
        dyc = dyc_ref[...]
        dyc0 = dyc * sg
        dpw = dyc0 * sc_ref[...]
        sgn, _ = _silu_and_grad(gn_ref[...])
        dpwn = dyn_ref[...] * sgn * sc_ref[...] * (i < nt - 1).astype(F32)
        posn = _pool_den((i + 1) * tm, POOL_HALO)
        dps, qs = [], []
        for g, w in enumerate(POOL_WINDOWS):
            cols = slice(g * 128, (g + 1) * 128)
            pw = _dot(ps[g], wp_ref[g])
            dgc_ref[:, cols] = (dyc[:, cols] * pw * sc_ref[:, cols] * dsg[:, cols]).astype(BF16)
            dsc_ref[:, cols] += jnp.sum(dyc0[:, cols] * pw, axis=0, keepdims=True)
            dwp_ref[g] += _dot_tn(ps[g], dpw[:, cols])
            dp = _dot(dpw[:, cols], wpt_ref[g])
            dpn = _dot(dpwn[:, cols], wpt_ref[g])
            dps.append(dp)
            qs.append(jnp.concatenate([dp / jnp.minimum(pos, float(w)), dpn / jnp.minimum(posn, float(w))], axis=0))
        ext = jnp.concatenate(qs, axis=1)
        f2 = ext + pltpu.roll(ext, L - 1, 0)
        f4 = f2 + pltpu.roll(f2, L - 2, 0)
        f8 = f4 + pltpu.roll(f4, L - 4, 0)
        f16 = f8 + pltpu.roll(f8, L - 8, 0)
        for g, f in enumerate((f2, f4, f8, f16)):
            cols = slice(g * 128, (g + 1) * 128)
            dxc_ref[:, cols] = (f[0:tm, cols] - dps[g]).astype(BF16)
        xc_out.send(i * tm)
        gc_out.send(i * tm)

    nxt = lambda i: jnp.minimum((i + 1) * hb, last_hb)
    anyspec = pl.BlockSpec(memory_space=pl.ANY)
    stages, stage_sems = _stage_scratch(tm, (POOL_W, POOL_W))
    return pl.pallas_call(
        body,
        name="pool_bwd",
        grid=(nt,),
        in_specs=[
            _col_block(tm, 5),
            pl.BlockSpec((POOL_HALO, 512), lambda i: (jnp.maximum(i * hb - 1, 0), 5)),
            _col_block(tm, 9),
            pl.BlockSpec((POOL_HALO, 512), lambda i: (nxt(i), 9)),
            _col_block(tm, 3),
            pl.BlockSpec((POOL_HALO, 512), lambda i: (nxt(i), 3)),
            _full((4, 128, 128)), _full((4, 128, 128)), _full((1, POOL_W)), anyspec,
        ],
        out_specs=[anyspec, _full((4, 128, 128)), _full((1, POOL_W))],
        input_output_aliases={9: 0},
        out_shape=[
            jax.ShapeDtypeStruct((T, IN_COLS), BF16),
            jax.ShapeDtypeStruct((4, 128, 128), F32), jax.ShapeDtypeStruct((1, POOL_W), F32),
        ],
        scratch_shapes=stages + stage_sems,
        compiler_params=_cparams(1),
    )(z, z, z, z, dy, dy, wp, wp_t, scale, dz)


def _row_tile(rows, cols):
    tr = 8
    while tr * 2 * cols * 4 <= 2 * 1024 * 1024 and rows % (tr * 2) == 0:
        tr *= 2
    return tr


def _add_own_half(part, recv, cidx):
    _, _, R2, C = part.shape
    tr = _row_tile(R2, C)

    def body(c_ref, a_ref, r_ref, o_ref):
        o_ref[...] = (a_ref[...] + r_ref[...]).astype(BF16)

    return pl.pallas_call(
        body,
        name="add_own_half",
        grid_spec=pltpu.PrefetchScalarGridSpec(
            num_scalar_prefetch=1,
            grid=(N_CHIPS, R2 // tr),
            in_specs=[
                pl.BlockSpec((None, None, tr, C), lambda j, i, c: (j, c[0], i, 0)),
                pl.BlockSpec((None, tr, C), lambda j, i, c: (j, i, 0)),
            ],
            out_specs=pl.BlockSpec((None, tr, C), lambda j, i, c: (j, i, 0)),
        ),
        out_shape=jax.ShapeDtypeStruct((N_CHIPS, R2, C), BF16),
        compiler_params=_cparams(2),
    )(cidx, part, recv)


def _add2(a, b):
    R, C = a.shape
    tr = _row_tile(R, C)

    def body(a_ref, b_ref, o_ref):
        o_ref[...] = a_ref[...] + b_ref[...]

    spec = pl.BlockSpec((tr, C), lambda i: (i, 0))
    return pl.pallas_call(
        body, name="add2", grid=(R // tr,), in_specs=[spec, spec], out_specs=spec,
        out_shape=jax.ShapeDtypeStruct((R, C), F32), compiler_params=_cparams(1),
    )(a, b)


def _sum_chips(parts):
    _, R, C = parts.shape
    tr = _row_tile(R, N_CHIPS * C)

    def body(p_ref, o_ref):
        p = [p_ref[j].astype(F32) for j in range(N_CHIPS)]
        o_ref[...] = ((p[0] + p[1]) + p[2]) + p[3]

    return pl.pallas_call(
        body, name="sum_chips", grid=(R // tr,),
        in_specs=[pl.BlockSpec((N_CHIPS, tr, C), lambda i: (0, i, 0))],
        out_specs=pl.BlockSpec((tr, C), lambda i: (i, 0)),
        out_shape=jax.ShapeDtypeStruct((R, C), F32), compiler_params=_cparams(1),
    )(parts)


def _adamw_math(w, g, m, v):
    m = ADAM_B1 * m + (1.0 - ADAM_B1) * g
    v = ADAM_B2 * v + (1.0 - ADAM_B2) * (g * g)
    m_hat = m / (1.0 - ADAM_B1 ** ADAM_STEP)
    v_hat = v / (1.0 - ADAM_B2 ** ADAM_STEP)
    delta = -ADAM_LR * (m_hat / (jnp.sqrt(v_hat) + ADAM_EPS) + ADAM_WD * w)
    return delta, m, v


def _adamw(w, g, m, v):
    R, C = w.shape
    tr = _row_tile(R, C)

    def body(w_ref, g_ref, m_ref, v_ref, d_ref, mo_ref, vo_ref):
        d_ref[...], mo_ref[...], vo_ref[...] = _adamw_math(w_ref[...], g_ref[...], m_ref[...], v_ref[...])

    spec = pl.BlockSpec((tr, C), lambda i: (i, 0))
    shp = jax.ShapeDtypeStruct((R, C), F32)
    return pl.pallas_call(
        body, name="adamw", grid=(R // tr,), in_specs=[spec] * 4, out_specs=[spec] * 3,
        out_shape=[shp] * 3, compiler_params=_cparams(1),
    )(w, g, m, v)


def _adamw_halves(w, mine, theirs, m, v, cidx, rider=None):
    _, _, R2, C = w.shape
    tr = _row_tile(R2, C)
    nr = R2 // tr

    def body(c_ref, w_ref, a0_ref, b0_ref, a1_ref, b1_ref, m_ref, v_ref, g_ref, d_ref, mo_ref, vo_ref):
        own = pl.program_id(1) == c_ref[0]
        g0 = jnp.where(own, a0_ref[...], b0_ref[...])
        g1 = jnp.where(own, a1_ref[...], b1_ref[...])
        g = jnp.where(pl.program_id(0) == 0, g0, g1)
        g_ref[...] = g
        d_ref[...], mo_ref[...], vo_ref[...] = _adamw_math(w_ref[...], g, m_ref[...], v_ref[...])

    full = pl.BlockSpec((None, None, tr, C), lambda l, h, i, c: (l, h, i, 0))

    def pick(layer, mine_side):
        def index(l, h, i, c):
            used = (l == layer) & ((h == c[0]) == mine_side)
            return (jnp.where(used, i, 0), 0)
        return pl.BlockSpec((tr, C), index)

    shp = jax.ShapeDtypeStruct(w.shape, F32)
    r_in, r_out, r_shapes, r_sems, r_args = _rider_specs(rider)
    last = lambda: (pl.program_id(0) == 1) & (pl.program_id(1) == 1) & (pl.program_id(2) == nr - 1)
    first = lambda: (pl.program_id(0) == 0) & (pl.program_id(1) == 0) & (pl.program_id(2) == 0)
    return pl.pallas_call(
        _ride(body, 8, 4, rider, first, last),
        name="adamw_halves" + ("" if rider is None else "_ride"),
        grid_spec=pltpu.PrefetchScalarGridSpec(
            num_scalar_prefetch=1, grid=(2, 2, nr),
            in_specs=[full, pick(0, True), pick(0, False), pick(1, True), pick(1, False), full, full] + r_in,
            out_specs=[full] * 4 + r_out,
            scratch_shapes=r_sems,
        ),
        out_shape=[shp] * 4 + r_shapes,
        compiler_params=_cparams(3),
    )(cidx, w, mine[0], theirs[0], mine[1], theirs[1], m, v, *r_args)


_ANY = pl.BlockSpec(memory_space=pl.ANY)


def _mesh_pos():
    return lax.axis_index("x"), lax.axis_index("y"), lax.axis_index("c")


def _other_chips(x, y):
    return [(2 * x + (1 - y), x, 1 - y), (2 * (1 - x) + y, 1 - x, y), (2 * (1 - x) + (1 - y), 1 - x, 1 - y)]


def _gathered_shapes(shards):
    return [jax.ShapeDtypeStruct((2, N_CHIPS) + s.shape[1:], s.dtype) for s in shards]


def _gather_sems(n):
    return [pltpu.SemaphoreType.DMA((2 * n,)), pltpu.SemaphoreType.DMA((6 * n,)), pltpu.SemaphoreType.DMA((6 * n,))]


def _gather_steps(ins, outs, lsem, ssem, rsem):
    n = len(ins)
    x, y, c = _mesh_pos()
    me = 2 * x + y
    sib = (x, y, 1 - c)
    chips = _other_chips(x, y)

    def ici(k, d):
        return pltpu.make_async_remote_copy(
            ins[k].at[c], outs[k].at[c, me], ssem.at[6 * k + d], rsem.at[6 * k + d],
            device_id=(chips[d][1], chips[d][2], c), device_id_type=MESH_ID)

    def landed(k, d):
        return pltpu.make_async_remote_copy(
            ins[k].at[c], outs[k].at[c, chips[d][0]], ssem.at[6 * k + d], rsem.at[6 * k + d],
            device_id=sib, device_id_type=MESH_ID)

    def fwd(k, d, half):
        return pltpu.make_async_remote_copy(
            outs[k].at[half, chips[d][0]], outs[k].at[half, chips[d][0]], ssem.at[6 * k + 3 + d],
            rsem.at[6 * k + 3 + d], device_id=sib, device_id_type=MESH_ID)

    def local(k, h):
        return pltpu.make_async_copy(ins[k].at[h], outs[k].at[h, me], lsem.at[2 * k + h])

    def start():
        for k in range(n):
            for h in range(2):
                local(k, h).start()
            for d in range(3):
                ici(k, d).start()

    def mid():
        for d in range(3):
            for k in range(n):
                landed(k, d).wait_recv()
                fwd(k, d, c).start()

    def end():
        for d in range(3):
            for k in range(n):
                fwd(k, d, 1 - c).wait_recv()
        for k in range(n):
            for d in range(3):
                ici(k, d).wait_send()
                fwd(k, d, c).wait_send()
            for h in range(2):
                local(k, h).wait()

    return start, mid, end


def _gather_rider(shards):
    return _Rider(shards, _gathered_shapes(shards), _gather_sems(len(shards)), _gather_steps)


def _pair_rider(arrs, other_half):
    n = len(arrs)

    def steps(ins, outs, ssem, rsem):
        x, y, c = _mesh_pos()

        def copy(k):
            return pltpu.make_async_remote_copy(ins[k].at[:, 1 - c] if other_half else ins[k], outs[k], ssem.at[k],
                                                rsem.at[k], device_id=(x, y, 1 - c), device_id_type=MESH_ID)

        def start():
            for k in range(n):
                copy(k).start()

        def end():
            for k in range(n):
                copy(k).wait()

        return start, end

    shapes = [jax.ShapeDtypeStruct(a.shape[:1] + a.shape[2:] if other_half else a.shape, a.dtype) for a in arrs]
    return _Rider(arrs, shapes, [pltpu.SemaphoreType.DMA((n,)), pltpu.SemaphoreType.DMA((n,))], steps)


def _chip_rider(arrs, broadcast):
    n = len(arrs)

    def steps(ins, outs, lsem, ssem, rsem):
        x, y, c = _mesh_pos()
        me = 2 * x + y

        def copies():
            cps = [pltpu.make_async_copy(ins[k] if broadcast else ins[k].at[me], outs[k].at[me], lsem.at[k])
                   for k in range(n)]
            for k in range(n):
                for d, (j, tx, ty) in enumerate(_other_chips(x, y)):
                    cps.append(pltpu.make_async_remote_copy(
                        ins[k] if broadcast else ins[k].at[j], outs[k].at[me], ssem.at[3 * k + d], rsem.at[3 * k + d],
                        device_id=(tx, ty, c), device_id_type=MESH_ID))
            return cps

        def start():
            for cp in copies():
                cp.start()

        def end():
            for cp in copies():
                cp.wait()

        return start, end

    shapes = [jax.ShapeDtypeStruct(((N_CHIPS,) + a.shape) if broadcast else a.shape, a.dtype) for a in arrs]
    sems = [pltpu.SemaphoreType.DMA((n,)), pltpu.SemaphoreType.DMA((3 * n,)), pltpu.SemaphoreType.DMA((3 * n,))]
    return _Rider(arrs, shapes, sems, steps)


def _run_rider(name, rider):
    n, m = len(rider.arrs), len(rider.out_shapes)

    def body(*refs):
        for step in rider.steps(refs[:n], refs[n:n + m], *refs[n + m:]):
            step()

    return pl.pallas_call(
        body, name=name, in_specs=[_ANY] * n, out_specs=[_ANY] * m, out_shape=rider.out_shapes,
        scratch_shapes=rider.sems,
    )(*rider.arrs)


SMALL = ("norm_g", "lam_re", "lam_im", "b_re", "b_im", "c_re", "c_im", "d_skip", "log_dt", "b_glu", "ln_g", "ln_b",
         "w_s", "b_s", "w_pool", "pool_scale", "final_g")
BIG = ("w_in", "w_glu", "w_out")
WEIGHTS = ("norm_g", "w_in", "lam_re", "lam_im", "b_re", "b_im", "c_re", "c_im", "d_skip", "log_dt", "w_glu", "b_glu",
           "ln_g", "ln_b", "w_s", "b_s", "w_pool", "pool_scale", "w_out", "final_g")
PACK_UNIT = 8 * 128
PACK_ROWS = 1024


def _pack(arrs):
    parts, total = [], 0
    for a in arrs:
        f = a.reshape(-1).astype(F32)
        pad = (-f.shape[0]) % PACK_UNIT
        parts.append(jnp.pad(f, (0, pad)) if pad else f)
        total += f.shape[0] + pad
    tail = (-total) % (PACK_ROWS * 128)
    if tail:
        parts.append(jnp.zeros((tail,), F32))
    return jnp.concatenate(parts).reshape(-1, 128)


def _unpack(buf, like):
    flat = buf.reshape(-1)
    out, off = [], 0
    for a in like:
        n = math.prod(a.shape)
        out.append(flat[off:off + n].reshape(a.shape))
        off += n + ((-n) % PACK_UNIT)
    return out


def _layer_params(l, wt, g_glu):
    a_re, a_im, bb_re, bb_im = _s5_prep(wt["lam_re"][l], wt["lam_im"][l], wt["b_re"][l], wt["b_im"][l], wt["log_dt"][l])
    b4re, b4im = _block_diag_in(bb_re), _block_diag_in(bb_im)
    c4re, c4im = _block_diag_out(wt["c_re"][l]), _block_diag_out(-wt["c_im"][l])
    tr = lambda m: jnp.swapaxes(m, 1, 2).astype(BF16)
    causal = jnp.tril(jnp.ones((CHUNK, CHUNK), dtype=bool))
    ws = jnp.where(causal[None], wt["w_s"][l], 0.0)
    wglu = g_glu[l].reshape(S5_W, S5_W)
    return dict(
        b4re=b4re.astype(BF16), b4im=b4im.astype(BF16), c4re=c4re.astype(BF16), c4im=c4im.astype(BF16),
        b4re_t=tr(b4re), b4im_t=tr(b4im), c4re_t=tr(c4re), c4im_t=tr(c4im),
        dvec=wt["d_skip"][l].reshape(1, S5_W), wglu=wglu, wglu_t=wglu.T, bglu=wt["b_glu"][l].reshape(1, S5_W),
        coef_f=_scan_coefs(a_re, a_im, False), coef_r=_scan_coefs(a_re, a_im, True),
        ws=ws.astype(BF16), ws_t=tr(ws),
        bsf=jnp.broadcast_to(wt["b_s"][l][:, None, :], (SGU_HEADS, CHUNK, CHUNK)).transpose(2, 0, 1).reshape(CHUNK, SGU_W),
        lng=wt["ln_g"][l].reshape(1, SGU_W), lnb=wt["ln_b"][l].reshape(1, SGU_W),
        wp=wt["w_pool"][l].astype(BF16), wp_t=tr(wt["w_pool"][l]), scale=wt["pool_scale"][l].reshape(1, POOL_W),
        norm_g=wt["norm_g"][l].reshape(1, D_MODEL),
    )


def _local_step(x0, tgt, wt, g_in0, rest, rest_gathered, cidx=None, order=None):
    dist = cidx is not None
    xs, saved, params = [x0], [], []
    for l in range(DEPTH):
        norm_g = wt["norm_g"][l].reshape(1, D_MODEL)
        out_rider = None
        if l == 0 and not rest_gathered:
            w_in1, w_glu_b, w_out_b = rest
            h = _rms_h(xs[-1], norm_g)
            z, g_in0, g_glu, g_out = _inproj_first(h, g_in0, order, [w_glu_b, w_out_b])
            out_rider = _gather_rider([w_in1])
        elif l == 0:
            g_in1, g_glu, g_out = rest
            z, h = _inproj(xs[-1], norm_g, g_in0)
        else:
            z, h = _inproj(xs[-1], norm_g, g_in1)
        p = _layer_params(l, wt, g_glu)
        params.append(p)
        ya, yraw, sre, sim, *gathered = _s5_fwd(z, p, out_rider)
        if gathered:
            (g_in1,) = gathered
        yb = _sgu_fwd(z, p["ws"], p["bsf"], p["lng"], p["lnb"])
        yc = _pool_fwd(z, p["wp"], p["scale"])
        w_out = g_out[l].reshape(D_MODEL, D_MODEL)
        if l < DEPTH - 1:
            xn, y = _outproj(ya, yb, yc, w_out, xs[-1])
            xs.append(xn)
        else:
            dx, loss, dfg, y = _outproj_loss(ya, yb, yc, w_out, xs[-1], wt["final_g"].reshape(1, D_MODEL), tgt)
        saved.append((z, h, yraw, sre, sim, y))
    g_in = (g_in0, g_in1)

    gr = {k: [None] * DEPTH for k in WEIGHTS if k != "final_g"}
    mine, theirs, chip_sum = [None] * DEPTH, [None] * DEPTH, None
    halves = lambda a, rows: a.reshape(N_CHIPS, 2, rows // 2, a.shape[-1])
    for l in reversed(range(DEPTH)):
        p = params[l]
        z, h, yraw, sre, sim, y = saved[l]
        w_out = g_out[l].reshape(D_MODEL, D_MODEL)
        dy = _outproj_bwd_dy(dx, w_out)
        gr["w_out"][l] = _outproj_bwd_dw(y, dx)
        ride_c = _chip_rider(chip_sum, False) if dist and l == 0 else None
        dz, dbre, dbim, dcre, dcim, dd, dwg, dbg, da, *landed = _s5_bwd(dy, z, yraw, sre, sim, p, ride_c)
        if ride_c:
            mine[1] = [_sum_chips(r) for r in landed]
        ride_e = _pair_rider(mine[1], False) if dist and l == 0 else None
        dz, dws, dbsf, dlng, dlnb, *got = _sgu_bwd(dy, z, dz, p["ws"], p["ws_t"], p["bsf"], p["lng"], p["lnb"], ride_e)
        if ride_e:
            theirs[1] = got
        dz, dwp, dsc = _pool_bwd(dy, z, dz, p["wp"], p["wp_t"], p["scale"])
        gr["w_in"][l] = _inproj_bwd_dw(h, dz)
        if not dist:
            dx, dng = _inproj_bwd_dx(dz, g_in[l], xs[l], p["norm_g"], dx)
        else:
            part = [halves(gr["w_in"][l], D_MODEL), halves(dwg, S5_W // N_CHIPS),
                    halves(gr["w_out"][l], D_MODEL // N_CHIPS)]
            ride_a = _pair_rider(part, True)
            if l == 1:
                dx, dng, *from_sib = _inproj_bwd_dx(dz, g_in[l], xs[l], p["norm_g"], dx, ride_a)
                chip_sum = [_add_own_half(a, r, cidx) for a, r in zip(part, from_sib)]
            else:
                nt = x0.shape[0] // _dx_tile(x0.shape[0])
                n_top = max(nt // 4, 1)
                dx_top, dng_top, *from_sib = _inproj_bwd_dx(dz, g_in[l], xs[l], p["norm_g"], dx, ride_a,
                                                            tiles=(0, n_top))
                chip_sum0 = [_add_own_half(a, r, cidx) for a, r in zip(part, from_sib)]
                dx, dng_rest, *landed = _inproj_bwd_dx(dz, g_in[l], xs[l], p["norm_g"], dx,
                                                       _chip_rider(chip_sum0, False), tiles=(n_top, nt - n_top),
                                                       prev=dx_top)
                dng = dng_top + dng_rest
                mine[0] = [_sum_chips(r) for r in landed]
                theirs[0] = _run_rider("grad_result_exchange", _pair_rider(mine[0], False))

        raw = (wt["lam_re"][l], wt["lam_im"][l], wt["b_re"][l], wt["b_im"][l], wt["log_dt"][l])
        _, vjp = jax.vjp(_s5_prep, *raw)
        da = jnp.sum(da, axis=1)
        cot = (da[0].reshape(S5_GROUPS, S5_STATE), da[1].reshape(S5_GROUPS, S5_STATE),
               _block_diag_in_grad(dbre), _block_diag_in_grad(dbim))
        gr["lam_re"][l], gr["lam_im"][l], gr["b_re"][l], gr["b_im"][l], gr["log_dt"][l] = vjp(cot)
        gr["c_re"][l] = _block_diag_out_grad(dcre)
        gr["c_im"][l] = -_block_diag_out_grad(dcim)
        gr["d_skip"][l] = dd.reshape(S5_GROUPS, S5_CH)
        gr["w_glu"][l] = dwg
        gr["b_glu"][l] = dbg.reshape(S5_W)
        causal = jnp.tril(jnp.ones((CHUNK, CHUNK), dtype=bool))
        gr["w_s"][l] = jnp.where(causal[None], dws, 0.0)
        gr["b_s"][l] = dbsf.reshape(CHUNK, SGU_HEADS, CHUNK).sum(-1).T
        gr["ln_g"][l] = dlng.reshape(SGU_W)
        gr["ln_b"][l] = dlnb.reshape(SGU_W)
        gr["w_pool"][l] = dwp
        gr["pool_scale"][l] = dsc.reshape(POOL_W)
        gr["norm_g"][l] = dng.reshape(D_MODEL)

    grads = {k: (v if k in BIG else jnp.stack(v)) for k, v in gr.items()}
    grads["final_g"] = dfg.reshape(D_MODEL)
    if dist:
        for i, k in enumerate(BIG):
            grads[k] = ([mine[l][i] for l in range(DEPTH)], [theirs[l][i] for l in range(DEPTH)])
    return loss, dx, grads


def kernel(x, norm_g, w_in, lam_re, lam_im, b_re, b_im, c_re, c_im, d_skip, log_dt, w_glu, b_glu, ln_g, ln_b, w_s, b_s, w_pool, pool_scale, w_out, final_g, loss_target, m_norm_g, m_w_in, m_lam_re, m_lam_im, m_b_re, m_b_im, m_c_re, m_c_im, m_d_skip, m_log_dt, m_w_glu, m_b_glu, m_ln_g, m_ln_b, m_w_s, m_b_s, m_w_pool, m_pool_scale, m_w_out, m_final_g, v_norm_g, v_w_in, v_lam_re, v_lam_im, v_b_re, v_b_im, v_c_re, v_c_im, v_d_skip, v_log_dt, v_w_glu, v_b_glu, v_ln_g, v_ln_b, v_w_s, v_b_s, v_w_pool, v_pool_scale, v_w_out, v_final_g):
    wt = dict(norm_g=norm_g, w_in=w_in, lam_re=lam_re, lam_im=lam_im, b_re=b_re, b_im=b_im, c_re=c_re, c_im=c_im,
              d_skip=d_skip, log_dt=log_dt, w_glu=w_glu, b_glu=b_glu, ln_g=ln_g, ln_b=ln_b, w_s=w_s, b_s=b_s,
              w_pool=w_pool, pool_scale=pool_scale, w_out=w_out, final_g=final_g)
    mom = dict(norm_g=m_norm_g, w_in=m_w_in, lam_re=m_lam_re, lam_im=m_lam_im, b_re=m_b_re, b_im=m_b_im, c_re=m_c_re,
               c_im=m_c_im, d_skip=m_d_skip, log_dt=m_log_dt, w_glu=m_w_glu, b_glu=m_b_glu, ln_g=m_ln_g, ln_b=m_ln_b,
               w_s=m_w_s, b_s=m_b_s, w_pool=m_w_pool, pool_scale=m_pool_scale, w_out=m_w_out, final_g=m_final_g)
    vel = dict(norm_g=v_norm_g, w_in=v_w_in, lam_re=v_lam_re, lam_im=v_lam_im, b_re=v_b_re, b_im=v_b_im, c_re=v_c_re,
               c_im=v_c_im, d_skip=v_d_skip, log_dt=v_log_dt, w_glu=v_w_glu, b_glu=v_b_glu, ln_g=v_ln_g, ln_b=v_ln_b,
               w_s=v_w_s, b_s=v_b_s, w_pool=v_w_pool, pool_scale=v_pool_scale, w_out=v_w_out, final_g=v_final_g)
    T = x.shape[1]
    cidx = lax.axis_index("c").astype(jnp.int32).reshape(1)

    w_in_b = w_in.astype(BF16)
    w0 = w_in_b[0].reshape(2, HALF_D, SHARD_COLS)
    rest = (w_in_b[1].reshape(2, HALF_D, SHARD_COLS), w_glu.astype(BF16), w_out.astype(BF16))
    mx, my = lax.axis_index("x"), lax.axis_index("y")
    order = jnp.stack([2 * mx + my] + [j for j, _, _ in _other_chips(mx, my)]).astype(jnp.int32)
    loss, grad_x, grads = _local_step(x.reshape(T, D_MODEL), loss_target.reshape(T, D_MODEL), wt, w0, rest, False,
                                      cidx, order)

    packed = _pack([grads[k] for k in SMALL] + [loss[0, 0:1]])
    (sib_packed,) = _run_rider("small_pair_exchange", _pair_rider([packed], False))
    chip_packed = _add2(packed, sib_packed)
    half_rows = chip_packed.shape[0] // 2
    my_half = lax.dynamic_index_in_dim(chip_packed.reshape(2, half_rows, 128), cidx[0], 0, keepdims=False)
    small_ride = _chip_rider([my_half], True)

    out_g, out_d, out_m, out_v = {}, {}, {}, {}
    all_half = None
    for k in BIG:
        shape = wt[k].shape
        quad = lambda t: t.reshape(2, 2, shape[1] // 2, shape[2])
        g, d, m, v, *landed = _adamw_halves(quad(wt[k]), grads[k][0], grads[k][1], quad(mom[k]), quad(vel[k]), cidx,
                                            small_ride if k == BIG[0] else None)
        if landed:
            (all_half,) = landed
        out_g[k], out_d[k], out_m[k], out_v[k] = (t.reshape(shape) for t in (g, d, m, v))

    mine_half = _sum_chips(all_half)
    (their_half,) = _run_rider("small_result_exchange", _pair_rider([mine_half], False))
    total = jnp.where(cidx[0] == 0, jnp.concatenate([mine_half, their_half]), jnp.concatenate([their_half, mine_half]))
    like = [wt[k] for k in SMALL]
    small_g = _unpack(total, like + [loss[0, 0:1]])
    loss_out = small_g[-1].reshape(())
    w_p, m_p, v_p = _pack(like), _pack([mom[k] for k in SMALL]), _pack([vel[k] for k in SMALL])
    d_p, mo_p, vo_p = _adamw(w_p, total, m_p, v_p)
    for k, g, d, m, v in zip(SMALL, small_g[:-1], _unpack(d_p, like), _unpack(mo_p, like), _unpack(vo_p, like)):
        out_g[k], out_d[k], out_m[k], out_v[k] = g, d, m, v

    return (loss_out, grad_x.reshape(x.shape), *[out_g[k] for k in WEIGHTS], *[out_d[k] for k in WEIGHTS],
            *[out_m[k] for k in WEIGHTS], *[out_v[k] for k in WEIGHTS])
```

```python
import functools
import math

import jax
import jax.numpy as jnp
from jax import lax
from jax.experimental import pallas as pl
from jax.experimental.pallas import tpu as pltpu

F32 = jnp.float32
BF16 = jnp.bfloat16

D_MODEL = 2048
DEPTH = 2
S5_W = 512
SGU_W = 1024
POOL_W = 512
IN_COLS = 5120
N_CHIPS = 4
SHARD_COLS = IN_COLS // N_CHIPS
S5_GROUPS = 32
S5_STATE = 64
S5_CH = 16
STATE_W = S5_GROUPS * S5_STATE
SUPER = 4
CHUNK = 128
SGU_HEADS = 8
POOL_WINDOWS = (2, 4, 8, 16)
POOL_HALO = 16
RMS_EPS = 1e-6
LN_EPS = 1e-5
SCAN_COLS = 512

ADAM_LR = 0.001
ADAM_B1 = 0.9
ADAM_B2 = 0.999
ADAM_EPS = 1e-08
ADAM_WD = 0.01
ADAM_STEP = 10

VMEM_LIMIT = 56 * 1024 * 1024
MESH_ID = pl.DeviceIdType.MESH

_GELU_K0 = math.sqrt(2.0 / math.pi)
_GELU_K1 = 0.044715


def _cparams(n_axes):
    return pltpu.CompilerParams(dimension_semantics=("arbitrary",) * n_axes, vmem_limit_bytes=VMEM_LIMIT)


def _gelu(x):
    t = jnp.tanh(_GELU_K0 * (x + _GELU_K1 * (x * x * x)))
    return 0.5 * x * (1.0 + t)


def _gelu_and_grad(x):
    x2 = x * x
    t = jnp.tanh(_GELU_K0 * (x + _GELU_K1 * (x * x2)))
    g = 0.5 * x * (1.0 + t)
    dg = 0.5 * (1.0 + t) + 0.5 * x * (1.0 - t * t) * (_GELU_K0 * (1.0 + 3.0 * _GELU_K1 * x2))
    return g, dg


def _silu_and_grad(x):
    s = jax.nn.sigmoid(x)
    return x * s, s * (1.0 + x * (1.0 - s))


def _dot(a, b):
    return jnp.dot(a.astype(BF16), b.astype(BF16), preferred_element_type=F32)


def _dot_nt(a, b):
    return lax.dot_general(a.astype(BF16), b.astype(BF16), (((1,), (1,)), ((), ())), preferred_element_type=F32)


def _dot_tn(a, b):
    return lax.dot_general(a.astype(BF16), b.astype(BF16), (((0,), (0,)), ((), ())), preferred_element_type=F32)


def _full(shape):
    nd = len(shape)
    return pl.BlockSpec(shape, lambda *_: (0,) * nd)


class _Rider:
    def __init__(self, arrs, out_shapes, sems, steps):
        self.arrs, self.out_shapes, self.sems, self.steps = list(arrs), list(out_shapes), list(sems), steps


def _ride(body, n_in, n_out, rider, first, last, middle=None):
    if rider is None:
        return body
    ri, ro, ns = len(rider.arrs), len(rider.out_shapes), len(rider.sems)

    def wrapped(*refs):
        o0 = n_in + ri
        steps = rider.steps(refs[n_in:o0], refs[o0 + n_out:o0 + n_out + ro], *refs[len(refs) - ns:])
        pl.when(first())(steps[0])
        if len(steps) == 3:
            pl.when(middle())(steps[1])
        body(*refs[:n_in], *refs[o0:o0 + n_out], *refs[o0 + n_out + ro:len(refs) - ns])
        pl.when(last())(steps[-1])

    return wrapped


class _ColumnWriter:
    def __init__(self, stage_ref, sem_ref, dst_ref, col0, step, n_steps):
        self.stage, self.sem, self.dst, self.col0, self.step, self.n = stage_ref, sem_ref, dst_ref, col0, step, n_steps
        self.tm, self.w = stage_ref.shape[1], stage_ref.shape[2]

    def _copy(self, slot, row0):
        return pltpu.make_async_copy(self.stage.at[slot],
                                     self.dst.at[pl.ds(row0, self.tm), pl.ds(self.col0, self.w)], self.sem.at[slot])

    def slot(self):
        s = self.step % 2

        @pl.when(self.step >= 2)
        def _():
            self._copy(s, 0).wait()

        return self.stage.at[s]

    def send(self, row0):
        s = self.step % 2
        self._copy(s, row0).start()

        @pl.when(self.step == self.n - 1)
        def _():
            self._copy(s, 0).wait()
            if self.n >= 2:
                self._copy(1 - s, 0).wait()


def _stage_scratch(tm, widths):
    return ([pltpu.VMEM((2, tm, w), BF16) for w in widths], [pltpu.SemaphoreType.DMA((2,)) for _ in widths])


def _rider_specs(rider):
    if rider is None:
        return [], [], [], [], []
    anyspec = pl.BlockSpec(memory_space=pl.ANY)
    return ([anyspec] * len(rider.arrs), [anyspec] * len(rider.out_shapes), rider.out_shapes, rider.sems, rider.arrs)


HALF_D = D_MODEL // 2


def _inproj(x, g, w):
    T = x.shape[0]
    tm = min(512, T)

    def body(x_ref, g_ref, w_ref, z_ref, h_ref, hs_ref):
        @pl.when(pl.program_id(1) == 0)
        def _():
            xv = x_ref[...]
            r = lax.rsqrt(jnp.mean(xv * xv, axis=-1, keepdims=True) + RMS_EPS)
            hv = (xv * r * g_ref[...]).astype(BF16)
            hs_ref[...] = hv
            h_ref[...] = hv

        z_ref[...] = (jnp.dot(hs_ref[:, 0:HALF_D], w_ref[0], preferred_element_type=F32)
                      + jnp.dot(hs_ref[:, HALF_D:D_MODEL], w_ref[1], preferred_element_type=F32))

    return pl.pallas_call(
        body,
        name="inproj",
        grid=(T // tm, N_CHIPS),
        in_specs=[
            pl.BlockSpec((tm, D_MODEL), lambda i, j: (i, 0)),
            pl.BlockSpec((1, D_MODEL), lambda i, j: (0, 0)),
            pl.BlockSpec((2, None, HALF_D, SHARD_COLS), lambda i, j: (0, j, 0, 0)),
        ],
        out_specs=[
            pl.BlockSpec((tm, SHARD_COLS), lambda i, j: (i, j)),
            pl.BlockSpec((tm, D_MODEL), lambda i, j: (i, 0)),
        ],
        out_shape=[jax.ShapeDtypeStruct((T, IN_COLS), F32), jax.ShapeDtypeStruct((T, D_MODEL), BF16)],
        scratch_shapes=[pltpu.VMEM((tm, D_MODEL), BF16)],
        compiler_params=_cparams(2),
    )(x, g, w)


def _rms_h(x, g):
    T = x.shape[0]
    tm = min(512, T)

    def body(x_ref, g_ref, h_ref):
        xv = x_ref[...]
        r = lax.rsqrt(jnp.mean(xv * xv, axis=-1, keepdims=True) + RMS_EPS)
        h_ref[...] = (xv * r * g_ref[...]).astype(BF16)

    return pl.pallas_call(
        body, name="rms_h", grid=(T // tm,),
        in_specs=[pl.BlockSpec((tm, D_MODEL), lambda i: (i, 0)), pl.BlockSpec((1, D_MODEL), lambda i: (0, 0))],
        out_specs=pl.BlockSpec((tm, D_MODEL), lambda i: (i, 0)),
        out_shape=jax.ShapeDtypeStruct((T, D_MODEL), BF16), compiler_params=_cparams(1),
    )(x, g)


def _inproj_first(h, w0, order, riders):
    T = h.shape[0]
    tm = min(512, T)
    ni = T // tm
    n = len(riders)

    def body(order_ref, h_ref, w0_ref, *refs):
        rin = refs[:n]
        z_ref, gin_ref = refs[n:n + 2]
        rout = refs[n + 2:2 * n + 2]
        wbuf, csem, lsem, ssem, rsem = refs[2 * n + 2:2 * n + 7]
        s, i = pl.program_id(0), pl.program_id(1)
        x, y, c = _mesh_pos()
        me = 2 * x + y
        sib = (x, y, 1 - c)
        chips = _other_chips(x, y)
        if n:
            r_start, r_mid, r_end = _gather_steps(rin, rout, *refs[2 * n + 7:])

        def ici(d):
            return pltpu.make_async_remote_copy(w0_ref.at[c], gin_ref.at[c, me], ssem.at[d], rsem.at[d],
                                                device_id=(chips[d][1], chips[d][2], c), device_id_type=MESH_ID)

        def landed(d):
            return pltpu.make_async_remote_copy(w0_ref.at[c], gin_ref.at[c, chips[d][0]], ssem.at[d], rsem.at[d],
                                                device_id=sib, device_id_type=MESH_ID)

        def fwd(d, half):
            blk = gin_ref.at[half, chips[d][0]]
            return pltpu.make_async_remote_copy(blk, blk, ssem.at[3 + d], rsem.at[3 + d], device_id=sib,
                                                device_id_type=MESH_ID)

        def local(hf):
            return pltpu.make_async_copy(w0_ref.at[hf], gin_ref.at[hf, me], lsem.at[hf])

        def load(src):
            cp = pltpu.make_async_copy(src, wbuf, csem.at[0])
            cp.start()
            cp.wait()

        @pl.when((s == 0) & (i == 0))
        def _():
            for d in range(3):
                ici(d).start()
            local(0).start()
            local(1).start()
            load(w0_ref)

        for d in range(3):
            @pl.when((s == d + 1) & (i == 0))
            def _(d=d):
                landed(d).wait_recv()
                fwd(d, c).start()
                fwd(d, 1 - c).wait_recv()
                load(gin_ref.at[:, chips[d][0]])
                if d == 1 and n:
                    r_start()

        z_ref[...] = (jnp.dot(h_ref[:, 0:HALF_D], wbuf[0], preferred_element_type=F32)
                      + jnp.dot(h_ref[:, HALF_D:D_MODEL], wbuf[1], preferred_element_type=F32))

        @pl.when((s == N_CHIPS - 1) & (i == ni - 1))
        def _():
            for d in range(3):
                ici(d).wait_send()
                fwd(d, c).wait_send()
            local(0).wait()
            local(1).wait()
            if n:
                r_mid()
                r_end()

    anyspec = pl.BlockSpec(memory_space=pl.ANY)
    return pl.pallas_call(
        body,
        name="inproj_first",
        grid_spec=pltpu.PrefetchScalarGridSpec(
            num_scalar_prefetch=1,
            grid=(N_CHIPS, ni),
            in_specs=[pl.BlockSpec((tm, D_MODEL), lambda s, i, o: (i, 0)), anyspec] + [anyspec] * n,
            out_specs=[pl.BlockSpec((tm, SHARD_COLS), lambda s, i, o: (i, o[s])), anyspec] + [anyspec] * n,
            scratch_shapes=[pltpu.VMEM((2, HALF_D, SHARD_COLS), BF16), pltpu.SemaphoreType.DMA((1,)),
                            pltpu.SemaphoreType.DMA((2,)), pltpu.SemaphoreType.DMA((6,)),
                            pltpu.SemaphoreType.DMA((6,))] + (_gather_sems(n) if n else []),
        ),
        out_shape=[jax.ShapeDtypeStruct((T, IN_COLS), F32),
                   jax.ShapeDtypeStruct((2, N_CHIPS, HALF_D, SHARD_COLS), BF16)] + _gathered_shapes(riders),
        compiler_params=_cparams(2),
    )(order, h, w0, *riders)


def _outproj(ya, yb, yc, w, x):
    T = x.shape[0]
    tm = min(512, T)
    tn = 1024

    def body(ya_ref, yb_ref, yc_ref, w_ref, x_ref, o_ref, y_ref):
        acc = jnp.dot(ya_ref[...], w_ref[0:S5_W, :], preferred_element_type=F32)
        acc += jnp.dot(yb_ref[...], w_ref[S5_W:S5_W + SGU_W, :], preferred_element_type=F32)
        acc += jnp.dot(yc_ref[...], w_ref[S5_W + SGU_W:D_MODEL, :], preferred_element_type=F32)
        o_ref[...] = x_ref[...] + acc

        @pl.when(pl.program_id(1) == 0)
        def _():
            y_ref[:, 0:S5_W] = ya_ref[...]
            y_ref[:, S5_W:S5_W + SGU_W] = yb_ref[...]
            y_ref[:, S5_W + SGU_W:D_MODEL] = yc_ref[...]

    return pl.pallas_call(
        body,
        name="outproj",
        grid=(T // tm, D_MODEL // tn),
        in_specs=[
            pl.BlockSpec((tm, S5_W), lambda i, j: (i, 0)),
            pl.BlockSpec((tm, SGU_W), lambda i, j: (i, 0)),
            pl.BlockSpec((tm, POOL_W), lambda i, j: (i, 0)),
            pl.BlockSpec((D_MODEL, tn), lambda i, j: (0, j)),
            pl.BlockSpec((tm, tn), lambda i, j: (i, j)),
        ],
        out_specs=[
            pl.BlockSpec((tm, tn), lambda i, j: (i, j)),
            pl.BlockSpec((tm, D_MODEL), lambda i, j: (i, 0)),
        ],
        out_shape=[jax.ShapeDtypeStruct((T, D_MODEL), F32), jax.ShapeDtypeStruct((T, D_MODEL), BF16)],
        compiler_params=_cparams(2),
    )(ya, yb, yc, w, x)


def _outproj_bwd_dy(dxo, w):
    T = dxo.shape[0]
    tm = min(512, T)
    tn = 1024

    def body(d_ref, w_ref, o_ref, ds_ref):
        @pl.when(pl.program_id(1) == 0)
        def _():
            ds_ref[...] = d_ref[...].astype(BF16)

        o_ref[...] = lax.dot_general(ds_ref[...], w_ref[...], (((1,), (1,)), ((), ())), preferred_element_type=F32)

    return pl.pallas_call(
        body,
        name="outproj_bwd_dy",
        grid=(T // tm, D_MODEL // tn),
        in_specs=[
            pl.BlockSpec((tm, D_MODEL), lambda i, j: (i, 0)),
            pl.BlockSpec((tn, D_MODEL), lambda i, j: (j, 0)),
        ],
        out_specs=pl.BlockSpec((tm, tn), lambda i, j: (i, j)),
        out_shape=jax.ShapeDtypeStruct((T, D_MODEL), F32),
        scratch_shapes=[pltpu.VMEM((tm, D_MODEL), BF16)],
        compiler_params=_cparams(2),
    )(dxo, w)


def _outproj_bwd_dw(y, dxo):
    T = y.shape[0]
    tm = min(512, T)
    tr = 1024

    def body(y_ref, d_ref, o_ref):
        @pl.when(pl.program_id(1) == 0)
        def _():
            o_ref[...] = jnp.zeros_like(o_ref)

        o_ref[...] += _dot_tn(y_ref[...], d_ref[...])

    return pl.pallas_call(
        body,
        name="outproj_bwd_dw",
        grid=(D_MODEL // tr, T // tm),
        in_specs=[
            pl.BlockSpec((tm, tr), lambda p, t: (t, p)),
            pl.BlockSpec((tm, D_MODEL), lambda p, t: (t, 0)),
        ],
        out_specs=pl.BlockSpec((tr, D_MODEL), lambda p, t: (p, 0)),
        out_shape=jax.ShapeDtypeStruct((D_MODEL, D_MODEL), F32),
        compiler_params=_cparams(2),
    )(y, dxo)


def _inproj_bwd_dw(h, dz):
    T = h.shape[0]
    tm = min(512, T)

    def body(h_ref, dz_ref, o_ref):
        @pl.when(pl.program_id(1) == 0)
        def _():
            o_ref[...] = jnp.zeros_like(o_ref)

        o_ref[...] += _dot_tn(h_ref[...], dz_ref[...])

    return pl.pallas_call(
        body,
        name="inproj_bwd_dw",
        grid=(N_CHIPS, T // tm),
        in_specs=[
            pl.BlockSpec((tm, D_MODEL), lambda j, t: (t, 0)),
            pl.BlockSpec((tm, SHARD_COLS), lambda j, t: (t, j)),
        ],
        out_specs=pl.BlockSpec((None, D_MODEL, SHARD_COLS), lambda j, t: (j, 0, 0)),
        out_shape=jax.ShapeDtypeStruct((N_CHIPS, D_MODEL, SHARD_COLS), F32),
        compiler_params=_cparams(2),
    )(h, dz)


def _dx_tile(T):
    return min(512, max(T // 4, 8))


def _inproj_bwd_dx(dz, w4, x, g, dxo, rider=None, tiles=None, prev=None):
    T = x.shape[0]
    tm = _dx_tile(T)
    t0, ni = tiles if tiles else (0, T // tm)
    nk = N_CHIPS
    nt = (((1,), (1,)), ((), ()))
    n_in = 5 if prev is None else 6

    def body(dz_ref, w_ref, x_ref, g_ref, dxo_ref, *rest):
        dx_ref, dg_ref, acc_ref = rest[-3:]
        i, j = pl.program_id(0), pl.program_id(1)
        lo = lax.dot_general(dz_ref[...], w_ref[0], nt, preferred_element_type=F32)
        hi = lax.dot_general(dz_ref[...], w_ref[1], nt, preferred_element_type=F32)

        @pl.when(j == 0)
        def _():
            acc_ref[:, 0:HALF_D] = lo
            acc_ref[:, HALF_D:D_MODEL] = hi

        @pl.when(j > 0)
        def _():
            acc_ref[:, 0:HALF_D] += lo
            acc_ref[:, HALF_D:D_MODEL] += hi

        @pl.when(j == nk - 1)
        def _():
            @pl.when(i == 0)
            def _():
                dg_ref[...] = jnp.zeros_like(dg_ref)

            rc = min(128, tm)
            for c in range(tm // rc):
                rows = slice(c * rc, (c + 1) * rc)
                dh = acc_ref[rows, :]
                xv = x_ref[rows, :]
                r = lax.rsqrt(jnp.mean(xv * xv, axis=-1, keepdims=True) + RMS_EPS)
                xh = xv * r
                w = dh * g_ref[...]
                dx_ref[rows, :] = dxo_ref[rows, :] + r * (w - xh * jnp.mean(w * xh, axis=-1, keepdims=True))
                dg_ref[...] += jnp.sum(dh * xh, axis=0, keepdims=True)

    r_in, r_out, r_shapes, r_sems, r_args = _rider_specs(rider)
    return pl.pallas_call(
        _ride(body, n_in, 2, rider, lambda: (pl.program_id(0) == 0) & (pl.program_id(1) == 0),
              lambda: (pl.program_id(0) == ni - 1) & (pl.program_id(1) == nk - 1)),
        name="inproj_bwd_dx" + ("" if rider is None else "_ride") + ("" if prev is None else "_rest"),
        grid=(ni, nk),
        in_specs=[
            pl.BlockSpec((tm, SHARD_COLS), lambda i, j: (i + t0, j)),
            pl.BlockSpec((2, None, HALF_D, SHARD_COLS), lambda i, j: (0, j, 0, 0)),
            pl.BlockSpec((tm, D_MODEL), lambda i, j: (i + t0, 0)),
            pl.BlockSpec((1, D_MODEL), lambda i, j: (0, 0)),
            pl.BlockSpec((tm, D_MODEL), lambda i, j: (i + t0, 0)),
        ] + ([] if prev is None else [pl.BlockSpec(memory_space=pl.ANY)]) + r_in,
        out_specs=[
            pl.BlockSpec((tm, D_MODEL), lambda i, j: (i + t0, 0)),
            pl.BlockSpec((1, D_MODEL), lambda i, j: (0, 0)),
        ] + r_out,
        out_shape=[jax.ShapeDtypeStruct((T, D_MODEL), F32), jax.ShapeDtypeStruct((1, D_MODEL), F32)] + r_shapes,
        scratch_shapes=[pltpu.VMEM((tm, D_MODEL), F32)] + r_sems,
        input_output_aliases={} if prev is None else {5: 0},
        compiler_params=_cparams(2),
    )(dz, w4, x, g, dxo, *([] if prev is None else [prev]), *r_args)


def _outproj_loss(ya, yb, yc, w, x, g, tgt):
    T = x.shape[0]
    tm = min(256, T)

    def body(ya_ref, yb_ref, yc_ref, w_ref, x_ref, g_ref, t_ref, dx_ref, l_ref, dg_ref, y_ref):
        i = pl.program_id(0)
        acc = jnp.dot(ya_ref[...], w_ref[0:S5_W, :], preferred_element_type=F32)
        acc += jnp.dot(yb_ref[...], w_ref[S5_W:S5_W + SGU_W, :], preferred_element_type=F32)
        acc += jnp.dot(yc_ref[...], w_ref[S5_W + SGU_W:D_MODEL, :], preferred_element_type=F32)
        y_ref[:, 0:S5_W] = ya_ref[...]
        y_ref[:, S5_W:S5_W + SGU_W] = yb_ref[...]
        y_ref[:, S5_W + SGU_W:D_MODEL] = yc_ref[...]
        xv = x_ref[...] + acc
        r = lax.rsqrt(jnp.mean(xv * xv, axis=-1, keepdims=True) + RMS_EPS)
        xh = xv * r
        err = xh * g_ref[...] - t_ref[...]
        lpart = 0.5 * jnp.sum(jnp.mean(err * err, axis=-1, keepdims=True), axis=0, keepdims=True)
        dout = err * (1.0 / D_MODEL)
        w = dout * g_ref[...]
        dx_ref[...] = r * (w - xh * jnp.mean(w * xh, axis=-1, keepdims=True))
        gpart = jnp.sum(dout * xh, axis=0, keepdims=True)

        @pl.when(i == 0)
        def _():
            l_ref[...] = jnp.broadcast_to(lpart, l_ref.shape)
            dg_ref[...] = gpart

        @pl.when(i > 0)
        def _():
            l_ref[...] += jnp.broadcast_to(lpart, l_ref.shape)
            dg_ref[...] += gpart

    row = lambda w: pl.BlockSpec((tm, w), lambda i: (i, 0))
    return pl.pallas_call(
        body,
        name="outproj_loss",
        grid=(T // tm,),
        in_specs=[row(S5_W), row(SGU_W), row(POOL_W), _full((D_MODEL, D_MODEL)), row(D_MODEL), _full((1, D_MODEL)),
                  row(D_MODEL)],
        out_specs=[row(D_MODEL), _full((1, 128)), _full((1, D_MODEL)), row(D_MODEL)],
        out_shape=[
            jax.ShapeDtypeStruct((T, D_MODEL), F32),
            jax.ShapeDtypeStruct((1, 128), F32),
            jax.ShapeDtypeStruct((1, D_MODEL), F32),
            jax.ShapeDtypeStruct((T, D_MODEL), BF16),
        ],
        compiler_params=_cparams(1),
    )(ya, yb, yc, w, x, g, tgt)


def _s5_prep(lam_re, lam_im, b_re, b_im, log_dt):
    lam = lax.complex(lam_re, lam_im)
    dt = jnp.exp(log_dt)[:, None]
    a = jnp.exp(lam * dt)
    bbar = ((a - 1.0) / lam)[..., None] * lax.complex(b_re, b_im)
    return jnp.real(a), jnp.imag(a), jnp.real(bbar), jnp.imag(bbar)


def _block_diag_in(m):
    m4 = m.reshape(SUPER, 8, S5_STATE, S5_CH)
    eye = jnp.eye(8, dtype=m.dtype)
    out = jnp.einsum("jgph,gk->jghkp", m4, eye)
    return out.reshape(SUPER, 8 * S5_CH, 8 * S5_STATE)


def _block_diag_in_grad(d):
    d6 = d.reshape(SUPER, 8, S5_CH, 8, S5_STATE)
    diag = jnp.einsum("jghgp->jgph", d6)
    return diag.reshape(S5_GROUPS, S5_STATE, S5_CH)


def _block_diag_out(m):
    m4 = m.reshape(SUPER, 8, S5_CH, S5_STATE)
    eye = jnp.eye(8, dtype=m.dtype)
    out = jnp.einsum("jghp,gk->jgpkh", m4, eye)
    return out.reshape(SUPER, 8 * S5_STATE, 8 * S5_CH)


def _block_diag_out_grad(d):
    d6 = d.reshape(SUPER, 8, S5_STATE, 8, S5_CH)
    diag = jnp.einsum("jgpgh->jghp", d6)
    return diag.reshape(S5_GROUPS, S5_CH, S5_STATE)


def _scan_coefs(a_re, a_im, reverse):
    a = lax.complex(a_re.reshape(-1), a_im.reshape(-1))
    if reverse:
        a = jnp.conj(a)
    pw = [a]
    for _ in range(7):
        pw.append(pw[-1] * a)
    rows = jnp.arange(8)

    def masked(k):
        m = (rows + k <= 7) if reverse else (rows >= k)
        return jnp.where(m[:, None], pw[k - 1][None, :], 0.0)

    a1, a2, a4 = masked(1), masked(2), masked(4)
    carry = jnp.stack([pw[7 - r] for r in range(8)]) if reverse else jnp.stack(pw)
    parts = []
    for c in (a1, a2, a4, carry):
        parts += [jnp.real(c), jnp.imag(c)]
    return jnp.stack(parts).astype(F32)


def _scan_block(r, im, coef_ref, cs, reverse):
    for k, idx in ((1, 0), (2, 2), (4, 4)):
        ar = coef_ref[idx, :, cs]
        ai = coef_ref[idx + 1, :, cs]
        sh = 8 - k if reverse else k
        rr = pltpu.roll(r, sh, 0)
        ri = pltpu.roll(im, sh, 0)
        r, im = r + ar * rr - ai * ri, im + ar * ri + ai * rr
    return r, im


def _s5_fwd(z, p, rider=None):
    T = z.shape[0]
    tm = min(256, T)
    nblk = tm // 8
    W = STATE_W

    def body(xa_ref, ga_ref, bre_ref, bim_ref, cre_ref, cim_ref, dv_ref, wg_ref, bg_ref, coef_ref,
             ya_ref, yraw_ref, sre_ref, sim_ref, wre, wim):
        @pl.when(pl.program_id(0) == 0)
        def _():
            wre[0:8, :] = jnp.zeros((8, W), F32)
            wim[0:8, :] = jnp.zeros((8, W), F32)

        xa = xa_ref[...]
        xab = xa.astype(BF16)
        for j in range(SUPER):
            xj = xab[:, j * 128:(j + 1) * 128]
            wre[8:8 + tm, j * 512:(j + 1) * 512] = jnp.dot(xj, bre_ref[j], preferred_element_type=F32)
            wim[8:8 + tm, j * 512:(j + 1) * 512] = jnp.dot(xj, bim_ref[j], preferred_element_type=F32)

        def blk(b, carry):
            base = pl.multiple_of(8 + b * 8, 8)
            for cc in range(W // SCAN_COLS):
                cs = pl.ds(cc * SCAN_COLS, SCAN_COLS)
                r, im = _scan_block(wre[pl.ds(base, 8), cs], wim[pl.ds(base, 8), cs], coef_ref, cs, False)
                cr = wre[pl.ds(base - 1, 1), cs]
                ci = wim[pl.ds(base - 1, 1), cs]
                pr = coef_ref[6, :, cs]
                pi = coef_ref[7, :, cs]
                wre[pl.ds(base, 8), cs] = r + pr * cr - pi * ci
                wim[pl.ds(base, 8), cs] = im + pr * ci + pi * cr
            return carry

        lax.fori_loop(0, nblk, blk, 0)
        wre[0:8, :] = wre[tm:tm + 8, :]
        wim[0:8, :] = wim[tm:tm + 8, :]
        sre_ref[...] = wre[8:8 + tm, :]
        sim_ref[...] = wim[8:8 + tm, :]

        for j in range(SUPER):
            yr = jnp.dot(wre[8:8 + tm, j * 512:(j + 1) * 512].astype(BF16), cre_ref[j], preferred_element_type=F32)
            yr += jnp.dot(wim[8:8 + tm, j * 512:(j + 1) * 512].astype(BF16), cim_ref[j], preferred_element_type=F32)
            yraw_ref[:, j * 128:(j + 1) * 128] = yr
        yraw = yraw_ref[...] + dv_ref[...] * xa
        yraw_ref[...] = yraw
        yg = _gelu(yraw)
        q = jnp.dot(yg.astype(BF16), wg_ref[...], preferred_element_type=F32) + bg_ref[...]
        sga, _ = _silu_and_grad(ga_ref[...])
        ya_ref[...] = (yg * jax.nn.sigmoid(q) * sga).astype(BF16)

    nt = T // tm
    r_in, r_out, r_shapes, r_sems, r_args = _rider_specs(rider)
    return pl.pallas_call(
        _ride(body, 10, 4, rider, lambda: pl.program_id(0) == 0, lambda: pl.program_id(0) == nt - 1,
              lambda: pl.program_id(0) == nt - 1),
        name="s5_fwd" + ("" if rider is None else "_ride"),
        grid=(nt,),
        in_specs=[
            pl.BlockSpec((tm, S5_W), lambda i: (i, 0)),
            pl.BlockSpec((tm, S5_W), lambda i: (i, 6)),
            _full((SUPER, 128, 512)), _full((SUPER, 128, 512)),
            _full((SUPER, 512, 128)), _full((SUPER, 512, 128)),
            _full((1, S5_W)), _full((S5_W, S5_W)), _full((1, S5_W)),
            _full((8, 8, W)),
        ] + r_in,
        out_specs=[
            pl.BlockSpec((tm, S5_W), lambda i: (i, 0)),
            pl.BlockSpec((tm, S5_W), lambda i: (i, 0)),
            pl.BlockSpec((tm, W), lambda i: (i, 0)),
            pl.BlockSpec((tm, W), lambda i: (i, 0)),
        ] + r_out,
        out_shape=[
            jax.ShapeDtypeStruct((T, S5_W), BF16),
            jax.ShapeDtypeStruct((T, S5_W), F32),
            jax.ShapeDtypeStruct((T, W), F32),
            jax.ShapeDtypeStruct((T, W), F32),
        ] + r_shapes,
        scratch_shapes=[pltpu.VMEM((tm + 8, W), F32), pltpu.VMEM((tm + 8, W), F32)] + r_sems,
        compiler_params=_cparams(1),
    )(z, z, p["b4re"], p["b4im"], p["c4re"], p["c4im"], p["dvec"], p["wglu"], p["bglu"], p["coef_f"], *r_args)


def _s5_bwd(dy, z, yraw, sre, sim, p, rider=None):
    T = z.shape[0]
    tm = min(256, T)
    nt = T // tm
    nblk = tm // 8
    W = STATE_W
    rev = lambda i: nt - 1 - i

    def body(dya_ref, xa_ref, ga_ref, yraw_ref, sre_ref, sim_ref, hre_ref, him_ref,
             bre_t_ref, bim_t_ref, cre_t_ref, cim_t_ref, dv_ref, wg_ref, wgt_ref, bg_ref, coef_ref,
             dz_ref, dbre_ref, dbim_ref, dcre_ref, dcim_ref, dd_ref, dwg_ref, dbg_ref, da_ref,
             wre, wim, dyr_ref, xa_stage, ga_stage, xa_sem, ga_sem):
        i = pl.program_id(0)
        xa_out = _ColumnWriter(xa_stage, xa_sem, dz_ref, 0, i, nt)
        ga_out = _ColumnWriter(ga_stage, ga_sem, dz_ref, 6 * 512, i, nt)
        dxa_ref, dga_ref = xa_out.slot(), ga_out.slot()

        @pl.when(i == 0)
        def _():
            wre[tm:tm + 8, :] = jnp.zeros((8, W), F32)
            wim[tm:tm + 8, :] = jnp.zeros((8, W), F32)
            for ref in (dbre_ref, dbim_ref, dcre_ref, dcim_ref, dd_ref, dwg_ref, dbg_ref, da_ref):
                ref[...] = jnp.zeros_like(ref)

        xa = xa_ref[...]
        dya = dya_ref[...]
        yg, dgelu = _gelu_and_grad(yraw_ref[...])
        ygb = yg.astype(BF16)
        q = jnp.dot(ygb, wg_ref[...], preferred_element_type=F32) + bg_ref[...]
        sq = jax.nn.sigmoid(q)
        sga, dsga = _silu_and_grad(ga_ref[...])
        dga_ref[...] = (dya * (yg * sq) * dsga).astype(BF16)
        dya0 = dya * sga
        dq = dya0 * yg * sq * (1.0 - sq)
        dqb = dq.astype(BF16)
        dyg = dya0 * sq + jnp.dot(dqb, wgt_ref[...], preferred_element_type=F32)
        dwg_ref[...] += _dot_tn(ygb, dqb)
        dbg_ref[...] += jnp.sum(dq, axis=0, keepdims=True)
        dyraw = dyg * dgelu
        dd_ref[...] += jnp.sum(dyraw * xa, axis=0, keepdims=True)
        dyr_ref[...] = dyraw.astype(BF16)

        for j in range(SUPER):
            dj = dyr_ref[:, j * 128:(j + 1) * 128]
            wre[0:tm, j * 512:(j + 1) * 512] = jnp.dot(dj, cre_t_ref[j], preferred_element_type=F32)
            wim[0:tm, j * 512:(j + 1) * 512] = jnp.dot(dj, cim_t_ref[j], preferred_element_type=F32)

        row0 = lax.broadcasted_iota(jnp.int32, (8, SCAN_COLS), 0) == 0
        head_on = (i < nt - 1).astype(F32)

        def one_block(base, first):
            for cc in range(W // SCAN_COLS):
                cs = pl.ds(cc * SCAN_COLS, SCAN_COLS)
                r, im = _scan_block(wre[pl.ds(base, 8), cs], wim[pl.ds(base, 8), cs], coef_ref, cs, True)
                cr = wre[pl.ds(base + 8, 1), cs]
                ci = wim[pl.ds(base + 8, 1), cs]
                pr = coef_ref[6, :, cs]
                pi = coef_ref[7, :, cs]
                r, im = r + pr * cr - pi * ci, im + pr * ci + pi * cr
                wre[pl.ds(base, 8), cs] = r
                wim[pl.ds(base, 8), cs] = im
                if first:
                    pre = hre_ref[7:8, cs] * head_on
                    pim = him_ref[7:8, cs] * head_on
                else:
                    pre = sre_ref[pl.ds(base - 1, 1), cs]
                    pim = sim_ref[pl.ds(base - 1, 1), cs]
                spr = jnp.where(row0, pre, pltpu.roll(sre_ref[pl.ds(base, 8), cs], 1, 0))
                spi = jnp.where(row0, pim, pltpu.roll(sim_ref[pl.ds(base, 8), cs], 1, 0))
                da_ref[0, :, cs] += r * spr + im * spi
                da_ref[1, :, cs] += im * spr - r * spi

        def blk(b, carry):
            one_block(pl.multiple_of((nblk - 1 - b) * 8, 8), False)
            return carry

        lax.fori_loop(0, nblk - 1, blk, 0)
        one_block(0, True)
        wre[tm:tm + 8, :] = wre[0:8, :]
        wim[tm:tm + 8, :] = wim[0:8, :]

        xab = xa.astype(BF16)
        for j in range(SUPER):
            cols = slice(j * 512, (j + 1) * 512)
            gre = wre[0:tm, cols].astype(BF16)
            gim = wim[0:tm, cols].astype(BF16)
            xj = xab[:, j * 128:(j + 1) * 128]
            dj = dyr_ref[:, j * 128:(j + 1) * 128]
            dbre_ref[j] += _dot_tn(xj, gre)
            dbim_ref[j] += _dot_tn(xj, gim)
            dcre_ref[j] += _dot_tn(sre_ref[:, cols], dj)
            dcim_ref[j] += _dot_tn(sim_ref[:, cols], dj)
            dxj = jnp.dot(gre, bre_t_ref[j], preferred_element_type=F32)
            dxj += jnp.dot(gim, bim_t_ref[j], preferred_element_type=F32)
            dxj += dyraw[:, j * 128:(j + 1) * 128] * dv_ref[:, j * 128:(j + 1) * 128]
            dxa_ref[:, j * 128:(j + 1) * 128] = dxj.astype(BF16)
        xa_out.send(rev(i) * tm)
        ga_out.send(rev(i) * tm)

    acc = lambda shape: _full(shape)
    hb = tm // 8
    r_in, r_out, r_shapes, r_sems, r_args = _rider_specs(rider)
    stages, stage_sems = _stage_scratch(tm, (S5_W, S5_W))
    return pl.pallas_call(
        _ride(body, 17, 9, rider, lambda: pl.program_id(0) == 0, lambda: pl.program_id(0) == nt - 1),
        name="s5_bwd" + ("" if rider is None else "_ride"),
        grid=(nt,),
        in_specs=[
            pl.BlockSpec((tm, S5_W), lambda i: (rev(i), 0)),
            pl.BlockSpec((tm, S5_W), lambda i: (rev(i), 0)),
            pl.BlockSpec((tm, S5_W), lambda i: (rev(i), 6)),
            pl.BlockSpec((tm, S5_W), lambda i: (rev(i), 0)),
            pl.BlockSpec((tm, W), lambda i: (rev(i), 0)),
            pl.BlockSpec((tm, W), lambda i: (rev(i), 0)),
            pl.BlockSpec((8, W), lambda i: (jnp.maximum(rev(i) * hb - 1, 0), 0)),
            pl.BlockSpec((8, W), lambda i: (jnp.maximum(rev(i) * hb - 1, 0), 0)),
            _full((SUPER, 512, 128)), _full((SUPER, 512, 128)),
            _full((SUPER, 128, 512)), _full((SUPER, 128, 512)),
            _full((1, S5_W)), _full((S5_W, S5_W)), _full((S5_W, S5_W)), _full((1, S5_W)),
            _full((8, 8, W)),
        ] + r_in,
        out_specs=[
            pl.BlockSpec(memory_space=pl.ANY),
            acc((SUPER, 128, 512)), acc((SUPER, 128, 512)),
            acc((SUPER, 512, 128)), acc((SUPER, 512, 128)),
            acc((1, S5_W)), acc((S5_W, S5_W)), acc((1, S5_W)), acc((2, 8, W)),
        ] + r_out,
        out_shape=[
            jax.ShapeDtypeStruct((T, IN_COLS), BF16),
            jax.ShapeDtypeStruct((SUPER, 128, 512), F32), jax.ShapeDtypeStruct((SUPER, 128, 512), F32),
            jax.ShapeDtypeStruct((SUPER, 512, 128), F32), jax.ShapeDtypeStruct((SUPER, 512, 128), F32),
            jax.ShapeDtypeStruct((1, S5_W), F32), jax.ShapeDtypeStruct((S5_W, S5_W), F32),
            jax.ShapeDtypeStruct((1, S5_W), F32), jax.ShapeDtypeStruct((2, 8, W), F32),
        ] + r_shapes,
        scratch_shapes=[pltpu.VMEM((tm + 8, W), F32), pltpu.VMEM((tm + 8, W), F32), pltpu.VMEM((tm, S5_W), BF16)]
        + stages + stage_sems + r_sems,
        compiler_params=_cparams(1),
    )(dy, z, z, yraw, sre, sim, sre, sim,
      p["b4re_t"], p["b4im_t"], p["c4re_t"], p["c4im_t"], p["dvec"], p["wglu"], p["wglu_t"], p["bglu"], p["coef_r"],
      *r_args)


def _ln_fwd(vf, lng, lnb):
    mu = jnp.mean(vf, axis=-1, keepdims=True)
    d = vf - mu
    rstd = lax.rsqrt(jnp.mean(d * d, axis=-1, keepdims=True) + LN_EPS)
    xh = d * rstd
    return xh, rstd, xh * lng + lnb


def _col_block(tm, b):
    return pl.BlockSpec((tm, 512), lambda i: (i, b))


def _ln_halves(vf0, vf1):
    mu = (jnp.sum(vf0, axis=-1, keepdims=True) + jnp.sum(vf1, axis=-1, keepdims=True)) * (1.0 / SGU_W)
    d0, d1 = vf0 - mu, vf1 - mu
    var = (jnp.sum(d0 * d0, axis=-1, keepdims=True) + jnp.sum(d1 * d1, axis=-1, keepdims=True)) * (1.0 / SGU_W)
    rstd = lax.rsqrt(var + LN_EPS)
    return d0 * rstd, d1 * rstd, rstd


def _sgu_fwd(z, ws, bsf, lng, lnb):
    T = z.shape[0]
    tm = min(512, T)

    def body(u0, u1, v0, v1, g0, g1, ws_ref, bs_ref, lng_ref, lnb_ref, yb_ref, vn_ref):
        for c in range(tm // CHUNK):
            rows = slice(c * CHUNK, (c + 1) * CHUNK)
            xh0, xh1, _ = _ln_halves(_gelu(v0[rows, :]), _gelu(v1[rows, :]))
            vn_ref[:, 0:512] = (xh0 * lng_ref[:, 0:512] + lnb_ref[:, 0:512]).astype(BF16)
            vn_ref[:, 512:1024] = (xh1 * lng_ref[:, 512:1024] + lnb_ref[:, 512:1024]).astype(BF16)
            for half, (u_ref, g_ref) in enumerate(((u0, g0), (u1, g1))):
                sg, _ = _silu_and_grad(g_ref[rows, :])
                m = _gelu(u_ref[rows, :]) * sg
                for hh in range(SGU_HEADS // 2):
                    h = half * (SGU_HEADS // 2) + hh
                    cols = slice(h * 128, (h + 1) * 128)
                    s = jnp.dot(ws_ref[h], vn_ref[:, cols], preferred_element_type=F32) + bs_ref[:, cols]
                    yb_ref[rows, cols] = (m[:, hh * 128:(hh + 1) * 128] * s).astype(BF16)

    return pl.pallas_call(
        body,
        name="sgu_fwd",
        grid=(T // tm,),
        in_specs=[_col_block(tm, b) for b in (1, 2, 3, 4, 7, 8)] + [
            _full((SGU_HEADS, CHUNK, CHUNK)), _full((CHUNK, SGU_W)), _full((1, SGU_W)), _full((1, SGU_W)),
        ],
        out_specs=pl.BlockSpec((tm, SGU_W), lambda i: (i, 0)),
        out_shape=jax.ShapeDtypeStruct((T, SGU_W), BF16),
        scratch_shapes=[pltpu.VMEM((CHUNK, SGU_W), BF16)],
        compiler_params=_cparams(1),
    )(z, z, z, z, z, z, ws, bsf, lng, lnb)


def _sgu_bwd(dy, z, dz, ws, ws_t, bsf, lng, lnb, rider=None):
    T = z.shape[0]
    tm = min(512, T)
    HH = SGU_HEADS // 2

    def body(u0, u1, v0, v1, g0, g1, dy0, dy1, ws_ref, wst_ref, bs_ref, lng_ref, lnb_ref, dz_in,
             dz_ref, dws_ref, dbs_ref, dlng_ref, dlnb_ref, vn_ref, dvn_ref, *stage):
        step = pl.program_id(0)
        outs = [_ColumnWriter(stage[k], stage[3 + k], dz_ref, col, step, T // tm)
                for k, col in enumerate((512, 1536, 3584))]
        du_ref, dv_ref, dgb_ref = (o.slot() for o in outs)

        @pl.when(step == 0)
        def _():
            for ref in (dws_ref, dbs_ref, dlng_ref, dlnb_ref):
                ref[...] = jnp.zeros_like(ref)

        for c in range(tm // CHUNK):
            rows = slice(c * CHUNK, (c + 1) * CHUNK)
            vf0, dgv0 = _gelu_and_grad(v0[rows, :])
            vf1, dgv1 = _gelu_and_grad(v1[rows, :])
            xh0, xh1, rstd = _ln_halves(vf0, vf1)
            vn_ref[:, 0:512] = (xh0 * lng_ref[:, 0:512] + lnb_ref[:, 0:512]).astype(BF16)
            vn_ref[:, 512:1024] = (xh1 * lng_ref[:, 512:1024] + lnb_ref[:, 512:1024]).astype(BF16)
            for half, (u_ref, g_ref, dy_ref) in enumerate(((u0, g0, dy0), (u1, g1, dy1))):
                ug, dgu = _gelu_and_grad(u_ref[rows, :])
                sg, dsg = _silu_and_grad(g_ref[rows, :])
                dyb = dy_ref[rows, :]
                dyb0 = dyb * sg
                ds_half = dyb0 * ug
                du_scale = dyb0 * dgu
                dg_scale = dyb * ug * dsg
                for hh in range(HH):
                    h = half * HH + hh
                    cols = slice(h * 128, (h + 1) * 128)
                    lc = slice(hh * 128, (hh + 1) * 128)
                    s = jnp.dot(ws_ref[h], vn_ref[:, cols], preferred_element_type=F32) + bs_ref[:, cols]
                    du_ref[rows, cols] = (du_scale[:, lc] * s).astype(BF16)
                    dgb_ref[rows, cols] = (dg_scale[:, lc] * s).astype(BF16)
                    ds = ds_half[:, lc]
                    dbs_ref[:, cols] += ds
                    dsb = ds.astype(BF16)
                    dws_ref[h] += _dot_nt(dsb, vn_ref[:, cols])
                    dvn_ref[:, cols] = jnp.dot(wst_ref[h], dsb, preferred_element_type=F32)
            dvn0 = dvn_ref[:, 0:512]
            dvn1 = dvn_ref[:, 512:1024]
            dlnb_ref[:, 0:512] += jnp.sum(dvn0, axis=0, keepdims=True)
            dlnb_ref[:, 512:1024] += jnp.sum(dvn1, axis=0, keepdims=True)
            dlng_ref[:, 0:512] += jnp.sum(dvn0 * xh0, axis=0, keepdims=True)
            dlng_ref[:, 512:1024] += jnp.sum(dvn1 * xh1, axis=0, keepdims=True)
            dxh0 = dvn0 * lng_ref[:, 0:512]
            dxh1 = dvn1 * lng_ref[:, 512:1024]
            m1 = (jnp.sum(dxh0, axis=-1, keepdims=True) + jnp.sum(dxh1, axis=-1, keepdims=True)) * (1.0 / SGU_W)
            m2 = (jnp.sum(dxh0 * xh0, axis=-1, keepdims=True) + jnp.sum(dxh1 * xh1, axis=-1, keepdims=True)) * (1.0 / SGU_W)
            dv_ref[rows, 0:512] = (rstd * (dxh0 - m1 - xh0 * m2) * dgv0).astype(BF16)
            dv_ref[rows, 512:1024] = (rstd * (dxh1 - m1 - xh1 * m2) * dgv1).astype(BF16)
        for o in outs:
            o.send(step * tm)

    anyspec = pl.BlockSpec(memory_space=pl.ANY)
    r_in, r_out, r_shapes, r_sems, r_args = _rider_specs(rider)
    stages, stage_sems = _stage_scratch(tm, (SGU_W, SGU_W, SGU_W))
    return pl.pallas_call(
        _ride(body, 14, 5, rider, lambda: pl.program_id(0) == 0, lambda: pl.program_id(0) == T // tm - 1),
        name="sgu_bwd" + ("" if rider is None else "_ride"),
        grid=(T // tm,),
        in_specs=[_col_block(tm, b) for b in (1, 2, 3, 4, 7, 8)] + [_col_block(tm, 1), _col_block(tm, 2)] + [
            _full((SGU_HEADS, CHUNK, CHUNK)), _full((SGU_HEADS, CHUNK, CHUNK)),
            _full((CHUNK, SGU_W)), _full((1, SGU_W)), _full((1, SGU_W)), anyspec,
        ] + r_in,
        out_specs=[anyspec,
                   _full((SGU_HEADS, CHUNK, CHUNK)), _full((CHUNK, SGU_W)), _full((1, SGU_W)), _full((1, SGU_W))] + r_out,
        input_output_aliases={13: 0},
        out_shape=[
            jax.ShapeDtypeStruct((T, IN_COLS), BF16),
            jax.ShapeDtypeStruct((SGU_HEADS, CHUNK, CHUNK), F32), jax.ShapeDtypeStruct((CHUNK, SGU_W), F32),
            jax.ShapeDtypeStruct((1, SGU_W), F32), jax.ShapeDtypeStruct((1, SGU_W), F32),
        ] + r_shapes,
        scratch_shapes=[pltpu.VMEM((CHUNK, SGU_W), BF16), pltpu.VMEM((CHUNK, SGU_W), F32)] + stages + stage_sems + r_sems,
        compiler_params=_cparams(1),
    )(z, z, z, z, z, z, dy, dy, ws, ws_t, bsf, lng, lnb, dz, *r_args)


def _pool_den(first_row, n):
    return (lax.broadcasted_iota(jnp.int32, (n, 1), 0) + first_row + 1).astype(F32)


def _pool_p(ext, xc, pos, tm):
    w2 = ext + pltpu.roll(ext, 1, 0)
    w4 = w2 + pltpu.roll(w2, 2, 0)
    w8 = w4 + pltpu.roll(w4, 4, 0)
    w16 = w8 + pltpu.roll(w8, 8, 0)
    out = []
    for g, (w, ws) in enumerate(zip(POOL_WINDOWS, (w2, w4, w8, w16))):
        cols = slice(g * 128, (g + 1) * 128)
        mean = ws[POOL_HALO:POOL_HALO + tm, cols] / jnp.minimum(pos, float(w))
        out.append(mean - xc[:, cols])
    return out


def _pool_fwd(z, wp, scale):
    T = z.shape[0]
    tm = min(512, T)
    hb = tm // POOL_HALO

    def body(xc_ref, hx_ref, gc_ref, wp_ref, sc_ref, yc_ref):
        i = pl.program_id(0)
        xc = xc_ref[...]
        halo = hx_ref[...] * (i > 0).astype(F32)
        ext = jnp.concatenate([halo, xc], axis=0)
        ps = _pool_p(ext, xc, _pool_den(i * tm, tm), tm)
        sg, _ = _silu_and_grad(gc_ref[...])
        for g in range(4):
            cols = slice(g * 128, (g + 1) * 128)
            pw = _dot(ps[g], wp_ref[g])
            yc_ref[:, cols] = (pw * sc_ref[:, cols] * sg[:, cols]).astype(BF16)

    return pl.pallas_call(
        body,
        name="pool_fwd",
        grid=(T // tm,),
        in_specs=[
            _col_block(tm, 5),
            pl.BlockSpec((POOL_HALO, 512), lambda i: (jnp.maximum(i * hb - 1, 0), 5)),
            _col_block(tm, 9),
            _full((4, 128, 128)), _full((1, POOL_W)),
        ],
        out_specs=pl.BlockSpec((tm, POOL_W), lambda i: (i, 0)),
        out_shape=jax.ShapeDtypeStruct((T, POOL_W), BF16),
        compiler_params=_cparams(1),
    )(z, z, z, wp, scale)


def _pool_bwd(dy, z, dz, wp, wp_t, scale):
    T = z.shape[0]
    tm = min(512, T)
    nt = T // tm
    hb = tm // POOL_HALO
    last_hb = T // POOL_HALO - 1
    L = tm + POOL_HALO

    def body(xc_ref, hx_ref, gc_ref, gn_ref, dyc_ref, dyn_ref, wp_ref, wpt_ref, sc_ref, dz_in,
             dz_ref, dwp_ref, dsc_ref, xc_stage, gc_stage, xc_sem, gc_sem):
        i = pl.program_id(0)
        xc_out = _ColumnWriter(xc_stage, xc_sem, dz_ref, 5 * 512, i, nt)
        gc_out = _ColumnWriter(gc_stage, gc_sem, dz_ref, 9 * 512, i, nt)
        dxc_ref, dgc_ref = xc_out.slot(), gc_out.slot()

        @pl.when(i == 0)
        def _():
            dwp_ref[...] = jnp.zeros_like(dwp_ref)
            dsc_ref[...] = jnp.zeros_like(dsc_ref)

        xc = xc_ref[...]
        halo = hx_ref[...] * (i > 0).astype(F32)
        pos = _pool_den(i * tm, tm)
        ps = _pool_p(jnp.concatenate([halo, xc], axis=0), xc, pos, tm)
        sg, dsg = _silu_and_grad(gc_ref[...])
        dyc = dyc_ref[...]
        dyc0 = dyc * sg
        dpw = dyc0 * sc_ref[...]
        sgn, _ = _silu_and_grad(gn_ref[...])
        dpwn = dyn_ref[...] * sgn * sc_ref[...] * (i < nt - 1).astype(F32)
        posn = _pool_den((i + 1) * tm, POOL_HALO)
        dps, qs = [], []
        for g, w in enumerate(POOL_WINDOWS):
            cols = slice(g * 128, (g + 1) * 128)
            pw = _dot(ps[g], wp_ref[g])
            dgc_ref[:, cols] = (dyc[:, cols] * pw * sc_ref[:, cols] * dsg[:, cols]).astype(BF16)
            dsc_ref[:, cols] += jnp.sum(dyc0[:, cols] * pw, axis=0, keepdims=True)
            dwp_ref[g] += _dot_tn(ps[g], dpw[:, cols])
            dp = _dot(dpw[:, cols], wpt_ref[g])
            dpn = _dot(dpwn[:, cols], wpt_ref[g])
            dps.append(dp)
            qs.append(jnp.concatenate([dp / jnp.minimum(pos, float(w)), dpn / jnp.minimum(posn, float(w))], axis=0))
        ext = jnp.concatenate(qs, axis=1)
        f2 = ext + pltpu.roll(ext, L - 1, 0)
        f4 = f2 + pltpu.roll(f2, L - 2, 0)
        f8 = f4 + pltpu.roll(f4, L - 4, 0)
        f16 = f8 + pltpu.roll(f8, L - 8, 0)
        for g, f in enumerate((f2, f4, f8, f16)):
            cols = slice(g * 128, (g + 1) * 128)
            dxc_ref[:, cols] = (f[0:tm, cols] - dps[g]).astype(BF16)
        xc_out.send(i * tm)
        gc_out.send(i * tm)

    nxt = lambda i: jnp.minimum((i + 1) * hb, last_hb)
    anyspec = pl.BlockSpec(memory_space=pl.ANY)
    stages, stage_sems = _stage_scratch(tm, (POOL_W, POOL_W))
    return pl.pallas_call(
        body,
        name="pool_bwd",
        grid=(nt,),
        in_specs=[
            _col_block(tm, 5),
            pl.BlockSpec((POOL_HALO, 512), lambda i: (jnp.maximum(i * hb - 1, 0), 5)),
            _col_block(tm, 9),
            pl.BlockSpec((POOL_HALO, 512), lambda i: (nxt(i), 9)),
            _col_block(tm, 3),
            pl.BlockSpec((POOL_HALO, 512), lambda i: (nxt(i), 3)),
            _full((4, 128, 128)), _full((4, 128, 128)), _full((1, POOL_W)), anyspec,
        ],
        out_specs=[anyspec, _full((4, 128, 128)), _full((1, POOL_W))],
        input_output_aliases={9: 0},
        out_shape=[
            jax.ShapeDtypeStruct((T, IN_COLS), BF16),
            jax.ShapeDtypeStruct((4, 128, 128), F32), jax.ShapeDtypeStruct((1, POOL_W), F32),
        ],
        scratch_shapes=stages + stage_sems,
        compiler_params=_cparams(1),
    )(z, z, z, z, dy, dy, wp, wp_t, scale, dz)


def _row_tile(rows, cols):
    tr = 8
    while tr * 2 * cols * 4 <= 2 * 1024 * 1024 and rows % (tr * 2) == 0:
        tr *= 2
    return tr


def _add_own_half(part, recv, cidx):
    _, _, R2, C = part.shape
    tr = _row_tile(R2, C)

    def body(c_ref, a_ref, r_ref, o_ref):
        o_ref[...] = (a_ref[...] + r_ref[...]).astype(BF16)

    return pl.pallas_call(
        body,
        name="add_own_half",
        grid_spec=pltpu.PrefetchScalarGridSpec(
            num_scalar_prefetch=1,
            grid=(N_CHIPS, R2 // tr),
            in_specs=[
                pl.BlockSpec((None, None, tr, C), lambda j, i, c: (j, c[0], i, 0)),
                pl.BlockSpec((None, tr, C), lambda j, i, c: (j, i, 0)),
            ],
            out_specs=pl.BlockSpec((None, tr, C), lambda j, i, c: (j, i, 0)),
        ),
        out_shape=jax.ShapeDtypeStruct((N_CHIPS, R2, C), BF16),
        compiler_params=_cparams(2),
    )(cidx, part, recv)


def _add2(a, b):
    R, C = a.shape
    tr = _row_tile(R, C)

    def body(a_ref, b_ref, o_ref):
        o_ref[...] = a_ref[...] + b_ref[...]

    spec = pl.BlockSpec((tr, C), lambda i: (i, 0))
    return pl.pallas_call(
        body, name="add2", grid=(R // tr,), in_specs=[spec, spec], out_specs=spec,
        out_shape=jax.ShapeDtypeStruct((R, C), F32), compiler_params=_cparams(1),
    )(a, b)


def _sum_chips(parts):
    _, R, C = parts.shape
    tr = _row_tile(R, N_CHIPS * C)

    def body(p_ref, o_ref):
        p = [p_ref[j].astype(F32) for j in range(N_CHIPS)]
        o_ref[...] = ((p[0] + p[1]) + p[2]) + p[3]

    return pl.pallas_call(
        body, name="sum_chips", grid=(R // tr,),
        in_specs=[pl.BlockSpec((N_CHIPS, tr, C), lambda i: (0, i, 0))],
        out_specs=pl.BlockSpec((tr, C), lambda i: (i, 0)),
        out_shape=jax.ShapeDtypeStruct((R, C), F32), compiler_params=_cparams(1),
    )(parts)


def _adamw_math(w, g, m, v):
    m = ADAM_B1 * m + (1.0 - ADAM_B1) * g
    v = ADAM_B2 * v + (1.0 - ADAM_B2) * (g * g)
    m_hat = m / (1.0 - ADAM_B1 ** ADAM_STEP)
    v_hat = v / (1.0 - ADAM_B2 ** ADAM_STEP)
    delta = -ADAM_LR * (m_hat / (jnp.sqrt(v_hat) + ADAM_EPS) + ADAM_WD * w)
    return delta, m, v


def _adamw(w, g, m, v):
    R, C = w.shape
    tr = _row_tile(R, C)

    def body(w_ref, g_ref, m_ref, v_ref, d_ref, mo_ref, vo_ref):
        d_ref[...], mo_ref[...], vo_ref[...] = _adamw_math(w_ref[...], g_ref[...], m_ref[...], v_ref[...])

    spec = pl.BlockSpec((tr, C), lambda i: (i, 0))
    shp = jax.ShapeDtypeStruct((R, C), F32)
    return pl.pallas_call(
        body, name="adamw", grid=(R // tr,), in_specs=[spec] * 4, out_specs=[spec] * 3,
        out_shape=[shp] * 3, compiler_params=_cparams(1),
    )(w, g, m, v)


def _adamw_halves(w, mine, theirs, m, v, cidx, rider=None):
    _, _, R2, C = w.shape
    tr = _row_tile(R2, C)
    nr = R2 // tr

    def body(c_ref, w_ref, a0_ref, b0_ref, a1_ref, b1_ref, m_ref, v_ref, g_ref, d_ref, mo_ref, vo_ref):
        own = pl.program_id(1) == c_ref[0]
        g0 = jnp.where(own, a0_ref[...], b0_ref[...])
        g1 = jnp.where(own, a1_ref[...], b1_ref[...])
        g = jnp.where(pl.program_id(0) == 0, g0, g1)
        g_ref[...] = g
        d_ref[...], mo_ref[...], vo_ref[...] = _adamw_math(w_ref[...], g, m_ref[...], v_ref[...])

    full = pl.BlockSpec((None, None, tr, C), lambda l, h, i, c: (l, h, i, 0))

    def pick(layer, mine_side):
        def index(l, h, i, c):
            used = (l == layer) & ((h == c[0]) == mine_side)
            return (jnp.where(used, i, 0), 0)
        return pl.BlockSpec((tr, C), index)

    shp = jax.ShapeDtypeStruct(w.shape, F32)
    r_in, r_out, r_shapes, r_sems, r_args = _rider_specs(rider)
    last = lambda: (pl.program_id(0) == 1) & (pl.program_id(1) == 1) & (pl.program_id(2) == nr - 1)
    first = lambda: (pl.program_id(0) == 0) & (pl.program_id(1) == 0) & (pl.program_id(2) == 0)
    return pl.pallas_call(
        _ride(body, 8, 4, rider, first, last),
        name="adamw_halves" + ("" if rider is None else "_ride"),
        grid_spec=pltpu.PrefetchScalarGridSpec(
            num_scalar_prefetch=1, grid=(2, 2, nr),
            in_specs=[full, pick(0, True), pick(0, False), pick(1, True), pick(1, False), full, full] + r_in,
            out_specs=[full] * 4 + r_out,
            scratch_shapes=r_sems,
        ),
        out_shape=[shp] * 4 + r_shapes,
        compiler_params=_cparams(3),
    )(cidx, w, mine[0], theirs[0], mine[1], theirs[1], m, v, *r_args)


_ANY = pl.BlockSpec(memory_space=pl.ANY)


def _mesh_pos():
    return lax.axis_index("x"), lax.axis_index("y"), lax.axis_index("c")


def _other_chips(x, y):
    return [(2 * x + (1 - y), x, 1 - y), (2 * (1 - x) + y, 1 - x, y), (2 * (1 - x) + (1 - y), 1 - x, 1 - y)]


def _gathered_shapes(shards):
    return [jax.ShapeDtypeStruct((2, N_CHIPS) + s.shape[1:], s.dtype) for s in shards]


def _gather_sems(n):
    return [pltpu.SemaphoreType.DMA((2 * n,)), pltpu.SemaphoreType.DMA((6 * n,)), pltpu.SemaphoreType.DMA((6 * n,))]


def _gather_steps(ins, outs, lsem, ssem, rsem):
    n = len(ins)
    x, y, c = _mesh_pos()
    me = 2 * x + y
    sib = (x, y, 1 - c)
    chips = _other_chips(x, y)

    def ici(k, d):
        return pltpu.make_async_remote_copy(
            ins[k].at[c], outs[k].at[c, me], ssem.at[6 * k + d], rsem.at[6 * k + d],
            device_id=(chips[d][1], chips[d][2], c), device_id_type=MESH_ID)

    def landed(k, d):
        return pltpu.make_async_remote_copy(
            ins[k].at[c], outs[k].at[c, chips[d][0]], ssem.at[6 * k + d], rsem.at[6 * k + d],
            device_id=sib, device_id_type=MESH_ID)

    def fwd(k, d, half):
        return pltpu.make_async_remote_copy(
            outs[k].at[half, chips[d][0]], outs[k].at[half, chips[d][0]], ssem.at[6 * k + 3 + d],
            rsem.at[6 * k + 3 + d], device_id=sib, device_id_type=MESH_ID)

    def local(k, h):
        return pltpu.make_async_copy(ins[k].at[h], outs[k].at[h, me], lsem.at[2 * k + h])

    def start():
        for k in range(n):
            for h in range(2):
                local(k, h).start()
            for d in range(3):
                ici(k, d).start()

    def mid():
        for d in range(3):
            for k in range(n):
                landed(k, d).wait_recv()
                fwd(k, d, c).start()

    def end():
        for d in range(3):
            for k in range(n):
                fwd(k, d, 1 - c).wait_recv()
        for k in range(n):
            for d in range(3):
                ici(k, d).wait_send()
                fwd(k, d, c).wait_send()
            for h in range(2):
                local(k, h).wait()

    return start, mid, end


def _gather_rider(shards):
    return _Rider(shards, _gathered_shapes(shards), _gather_sems(len(shards)), _gather_steps)


def _pair_rider(arrs, other_half):
    n = len(arrs)

    def steps(ins, outs, ssem, rsem):
        x, y, c = _mesh_pos()

        def copy(k):
            return pltpu.make_async_remote_copy(ins[k].at[:, 1 - c] if other_half else ins[k], outs[k], ssem.at[k],
                                                rsem.at[k], device_id=(x, y, 1 - c), device_id_type=MESH_ID)

        def start():
            for k in range(n):
                copy(k).start()

        def end():
            for k in range(n):
                copy(k).wait()

        return start, end

    shapes = [jax.ShapeDtypeStruct(a.shape[:1] + a.shape[2:] if other_half else a.shape, a.dtype) for a in arrs]
    return _Rider(arrs, shapes, [pltpu.SemaphoreType.DMA((n,)), pltpu.SemaphoreType.DMA((n,))], steps)


def _chip_rider(arrs, broadcast):
    n = len(arrs)

    def steps(ins, outs, lsem, ssem, rsem):
        x, y, c = _mesh_pos()
        me = 2 * x + y

        def copies():
            cps = [pltpu.make_async_copy(ins[k] if broadcast else ins[k].at[me], outs[k].at[me], lsem.at[k])
                   for k in range(n)]
            for k in range(n):
                for d, (j, tx, ty) in enumerate(_other_chips(x, y)):
                    cps.append(pltpu.make_async_remote_copy(
                        ins[k] if broadcast else ins[k].at[j], outs[k].at[me], ssem.at[3 * k + d], rsem.at[3 * k + d],
                        device_id=(tx, ty, c), device_id_type=MESH_ID))
            return cps

        def start():
            for cp in copies():
                cp.start()

        def end():
            for cp in copies():
                cp.wait()

        return start, end

    shapes = [jax.ShapeDtypeStruct(((N_CHIPS,) + a.shape) if broadcast else a.shape, a.dtype) for a in arrs]
    sems = [pltpu.SemaphoreType.DMA((n,)), pltpu.SemaphoreType.DMA((3 * n,)), pltpu.SemaphoreType.DMA((3 * n,))]
    return _Rider(arrs, shapes, sems, steps)


def _run_rider(name, rider):
    n, m = len(rider.arrs), len(rider.out_shapes)

    def body(*refs):
        for step in rider.steps(refs[:n], refs[n:n + m], *refs[n + m:]):
            step()

    return pl.pallas_call(
        body, name=name, in_specs=[_ANY] * n, out_specs=[_ANY] * m, out_shape=rider.out_shapes,
        scratch_shapes=rider.sems,
    )(*rider.arrs)


SMALL = ("norm_g", "lam_re", "lam_im", "b_re", "b_im", "c_re", "c_im", "d_skip", "log_dt", "b_glu", "ln_g", "ln_b",
         "w_s", "b_s", "w_pool", "pool_scale", "final_g")
BIG = ("w_in", "w_glu", "w_out")
WEIGHTS = ("norm_g", "w_in", "lam_re", "lam_im", "b_re", "b_im", "c_re", "c_im", "d_skip", "log_dt", "w_glu", "b_glu",
           "ln_g", "ln_b", "w_s", "b_s", "w_pool", "pool_scale", "w_out", "final_g")
PACK_UNIT = 8 * 128
PACK_ROWS = 1024


def _pack(arrs):
    parts, total = [], 0
    for a in arrs:
        f = a.reshape(-1).astype(F32)
        pad = (-f.shape[0]) % PACK_UNIT
        parts.append(jnp.pad(f, (0, pad)) if pad else f)
        total += f.shape[0] + pad
    tail = (-total) % (PACK_ROWS * 128)
    if tail:
        parts.append(jnp.zeros((tail,), F32))
    return jnp.concatenate(parts).reshape(-1, 128)


def _unpack(buf, like):
    flat = buf.reshape(-1)
    out, off = [], 0
    for a in like:
        n = math.prod(a.shape)
        out.append(flat[off:off + n].reshape(a.shape))
        off += n + ((-n) % PACK_UNIT)
    return out


def _layer_params(l, wt, g_glu):
    a_re, a_im, bb_re, bb_im = _s5_prep(wt["lam_re"][l], wt["lam_im"][l], wt["b_re"][l], wt["b_im"][l], wt["log_dt"][l])
    b4re, b4im = _block_diag_in(bb_re), _block_diag_in(bb_im)
    c4re, c4im = _block_diag_out(wt["c_re"][l]), _block_diag_out(-wt["c_im"][l])
    tr = lambda m: jnp.swapaxes(m, 1, 2).astype(BF16)
    causal = jnp.tril(jnp.ones((CHUNK, CHUNK), dtype=bool))
    ws = jnp.where(causal[None], wt["w_s"][l], 0.0)
    wglu = g_glu[l].reshape(S5_W, S5_W)
    return dict(
        b4re=b4re.astype(BF16), b4im=b4im.astype(BF16), c4re=c4re.astype(BF16), c4im=c4im.astype(BF16),
        b4re_t=tr(b4re), b4im_t=tr(b4im), c4re_t=tr(c4re), c4im_t=tr(c4im),
        dvec=wt["d_skip"][l].reshape(1, S5_W), wglu=wglu, wglu_t=wglu.T, bglu=wt["b_glu"][l].reshape(1, S5_W),
        coef_f=_scan_coefs(a_re, a_im, False), coef_r=_scan_coefs(a_re, a_im, True),
        ws=ws.astype(BF16), ws_t=tr(ws),
        bsf=jnp.broadcast_to(wt["b_s"][l][:, None, :], (SGU_HEADS, CHUNK, CHUNK)).transpose(2, 0, 1).reshape(CHUNK, SGU_W),
        lng=wt["ln_g"][l].reshape(1, SGU_W), lnb=wt["ln_b"][l].reshape(1, SGU_W),
        wp=wt["w_pool"][l].astype(BF16), wp_t=tr(wt["w_pool"][l]), scale=wt["pool_scale"][l].reshape(1, POOL_W),
        norm_g=wt["norm_g"][l].reshape(1, D_MODEL),
    )


def _local_step(x0, tgt, wt, g_in0, rest, rest_gathered, cidx=None, order=None):
    dist = cidx is not None
    xs, saved, params = [x0], [], []
    for l in range(DEPTH):
        norm_g = wt["norm_g"][l].reshape(1, D_MODEL)
        out_rider = None
        if l == 0 and not rest_gathered:
            w_in1, w_glu_b, w_out_b = rest
            h = _rms_h(xs[-1], norm_g)
            z, g_in0, g_glu, g_out = _inproj_first(h, g_in0, order, [w_glu_b, w_out_b])
            out_rider = _gather_rider([w_in1])
        elif l == 0:
            g_in1, g_glu, g_out = rest
            z, h = _inproj(xs[-1], norm_g, g_in0)
        else:
            z, h = _inproj(xs[-1], norm_g, g_in1)
        p = _layer_params(l, wt, g_glu)
        params.append(p)
        ya, yraw, sre, sim, *gathered = _s5_fwd(z, p, out_rider)
        if gathered:
            (g_in1,) = gathered
        yb = _sgu_fwd(z, p["ws"], p["bsf"], p["lng"], p["lnb"])
        yc = _pool_fwd(z, p["wp"], p["scale"])
        w_out = g_out[l].reshape(D_MODEL, D_MODEL)
        if l < DEPTH - 1:
            xn, y = _outproj(ya, yb, yc, w_out, xs[-1])
            xs.append(xn)
        else:
            dx, loss, dfg, y = _outproj_loss(ya, yb, yc, w_out, xs[-1], wt["final_g"].reshape(1, D_MODEL), tgt)
        saved.append((z, h, yraw, sre, sim, y))
    g_in = (g_in0, g_in1)

    gr = {k: [None] * DEPTH for k in WEIGHTS if k != "final_g"}
    mine, theirs, chip_sum = [None] * DEPTH, [None] * DEPTH, None
    halves = lambda a, rows: a.reshape(N_CHIPS, 2, rows // 2, a.shape[-1])
    for l in reversed(range(DEPTH)):
        p = params[l]
        z, h, yraw, sre, sim, y = saved[l]
        w_out = g_out[l].reshape(D_MODEL, D_MODEL)
        dy = _outproj_bwd_dy(dx, w_out)
        gr["w_out"][l] = _outproj_bwd_dw(y, dx)
        ride_c = _chip_rider(chip_sum, False) if dist and l == 0 else None
        dz, dbre, dbim, dcre, dcim, dd, dwg, dbg, da, *landed = _s5_bwd(dy, z, yraw, sre, sim, p, ride_c)
        if ride_c:
            mine[1] = [_sum_chips(r) for r in landed]
        ride_e = _pair_rider(mine[1], False) if dist and l == 0 else None
        dz, dws, dbsf, dlng, dlnb, *got = _sgu_bwd(dy, z, dz, p["ws"], p["ws_t"], p["bsf"], p["lng"], p["lnb"], ride_e)
        if ride_e:
            theirs[1] = got
        dz, dwp, dsc = _pool_bwd(dy, z, dz, p["wp"], p["wp_t"], p["scale"])
        gr["w_in"][l] = _inproj_bwd_dw(h, dz)
        if not dist:
            dx, dng = _inproj_bwd_dx(dz, g_in[l], xs[l], p["norm_g"], dx)
        else:
            part = [halves(gr["w_in"][l], D_MODEL), halves(dwg, S5_W // N_CHIPS),
                    halves(gr["w_out"][l], D_MODEL // N_CHIPS)]
            ride_a = _pair_rider(part, True)
            if l == 1:
                dx, dng, *from_sib = _inproj_bwd_dx(dz, g_in[l], xs[l], p["norm_g"], dx, ride_a)
                chip_sum = [_add_own_half(a, r, cidx) for a, r in zip(part, from_sib)]
            else:
                nt = x0.shape[0] // _dx_tile(x0.shape[0])
                n_top = max(nt // 4, 1)
                dx_top, dng_top, *from_sib = _inproj_bwd_dx(dz, g_in[l], xs[l], p["norm_g"], dx, ride_a,
                                                            tiles=(0, n_top))
                chip_sum0 = [_add_own_half(a, r, cidx) for a, r in zip(part, from_sib)]
                dx, dng_rest, *landed = _inproj_bwd_dx(dz, g_in[l], xs[l], p["norm_g"], dx,
                                                       _chip_rider(chip_sum0, False), tiles=(n_top, nt - n_top),
                                                       prev=dx_top)
                dng = dng_top + dng_rest
                mine[0] = [_sum_chips(r) for r in landed]

        raw = (wt["lam_re"][l], wt["lam_im"][l], wt["b_re"][l], wt["b_im"][l], wt["log_dt"][l])
        _, vjp = jax.vjp(_s5_prep, *raw)
        da = jnp.sum(da, axis=1)
        cot = (da[0].reshape(S5_GROUPS, S5_STATE), da[1].reshape(S5_GROUPS, S5_STATE),
               _block_diag_in_grad(dbre), _block_diag_in_grad(dbim))
        gr["lam_re"][l], gr["lam_im"][l], gr["b_re"][l], gr["b_im"][l], gr["log_dt"][l] = vjp(cot)
        gr["c_re"][l] = _block_diag_out_grad(dcre)
        gr["c_im"][l] = -_block_diag_out_grad(dcim)
        gr["d_skip"][l] = dd.reshape(S5_GROUPS, S5_CH)
        gr["w_glu"][l] = dwg
        gr["b_glu"][l] = dbg.reshape(S5_W)
        causal = jnp.tril(jnp.ones((CHUNK, CHUNK), dtype=bool))
        gr["w_s"][l] = jnp.where(causal[None], dws, 0.0)
        gr["b_s"][l] = dbsf.reshape(CHUNK, SGU_HEADS, CHUNK).sum(-1).T
        gr["ln_g"][l] = dlng.reshape(SGU_W)
        gr["ln_b"][l] = dlnb.reshape(SGU_W)
        gr["w_pool"][l] = dwp
        gr["pool_scale"][l] = dsc.reshape(POOL_W)
        gr["norm_g"][l] = dng.reshape(D_MODEL)

    grads = {k: (v if k in BIG else jnp.stack(v)) for k, v in gr.items()}
    grads["final_g"] = dfg.reshape(D_MODEL)
    if dist:
        for i, k in enumerate(BIG):
            grads[k] = ([mine[l][i] for l in range(DEPTH)], [None, theirs[1][i]])
    return loss, dx, grads


def kernel(x, norm_g, w_in, lam_re, lam_im, b_re, b_im, c_re, c_im, d_skip, log_dt, w_glu, b_glu, ln_g, ln_b, w_s, b_s, w_pool, pool_scale, w_out, final_g, loss_target, m_norm_g, m_w_in, m_lam_re, m_lam_im, m_b_re, m_b_im, m_c_re, m_c_im, m_d_skip, m_log_dt, m_w_glu, m_b_glu, m_ln_g, m_ln_b, m_w_s, m_b_s, m_w_pool, m_pool_scale, m_w_out, m_final_g, v_norm_g, v_w_in, v_lam_re, v_lam_im, v_b_re, v_b_im, v_c_re, v_c_im, v_d_skip, v_log_dt, v_w_glu, v_b_glu, v_ln_g, v_ln_b, v_w_s, v_b_s, v_w_pool, v_pool_scale, v_w_out, v_final_g):
    wt = dict(norm_g=norm_g, w_in=w_in, lam_re=lam_re, lam_im=lam_im, b_re=b_re, b_im=b_im, c_re=c_re, c_im=c_im,
              d_skip=d_skip, log_dt=log_dt, w_glu=w_glu, b_glu=b_glu, ln_g=ln_g, ln_b=ln_b, w_s=w_s, b_s=b_s,
              w_pool=w_pool, pool_scale=pool_scale, w_out=w_out, final_g=final_g)
    mom = dict(norm_g=m_norm_g, w_in=m_w_in, lam_re=m_lam_re, lam_im=m_lam_im, b_re=m_b_re, b_im=m_b_im, c_re=m_c_re,
               c_im=m_c_im, d_skip=m_d_skip, log_dt=m_log_dt, w_glu=m_w_glu, b_glu=m_b_glu, ln_g=m_ln_g, ln_b=m_ln_b,
               w_s=m_w_s, b_s=m_b_s, w_pool=m_w_pool, pool_scale=m_pool_scale, w_out=m_w_out, final_g=m_final_g)
    vel = dict(norm_g=v_norm_g, w_in=v_w_in, lam_re=v_lam_re, lam_im=v_lam_im, b_re=v_b_re, b_im=v_b_im, c_re=v_c_re,
               c_im=v_c_im, d_skip=v_d_skip, log_dt=v_log_dt, w_glu=v_w_glu, b_glu=v_b_glu, ln_g=v_ln_g, ln_b=v_ln_b,
               w_s=v_w_s, b_s=v_b_s, w_pool=v_w_pool, pool_scale=v_pool_scale, w_out=v_w_out, final_g=v_final_g)
    T = x.shape[1]
    cidx = lax.axis_index("c").astype(jnp.int32).reshape(1)

    w_in_b = w_in.astype(BF16)
    w0 = w_in_b[0].reshape(2, HALF_D, SHARD_COLS)
    rest = (w_in_b[1].reshape(2, HALF_D, SHARD_COLS), w_glu.astype(BF16), w_out.astype(BF16))
    mx, my = lax.axis_index("x"), lax.axis_index("y")
    order = jnp.stack([2 * mx + my] + [j for j, _, _ in _other_chips(mx, my)]).astype(jnp.int32)
    loss, grad_x, grads = _local_step(x.reshape(T, D_MODEL), loss_target.reshape(T, D_MODEL), wt, w0, rest, False,
                                      cidx, order)

    packed = _pack([grads[k] for k in SMALL] + [loss[0, 0:1]])
    sib_packed, *theirs0 = _run_rider("pair_exchange", _pair_rider([packed] + [grads[k][0][0] for k in BIG], False))
    for k, t in zip(BIG, theirs0):
        grads[k][1][0] = t
    chip_packed = _add2(packed, sib_packed)
    half_rows = chip_packed.shape[0] // 2
    my_half = lax.dynamic_index_in_dim(chip_packed.reshape(2, half_rows, 128), cidx[0], 0, keepdims=False)
    small_ride = _chip_rider([my_half], True)

    out_g, out_d, out_m, out_v = {}, {}, {}, {}
    all_half = None
    for k in BIG:
        shape = wt[k].shape
        quad = lambda t: t.reshape(2, 2, shape[1] // 2, shape[2])
        g, d, m, v, *landed = _adamw_halves(quad(wt[k]), grads[k][0], grads[k][1], quad(mom[k]), quad(vel[k]), cidx,
                                            small_ride if k == BIG[0] else None)
        if landed:
            (all_half,) = landed
        out_g[k], out_d[k], out_m[k], out_v[k] = (t.reshape(shape) for t in (g, d, m, v))

    mine_half = _sum_chips(all_half)
    (their_half,) = _run_rider("small_result_exchange", _pair_rider([mine_half], False))
    total = jnp.where(cidx[0] == 0, jnp.concatenate([mine_half, their_half]), jnp.concatenate([their_half, mine_half]))
    like = [wt[k] for k in SMALL]
    small_g = _unpack(total, like + [loss[0, 0:1]])
    loss_out = small_g[-1].reshape(())
    w_p, m_p, v_p = _pack(like), _pack([mom[k] for k in SMALL]), _pack([vel[k] for k in SMALL])
    d_p, mo_p, vo_p = _adamw(w_p, total, m_p, v_p)
    for k, g, d, m, v in zip(SMALL, small_g[:-1], _unpack(d_p, like), _unpack(mo_p, like), _unpack(vo_p, like)):
        out_g[k], out_d[k], out_m[k], out_v[k] = g, d, m, v

    return (loss_out, grad_x.reshape(x.shape), *[out_g[k] for k in WEIGHTS], *[out_d[k] for k in WEIGHTS],
            *[out_m[k] for k in WEIGHTS], *[out_v[k] for k in WEIGHTS])
```

```python
import functools
import math

import jax
import jax.numpy as jnp
from jax import lax
from jax.experimental import pallas as pl
from jax.experimental.pallas import tpu as pltpu

F32 = jnp.float32
BF16 = jnp.bfloat16

D_MODEL = 2048
DEPTH = 2
S5_W = 512
SGU_W = 1024
POOL_W = 512
IN_COLS = 5120
N_CHIPS = 4
SHARD_COLS = IN_COLS // N_CHIPS
S5_GROUPS = 32
S5_STATE = 64
S5_CH = 16
STATE_W = S5_GROUPS * S5_STATE
SUPER = 4
CHUNK = 128
SGU_HEADS = 8
POOL_WINDOWS = (2, 4, 8, 16)
POOL_HALO = 16
RMS_EPS = 1e-6
LN_EPS = 1e-5
SCAN_COLS = 512

ADAM_LR = 0.001
ADAM_B1 = 0.9
ADAM_B2 = 0.999
ADAM_EPS = 1e-08
ADAM_WD = 0.01
ADAM_STEP = 10

VMEM_LIMIT = 56 * 1024 * 1024
MESH_ID = pl.DeviceIdType.MESH

_GELU_K0 = math.sqrt(2.0 / math.pi)
_GELU_K1 = 0.044715


def _cparams(n_axes):
    return pltpu.CompilerParams(dimension_semantics=("arbitrary",) * n_axes, vmem_limit_bytes=VMEM_LIMIT)


def _gelu(x):
    t = jnp.tanh(_GELU_K0 * (x + _GELU_K1 * (x * x * x)))
    return 0.5 * x * (1.0 + t)


def _gelu_and_grad(x):
    x2 = x * x
    t = jnp.tanh(_GELU_K0 * (x + _GELU_K1 * (x * x2)))
    g = 0.5 * x * (1.0 + t)
    dg = 0.5 * (1.0 + t) + 0.5 * x * (1.0 - t * t) * (_GELU_K0 * (1.0 + 3.0 * _GELU_K1 * x2))
    return g, dg


def _silu_and_grad(x):
    s = jax.nn.sigmoid(x)
    return x * s, s * (1.0 + x * (1.0 - s))


def _dot(a, b):
    return jnp.dot(a.astype(BF16), b.astype(BF16), preferred_element_type=F32)


def _dot_nt(a, b):
    return lax.dot_general(a.astype(BF16), b.astype(BF16), (((1,), (1,)), ((), ())), preferred_element_type=F32)


def _dot_tn(a, b):
    return lax.dot_general(a.astype(BF16), b.astype(BF16), (((0,), (0,)), ((), ())), preferred_element_type=F32)


def _full(shape):
    nd = len(shape)
    return pl.BlockSpec(shape, lambda *_: (0,) * nd)


class _Rider:
    def __init__(self, arrs, out_shapes, sems, steps):
        self.arrs, self.out_shapes, self.sems, self.steps = list(arrs), list(out_shapes), list(sems), steps


def _ride(body, n_in, n_out, rider, first, last, middle=None):
    if rider is None:
        return body
    ri, ro, ns = len(rider.arrs), len(rider.out_shapes), len(rider.sems)

    def wrapped(*refs):
        o0 = n_in + ri
        steps = rider.steps(refs[n_in:o0], refs[o0 + n_out:o0 + n_out + ro], *refs[len(refs) - ns:])
        pl.when(first())(steps[0])
        if len(steps) == 3:
            pl.when(middle())(steps[1])
        body(*refs[:n_in], *refs[o0:o0 + n_out], *refs[o0 + n_out + ro:len(refs) - ns])
        pl.when(last())(steps[-1])

    return wrapped


class _ColumnWriter:
    def __init__(self, stage_ref, sem_ref, dst_ref, col0, step, n_steps):
        self.stage, self.sem, self.dst, self.col0, self.step, self.n = stage_ref, sem_ref, dst_ref, col0, step, n_steps
        self.tm, self.w = stage_ref.shape[1], stage_ref.shape[2]

    def _copy(self, slot, row0):
        return pltpu.make_async_copy(self.stage.at[slot],
                                     self.dst.at[pl.ds(row0, self.tm), pl.ds(self.col0, self.w)], self.sem.at[slot])

    def slot(self):
        s = self.step % 2

        @pl.when(self.step >= 2)
        def _():
            self._copy(s, 0).wait()

        return self.stage.at[s]

    def send(self, row0):
        s = self.step % 2
        self._copy(s, row0).start()

        @pl.when(self.step == self.n - 1)
        def _():
            self._copy(s, 0).wait()
            if self.n >= 2:
                self._copy(1 - s, 0).wait()


def _stage_scratch(tm, widths):
    return ([pltpu.VMEM((2, tm, w), BF16) for w in widths], [pltpu.SemaphoreType.DMA((2,)) for _ in widths])


def _rider_specs(rider):
    if rider is None:
        return [], [], [], [], []
    anyspec = pl.BlockSpec(memory_space=pl.ANY)
    return ([anyspec] * len(rider.arrs), [anyspec] * len(rider.out_shapes), rider.out_shapes, rider.sems, rider.arrs)


HALF_D = D_MODEL // 2


def _w_in_rows(parts):
    kp = HALF_D // len(parts)
    return [(p, hf, hf * HALF_D + p * kp, kp) for p in range(len(parts)) for hf in range(2)]


def _w_in_specs(parts):
    kp = HALF_D // len(parts)
    return [pl.BlockSpec((2, None, kp, SHARD_COLS), lambda i, j: (0, j, 0, 0)) for _ in parts]


def _inproj(x, g, w_parts):
    T = x.shape[0]
    tm = min(512, T)
    n = len(w_parts)

    def body(x_ref, g_ref, *refs):
        w_refs = refs[:n]
        z_ref, h_ref, hs_ref = refs[n:]

        @pl.when(pl.program_id(1) == 0)
        def _():
            xv = x_ref[...]
            r = lax.rsqrt(jnp.mean(xv * xv, axis=-1, keepdims=True) + RMS_EPS)
            hv = (xv * r * g_ref[...]).astype(BF16)
            hs_ref[...] = hv
            h_ref[...] = hv

        acc = None
        for p, hf, r0, kp in _w_in_rows(w_parts):
            t = jnp.dot(hs_ref[:, r0:r0 + kp], w_refs[p][hf], preferred_element_type=F32)
            acc = t if acc is None else acc + t
        z_ref[...] = acc

    return pl.pallas_call(
        body,
        name="inproj",
        grid=(T // tm, N_CHIPS),
        in_specs=[
            pl.BlockSpec((tm, D_MODEL), lambda i, j: (i, 0)),
            pl.BlockSpec((1, D_MODEL), lambda i, j: (0, 0)),
        ] + _w_in_specs(w_parts),
        out_specs=[
            pl.BlockSpec((tm, SHARD_COLS), lambda i, j: (i, j)),
            pl.BlockSpec((tm, D_MODEL), lambda i, j: (i, 0)),
        ],
        out_shape=[jax.ShapeDtypeStruct((T, IN_COLS), F32), jax.ShapeDtypeStruct((T, D_MODEL), BF16)],
        scratch_shapes=[pltpu.VMEM((tm, D_MODEL), BF16)],
        compiler_params=_cparams(2),
    )(x, g, *w_parts)


def _rms_h(x, g):
    T = x.shape[0]
    tm = min(512, T)

    def body(x_ref, g_ref, h_ref):
        xv = x_ref[...]
        r = lax.rsqrt(jnp.mean(xv * xv, axis=-1, keepdims=True) + RMS_EPS)
        h_ref[...] = (xv * r * g_ref[...]).astype(BF16)

    return pl.pallas_call(
        body, name="rms_h", grid=(T // tm,),
        in_specs=[pl.BlockSpec((tm, D_MODEL), lambda i: (i, 0)), pl.BlockSpec((1, D_MODEL), lambda i: (0, 0))],
        out_specs=pl.BlockSpec((tm, D_MODEL), lambda i: (i, 0)),
        out_shape=jax.ShapeDtypeStruct((T, D_MODEL), BF16), compiler_params=_cparams(1),
    )(x, g)


def _inproj_first(h, w0, order, riders):
    T = h.shape[0]
    tm = min(512, T)
    ni = T // tm
    n = len(riders)

    def body(order_ref, h_ref, w0_ref, *refs):
        rin = refs[:n]
        z_ref, gin_ref = refs[n:n + 2]
        rout = refs[n + 2:2 * n + 2]
        wbuf, csem, lsem, ssem, rsem = refs[2 * n + 2:2 * n + 7]
        s, i = pl.program_id(0), pl.program_id(1)
        x, y, c = _mesh_pos()
        me = 2 * x + y
        sib = (x, y, 1 - c)
        chips = _other_chips(x, y)
        if n:
            r_start, r_mid, r_end = _gather_steps(rin, rout, *refs[2 * n + 7:])

        def ici(d):
            return pltpu.make_async_remote_copy(w0_ref.at[c], gin_ref.at[c, me], ssem.at[d], rsem.at[d],
                                                device_id=(chips[d][1], chips[d][2], c), device_id_type=MESH_ID)

        def landed(d):
            return pltpu.make_async_remote_copy(w0_ref.at[c], gin_ref.at[c, chips[d][0]], ssem.at[d], rsem.at[d],
                                                device_id=sib, device_id_type=MESH_ID)

        def fwd(d, half):
            blk = gin_ref.at[half, chips[d][0]]
            return pltpu.make_async_remote_copy(blk, blk, ssem.at[3 + d], rsem.at[3 + d], device_id=sib,
                                                device_id_type=MESH_ID)

        def local(hf):
            return pltpu.make_async_copy(w0_ref.at[hf], gin_ref.at[hf, me], lsem.at[hf])

        def load(src):
            cp = pltpu.make_async_copy(src, wbuf, csem.at[0])
            cp.start()
            cp.wait()

        @pl.when((s == 0) & (i == 0))
        def _():
            for d in range(3):
                ici(d).start()
            local(0).start()
            local(1).start()
            load(w0_ref)

        for d in range(3):
            @pl.when((s == d + 1) & (i == 0))
            def _(d=d):
                landed(d).wait_recv()
                fwd(d, c).start()
                fwd(d, 1 - c).wait_recv()
                load(gin_ref.at[:, chips[d][0]])
                if d == 1 and n:
                    r_start()

        z_ref[...] = (jnp.dot(h_ref[:, 0:HALF_D], wbuf[0], preferred_element_type=F32)
                      + jnp.dot(h_ref[:, HALF_D:D_MODEL], wbuf[1], preferred_element_type=F32))

        @pl.when((s == N_CHIPS - 1) & (i == ni - 1))
        def _():
            for d in range(3):
                ici(d).wait_send()
                fwd(d, c).wait_send()
            local(0).wait()
            local(1).wait()
            if n:
                r_mid()
                r_end()

    anyspec = pl.BlockSpec(memory_space=pl.ANY)
    return pl.pallas_call(
        body,
        name="inproj_first",
        grid_spec=pltpu.PrefetchScalarGridSpec(
            num_scalar_prefetch=1,
            grid=(N_CHIPS, ni),
            in_specs=[pl.BlockSpec((tm, D_MODEL), lambda s, i, o: (i, 0)), anyspec] + [anyspec] * n,
            out_specs=[pl.BlockSpec((tm, SHARD_COLS), lambda s, i, o: (i, o[s])), anyspec] + [anyspec] * n,
            scratch_shapes=[pltpu.VMEM((2, HALF_D, SHARD_COLS), BF16), pltpu.SemaphoreType.DMA((1,)),
                            pltpu.SemaphoreType.DMA((2,)), pltpu.SemaphoreType.DMA((6,)),
                            pltpu.SemaphoreType.DMA((6,))] + (_gather_sems(n) if n else []),
        ),
        out_shape=[jax.ShapeDtypeStruct((T, IN_COLS), F32),
                   jax.ShapeDtypeStruct((2, N_CHIPS, HALF_D, SHARD_COLS), BF16)] + _gathered_shapes(riders),
        compiler_params=_cparams(2),
    )(order, h, w0, *riders)


def _outproj(ya, yb, yc, w, x, rider=None):
    T = x.shape[0]
    tm = min(512, T)
    tn = 1024
    ni, nj = T // tm, D_MODEL // tn
    r_in, r_out, r_shapes, r_sems, r_args = _rider_specs(rider)
    at_end = lambda: (pl.program_id(0) == ni - 1) & (pl.program_id(1) == nj - 1)

    def body(ya_ref, yb_ref, yc_ref, w_ref, x_ref, o_ref, y_ref):
        acc = jnp.dot(ya_ref[...], w_ref[0:S5_W, :], preferred_element_type=F32)
        acc += jnp.dot(yb_ref[...], w_ref[S5_W:S5_W + SGU_W, :], preferred_element_type=F32)
        acc += jnp.dot(yc_ref[...], w_ref[S5_W + SGU_W:D_MODEL, :], preferred_element_type=F32)
        o_ref[...] = x_ref[...] + acc

        @pl.when(pl.program_id(1) == 0)
        def _():
            y_ref[:, 0:S5_W] = ya_ref[...]
            y_ref[:, S5_W:S5_W + SGU_W] = yb_ref[...]
            y_ref[:, S5_W + SGU_W:D_MODEL] = yc_ref[...]

    return pl.pallas_call(
        _ride(body, 5, 2, rider, lambda: (pl.program_id(0) == 0) & (pl.program_id(1) == 0), at_end, at_end),
        name="outproj" + ("" if rider is None else "_ride"),
        grid=(ni, nj),
        in_specs=[
            pl.BlockSpec((tm, S5_W), lambda i, j: (i, 0)),
            pl.BlockSpec((tm, SGU_W), lambda i, j: (i, 0)),
            pl.BlockSpec((tm, POOL_W), lambda i, j: (i, 0)),
            pl.BlockSpec((D_MODEL, tn), lambda i, j: (0, j)),
            pl.BlockSpec((tm, tn), lambda i, j: (i, j)),
        ] + r_in,
        out_specs=[
            pl.BlockSpec((tm, tn), lambda i, j: (i, j)),
            pl.BlockSpec((tm, D_MODEL), lambda i, j: (i, 0)),
        ] + r_out,
        out_shape=[jax.ShapeDtypeStruct((T, D_MODEL), F32), jax.ShapeDtypeStruct((T, D_MODEL), BF16)] + r_shapes,
        scratch_shapes=r_sems,
        compiler_params=_cparams(2),
    )(ya, yb, yc, w, x, *r_args)


def _outproj_bwd_dy(dxo, w):
    T = dxo.shape[0]
    tm = min(512, T)
    tn = 1024

    def body(d_ref, w_ref, o_ref, ds_ref):
        @pl.when(pl.program_id(1) == 0)
        def _():
            ds_ref[...] = d_ref[...].astype(BF16)

        o_ref[...] = lax.dot_general(ds_ref[...], w_ref[...], (((1,), (1,)), ((), ())), preferred_element_type=F32)

    return pl.pallas_call(
        body,
        name="outproj_bwd_dy",
        grid=(T // tm, D_MODEL // tn),
        in_specs=[
            pl.BlockSpec((tm, D_MODEL), lambda i, j: (i, 0)),
            pl.BlockSpec((tn, D_MODEL), lambda i, j: (j, 0)),
        ],
        out_specs=pl.BlockSpec((tm, tn), lambda i, j: (i, j)),
        out_shape=jax.ShapeDtypeStruct((T, D_MODEL), F32),
        scratch_shapes=[pltpu.VMEM((tm, D_MODEL), BF16)],
        compiler_params=_cparams(2),
    )(dxo, w)


def _outproj_bwd_dw(y, dxo):
    T = y.shape[0]
    tm = min(512, T)
    tr = 1024

    def body(y_ref, d_ref, o_ref):
        @pl.when(pl.program_id(1) == 0)
        def _():
            o_ref[...] = jnp.zeros_like(o_ref)

        o_ref[...] += _dot_tn(y_ref[...], d_ref[...])

    return pl.pallas_call(
        body,
        name="outproj_bwd_dw",
        grid=(D_MODEL // tr, T // tm),
        in_specs=[
            pl.BlockSpec((tm, tr), lambda p, t: (t, p)),
            pl.BlockSpec((tm, D_MODEL), lambda p, t: (t, 0)),
        ],
        out_specs=pl.BlockSpec((tr, D_MODEL), lambda p, t: (p, 0)),
        out_shape=jax.ShapeDtypeStruct((D_MODEL, D_MODEL), F32),
        compiler_params=_cparams(2),
    )(y, dxo)


def _inproj_bwd_dw(h, dz):
    T = h.shape[0]
    tm = min(512, T)

    def body(h_ref, dz_ref, o_ref):
        @pl.when(pl.program_id(1) == 0)
        def _():
            o_ref[...] = jnp.zeros_like(o_ref)

        o_ref[...] += _dot_tn(h_ref[...], dz_ref[...])

    return pl.pallas_call(
        body,
        name="inproj_bwd_dw",
        grid=(N_CHIPS, T // tm),
        in_specs=[
            pl.BlockSpec((tm, D_MODEL), lambda j, t: (t, 0)),
            pl.BlockSpec((tm, SHARD_COLS), lambda j, t: (t, j)),
        ],
        out_specs=pl.BlockSpec((None, D_MODEL, SHARD_COLS), lambda j, t: (j, 0, 0)),
        out_shape=jax.ShapeDtypeStruct((N_CHIPS, D_MODEL, SHARD_COLS), F32),
        compiler_params=_cparams(2),
    )(h, dz)


def _dx_tile(T):
    return min(512, max(T // 4, 8))


def _inproj_bwd_dx(dz, w4, x, g, dxo, rider=None, tiles=None, prev=None):
    T = x.shape[0]
    tm = _dx_tile(T)
    t0, ni = tiles if tiles else (0, T // tm)
    nk = N_CHIPS
    nt = (((1,), (1,)), ((), ()))
    w_parts = w4 if isinstance(w4, (list, tuple)) else [w4]
    nw = len(w_parts)
    n_in = 4 + nw + (0 if prev is None else 1)

    def body(dz_ref, *rest):
        w_refs = rest[:nw]
        x_ref, g_ref, dxo_ref = rest[nw:nw + 3]
        dx_ref, dg_ref, acc_ref = rest[-3:]
        i, j = pl.program_id(0), pl.program_id(1)
        for p, hf, r0, kp in _w_in_rows(w_parts):
            t = lax.dot_general(dz_ref[...], w_refs[p][hf], nt, preferred_element_type=F32)

            @pl.when(j == 0)
            def _(t=t, r0=r0, kp=kp):
                acc_ref[:, r0:r0 + kp] = t

            @pl.when(j > 0)
            def _(t=t, r0=r0, kp=kp):
                acc_ref[:, r0:r0 + kp] += t

        @pl.when(j == nk - 1)
        def _():
            @pl.when(i == 0)
            def _():
                dg_ref[...] = jnp.zeros_like(dg_ref)

            rc = min(128, tm)
            for c in range(tm // rc):
                rows = slice(c * rc, (c + 1) * rc)
                dh = acc_ref[rows, :]
                xv = x_ref[rows, :]
                r = lax.rsqrt(jnp.mean(xv * xv, axis=-1, keepdims=True) + RMS_EPS)
                xh = xv * r
                w = dh * g_ref[...]
                dx_ref[rows, :] = dxo_ref[rows, :] + r * (w - xh * jnp.mean(w * xh, axis=-1, keepdims=True))
                dg_ref[...] += jnp.sum(dh * xh, axis=0, keepdims=True)

    r_in, r_out, r_shapes, r_sems, r_args = _rider_specs(rider)
    return pl.pallas_call(
        _ride(body, n_in, 2, rider, lambda: (pl.program_id(0) == 0) & (pl.program_id(1) == 0),
              lambda: (pl.program_id(0) == ni - 1) & (pl.program_id(1) == nk - 1)),
        name="inproj_bwd_dx" + ("" if rider is None else "_ride") + ("" if prev is None else "_rest"),
        grid=(ni, nk),
        in_specs=[pl.BlockSpec((tm, SHARD_COLS), lambda i, j: (i + t0, j))] + _w_in_specs(w_parts) + [
            pl.BlockSpec((tm, D_MODEL), lambda i, j: (i + t0, 0)),
            pl.BlockSpec((1, D_MODEL), lambda i, j: (0, 0)),
            pl.BlockSpec((tm, D_MODEL), lambda i, j: (i + t0, 0)),
        ] + ([] if prev is None else [pl.BlockSpec(memory_space=pl.ANY)]) + r_in,
        out_specs=[
            pl.BlockSpec((tm, D_MODEL), lambda i, j: (i + t0, 0)),
            pl.BlockSpec((1, D_MODEL), lambda i, j: (0, 0)),
        ] + r_out,
        out_shape=[jax.ShapeDtypeStruct((T, D_MODEL), F32), jax.ShapeDtypeStruct((1, D_MODEL), F32)] + r_shapes,
        scratch_shapes=[pltpu.VMEM((tm, D_MODEL), F32)] + r_sems,
        input_output_aliases={} if prev is None else {4 + nw: 0},
        compiler_params=_cparams(2),
    )(dz, *w_parts, x, g, dxo, *([] if prev is None else [prev]), *r_args)


def _outproj_loss(ya, yb, yc, w, x, g, tgt):
    T = x.shape[0]
    tm = min(256, T)

    def body(ya_ref, yb_ref, yc_ref, w_ref, x_ref, g_ref, t_ref, dx_ref, l_ref, dg_ref, y_ref):
        i = pl.program_id(0)
        acc = jnp.dot(ya_ref[...], w_ref[0:S5_W, :], preferred_element_type=F32)
        acc += jnp.dot(yb_ref[...], w_ref[S5_W:S5_W + SGU_W, :], preferred_element_type=F32)
        acc += jnp.dot(yc_ref[...], w_ref[S5_W + SGU_W:D_MODEL, :], preferred_element_type=F32)
        y_ref[:, 0:S5_W] = ya_ref[...]
        y_ref[:, S5_W:S5_W + SGU_W] = yb_ref[...]
        y_ref[:, S5_W + SGU_W:D_MODEL] = yc_ref[...]
        xv = x_ref[...] + acc
        r = lax.rsqrt(jnp.mean(xv * xv, axis=-1, keepdims=True) + RMS_EPS)
        xh = xv * r
        err = xh * g_ref[...] - t_ref[...]
        lpart = 0.5 * jnp.sum(jnp.mean(err * err, axis=-1, keepdims=True), axis=0, keepdims=True)
        dout = err * (1.0 / D_MODEL)
        w = dout * g_ref[...]
        dx_ref[...] = r * (w - xh * jnp.mean(w * xh, axis=-1, keepdims=True))
        gpart = jnp.sum(dout * xh, axis=0, keepdims=True)

        @pl.when(i == 0)
        def _():
            l_ref[...] = jnp.broadcast_to(lpart, l_ref.shape)
            dg_ref[...] = gpart

        @pl.when(i > 0)
        def _():
            l_ref[...] += jnp.broadcast_to(lpart, l_ref.shape)
            dg_ref[...] += gpart

    row = lambda w: pl.BlockSpec((tm, w), lambda i: (i, 0))
    return pl.pallas_call(
        body,
        name="outproj_loss",
        grid=(T // tm,),
        in_specs=[row(S5_W), row(SGU_W), row(POOL_W), _full((D_MODEL, D_MODEL)), row(D_MODEL), _full((1, D_MODEL)),
                  row(D_MODEL)],
        out_specs=[row(D_MODEL), _full((1, 128)), _full((1, D_MODEL)), row(D_MODEL)],
        out_shape=[
            jax.ShapeDtypeStruct((T, D_MODEL), F32),
            jax.ShapeDtypeStruct((1, 128), F32),
            jax.ShapeDtypeStruct((1, D_MODEL), F32),
            jax.ShapeDtypeStruct((T, D_MODEL), BF16),
        ],
        compiler_params=_cparams(1),
    )(ya, yb, yc, w, x, g, tgt)


def _s5_prep(lam_re, lam_im, b_re, b_im, log_dt):
    lam = lax.complex(lam_re, lam_im)
    dt = jnp.exp(log_dt)[:, None]
    a = jnp.exp(lam * dt)
    bbar = ((a - 1.0) / lam)[..., None] * lax.complex(b_re, b_im)
    return jnp.real(a), jnp.imag(a), jnp.real(bbar), jnp.imag(bbar)


def _block_diag_in(m):
    m4 = m.reshape(SUPER, 8, S5_STATE, S5_CH)
    eye = jnp.eye(8, dtype=m.dtype)
    out = jnp.einsum("jgph,gk->jghkp", m4, eye)
    return out.reshape(SUPER, 8 * S5_CH, 8 * S5_STATE)


def _block_diag_in_grad(d):
    d6 = d.reshape(SUPER, 8, S5_CH, 8, S5_STATE)
    diag = jnp.einsum("jghgp->jgph", d6)
    return diag.reshape(S5_GROUPS, S5_STATE, S5_CH)


def _block_diag_out(m):
    m4 = m.reshape(SUPER, 8, S5_CH, S5_STATE)
    eye = jnp.eye(8, dtype=m.dtype)
    out = jnp.einsum("jghp,gk->jgpkh", m4, eye)
    return out.reshape(SUPER, 8 * S5_STATE, 8 * S5_CH)


def _block_diag_out_grad(d):
    d6 = d.reshape(SUPER, 8, S5_STATE, 8, S5_CH)
    diag = jnp.einsum("jgpgh->jghp", d6)
    return diag.reshape(S5_GROUPS, S5_CH, S5_STATE)


def _scan_coefs(a_re, a_im, reverse):
    a = lax.complex(a_re.reshape(-1), a_im.reshape(-1))
    if reverse:
        a = jnp.conj(a)
    pw = [a]
    for _ in range(7):
        pw.append(pw[-1] * a)
    rows = jnp.arange(8)

    def masked(k):
        m = (rows + k <= 7) if reverse else (rows >= k)
        return jnp.where(m[:, None], pw[k - 1][None, :], 0.0)

    a1, a2, a4 = masked(1), masked(2), masked(4)
    carry = jnp.stack([pw[7 - r] for r in range(8)]) if reverse else jnp.stack(pw)
    parts = []
    for c in (a1, a2, a4, carry):
        parts += [jnp.real(c), jnp.imag(c)]
    return jnp.stack(parts).astype(F32)


def _scan_block(r, im, coef_ref, cs, reverse):
    for k, idx in ((1, 0), (2, 2), (4, 4)):
        ar = coef_ref[idx, :, cs]
        ai = coef_ref[idx + 1, :, cs]
        sh = 8 - k if reverse else k
        rr = pltpu.roll(r, sh, 0)
        ri = pltpu.roll(im, sh, 0)
        r, im = r + ar * rr - ai * ri, im + ar * ri + ai * rr
    return r, im


def _s5_fwd(z, p, rider=None):
    T = z.shape[0]
    tm = min(256, T)
    nblk = tm // 8
    W = STATE_W

    def body(xa_ref, ga_ref, bre_ref, bim_ref, cre_ref, cim_ref, dv_ref, wg_ref, bg_ref, coef_ref,
             ya_ref, yraw_ref, sre_ref, sim_ref, wre, wim):
        @pl.when(pl.program_id(0) == 0)
        def _():
            wre[0:8, :] = jnp.zeros((8, W), F32)
            wim[0:8, :] = jnp.zeros((8, W), F32)

        xa = xa_ref[...]
        xab = xa.astype(BF16)
        for j in range(SUPER):
            xj = xab[:, j * 128:(j + 1) * 128]
            wre[8:8 + tm, j * 512:(j + 1) * 512] = jnp.dot(xj, bre_ref[j], preferred_element_type=F32)
            wim[8:8 + tm, j * 512:(j + 1) * 512] = jnp.dot(xj, bim_ref[j], preferred_element_type=F32)

        def blk(b, carry):
            base = pl.multiple_of(8 + b * 8, 8)
            for cc in range(W // SCAN_COLS):
                cs = pl.ds(cc * SCAN_COLS, SCAN_COLS)
                r, im = _scan_block(wre[pl.ds(base, 8), cs], wim[pl.ds(base, 8), cs], coef_ref, cs, False)
                cr = wre[pl.ds(base - 1, 1), cs]
                ci = wim[pl.ds(base - 1, 1), cs]
                pr = coef_ref[6, :, cs]
                pi = coef_ref[7, :, cs]
                wre[pl.ds(base, 8), cs] = r + pr * cr - pi * ci
                wim[pl.ds(base, 8), cs] = im + pr * ci + pi * cr
            return carry

        lax.fori_loop(0, nblk, blk, 0)
        wre[0:8, :] = wre[tm:tm + 8, :]
        wim[0:8, :] = wim[tm:tm + 8, :]
        sre_ref[...] = wre[8:8 + tm, :]
        sim_ref[...] = wim[8:8 + tm, :]

        for j in range(SUPER):
            yr = jnp.dot(wre[8:8 + tm, j * 512:(j + 1) * 512].astype(BF16), cre_ref[j], preferred_element_type=F32)
            yr += jnp.dot(wim[8:8 + tm, j * 512:(j + 1) * 512].astype(BF16), cim_ref[j], preferred_element_type=F32)
            yraw_ref[:, j * 128:(j + 1) * 128] = yr
        yraw = yraw_ref[...] + dv_ref[...] * xa
        yraw_ref[...] = yraw
        yg = _gelu(yraw)
        q = jnp.dot(yg.astype(BF16), wg_ref[...], preferred_element_type=F32) + bg_ref[...]
        sga, _ = _silu_and_grad(ga_ref[...])
        ya_ref[...] = (yg * jax.nn.sigmoid(q) * sga).astype(BF16)

    nt = T // tm
    r_in, r_out, r_shapes, r_sems, r_args = _rider_specs(rider)
    return pl.pallas_call(
        _ride(body, 10, 4, rider, lambda: pl.program_id(0) == 0, lambda: pl.program_id(0) == nt - 1,
              lambda: pl.program_id(0) == nt - 1),
        name="s5_fwd" + ("" if rider is None else "_ride"),
        grid=(nt,),
        in_specs=[
            pl.BlockSpec((tm, S5_W), lambda i: (i, 0)),
            pl.BlockSpec((tm, S5_W), lambda i: (i, 6)),
            _full((SUPER, 128, 512)), _full((SUPER, 128, 512)),
            _full((SUPER, 512, 128)), _full((SUPER, 512, 128)),
            _full((1, S5_W)), _full((S5_W, S5_W)), _full((1, S5_W)),
            _full((8, 8, W)),
        ] + r_in,
        out_specs=[
            pl.BlockSpec((tm, S5_W), lambda i: (i, 0)),
            pl.BlockSpec((tm, S5_W), lambda i: (i, 0)),
            pl.BlockSpec((tm, W), lambda i: (i, 0)),
            pl.BlockSpec((tm, W), lambda i: (i, 0)),
        ] + r_out,
        out_shape=[
            jax.ShapeDtypeStruct((T, S5_W), BF16),
            jax.ShapeDtypeStruct((T, S5_W), F32),
            jax.ShapeDtypeStruct((T, W), F32),
            jax.ShapeDtypeStruct((T, W), F32),
        ] + r_shapes,
        scratch_shapes=[pltpu.VMEM((tm + 8, W), F32), pltpu.VMEM((tm + 8, W), F32)] + r_sems,
        compiler_params=_cparams(1),
    )(z, z, p["b4re"], p["b4im"], p["c4re"], p["c4im"], p["dvec"], p["wglu"], p["bglu"], p["coef_f"], *r_args)


def _s5_bwd(dy, z, yraw, sre, sim, p, rider=None):
    T = z.shape[0]
    tm = min(256, T)
    nt = T // tm
    nblk = tm // 8
    W = STATE_W
    rev = lambda i: nt - 1 - i

    def body(dya_ref, xa_ref, ga_ref, yraw_ref, sre_ref, sim_ref, hre_ref, him_ref,
             bre_t_ref, bim_t_ref, cre_t_ref, cim_t_ref, dv_ref, wg_ref, wgt_ref, bg_ref, coef_ref,
             dz_ref, dbre_ref, dbim_ref, dcre_ref, dcim_ref, dd_ref, dwg_ref, dbg_ref, da_ref,
             wre, wim, dyr_ref, xa_stage, ga_stage, xa_sem, ga_sem):
        i = pl.program_id(0)
        xa_out = _ColumnWriter(xa_stage, xa_sem, dz_ref, 0, i, nt)
        ga_out = _ColumnWriter(ga_stage, ga_sem, dz_ref, 6 * 512, i, nt)
        dxa_ref, dga_ref = xa_out.slot(), ga_out.slot()

        @pl.when(i == 0)
        def _():
            wre[tm:tm + 8, :] = jnp.zeros((8, W), F32)
            wim[tm:tm + 8, :] = jnp.zeros((8, W), F32)
            for ref in (dbre_ref, dbim_ref, dcre_ref, dcim_ref, dd_ref, dwg_ref, dbg_ref, da_ref):
                ref[...] = jnp.zeros_like(ref)

        xa = xa_ref[...]
        dya = dya_ref[...]
        yg, dgelu = _gelu_and_grad(yraw_ref[...])
        ygb = yg.astype(BF16)
        q = jnp.dot(ygb, wg_ref[...], preferred_element_type=F32) + bg_ref[...]
        sq = jax.nn.sigmoid(q)
        sga, dsga = _silu_and_grad(ga_ref[...])
        dga_ref[...] = (dya * (yg * sq) * dsga).astype(BF16)
        dya0 = dya * sga
        dq = dya0 * yg * sq * (1.0 - sq)
        dqb = dq.astype(BF16)
        dyg = dya0 * sq + jnp.dot(dqb, wgt_ref[...], preferred_element_type=F32)
        dwg_ref[...] += _dot_tn(ygb, dqb)
        dbg_ref[...] += jnp.sum(dq, axis=0, keepdims=True)
        dyraw = dyg * dgelu
        dd_ref[...] += jnp.sum(dyraw * xa, axis=0, keepdims=True)
        dyr_ref[...] = dyraw.astype(BF16)

        for j in range(SUPER):
            dj = dyr_ref[:, j * 128:(j + 1) * 128]
            wre[0:tm, j * 512:(j + 1) * 512] = jnp.dot(dj, cre_t_ref[j], preferred_element_type=F32)
            wim[0:tm, j * 512:(j + 1) * 512] = jnp.dot(dj, cim_t_ref[j], preferred_element_type=F32)

        row0 = lax.broadcasted_iota(jnp.int32, (8, SCAN_COLS), 0) == 0
        head_on = (i < nt - 1).astype(F32)

        def one_block(base, first):
            for cc in range(W // SCAN_COLS):
                cs = pl.ds(cc * SCAN_COLS, SCAN_COLS)
                r, im = _scan_block(wre[pl.ds(base, 8), cs], wim[pl.ds(base, 8), cs], coef_ref, cs, True)
                cr = wre[pl.ds(base + 8, 1), cs]
                ci = wim[pl.ds(base + 8, 1), cs]
                pr = coef_ref[6, :, cs]
                pi = coef_ref[7, :, cs]
                r, im = r + pr * cr - pi * ci, im + pr * ci + pi * cr
                wre[pl.ds(base, 8), cs] = r
                wim[pl.ds(base, 8), cs] = im
                if first:
                    pre = hre_ref[7:8, cs] * head_on
                    pim = him_ref[7:8, cs] * head_on
                else:
                    pre = sre_ref[pl.ds(base - 1, 1), cs]
                    pim = sim_ref[pl.ds(base - 1, 1), cs]
                spr = jnp.where(row0, pre, pltpu.roll(sre_ref[pl.ds(base, 8), cs], 1, 0))
                spi = jnp.where(row0, pim, pltpu.roll(sim_ref[pl.ds(base, 8), cs], 1, 0))
                da_ref[0, :, cs] += r * spr + im * spi
                da_ref[1, :, cs] += im * spr - r * spi

        def blk(b, carry):
            one_block(pl.multiple_of((nblk - 1 - b) * 8, 8), False)
            return carry

        lax.fori_loop(0, nblk - 1, blk, 0)
        one_block(0, True)
        wre[tm:tm + 8, :] = wre[0:8, :]
        wim[tm:tm + 8, :] = wim[0:8, :]

        xab = xa.astype(BF16)
        for j in range(SUPER):
            cols = slice(j * 512, (j + 1) * 512)
            gre = wre[0:tm, cols].astype(BF16)
            gim = wim[0:tm, cols].astype(BF16)
            xj = xab[:, j * 128:(j + 1) * 128]
            dj = dyr_ref[:, j * 128:(j + 1) * 128]
            dbre_ref[j] += _dot_tn(xj, gre)
            dbim_ref[j] += _dot_tn(xj, gim)
            dcre_ref[j] += _dot_tn(sre_ref[:, cols], dj)
            dcim_ref[j] += _dot_tn(sim_ref[:, cols], dj)
            dxj = jnp.dot(gre, bre_t_ref[j], preferred_element_type=F32)
            dxj += jnp.dot(gim, bim_t_ref[j], preferred_element_type=F32)
            dxj += dyraw[:, j * 128:(j + 1) * 128] * dv_ref[:, j * 128:(j + 1) * 128]
            dxa_ref[:, j * 128:(j + 1) * 128] = dxj.astype(BF16)
        xa_out.send(rev(i) * tm)
        ga_out.send(rev(i) * tm)

    acc = lambda shape: _full(shape)
    hb = tm // 8
    r_in, r_out, r_shapes, r_sems, r_args = _rider_specs(rider)
    stages, stage_sems = _stage_scratch(tm, (S5_W, S5_W))
    return pl.pallas_call(
        _ride(body, 17, 9, rider, lambda: pl.program_id(0) == 0, lambda: pl.program_id(0) == nt - 1),
        name="s5_bwd" + ("" if rider is None else "_ride"),
        grid=(nt,),
        in_specs=[
            pl.BlockSpec((tm, S5_W), lambda i: (rev(i), 0)),
            pl.BlockSpec((tm, S5_W), lambda i: (rev(i), 0)),
            pl.BlockSpec((tm, S5_W), lambda i: (rev(i), 6)),
            pl.BlockSpec((tm, S5_W), lambda i: (rev(i), 0)),
            pl.BlockSpec((tm, W), lambda i: (rev(i), 0)),
            pl.BlockSpec((tm, W), lambda i: (rev(i), 0)),
            pl.BlockSpec((8, W), lambda i: (jnp.maximum(rev(i) * hb - 1, 0), 0)),
            pl.BlockSpec((8, W), lambda i: (jnp.maximum(rev(i) * hb - 1, 0), 0)),
            _full((SUPER, 512, 128)), _full((SUPER, 512, 128)),
            _full((SUPER, 128, 512)), _full((SUPER, 128, 512)),
            _full((1, S5_W)), _full((S5_W, S5_W)), _full((S5_W, S5_W)), _full((1, S5_W)),
            _full((8, 8, W)),
        ] + r_in,
        out_specs=[
            pl.BlockSpec(memory_space=pl.ANY),
            acc((SUPER, 128, 512)), acc((SUPER, 128, 512)),
            acc((SUPER, 512, 128)), acc((SUPER, 512, 128)),
            acc((1, S5_W)), acc((S5_W, S5_W)), acc((1, S5_W)), acc((2, 8, W)),
        ] + r_out,
        out_shape=[
            jax.ShapeDtypeStruct((T, IN_COLS), BF16),
            jax.ShapeDtypeStruct((SUPER, 128, 512), F32), jax.ShapeDtypeStruct((SUPER, 128, 512), F32),
            jax.ShapeDtypeStruct((SUPER, 512, 128), F32), jax.ShapeDtypeStruct((SUPER, 512, 128), F32),
            jax.ShapeDtypeStruct((1, S5_W), F32), jax.ShapeDtypeStruct((S5_W, S5_W), F32),
            jax.ShapeDtypeStruct((1, S5_W), F32), jax.ShapeDtypeStruct((2, 8, W), F32),
        ] + r_shapes,
        scratch_shapes=[pltpu.VMEM((tm + 8, W), F32), pltpu.VMEM((tm + 8, W), F32), pltpu.VMEM((tm, S5_W), BF16)]
        + stages + stage_sems + r_sems,
        compiler_params=_cparams(1),
    )(dy, z, z, yraw, sre, sim, sre, sim,
      p["b4re_t"], p["b4im_t"], p["c4re_t"], p["c4im_t"], p["dvec"], p["wglu"], p["wglu_t"], p["bglu"], p["coef_r"],
      *r_args)


def _ln_fwd(vf, lng, lnb):
    mu = jnp.mean(vf, axis=-1, keepdims=True)
    d = vf - mu
    rstd = lax.rsqrt(jnp.mean(d * d, axis=-1, keepdims=True) + LN_EPS)
    xh = d * rstd
    return xh, rstd, xh * lng + lnb


def _col_block(tm, b):
    return pl.BlockSpec((tm, 512), lambda i: (i, b))


def _ln_halves(vf0, vf1):
    mu = (jnp.sum(vf0, axis=-1, keepdims=True) + jnp.sum(vf1, axis=-1, keepdims=True)) * (1.0 / SGU_W)
    d0, d1 = vf0 - mu, vf1 - mu
    var = (jnp.sum(d0 * d0, axis=-1, keepdims=True) + jnp.sum(d1 * d1, axis=-1, keepdims=True)) * (1.0 / SGU_W)
    rstd = lax.rsqrt(var + LN_EPS)
    return d0 * rstd, d1 * rstd, rstd


def _sgu_fwd(z, ws, bsf, lng, lnb):
    T = z.shape[0]
    tm = min(512, T)

    def body(u0, u1, v0, v1, g0, g1, ws_ref, bs_ref, lng_ref, lnb_ref, yb_ref, vn_ref):
        for c in range(tm // CHUNK):
            rows = slice(c * CHUNK, (c + 1) * CHUNK)
            xh0, xh1, _ = _ln_halves(_gelu(v0[rows, :]), _gelu(v1[rows, :]))
            vn_ref[:, 0:512] = (xh0 * lng_ref[:, 0:512] + lnb_ref[:, 0:512]).astype(BF16)
            vn_ref[:, 512:1024] = (xh1 * lng_ref[:, 512:1024] + lnb_ref[:, 512:1024]).astype(BF16)
            for half, (u_ref, g_ref) in enumerate(((u0, g0), (u1, g1))):
                sg, _ = _silu_and_grad(g_ref[rows, :])
                m = _gelu(u_ref[rows, :]) * sg
                for hh in range(SGU_HEADS // 2):
                    h = half * (SGU_HEADS // 2) + hh
                    cols = slice(h * 128, (h + 1) * 128)
                    s = jnp.dot(ws_ref[h], vn_ref[:, cols], preferred_element_type=F32) + bs_ref[:, cols]
                    yb_ref[rows, cols] = (m[:, hh * 128:(hh + 1) * 128] * s).astype(BF16)

    return pl.pallas_call(
        body,
        name="sgu_fwd",
        grid=(T // tm,),
        in_specs=[_col_block(tm, b) for b in (1, 2, 3, 4, 7, 8)] + [
            _full((SGU_HEADS, CHUNK, CHUNK)), _full((CHUNK, SGU_W)), _full((1, SGU_W)), _full((1, SGU_W)),
        ],
        out_specs=pl.BlockSpec((tm, SGU_W), lambda i: (i, 0)),
        out_shape=jax.ShapeDtypeStruct((T, SGU_W), BF16),
        scratch_shapes=[pltpu.VMEM((CHUNK, SGU_W), BF16)],
        compiler_params=_cparams(1),
    )(z, z, z, z, z, z, ws, bsf, lng, lnb)


def _sgu_bwd(dy, z, dz, ws, ws_t, bsf, lng, lnb, rider=None):
    T = z.shape[0]
    tm = min(512, T)
    HH = SGU_HEADS // 2

    def body(u0, u1, v0, v1, g0, g1, dy0, dy1, ws_ref, wst_ref, bs_ref, lng_ref, lnb_ref, dz_in,
             dz_ref, dws_ref, dbs_ref, dlng_ref, dlnb_ref, vn_ref, dvn_ref, *stage):
        step = pl.program_id(0)
        outs = [_ColumnWriter(stage[k], stage[3 + k], dz_ref, col, step, T // tm)
                for k, col in enumerate((512, 1536, 3584))]
        du_ref, dv_ref, dgb_ref = (o.slot() for o in outs)

        @pl.when(step == 0)
        def _():
            for ref in (dws_ref, dbs_ref, dlng_ref, dlnb_ref):
                ref[...] = jnp.zeros_like(ref)

        for c in range(tm // CHUNK):
            rows = slice(c * CHUNK, (c + 1) * CHUNK)
            vf0, dgv0 = _gelu_and_grad(v0[rows, :])
            vf1, dgv1 = _gelu_and_grad(v1[rows, :])
            xh0, xh1, rstd = _ln_halves(vf0, vf1)
            vn_ref[:, 0:512] = (xh0 * lng_ref[:, 0:512] + lnb_ref[:, 0:512]).astype(BF16)
            vn_ref[:, 512:1024] = (xh1 * lng_ref[:, 512:1024] + lnb_ref[:, 512:1024]).astype(BF16)
            for half, (u_ref, g_ref, dy_ref) in enumerate(((u0, g0, dy0), (u1, g1, dy1))):
                ug, dgu = _gelu_and_grad(u_ref[rows, :])
                sg, dsg = _silu_and_grad(g_ref[rows, :])
                dyb = dy_ref[rows, :]
                dyb0 = dyb * sg
                ds_half = dyb0 * ug
                du_scale = dyb0 * dgu
                dg_scale = dyb * ug * dsg
                for hh in range(HH):
                    h = half * HH + hh
                    cols = slice(h * 128, (h + 1) * 128)
                    lc = slice(hh * 128, (hh + 1) * 128)
                    s = jnp.dot(ws_ref[h], vn_ref[:, cols], preferred_element_type=F32) + bs_ref[:, cols]
                    du_ref[rows, cols] = (du_scale[:, lc] * s).astype(BF16)
                    dgb_ref[rows, cols] = (dg_scale[:, lc] * s).astype(BF16)
                    ds = ds_half[:, lc]
                    dbs_ref[:, cols] += ds
                    dsb = ds.astype(BF16)
                    dws_ref[h] += _dot_nt(dsb, vn_ref[:, cols])
                    dvn_ref[:, cols] = jnp.dot(wst_ref[h], dsb, preferred_element_type=F32)
            dvn0 = dvn_ref[:, 0:512]
            dvn1 = dvn_ref[:, 512:1024]
            dlnb_ref[:, 0:512] += jnp.sum(dvn0, axis=0, keepdims=True)
            dlnb_ref[:, 512:1024] += jnp.sum(dvn1, axis=0, keepdims=True)
            dlng_ref[:, 0:512] += jnp.sum(dvn0 * xh0, axis=0, keepdims=True)
            dlng_ref[:, 512:1024] += jnp.sum(dvn1 * xh1, axis=0, keepdims=True)
            dxh0 = dvn0 * lng_ref[:, 0:512]
            dxh1 = dvn1 * lng_ref[:, 512:1024]
            m1 = (jnp.sum(dxh0, axis=-1, keepdims=True) + jnp.sum(dxh1, axis=-1, keepdims=True)) * (1.0 / SGU_W)
            m2 = (jnp.sum(dxh0 * xh0, axis=-1, keepdims=True) + jnp.sum(dxh1 * xh1, axis=-1, keepdims=True)) * (1.0 / SGU_W)
            dv_ref[rows, 0:512] = (rstd * (dxh0 - m1 - xh0 * m2) * dgv0).astype(BF16)
            dv_ref[rows, 512:1024] = (rstd * (dxh1 - m1 - xh1 * m2) * dgv1).astype(BF16)
        for o in outs:
            o.send(step * tm)

    anyspec = pl.BlockSpec(memory_space=pl.ANY)
    r_in, r_out, r_shapes, r_sems, r_args = _rider_specs(rider)
    stages, stage_sems = _stage_scratch(tm, (SGU_W, SGU_W, SGU_W))
    return pl.pallas_call(
        _ride(body, 14, 5, rider, lambda: pl.program_id(0) == 0, lambda: pl.program_id(0) == T // tm - 1),
        name="sgu_bwd" + ("" if rider is None else "_ride"),
        grid=(T // tm,),
        in_specs=[_col_block(tm, b) for b in (1, 2, 3, 4, 7, 8)] + [_col_block(tm, 1), _col_block(tm, 2)] + [
            _full((SGU_HEADS, CHUNK, CHUNK)), _full((SGU_HEADS, CHUNK, CHUNK)),
            _full((CHUNK, SGU_W)), _full((1, SGU_W)), _full((1, SGU_W)), anyspec,
        ] + r_in,
        out_specs=[anyspec,
                   _full((SGU_HEADS, CHUNK, CHUNK)), _full((CHUNK, SGU_W)), _full((1, SGU_W)), _full((1, SGU_W))] + r_out,
        input_output_aliases={13: 0},
        out_shape=[
            jax.ShapeDtypeStruct((T, IN_COLS), BF16),
            jax.ShapeDtypeStruct((SGU_HEADS, CHUNK, CHUNK), F32), jax.ShapeDtypeStruct((CHUNK, SGU_W), F32),
            jax.ShapeDtypeStruct((1, SGU_W), F32), jax.ShapeDtypeStruct((1, SGU_W), F32),
        ] + r_shapes,
        scratch_shapes=[pltpu.VMEM((CHUNK, SGU_W), BF16), pltpu.VMEM((CHUNK, SGU_W), F32)] + stages + stage_sems + r_sems,
        compiler_params=_cparams(1),
    )(z, z, z, z, z, z, dy, dy, ws, ws_t, bsf, lng, lnb, dz, *r_args)


def _pool_den(first_row, n):
    return (lax.broadcasted_iota(jnp.int32, (n, 1), 0) + first_row + 1).astype(F32)


def _pool_p(ext, xc, pos, tm):
    w2 = ext + pltpu.roll(ext, 1, 0)
    w4 = w2 + pltpu.roll(w2, 2, 0)
    w8 = w4 + pltpu.roll(w4, 4, 0)
    w16 = w8 + pltpu.roll(w8, 8, 0)
    out = []
    for g, (w, ws) in enumerate(zip(POOL_WINDOWS, (w2, w4, w8, w16))):
        cols = slice(g * 128, (g + 1) * 128)
        mean = ws[POOL_HALO:POOL_HALO + tm, cols] / jnp.minimum(pos, float(w))
        out.append(mean - xc[:, cols])
    return out


def _pool_fwd(z, wp, scale):
    T = z.shape[0]
    tm = min(512, T)
    hb = tm // POOL_HALO

    def body(xc_ref, hx_ref, gc_ref, wp_ref, sc_ref, yc_ref):
        i = pl.program_id(0)
        xc = xc_ref[...]
        halo = hx_ref[...] * (i > 0).astype(F32)
        ext = jnp.concatenate([halo, xc], axis=0)
        ps = _pool_p(ext, xc, _pool_den(i * tm, tm), tm)
        sg, _ = _silu_and_grad(gc_ref[...])
        for g in range(4):
            cols = slice(g * 128, (g + 1) * 128)
            pw = _dot(ps[g], wp_ref[g])
            yc_ref[:, cols] = (pw * sc_ref[:, cols] * sg[:, cols]).astype(BF16)

    return pl.pallas_call(
        body,
        name="pool_fwd",
        grid=(T // tm,),
        in_specs=[
            _col_block(tm, 5),
            pl.BlockSpec((POOL_HALO, 512), lambda i: (jnp.maximum(i * hb - 1, 0), 5)),
            _col_block(tm, 9),
            _full((4, 128, 128)), _full((1, POOL_W)),
        ],
        out_specs=pl.BlockSpec((tm, POOL_W), lambda i: (i, 0)),
        out_shape=jax.ShapeDtypeStruct((T, POOL_W), BF16),
        compiler_params=_cparams(1),
    )(z, z, z, wp, scale)


def _pool_bwd(dy, z, dz, wp, wp_t, scale):
    T = z.shape[0]
    tm = min(512, T)
    nt = T // tm
    hb = tm // POOL_HALO
    last_hb = T // POOL_HALO - 1
    L = tm + POOL_HALO

    def body(xc_ref, hx_ref, gc_ref, gn_ref, dyc_ref, dyn_ref, wp_ref, wpt_ref, sc_ref, dz_in,
             dz_ref, dwp_ref, dsc_ref, xc_stage, gc_stage, xc_sem, gc_sem):
        i = pl.program_id(0)
        xc_out = _ColumnWriter(xc_stage, xc_sem, dz_ref, 5 * 512, i, nt)
        gc_out = _ColumnWriter(gc_stage, gc_sem, dz_ref, 9 * 512, i, nt)
        dxc_ref, dgc_ref = xc_out.slot(), gc_out.slot()

        @pl.when(i == 0)
        def _():
            dwp_ref[...] = jnp.zeros_like(dwp_ref)
            dsc_ref[...] = jnp.zeros_like(dsc_ref)

        xc = xc_ref[...]
        halo = hx_ref[...] * (i > 0).astype(F32)
        pos = _pool_den(i * tm, tm)
        ps = _pool_p(jnp.concatenate([halo, xc], axis=0), xc, pos, tm)
        sg, dsg = _silu_and_grad(gc_ref[...])
        dyc = dyc_ref[...]
        dyc0 = dyc * sg
        dpw = dyc0 * sc_ref[...]
        sgn, _ = _silu_and_grad(gn_ref[...])
        dpwn = dyn_ref[...] * sgn * sc_ref[...] * (i < nt - 1).astype(F32)
        posn = _pool_den((i + 1) * tm, POOL_HALO)
        dps, qs = [], []
        for g, w in enumerate(POOL_WINDOWS):
            cols = slice(g * 128, (g + 1) * 128)
            pw = _dot(ps[g], wp_ref[g])
            dgc_ref[:, cols] = (dyc[:, cols] * pw * sc_ref[:, cols] * dsg[:, cols]).astype(BF16)
            dsc_ref[:, cols] += jnp.sum(dyc0[:, cols] * pw, axis=0, keepdims=True)
            dwp_ref[g] += _dot_tn(ps[g], dpw[:, cols])
            dp = _dot(dpw[:, cols], wpt_ref[g])
            dpn = _dot(dpwn[:, cols], wpt_ref[g])
            dps.append(dp)
            qs.append(jnp.concatenate([dp / jnp.minimum(pos, float(w)), dpn / jnp.minimum(posn, float(w))], axis=0))
        ext = jnp.concatenate(qs, axis=1)
        f2 = ext + pltpu.roll(ext, L - 1, 0)
        f4 = f2 + pltpu.roll(f2, L - 2, 0)
        f8 = f4 + pltpu.roll(f4, L - 4, 0)
        f16 = f8 + pltpu.roll(f8, L - 8, 0)
        for g, f in enumerate((f2, f4, f8, f16)):
            cols = slice(g * 128, (g + 1) * 128)
            dxc_ref[:, cols] = (f[0:tm, cols] - dps[g]).astype(BF16)
        xc_out.send(i * tm)
        gc_out.send(i * tm)

    nxt = lambda i: jnp.minimum((i + 1) * hb, last_hb)
    anyspec = pl.BlockSpec(memory_space=pl.ANY)
    stages, stage_sems = _stage_scratch(tm, (POOL_W, POOL_W))
    return pl.pallas_call(
        body,
        name="pool_bwd",
        grid=(nt,),
        in_specs=[
            _col_block(tm, 5),
            pl.BlockSpec((POOL_HALO, 512), lambda i: (jnp.maximum(i * hb - 1, 0), 5)),
            _col_block(tm, 9),
            pl.BlockSpec((POOL_HALO, 512), lambda i: (nxt(i), 9)),
            _col_block(tm, 3),
            pl.BlockSpec((POOL_HALO, 512), lambda i: (nxt(i), 3)),
            _full((4, 128, 128)), _full((4, 128, 128)), _full((1, POOL_W)), anyspec,
        ],
        out_specs=[anyspec, _full((4, 128, 128)), _full((1, POOL_W))],
        input_output_aliases={9: 0},
        out_shape=[
            jax.ShapeDtypeStruct((T, IN_COLS), BF16),
            jax.ShapeDtypeStruct((4, 128, 128), F32), jax.ShapeDtypeStruct((1, POOL_W), F32),
        ],
        scratch_shapes=stages + stage_sems,
        compiler_params=_cparams(1),
    )(z, z, z, z, dy, dy, wp, wp_t, scale, dz)


def _row_tile(rows, cols):
    tr = 8
    while tr * 2 * cols * 4 <= 2 * 1024 * 1024 and rows % (tr * 2) == 0:
        tr *= 2
    return tr


def _add_own_half(part, recv, cidx):
    _, _, R2, C = part.shape
    tr = _row_tile(R2, C)

    def body(c_ref, a_ref, r_ref, o_ref):
        o_ref[...] = (a_ref[...] + r_ref[...]).astype(BF16)

    return pl.pallas_call(
        body,
        name="add_own_half",
        grid_spec=pltpu.PrefetchScalarGridSpec(
            num_scalar_prefetch=1,
            grid=(N_CHIPS, R2 // tr),
            in_specs=[
                pl.BlockSpec((None, None, tr, C), lambda j, i, c: (j, c[0], i, 0)),
                pl.BlockSpec((None, tr, C), lambda j, i, c: (j, i, 0)),
            ],
            out_specs=pl.BlockSpec((None, tr, C), lambda j, i, c: (j, i, 0)),
        ),
        out_shape=jax.ShapeDtypeStruct((N_CHIPS, R2, C), BF16),
        compiler_params=_cparams(2),
    )(cidx, part, recv)


def _add2(a, b):
    R, C = a.shape
    tr = _row_tile(R, C)

    def body(a_ref, b_ref, o_ref):
        o_ref[...] = a_ref[...] + b_ref[...]

    spec = pl.BlockSpec((tr, C), lambda i: (i, 0))
    return pl.pallas_call(
        body, name="add2", grid=(R // tr,), in_specs=[spec, spec], out_specs=spec,
        out_shape=jax.ShapeDtypeStruct((R, C), F32), compiler_params=_cparams(1),
    )(a, b)


def _sum_chips(parts):
    _, R, C = parts.shape
    tr = _row_tile(R, N_CHIPS * C)

    def body(p_ref, o_ref):
        p = [p_ref[j].astype(F32) for j in range(N_CHIPS)]
        o_ref[...] = ((p[0] + p[1]) + p[2]) + p[3]

    return pl.pallas_call(
        body, name="sum_chips", grid=(R // tr,),
        in_specs=[pl.BlockSpec((N_CHIPS, tr, C), lambda i: (0, i, 0))],
        out_specs=pl.BlockSpec((tr, C), lambda i: (i, 0)),
        out_shape=jax.ShapeDtypeStruct((R, C), F32), compiler_params=_cparams(1),
    )(parts)


def _adamw_math(w, g, m, v):
    m = ADAM_B1 * m + (1.0 - ADAM_B1) * g
    v = ADAM_B2 * v + (1.0 - ADAM_B2) * (g * g)
    m_hat = m / (1.0 - ADAM_B1 ** ADAM_STEP)
    v_hat = v / (1.0 - ADAM_B2 ** ADAM_STEP)
    delta = -ADAM_LR * (m_hat / (jnp.sqrt(v_hat) + ADAM_EPS) + ADAM_WD * w)
    return delta, m, v


def _adamw(w, g, m, v):
    R, C = w.shape
    tr = _row_tile(R, C)

    def body(w_ref, g_ref, m_ref, v_ref, d_ref, mo_ref, vo_ref):
        d_ref[...], mo_ref[...], vo_ref[...] = _adamw_math(w_ref[...], g_ref[...], m_ref[...], v_ref[...])

    spec = pl.BlockSpec((tr, C), lambda i: (i, 0))
    shp = jax.ShapeDtypeStruct((R, C), F32)
    return pl.pallas_call(
        body, name="adamw", grid=(R // tr,), in_specs=[spec] * 4, out_specs=[spec] * 3,
        out_shape=[shp] * 3, compiler_params=_cparams(1),
    )(w, g, m, v)


def _adamw_halves(w, mine, theirs, m, v, cidx, rider=None):
    _, _, R2, C = w.shape
    tr = _row_tile(R2, C)
    nr = R2 // tr

    def body(c_ref, w_ref, a0_ref, b0_ref, a1_ref, b1_ref, m_ref, v_ref, g_ref, d_ref, mo_ref, vo_ref):
        own = pl.program_id(1) == c_ref[0]
        g0 = jnp.where(own, a0_ref[...], b0_ref[...])
        g1 = jnp.where(own, a1_ref[...], b1_ref[...])
        g = jnp.where(pl.program_id(0) == 0, g0, g1)
        g_ref[...] = g
        d_ref[...], mo_ref[...], vo_ref[...] = _adamw_math(w_ref[...], g, m_ref[...], v_ref[...])

    full = pl.BlockSpec((None, None, tr, C), lambda l, h, i, c: (l, h, i, 0))

    def pick(layer, mine_side):
        def index(l, h, i, c):
            used = (l == layer) & ((h == c[0]) == mine_side)
            return (jnp.where(used, i, 0), 0)
        return pl.BlockSpec((tr, C), index)

    shp = jax.ShapeDtypeStruct(w.shape, F32)
    r_in, r_out, r_shapes, r_sems, r_args = _rider_specs(rider)
    last = lambda: (pl.program_id(0) == 1) & (pl.program_id(1) == 1) & (pl.program_id(2) == nr - 1)
    first = lambda: (pl.program_id(0) == 0) & (pl.program_id(1) == 0) & (pl.program_id(2) == 0)
    return pl.pallas_call(
        _ride(body, 8, 4, rider, first, last),
        name="adamw_halves" + ("" if rider is None else "_ride"),
        grid_spec=pltpu.PrefetchScalarGridSpec(
            num_scalar_prefetch=1, grid=(2, 2, nr),
            in_specs=[full, pick(0, True), pick(0, False), pick(1, True), pick(1, False), full, full] + r_in,
            out_specs=[full] * 4 + r_out,
            scratch_shapes=r_sems,
        ),
        out_shape=[shp] * 4 + r_shapes,
        compiler_params=_cparams(3),
    )(cidx, w, mine[0], theirs[0], mine[1], theirs[1], m, v, *r_args)


_ANY = pl.BlockSpec(memory_space=pl.ANY)


def _mesh_pos():
    return lax.axis_index("x"), lax.axis_index("y"), lax.axis_index("c")


def _other_chips(x, y):
    return [(2 * x + (1 - y), x, 1 - y), (2 * (1 - x) + y, 1 - x, y), (2 * (1 - x) + (1 - y), 1 - x, 1 - y)]


def _gathered_shapes(shards):
    return [jax.ShapeDtypeStruct((2, N_CHIPS) + s.shape[1:], s.dtype) for s in shards]


def _gather_sems(n):
    return [pltpu.SemaphoreType.DMA((2 * n,)), pltpu.SemaphoreType.DMA((6 * n,)), pltpu.SemaphoreType.DMA((6 * n,))]


def _gather_steps(ins, outs, lsem, ssem, rsem):
    n = len(ins)
    x, y, c = _mesh_pos()
    me = 2 * x + y
    sib = (x, y, 1 - c)
    chips = _other_chips(x, y)

    def ici(k, d):
        return pltpu.make_async_remote_copy(
            ins[k].at[c], outs[k].at[c, me], ssem.at[6 * k + d], rsem.at[6 * k + d],
            device_id=(chips[d][1], chips[d][2], c), device_id_type=MESH_ID)

    def landed(k, d):
        return pltpu.make_async_remote_copy(
            ins[k].at[c], outs[k].at[c, chips[d][0]], ssem.at[6 * k + d], rsem.at[6 * k + d],
            device_id=sib, device_id_type=MESH_ID)

    def fwd(k, d, half):
        return pltpu.make_async_remote_copy(
            outs[k].at[half, chips[d][0]], outs[k].at[half, chips[d][0]], ssem.at[6 * k + 3 + d],
            rsem.at[6 * k + 3 + d], device_id=sib, device_id_type=MESH_ID)

    def local(k, h):
        return pltpu.make_async_copy(ins[k].at[h], outs[k].at[h, me], lsem.at[2 * k + h])

    def start():
        for k in range(n):
            for h in range(2):
                local(k, h).start()
            for d in range(3):
                ici(k, d).start()

    def mid():
        for d in range(3):
            for k in range(n):
                landed(k, d).wait_recv()
                fwd(k, d, c).start()

    def end():
        for d in range(3):
            for k in range(n):
                fwd(k, d, 1 - c).wait_recv()
        for k in range(n):
            for d in range(3):
                ici(k, d).wait_send()
                fwd(k, d, c).wait_send()
            for h in range(2):
                local(k, h).wait()

    return start, mid, end


def _gather_rider(shards):
    return _Rider(shards, _gathered_shapes(shards), _gather_sems(len(shards)), _gather_steps)


def _pair_rider(arrs, other_half):
    n = len(arrs)

    def steps(ins, outs, ssem, rsem):
        x, y, c = _mesh_pos()

        def copy(k):
            return pltpu.make_async_remote_copy(ins[k].at[:, 1 - c] if other_half else ins[k], outs[k], ssem.at[k],
                                                rsem.at[k], device_id=(x, y, 1 - c), device_id_type=MESH_ID)

        def start():
            for k in range(n):
                copy(k).start()

        def end():
            for k in range(n):
                copy(k).wait()

        return start, end

    shapes = [jax.ShapeDtypeStruct(a.shape[:1] + a.shape[2:] if other_half else a.shape, a.dtype) for a in arrs]
    return _Rider(arrs, shapes, [pltpu.SemaphoreType.DMA((n,)), pltpu.SemaphoreType.DMA((n,))], steps)


def _chip_rider(arrs, broadcast):
    n = len(arrs)

    def steps(ins, outs, lsem, ssem, rsem):
        x, y, c = _mesh_pos()
        me = 2 * x + y

        def copies():
            cps = [pltpu.make_async_copy(ins[k] if broadcast else ins[k].at[me], outs[k].at[me], lsem.at[k])
                   for k in range(n)]
            for k in range(n):
                for d, (j, tx, ty) in enumerate(_other_chips(x, y)):
                    cps.append(pltpu.make_async_remote_copy(
                        ins[k] if broadcast else ins[k].at[j], outs[k].at[me], ssem.at[3 * k + d], rsem.at[3 * k + d],
                        device_id=(tx, ty, c), device_id_type=MESH_ID))
            return cps

        def start():
            for cp in copies():
                cp.start()

        def end():
            for cp in copies():
                cp.wait()

        return start, end

    shapes = [jax.ShapeDtypeStruct(((N_CHIPS,) + a.shape) if broadcast else a.shape, a.dtype) for a in arrs]
    sems = [pltpu.SemaphoreType.DMA((n,)), pltpu.SemaphoreType.DMA((3 * n,)), pltpu.SemaphoreType.DMA((3 * n,))]
    return _Rider(arrs, shapes, sems, steps)


def _run_rider(name, rider):
    n, m = len(rider.arrs), len(rider.out_shapes)

    def body(*refs):
        for step in rider.steps(refs[:n], refs[n:n + m], *refs[n + m:]):
            step()

    return pl.pallas_call(
        body, name=name, in_specs=[_ANY] * n, out_specs=[_ANY] * m, out_shape=rider.out_shapes,
        scratch_shapes=rider.sems,
    )(*rider.arrs)


SMALL = ("norm_g", "lam_re", "lam_im", "b_re", "b_im", "c_re", "c_im", "d_skip", "log_dt", "b_glu", "ln_g", "ln_b",
         "w_s", "b_s", "w_pool", "pool_scale", "final_g")
BIG = ("w_in", "w_glu", "w_out")
WEIGHTS = ("norm_g", "w_in", "lam_re", "lam_im", "b_re", "b_im", "c_re", "c_im", "d_skip", "log_dt", "w_glu", "b_glu",
           "ln_g", "ln_b", "w_s", "b_s", "w_pool", "pool_scale", "w_out", "final_g")
PACK_UNIT = 8 * 128
PACK_ROWS = 1024


def _pack(arrs):
    parts, total = [], 0
    for a in arrs:
        f = a.reshape(-1).astype(F32)
        pad = (-f.shape[0]) % PACK_UNIT
        parts.append(jnp.pad(f, (0, pad)) if pad else f)
        total += f.shape[0] + pad
    tail = (-total) % (PACK_ROWS * 128)
    if tail:
        parts.append(jnp.zeros((tail,), F32))
    return jnp.concatenate(parts).reshape(-1, 128)


def _unpack(buf, like):
    flat = buf.reshape(-1)
    out, off = [], 0
    for a in like:
        n = math.prod(a.shape)
        out.append(flat[off:off + n].reshape(a.shape))
        off += n + ((-n) % PACK_UNIT)
    return out


def _layer_params(l, wt, g_glu):
    a_re, a_im, bb_re, bb_im = _s5_prep(wt["lam_re"][l], wt["lam_im"][l], wt["b_re"][l], wt["b_im"][l], wt["log_dt"][l])
    b4re, b4im = _block_diag_in(bb_re), _block_diag_in(bb_im)
    c4re, c4im = _block_diag_out(wt["c_re"][l]), _block_diag_out(-wt["c_im"][l])
    tr = lambda m: jnp.swapaxes(m, 1, 2).astype(BF16)
    causal = jnp.tril(jnp.ones((CHUNK, CHUNK), dtype=bool))
    ws = jnp.where(causal[None], wt["w_s"][l], 0.0)
    wglu = g_glu[l].reshape(S5_W, S5_W)
    return dict(
        b4re=b4re.astype(BF16), b4im=b4im.astype(BF16), c4re=c4re.astype(BF16), c4im=c4im.astype(BF16),
        b4re_t=tr(b4re), b4im_t=tr(b4im), c4re_t=tr(c4re), c4im_t=tr(c4im),
        dvec=wt["d_skip"][l].reshape(1, S5_W), wglu=wglu, wglu_t=wglu.T, bglu=wt["b_glu"][l].reshape(1, S5_W),
        coef_f=_scan_coefs(a_re, a_im, False), coef_r=_scan_coefs(a_re, a_im, True),
        ws=ws.astype(BF16), ws_t=tr(ws),
        bsf=jnp.broadcast_to(wt["b_s"][l][:, None, :], (SGU_HEADS, CHUNK, CHUNK)).transpose(2, 0, 1).reshape(CHUNK, SGU_W),
        lng=wt["ln_g"][l].reshape(1, SGU_W), lnb=wt["ln_b"][l].reshape(1, SGU_W),
        wp=wt["w_pool"][l].astype(BF16), wp_t=tr(wt["w_pool"][l]), scale=wt["pool_scale"][l].reshape(1, POOL_W),
        norm_g=wt["norm_g"][l].reshape(1, D_MODEL),
    )


def _local_step(x0, tgt, wt, g_in0, rest, rest_gathered, cidx=None, order=None):
    dist = cidx is not None
    xs, saved, params = [x0], [], []
    for l in range(DEPTH):
        norm_g = wt["norm_g"][l].reshape(1, D_MODEL)
        s5_rider = out_rider = None
        if l == 0 and not rest_gathered:
            (w_in1a, w_in1b), w_glu_b, w_out_b = rest
            h = _rms_h(xs[-1], norm_g)
            z, g_in0, g_glu, g_in1a = _inproj_first(h, g_in0, order, [w_glu_b, w_in1a])
            s5_rider, out_rider = _gather_rider([w_out_b]), _gather_rider([w_in1b])
        elif l == 0:
            g_in1, g_glu, g_out = rest
            z, h = _inproj(xs[-1], norm_g, [g_in0])
        else:
            z, h = _inproj(xs[-1], norm_g, g_in1 if isinstance(g_in1, list) else [g_in1])
        p = _layer_params(l, wt, g_glu)
        params.append(p)
        ya, yraw, sre, sim, *gathered = _s5_fwd(z, p, s5_rider)
        if gathered:
            (g_out,) = gathered
        yb = _sgu_fwd(z, p["ws"], p["bsf"], p["lng"], p["lnb"])
        yc = _pool_fwd(z, p["wp"], p["scale"])
        w_out = g_out[l].reshape(D_MODEL, D_MODEL)
        if l < DEPTH - 1:
            xn, y, *gathered = _outproj(ya, yb, yc, w_out, xs[-1], out_rider)
            if gathered:
                g_in1 = [g_in1a, gathered[0]]
            xs.append(xn)
        else:
            dx, loss, dfg, y = _outproj_loss(ya, yb, yc, w_out, xs[-1], wt["final_g"].reshape(1, D_MODEL), tgt)
        saved.append((z, h, yraw, sre, sim, y))
    g_in = ([g_in0], g_in1 if isinstance(g_in1, list) else [g_in1])

    gr = {k: [None] * DEPTH for k in WEIGHTS if k != "final_g"}
    mine, theirs, chip_sum = [None] * DEPTH, [None] * DEPTH, None
    halves = lambda a, rows: a.reshape(N_CHIPS, 2, rows // 2, a.shape[-1])
    for l in reversed(range(DEPTH)):
        p = params[l]
        z, h, yraw, sre, sim, y = saved[l]
        w_out = g_out[l].reshape(D_MODEL, D_MODEL)
        dy = _outproj_bwd_dy(dx, w_out)
        gr["w_out"][l] = _outproj_bwd_dw(y, dx)
        ride_c = _chip_rider(chip_sum, False) if dist and l == 0 else None
        dz, dbre, dbim, dcre, dcim, dd, dwg, dbg, da, *landed = _s5_bwd(dy, z, yraw, sre, sim, p, ride_c)
        if ride_c:
            mine[1] = [_sum_chips(r) for r in landed]
        ride_e = _pair_rider(mine[1], False) if dist and l == 0 else None
        dz, dws, dbsf, dlng, dlnb, *got = _sgu_bwd(dy, z, dz, p["ws"], p["ws_t"], p["bsf"], p["lng"], p["lnb"], ride_e)
        if ride_e:
            theirs[1] = got
        dz, dwp, dsc = _pool_bwd(dy, z, dz, p["wp"], p["wp_t"], p["scale"])
        gr["w_in"][l] = _inproj_bwd_dw(h, dz)
        if not dist:
            dx, dng = _inproj_bwd_dx(dz, g_in[l], xs[l], p["norm_g"], dx)
        else:
            part = [halves(gr["w_in"][l], D_MODEL), halves(dwg, S5_W // N_CHIPS),
                    halves(gr["w_out"][l], D_MODEL // N_CHIPS)]
            ride_a = _pair_rider(part, True)
            if l == 1:
                dx, dng, *from_sib = _inproj_bwd_dx(dz, g_in[l], xs[l], p["norm_g"], dx, ride_a)
                chip_sum = [_add_own_half(a, r, cidx) for a, r in zip(part, from_sib)]
            else:
                nt = x0.shape[0] // _dx_tile(x0.shape[0])
                n_top = max(nt // 4, 1)
                dx_top, dng_top, *from_sib = _inproj_bwd_dx(dz, g_in[l], xs[l], p["norm_g"], dx, ride_a,
                                                            tiles=(0, n_top))
                chip_sum0 = [_add_own_half(a, r, cidx) for a, r in zip(part, from_sib)]
                dx, dng_rest, *landed = _inproj_bwd_dx(dz, g_in[l], xs[l], p["norm_g"], dx,
                                                       _chip_rider(chip_sum0, False), tiles=(n_top, nt - n_top),
                                                       prev=dx_top)
                dng = dng_top + dng_rest
                mine[0] = [_sum_chips(r) for r in landed]

        raw = (wt["lam_re"][l], wt["lam_im"][l], wt["b_re"][l], wt["b_im"][l], wt["log_dt"][l])
        _, vjp = jax.vjp(_s5_prep, *raw)
        da = jnp.sum(da, axis=1)
        cot = (da[0].reshape(S5_GROUPS, S5_STATE), da[1].reshape(S5_GROUPS, S5_STATE),
               _block_diag_in_grad(dbre), _block_diag_in_grad(dbim))
        gr["lam_re"][l], gr["lam_im"][l], gr["b_re"][l], gr["b_im"][l], gr["log_dt"][l] = vjp(cot)
        gr["c_re"][l] = _block_diag_out_grad(dcre)
        gr["c_im"][l] = -_block_diag_out_grad(dcim)
        gr["d_skip"][l] = dd.reshape(S5_GROUPS, S5_CH)
        gr["w_glu"][l] = dwg
        gr["b_glu"][l] = dbg.reshape(S5_W)
        causal = jnp.tril(jnp.ones((CHUNK, CHUNK), dtype=bool))
        gr["w_s"][l] = jnp.where(causal[None], dws, 0.0)
        gr["b_s"][l] = dbsf.reshape(CHUNK, SGU_HEADS, CHUNK).sum(-1).T
        gr["ln_g"][l] = dlng.reshape(SGU_W)
        gr["ln_b"][l] = dlnb.reshape(SGU_W)
        gr["w_pool"][l] = dwp
        gr["pool_scale"][l] = dsc.reshape(POOL_W)
        gr["norm_g"][l] = dng.reshape(D_MODEL)

    grads = {k: (v if k in BIG else jnp.stack(v)) for k, v in gr.items()}
    grads["final_g"] = dfg.reshape(D_MODEL)
    if dist:
        for i, k in enumerate(BIG):
            grads[k] = ([mine[l][i] for l in range(DEPTH)], [None, theirs[1][i]])
    return loss, dx, grads


def kernel(x, norm_g, w_in, lam_re, lam_im, b_re, b_im, c_re, c_im, d_skip, log_dt, w_glu, b_glu, ln_g, ln_b, w_s, b_s, w_pool, pool_scale, w_out, final_g, loss_target, m_norm_g, m_w_in, m_lam_re, m_lam_im, m_b_re, m_b_im, m_c_re, m_c_im, m_d_skip, m_log_dt, m_w_glu, m_b_glu, m_ln_g, m_ln_b, m_w_s, m_b_s, m_w_pool, m_pool_scale, m_w_out, m_final_g, v_norm_g, v_w_in, v_lam_re, v_lam_im, v_b_re, v_b_im, v_c_re, v_c_im, v_d_skip, v_log_dt, v_w_glu, v_b_glu, v_ln_g, v_ln_b, v_w_s, v_b_s, v_w_pool, v_pool_scale, v_w_out, v_final_g):
    wt = dict(norm_g=norm_g, w_in=w_in, lam_re=lam_re, lam_im=lam_im, b_re=b_re, b_im=b_im, c_re=c_re, c_im=c_im,
              d_skip=d_skip, log_dt=log_dt, w_glu=w_glu, b_glu=b_glu, ln_g=ln_g, ln_b=ln_b, w_s=w_s, b_s=b_s,
              w_pool=w_pool, pool_scale=pool_scale, w_out=w_out, final_g=final_g)
    mom = dict(norm_g=m_norm_g, w_in=m_w_in, lam_re=m_lam_re, lam_im=m_lam_im, b_re=m_b_re, b_im=m_b_im, c_re=m_c_re,
               c_im=m_c_im, d_skip=m_d_skip, log_dt=m_log_dt, w_glu=m_w_glu, b_glu=m_b_glu, ln_g=m_ln_g, ln_b=m_ln_b,
               w_s=m_w_s, b_s=m_b_s, w_pool=m_w_pool, pool_scale=m_pool_scale, w_out=m_w_out, final_g=m_final_g)
    vel = dict(norm_g=v_norm_g, w_in=v_w_in, lam_re=v_lam_re, lam_im=v_lam_im, b_re=v_b_re, b_im=v_b_im, c_re=v_c_re,
               c_im=v_c_im, d_skip=v_d_skip, log_dt=v_log_dt, w_glu=v_w_glu, b_glu=v_b_glu, ln_g=v_ln_g, ln_b=v_ln_b,
               w_s=v_w_s, b_s=v_b_s, w_pool=v_w_pool, pool_scale=v_pool_scale, w_out=v_w_out, final_g=v_final_g)
    T = x.shape[1]
    cidx = lax.axis_index("c").astype(jnp.int32).reshape(1)

    w_in_b = w_in.astype(BF16)
    w0 = w_in_b[0].reshape(2, HALF_D, SHARD_COLS)
    w1 = w_in_b[1].reshape(2, 2, HALF_D // 2, SHARD_COLS)
    rest = ((w1[:, 0], w1[:, 1]), w_glu.astype(BF16), w_out.astype(BF16))
    mx, my = lax.axis_index("x"), lax.axis_index("y")
    order = jnp.stack([2 * mx + my] + [j for j, _, _ in _other_chips(mx, my)]).astype(jnp.int32)
    loss, grad_x, grads = _local_step(x.reshape(T, D_MODEL), loss_target.reshape(T, D_MODEL), wt, w0, rest, False,
                                      cidx, order)

    packed = _pack([grads[k] for k in SMALL] + [loss[0, 0:1]])
    sib_packed, *theirs0 = _run_rider("pair_exchange", _pair_rider([packed] + [grads[k][0][0] for k in BIG], False))
    for k, t in zip(BIG, theirs0):
        grads[k][1][0] = t
    chip_packed = _add2(packed, sib_packed)
    half_rows = chip_packed.shape[0] // 2
    my_half = lax.dynamic_index_in_dim(chip_packed.reshape(2, half_rows, 128), cidx[0], 0, keepdims=False)
    small_ride = _chip_rider([my_half], True)

    out_g, out_d, out_m, out_v = {}, {}, {}, {}
    all_half = None
    for k in BIG:
        shape = wt[k].shape
        quad = lambda t: t.reshape(2, 2, shape[1] // 2, shape[2])
        g, d, m, v, *landed = _adamw_halves(quad(wt[k]), grads[k][0], grads[k][1], quad(mom[k]), quad(vel[k]), cidx,
                                            small_ride if k == BIG[0] else None)
        if landed:
            (all_half,) = landed
        out_g[k], out_d[k], out_m[k], out_v[k] = (t.reshape(shape) for t in (g, d, m, v))

    mine_half = _sum_chips(all_half)
    (their_half,) = _run_rider("small_result_exchange", _pair_rider([mine_half], False))
    total = jnp.where(cidx[0] == 0, jnp.concatenate([mine_half, their_half]), jnp.concatenate([their_half, mine_half]))
    like = [wt[k] for k in SMALL]
    small_g = _unpack(total, like + [loss[0, 0:1]])
    loss_out = small_g[-1].reshape(())
    w_p, m_p, v_p = _pack(like), _pack([mom[k] for k in SMALL]), _pack([vel[k] for k in SMALL])
    d_p, mo_p, vo_p = _adamw(w_p, total, m_p, v_p)
    for k, g, d, m, v in zip(SMALL, small_g[:-1], _unpack(d_p, like), _unpack(mo_p, like), _unpack(vo_p, like)):
        out_g[k], out_d[k], out_m[k], out_v[k] = g, d, m, v

    return (loss_out, grad_x.reshape(x.shape), *[out_g[k] for k in WEIGHTS], *[out_d[k] for k in WEIGHTS],
            *[out_m[k] for k in WEIGHTS], *[out_v[k] for k in WEIGHTS])
```

```python
import functools
import math

import jax
import jax.numpy as jnp
from jax import lax
from jax.experimental import pallas as pl
from jax.experimental.pallas import tpu as pltpu

F32 = jnp.float32
BF16 = jnp.bfloat16

D_MODEL = 2048
DEPTH = 2
S5_W = 512
SGU_W = 1024
POOL_W = 512
IN_COLS = 5120
N_CHIPS = 4
SHARD_COLS = IN_COLS // N_CHIPS
S5_GROUPS = 32
S5_STATE = 64
S5_CH = 16
STATE_W = S5_GROUPS * S5_STATE
SUPER = 4
CHUNK = 128
SGU_HEADS = 8
POOL_WINDOWS = (2, 4, 8, 16)
POOL_HALO = 16
RMS_EPS = 1e-6
LN_EPS = 1e-5
SCAN_COLS = 512

ADAM_LR = 0.001
ADAM_B1 = 0.9
ADAM_B2 = 0.999
ADAM_EPS = 1e-08
ADAM_WD = 0.01
ADAM_STEP = 10

VMEM_LIMIT = 56 * 1024 * 1024
MESH_ID = pl.DeviceIdType.MESH

_GELU_K0 = math.sqrt(2.0 / math.pi)
_GELU_K1 = 0.044715


def _cparams(n_axes):
    return pltpu.CompilerParams(dimension_semantics=("arbitrary",) * n_axes, vmem_limit_bytes=VMEM_LIMIT)


def _gelu(x):
    t = jnp.tanh(_GELU_K0 * (x + _GELU_K1 * (x * x * x)))
    return 0.5 * x * (1.0 + t)


def _gelu_and_grad(x):
    x2 = x * x
    t = jnp.tanh(_GELU_K0 * (x + _GELU_K1 * (x * x2)))
    g = 0.5 * x * (1.0 + t)
    dg = 0.5 * (1.0 + t) + 0.5 * x * (1.0 - t * t) * (_GELU_K0 * (1.0 + 3.0 * _GELU_K1 * x2))
    return g, dg


def _silu_and_grad(x):
    s = jax.nn.sigmoid(x)
    return x * s, s * (1.0 + x * (1.0 - s))


def _dot(a, b):
    return jnp.dot(a.astype(BF16), b.astype(BF16), preferred_element_type=F32)


def _dot_nt(a, b):
    return lax.dot_general(a.astype(BF16), b.astype(BF16), (((1,), (1,)), ((), ())), preferred_element_type=F32)


def _dot_tn(a, b):
    return lax.dot_general(a.astype(BF16), b.astype(BF16), (((0,), (0,)), ((), ())), preferred_element_type=F32)


def _full(shape):
    nd = len(shape)
    return pl.BlockSpec(shape, lambda *_: (0,) * nd)


class _Rider:
    def __init__(self, arrs, out_shapes, sems, steps):
        self.arrs, self.out_shapes, self.sems, self.steps = list(arrs), list(out_shapes), list(sems), steps


def _ride(body, n_in, n_out, rider, first, last, middle=None):
    if rider is None:
        return body
    ri, ro, ns = len(rider.arrs), len(rider.out_shapes), len(rider.sems)

    def wrapped(*refs):
        o0 = n_in + ri
        steps = rider.steps(refs[n_in:o0], refs[o0 + n_out:o0 + n_out + ro], *refs[len(refs) - ns:])
        pl.when(first())(steps[0])
        if len(steps) == 3:
            pl.when(middle())(steps[1])
        body(*refs[:n_in], *refs[o0:o0 + n_out], *refs[o0 + n_out + ro:len(refs) - ns])
        pl.when(last())(steps[-1])

    return wrapped


class _ColumnWriter:
    def __init__(self, stage_ref, sem_ref, dst_ref, col0, step, n_steps):
        self.stage, self.sem, self.dst, self.col0, self.step, self.n = stage_ref, sem_ref, dst_ref, col0, step, n_steps
        self.tm, self.w = stage_ref.shape[1], stage_ref.shape[2]

    def _copy(self, slot, row0):
        return pltpu.make_async_copy(self.stage.at[slot],
                                     self.dst.at[pl.ds(row0, self.tm), pl.ds(self.col0, self.w)], self.sem.at[slot])

    def slot(self):
        s = self.step % 2

        @pl.when(self.step >= 2)
        def _():
            self._copy(s, 0).wait()

        return self.stage.at[s]

    def send(self, row0):
        s = self.step % 2
        self._copy(s, row0).start()

        @pl.when(self.step == self.n - 1)
        def _():
            self._copy(s, 0).wait()
            if self.n >= 2:
                self._copy(1 - s, 0).wait()


def _stage_scratch(tm, widths):
    return ([pltpu.VMEM((2, tm, w), BF16) for w in widths], [pltpu.SemaphoreType.DMA((2,)) for _ in widths])


def _rider_specs(rider):
    if rider is None:
        return [], [], [], [], []
    anyspec = pl.BlockSpec(memory_space=pl.ANY)
    return ([anyspec] * len(rider.arrs), [anyspec] * len(rider.out_shapes), rider.out_shapes, rider.sems, rider.arrs)


HALF_D = D_MODEL // 2


def _inproj(x, g, w):
    T = x.shape[0]
    tm = min(512, T)

    def body(x_ref, g_ref, w_ref, z_ref, h_ref, hs_ref):
        @pl.when(pl.program_id(1) == 0)
        def _():
            xv = x_ref[...]
            r = lax.rsqrt(jnp.mean(xv * xv, axis=-1, keepdims=True) + RMS_EPS)
            hv = (xv * r * g_ref[...]).astype(BF16)
            hs_ref[...] = hv
            h_ref[...] = hv

        z_ref[...] = (jnp.dot(hs_ref[:, 0:HALF_D], w_ref[0], preferred_element_type=F32)
                      + jnp.dot(hs_ref[:, HALF_D:D_MODEL], w_ref[1], preferred_element_type=F32))

    return pl.pallas_call(
        body,
        name="inproj",
        grid=(T // tm, N_CHIPS),
        in_specs=[
            pl.BlockSpec((tm, D_MODEL), lambda i, j: (i, 0)),
            pl.BlockSpec((1, D_MODEL), lambda i, j: (0, 0)),
            pl.BlockSpec((2, None, HALF_D, SHARD_COLS), lambda i, j: (0, j, 0, 0)),
        ],
        out_specs=[
            pl.BlockSpec((tm, SHARD_COLS), lambda i, j: (i, j)),
            pl.BlockSpec((tm, D_MODEL), lambda i, j: (i, 0)),
        ],
        out_shape=[jax.ShapeDtypeStruct((T, IN_COLS), F32), jax.ShapeDtypeStruct((T, D_MODEL), BF16)],
        scratch_shapes=[pltpu.VMEM((tm, D_MODEL), BF16)],
        compiler_params=_cparams(2),
    )(x, g, w)


def _rms_h(x, g):
    T = x.shape[0]
    tm = min(512, T)

    def body(x_ref, g_ref, h_ref):
        xv = x_ref[...]
        r = lax.rsqrt(jnp.mean(xv * xv, axis=-1, keepdims=True) + RMS_EPS)
        h_ref[...] = (xv * r * g_ref[...]).astype(BF16)

    return pl.pallas_call(
        body, name="rms_h", grid=(T // tm,),
        in_specs=[pl.BlockSpec((tm, D_MODEL), lambda i: (i, 0)), pl.BlockSpec((1, D_MODEL), lambda i: (0, 0))],
        out_specs=pl.BlockSpec((tm, D_MODEL), lambda i: (i, 0)),
        out_shape=jax.ShapeDtypeStruct((T, D_MODEL), BF16), compiler_params=_cparams(1),
    )(x, g)


def _inproj_first(h, w0, order, riders):
    T = h.shape[0]
    tm = min(512, T)
    ni = T // tm
    n = len(riders)

    def body(order_ref, h_ref, w0_ref, *refs):
        rin = refs[:n]
        z_ref, gin_ref = refs[n:n + 2]
        rout = refs[n + 2:2 * n + 2]
        wbuf, csem, lsem, ssem, rsem = refs[2 * n + 2:2 * n + 7]
        s, i = pl.program_id(0), pl.program_id(1)
        x, y, c = _mesh_pos()
        me = 2 * x + y
        sib = (x, y, 1 - c)
        chips = _other_chips(x, y)
        if n:
            r_start, r_mid, r_end = _gather_steps(rin, rout, *refs[2 * n + 7:])

        def ici(d):
            return pltpu.make_async_remote_copy(w0_ref.at[c], gin_ref.at[c, me], ssem.at[d], rsem.at[d],
                                                device_id=(chips[d][1], chips[d][2], c), device_id_type=MESH_ID)

        def landed(d):
            return pltpu.make_async_remote_copy(w0_ref.at[c], gin_ref.at[c, chips[d][0]], ssem.at[d], rsem.at[d],
                                                device_id=sib, device_id_type=MESH_ID)

        def fwd(d, half):
            blk = gin_ref.at[half, chips[d][0]]
            return pltpu.make_async_remote_copy(blk, blk, ssem.at[3 + d], rsem.at[3 + d], device_id=sib,
                                                device_id_type=MESH_ID)

        def local(hf):
            return pltpu.make_async_copy(w0_ref.at[hf], gin_ref.at[hf, me], lsem.at[hf])

        def load(src):
            cp = pltpu.make_async_copy(src, wbuf, csem.at[0])
            cp.start()
            cp.wait()

        @pl.when((s == 0) & (i == 0))
        def _():
            for d in range(3):
                ici(d).start()
            local(0).start()
            local(1).start()
            load(w0_ref)

        for d in range(3):
            @pl.when((s == d + 1) & (i == 0))
            def _(d=d):
                landed(d).wait_recv()
                fwd(d, c).start()
                fwd(d, 1 - c).wait_recv()
                load(gin_ref.at[:, chips[d][0]])
                if d == 1 and n:
                    r_start()

        z_ref[...] = (jnp.dot(h_ref[:, 0:HALF_D], wbuf[0], preferred_element_type=F32)
                      + jnp.dot(h_ref[:, HALF_D:D_MODEL], wbuf[1], preferred_element_type=F32))

        @pl.when((s == N_CHIPS - 1) & (i == ni - 1))
        def _():
            for d in range(3):
                ici(d).wait_send()
                fwd(d, c).wait_send()
            local(0).wait()
            local(1).wait()
            if n:
                r_mid()
                r_end()

    anyspec = pl.BlockSpec(memory_space=pl.ANY)
    return pl.pallas_call(
        body,
        name="inproj_first",
        grid_spec=pltpu.PrefetchScalarGridSpec(
            num_scalar_prefetch=1,
            grid=(N_CHIPS, ni),
            in_specs=[pl.BlockSpec((tm, D_MODEL), lambda s, i, o: (i, 0)), anyspec] + [anyspec] * n,
            out_specs=[pl.BlockSpec((tm, SHARD_COLS), lambda s, i, o: (i, o[s])), anyspec] + [anyspec] * n,
            scratch_shapes=[pltpu.VMEM((2, HALF_D, SHARD_COLS), BF16), pltpu.SemaphoreType.DMA((1,)),
                            pltpu.SemaphoreType.DMA((2,)), pltpu.SemaphoreType.DMA((6,)),
                            pltpu.SemaphoreType.DMA((6,))] + (_gather_sems(n) if n else []),
        ),
        out_shape=[jax.ShapeDtypeStruct((T, IN_COLS), F32),
                   jax.ShapeDtypeStruct((2, N_CHIPS, HALF_D, SHARD_COLS), BF16)] + _gathered_shapes(riders),
        compiler_params=_cparams(2),
    )(order, h, w0, *riders)


def _outproj(ya, yb, yc, w, x):
    T = x.shape[0]
    tm = min(512, T)
    tn = 1024

    def body(ya_ref, yb_ref, yc_ref, w_ref, x_ref, o_ref, y_ref):
        acc = jnp.dot(ya_ref[...], w_ref[0:S5_W, :], preferred_element_type=F32)
        acc += jnp.dot(yb_ref[...], w_ref[S5_W:S5_W + SGU_W, :], preferred_element_type=F32)
        acc += jnp.dot(yc_ref[...], w_ref[S5_W + SGU_W:D_MODEL, :], preferred_element_type=F32)
        o_ref[...] = x_ref[...] + acc

        @pl.when(pl.program_id(1) == 0)
        def _():
            y_ref[:, 0:S5_W] = ya_ref[...]
            y_ref[:, S5_W:S5_W + SGU_W] = yb_ref[...]
            y_ref[:, S5_W + SGU_W:D_MODEL] = yc_ref[...]

    return pl.pallas_call(
        body,
        name="outproj",
        grid=(T // tm, D_MODEL // tn),
        in_specs=[
            pl.BlockSpec((tm, S5_W), lambda i, j: (i, 0)),
            pl.BlockSpec((tm, SGU_W), lambda i, j: (i, 0)),
            pl.BlockSpec((tm, POOL_W), lambda i, j: (i, 0)),
            pl.BlockSpec((D_MODEL, tn), lambda i, j: (0, j)),
            pl.BlockSpec((tm, tn), lambda i, j: (i, j)),
        ],
        out_specs=[
            pl.BlockSpec((tm, tn), lambda i, j: (i, j)),
            pl.BlockSpec((tm, D_MODEL), lambda i, j: (i, 0)),
        ],
        out_shape=[jax.ShapeDtypeStruct((T, D_MODEL), F32), jax.ShapeDtypeStruct((T, D_MODEL), BF16)],
        compiler_params=_cparams(2),
    )(ya, yb, yc, w, x)


def _outproj_bwd_dy(dxo, w):
    T = dxo.shape[0]
    tm = min(512, T)
    tn = 1024

    def body(d_ref, w_ref, o_ref, ds_ref):
        @pl.when(pl.program_id(1) == 0)
        def _():
            ds_ref[...] = d_ref[...].astype(BF16)

        o_ref[...] = lax.dot_general(ds_ref[...], w_ref[...], (((1,), (1,)), ((), ())), preferred_element_type=F32)

    return pl.pallas_call(
        body,
        name="outproj_bwd_dy",
        grid=(T // tm, D_MODEL // tn),
        in_specs=[
            pl.BlockSpec((tm, D_MODEL), lambda i, j: (i, 0)),
            pl.BlockSpec((tn, D_MODEL), lambda i, j: (j, 0)),
        ],
        out_specs=pl.BlockSpec((tm, tn), lambda i, j: (i, j)),
        out_shape=jax.ShapeDtypeStruct((T, D_MODEL), F32),
        scratch_shapes=[pltpu.VMEM((tm, D_MODEL), BF16)],
        compiler_params=_cparams(2),
    )(dxo, w)


def _outproj_bwd_dw(y, dxo):
    T = y.shape[0]
    tm = min(512, T)
    tr = 1024

    def body(y_ref, d_ref, o_ref):
        @pl.when(pl.program_id(1) == 0)
        def _():
            o_ref[...] = jnp.zeros_like(o_ref)

        o_ref[...] += _dot_tn(y_ref[...], d_ref[...])

    return pl.pallas_call(
        body,
        name="outproj_bwd_dw",
        grid=(D_MODEL // tr, T // tm),
        in_specs=[
            pl.BlockSpec((tm, tr), lambda p, t: (t, p)),
            pl.BlockSpec((tm, D_MODEL), lambda p, t: (t, 0)),
        ],
        out_specs=pl.BlockSpec((tr, D_MODEL), lambda p, t: (p, 0)),
        out_shape=jax.ShapeDtypeStruct((D_MODEL, D_MODEL), F32),
        compiler_params=_cparams(2),
    )(y, dxo)


def _inproj_bwd_dw(h, dz):
    T = h.shape[0]
    tm = min(512, T)

    def body(h_ref, dz_ref, o_ref):
        @pl.when(pl.program_id(1) == 0)
        def _():
            o_ref[...] = jnp.zeros_like(o_ref)

        o_ref[...] += _dot_tn(h_ref[...], dz_ref[...])

    return pl.pallas_call(
        body,
        name="inproj_bwd_dw",
        grid=(N_CHIPS, T // tm),
        in_specs=[
            pl.BlockSpec((tm, D_MODEL), lambda j, t: (t, 0)),
            pl.BlockSpec((tm, SHARD_COLS), lambda j, t: (t, j)),
        ],
        out_specs=pl.BlockSpec((None, D_MODEL, SHARD_COLS), lambda j, t: (j, 0, 0)),
        out_shape=jax.ShapeDtypeStruct((N_CHIPS, D_MODEL, SHARD_COLS), F32),
        compiler_params=_cparams(2),
    )(h, dz)


def _dx_tile(T):
    return min(512, max(T // 4, 8))


def _inproj_bwd_dx(dz, w4, x, g, dxo, rider=None, tiles=None, prev=None):
    T = x.shape[0]
    tm = _dx_tile(T)
    t0, ni = tiles if tiles else (0, T // tm)
    nk = N_CHIPS
    nt = (((1,), (1,)), ((), ()))
    n_in = 5 if prev is None else 6

    def body(dz_ref, w_ref, x_ref, g_ref, dxo_ref, *rest):
        dx_ref, dg_ref, acc_ref = rest[-3:]
        i, j = pl.program_id(0), pl.program_id(1)
        lo = lax.dot_general(dz_ref[...], w_ref[0], nt, preferred_element_type=F32)
        hi = lax.dot_general(dz_ref[...], w_ref[1], nt, preferred_element_type=F32)

        @pl.when(j == 0)
        def _():
            acc_ref[:, 0:HALF_D] = lo
            acc_ref[:, HALF_D:D_MODEL] = hi

        @pl.when(j > 0)
        def _():
            acc_ref[:, 0:HALF_D] += lo
            acc_ref[:, HALF_D:D_MODEL] += hi

        @pl.when(j == nk - 1)
        def _():
            @pl.when(i == 0)
            def _():
                dg_ref[...] = jnp.zeros_like(dg_ref)

            rc = min(128, tm)
            for c in range(tm // rc):
                rows = slice(c * rc, (c + 1) * rc)
                dh = acc_ref[rows, :]
                xv = x_ref[rows, :]
                r = lax.rsqrt(jnp.mean(xv * xv, axis=-1, keepdims=True) + RMS_EPS)
                xh = xv * r
                w = dh * g_ref[...]
                dx_ref[rows, :] = dxo_ref[rows, :] + r * (w - xh * jnp.mean(w * xh, axis=-1, keepdims=True))
                dg_ref[...] += jnp.sum(dh * xh, axis=0, keepdims=True)

    r_in, r_out, r_shapes, r_sems, r_args = _rider_specs(rider)
    return pl.pallas_call(
        _ride(body, n_in, 2, rider, lambda: (pl.program_id(0) == 0) & (pl.program_id(1) == 0),
              lambda: (pl.program_id(0) == ni - 1) & (pl.program_id(1) == nk - 1)),
        name="inproj_bwd_dx" + ("" if rider is None else "_ride") + ("" if prev is None else "_rest"),
        grid=(ni, nk),
        in_specs=[
            pl.BlockSpec((tm, SHARD_COLS), lambda i, j: (i + t0, j)),
            pl.BlockSpec((2, None, HALF_D, SHARD_COLS), lambda i, j: (0, j, 0, 0)),
            pl.BlockSpec((tm, D_MODEL), lambda i, j: (i + t0, 0)),
            pl.BlockSpec((1, D_MODEL), lambda i, j: (0, 0)),
            pl.BlockSpec((tm, D_MODEL), lambda i, j: (i + t0, 0)),
        ] + ([] if prev is None else [pl.BlockSpec(memory_space=pl.ANY)]) + r_in,
        out_specs=[
            pl.BlockSpec((tm, D_MODEL), lambda i, j: (i + t0, 0)),
            pl.BlockSpec((1, D_MODEL), lambda i, j: (0, 0)),
        ] + r_out,
        out_shape=[jax.ShapeDtypeStruct((T, D_MODEL), F32), jax.ShapeDtypeStruct((1, D_MODEL), F32)] + r_shapes,
        scratch_shapes=[pltpu.VMEM((tm, D_MODEL), F32)] + r_sems,
        input_output_aliases={} if prev is None else {5: 0},
        compiler_params=_cparams(2),
    )(dz, w4, x, g, dxo, *([] if prev is None else [prev]), *r_args)


def _outproj_loss(ya, yb, yc, w, x, g, tgt):
    T = x.shape[0]
    tm = min(256, T)

    def body(ya_ref, yb_ref, yc_ref, w_ref, x_ref, g_ref, t_ref, dx_ref, l_ref, dg_ref, y_ref):
        i = pl.program_id(0)
        acc = jnp.dot(ya_ref[...], w_ref[0:S5_W, :], preferred_element_type=F32)
        acc += jnp.dot(yb_ref[...], w_ref[S5_W:S5_W + SGU_W, :], preferred_element_type=F32)
        acc += jnp.dot(yc_ref[...], w_ref[S5_W + SGU_W:D_MODEL, :], preferred_element_type=F32)
        y_ref[:, 0:S5_W] = ya_ref[...]
        y_ref[:, S5_W:S5_W + SGU_W] = yb_ref[...]
        y_ref[:, S5_W + SGU_W:D_MODEL] = yc_ref[...]
        xv = x_ref[...] + acc
        r = lax.rsqrt(jnp.mean(xv * xv, axis=-1, keepdims=True) + RMS_EPS)
        xh = xv * r
        err = xh * g_ref[...] - t_ref[...]
        lpart = 0.5 * jnp.sum(jnp.mean(err * err, axis=-1, keepdims=True), axis=0, keepdims=True)
        dout = err * (1.0 / D_MODEL)
        w = dout * g_ref[...]
        dx_ref[...] = r * (w - xh * jnp.mean(w * xh, axis=-1, keepdims=True))
        gpart = jnp.sum(dout * xh, axis=0, keepdims=True)

        @pl.when(i == 0)
        def _():
            l_ref[...] = jnp.broadcast_to(lpart, l_ref.shape)
            dg_ref[...] = gpart

        @pl.when(i > 0)
        def _():
            l_ref[...] += jnp.broadcast_to(lpart, l_ref.shape)
            dg_ref[...] += gpart

    row = lambda w: pl.BlockSpec((tm, w), lambda i: (i, 0))
    return pl.pallas_call(
        body,
        name="outproj_loss",
        grid=(T // tm,),
        in_specs=[row(S5_W), row(SGU_W), row(POOL_W), _full((D_MODEL, D_MODEL)), row(D_MODEL), _full((1, D_MODEL)),
                  row(D_MODEL)],
        out_specs=[row(D_MODEL), _full((1, 128)), _full((1, D_MODEL)), row(D_MODEL)],
        out_shape=[
            jax.ShapeDtypeStruct((T, D_MODEL), F32),
            jax.ShapeDtypeStruct((1, 128), F32),
            jax.ShapeDtypeStruct((1, D_MODEL), F32),
            jax.ShapeDtypeStruct((T, D_MODEL), BF16),
        ],
        compiler_params=_cparams(1),
    )(ya, yb, yc, w, x, g, tgt)


def _s5_prep(lam_re, lam_im, b_re, b_im, log_dt):
    lam = lax.complex(lam_re, lam_im)
    dt = jnp.exp(log_dt)[:, None]
    a = jnp.exp(lam * dt)
    bbar = ((a - 1.0) / lam)[..., None] * lax.complex(b_re, b_im)
    return jnp.real(a), jnp.imag(a), jnp.real(bbar), jnp.imag(bbar)


def _block_diag_in(m):
    m4 = m.reshape(SUPER, 8, S5_STATE, S5_CH)
    eye = jnp.eye(8, dtype=m.dtype)
    out = jnp.einsum("jgph,gk->jghkp", m4, eye)
    return out.reshape(SUPER, 8 * S5_CH, 8 * S5_STATE)


def _block_diag_in_grad(d):
    d6 = d.reshape(SUPER, 8, S5_CH, 8, S5_STATE)
    diag = jnp.einsum("jghgp->jgph", d6)
    return diag.reshape(S5_GROUPS, S5_STATE, S5_CH)


def _block_diag_out(m):
    m4 = m.reshape(SUPER, 8, S5_CH, S5_STATE)
    eye = jnp.eye(8, dtype=m.dtype)
    out = jnp.einsum("jghp,gk->jgpkh", m4, eye)
    return out.reshape(SUPER, 8 * S5_STATE, 8 * S5_CH)


def _block_diag_out_grad(d):
    d6 = d.reshape(SUPER, 8, S5_STATE, 8, S5_CH)
    diag = jnp.einsum("jgpgh->jghp", d6)
    return diag.reshape(S5_GROUPS, S5_CH, S5_STATE)


def _scan_coefs(a_re, a_im, reverse):
    a = lax.complex(a_re.reshape(-1), a_im.reshape(-1))
    if reverse:
        a = jnp.conj(a)
    pw = [a]
    for _ in range(7):
        pw.append(pw[-1] * a)
    rows = jnp.arange(8)

    def masked(k):
        m = (rows + k <= 7) if reverse else (rows >= k)
        return jnp.where(m[:, None], pw[k - 1][None, :], 0.0)

    a1, a2, a4 = masked(1), masked(2), masked(4)
    carry = jnp.stack([pw[7 - r] for r in range(8)]) if reverse else jnp.stack(pw)
    parts = []
    for c in (a1, a2, a4, carry):
        parts += [jnp.real(c), jnp.imag(c)]
    return jnp.stack(parts).astype(F32)


def _scan_block(r, im, coef_ref, cs, reverse):
    for k, idx in ((1, 0), (2, 2), (4, 4)):
        ar = coef_ref[idx, :, cs]
        ai = coef_ref[idx + 1, :, cs]
        sh = 8 - k if reverse else k
        rr = pltpu.roll(r, sh, 0)
        ri = pltpu.roll(im, sh, 0)
        r, im = r + ar * rr - ai * ri, im + ar * ri + ai * rr
    return r, im


def _s5_fwd(z, p, rider=None):
    T = z.shape[0]
    tm = min(512, T)
    nblk = tm // 8
    W = STATE_W

    def body(xa_ref, ga_ref, bre_ref, bim_ref, cre_ref, cim_ref, dv_ref, wg_ref, bg_ref, coef_ref,
             ya_ref, yraw_ref, sre_ref, sim_ref, wre, wim):
        @pl.when(pl.program_id(0) == 0)
        def _():
            wre[0:8, :] = jnp.zeros((8, W), F32)
            wim[0:8, :] = jnp.zeros((8, W), F32)

        xa = xa_ref[...]
        xab = xa.astype(BF16)
        for j in range(SUPER):
            xj = xab[:, j * 128:(j + 1) * 128]
            wre[8:8 + tm, j * 512:(j + 1) * 512] = jnp.dot(xj, bre_ref[j], preferred_element_type=F32)
            wim[8:8 + tm, j * 512:(j + 1) * 512] = jnp.dot(xj, bim_ref[j], preferred_element_type=F32)

        def blk(b, carry):
            base = pl.multiple_of(8 + b * 8, 8)
            for cc in range(W // SCAN_COLS):
                cs = pl.ds(cc * SCAN_COLS, SCAN_COLS)
                r, im = _scan_block(wre[pl.ds(base, 8), cs], wim[pl.ds(base, 8), cs], coef_ref, cs, False)
                cr = wre[pl.ds(base - 1, 1), cs]
                ci = wim[pl.ds(base - 1, 1), cs]
                pr = coef_ref[6, :, cs]
                pi = coef_ref[7, :, cs]
                wre[pl.ds(base, 8), cs] = r + pr * cr - pi * ci
                wim[pl.ds(base, 8), cs] = im + pr * ci + pi * cr
            return carry

        lax.fori_loop(0, nblk, blk, 0)
        wre[0:8, :] = wre[tm:tm + 8, :]
        wim[0:8, :] = wim[tm:tm + 8, :]
        sre_ref[...] = wre[8:8 + tm, :]
        sim_ref[...] = wim[8:8 + tm, :]

        for j in range(SUPER):
            yr = jnp.dot(wre[8:8 + tm, j * 512:(j + 1) * 512].astype(BF16), cre_ref[j], preferred_element_type=F32)
            yr += jnp.dot(wim[8:8 + tm, j * 512:(j + 1) * 512].astype(BF16), cim_ref[j], preferred_element_type=F32)
            yraw_ref[:, j * 128:(j + 1) * 128] = yr
        yraw = yraw_ref[...] + dv_ref[...] * xa
        yraw_ref[...] = yraw
        yg = _gelu(yraw)
        q = jnp.dot(yg.astype(BF16), wg_ref[...], preferred_element_type=F32) + bg_ref[...]
        sga, _ = _silu_and_grad(ga_ref[...])
        ya_ref[...] = (yg * jax.nn.sigmoid(q) * sga).astype(BF16)

    nt = T // tm
    r_in, r_out, r_shapes, r_sems, r_args = _rider_specs(rider)
    return pl.pallas_call(
        _ride(body, 10, 4, rider, lambda: pl.program_id(0) == 0, lambda: pl.program_id(0) == nt - 1,
              lambda: pl.program_id(0) == nt - 1),
        name="s5_fwd" + ("" if rider is None else "_ride"),
        grid=(nt,),
        in_specs=[
            pl.BlockSpec((tm, S5_W), lambda i: (i, 0)),
            pl.BlockSpec((tm, S5_W), lambda i: (i, 6)),
            _full((SUPER, 128, 512)), _full((SUPER, 128, 512)),
            _full((SUPER, 512, 128)), _full((SUPER, 512, 128)),
            _full((1, S5_W)), _full((S5_W, S5_W)), _full((1, S5_W)),
            _full((8, 8, W)),
        ] + r_in,
        out_specs=[
            pl.BlockSpec((tm, S5_W), lambda i: (i, 0)),
            pl.BlockSpec((tm, S5_W), lambda i: (i, 0)),
            pl.BlockSpec((tm, W), lambda i: (i, 0)),
            pl.BlockSpec((tm, W), lambda i: (i, 0)),
        ] + r_out,
        out_shape=[
            jax.ShapeDtypeStruct((T, S5_W), BF16),
            jax.ShapeDtypeStruct((T, S5_W), F32),
            jax.ShapeDtypeStruct((T, W), F32),
            jax.ShapeDtypeStruct((T, W), F32),
        ] + r_shapes,
        scratch_shapes=[pltpu.VMEM((tm + 8, W), F32), pltpu.VMEM((tm + 8, W), F32)] + r_sems,
        compiler_params=_cparams(1),
    )(z, z, p["b4re"], p["b4im"], p["c4re"], p["c4im"], p["dvec"], p["wglu"], p["bglu"], p["coef_f"], *r_args)


def _s5_bwd(dy, z, yraw, sre, sim, p, rider=None):
    T = z.shape[0]
    tm = min(512, T)
    nt = T // tm
    nblk = tm // 8
    W = STATE_W
    rev = lambda i: nt - 1 - i

    def body(dya_ref, xa_ref, ga_ref, yraw_ref, sre_ref, sim_ref, hre_ref, him_ref,
             bre_t_ref, bim_t_ref, cre_t_ref, cim_t_ref, dv_ref, wg_ref, wgt_ref, bg_ref, coef_ref,
             dz_ref, dbre_ref, dbim_ref, dcre_ref, dcim_ref, dd_ref, dwg_ref, dbg_ref, da_ref,
             wre, wim, dyr_ref, xa_stage, ga_stage, xa_sem, ga_sem):
        i = pl.program_id(0)
        xa_out = _ColumnWriter(xa_stage, xa_sem, dz_ref, 0, i, nt)
        ga_out = _ColumnWriter(ga_stage, ga_sem, dz_ref, 6 * 512, i, nt)
        dxa_ref, dga_ref = xa_out.slot(), ga_out.slot()

        @pl.when(i == 0)
        def _():
            wre[tm:tm + 8, :] = jnp.zeros((8, W), F32)
            wim[tm:tm + 8, :] = jnp.zeros((8, W), F32)
            for ref in (dbre_ref, dbim_ref, dcre_ref, dcim_ref, dd_ref, dwg_ref, dbg_ref, da_ref):
                ref[...] = jnp.zeros_like(ref)

        xa = xa_ref[...]
        dya = dya_ref[...]
        yg, dgelu = _gelu_and_grad(yraw_ref[...])
        ygb = yg.astype(BF16)
        q = jnp.dot(ygb, wg_ref[...], preferred_element_type=F32) + bg_ref[...]
        sq = jax.nn.sigmoid(q)
        sga, dsga = _silu_and_grad(ga_ref[...])
        dga_ref[...] = (dya * (yg * sq) * dsga).astype(BF16)
        dya0 = dya * sga
        dq = dya0 * yg * sq * (1.0 - sq)
        dqb = dq.astype(BF16)
        dyg = dya0 * sq + jnp.dot(dqb, wgt_ref[...], preferred_element_type=F32)
        dwg_ref[...] += _dot_tn(ygb, dqb)
        dbg_ref[...] += jnp.sum(dq, axis=0, keepdims=True)
        dyraw = dyg * dgelu
        dd_ref[...] += jnp.sum(dyraw * xa, axis=0, keepdims=True)
        dyr_ref[...] = dyraw.astype(BF16)

        for j in range(SUPER):
            dj = dyr_ref[:, j * 128:(j + 1) * 128]
            wre[0:tm, j * 512:(j + 1) * 512] = jnp.dot(dj, cre_t_ref[j], preferred_element_type=F32)
            wim[0:tm, j * 512:(j + 1) * 512] = jnp.dot(dj, cim_t_ref[j], preferred_element_type=F32)

        row0 = lax.broadcasted_iota(jnp.int32, (8, SCAN_COLS), 0) == 0
        head_on = (i < nt - 1).astype(F32)

        def one_block(base, first):
            for cc in range(W // SCAN_COLS):
                cs = pl.ds(cc * SCAN_COLS, SCAN_COLS)
                r, im = _scan_block(wre[pl.ds(base, 8), cs], wim[pl.ds(base, 8), cs], coef_ref, cs, True)
                cr = wre[pl.ds(base + 8, 1), cs]
                ci = wim[pl.ds(base + 8, 1), cs]
                pr = coef_ref[6, :, cs]
                pi = coef_ref[7, :, cs]
                r, im = r + pr * cr - pi * ci, im + pr * ci + pi * cr
                wre[pl.ds(base, 8), cs] = r
                wim[pl.ds(base, 8), cs] = im
                if first:
                    pre = hre_ref[7:8, cs] * head_on
                    pim = him_ref[7:8, cs] * head_on
                else:
                    pre = sre_ref[pl.ds(base - 1, 1), cs]
                    pim = sim_ref[pl.ds(base - 1, 1), cs]
                spr = jnp.where(row0, pre, pltpu.roll(sre_ref[pl.ds(base, 8), cs], 1, 0))
                spi = jnp.where(row0, pim, pltpu.roll(sim_ref[pl.ds(base, 8), cs], 1, 0))
                da_ref[0, :, cs] += r * spr + im * spi
                da_ref[1, :, cs] += im * spr - r * spi

        def blk(b, carry):
            one_block(pl.multiple_of((nblk - 1 - b) * 8, 8), False)
            return carry

        lax.fori_loop(0, nblk - 1, blk, 0)
        one_block(0, True)
        wre[tm:tm + 8, :] = wre[0:8, :]
        wim[tm:tm + 8, :] = wim[0:8, :]

        xab = xa.astype(BF16)
        for j in range(SUPER):
            cols = slice(j * 512, (j + 1) * 512)
            gre = wre[0:tm, cols].astype(BF16)
            gim = wim[0:tm, cols].astype(BF16)
            xj = xab[:, j * 128:(j + 1) * 128]
            dj = dyr_ref[:, j * 128:(j + 1) * 128]
            dbre_ref[j] += _dot_tn(xj, gre)
            dbim_ref[j] += _dot_tn(xj, gim)
            dcre_ref[j] += _dot_tn(sre_ref[:, cols], dj)
            dcim_ref[j] += _dot_tn(sim_ref[:, cols], dj)
            dxj = jnp.dot(gre, bre_t_ref[j], preferred_element_type=F32)
            dxj += jnp.dot(gim, bim_t_ref[j], preferred_element_type=F32)
            dxj += dyraw[:, j * 128:(j + 1) * 128] * dv_ref[:, j * 128:(j + 1) * 128]
            dxa_ref[:, j * 128:(j + 1) * 128] = dxj.astype(BF16)
        xa_out.send(rev(i) * tm)
        ga_out.send(rev(i) * tm)

    acc = lambda shape: _full(shape)
    hb = tm // 8
    r_in, r_out, r_shapes, r_sems, r_args = _rider_specs(rider)
    stages, stage_sems = _stage_scratch(tm, (S5_W, S5_W))
    return pl.pallas_call(
        _ride(body, 17, 9, rider, lambda: pl.program_id(0) == 0, lambda: pl.program_id(0) == nt - 1),
        name="s5_bwd" + ("" if rider is None else "_ride"),
        grid=(nt,),
        in_specs=[
            pl.BlockSpec((tm, S5_W), lambda i: (rev(i), 0)),
            pl.BlockSpec((tm, S5_W), lambda i: (rev(i), 0)),
            pl.BlockSpec((tm, S5_W), lambda i: (rev(i), 6)),
            pl.BlockSpec((tm, S5_W), lambda i: (rev(i), 0)),
            pl.BlockSpec((tm, W), lambda i: (rev(i), 0)),
            pl.BlockSpec((tm, W), lambda i: (rev(i), 0)),
            pl.BlockSpec((8, W), lambda i: (jnp.maximum(rev(i) * hb - 1, 0), 0)),
            pl.BlockSpec((8, W), lambda i: (jnp.maximum(rev(i) * hb - 1, 0), 0)),
            _full((SUPER, 512, 128)), _full((SUPER, 512, 128)),
            _full((SUPER, 128, 512)), _full((SUPER, 128, 512)),
            _full((1, S5_W)), _full((S5_W, S5_W)), _full((S5_W, S5_W)), _full((1, S5_W)),
            _full((8, 8, W)),
        ] + r_in,
        out_specs=[
            pl.BlockSpec(memory_space=pl.ANY),
            acc((SUPER, 128, 512)), acc((SUPER, 128, 512)),
            acc((SUPER, 512, 128)), acc((SUPER, 512, 128)),
            acc((1, S5_W)), acc((S5_W, S5_W)), acc((1, S5_W)), acc((2, 8, W)),
        ] + r_out,
        out_shape=[
            jax.ShapeDtypeStruct((T, IN_COLS), BF16),
            jax.ShapeDtypeStruct((SUPER, 128, 512), F32), jax.ShapeDtypeStruct((SUPER, 128, 512), F32),
            jax.ShapeDtypeStruct((SUPER, 512, 128), F32), jax.ShapeDtypeStruct((SUPER, 512, 128), F32),
            jax.ShapeDtypeStruct((1, S5_W), F32), jax.ShapeDtypeStruct((S5_W, S5_W), F32),
            jax.ShapeDtypeStruct((1, S5_W), F32), jax.ShapeDtypeStruct((2, 8, W), F32),
        ] + r_shapes,
        scratch_shapes=[pltpu.VMEM((tm + 8, W), F32), pltpu.VMEM((tm + 8, W), F32), pltpu.VMEM((tm, S5_W), BF16)]
        + stages + stage_sems + r_sems,
        compiler_params=_cparams(1),
    )(dy, z, z, yraw, sre, sim, sre, sim,
      p["b4re_t"], p["b4im_t"], p["c4re_t"], p["c4im_t"], p["dvec"], p["wglu"], p["wglu_t"], p["bglu"], p["coef_r"],
      *r_args)


def _ln_fwd(vf, lng, lnb):
    mu = jnp.mean(vf, axis=-1, keepdims=True)
    d = vf - mu
    rstd = lax.rsqrt(jnp.mean(d * d, axis=-1, keepdims=True) + LN_EPS)
    xh = d * rstd
    return xh, rstd, xh * lng + lnb


def _col_block(tm, b):
    return pl.BlockSpec((tm, 512), lambda i: (i, b))


def _ln_halves(vf0, vf1):
    mu = (jnp.sum(vf0, axis=-1, keepdims=True) + jnp.sum(vf1, axis=-1, keepdims=True)) * (1.0 / SGU_W)
    d0, d1 = vf0 - mu, vf1 - mu
    var = (jnp.sum(d0 * d0, axis=-1, keepdims=True) + jnp.sum(d1 * d1, axis=-1, keepdims=True)) * (1.0 / SGU_W)
    rstd = lax.rsqrt(var + LN_EPS)
    return d0 * rstd, d1 * rstd, rstd


def _sgu_fwd(z, ws, bsf, lng, lnb):
    T = z.shape[0]
    tm = min(512, T)

    def body(u0, u1, v0, v1, g0, g1, ws_ref, bs_ref, lng_ref, lnb_ref, yb_ref, vn_ref):
        for c in range(tm // CHUNK):
            rows = slice(c * CHUNK, (c + 1) * CHUNK)
            xh0, xh1, _ = _ln_halves(_gelu(v0[rows, :]), _gelu(v1[rows, :]))
            vn_ref[:, 0:512] = (xh0 * lng_ref[:, 0:512] + lnb_ref[:, 0:512]).astype(BF16)
            vn_ref[:, 512:1024] = (xh1 * lng_ref[:, 512:1024] + lnb_ref[:, 512:1024]).astype(BF16)
            for half, (u_ref, g_ref) in enumerate(((u0, g0), (u1, g1))):
                sg, _ = _silu_and_grad(g_ref[rows, :])
                m = _gelu(u_ref[rows, :]) * sg
                for hh in range(SGU_HEADS // 2):
                    h = half * (SGU_HEADS // 2) + hh
                    cols = slice(h * 128, (h + 1) * 128)
                    s = jnp.dot(ws_ref[h], vn_ref[:, cols], preferred_element_type=F32) + bs_ref[:, cols]
                    yb_ref[rows, cols] = (m[:, hh * 128:(hh + 1) * 128] * s).astype(BF16)

    return pl.pallas_call(
        body,
        name="sgu_fwd",
        grid=(T // tm,),
        in_specs=[_col_block(tm, b) for b in (1, 2, 3, 4, 7, 8)] + [
            _full((SGU_HEADS, CHUNK, CHUNK)), _full((CHUNK, SGU_W)), _full((1, SGU_W)), _full((1, SGU_W)),
        ],
        out_specs=pl.BlockSpec((tm, SGU_W), lambda i: (i, 0)),
        out_shape=jax.ShapeDtypeStruct((T, SGU_W), BF16),
        scratch_shapes=[pltpu.VMEM((CHUNK, SGU_W), BF16)],
        compiler_params=_cparams(1),
    )(z, z, z, z, z, z, ws, bsf, lng, lnb)


def _sgu_bwd(dy, z, dz, ws, ws_t, bsf, lng, lnb, rider=None):
    T = z.shape[0]
    tm = min(512, T)
    HH = SGU_HEADS // 2

    def body(u0, u1, v0, v1, g0, g1, dy0, dy1, ws_ref, wst_ref, bs_ref, lng_ref, lnb_ref, dz_in,
             dz_ref, dws_ref, dbs_ref, dlng_ref, dlnb_ref, vn_ref, dvn_ref, *stage):
        step = pl.program_id(0)
        outs = [_ColumnWriter(stage[k], stage[3 + k], dz_ref, col, step, T // tm)
                for k, col in enumerate((512, 1536, 3584))]
        du_ref, dv_ref, dgb_ref = (o.slot() for o in outs)

        @pl.when(step == 0)
        def _():
            for ref in (dws_ref, dbs_ref, dlng_ref, dlnb_ref):
                ref[...] = jnp.zeros_like(ref)

        for c in range(tm // CHUNK):
            rows = slice(c * CHUNK, (c + 1) * CHUNK)
            vf0, dgv0 = _gelu_and_grad(v0[rows, :])
            vf1, dgv1 = _gelu_and_grad(v1[rows, :])
            xh0, xh1, rstd = _ln_halves(vf0, vf1)
            vn_ref[:, 0:512] = (xh0 * lng_ref[:, 0:512] + lnb_ref[:, 0:512]).astype(BF16)
            vn_ref[:, 512:1024] = (xh1 * lng_ref[:, 512:1024] + lnb_ref[:, 512:1024]).astype(BF16)
            for half, (u_ref, g_ref, dy_ref) in enumerate(((u0, g0, dy0), (u1, g1, dy1))):
                ug, dgu = _gelu_and_grad(u_ref[rows, :])
                sg, dsg = _silu_and_grad(g_ref[rows, :])
                dyb = dy_ref[rows, :]
                dyb0 = dyb * sg
                ds_half = dyb0 * ug
                du_scale = dyb0 * dgu
                dg_scale = dyb * ug * dsg
                for hh in range(HH):
                    h = half * HH + hh
                    cols = slice(h * 128, (h + 1) * 128)
                    lc = slice(hh * 128, (hh + 1) * 128)
                    s = jnp.dot(ws_ref[h], vn_ref[:, cols], preferred_element_type=F32) + bs_ref[:, cols]
                    du_ref[rows, cols] = (du_scale[:, lc] * s).astype(BF16)
                    dgb_ref[rows, cols] = (dg_scale[:, lc] * s).astype(BF16)
                    ds = ds_half[:, lc]
                    dbs_ref[:, cols] += ds
                    dsb = ds.astype(BF16)
                    dws_ref[h] += _dot_nt(dsb, vn_ref[:, cols])
                    dvn_ref[:, cols] = jnp.dot(wst_ref[h], dsb, preferred_element_type=F32)
            dvn0 = dvn_ref[:, 0:512]
            dvn1 = dvn_ref[:, 512:1024]
            dlnb_ref[:, 0:512] += jnp.sum(dvn0, axis=0, keepdims=True)
            dlnb_ref[:, 512:1024] += jnp.sum(dvn1, axis=0, keepdims=True)
            dlng_ref[:, 0:512] += jnp.sum(dvn0 * xh0, axis=0, keepdims=True)
            dlng_ref[:, 512:1024] += jnp.sum(dvn1 * xh1, axis=0, keepdims=True)
            dxh0 = dvn0 * lng_ref[:, 0:512]
            dxh1 = dvn1 * lng_ref[:, 512:1024]
            m1 = (jnp.sum(dxh0, axis=-1, keepdims=True) + jnp.sum(dxh1, axis=-1, keepdims=True)) * (1.0 / SGU_W)
            m2 = (jnp.sum(dxh0 * xh0, axis=-1, keepdims=True) + jnp.sum(dxh1 * xh1, axis=-1, keepdims=True)) * (1.0 / SGU_W)
            dv_ref[rows, 0:512] = (rstd * (dxh0 - m1 - xh0 * m2) * dgv0).astype(BF16)
            dv_ref[rows, 512:1024] = (rstd * (dxh1 - m1 - xh1 * m2) * dgv1).astype(BF16)
        for o in outs:
            o.send(step * tm)

    anyspec = pl.BlockSpec(memory_space=pl.ANY)
    r_in, r_out, r_shapes, r_sems, r_args = _rider_specs(rider)
    stages, stage_sems = _stage_scratch(tm, (SGU_W, SGU_W, SGU_W))
    return pl.pallas_call(
        _ride(body, 14, 5, rider, lambda: pl.program_id(0) == 0, lambda: pl.program_id(0) == T // tm - 1),
        name="sgu_bwd" + ("" if rider is None else "_ride"),
        grid=(T // tm,),
        in_specs=[_col_block(tm, b) for b in (1, 2, 3, 4, 7, 8)] + [_col_block(tm, 1), _col_block(tm, 2)] + [
            _full((SGU_HEADS, CHUNK, CHUNK)), _full((SGU_HEADS, CHUNK, CHUNK)),
            _full((CHUNK, SGU_W)), _full((1, SGU_W)), _full((1, SGU_W)), anyspec,
        ] + r_in,
        out_specs=[anyspec,
                   _full((SGU_HEADS, CHUNK, CHUNK)), _full((CHUNK, SGU_W)), _full((1, SGU_W)), _full((1, SGU_W))] + r_out,
        input_output_aliases={13: 0},
        out_shape=[
            jax.ShapeDtypeStruct((T, IN_COLS), BF16),
            jax.ShapeDtypeStruct((SGU_HEADS, CHUNK, CHUNK), F32), jax.ShapeDtypeStruct((CHUNK, SGU_W), F32),
            jax.ShapeDtypeStruct((1, SGU_W), F32), jax.ShapeDtypeStruct((1, SGU_W), F32),
        ] + r_shapes,
        scratch_shapes=[pltpu.VMEM((CHUNK, SGU_W), BF16), pltpu.VMEM((CHUNK, SGU_W), F32)] + stages + stage_sems + r_sems,
        compiler_params=_cparams(1),
    )(z, z, z, z, z, z, dy, dy, ws, ws_t, bsf, lng, lnb, dz, *r_args)


def _pool_den(first_row, n):
    return (lax.broadcasted_iota(jnp.int32, (n, 1), 0) + first_row + 1).astype(F32)


def _pool_p(ext, xc, pos, tm):
    w2 = ext + pltpu.roll(ext, 1, 0)
    w4 = w2 + pltpu.roll(w2, 2, 0)
    w8 = w4 + pltpu.roll(w4, 4, 0)
    w16 = w8 + pltpu.roll(w8, 8, 0)
    out = []
    for g, (w, ws) in enumerate(zip(POOL_WINDOWS, (w2, w4, w8, w16))):
        cols = slice(g * 128, (g + 1) * 128)
        mean = ws[POOL_HALO:POOL_HALO + tm, cols] / jnp.minimum(pos, float(w))
        out.append(mean - xc[:, cols])
    return out


def _pool_fwd(z, wp, scale):
    T = z.shape[0]
    tm = min(512, T)
    hb = tm // POOL_HALO

    def body(xc_ref, hx_ref, gc_ref, wp_ref, sc_ref, yc_ref):
        i = pl.program_id(0)
        xc = xc_ref[...]
        halo = hx_ref[...] * (i > 0).astype(F32)
        ext = jnp.concatenate([halo, xc], axis=0)
        ps = _pool_p(ext, xc, _pool_den(i * tm, tm), tm)
        sg, _ = _silu_and_grad(gc_ref[...])
        for g in range(4):
            cols = slice(g * 128, (g + 1) * 128)
            pw = _dot(ps[g], wp_ref[g])
            yc_ref[:, cols] = (pw * sc_ref[:, cols] * sg[:, cols]).astype(BF16)

    return pl.pallas_call(
        body,
        name="pool_fwd",
        grid=(T // tm,),
        in_specs=[
            _col_block(tm, 5),
            pl.BlockSpec((POOL_HALO, 512), lambda i: (jnp.maximum(i * hb - 1, 0), 5)),
            _col_block(tm, 9),
            _full((4, 128, 128)), _full((1, POOL_W)),
        ],
        out_specs=pl.BlockSpec((tm, POOL_W), lambda i: (i, 0)),
        out_shape=jax.ShapeDtypeStruct((T, POOL_W), BF16),
        compiler_params=_cparams(1),
    )(z, z, z, wp, scale)


def _pool_bwd(dy, z, dz, wp, wp_t, scale):
    T = z.shape[0]
    tm = min(512, T)
    nt = T // tm
    hb = tm // POOL_HALO
    last_hb = T // POOL_HALO - 1
    L = tm + POOL_HALO

    def body(xc_ref, hx_ref, gc_ref, gn_ref, dyc_ref, dyn_ref, wp_ref, wpt_ref, sc_ref, dz_in,
             dz_ref, dwp_ref, dsc_ref, xc_stage, gc_stage, xc_sem, gc_sem):
        i = pl.program_id(0)
        xc_out = _ColumnWriter(xc_stage, xc_sem, dz_ref, 5 * 512, i, nt)
        gc_out = _ColumnWriter(gc_stage, gc_sem, dz_ref, 9 * 512, i, nt)
        dxc_ref, dgc_ref = xc_out.slot(), gc_out.slot()

        @pl.when(i == 0)
        def _():
            dwp_ref[...] = jnp.zeros_like(dwp_ref)
            dsc_ref[...] = jnp.zeros_like(dsc_ref)

        xc = xc_ref[...]
        halo = hx_ref[...] * (i > 0).astype(F32)
        pos = _pool_den(i * tm, tm)
        ps = _pool_p(jnp.concatenate([halo, xc], axis=0), xc, pos, tm)
        sg, dsg = _silu_and_grad(gc_ref[...])
        dyc = dyc_ref[...]
        dyc0 = dyc * sg
        dpw = dyc0 * sc_ref[...]
        sgn, _ = _silu_and_grad(gn_ref[...])
        dpwn = dyn_ref[...] * sgn * sc_ref[...] * (i < nt - 1).astype(F32)
        posn = _pool_den((i + 1) * tm, POOL_HALO)
        dps, qs = [], []
        for g, w in enumerate(POOL_WINDOWS):
            cols = slice(g * 128, (g + 1) * 128)
            pw = _dot(ps[g], wp_ref[g])
            dgc_ref[:, cols] = (dyc[:, cols] * pw * sc_ref[:, cols] * dsg[:, cols]).astype(BF16)
            dsc_ref[:, cols] += jnp.sum(dyc0[:, cols] * pw, axis=0, keepdims=True)
            dwp_ref[g] += _dot_tn(ps[g], dpw[:, cols])
            dp = _dot(dpw[:, cols], wpt_ref[g])
            dpn = _dot(dpwn[:, cols], wpt_ref[g])
            dps.append(dp)
            qs.append(jnp.concatenate([dp / jnp.minimum(pos, float(w)), dpn / jnp.minimum(posn, float(w))], axis=0))
        ext = jnp.concatenate(qs, axis=1)
        f2 = ext + pltpu.roll(ext, L - 1, 0)
        f4 = f2 + pltpu.roll(f2, L - 2, 0)
        f8 = f4 + pltpu.roll(f4, L - 4, 0)
        f16 = f8 + pltpu.roll(f8, L - 8, 0)
        for g, f in enumerate((f2, f4, f8, f16)):
            cols = slice(g * 128, (g + 1) * 128)
            dxc_ref[:, cols] = (f[0:tm, cols] - dps[g]).astype(BF16)
        xc_out.send(i * tm)
        gc_out.send(i * tm)

    nxt = lambda i: jnp.minimum((i + 1) * hb, last_hb)
    anyspec = pl.BlockSpec(memory_space=pl.ANY)
    stages, stage_sems = _stage_scratch(tm, (POOL_W, POOL_W))
    return pl.pallas_call(
        body,
        name="pool_bwd",
        grid=(nt,),
        in_specs=[
            _col_block(tm, 5),
            pl.BlockSpec((POOL_HALO, 512), lambda i: (jnp.maximum(i * hb - 1, 0), 5)),
            _col_block(tm, 9),
            pl.BlockSpec((POOL_HALO, 512), lambda i: (nxt(i), 9)),
            _col_block(tm, 3),
            pl.BlockSpec((POOL_HALO, 512), lambda i: (nxt(i), 3)),
            _full((4, 128, 128)), _full((4, 128, 128)), _full((1, POOL_W)), anyspec,
        ],
        out_specs=[anyspec, _full((4, 128, 128)), _full((1, POOL_W))],
        input_output_aliases={9: 0},
        out_shape=[
            jax.ShapeDtypeStruct((T, IN_COLS), BF16),
            jax.ShapeDtypeStruct((4, 128, 128), F32), jax.ShapeDtypeStruct((1, POOL_W), F32),
        ],
        scratch_shapes=stages + stage_sems,
        compiler_params=_cparams(1),
    )(z, z, z, z, dy, dy, wp, wp_t, scale, dz)


def _row_tile(rows, cols):
    tr = 8
    while tr * 2 * cols * 4 <= 2 * 1024 * 1024 and rows % (tr * 2) == 0:
        tr *= 2
    return tr


def _add_own_half(part, recv, cidx):
    _, _, R2, C = part.shape
    tr = _row_tile(R2, C)

    def body(c_ref, a_ref, r_ref, o_ref):
        o_ref[...] = (a_ref[...] + r_ref[...]).astype(BF16)

    return pl.pallas_call(
        body,
        name="add_own_half",
        grid_spec=pltpu.PrefetchScalarGridSpec(
            num_scalar_prefetch=1,
            grid=(N_CHIPS, R2 // tr),
            in_specs=[
                pl.BlockSpec((None, None, tr, C), lambda j, i, c: (j, c[0], i, 0)),
                pl.BlockSpec((None, tr, C), lambda j, i, c: (j, i, 0)),
            ],
            out_specs=pl.BlockSpec((None, tr, C), lambda j, i, c: (j, i, 0)),
        ),
        out_shape=jax.ShapeDtypeStruct((N_CHIPS, R2, C), BF16),
        compiler_params=_cparams(2),
    )(cidx, part, recv)


def _add2(a, b):
    R, C = a.shape
    tr = _row_tile(R, C)

    def body(a_ref, b_ref, o_ref):
        o_ref[...] = a_ref[...] + b_ref[...]

    spec = pl.BlockSpec((tr, C), lambda i: (i, 0))
    return pl.pallas_call(
        body, name="add2", grid=(R // tr,), in_specs=[spec, spec], out_specs=spec,
        out_shape=jax.ShapeDtypeStruct((R, C), F32), compiler_params=_cparams(1),
    )(a, b)


def _sum_chips(parts):
    _, R, C = parts.shape
    tr = _row_tile(R, N_CHIPS * C)

    def body(p_ref, o_ref):
        p = [p_ref[j].astype(F32) for j in range(N_CHIPS)]
        o_ref[...] = ((p[0] + p[1]) + p[2]) + p[3]

    return pl.pallas_call(
        body, name="sum_chips", grid=(R // tr,),
        in_specs=[pl.BlockSpec((N_CHIPS, tr, C), lambda i: (0, i, 0))],
        out_specs=pl.BlockSpec((tr, C), lambda i: (i, 0)),
        out_shape=jax.ShapeDtypeStruct((R, C), F32), compiler_params=_cparams(1),
    )(parts)


def _adamw_math(w, g, m, v):
    m = ADAM_B1 * m + (1.0 - ADAM_B1) * g
    v = ADAM_B2 * v + (1.0 - ADAM_B2) * (g * g)
    m_hat = m / (1.0 - ADAM_B1 ** ADAM_STEP)
    v_hat = v / (1.0 - ADAM_B2 ** ADAM_STEP)
    delta = -ADAM_LR * (m_hat / (jnp.sqrt(v_hat) + ADAM_EPS) + ADAM_WD * w)
    return delta, m, v


def _adamw(w, g, m, v):
    R, C = w.shape
    tr = _row_tile(R, C)

    def body(w_ref, g_ref, m_ref, v_ref, d_ref, mo_ref, vo_ref):
        d_ref[...], mo_ref[...], vo_ref[...] = _adamw_math(w_ref[...], g_ref[...], m_ref[...], v_ref[...])

    spec = pl.BlockSpec((tr, C), lambda i: (i, 0))
    shp = jax.ShapeDtypeStruct((R, C), F32)
    return pl.pallas_call(
        body, name="adamw", grid=(R // tr,), in_specs=[spec] * 4, out_specs=[spec] * 3,
        out_shape=[shp] * 3, compiler_params=_cparams(1),
    )(w, g, m, v)


def _adamw_halves(w, mine, theirs, m, v, cidx, rider=None):
    _, _, R2, C = w.shape
    tr = _row_tile(R2, C)
    nr = R2 // tr

    def body(c_ref, w_ref, a0_ref, b0_ref, a1_ref, b1_ref, m_ref, v_ref, g_ref, d_ref, mo_ref, vo_ref):
        own = pl.program_id(1) == c_ref[0]
        g0 = jnp.where(own, a0_ref[...], b0_ref[...])
        g1 = jnp.where(own, a1_ref[...], b1_ref[...])
        g = jnp.where(pl.program_id(0) == 0, g0, g1)
        g_ref[...] = g
        d_ref[...], mo_ref[...], vo_ref[...] = _adamw_math(w_ref[...], g, m_ref[...], v_ref[...])

    full = pl.BlockSpec((None, None, tr, C), lambda l, h, i, c: (l, h, i, 0))

    def pick(layer, mine_side):
        def index(l, h, i, c):
            used = (l == layer) & ((h == c[0]) == mine_side)
            return (jnp.where(used, i, 0), 0)
        return pl.BlockSpec((tr, C), index)

    shp = jax.ShapeDtypeStruct(w.shape, F32)
    r_in, r_out, r_shapes, r_sems, r_args = _rider_specs(rider)
    last = lambda: (pl.program_id(0) == 1) & (pl.program_id(1) == 1) & (pl.program_id(2) == nr - 1)
    first = lambda: (pl.program_id(0) == 0) & (pl.program_id(1) == 0) & (pl.program_id(2) == 0)
    return pl.pallas_call(
        _ride(body, 8, 4, rider, first, last),
        name="adamw_halves" + ("" if rider is None else "_ride"),
        grid_spec=pltpu.PrefetchScalarGridSpec(
            num_scalar_prefetch=1, grid=(2, 2, nr),
            in_specs=[full, pick(0, True), pick(0, False), pick(1, True), pick(1, False), full, full] + r_in,
            out_specs=[full] * 4 + r_out,
            scratch_shapes=r_sems,
        ),
        out_shape=[shp] * 4 + r_shapes,
        compiler_params=_cparams(3),
    )(cidx, w, mine[0], theirs[0], mine[1], theirs[1], m, v, *r_args)


_ANY = pl.BlockSpec(memory_space=pl.ANY)


def _mesh_pos():
    return lax.axis_index("x"), lax.axis_index("y"), lax.axis_index("c")


def _other_chips(x, y):
    return [(2 * x + (1 - y), x, 1 - y), (2 * (1 - x) + y, 1 - x, y), (2 * (1 - x) + (1 - y), 1 - x, 1 - y)]


def _gathered_shapes(shards):
    return [jax.ShapeDtypeStruct((2, N_CHIPS) + s.shape[1:], s.dtype) for s in shards]


def _gather_sems(n):
    return [pltpu.SemaphoreType.DMA((2 * n,)), pltpu.SemaphoreType.DMA((6 * n,)), pltpu.SemaphoreType.DMA((6 * n,))]


def _gather_steps(ins, outs, lsem, ssem, rsem):
    n = len(ins)
    x, y, c = _mesh_pos()
    me = 2 * x + y
    sib = (x, y, 1 - c)
    chips = _other_chips(x, y)

    def ici(k, d):
        return pltpu.make_async_remote_copy(
            ins[k].at[c], outs[k].at[c, me], ssem.at[6 * k + d], rsem.at[6 * k + d],
            device_id=(chips[d][1], chips[d][2], c), device_id_type=MESH_ID)

    def landed(k, d):
        return pltpu.make_async_remote_copy(
            ins[k].at[c], outs[k].at[c, chips[d][0]], ssem.at[6 * k + d], rsem.at[6 * k + d],
            device_id=sib, device_id_type=MESH_ID)

    def fwd(k, d, half):
        return pltpu.make_async_remote_copy(
            outs[k].at[half, chips[d][0]], outs[k].at[half, chips[d][0]], ssem.at[6 * k + 3 + d],
            rsem.at[6 * k + 3 + d], device_id=sib, device_id_type=MESH_ID)

    def local(k, h):
        return pltpu.make_async_copy(ins[k].at[h], outs[k].at[h, me], lsem.at[2 * k + h])

    def start():
        for k in range(n):
            for h in range(2):
                local(k, h).start()
            for d in range(3):
                ici(k, d).start()

    def mid():
        for d in range(3):
            for k in range(n):
                landed(k, d).wait_recv()
                fwd(k, d, c).start()

    def end():
        for d in range(3):
            for k in range(n):
                fwd(k, d, 1 - c).wait_recv()
        for k in range(n):
            for d in range(3):
                ici(k, d).wait_send()
                fwd(k, d, c).wait_send()
            for h in range(2):
                local(k, h).wait()

    return start, mid, end


def _gather_rider(shards):
    return _Rider(shards, _gathered_shapes(shards), _gather_sems(len(shards)), _gather_steps)


def _pair_rider(arrs, other_half):
    n = len(arrs)

    def steps(ins, outs, ssem, rsem):
        x, y, c = _mesh_pos()

        def copy(k):
            return pltpu.make_async_remote_copy(ins[k].at[:, 1 - c] if other_half else ins[k], outs[k], ssem.at[k],
                                                rsem.at[k], device_id=(x, y, 1 - c), device_id_type=MESH_ID)

        def start():
            for k in range(n):
                copy(k).start()

        def end():
            for k in range(n):
                copy(k).wait()

        return start, end

    shapes = [jax.ShapeDtypeStruct(a.shape[:1] + a.shape[2:] if other_half else a.shape, a.dtype) for a in arrs]
    return _Rider(arrs, shapes, [pltpu.SemaphoreType.DMA((n,)), pltpu.SemaphoreType.DMA((n,))], steps)


def _chip_rider(arrs, broadcast):
    n = len(arrs)

    def steps(ins, outs, lsem, ssem, rsem):
        x, y, c = _mesh_pos()
        me = 2 * x + y

        def copies():
            cps = [pltpu.make_async_copy(ins[k] if broadcast else ins[k].at[me], outs[k].at[me], lsem.at[k])
                   for k in range(n)]
            for k in range(n):
                for d, (j, tx, ty) in enumerate(_other_chips(x, y)):
                    cps.append(pltpu.make_async_remote_copy(
                        ins[k] if broadcast else ins[k].at[j], outs[k].at[me], ssem.at[3 * k + d], rsem.at[3 * k + d],
                        device_id=(tx, ty, c), device_id_type=MESH_ID))
            return cps

        def start():
            for cp in copies():
                cp.start()

        def end():
            for cp in copies():
                cp.wait()

        return start, end

    shapes = [jax.ShapeDtypeStruct(((N_CHIPS,) + a.shape) if broadcast else a.shape, a.dtype) for a in arrs]
    sems = [pltpu.SemaphoreType.DMA((n,)), pltpu.SemaphoreType.DMA((3 * n,)), pltpu.SemaphoreType.DMA((3 * n,))]
    return _Rider(arrs, shapes, sems, steps)


def _run_rider(name, rider):
    n, m = len(rider.arrs), len(rider.out_shapes)

    def body(*refs):
        for step in rider.steps(refs[:n], refs[n:n + m], *refs[n + m:]):
            step()

    return pl.pallas_call(
        body, name=name, in_specs=[_ANY] * n, out_specs=[_ANY] * m, out_shape=rider.out_shapes,
        scratch_shapes=rider.sems,
    )(*rider.arrs)


SMALL = ("norm_g", "lam_re", "lam_im", "b_re", "b_im", "c_re", "c_im", "d_skip", "log_dt", "b_glu", "ln_g", "ln_b",
         "w_s", "b_s", "w_pool", "pool_scale", "final_g")
BIG = ("w_in", "w_glu", "w_out")
WEIGHTS = ("norm_g", "w_in", "lam_re", "lam_im", "b_re", "b_im", "c_re", "c_im", "d_skip", "log_dt", "w_glu", "b_glu",
           "ln_g", "ln_b", "w_s", "b_s", "w_pool", "pool_scale", "w_out", "final_g")
PACK_UNIT = 8 * 128
PACK_ROWS = 1024


def _pack(arrs):
    parts, total = [], 0
    for a in arrs:
        f = a.reshape(-1).astype(F32)
        pad = (-f.shape[0]) % PACK_UNIT
        parts.append(jnp.pad(f, (0, pad)) if pad else f)
        total += f.shape[0] + pad
    tail = (-total) % (PACK_ROWS * 128)
    if tail:
        parts.append(jnp.zeros((tail,), F32))
    return jnp.concatenate(parts).reshape(-1, 128)


def _unpack(buf, like):
    flat = buf.reshape(-1)
    out, off = [], 0
    for a in like:
        n = math.prod(a.shape)
        out.append(flat[off:off + n].reshape(a.shape))
        off += n + ((-n) % PACK_UNIT)
    return out


def _layer_params(l, wt, g_glu):
    a_re, a_im, bb_re, bb_im = _s5_prep(wt["lam_re"][l], wt["lam_im"][l], wt["b_re"][l], wt["b_im"][l], wt["log_dt"][l])
    b4re, b4im = _block_diag_in(bb_re), _block_diag_in(bb_im)
    c4re, c4im = _block_diag_out(wt["c_re"][l]), _block_diag_out(-wt["c_im"][l])
    tr = lambda m: jnp.swapaxes(m, 1, 2).astype(BF16)
    causal = jnp.tril(jnp.ones((CHUNK, CHUNK), dtype=bool))
    ws = jnp.where(causal[None], wt["w_s"][l], 0.0)
    wglu = g_glu[l].reshape(S5_W, S5_W)
    return dict(
        b4re=b4re.astype(BF16), b4im=b4im.astype(BF16), c4re=c4re.astype(BF16), c4im=c4im.astype(BF16),
        b4re_t=tr(b4re), b4im_t=tr(b4im), c4re_t=tr(c4re), c4im_t=tr(c4im),
        dvec=wt["d_skip"][l].reshape(1, S5_W), wglu=wglu, wglu_t=wglu.T, bglu=wt["b_glu"][l].reshape(1, S5_W),
        coef_f=_scan_coefs(a_re, a_im, False), coef_r=_scan_coefs(a_re, a_im, True),
        ws=ws.astype(BF16), ws_t=tr(ws),
        bsf=jnp.broadcast_to(wt["b_s"][l][:, None, :], (SGU_HEADS, CHUNK, CHUNK)).transpose(2, 0, 1).reshape(CHUNK, SGU_W),
        lng=wt["ln_g"][l].reshape(1, SGU_W), lnb=wt["ln_b"][l].reshape(1, SGU_W),
        wp=wt["w_pool"][l].astype(BF16), wp_t=tr(wt["w_pool"][l]), scale=wt["pool_scale"][l].reshape(1, POOL_W),
        norm_g=wt["norm_g"][l].reshape(1, D_MODEL),
    )


def _local_step(x0, tgt, wt, g_in0, rest, rest_gathered, cidx=None, order=None):
    dist = cidx is not None
    xs, saved, params = [x0], [], []
    for l in range(DEPTH):
        norm_g = wt["norm_g"][l].reshape(1, D_MODEL)
        out_rider = None
        if l == 0 and not rest_gathered:
            w_in1, w_glu_b, w_out_b = rest
            h = _rms_h(xs[-1], norm_g)
            z, g_in0, g_glu, g_out = _inproj_first(h, g_in0, order, [w_glu_b, w_out_b])
            out_rider = _gather_rider([w_in1])
        elif l == 0:
            g_in1, g_glu, g_out = rest
            z, h = _inproj(xs[-1], norm_g, g_in0)
        else:
            z, h = _inproj(xs[-1], norm_g, g_in1)
        p = _layer_params(l, wt, g_glu)
        params.append(p)
        ya, yraw, sre, sim, *gathered = _s5_fwd(z, p, out_rider)
        if gathered:
            (g_in1,) = gathered
        yb = _sgu_fwd(z, p["ws"], p["bsf"], p["lng"], p["lnb"])
        yc = _pool_fwd(z, p["wp"], p["scale"])
        w_out = g_out[l].reshape(D_MODEL, D_MODEL)
        if l < DEPTH - 1:
            xn, y = _outproj(ya, yb, yc, w_out, xs[-1])
            xs.append(xn)
        else:
            dx, loss, dfg, y = _outproj_loss(ya, yb, yc, w_out, xs[-1], wt["final_g"].reshape(1, D_MODEL), tgt)
        saved.append((z, h, yraw, sre, sim, y))
    g_in = (g_in0, g_in1)

    gr = {k: [None] * DEPTH for k in WEIGHTS if k != "final_g"}
    mine, theirs, chip_sum = [None] * DEPTH, [None] * DEPTH, None
    halves = lambda a, rows: a.reshape(N_CHIPS, 2, rows // 2, a.shape[-1])
    for l in reversed(range(DEPTH)):
        p = params[l]
        z, h, yraw, sre, sim, y = saved[l]
        w_out = g_out[l].reshape(D_MODEL, D_MODEL)
        dy = _outproj_bwd_dy(dx, w_out)
        gr["w_out"][l] = _outproj_bwd_dw(y, dx)
        ride_c = _chip_rider(chip_sum, False) if dist and l == 0 else None
        dz, dbre, dbim, dcre, dcim, dd, dwg, dbg, da, *landed = _s5_bwd(dy, z, yraw, sre, sim, p, ride_c)
        if ride_c:
            mine[1] = [_sum_chips(r) for r in landed]
        ride_e = _pair_rider(mine[1], False) if dist and l == 0 else None
        dz, dws, dbsf, dlng, dlnb, *got = _sgu_bwd(dy, z, dz, p["ws"], p["ws_t"], p["bsf"], p["lng"], p["lnb"], ride_e)
        if ride_e:
            theirs[1] = got
        dz, dwp, dsc = _pool_bwd(dy, z, dz, p["wp"], p["wp_t"], p["scale"])
        gr["w_in"][l] = _inproj_bwd_dw(h, dz)
        if not dist:
            dx, dng = _inproj_bwd_dx(dz, g_in[l], xs[l], p["norm_g"], dx)
        else:
            part = [halves(gr["w_in"][l], D_MODEL), halves(dwg, S5_W // N_CHIPS),
                    halves(gr["w_out"][l], D_MODEL // N_CHIPS)]
            ride_a = _pair_rider(part, True)
            if l == 1:
                dx, dng, *from_sib = _inproj_bwd_dx(dz, g_in[l], xs[l], p["norm_g"], dx, ride_a)
                chip_sum = [_add_own_half(a, r, cidx) for a, r in zip(part, from_sib)]
            else:
                nt = x0.shape[0] // _dx_tile(x0.shape[0])
                n_top = max(nt // 4, 1)
                dx_top, dng_top, *from_sib = _inproj_bwd_dx(dz, g_in[l], xs[l], p["norm_g"], dx, ride_a,
                                                            tiles=(0, n_top))
                chip_sum0 = [_add_own_half(a, r, cidx) for a, r in zip(part, from_sib)]
                dx, dng_rest, *landed = _inproj_bwd_dx(dz, g_in[l], xs[l], p["norm_g"], dx,
                                                       _chip_rider(chip_sum0, False), tiles=(n_top, nt - n_top),
                                                       prev=dx_top)
                dng = dng_top + dng_rest
                mine[0] = [_sum_chips(r) for r in landed]

        raw = (wt["lam_re"][l], wt["lam_im"][l], wt["b_re"][l], wt["b_im"][l], wt["log_dt"][l])
        _, vjp = jax.vjp(_s5_prep, *raw)
        da = jnp.sum(da, axis=1)
        cot = (da[0].reshape(S5_GROUPS, S5_STATE), da[1].reshape(S5_GROUPS, S5_STATE),
               _block_diag_in_grad(dbre), _block_diag_in_grad(dbim))
        gr["lam_re"][l], gr["lam_im"][l], gr["b_re"][l], gr["b_im"][l], gr["log_dt"][l] = vjp(cot)
        gr["c_re"][l] = _block_diag_out_grad(dcre)
        gr["c_im"][l] = -_block_diag_out_grad(dcim)
        gr["d_skip"][l] = dd.reshape(S5_GROUPS, S5_CH)
        gr["w_glu"][l] = dwg
        gr["b_glu"][l] = dbg.reshape(S5_W)
        causal = jnp.tril(jnp.ones((CHUNK, CHUNK), dtype=bool))
        gr["w_s"][l] = jnp.where(causal[None], dws, 0.0)
        gr["b_s"][l] = dbsf.reshape(CHUNK, SGU_HEADS, CHUNK).sum(-1).T
        gr["ln_g"][l] = dlng.reshape(SGU_W)
        gr["ln_b"][l] = dlnb.reshape(SGU_W)
        gr["w_pool"][l] = dwp
        gr["pool_scale"][l] = dsc.reshape(POOL_W)
        gr["norm_g"][l] = dng.reshape(D_MODEL)

    grads = {k: (v if k in BIG else jnp.stack(v)) for k, v in gr.items()}
    grads["final_g"] = dfg.reshape(D_MODEL)
    if dist:
        for i, k in enumerate(BIG):
            grads[k] = ([mine[l][i] for l in range(DEPTH)], [None, theirs[1][i]])
    return loss, dx, grads


def kernel(x, norm_g, w_in, lam_re, lam_im, b_re, b_im, c_re, c_im, d_skip, log_dt, w_glu, b_glu, ln_g, ln_b, w_s, b_s, w_pool, pool_scale, w_out, final_g, loss_target, m_norm_g, m_w_in, m_lam_re, m_lam_im, m_b_re, m_b_im, m_c_re, m_c_im, m_d_skip, m_log_dt, m_w_glu, m_b_glu, m_ln_g, m_ln_b, m_w_s, m_b_s, m_w_pool, m_pool_scale, m_w_out, m_final_g, v_norm_g, v_w_in, v_lam_re, v_lam_im, v_b_re, v_b_im, v_c_re, v_c_im, v_d_skip, v_log_dt, v_w_glu, v_b_glu, v_ln_g, v_ln_b, v_w_s, v_b_s, v_w_pool, v_pool_scale, v_w_out, v_final_g):
    wt = dict(norm_g=norm_g, w_in=w_in, lam_re=lam_re, lam_im=lam_im, b_re=b_re, b_im=b_im, c_re=c_re, c_im=c_im,
              d_skip=d_skip, log_dt=log_dt, w_glu=w_glu, b_glu=b_glu, ln_g=ln_g, ln_b=ln_b, w_s=w_s, b_s=b_s,
              w_pool=w_pool, pool_scale=pool_scale, w_out=w_out, final_g=final_g)
    mom = dict(norm_g=m_norm_g, w_in=m_w_in, lam_re=m_lam_re, lam_im=m_lam_im, b_re=m_b_re, b_im=m_b_im, c_re=m_c_re,
               c_im=m_c_im, d_skip=m_d_skip, log_dt=m_log_dt, w_glu=m_w_glu, b_glu=m_b_glu, ln_g=m_ln_g, ln_b=m_ln_b,
               w_s=m_w_s, b_s=m_b_s, w_pool=m_w_pool, pool_scale=m_pool_scale, w_out=m_w_out, final_g=m_final_g)
    vel = dict(norm_g=v_norm_g, w_in=v_w_in, lam_re=v_lam_re, lam_im=v_lam_im, b_re=v_b_re, b_im=v_b_im, c_re=v_c_re,
               c_im=v_c_im, d_skip=v_d_skip, log_dt=v_log_dt, w_glu=v_w_glu, b_glu=v_b_glu, ln_g=v_ln_g, ln_b=v_ln_b,
               w_s=v_w_s, b_s=v_b_s, w_pool=v_w_pool, pool_scale=v_pool_scale, w_out=v_w_out, final_g=v_final_g)
    T = x.shape[1]
    cidx = lax.axis_index("c").astype(jnp.int32).reshape(1)

    w_in_b = w_in.astype(BF16)
    w0 = w_in_b[0].reshape(2, HALF_D, SHARD_COLS)
    rest = (w_in_b[1].reshape(2, HALF_D, SHARD_COLS), w_glu.astype(BF16), w_out.astype(BF16))
    mx, my = lax.axis_index("x"), lax.axis_index("y")
    order = jnp.stack([2 * mx + my] + [j for j, _, _ in _other_chips(mx, my)]).astype(jnp.int32)
    loss, grad_x, grads = _local_step(x.reshape(T, D_MODEL), loss_target.reshape(T, D_MODEL), wt, w0, rest, False,
                                      cidx, order)

    packed = _pack([grads[k] for k in SMALL] + [loss[0, 0:1]])
    sib_packed, *theirs0 = _run_rider("pair_exchange", _pair_rider([packed] + [grads[k][0][0] for k in BIG], False))
    for k, t in zip(BIG, theirs0):
        grads[k][1][0] = t
    chip_packed = _add2(packed, sib_packed)
    half_rows = chip_packed.shape[0] // 2
    my_half = lax.dynamic_index_in_dim(chip_packed.reshape(2, half_rows, 128), cidx[0], 0, keepdims=False)
    small_ride = _chip_rider([my_half], True)

    out_g, out_d, out_m, out_v = {}, {}, {}, {}
    all_half = None
    for k in BIG:
        shape = wt[k].shape
        quad = lambda t: t.reshape(2, 2, shape[1] // 2, shape[2])
        g, d, m, v, *landed = _adamw_halves(quad(wt[k]), grads[k][0], grads[k][1], quad(mom[k]), quad(vel[k]), cidx,
                                            small_ride if k == BIG[0] else None)
        if landed:
            (all_half,) = landed
        out_g[k], out_d[k], out_m[k], out_v[k] = (t.reshape(shape) for t in (g, d, m, v))

    mine_half = _sum_chips(all_half)
    (their_half,) = _run_rider("small_result_exchange", _pair_rider([mine_half], False))
    total = jnp.where(cidx[0] == 0, jnp.concatenate([mine_half, their_half]), jnp.concatenate([their_half, mine_half]))
    like = [wt[k] for k in SMALL]
    small_g = _unpack(total, like + [loss[0, 0:1]])
    loss_out = small_g[-1].reshape(())
    w_p, m_p, v_p = _pack(like), _pack([mom[k] for k in SMALL]), _pack([vel[k] for k in SMALL])
    d_p, mo_p, vo_p = _adamw(w_p, total, m_p, v_p)
    for k, g, d, m, v in zip(SMALL, small_g[:-1], _unpack(d_p, like), _unpack(mo_p, like), _unpack(vo_p, like)):
        out_g[k], out_d[k], out_m[k], out_v[k] = g, d, m, v

    return (loss_out, grad_x.reshape(x.shape), *[out_g[k] for k in WEIGHTS], *[out_d[k] for k in WEIGHTS],
            *[out_m[k] for k in WEIGHTS], *[out_v[k] for k in WEIGHTS])
```

```python
import functools
import math

import jax
import jax.numpy as jnp
from jax import lax
from jax.experimental import pallas as pl
from jax.experimental.pallas import tpu as pltpu

F32 = jnp.float32
BF16 = jnp.bfloat16

D_MODEL = 2048
DEPTH = 2
S5_W = 512
SGU_W = 1024
POOL_W = 512
IN_COLS = 5120
N_CHIPS = 4
SHARD_COLS = IN_COLS // N_CHIPS
S5_GROUPS = 32
S5_STATE = 64
S5_CH = 16
STATE_W = S5_GROUPS * S5_STATE
SUPER = 4
CHUNK = 128
SGU_HEADS = 8
POOL_WINDOWS = (2, 4, 8, 16)
POOL_HALO = 16
RMS_EPS = 1e-6
LN_EPS = 1e-5
SCAN_COLS = 512

ADAM_LR = 0.001
ADAM_B1 = 0.9
ADAM_B2 = 0.999
ADAM_EPS = 1e-08
ADAM_WD = 0.01
ADAM_STEP = 10

VMEM_LIMIT = 56 * 1024 * 1024
MESH_ID = pl.DeviceIdType.MESH

_GELU_K0 = math.sqrt(2.0 / math.pi)
_GELU_K1 = 0.044715


def _cparams(n_axes):
    return pltpu.CompilerParams(dimension_semantics=("arbitrary",) * n_axes, vmem_limit_bytes=VMEM_LIMIT)


def _gelu(x):
    t = jnp.tanh(_GELU_K0 * (x + _GELU_K1 * (x * x * x)))
    return 0.5 * x * (1.0 + t)


def _gelu_and_grad(x):
    x2 = x * x
    t = jnp.tanh(_GELU_K0 * (x + _GELU_K1 * (x * x2)))
    g = 0.5 * x * (1.0 + t)
    dg = 0.5 * (1.0 + t) + 0.5 * x * (1.0 - t * t) * (_GELU_K0 * (1.0 + 3.0 * _GELU_K1 * x2))
    return g, dg


def _silu_and_grad(x):
    s = jax.nn.sigmoid(x)
    return x * s, s * (1.0 + x * (1.0 - s))


def _dot(a, b):
    return jnp.dot(a.astype(BF16), b.astype(BF16), preferred_element_type=F32)


def _dot_nt(a, b):
    return lax.dot_general(a.astype(BF16), b.astype(BF16), (((1,), (1,)), ((), ())), preferred_element_type=F32)


def _dot_tn(a, b):
    return lax.dot_general(a.astype(BF16), b.astype(BF16), (((0,), (0,)), ((), ())), preferred_element_type=F32)


def _full(shape):
    nd = len(shape)
    return pl.BlockSpec(shape, lambda *_: (0,) * nd)


class _Rider:
    def __init__(self, arrs, out_shapes, sems, steps):
        self.arrs, self.out_shapes, self.sems, self.steps = list(arrs), list(out_shapes), list(sems), steps


def _ride(body, n_in, n_out, rider, first, last, middle=None):
    if rider is None:
        return body
    ri, ro, ns = len(rider.arrs), len(rider.out_shapes), len(rider.sems)

    def wrapped(*refs):
        o0 = n_in + ri
        steps = rider.steps(refs[n_in:o0], refs[o0 + n_out:o0 + n_out + ro], *refs[len(refs) - ns:])
        pl.when(first())(steps[0])
        if len(steps) == 3:
            pl.when(middle())(steps[1])
        body(*refs[:n_in], *refs[o0:o0 + n_out], *refs[o0 + n_out + ro:len(refs) - ns])
        pl.when(last())(steps[-1])

    return wrapped


class _ColumnWriter:
    def __init__(self, stage_ref, sem_ref, dst_ref, col0, step, n_steps):
        self.stage, self.sem, self.dst, self.col0, self.step, self.n = stage_ref, sem_ref, dst_ref, col0, step, n_steps
        self.tm, self.w = stage_ref.shape[1], stage_ref.shape[2]

    def _copy(self, slot, row0):
        return pltpu.make_async_copy(self.stage.at[slot],
                                     self.dst.at[pl.ds(row0, self.tm), pl.ds(self.col0, self.w)], self.sem.at[slot])

    def slot(self):
        s = self.step % 2

        @pl.when(self.step >= 2)
        def _():
            self._copy(s, 0).wait()

        return self.stage.at[s]

    def send(self, row0):
        s = self.step % 2
        self._copy(s, row0).start()

        @pl.when(self.step == self.n - 1)
        def _():
            self._copy(s, 0).wait()
            if self.n >= 2:
                self._copy(1 - s, 0).wait()


def _stage_scratch(tm, widths):
    return ([pltpu.VMEM((2, tm, w), BF16) for w in widths], [pltpu.SemaphoreType.DMA((2,)) for _ in widths])


def _rider_specs(rider):
    if rider is None:
        return [], [], [], [], []
    anyspec = pl.BlockSpec(memory_space=pl.ANY)
    return ([anyspec] * len(rider.arrs), [anyspec] * len(rider.out_shapes), rider.out_shapes, rider.sems, rider.arrs)


HALF_D = D_MODEL // 2


def _inproj(x, g, w):
    T = x.shape[0]
    tm = min(512, T)

    def body(x_ref, g_ref, w_ref, z_ref, h_ref, hs_ref):
        @pl.when(pl.program_id(1) == 0)
        def _():
            xv = x_ref[...]
            r = lax.rsqrt(jnp.mean(xv * xv, axis=-1, keepdims=True) + RMS_EPS)
            hv = (xv * r * g_ref[...]).astype(BF16)
            hs_ref[...] = hv
            h_ref[...] = hv

        z_ref[...] = (jnp.dot(hs_ref[:, 0:HALF_D], w_ref[0], preferred_element_type=F32)
                      + jnp.dot(hs_ref[:, HALF_D:D_MODEL], w_ref[1], preferred_element_type=F32)).astype(BF16)

    return pl.pallas_call(
        body,
        name="inproj",
        grid=(T // tm, N_CHIPS),
        in_specs=[
            pl.BlockSpec((tm, D_MODEL), lambda i, j: (i, 0)),
            pl.BlockSpec((1, D_MODEL), lambda i, j: (0, 0)),
            pl.BlockSpec((2, None, HALF_D, SHARD_COLS), lambda i, j: (0, j, 0, 0)),
        ],
        out_specs=[
            pl.BlockSpec((tm, SHARD_COLS), lambda i, j: (i, j)),
            pl.BlockSpec((tm, D_MODEL), lambda i, j: (i, 0)),
        ],
        out_shape=[jax.ShapeDtypeStruct((T, IN_COLS), BF16), jax.ShapeDtypeStruct((T, D_MODEL), BF16)],
        scratch_shapes=[pltpu.VMEM((tm, D_MODEL), BF16)],
        compiler_params=_cparams(2),
    )(x, g, w)


def _rms_h(x, g):
    T = x.shape[0]
    tm = min(512, T)

    def body(x_ref, g_ref, h_ref):
        xv = x_ref[...]
        r = lax.rsqrt(jnp.mean(xv * xv, axis=-1, keepdims=True) + RMS_EPS)
        h_ref[...] = (xv * r * g_ref[...]).astype(BF16)

    return pl.pallas_call(
        body, name="rms_h", grid=(T // tm,),
        in_specs=[pl.BlockSpec((tm, D_MODEL), lambda i: (i, 0)), pl.BlockSpec((1, D_MODEL), lambda i: (0, 0))],
        out_specs=pl.BlockSpec((tm, D_MODEL), lambda i: (i, 0)),
        out_shape=jax.ShapeDtypeStruct((T, D_MODEL), BF16), compiler_params=_cparams(1),
    )(x, g)


def _inproj_first(h, w0, order, riders):
    T = h.shape[0]
    tm = min(512, T)
    ni = T // tm
    n = len(riders)

    def body(order_ref, h_ref, w0_ref, *refs):
        rin = refs[:n]
        z_ref, gin_ref = refs[n:n + 2]
        rout = refs[n + 2:2 * n + 2]
        wbuf, csem, lsem, ssem, rsem = refs[2 * n + 2:2 * n + 7]
        s, i = pl.program_id(0), pl.program_id(1)
        x, y, c = _mesh_pos()
        me = 2 * x + y
        sib = (x, y, 1 - c)
        chips = _other_chips(x, y)
        if n:
            r_start, r_mid, r_end = _gather_steps(rin, rout, *refs[2 * n + 7:])

        def ici(d):
            return pltpu.make_async_remote_copy(w0_ref.at[c], gin_ref.at[c, me], ssem.at[d], rsem.at[d],
                                                device_id=(chips[d][1], chips[d][2], c), device_id_type=MESH_ID)

        def landed(d):
            return pltpu.make_async_remote_copy(w0_ref.at[c], gin_ref.at[c, chips[d][0]], ssem.at[d], rsem.at[d],
                                                device_id=sib, device_id_type=MESH_ID)

        def fwd(d, half):
            blk = gin_ref.at[half, chips[d][0]]
            return pltpu.make_async_remote_copy(blk, blk, ssem.at[3 + d], rsem.at[3 + d], device_id=sib,
                                                device_id_type=MESH_ID)

        def local(hf):
            return pltpu.make_async_copy(w0_ref.at[hf], gin_ref.at[hf, me], lsem.at[hf])

        def load(src):
            cp = pltpu.make_async_copy(src, wbuf, csem.at[0])
            cp.start()
            cp.wait()

        @pl.when((s == 0) & (i == 0))
        def _():
            for d in range(3):
                ici(d).start()
            local(0).start()
            local(1).start()
            load(w0_ref)

        for d in range(3):
            @pl.when((s == d + 1) & (i == 0))
            def _(d=d):
                landed(d).wait_recv()
                fwd(d, c).start()
                fwd(d, 1 - c).wait_recv()
                load(gin_ref.at[:, chips[d][0]])
                if d == 1 and n:
                    r_start()

        z_ref[...] = (jnp.dot(h_ref[:, 0:HALF_D], wbuf[0], preferred_element_type=F32)
                      + jnp.dot(h_ref[:, HALF_D:D_MODEL], wbuf[1], preferred_element_type=F32)).astype(BF16)

        @pl.when((s == N_CHIPS - 1) & (i == ni - 1))
        def _():
            for d in range(3):
                ici(d).wait_send()
                fwd(d, c).wait_send()
            local(0).wait()
            local(1).wait()
            if n:
                r_mid()
                r_end()

    anyspec = pl.BlockSpec(memory_space=pl.ANY)
    return pl.pallas_call(
        body,
        name="inproj_first",
        grid_spec=pltpu.PrefetchScalarGridSpec(
            num_scalar_prefetch=1,
            grid=(N_CHIPS, ni),
            in_specs=[pl.BlockSpec((tm, D_MODEL), lambda s, i, o: (i, 0)), anyspec] + [anyspec] * n,
            out_specs=[pl.BlockSpec((tm, SHARD_COLS), lambda s, i, o: (i, o[s])), anyspec] + [anyspec] * n,
            scratch_shapes=[pltpu.VMEM((2, HALF_D, SHARD_COLS), BF16), pltpu.SemaphoreType.DMA((1,)),
                            pltpu.SemaphoreType.DMA((2,)), pltpu.SemaphoreType.DMA((6,)),
                            pltpu.SemaphoreType.DMA((6,))] + (_gather_sems(n) if n else []),
        ),
        out_shape=[jax.ShapeDtypeStruct((T, IN_COLS), BF16),
                   jax.ShapeDtypeStruct((2, N_CHIPS, HALF_D, SHARD_COLS), BF16)] + _gathered_shapes(riders),
        compiler_params=_cparams(2),
    )(order, h, w0, *riders)


def _outproj(ya, yb, yc, w, x):
    T = x.shape[0]
    tm = min(512, T)
    tn = 1024

    def body(ya_ref, yb_ref, yc_ref, w_ref, x_ref, o_ref, y_ref):
        acc = jnp.dot(ya_ref[...], w_ref[0:S5_W, :], preferred_element_type=F32)
        acc += jnp.dot(yb_ref[...], w_ref[S5_W:S5_W + SGU_W, :], preferred_element_type=F32)
        acc += jnp.dot(yc_ref[...], w_ref[S5_W + SGU_W:D_MODEL, :], preferred_element_type=F32)
        o_ref[...] = x_ref[...] + acc

        @pl.when(pl.program_id(1) == 0)
        def _():
            y_ref[:, 0:S5_W] = ya_ref[...]
            y_ref[:, S5_W:S5_W + SGU_W] = yb_ref[...]
            y_ref[:, S5_W + SGU_W:D_MODEL] = yc_ref[...]

    return pl.pallas_call(
        body,
        name="outproj",
        grid=(T // tm, D_MODEL // tn),
        in_specs=[
            pl.BlockSpec((tm, S5_W), lambda i, j: (i, 0)),
            pl.BlockSpec((tm, SGU_W), lambda i, j: (i, 0)),
            pl.BlockSpec((tm, POOL_W), lambda i, j: (i, 0)),
            pl.BlockSpec((D_MODEL, tn), lambda i, j: (0, j)),
            pl.BlockSpec((tm, tn), lambda i, j: (i, j)),
        ],
        out_specs=[
            pl.BlockSpec((tm, tn), lambda i, j: (i, j)),
            pl.BlockSpec((tm, D_MODEL), lambda i, j: (i, 0)),
        ],
        out_shape=[jax.ShapeDtypeStruct((T, D_MODEL), F32), jax.ShapeDtypeStruct((T, D_MODEL), BF16)],
        compiler_params=_cparams(2),
    )(ya, yb, yc, w, x)


def _outproj_bwd_dy(dxo, w):
    T = dxo.shape[0]
    tm = min(512, T)
    tn = 1024

    def body(d_ref, w_ref, o_ref, ds_ref):
        @pl.when(pl.program_id(1) == 0)
        def _():
            ds_ref[...] = d_ref[...].astype(BF16)

        o_ref[...] = lax.dot_general(ds_ref[...], w_ref[...], (((1,), (1,)), ((), ())), preferred_element_type=F32)

    return pl.pallas_call(
        body,
        name="outproj_bwd_dy",
        grid=(T // tm, D_MODEL // tn),
        in_specs=[
            pl.BlockSpec((tm, D_MODEL), lambda i, j: (i, 0)),
            pl.BlockSpec((tn, D_MODEL), lambda i, j: (j, 0)),
        ],
        out_specs=pl.BlockSpec((tm, tn), lambda i, j: (i, j)),
        out_shape=jax.ShapeDtypeStruct((T, D_MODEL), F32),
        scratch_shapes=[pltpu.VMEM((tm, D_MODEL), BF16)],
        compiler_params=_cparams(2),
    )(dxo, w)


def _outproj_bwd_dw(y, dxo):
    T = y.shape[0]
    tm = min(512, T)
    tr = 1024

    def body(y_ref, d_ref, o_ref):
        @pl.when(pl.program_id(1) == 0)
        def _():
            o_ref[...] = jnp.zeros_like(o_ref)

        o_ref[...] += _dot_tn(y_ref[...], d_ref[...])

    return pl.pallas_call(
        body,
        name="outproj_bwd_dw",
        grid=(D_MODEL // tr, T // tm),
        in_specs=[
            pl.BlockSpec((tm, tr), lambda p, t: (t, p)),
            pl.BlockSpec((tm, D_MODEL), lambda p, t: (t, 0)),
        ],
        out_specs=pl.BlockSpec((tr, D_MODEL), lambda p, t: (p, 0)),
        out_shape=jax.ShapeDtypeStruct((D_MODEL, D_MODEL), F32),
        compiler_params=_cparams(2),
    )(y, dxo)


def _inproj_bwd_dw(h, dz):
    T = h.shape[0]
    tm = min(512, T)

    def body(h_ref, dz_ref, o_ref):
        @pl.when(pl.program_id(1) == 0)
        def _():
            o_ref[...] = jnp.zeros_like(o_ref)

        o_ref[...] += _dot_tn(h_ref[...], dz_ref[...])

    return pl.pallas_call(
        body,
        name="inproj_bwd_dw",
        grid=(N_CHIPS, T // tm),
        in_specs=[
            pl.BlockSpec((tm, D_MODEL), lambda j, t: (t, 0)),
            pl.BlockSpec((tm, SHARD_COLS), lambda j, t: (t, j)),
        ],
        out_specs=pl.BlockSpec((None, D_MODEL, SHARD_COLS), lambda j, t: (j, 0, 0)),
        out_shape=jax.ShapeDtypeStruct((N_CHIPS, D_MODEL, SHARD_COLS), F32),
        compiler_params=_cparams(2),
    )(h, dz)


def _dx_tile(T):
    return min(512, max(T // 4, 8))


def _inproj_bwd_dx(dz, w4, x, g, dxo, rider=None, tiles=None, prev=None):
    T = x.shape[0]
    tm = _dx_tile(T)
    t0, ni = tiles if tiles else (0, T // tm)
    nk = N_CHIPS
    nt = (((1,), (1,)), ((), ()))
    n_in = 5 if prev is None else 6

    def body(dz_ref, w_ref, x_ref, g_ref, dxo_ref, *rest):
        dx_ref, dg_ref, acc_ref = rest[-3:]
        i, j = pl.program_id(0), pl.program_id(1)
        lo = lax.dot_general(dz_ref[...], w_ref[0], nt, preferred_element_type=F32)
        hi = lax.dot_general(dz_ref[...], w_ref[1], nt, preferred_element_type=F32)

        @pl.when(j == 0)
        def _():
            acc_ref[:, 0:HALF_D] = lo
            acc_ref[:, HALF_D:D_MODEL] = hi

        @pl.when(j > 0)
        def _():
            acc_ref[:, 0:HALF_D] += lo
            acc_ref[:, HALF_D:D_MODEL] += hi

        @pl.when(j == nk - 1)
        def _():
            @pl.when(i == 0)
            def _():
                dg_ref[...] = jnp.zeros_like(dg_ref)

            rc = min(128, tm)
            for c in range(tm // rc):
                rows = slice(c * rc, (c + 1) * rc)
                dh = acc_ref[rows, :]
                xv = x_ref[rows, :]
                r = lax.rsqrt(jnp.mean(xv * xv, axis=-1, keepdims=True) + RMS_EPS)
                xh = xv * r
                w = dh * g_ref[...]
                dx_ref[rows, :] = dxo_ref[rows, :] + r * (w - xh * jnp.mean(w * xh, axis=-1, keepdims=True))
                dg_ref[...] += jnp.sum(dh * xh, axis=0, keepdims=True)

    r_in, r_out, r_shapes, r_sems, r_args = _rider_specs(rider)
    return pl.pallas_call(
        _ride(body, n_in, 2, rider, lambda: (pl.program_id(0) == 0) & (pl.program_id(1) == 0),
              lambda: (pl.program_id(0) == ni - 1) & (pl.program_id(1) == nk - 1)),
        name="inproj_bwd_dx" + ("" if rider is None else "_ride") + ("" if prev is None else "_rest"),
        grid=(ni, nk),
        in_specs=[
            pl.BlockSpec((tm, SHARD_COLS), lambda i, j: (i + t0, j)),
            pl.BlockSpec((2, None, HALF_D, SHARD_COLS), lambda i, j: (0, j, 0, 0)),
            pl.BlockSpec((tm, D_MODEL), lambda i, j: (i + t0, 0)),
            pl.BlockSpec((1, D_MODEL), lambda i, j: (0, 0)),
            pl.BlockSpec((tm, D_MODEL), lambda i, j: (i + t0, 0)),
        ] + ([] if prev is None else [pl.BlockSpec(memory_space=pl.ANY)]) + r_in,
        out_specs=[
            pl.BlockSpec((tm, D_MODEL), lambda i, j: (i + t0, 0)),
            pl.BlockSpec((1, D_MODEL), lambda i, j: (0, 0)),
        ] + r_out,
        out_shape=[jax.ShapeDtypeStruct((T, D_MODEL), F32), jax.ShapeDtypeStruct((1, D_MODEL), F32)] + r_shapes,
        scratch_shapes=[pltpu.VMEM((tm, D_MODEL), F32)] + r_sems,
        input_output_aliases={} if prev is None else {5: 0},
        compiler_params=_cparams(2),
    )(dz, w4, x, g, dxo, *([] if prev is None else [prev]), *r_args)


def _outproj_loss(ya, yb, yc, w, x, g, tgt):
    T = x.shape[0]
    tm = min(256, T)

    def body(ya_ref, yb_ref, yc_ref, w_ref, x_ref, g_ref, t_ref, dx_ref, l_ref, dg_ref, y_ref):
        i = pl.program_id(0)
        acc = jnp.dot(ya_ref[...], w_ref[0:S5_W, :], preferred_element_type=F32)
        acc += jnp.dot(yb_ref[...], w_ref[S5_W:S5_W + SGU_W, :], preferred_element_type=F32)
        acc += jnp.dot(yc_ref[...], w_ref[S5_W + SGU_W:D_MODEL, :], preferred_element_type=F32)
        y_ref[:, 0:S5_W] = ya_ref[...]
        y_ref[:, S5_W:S5_W + SGU_W] = yb_ref[...]
        y_ref[:, S5_W + SGU_W:D_MODEL] = yc_ref[...]
        xv = x_ref[...] + acc
        r = lax.rsqrt(jnp.mean(xv * xv, axis=-1, keepdims=True) + RMS_EPS)
        xh = xv * r
        err = xh * g_ref[...] - t_ref[...]
        lpart = 0.5 * jnp.sum(jnp.mean(err * err, axis=-1, keepdims=True), axis=0, keepdims=True)
        dout = err * (1.0 / D_MODEL)
        w = dout * g_ref[...]
        dx_ref[...] = r * (w - xh * jnp.mean(w * xh, axis=-1, keepdims=True))
        gpart = jnp.sum(dout * xh, axis=0, keepdims=True)

        @pl.when(i == 0)
        def _():
            l_ref[...] = jnp.broadcast_to(lpart, l_ref.shape)
            dg_ref[...] = gpart

        @pl.when(i > 0)
        def _():
            l_ref[...] += jnp.broadcast_to(lpart, l_ref.shape)
            dg_ref[...] += gpart

    row = lambda w: pl.BlockSpec((tm, w), lambda i: (i, 0))
    return pl.pallas_call(
        body,
        name="outproj_loss",
        grid=(T // tm,),
        in_specs=[row(S5_W), row(SGU_W), row(POOL_W), _full((D_MODEL, D_MODEL)), row(D_MODEL), _full((1, D_MODEL)),
                  row(D_MODEL)],
        out_specs=[row(D_MODEL), _full((1, 128)), _full((1, D_MODEL)), row(D_MODEL)],
        out_shape=[
            jax.ShapeDtypeStruct((T, D_MODEL), F32),
            jax.ShapeDtypeStruct((1, 128), F32),
            jax.ShapeDtypeStruct((1, D_MODEL), F32),
            jax.ShapeDtypeStruct((T, D_MODEL), BF16),
        ],
        compiler_params=_cparams(1),
    )(ya, yb, yc, w, x, g, tgt)


def _s5_prep(lam_re, lam_im, b_re, b_im, log_dt):
    lam = lax.complex(lam_re, lam_im)
    dt = jnp.exp(log_dt)[:, None]
    a = jnp.exp(lam * dt)
    bbar = ((a - 1.0) / lam)[..., None] * lax.complex(b_re, b_im)
    return jnp.real(a), jnp.imag(a), jnp.real(bbar), jnp.imag(bbar)


def _block_diag_in(m):
    m4 = m.reshape(SUPER, 8, S5_STATE, S5_CH)
    eye = jnp.eye(8, dtype=m.dtype)
    out = jnp.einsum("jgph,gk->jghkp", m4, eye)
    return out.reshape(SUPER, 8 * S5_CH, 8 * S5_STATE)


def _block_diag_in_grad(d):
    d6 = d.reshape(SUPER, 8, S5_CH, 8, S5_STATE)
    diag = jnp.einsum("jghgp->jgph", d6)
    return diag.reshape(S5_GROUPS, S5_STATE, S5_CH)


def _block_diag_out(m):
    m4 = m.reshape(SUPER, 8, S5_CH, S5_STATE)
    eye = jnp.eye(8, dtype=m.dtype)
    out = jnp.einsum("jghp,gk->jgpkh", m4, eye)
    return out.reshape(SUPER, 8 * S5_STATE, 8 * S5_CH)


def _block_diag_out_grad(d):
    d6 = d.reshape(SUPER, 8, S5_STATE, 8, S5_CH)
    diag = jnp.einsum("jgpgh->jghp", d6)
    return diag.reshape(S5_GROUPS, S5_CH, S5_STATE)


def _scan_coefs(a_re, a_im, reverse):
    a = lax.complex(a_re.reshape(-1), a_im.reshape(-1))
    if reverse:
        a = jnp.conj(a)
    pw = [a]
    for _ in range(7):
        pw.append(pw[-1] * a)
    rows = jnp.arange(8)

    def masked(k):
        m = (rows + k <= 7) if reverse else (rows >= k)
        return jnp.where(m[:, None], pw[k - 1][None, :], 0.0)

    a1, a2, a4 = masked(1), masked(2), masked(4)
    carry = jnp.stack([pw[7 - r] for r in range(8)]) if reverse else jnp.stack(pw)
    parts = []
    for c in (a1, a2, a4, carry):
        parts += [jnp.real(c), jnp.imag(c)]
    return jnp.stack(parts).astype(F32)


def _scan_block(r, im, coef_ref, cs, reverse):
    for k, idx in ((1, 0), (2, 2), (4, 4)):
        ar = coef_ref[idx, :, cs]
        ai = coef_ref[idx + 1, :, cs]
        sh = 8 - k if reverse else k
        rr = pltpu.roll(r, sh, 0)
        ri = pltpu.roll(im, sh, 0)
        r, im = r + ar * rr - ai * ri, im + ar * ri + ai * rr
    return r, im


def _s5_fwd(z, p, rider=None):
    T = z.shape[0]
    tm = min(512, T)
    nblk = tm // 8
    W = STATE_W

    def body(xa_ref, ga_ref, bre_ref, bim_ref, cre_ref, cim_ref, dv_ref, wg_ref, bg_ref, coef_ref,
             ya_ref, yraw_ref, sre_ref, sim_ref, wre, wim):
        @pl.when(pl.program_id(0) == 0)
        def _():
            wre[0:8, :] = jnp.zeros((8, W), F32)
            wim[0:8, :] = jnp.zeros((8, W), F32)

        xa = xa_ref[...].astype(F32)
        xab = xa.astype(BF16)
        for j in range(SUPER):
            xj = xab[:, j * 128:(j + 1) * 128]
            wre[8:8 + tm, j * 512:(j + 1) * 512] = jnp.dot(xj, bre_ref[j], preferred_element_type=F32)
            wim[8:8 + tm, j * 512:(j + 1) * 512] = jnp.dot(xj, bim_ref[j], preferred_element_type=F32)

        def blk(b, carry):
            base = pl.multiple_of(8 + b * 8, 8)
            for cc in range(W // SCAN_COLS):
                cs = pl.ds(cc * SCAN_COLS, SCAN_COLS)
                r, im = _scan_block(wre[pl.ds(base, 8), cs], wim[pl.ds(base, 8), cs], coef_ref, cs, False)
                cr = wre[pl.ds(base - 1, 1), cs]
                ci = wim[pl.ds(base - 1, 1), cs]
                pr = coef_ref[6, :, cs]
                pi = coef_ref[7, :, cs]
                wre[pl.ds(base, 8), cs] = r + pr * cr - pi * ci
                wim[pl.ds(base, 8), cs] = im + pr * ci + pi * cr
            return carry

        lax.fori_loop(0, nblk, blk, 0)
        wre[0:8, :] = wre[tm:tm + 8, :]
        wim[0:8, :] = wim[tm:tm + 8, :]
        sre_ref[...] = wre[8:8 + tm, :]
        sim_ref[...] = wim[8:8 + tm, :]

        for j in range(SUPER):
            yr = jnp.dot(wre[8:8 + tm, j * 512:(j + 1) * 512].astype(BF16), cre_ref[j], preferred_element_type=F32)
            yr += jnp.dot(wim[8:8 + tm, j * 512:(j + 1) * 512].astype(BF16), cim_ref[j], preferred_element_type=F32)
            yraw_ref[:, j * 128:(j + 1) * 128] = yr
        yraw = yraw_ref[...] + dv_ref[...] * xa
        yraw_ref[...] = yraw
        yg = _gelu(yraw)
        q = jnp.dot(yg.astype(BF16), wg_ref[...], preferred_element_type=F32) + bg_ref[...]
        sga, _ = _silu_and_grad(ga_ref[...].astype(F32))
        ya_ref[...] = (yg * jax.nn.sigmoid(q) * sga).astype(BF16)

    nt = T // tm
    r_in, r_out, r_shapes, r_sems, r_args = _rider_specs(rider)
    return pl.pallas_call(
        _ride(body, 10, 4, rider, lambda: pl.program_id(0) == 0, lambda: pl.program_id(0) == nt - 1,
              lambda: pl.program_id(0) == nt - 1),
        name="s5_fwd" + ("" if rider is None else "_ride"),
        grid=(nt,),
        in_specs=[
            pl.BlockSpec((tm, S5_W), lambda i: (i, 0)),
            pl.BlockSpec((tm, S5_W), lambda i: (i, 6)),
            _full((SUPER, 128, 512)), _full((SUPER, 128, 512)),
            _full((SUPER, 512, 128)), _full((SUPER, 512, 128)),
            _full((1, S5_W)), _full((S5_W, S5_W)), _full((1, S5_W)),
            _full((8, 8, W)),
        ] + r_in,
        out_specs=[
            pl.BlockSpec((tm, S5_W), lambda i: (i, 0)),
            pl.BlockSpec((tm, S5_W), lambda i: (i, 0)),
            pl.BlockSpec((tm, W), lambda i: (i, 0)),
            pl.BlockSpec((tm, W), lambda i: (i, 0)),
        ] + r_out,
        out_shape=[
            jax.ShapeDtypeStruct((T, S5_W), BF16),
            jax.ShapeDtypeStruct((T, S5_W), F32),
            jax.ShapeDtypeStruct((T, W), F32),
            jax.ShapeDtypeStruct((T, W), F32),
        ] + r_shapes,
        scratch_shapes=[pltpu.VMEM((tm + 8, W), F32), pltpu.VMEM((tm + 8, W), F32)] + r_sems,
        compiler_params=_cparams(1),
    )(z, z, p["b4re"], p["b4im"], p["c4re"], p["c4im"], p["dvec"], p["wglu"], p["bglu"], p["coef_f"], *r_args)


def _s5_bwd(dy, z, yraw, sre, sim, p, rider=None):
    T = z.shape[0]
    tm = min(512, T)
    nt = T // tm
    nblk = tm // 8
    W = STATE_W
    rev = lambda i: nt - 1 - i

    def body(dya_ref, xa_ref, ga_ref, yraw_ref, sre_ref, sim_ref, hre_ref, him_ref,
             bre_t_ref, bim_t_ref, cre_t_ref, cim_t_ref, dv_ref, wg_ref, wgt_ref, bg_ref, coef_ref,
             dz_ref, dbre_ref, dbim_ref, dcre_ref, dcim_ref, dd_ref, dwg_ref, dbg_ref, da_ref,
             wre, wim, dyr_ref, xa_stage, ga_stage, xa_sem, ga_sem):
        i = pl.program_id(0)
        xa_out = _ColumnWriter(xa_stage, xa_sem, dz_ref, 0, i, nt)
        ga_out = _ColumnWriter(ga_stage, ga_sem, dz_ref, 6 * 512, i, nt)
        dxa_ref, dga_ref = xa_out.slot(), ga_out.slot()

        @pl.when(i == 0)
        def _():
            wre[tm:tm + 8, :] = jnp.zeros((8, W), F32)
            wim[tm:tm + 8, :] = jnp.zeros((8, W), F32)
            for ref in (dbre_ref, dbim_ref, dcre_ref, dcim_ref, dd_ref, dwg_ref, dbg_ref, da_ref):
                ref[...] = jnp.zeros_like(ref)

        xa = xa_ref[...].astype(F32)
        dya = dya_ref[...]
        yg, dgelu = _gelu_and_grad(yraw_ref[...])
        ygb = yg.astype(BF16)
        q = jnp.dot(ygb, wg_ref[...], preferred_element_type=F32) + bg_ref[...]
        sq = jax.nn.sigmoid(q)
        sga, dsga = _silu_and_grad(ga_ref[...].astype(F32))
        dga_ref[...] = (dya * (yg * sq) * dsga).astype(BF16)
        dya0 = dya * sga
        dq = dya0 * yg * sq * (1.0 - sq)
        dqb = dq.astype(BF16)
        dyg = dya0 * sq + jnp.dot(dqb, wgt_ref[...], preferred_element_type=F32)
        dwg_ref[...] += _dot_tn(ygb, dqb)
        dbg_ref[...] += jnp.sum(dq, axis=0, keepdims=True)
        dyraw = dyg * dgelu
        dd_ref[...] += jnp.sum(dyraw * xa, axis=0, keepdims=True)
        dyr_ref[...] = dyraw.astype(BF16)

        for j in range(SUPER):
            dj = dyr_ref[:, j * 128:(j + 1) * 128]
            wre[0:tm, j * 512:(j + 1) * 512] = jnp.dot(dj, cre_t_ref[j], preferred_element_type=F32)
            wim[0:tm, j * 512:(j + 1) * 512] = jnp.dot(dj, cim_t_ref[j], preferred_element_type=F32)

        row0 = lax.broadcasted_iota(jnp.int32, (8, SCAN_COLS), 0) == 0
        head_on = (i < nt - 1).astype(F32)

        def one_block(base, first):
            for cc in range(W // SCAN_COLS):
                cs = pl.ds(cc * SCAN_COLS, SCAN_COLS)
                r, im = _scan_block(wre[pl.ds(base, 8), cs], wim[pl.ds(base, 8), cs], coef_ref, cs, True)
                cr = wre[pl.ds(base + 8, 1), cs]
                ci = wim[pl.ds(base + 8, 1), cs]
                pr = coef_ref[6, :, cs]
                pi = coef_ref[7, :, cs]
                r, im = r + pr * cr - pi * ci, im + pr * ci + pi * cr
                wre[pl.ds(base, 8), cs] = r
                wim[pl.ds(base, 8), cs] = im
                if first:
                    pre = hre_ref[7:8, cs] * head_on
                    pim = him_ref[7:8, cs] * head_on
                else:
                    pre = sre_ref[pl.ds(base - 1, 1), cs]
                    pim = sim_ref[pl.ds(base - 1, 1), cs]
                spr = jnp.where(row0, pre, pltpu.roll(sre_ref[pl.ds(base, 8), cs], 1, 0))
                spi = jnp.where(row0, pim, pltpu.roll(sim_ref[pl.ds(base, 8), cs], 1, 0))
                da_ref[0, :, cs] += r * spr + im * spi
                da_ref[1, :, cs] += im * spr - r * spi

        def blk(b, carry):
            one_block(pl.multiple_of((nblk - 1 - b) * 8, 8), False)
            return carry

        lax.fori_loop(0, nblk - 1, blk, 0)
        one_block(0, True)
        wre[tm:tm + 8, :] = wre[0:8, :]
        wim[tm:tm + 8, :] = wim[0:8, :]

        xab = xa.astype(BF16)
        for j in range(SUPER):
            cols = slice(j * 512, (j + 1) * 512)
            gre = wre[0:tm, cols].astype(BF16)
            gim = wim[0:tm, cols].astype(BF16)
            xj = xab[:, j * 128:(j + 1) * 128]
            dj = dyr_ref[:, j * 128:(j + 1) * 128]
            dbre_ref[j] += _dot_tn(xj, gre)
            dbim_ref[j] += _dot_tn(xj, gim)
            dcre_ref[j] += _dot_tn(sre_ref[:, cols], dj)
            dcim_ref[j] += _dot_tn(sim_ref[:, cols], dj)
            dxj = jnp.dot(gre, bre_t_ref[j], preferred_element_type=F32)
            dxj += jnp.dot(gim, bim_t_ref[j], preferred_element_type=F32)
            dxj += dyraw[:, j * 128:(j + 1) * 128] * dv_ref[:, j * 128:(j + 1) * 128]
            dxa_ref[:, j * 128:(j + 1) * 128] = dxj.astype(BF16)
        xa_out.send(rev(i) * tm)
        ga_out.send(rev(i) * tm)

    acc = lambda shape: _full(shape)
    hb = tm // 8
    r_in, r_out, r_shapes, r_sems, r_args = _rider_specs(rider)
    stages, stage_sems = _stage_scratch(tm, (S5_W, S5_W))
    return pl.pallas_call(
        _ride(body, 17, 9, rider, lambda: pl.program_id(0) == 0, lambda: pl.program_id(0) == nt - 1),
        name="s5_bwd" + ("" if rider is None else "_ride"),
        grid=(nt,),
        in_specs=[
            pl.BlockSpec((tm, S5_W), lambda i: (rev(i), 0)),
            pl.BlockSpec((tm, S5_W), lambda i: (rev(i), 0)),
            pl.BlockSpec((tm, S5_W), lambda i: (rev(i), 6)),
            pl.BlockSpec((tm, S5_W), lambda i: (rev(i), 0)),
            pl.BlockSpec((tm, W), lambda i: (rev(i), 0)),
            pl.BlockSpec((tm, W), lambda i: (rev(i), 0)),
            pl.BlockSpec((8, W), lambda i: (jnp.maximum(rev(i) * hb - 1, 0), 0)),
            pl.BlockSpec((8, W), lambda i: (jnp.maximum(rev(i) * hb - 1, 0), 0)),
            _full((SUPER, 512, 128)), _full((SUPER, 512, 128)),
            _full((SUPER, 128, 512)), _full((SUPER, 128, 512)),
            _full((1, S5_W)), _full((S5_W, S5_W)), _full((S5_W, S5_W)), _full((1, S5_W)),
            _full((8, 8, W)),
        ] + r_in,
        out_specs=[
            pl.BlockSpec(memory_space=pl.ANY),
            acc((SUPER, 128, 512)), acc((SUPER, 128, 512)),
            acc((SUPER, 512, 128)), acc((SUPER, 512, 128)),
            acc((1, S5_W)), acc((S5_W, S5_W)), acc((1, S5_W)), acc((2, 8, W)),
        ] + r_out,
        out_shape=[
            jax.ShapeDtypeStruct((T, IN_COLS), BF16),
            jax.ShapeDtypeStruct((SUPER, 128, 512), F32), jax.ShapeDtypeStruct((SUPER, 128, 512), F32),
            jax.ShapeDtypeStruct((SUPER, 512, 128), F32), jax.ShapeDtypeStruct((SUPER, 512, 128), F32),
            jax.ShapeDtypeStruct((1, S5_W), F32), jax.ShapeDtypeStruct((S5_W, S5_W), F32),
            jax.ShapeDtypeStruct((1, S5_W), F32), jax.ShapeDtypeStruct((2, 8, W), F32),
        ] + r_shapes,
        scratch_shapes=[pltpu.VMEM((tm + 8, W), F32), pltpu.VMEM((tm + 8, W), F32), pltpu.VMEM((tm, S5_W), BF16)]
        + stages + stage_sems + r_sems,
        compiler_params=_cparams(1),
    )(dy, z, z, yraw, sre, sim, sre, sim,
      p["b4re_t"], p["b4im_t"], p["c4re_t"], p["c4im_t"], p["dvec"], p["wglu"], p["wglu_t"], p["bglu"], p["coef_r"],
      *r_args)


def _ln_fwd(vf, lng, lnb):
    mu = jnp.mean(vf, axis=-1, keepdims=True)
    d = vf - mu
    rstd = lax.rsqrt(jnp.mean(d * d, axis=-1, keepdims=True) + LN_EPS)
    xh = d * rstd
    return xh, rstd, xh * lng + lnb


def _col_block(tm, b):
    return pl.BlockSpec((tm, 512), lambda i: (i, b))


def _ln_halves(vf0, vf1):
    mu = (jnp.sum(vf0, axis=-1, keepdims=True) + jnp.sum(vf1, axis=-1, keepdims=True)) * (1.0 / SGU_W)
    d0, d1 = vf0 - mu, vf1 - mu
    var = (jnp.sum(d0 * d0, axis=-1, keepdims=True) + jnp.sum(d1 * d1, axis=-1, keepdims=True)) * (1.0 / SGU_W)
    rstd = lax.rsqrt(var + LN_EPS)
    return d0 * rstd, d1 * rstd, rstd


def _sgu_fwd(z, ws, bsf, lng, lnb):
    T = z.shape[0]
    tm = min(512, T)

    def body(u0, u1, v0, v1, g0, g1, ws_ref, bs_ref, lng_ref, lnb_ref, yb_ref, vn_ref):
        for c in range(tm // CHUNK):
            rows = slice(c * CHUNK, (c + 1) * CHUNK)
            xh0, xh1, _ = _ln_halves(_gelu(v0[rows, :].astype(F32)), _gelu(v1[rows, :].astype(F32)))
            vn_ref[:, 0:512] = (xh0 * lng_ref[:, 0:512] + lnb_ref[:, 0:512]).astype(BF16)
            vn_ref[:, 512:1024] = (xh1 * lng_ref[:, 512:1024] + lnb_ref[:, 512:1024]).astype(BF16)
            for half, (u_ref, g_ref) in enumerate(((u0, g0), (u1, g1))):
                sg, _ = _silu_and_grad(g_ref[rows, :].astype(F32))
                m = _gelu(u_ref[rows, :].astype(F32)) * sg
                for hh in range(SGU_HEADS // 2):
                    h = half * (SGU_HEADS // 2) + hh
                    cols = slice(h * 128, (h + 1) * 128)
                    s = jnp.dot(ws_ref[h], vn_ref[:, cols], preferred_element_type=F32) + bs_ref[:, cols]
                    yb_ref[rows, cols] = (m[:, hh * 128:(hh + 1) * 128] * s).astype(BF16)

    return pl.pallas_call(
        body,
        name="sgu_fwd",
        grid=(T // tm,),
        in_specs=[_col_block(tm, b) for b in (1, 2, 3, 4, 7, 8)] + [
            _full((SGU_HEADS, CHUNK, CHUNK)), _full((CHUNK, SGU_W)), _full((1, SGU_W)), _full((1, SGU_W)),
        ],
        out_specs=pl.BlockSpec((tm, SGU_W), lambda i: (i, 0)),
        out_shape=jax.ShapeDtypeStruct((T, SGU_W), BF16),
        scratch_shapes=[pltpu.VMEM((CHUNK, SGU_W), BF16)],
        compiler_params=_cparams(1),
    )(z, z, z, z, z, z, ws, bsf, lng, lnb)


def _sgu_bwd(dy, z, dz, ws, ws_t, bsf, lng, lnb, rider=None):
    T = z.shape[0]
    tm = min(512, T)
    HH = SGU_HEADS // 2

    def body(u0, u1, v0, v1, g0, g1, dy0, dy1, ws_ref, wst_ref, bs_ref, lng_ref, lnb_ref, dz_in,
             dz_ref, dws_ref, dbs_ref, dlng_ref, dlnb_ref, vn_ref, dvn_ref, *stage):
        step = pl.program_id(0)
        outs = [_ColumnWriter(stage[k], stage[3 + k], dz_ref, col, step, T // tm)
                for k, col in enumerate((512, 1536, 3584))]
        du_ref, dv_ref, dgb_ref = (o.slot() for o in outs)

        @pl.when(step == 0)
        def _():
            for ref in (dws_ref, dbs_ref, dlng_ref, dlnb_ref):
                ref[...] = jnp.zeros_like(ref)

        for c in range(tm // CHUNK):
            rows = slice(c * CHUNK, (c + 1) * CHUNK)
            vf0, dgv0 = _gelu_and_grad(v0[rows, :].astype(F32))
            vf1, dgv1 = _gelu_and_grad(v1[rows, :].astype(F32))
            xh0, xh1, rstd = _ln_halves(vf0, vf1)
            vn_ref[:, 0:512] = (xh0 * lng_ref[:, 0:512] + lnb_ref[:, 0:512]).astype(BF16)
            vn_ref[:, 512:1024] = (xh1 * lng_ref[:, 512:1024] + lnb_ref[:, 512:1024]).astype(BF16)
            for half, (u_ref, g_ref, dy_ref) in enumerate(((u0, g0, dy0), (u1, g1, dy1))):
                ug, dgu = _gelu_and_grad(u_ref[rows, :].astype(F32))
                sg, dsg = _silu_and_grad(g_ref[rows, :].astype(F32))
                dyb = dy_ref[rows, :]
                dyb0 = dyb * sg
                ds_half = dyb0 * ug
                du_scale = dyb0 * dgu
                dg_scale = dyb * ug * dsg
                for hh in range(HH):
                    h = half * HH + hh
                    cols = slice(h * 128, (h + 1) * 128)
                    lc = slice(hh * 128, (hh + 1) * 128)
                    s = jnp.dot(ws_ref[h], vn_ref[:, cols], preferred_element_type=F32) + bs_ref[:, cols]
                    du_ref[rows, cols] = (du_scale[:, lc] * s).astype(BF16)
                    dgb_ref[rows, cols] = (dg_scale[:, lc] * s).astype(BF16)
                    ds = ds_half[:, lc]
                    dbs_ref[:, cols] += ds
                    dsb = ds.astype(BF16)
                    dws_ref[h] += _dot_nt(dsb, vn_ref[:, cols])
                    dvn_ref[:, cols] = jnp.dot(wst_ref[h], dsb, preferred_element_type=F32)
            dvn0 = dvn_ref[:, 0:512]
            dvn1 = dvn_ref[:, 512:1024]
            dlnb_ref[:, 0:512] += jnp.sum(dvn0, axis=0, keepdims=True)
            dlnb_ref[:, 512:1024] += jnp.sum(dvn1, axis=0, keepdims=True)
            dlng_ref[:, 0:512] += jnp.sum(dvn0 * xh0, axis=0, keepdims=True)
            dlng_ref[:, 512:1024] += jnp.sum(dvn1 * xh1, axis=0, keepdims=True)
            dxh0 = dvn0 * lng_ref[:, 0:512]
            dxh1 = dvn1 * lng_ref[:, 512:1024]
            m1 = (jnp.sum(dxh0, axis=-1, keepdims=True) + jnp.sum(dxh1, axis=-1, keepdims=True)) * (1.0 / SGU_W)
            m2 = (jnp.sum(dxh0 * xh0, axis=-1, keepdims=True) + jnp.sum(dxh1 * xh1, axis=-1, keepdims=True)) * (1.0 / SGU_W)
            dv_ref[rows, 0:512] = (rstd * (dxh0 - m1 - xh0 * m2) * dgv0).astype(BF16)
            dv_ref[rows, 512:1024] = (rstd * (dxh1 - m1 - xh1 * m2) * dgv1).astype(BF16)
        for o in outs:
            o.send(step * tm)

    anyspec = pl.BlockSpec(memory_space=pl.ANY)
    r_in, r_out, r_shapes, r_sems, r_args = _rider_specs(rider)
    stages, stage_sems = _stage_scratch(tm, (SGU_W, SGU_W, SGU_W))
    return pl.pallas_call(
        _ride(body, 14, 5, rider, lambda: pl.program_id(0) == 0, lambda: pl.program_id(0) == T // tm - 1),
        name="sgu_bwd" + ("" if rider is None else "_ride"),
        grid=(T // tm,),
        in_specs=[_col_block(tm, b) for b in (1, 2, 3, 4, 7, 8)] + [_col_block(tm, 1), _col_block(tm, 2)] + [
            _full((SGU_HEADS, CHUNK, CHUNK)), _full((SGU_HEADS, CHUNK, CHUNK)),
            _full((CHUNK, SGU_W)), _full((1, SGU_W)), _full((1, SGU_W)), anyspec,
        ] + r_in,
        out_specs=[anyspec,
                   _full((SGU_HEADS, CHUNK, CHUNK)), _full((CHUNK, SGU_W)), _full((1, SGU_W)), _full((1, SGU_W))] + r_out,
        input_output_aliases={13: 0},
        out_shape=[
            jax.ShapeDtypeStruct((T, IN_COLS), BF16),
            jax.ShapeDtypeStruct((SGU_HEADS, CHUNK, CHUNK), F32), jax.ShapeDtypeStruct((CHUNK, SGU_W), F32),
            jax.ShapeDtypeStruct((1, SGU_W), F32), jax.ShapeDtypeStruct((1, SGU_W), F32),
        ] + r_shapes,
        scratch_shapes=[pltpu.VMEM((CHUNK, SGU_W), BF16), pltpu.VMEM((CHUNK, SGU_W), F32)] + stages + stage_sems + r_sems,
        compiler_params=_cparams(1),
    )(z, z, z, z, z, z, dy, dy, ws, ws_t, bsf, lng, lnb, dz, *r_args)


def _pool_den(first_row, n):
    return (lax.broadcasted_iota(jnp.int32, (n, 1), 0) + first_row + 1).astype(F32)


def _pool_p(ext, xc, pos, tm):
    w2 = ext + pltpu.roll(ext, 1, 0)
    w4 = w2 + pltpu.roll(w2, 2, 0)
    w8 = w4 + pltpu.roll(w4, 4, 0)
    w16 = w8 + pltpu.roll(w8, 8, 0)
    out = []
    for g, (w, ws) in enumerate(zip(POOL_WINDOWS, (w2, w4, w8, w16))):
        cols = slice(g * 128, (g + 1) * 128)
        mean = ws[POOL_HALO:POOL_HALO + tm, cols] / jnp.minimum(pos, float(w))
        out.append(mean - xc[:, cols])
    return out


def _pool_fwd(z, wp, scale):
    T = z.shape[0]
    tm = min(512, T)
    hb = tm // POOL_HALO

    def body(xc_ref, hx_ref, gc_ref, wp_ref, sc_ref, yc_ref):
        i = pl.program_id(0)
        xc = xc_ref[...].astype(F32)
        halo = hx_ref[...].astype(F32) * (i > 0).astype(F32)
        ext = jnp.concatenate([halo, xc], axis=0)
        ps = _pool_p(ext, xc, _pool_den(i * tm, tm), tm)
        sg, _ = _silu_and_grad(gc_ref[...].astype(F32))
        for g in range(4):
            cols = slice(g * 128, (g + 1) * 128)
            pw = _dot(ps[g], wp_ref[g])
            yc_ref[:, cols] = (pw * sc_ref[:, cols] * sg[:, cols]).astype(BF16)

    return pl.pallas_call(
        body,
        name="pool_fwd",
        grid=(T // tm,),
        in_specs=[
            _col_block(tm, 5),
            pl.BlockSpec((POOL_HALO, 512), lambda i: (jnp.maximum(i * hb - 1, 0), 5)),
            _col_block(tm, 9),
            _full((4, 128, 128)), _full((1, POOL_W)),
        ],
        out_specs=pl.BlockSpec((tm, POOL_W), lambda i: (i, 0)),
        out_shape=jax.ShapeDtypeStruct((T, POOL_W), BF16),
        compiler_params=_cparams(1),
    )(z, z, z, wp, scale)


def _pool_bwd(dy, z, dz, wp, wp_t, scale):
    T = z.shape[0]
    tm = min(512, T)
    nt = T // tm
    hb = tm // POOL_HALO
    last_hb = T // POOL_HALO - 1
    L = tm + POOL_HALO

    def body(xc_ref, hx_ref, gc_ref, gn_ref, dyc_ref, dyn_ref, wp_ref, wpt_ref, sc_ref, dz_in,
             dz_ref, dwp_ref, dsc_ref, xc_stage, gc_stage, xc_sem, gc_sem):
        i = pl.program_id(0)
        xc_out = _ColumnWriter(xc_stage, xc_sem, dz_ref, 5 * 512, i, nt)
        gc_out = _ColumnWriter(gc_stage, gc_sem, dz_ref, 9 * 512, i, nt)
        dxc_ref, dgc_ref = xc_out.slot(), gc_out.slot()

        @pl.when(i == 0)
        def _():
            dwp_ref[...] = jnp.zeros_like(dwp_ref)
            dsc_ref[...] = jnp.zeros_like(dsc_ref)

        xc = xc_ref[...].astype(F32)
        halo = hx_ref[...].astype(F32) * (i > 0).astype(F32)
        pos = _pool_den(i * tm, tm)
        ps = _pool_p(jnp.concatenate([halo, xc], axis=0), xc, pos, tm)
        sg, dsg = _silu_and_grad(gc_ref[...].astype(F32))
        dyc = dyc_ref[...]
        dyc0 = dyc * sg
        dpw = dyc0 * sc_ref[...]
        sgn, _ = _silu_and_grad(gn_ref[...].astype(F32))
        dpwn = dyn_ref[...] * sgn * sc_ref[...] * (i < nt - 1).astype(F32)
        posn = _pool_den((i + 1) * tm, POOL_HALO)
        dps, qs = [], []
        for g, w in enumerate(POOL_WINDOWS):
            cols = slice(g * 128, (g + 1) * 128)
            pw = _dot(ps[g], wp_ref[g])
            dgc_ref[:, cols] = (dyc[:, cols] * pw * sc_ref[:, cols] * dsg[:, cols]).astype(BF16)
            dsc_ref[:, cols] += jnp.sum(dyc0[:, cols] * pw, axis=0, keepdims=True)
            dwp_ref[g] += _dot_tn(ps[g], dpw[:, cols])
            dp = _dot(dpw[:, cols], wpt_ref[g])
            dpn = _dot(dpwn[:, cols], wpt_ref[g])
            dps.append(dp)
            qs.append(jnp.concatenate([dp / jnp.minimum(pos, float(w)), dpn / jnp.minimum(posn, float(w))], axis=0))
        ext = jnp.concatenate(qs, axis=1)
        f2 = ext + pltpu.roll(ext, L - 1, 0)
        f4 = f2 + pltpu.roll(f2, L - 2, 0)
        f8 = f4 + pltpu.roll(f4, L - 4, 0)
        f16 = f8 + pltpu.roll(f8, L - 8, 0)
        for g, f in enumerate((f2, f4, f8, f16)):
            cols = slice(g * 128, (g + 1) * 128)
            dxc_ref[:, cols] = (f[0:tm, cols] - dps[g]).astype(BF16)
        xc_out.send(i * tm)
        gc_out.send(i * tm)

    nxt = lambda i: jnp.minimum((i + 1) * hb, last_hb)
    anyspec = pl.BlockSpec(memory_space=pl.ANY)
    stages, stage_sems = _stage_scratch(tm, (POOL_W, POOL_W))
    return pl.pallas_call(
        body,
        name="pool_bwd",
        grid=(nt,),
        in_specs=[
            _col_block(tm, 5),
            pl.BlockSpec((POOL_HALO, 512), lambda i: (jnp.maximum(i * hb - 1, 0), 5)),
            _col_block(tm, 9),
            pl.BlockSpec((POOL_HALO, 512), lambda i: (nxt(i), 9)),
            _col_block(tm, 3),
            pl.BlockSpec((POOL_HALO, 512), lambda i: (nxt(i), 3)),
            _full((4, 128, 128)), _full((4, 128, 128)), _full((1, POOL_W)), anyspec,
        ],
        out_specs=[anyspec, _full((4, 128, 128)), _full((1, POOL_W))],
        input_output_aliases={9: 0},
        out_shape=[
            jax.ShapeDtypeStruct((T, IN_COLS), BF16),
            jax.ShapeDtypeStruct((4, 128, 128), F32), jax.ShapeDtypeStruct((1, POOL_W), F32),
        ],
        scratch_shapes=stages + stage_sems,
        compiler_params=_cparams(1),
    )(z, z, z, z, dy, dy, wp, wp_t, scale, dz)


def _row_tile(rows, cols):
    tr = 8
    while tr * 2 * cols * 4 <= 2 * 1024 * 1024 and rows % (tr * 2) == 0:
        tr *= 2
    return tr


def _add_own_half(part, recv, cidx):
    _, _, R2, C = part.shape
    tr = _row_tile(R2, C)

    def body(c_ref, a_ref, r_ref, o_ref):
        o_ref[...] = (a_ref[...] + r_ref[...]).astype(BF16)

    return pl.pallas_call(
        body,
        name="add_own_half",
        grid_spec=pltpu.PrefetchScalarGridSpec(
            num_scalar_prefetch=1,
            grid=(N_CHIPS, R2 // tr),
            in_specs=[
                pl.BlockSpec((None, None, tr, C), lambda j, i, c: (j, c[0], i, 0)),
                pl.BlockSpec((None, tr, C), lambda j, i, c: (j, i, 0)),
            ],
            out_specs=pl.BlockSpec((None, tr, C), lambda j, i, c: (j, i, 0)),
        ),
        out_shape=jax.ShapeDtypeStruct((N_CHIPS, R2, C), BF16),
        compiler_params=_cparams(2),
    )(cidx, part, recv)


def _add2(a, b):
    R, C = a.shape
    tr = _row_tile(R, C)

    def body(a_ref, b_ref, o_ref):
        o_ref[...] = a_ref[...] + b_ref[...]

    spec = pl.BlockSpec((tr, C), lambda i: (i, 0))
    return pl.pallas_call(
        body, name="add2", grid=(R // tr,), in_specs=[spec, spec], out_specs=spec,
        out_shape=jax.ShapeDtypeStruct((R, C), F32), compiler_params=_cparams(1),
    )(a, b)


def _sum_chips(parts):
    _, R, C = parts.shape
    tr = _row_tile(R, N_CHIPS * C)

    def body(p_ref, o_ref):
        p = [p_ref[j].astype(F32) for j in range(N_CHIPS)]
        o_ref[...] = ((p[0] + p[1]) + p[2]) + p[3]

    return pl.pallas_call(
        body, name="sum_chips", grid=(R // tr,),
        in_specs=[pl.BlockSpec((N_CHIPS, tr, C), lambda i: (0, i, 0))],
        out_specs=pl.BlockSpec((tr, C), lambda i: (i, 0)),
        out_shape=jax.ShapeDtypeStruct((R, C), F32), compiler_params=_cparams(1),
    )(parts)


def _adamw_math(w, g, m, v):
    m = ADAM_B1 * m + (1.0 - ADAM_B1) * g
    v = ADAM_B2 * v + (1.0 - ADAM_B2) * (g * g)
    m_hat = m / (1.0 - ADAM_B1 ** ADAM_STEP)
    v_hat = v / (1.0 - ADAM_B2 ** ADAM_STEP)
    delta = -ADAM_LR * (m_hat / (jnp.sqrt(v_hat) + ADAM_EPS) + ADAM_WD * w)
    return delta, m, v


def _adamw(w, g, m, v):
    R, C = w.shape
    tr = _row_tile(R, C)

    def body(w_ref, g_ref, m_ref, v_ref, d_ref, mo_ref, vo_ref):
        d_ref[...], mo_ref[...], vo_ref[...] = _adamw_math(w_ref[...], g_ref[...], m_ref[...], v_ref[...])

    spec = pl.BlockSpec((tr, C), lambda i: (i, 0))
    shp = jax.ShapeDtypeStruct((R, C), F32)
    return pl.pallas_call(
        body, name="adamw", grid=(R // tr,), in_specs=[spec] * 4, out_specs=[spec] * 3,
        out_shape=[shp] * 3, compiler_params=_cparams(1),
    )(w, g, m, v)


def _adamw_halves(w, mine, theirs, m, v, cidx, rider=None):
    _, _, R2, C = w.shape
    tr = _row_tile(R2, C)
    nr = R2 // tr

    def body(c_ref, w_ref, a0_ref, b0_ref, a1_ref, b1_ref, m_ref, v_ref, g_ref, d_ref, mo_ref, vo_ref):
        own = pl.program_id(1) == c_ref[0]
        g0 = jnp.where(own, a0_ref[...], b0_ref[...])
        g1 = jnp.where(own, a1_ref[...], b1_ref[...])
        g = jnp.where(pl.program_id(0) == 0, g0, g1)
        g_ref[...] = g
        d_ref[...], mo_ref[...], vo_ref[...] = _adamw_math(w_ref[...], g, m_ref[...], v_ref[...])

    full = pl.BlockSpec((None, None, tr, C), lambda l, h, i, c: (l, h, i, 0))

    def pick(layer, mine_side):
        def index(l, h, i, c):
            used = (l == layer) & ((h == c[0]) == mine_side)
            return (jnp.where(used, i, 0), 0)
        return pl.BlockSpec((tr, C), index)

    shp = jax.ShapeDtypeStruct(w.shape, F32)
    r_in, r_out, r_shapes, r_sems, r_args = _rider_specs(rider)
    last = lambda: (pl.program_id(0) == 1) & (pl.program_id(1) == 1) & (pl.program_id(2) == nr - 1)
    first = lambda: (pl.program_id(0) == 0) & (pl.program_id(1) == 0) & (pl.program_id(2) == 0)
    return pl.pallas_call(
        _ride(body, 8, 4, rider, first, last),
        name="adamw_halves" + ("" if rider is None else "_ride"),
        grid_spec=pltpu.PrefetchScalarGridSpec(
            num_scalar_prefetch=1, grid=(2, 2, nr),
            in_specs=[full, pick(0, True), pick(0, False), pick(1, True), pick(1, False), full, full] + r_in,
            out_specs=[full] * 4 + r_out,
            scratch_shapes=r_sems,
        ),
        out_shape=[shp] * 4 + r_shapes,
        compiler_params=_cparams(3),
    )(cidx, w, mine[0], theirs[0], mine[1], theirs[1], m, v, *r_args)


_ANY = pl.BlockSpec(memory_space=pl.ANY)


def _mesh_pos():
    return lax.axis_index("x"), lax.axis_index("y"), lax.axis_index("c")


def _other_chips(x, y):
    return [(2 * x + (1 - y), x, 1 - y), (2 * (1 - x) + y, 1 - x, y), (2 * (1 - x) + (1 - y), 1 - x, 1 - y)]


def _gathered_shapes(shards):
    return [jax.ShapeDtypeStruct((2, N_CHIPS) + s.shape[1:], s.dtype) for s in shards]


def _gather_sems(n):
    return [pltpu.SemaphoreType.DMA((2 * n,)), pltpu.SemaphoreType.DMA((6 * n,)), pltpu.SemaphoreType.DMA((6 * n,))]


def _gather_steps(ins, outs, lsem, ssem, rsem):
    n = len(ins)
    x, y, c = _mesh_pos()
    me = 2 * x + y
    sib = (x, y, 1 - c)
    chips = _other_chips(x, y)

    def ici(k, d):
        return pltpu.make_async_remote_copy(
            ins[k].at[c], outs[k].at[c, me], ssem.at[6 * k + d], rsem.at[6 * k + d],
            device_id=(chips[d][1], chips[d][2], c), device_id_type=MESH_ID)

    def landed(k, d):
        return pltpu.make_async_remote_copy(
            ins[k].at[c], outs[k].at[c, chips[d][0]], ssem.at[6 * k + d], rsem.at[6 * k + d],
            device_id=sib, device_id_type=MESH_ID)

    def fwd(k, d, half):
        return pltpu.make_async_remote_copy(
            outs[k].at[half, chips[d][0]], outs[k].at[half, chips[d][0]], ssem.at[6 * k + 3 + d],
            rsem.at[6 * k + 3 + d], device_id=sib, device_id_type=MESH_ID)

    def local(k, h):
        return pltpu.make_async_copy(ins[k].at[h], outs[k].at[h, me], lsem.at[2 * k + h])

    def start():
        for k in range(n):
            for h in range(2):
                local(k, h).start()
            for d in range(3):
                ici(k, d).start()

    def mid():
        for d in range(3):
            for k in range(n):
                landed(k, d).wait_recv()
                fwd(k, d, c).start()

    def end():
        for d in range(3):
            for k in range(n):
                fwd(k, d, 1 - c).wait_recv()
        for k in range(n):
            for d in range(3):
                ici(k, d).wait_send()
                fwd(k, d, c).wait_send()
            for h in range(2):
                local(k, h).wait()

    return start, mid, end


def _gather_rider(shards):
    return _Rider(shards, _gathered_shapes(shards), _gather_sems(len(shards)), _gather_steps)


def _pair_rider(arrs, other_half):
    n = len(arrs)

    def steps(ins, outs, ssem, rsem):
        x, y, c = _mesh_pos()

        def copy(k):
            return pltpu.make_async_remote_copy(ins[k].at[:, 1 - c] if other_half else ins[k], outs[k], ssem.at[k],
                                                rsem.at[k], device_id=(x, y, 1 - c), device_id_type=MESH_ID)

        def start():
            for k in range(n):
                copy(k).start()

        def end():
            for k in range(n):
                copy(k).wait()

        return start, end

    shapes = [jax.ShapeDtypeStruct(a.shape[:1] + a.shape[2:] if other_half else a.shape, a.dtype) for a in arrs]
    return _Rider(arrs, shapes, [pltpu.SemaphoreType.DMA((n,)), pltpu.SemaphoreType.DMA((n,))], steps)


def _chip_rider(arrs, broadcast):
    n = len(arrs)

    def steps(ins, outs, lsem, ssem, rsem):
        x, y, c = _mesh_pos()
        me = 2 * x + y

        def copies():
            cps = [pltpu.make_async_copy(ins[k] if broadcast else ins[k].at[me], outs[k].at[me], lsem.at[k])
                   for k in range(n)]
            for k in range(n):
                for d, (j, tx, ty) in enumerate(_other_chips(x, y)):
                    cps.append(pltpu.make_async_remote_copy(
                        ins[k] if broadcast else ins[k].at[j], outs[k].at[me], ssem.at[3 * k + d], rsem.at[3 * k + d],
                        device_id=(tx, ty, c), device_id_type=MESH_ID))
            return cps

        def start():
            for cp in copies():
                cp.start()

        def end():
            for cp in copies():
                cp.wait()

        return start, end

    shapes = [jax.ShapeDtypeStruct(((N_CHIPS,) + a.shape) if broadcast else a.shape, a.dtype) for a in arrs]
    sems = [pltpu.SemaphoreType.DMA((n,)), pltpu.SemaphoreType.DMA((3 * n,)), pltpu.SemaphoreType.DMA((3 * n,))]
    return _Rider(arrs, shapes, sems, steps)


def _run_rider(name, rider):
    n, m = len(rider.arrs), len(rider.out_shapes)

    def body(*refs):
        for step in rider.steps(refs[:n], refs[n:n + m], *refs[n + m:]):
            step()

    return pl.pallas_call(
        body, name=name, in_specs=[_ANY] * n, out_specs=[_ANY] * m, out_shape=rider.out_shapes,
        scratch_shapes=rider.sems,
    )(*rider.arrs)


SMALL = ("norm_g", "lam_re", "lam_im", "b_re", "b_im", "c_re", "c_im", "d_skip", "log_dt", "b_glu", "ln_g", "ln_b",
         "w_s", "b_s", "w_pool", "pool_scale", "final_g")
BIG = ("w_in", "w_glu", "w_out")
WEIGHTS = ("norm_g", "w_in", "lam_re", "lam_im", "b_re", "b_im", "c_re", "c_im", "d_skip", "log_dt", "w_glu", "b_glu",
           "ln_g", "ln_b", "w_s", "b_s", "w_pool", "pool_scale", "w_out", "final_g")
PACK_UNIT = 8 * 128
PACK_ROWS = 1024


def _pack(arrs):
    parts, total = [], 0
    for a in arrs:
        f = a.reshape(-1).astype(F32)
        pad = (-f.shape[0]) % PACK_UNIT
        parts.append(jnp.pad(f, (0, pad)) if pad else f)
        total += f.shape[0] + pad
    tail = (-total) % (PACK_ROWS * 128)
    if tail:
        parts.append(jnp.zeros((tail,), F32))
    return jnp.concatenate(parts).reshape(-1, 128)


def _unpack(buf, like):
    flat = buf.reshape(-1)
    out, off = [], 0
    for a in like:
        n = math.prod(a.shape)
        out.append(flat[off:off + n].reshape(a.shape))
        off += n + ((-n) % PACK_UNIT)
    return out


def _layer_params(l, wt, g_glu):
    a_re, a_im, bb_re, bb_im = _s5_prep(wt["lam_re"][l], wt["lam_im"][l], wt["b_re"][l], wt["b_im"][l], wt["log_dt"][l])
    b4re, b4im = _block_diag_in(bb_re), _block_diag_in(bb_im)
    c4re, c4im = _block_diag_out(wt["c_re"][l]), _block_diag_out(-wt["c_im"][l])
    tr = lambda m: jnp.swapaxes(m, 1, 2).astype(BF16)
    causal = jnp.tril(jnp.ones((CHUNK, CHUNK), dtype=bool))
    ws = jnp.where(causal[None], wt["w_s"][l], 0.0)
    wglu = g_glu[l].reshape(S5_W, S5_W)
    return dict(
        b4re=b4re.astype(BF16), b4im=b4im.astype(BF16), c4re=c4re.astype(BF16), c4im=c4im.astype(BF16),
        b4re_t=tr(b4re), b4im_t=tr(b4im), c4re_t=tr(c4re), c4im_t=tr(c4im),
        dvec=wt["d_skip"][l].reshape(1, S5_W), wglu=wglu, wglu_t=wglu.T, bglu=wt["b_glu"][l].reshape(1, S5_W),
        coef_f=_scan_coefs(a_re, a_im, False), coef_r=_scan_coefs(a_re, a_im, True),
        ws=ws.astype(BF16), ws_t=tr(ws),
        bsf=jnp.broadcast_to(wt["b_s"][l][:, None, :], (SGU_HEADS, CHUNK, CHUNK)).transpose(2, 0, 1).reshape(CHUNK, SGU_W),
        lng=wt["ln_g"][l].reshape(1, SGU_W), lnb=wt["ln_b"][l].reshape(1, SGU_W),
        wp=wt["w_pool"][l].astype(BF16), wp_t=tr(wt["w_pool"][l]), scale=wt["pool_scale"][l].reshape(1, POOL_W),
        norm_g=wt["norm_g"][l].reshape(1, D_MODEL),
    )


def _local_step(x0, tgt, wt, g_in0, rest, rest_gathered, cidx=None, order=None):
    dist = cidx is not None
    xs, saved, params = [x0], [], []
    for l in range(DEPTH):
        norm_g = wt["norm_g"][l].reshape(1, D_MODEL)
        out_rider = None
        if l == 0 and not rest_gathered:
            w_in1, w_glu_b, w_out_b = rest
            h = _rms_h(xs[-1], norm_g)
            z, g_in0, g_glu, g_out = _inproj_first(h, g_in0, order, [w_glu_b, w_out_b])
            out_rider = _gather_rider([w_in1])
        elif l == 0:
            g_in1, g_glu, g_out = rest
            z, h = _inproj(xs[-1], norm_g, g_in0)
        else:
            z, h = _inproj(xs[-1], norm_g, g_in1)
        p = _layer_params(l, wt, g_glu)
        params.append(p)
        ya, yraw, sre, sim, *gathered = _s5_fwd(z, p, out_rider)
        if gathered:
            (g_in1,) = gathered
        yb = _sgu_fwd(z, p["ws"], p["bsf"], p["lng"], p["lnb"])
        yc = _pool_fwd(z, p["wp"], p["scale"])
        w_out = g_out[l].reshape(D_MODEL, D_MODEL)
        if l < DEPTH - 1:
            xn, y = _outproj(ya, yb, yc, w_out, xs[-1])
            xs.append(xn)
        else:
            dx, loss, dfg, y = _outproj_loss(ya, yb, yc, w_out, xs[-1], wt["final_g"].reshape(1, D_MODEL), tgt)
        saved.append((z, h, yraw, sre, sim, y))
    g_in = (g_in0, g_in1)

    gr = {k: [None] * DEPTH for k in WEIGHTS if k != "final_g"}
    mine, theirs, chip_sum = [None] * DEPTH, [None] * DEPTH, None
    halves = lambda a, rows: a.reshape(N_CHIPS, 2, rows // 2, a.shape[-1])
    for l in reversed(range(DEPTH)):
        p = params[l]
        z, h, yraw, sre, sim, y = saved[l]
        w_out = g_out[l].reshape(D_MODEL, D_MODEL)
        dy = _outproj_bwd_dy(dx, w_out)
        gr["w_out"][l] = _outproj_bwd_dw(y, dx)
        ride_c = _chip_rider(chip_sum, False) if dist and l == 0 else None
        dz, dbre, dbim, dcre, dcim, dd, dwg, dbg, da, *landed = _s5_bwd(dy, z, yraw, sre, sim, p, ride_c)
        if ride_c:
            mine[1] = [_sum_chips(r) for r in landed]
        ride_e = _pair_rider(mine[1], False) if dist and l == 0 else None
        dz, dws, dbsf, dlng, dlnb, *got = _sgu_bwd(dy, z, dz, p["ws"], p["ws_t"], p["bsf"], p["lng"], p["lnb"], ride_e)
        if ride_e:
            theirs[1] = got
        dz, dwp, dsc = _pool_bwd(dy, z, dz, p["wp"], p["wp_t"], p["scale"])
        gr["w_in"][l] = _inproj_bwd_dw(h, dz)
        if not dist:
            dx, dng = _inproj_bwd_dx(dz, g_in[l], xs[l], p["norm_g"], dx)
        else:
            part = [halves(gr["w_in"][l], D_MODEL), halves(dwg, S5_W // N_CHIPS),
                    halves(gr["w_out"][l], D_MODEL // N_CHIPS)]
            ride_a = _pair_rider(part, True)
            if l == 1:
                dx, dng, *from_sib = _inproj_bwd_dx(dz, g_in[l], xs[l], p["norm_g"], dx, ride_a)
                chip_sum = [_add_own_half(a, r, cidx) for a, r in zip(part, from_sib)]
            else:
                nt = x0.shape[0] // _dx_tile(x0.shape[0])
                n_top = max(nt // 4, 1)
                dx_top, dng_top, *from_sib = _inproj_bwd_dx(dz, g_in[l], xs[l], p["norm_g"], dx, ride_a,
                                                            tiles=(0, n_top))
                chip_sum0 = [_add_own_half(a, r, cidx) for a, r in zip(part, from_sib)]
                dx, dng_rest, *landed = _inproj_bwd_dx(dz, g_in[l], xs[l], p["norm_g"], dx,
                                                       _chip_rider(chip_sum0, False), tiles=(n_top, nt - n_top),
                                                       prev=dx_top)
                dng = dng_top + dng_rest
                mine[0] = [_sum_chips(r) for r in landed]

        raw = (wt["lam_re"][l], wt["lam_im"][l], wt["b_re"][l], wt["b_im"][l], wt["log_dt"][l])
        _, vjp = jax.vjp(_s5_prep, *raw)
        da = jnp.sum(da, axis=1)
        cot = (da[0].reshape(S5_GROUPS, S5_STATE), da[1].reshape(S5_GROUPS, S5_STATE),
               _block_diag_in_grad(dbre), _block_diag_in_grad(dbim))
        gr["lam_re"][l], gr["lam_im"][l], gr["b_re"][l], gr["b_im"][l], gr["log_dt"][l] = vjp(cot)
        gr["c_re"][l] = _block_diag_out_grad(dcre)
        gr["c_im"][l] = -_block_diag_out_grad(dcim)
        gr["d_skip"][l] = dd.reshape(S5_GROUPS, S5_CH)
        gr["w_glu"][l] = dwg
        gr["b_glu"][l] = dbg.reshape(S5_W)
        causal = jnp.tril(jnp.ones((CHUNK, CHUNK), dtype=bool))
        gr["w_s"][l] = jnp.where(causal[None], dws, 0.0)
        gr["b_s"][l] = dbsf.reshape(CHUNK, SGU_HEADS, CHUNK).sum(-1).T
        gr["ln_g"][l] = dlng.reshape(SGU_W)
        gr["ln_b"][l] = dlnb.reshape(SGU_W)
        gr["w_pool"][l] = dwp
        gr["pool_scale"][l] = dsc.reshape(POOL_W)
        gr["norm_g"][l] = dng.reshape(D_MODEL)

    grads = {k: (v if k in BIG else jnp.stack(v)) for k, v in gr.items()}
    grads["final_g"] = dfg.reshape(D_MODEL)
    if dist:
        for i, k in enumerate(BIG):
            grads[k] = ([mine[l][i] for l in range(DEPTH)], [None, theirs[1][i]])
    return loss, dx, grads


def kernel(x, norm_g, w_in, lam_re, lam_im, b_re, b_im, c_re, c_im, d_skip, log_dt, w_glu, b_glu, ln_g, ln_b, w_s, b_s, w_pool, pool_scale, w_out, final_g, loss_target, m_norm_g, m_w_in, m_lam_re, m_lam_im, m_b_re, m_b_im, m_c_re, m_c_im, m_d_skip, m_log_dt, m_w_glu, m_b_glu, m_ln_g, m_ln_b, m_w_s, m_b_s, m_w_pool, m_pool_scale, m_w_out, m_final_g, v_norm_g, v_w_in, v_lam_re, v_lam_im, v_b_re, v_b_im, v_c_re, v_c_im, v_d_skip, v_log_dt, v_w_glu, v_b_glu, v_ln_g, v_ln_b, v_w_s, v_b_s, v_w_pool, v_pool_scale, v_w_out, v_final_g):
    wt = dict(norm_g=norm_g, w_in=w_in, lam_re=lam_re, lam_im=lam_im, b_re=b_re, b_im=b_im, c_re=c_re, c_im=c_im,
              d_skip=d_skip, log_dt=log_dt, w_glu=w_glu, b_glu=b_glu, ln_g=ln_g, ln_b=ln_b, w_s=w_s, b_s=b_s,
              w_pool=w_pool, pool_scale=pool_scale, w_out=w_out, final_g=final_g)
    mom = dict(norm_g=m_norm_g, w_in=m_w_in, lam_re=m_lam_re, lam_im=m_lam_im, b_re=m_b_re, b_im=m_b_im, c_re=m_c_re,
               c_im=m_c_im, d_skip=m_d_skip, log_dt=m_log_dt, w_glu=m_w_glu, b_glu=m_b_glu, ln_g=m_ln_g, ln_b=m_ln_b,
               w_s=m_w_s, b_s=m_b_s, w_pool=m_w_pool, pool_scale=m_pool_scale, w_out=m_w_out, final_g=m_final_g)
    vel = dict(norm_g=v_norm_g, w_in=v_w_in, lam_re=v_lam_re, lam_im=v_lam_im, b_re=v_b_re, b_im=v_b_im, c_re=v_c_re,
               c_im=v_c_im, d_skip=v_d_skip, log_dt=v_log_dt, w_glu=v_w_glu, b_glu=v_b_glu, ln_g=v_ln_g, ln_b=v_ln_b,
               w_s=v_w_s, b_s=v_b_s, w_pool=v_w_pool, pool_scale=v_pool_scale, w_out=v_w_out, final_g=v_final_g)
    T = x.shape[1]
    cidx = lax.axis_index("c").astype(jnp.int32).reshape(1)

    w_in_b = w_in.astype(BF16)
    w0 = w_in_b[0].reshape(2, HALF_D, SHARD_COLS)
    rest = (w_in_b[1].reshape(2, HALF_D, SHARD_COLS), w_glu.astype(BF16), w_out.astype(BF16))
    mx, my = lax.axis_index("x"), lax.axis_index("y")
    order = jnp.stack([2 * mx + my] + [j for j, _, _ in _other_chips(mx, my)]).astype(jnp.int32)
    loss, grad_x, grads = _local_step(x.reshape(T, D_MODEL), loss_target.reshape(T, D_MODEL), wt, w0, rest, False,
                                      cidx, order)

    packed = _pack([grads[k] for k in SMALL] + [loss[0, 0:1]])
    sib_packed, *theirs0 = _run_rider("pair_exchange", _pair_rider([packed] + [grads[k][0][0] for k in BIG], False))
    for k, t in zip(BIG, theirs0):
        grads[k][1][0] = t
    chip_packed = _add2(packed, sib_packed)
    half_rows = chip_packed.shape[0] // 2
    my_half = lax.dynamic_index_in_dim(chip_packed.reshape(2, half_rows, 128), cidx[0], 0, keepdims=False)
    small_ride = _chip_rider([my_half], True)

    out_g, out_d, out_m, out_v = {}, {}, {}, {}
    all_half = None
    for k in BIG:
        shape = wt[k].shape
        quad = lambda t: t.reshape(2, 2, shape[1] // 2, shape[2])
        g, d, m, v, *landed = _adamw_halves(quad(wt[k]), grads[k][0], grads[k][1], quad(mom[k]), quad(vel[k]), cidx,
                                            small_ride if k == BIG[0] else None)
        if landed:
            (all_half,) = landed
        out_g[k], out_d[k], out_m[k], out_v[k] = (t.reshape(shape) for t in (g, d, m, v))

    mine_half = _sum_chips(all_half)
    (their_half,) = _run_rider("small_result_exchange", _pair_rider([mine_half], False))
    total = jnp.where(cidx[0] == 0, jnp.concatenate([mine_half, their_half]), jnp.concatenate([their_half, mine_half]))
    like = [wt[k] for k in SMALL]
    small_g = _unpack(total, like + [loss[0, 0:1]])
    loss_out = small_g[-1].reshape(())
    w_p, m_p, v_p = _pack(like), _pack([mom[k] for k in SMALL]), _pack([vel[k] for k in SMALL])
    d_p, mo_p, vo_p = _adamw(w_p, total, m_p, v_p)
    for k, g, d, m, v in zip(SMALL, small_g[:-1], _unpack(d_p, like), _unpack(mo_p, like), _unpack(vo_p, like)):
        out_g[k], out_d[k], out_m[k], out_v[k] = g, d, m, v

    return (loss_out, grad_x.reshape(x.shape), *[out_g[k] for k in WEIGHTS], *[out_d[k] for k in WEIGHTS],
            *[out_m[k] for k in WEIGHTS], *[out_v[k] for k in WEIGHTS])
```

```python
import functools
import math

import jax
import jax.numpy as jnp
from jax import lax
from jax.experimental import pallas as pl
from jax.experimental.pallas import tpu as pltpu

F32 = jnp.float32
BF16 = jnp.bfloat16

D_MODEL = 2048
DEPTH = 2
S5_W = 512
SGU_W = 1024
POOL_W = 512
IN_COLS = 5120
N_CHIPS = 4
SHARD_COLS = IN_COLS // N_CHIPS
S5_GROUPS = 32
S5_STATE = 64
S5_CH = 16
STATE_W = S5_GROUPS * S5_STATE
SUPER = 4
CHUNK = 128
SGU_HEADS = 8
POOL_WINDOWS = (2, 4, 8, 16)
POOL_HALO = 16
RMS_EPS = 1e-6
LN_EPS = 1e-5
SCAN_COLS = 512

ADAM_LR = 0.001
ADAM_B1 = 0.9
ADAM_B2 = 0.999
ADAM_EPS = 1e-08
ADAM_WD = 0.01
ADAM_STEP = 10

VMEM_LIMIT = 56 * 1024 * 1024
MESH_ID = pl.DeviceIdType.MESH

_GELU_K0 = math.sqrt(2.0 / math.pi)
_GELU_K1 = 0.044715


def _cparams(n_axes):
    return pltpu.CompilerParams(dimension_semantics=("arbitrary",) * n_axes, vmem_limit_bytes=VMEM_LIMIT)


def _gelu(x):
    t = jnp.tanh(_GELU_K0 * (x + _GELU_K1 * (x * x * x)))
    return 0.5 * x * (1.0 + t)


def _gelu_and_grad(x):
    x2 = x * x
    t = jnp.tanh(_GELU_K0 * (x + _GELU_K1 * (x * x2)))
    g = 0.5 * x * (1.0 + t)
    dg = 0.5 * (1.0 + t) + 0.5 * x * (1.0 - t * t) * (_GELU_K0 * (1.0 + 3.0 * _GELU_K1 * x2))
    return g, dg


def _silu_and_grad(x):
    s = jax.nn.sigmoid(x)
    return x * s, s * (1.0 + x * (1.0 - s))


def _dot(a, b):
    return jnp.dot(a.astype(BF16), b.astype(BF16), preferred_element_type=F32)


def _dot_nt(a, b):
    return lax.dot_general(a.astype(BF16), b.astype(BF16), (((1,), (1,)), ((), ())), preferred_element_type=F32)


def _dot_tn(a, b):
    return lax.dot_general(a.astype(BF16), b.astype(BF16), (((0,), (0,)), ((), ())), preferred_element_type=F32)


def _full(shape):
    nd = len(shape)
    return pl.BlockSpec(shape, lambda *_: (0,) * nd)


class _Rider:
    def __init__(self, arrs, out_shapes, sems, steps):
        self.arrs, self.out_shapes, self.sems, self.steps = list(arrs), list(out_shapes), list(sems), steps


def _ride(body, n_in, n_out, rider, first, last, middle=None):
    if rider is None:
        return body
    ri, ro, ns = len(rider.arrs), len(rider.out_shapes), len(rider.sems)

    def wrapped(*refs):
        o0 = n_in + ri
        steps = rider.steps(refs[n_in:o0], refs[o0 + n_out:o0 + n_out + ro], *refs[len(refs) - ns:])
        pl.when(first())(steps[0])
        if len(steps) == 3:
            pl.when(middle())(steps[1])
        body(*refs[:n_in], *refs[o0:o0 + n_out], *refs[o0 + n_out + ro:len(refs) - ns])
        pl.when(last())(steps[-1])

    return wrapped


class _ColumnWriter:
    def __init__(self, stage_ref, sem_ref, dst_ref, col0, step, n_steps):
        self.stage, self.sem, self.dst, self.col0, self.step, self.n = stage_ref, sem_ref, dst_ref, col0, step, n_steps
        self.tm, self.w = stage_ref.shape[1], stage_ref.shape[2]

    def _copy(self, slot, row0):
        return pltpu.make_async_copy(self.stage.at[slot],
                                     self.dst.at[pl.ds(row0, self.tm), pl.ds(self.col0, self.w)], self.sem.at[slot])

    def slot(self):
        s = self.step % 2

        @pl.when(self.step >= 2)
        def _():
            self._copy(s, 0).wait()

        return self.stage.at[s]

    def send(self, row0):
        s = self.step % 2
        self._copy(s, row0).start()

        @pl.when(self.step == self.n - 1)
        def _():
            self._copy(s, 0).wait()
            if self.n >= 2:
                self._copy(1 - s, 0).wait()


def _stage_scratch(tm, widths):
    return ([pltpu.VMEM((2, tm, w), BF16) for w in widths], [pltpu.SemaphoreType.DMA((2,)) for _ in widths])


def _rider_specs(rider):
    if rider is None:
        return [], [], [], [], []
    anyspec = pl.BlockSpec(memory_space=pl.ANY)
    return ([anyspec] * len(rider.arrs), [anyspec] * len(rider.out_shapes), rider.out_shapes, rider.sems, rider.arrs)


HALF_D = D_MODEL // 2


def _inproj(x, g, w):
    T = x.shape[0]
    tm = min(512, T)

    def body(x_ref, g_ref, w_ref, z_ref, h_ref, hs_ref):
        @pl.when(pl.program_id(1) == 0)
        def _():
            xv = x_ref[...]
            r = lax.rsqrt(jnp.mean(xv * xv, axis=-1, keepdims=True) + RMS_EPS)
            hv = (xv * r * g_ref[...]).astype(BF16)
            hs_ref[...] = hv
            h_ref[...] = hv

        z_ref[...] = (jnp.dot(hs_ref[:, 0:HALF_D], w_ref[0], preferred_element_type=F32)
                      + jnp.dot(hs_ref[:, HALF_D:D_MODEL], w_ref[1], preferred_element_type=F32)).astype(BF16)

    return pl.pallas_call(
        body,
        name="inproj",
        grid=(T // tm, N_CHIPS),
        in_specs=[
            pl.BlockSpec((tm, D_MODEL), lambda i, j: (i, 0)),
            pl.BlockSpec((1, D_MODEL), lambda i, j: (0, 0)),
            pl.BlockSpec((2, None, HALF_D, SHARD_COLS), lambda i, j: (0, j, 0, 0)),
        ],
        out_specs=[
            pl.BlockSpec((tm, SHARD_COLS), lambda i, j: (i, j)),
            pl.BlockSpec((tm, D_MODEL), lambda i, j: (i, 0)),
        ],
        out_shape=[jax.ShapeDtypeStruct((T, IN_COLS), BF16), jax.ShapeDtypeStruct((T, D_MODEL), BF16)],
        scratch_shapes=[pltpu.VMEM((tm, D_MODEL), BF16)],
        compiler_params=_cparams(2),
    )(x, g, w)


def _rms_h(x, g):
    T = x.shape[0]
    tm = min(512, T)

    def body(x_ref, g_ref, h_ref):
        xv = x_ref[...]
        r = lax.rsqrt(jnp.mean(xv * xv, axis=-1, keepdims=True) + RMS_EPS)
        h_ref[...] = (xv * r * g_ref[...]).astype(BF16)

    return pl.pallas_call(
        body, name="rms_h", grid=(T // tm,),
        in_specs=[pl.BlockSpec((tm, D_MODEL), lambda i: (i, 0)), pl.BlockSpec((1, D_MODEL), lambda i: (0, 0))],
        out_specs=pl.BlockSpec((tm, D_MODEL), lambda i: (i, 0)),
        out_shape=jax.ShapeDtypeStruct((T, D_MODEL), BF16), compiler_params=_cparams(1),
    )(x, g)


def _inproj_first(h, w0, order, riders):
    T = h.shape[0]
    tm = min(512, T)
    ni = T // tm
    n = len(riders)

    def body(order_ref, h_ref, w0_ref, *refs):
        rin = refs[:n]
        z_ref, gin_ref = refs[n:n + 2]
        rout = refs[n + 2:2 * n + 2]
        wbuf, csem, lsem, ssem, rsem = refs[2 * n + 2:2 * n + 7]
        s, i = pl.program_id(0), pl.program_id(1)
        x, y, c = _mesh_pos()
        me = 2 * x + y
        sib = (x, y, 1 - c)
        chips = _other_chips(x, y)
        if n:
            r_start, r_mid, r_end = _gather_steps(rin, rout, *refs[2 * n + 7:])

        def ici(d):
            return pltpu.make_async_remote_copy(w0_ref.at[c], gin_ref.at[c, me], ssem.at[d], rsem.at[d],
                                                device_id=(chips[d][1], chips[d][2], c), device_id_type=MESH_ID)

        def landed(d):
            return pltpu.make_async_remote_copy(w0_ref.at[c], gin_ref.at[c, chips[d][0]], ssem.at[d], rsem.at[d],
                                                device_id=sib, device_id_type=MESH_ID)

        def fwd(d, half):
            blk = gin_ref.at[half, chips[d][0]]
            return pltpu.make_async_remote_copy(blk, blk, ssem.at[3 + d], rsem.at[3 + d], device_id=sib,
                                                device_id_type=MESH_ID)

        def local(hf):
            return pltpu.make_async_copy(w0_ref.at[hf], gin_ref.at[hf, me], lsem.at[hf])

        def load(src):
            cp = pltpu.make_async_copy(src, wbuf, csem.at[0])
            cp.start()
            cp.wait()

        @pl.when((s == 0) & (i == 0))
        def _():
            for d in range(3):
                ici(d).start()
            local(0).start()
            local(1).start()
            load(w0_ref)

        for d in range(3):
            @pl.when((s == d + 1) & (i == 0))
            def _(d=d):
                landed(d).wait_recv()
                fwd(d, c).start()
                fwd(d, 1 - c).wait_recv()
                load(gin_ref.at[:, chips[d][0]])
                if d == 1 and n:
                    r_start()

        z_ref[...] = (jnp.dot(h_ref[:, 0:HALF_D], wbuf[0], preferred_element_type=F32)
                      + jnp.dot(h_ref[:, HALF_D:D_MODEL], wbuf[1], preferred_element_type=F32)).astype(BF16)

        @pl.when((s == N_CHIPS - 1) & (i == ni - 1))
        def _():
            for d in range(3):
                ici(d).wait_send()
                fwd(d, c).wait_send()
            local(0).wait()
            local(1).wait()
            if n:
                r_mid()
                r_end()

    anyspec = pl.BlockSpec(memory_space=pl.ANY)
    return pl.pallas_call(
        body,
        name="inproj_first",
        grid_spec=pltpu.PrefetchScalarGridSpec(
            num_scalar_prefetch=1,
            grid=(N_CHIPS, ni),
            in_specs=[pl.BlockSpec((tm, D_MODEL), lambda s, i, o: (i, 0)), anyspec] + [anyspec] * n,
            out_specs=[pl.BlockSpec((tm, SHARD_COLS), lambda s, i, o: (i, o[s])), anyspec] + [anyspec] * n,
            scratch_shapes=[pltpu.VMEM((2, HALF_D, SHARD_COLS), BF16), pltpu.SemaphoreType.DMA((1,)),
                            pltpu.SemaphoreType.DMA((2,)), pltpu.SemaphoreType.DMA((6,)),
                            pltpu.SemaphoreType.DMA((6,))] + (_gather_sems(n) if n else []),
        ),
        out_shape=[jax.ShapeDtypeStruct((T, IN_COLS), BF16),
                   jax.ShapeDtypeStruct((2, N_CHIPS, HALF_D, SHARD_COLS), BF16)] + _gathered_shapes(riders),
        compiler_params=_cparams(2),
    )(order, h, w0, *riders)


def _outproj(ya, yb, yc, w, x):
    T = x.shape[0]
    tm = min(512, T)
    tn = 1024

    def body(ya_ref, yb_ref, yc_ref, w_ref, x_ref, o_ref, y_ref):
        acc = jnp.dot(ya_ref[...], w_ref[0:S5_W, :], preferred_element_type=F32)
        acc += jnp.dot(yb_ref[...], w_ref[S5_W:S5_W + SGU_W, :], preferred_element_type=F32)
        acc += jnp.dot(yc_ref[...], w_ref[S5_W + SGU_W:D_MODEL, :], preferred_element_type=F32)
        o_ref[...] = x_ref[...] + acc

        @pl.when(pl.program_id(1) == 0)
        def _():
            y_ref[:, 0:S5_W] = ya_ref[...]
            y_ref[:, S5_W:S5_W + SGU_W] = yb_ref[...]
            y_ref[:, S5_W + SGU_W:D_MODEL] = yc_ref[...]

    return pl.pallas_call(
        body,
        name="outproj",
        grid=(T // tm, D_MODEL // tn),
        in_specs=[
            pl.BlockSpec((tm, S5_W), lambda i, j: (i, 0)),
            pl.BlockSpec((tm, SGU_W), lambda i, j: (i, 0)),
            pl.BlockSpec((tm, POOL_W), lambda i, j: (i, 0)),
            pl.BlockSpec((D_MODEL, tn), lambda i, j: (0, j)),
            pl.BlockSpec((tm, tn), lambda i, j: (i, j)),
        ],
        out_specs=[
            pl.BlockSpec((tm, tn), lambda i, j: (i, j)),
            pl.BlockSpec((tm, D_MODEL), lambda i, j: (i, 0)),
        ],
        out_shape=[jax.ShapeDtypeStruct((T, D_MODEL), F32), jax.ShapeDtypeStruct((T, D_MODEL), BF16)],
        compiler_params=_cparams(2),
    )(ya, yb, yc, w, x)


def _outproj_bwd_dy(dxo, w):
    T = dxo.shape[0]
    tm = min(512, T)
    tn = 1024

    def body(d_ref, w_ref, o_ref, ds_ref):
        @pl.when(pl.program_id(1) == 0)
        def _():
            ds_ref[...] = d_ref[...].astype(BF16)

        o_ref[...] = lax.dot_general(ds_ref[...], w_ref[...], (((1,), (1,)), ((), ())),
                                     preferred_element_type=F32).astype(BF16)

    return pl.pallas_call(
        body,
        name="outproj_bwd_dy",
        grid=(T // tm, D_MODEL // tn),
        in_specs=[
            pl.BlockSpec((tm, D_MODEL), lambda i, j: (i, 0)),
            pl.BlockSpec((tn, D_MODEL), lambda i, j: (j, 0)),
        ],
        out_specs=pl.BlockSpec((tm, tn), lambda i, j: (i, j)),
        out_shape=jax.ShapeDtypeStruct((T, D_MODEL), BF16),
        scratch_shapes=[pltpu.VMEM((tm, D_MODEL), BF16)],
        compiler_params=_cparams(2),
    )(dxo, w)


def _outproj_bwd_dw(y, dxo):
    T = y.shape[0]
    tm = min(512, T)
    tr = 1024

    def body(y_ref, d_ref, o_ref):
        @pl.when(pl.program_id(1) == 0)
        def _():
            o_ref[...] = jnp.zeros_like(o_ref)

        o_ref[...] += _dot_tn(y_ref[...], d_ref[...])

    return pl.pallas_call(
        body,
        name="outproj_bwd_dw",
        grid=(D_MODEL // tr, T // tm),
        in_specs=[
            pl.BlockSpec((tm, tr), lambda p, t: (t, p)),
            pl.BlockSpec((tm, D_MODEL), lambda p, t: (t, 0)),
        ],
        out_specs=pl.BlockSpec((tr, D_MODEL), lambda p, t: (p, 0)),
        out_shape=jax.ShapeDtypeStruct((D_MODEL, D_MODEL), F32),
        compiler_params=_cparams(2),
    )(y, dxo)


def _inproj_bwd_dw(h, dz):
    T = h.shape[0]
    tm = min(512, T)

    def body(h_ref, dz_ref, o_ref):
        @pl.when(pl.program_id(1) == 0)
        def _():
            o_ref[...] = jnp.zeros_like(o_ref)

        o_ref[...] += _dot_tn(h_ref[...], dz_ref[...])

    return pl.pallas_call(
        body,
        name="inproj_bwd_dw",
        grid=(N_CHIPS, T // tm),
        in_specs=[
            pl.BlockSpec((tm, D_MODEL), lambda j, t: (t, 0)),
            pl.BlockSpec((tm, SHARD_COLS), lambda j, t: (t, j)),
        ],
        out_specs=pl.BlockSpec((None, D_MODEL, SHARD_COLS), lambda j, t: (j, 0, 0)),
        out_shape=jax.ShapeDtypeStruct((N_CHIPS, D_MODEL, SHARD_COLS), F32),
        compiler_params=_cparams(2),
    )(h, dz)


def _dx_tile(T):
    return min(512, max(T // 4, 8))


def _inproj_bwd_dx(dz, w4, x, g, dxo, rider=None, tiles=None, prev=None):
    T = x.shape[0]
    tm = _dx_tile(T)
    t0, ni = tiles if tiles else (0, T // tm)
    nk = N_CHIPS
    nt = (((1,), (1,)), ((), ()))
    n_in = 5 if prev is None else 6

    def body(dz_ref, w_ref, x_ref, g_ref, dxo_ref, *rest):
        dx_ref, dg_ref, acc_ref = rest[-3:]
        i, j = pl.program_id(0), pl.program_id(1)
        lo = lax.dot_general(dz_ref[...], w_ref[0], nt, preferred_element_type=F32)
        hi = lax.dot_general(dz_ref[...], w_ref[1], nt, preferred_element_type=F32)

        @pl.when(j == 0)
        def _():
            acc_ref[:, 0:HALF_D] = lo
            acc_ref[:, HALF_D:D_MODEL] = hi

        @pl.when(j > 0)
        def _():
            acc_ref[:, 0:HALF_D] += lo
            acc_ref[:, HALF_D:D_MODEL] += hi

        @pl.when(j == nk - 1)
        def _():
            @pl.when(i == 0)
            def _():
                dg_ref[...] = jnp.zeros_like(dg_ref)

            rc = min(128, tm)
            for c in range(tm // rc):
                rows = slice(c * rc, (c + 1) * rc)
                dh = acc_ref[rows, :]
                xv = x_ref[rows, :]
                r = lax.rsqrt(jnp.mean(xv * xv, axis=-1, keepdims=True) + RMS_EPS)
                xh = xv * r
                w = dh * g_ref[...]
                dx_ref[rows, :] = dxo_ref[rows, :] + r * (w - xh * jnp.mean(w * xh, axis=-1, keepdims=True))
                dg_ref[...] += jnp.sum(dh * xh, axis=0, keepdims=True)

    r_in, r_out, r_shapes, r_sems, r_args = _rider_specs(rider)
    return pl.pallas_call(
        _ride(body, n_in, 2, rider, lambda: (pl.program_id(0) == 0) & (pl.program_id(1) == 0),
              lambda: (pl.program_id(0) == ni - 1) & (pl.program_id(1) == nk - 1)),
        name="inproj_bwd_dx" + ("" if rider is None else "_ride") + ("" if prev is None else "_rest"),
        grid=(ni, nk),
        in_specs=[
            pl.BlockSpec((tm, SHARD_COLS), lambda i, j: (i + t0, j)),
            pl.BlockSpec((2, None, HALF_D, SHARD_COLS), lambda i, j: (0, j, 0, 0)),
            pl.BlockSpec((tm, D_MODEL), lambda i, j: (i + t0, 0)),
            pl.BlockSpec((1, D_MODEL), lambda i, j: (0, 0)),
            pl.BlockSpec((tm, D_MODEL), lambda i, j: (i + t0, 0)),
        ] + ([] if prev is None else [pl.BlockSpec(memory_space=pl.ANY)]) + r_in,
        out_specs=[
            pl.BlockSpec((tm, D_MODEL), lambda i, j: (i + t0, 0)),
            pl.BlockSpec((1, D_MODEL), lambda i, j: (0, 0)),
        ] + r_out,
        out_shape=[jax.ShapeDtypeStruct((T, D_MODEL), F32), jax.ShapeDtypeStruct((1, D_MODEL), F32)] + r_shapes,
        scratch_shapes=[pltpu.VMEM((tm, D_MODEL), F32)] + r_sems,
        input_output_aliases={} if prev is None else {5: 0},
        compiler_params=_cparams(2),
    )(dz, w4, x, g, dxo, *([] if prev is None else [prev]), *r_args)


def _outproj_loss(ya, yb, yc, w, x, g, tgt):
    T = x.shape[0]
    tm = min(256, T)

    def body(ya_ref, yb_ref, yc_ref, w_ref, x_ref, g_ref, t_ref, dx_ref, l_ref, dg_ref, y_ref):
        i = pl.program_id(0)
        acc = jnp.dot(ya_ref[...], w_ref[0:S5_W, :], preferred_element_type=F32)
        acc += jnp.dot(yb_ref[...], w_ref[S5_W:S5_W + SGU_W, :], preferred_element_type=F32)
        acc += jnp.dot(yc_ref[...], w_ref[S5_W + SGU_W:D_MODEL, :], preferred_element_type=F32)
        y_ref[:, 0:S5_W] = ya_ref[...]
        y_ref[:, S5_W:S5_W + SGU_W] = yb_ref[...]
        y_ref[:, S5_W + SGU_W:D_MODEL] = yc_ref[...]
        xv = x_ref[...] + acc
        r = lax.rsqrt(jnp.mean(xv * xv, axis=-1, keepdims=True) + RMS_EPS)
        xh = xv * r
        err = xh * g_ref[...] - t_ref[...]
        lpart = 0.5 * jnp.sum(jnp.mean(err * err, axis=-1, keepdims=True), axis=0, keepdims=True)
        dout = err * (1.0 / D_MODEL)
        w = dout * g_ref[...]
        dx_ref[...] = r * (w - xh * jnp.mean(w * xh, axis=-1, keepdims=True))
        gpart = jnp.sum(dout * xh, axis=0, keepdims=True)

        @pl.when(i == 0)
        def _():
            l_ref[...] = jnp.broadcast_to(lpart, l_ref.shape)
            dg_ref[...] = gpart

        @pl.when(i > 0)
        def _():
            l_ref[...] += jnp.broadcast_to(lpart, l_ref.shape)
            dg_ref[...] += gpart

    row = lambda w: pl.BlockSpec((tm, w), lambda i: (i, 0))
    return pl.pallas_call(
        body,
        name="outproj_loss",
        grid=(T // tm,),
        in_specs=[row(S5_W), row(SGU_W), row(POOL_W), _full((D_MODEL, D_MODEL)), row(D_MODEL), _full((1, D_MODEL)),
                  row(D_MODEL)],
        out_specs=[row(D_MODEL), _full((1, 128)), _full((1, D_MODEL)), row(D_MODEL)],
        out_shape=[
            jax.ShapeDtypeStruct((T, D_MODEL), F32),
            jax.ShapeDtypeStruct((1, 128), F32),
            jax.ShapeDtypeStruct((1, D_MODEL), F32),
            jax.ShapeDtypeStruct((T, D_MODEL), BF16),
        ],
        compiler_params=_cparams(1),
    )(ya, yb, yc, w, x, g, tgt)


def _s5_prep(lam_re, lam_im, b_re, b_im, log_dt):
    lam = lax.complex(lam_re, lam_im)
    dt = jnp.exp(log_dt)[:, None]
    a = jnp.exp(lam * dt)
    bbar = ((a - 1.0) / lam)[..., None] * lax.complex(b_re, b_im)
    return jnp.real(a), jnp.imag(a), jnp.real(bbar), jnp.imag(bbar)


def _block_diag_in(m):
    m4 = m.reshape(SUPER, 8, S5_STATE, S5_CH)
    eye = jnp.eye(8, dtype=m.dtype)
    out = jnp.einsum("jgph,gk->jghkp", m4, eye)
    return out.reshape(SUPER, 8 * S5_CH, 8 * S5_STATE)


def _block_diag_in_grad(d):
    d6 = d.reshape(SUPER, 8, S5_CH, 8, S5_STATE)
    diag = jnp.einsum("jghgp->jgph", d6)
    return diag.reshape(S5_GROUPS, S5_STATE, S5_CH)


def _block_diag_out(m):
    m4 = m.reshape(SUPER, 8, S5_CH, S5_STATE)
    eye = jnp.eye(8, dtype=m.dtype)
    out = jnp.einsum("jghp,gk->jgpkh", m4, eye)
    return out.reshape(SUPER, 8 * S5_STATE, 8 * S5_CH)


def _block_diag_out_grad(d):
    d6 = d.reshape(SUPER, 8, S5_STATE, 8, S5_CH)
    diag = jnp.einsum("jgpgh->jghp", d6)
    return diag.reshape(S5_GROUPS, S5_CH, S5_STATE)


def _scan_coefs(a_re, a_im, reverse):
    a = lax.complex(a_re.reshape(-1), a_im.reshape(-1))
    if reverse:
        a = jnp.conj(a)
    pw = [a]
    for _ in range(7):
        pw.append(pw[-1] * a)
    rows = jnp.arange(8)

    def masked(k):
        m = (rows + k <= 7) if reverse else (rows >= k)
        return jnp.where(m[:, None], pw[k - 1][None, :], 0.0)

    a1, a2, a4 = masked(1), masked(2), masked(4)
    carry = jnp.stack([pw[7 - r] for r in range(8)]) if reverse else jnp.stack(pw)
    parts = []
    for c in (a1, a2, a4, carry):
        parts += [jnp.real(c), jnp.imag(c)]
    return jnp.stack(parts).astype(F32)


def _scan_block(r, im, coef_ref, cs, reverse):
    for k, idx in ((1, 0), (2, 2), (4, 4)):
        ar = coef_ref[idx, :, cs]
        ai = coef_ref[idx + 1, :, cs]
        sh = 8 - k if reverse else k
        rr = pltpu.roll(r, sh, 0)
        ri = pltpu.roll(im, sh, 0)
        r, im = r + ar * rr - ai * ri, im + ar * ri + ai * rr
    return r, im


def _s5_fwd(z, p, rider=None):
    T = z.shape[0]
    tm = min(512, T)
    nblk = tm // 8
    W = STATE_W

    def body(xa_ref, ga_ref, bre_ref, bim_ref, cre_ref, cim_ref, dv_ref, wg_ref, bg_ref, coef_ref,
             ya_ref, yraw_ref, sre_ref, sim_ref, wre, wim):
        @pl.when(pl.program_id(0) == 0)
        def _():
            wre[0:8, :] = jnp.zeros((8, W), F32)
            wim[0:8, :] = jnp.zeros((8, W), F32)

        xa = xa_ref[...].astype(F32)
        xab = xa.astype(BF16)
        for j in range(SUPER):
            xj = xab[:, j * 128:(j + 1) * 128]
            wre[8:8 + tm, j * 512:(j + 1) * 512] = jnp.dot(xj, bre_ref[j], preferred_element_type=F32)
            wim[8:8 + tm, j * 512:(j + 1) * 512] = jnp.dot(xj, bim_ref[j], preferred_element_type=F32)

        def blk(b, carry):
            base = pl.multiple_of(8 + b * 8, 8)
            for cc in range(W // SCAN_COLS):
                cs = pl.ds(cc * SCAN_COLS, SCAN_COLS)
                r, im = _scan_block(wre[pl.ds(base, 8), cs], wim[pl.ds(base, 8), cs], coef_ref, cs, False)
                cr = wre[pl.ds(base - 1, 1), cs]
                ci = wim[pl.ds(base - 1, 1), cs]
                pr = coef_ref[6, :, cs]
                pi = coef_ref[7, :, cs]
                wre[pl.ds(base, 8), cs] = r + pr * cr - pi * ci
                wim[pl.ds(base, 8), cs] = im + pr * ci + pi * cr
            return carry

        lax.fori_loop(0, nblk, blk, 0)
        wre[0:8, :] = wre[tm:tm + 8, :]
        wim[0:8, :] = wim[tm:tm + 8, :]
        sre_ref[...] = wre[8:8 + tm, :]
        sim_ref[...] = wim[8:8 + tm, :]

        for j in range(SUPER):
            yr = jnp.dot(wre[8:8 + tm, j * 512:(j + 1) * 512].astype(BF16), cre_ref[j], preferred_element_type=F32)
            yr += jnp.dot(wim[8:8 + tm, j * 512:(j + 1) * 512].astype(BF16), cim_ref[j], preferred_element_type=F32)
            yraw_ref[:, j * 128:(j + 1) * 128] = yr
        yraw = yraw_ref[...] + dv_ref[...] * xa
        yraw_ref[...] = yraw
        yg = _gelu(yraw)
        q = jnp.dot(yg.astype(BF16), wg_ref[...], preferred_element_type=F32) + bg_ref[...]
        sga, _ = _silu_and_grad(ga_ref[...].astype(F32))
        ya_ref[...] = (yg * jax.nn.sigmoid(q) * sga).astype(BF16)

    nt = T // tm
    r_in, r_out, r_shapes, r_sems, r_args = _rider_specs(rider)
    return pl.pallas_call(
        _ride(body, 10, 4, rider, lambda: pl.program_id(0) == 0, lambda: pl.program_id(0) == nt - 1,
              lambda: pl.program_id(0) == nt - 1),
        name="s5_fwd" + ("" if rider is None else "_ride"),
        grid=(nt,),
        in_specs=[
            pl.BlockSpec((tm, S5_W), lambda i: (i, 0)),
            pl.BlockSpec((tm, S5_W), lambda i: (i, 6)),
            _full((SUPER, 128, 512)), _full((SUPER, 128, 512)),
            _full((SUPER, 512, 128)), _full((SUPER, 512, 128)),
            _full((1, S5_W)), _full((S5_W, S5_W)), _full((1, S5_W)),
            _full((8, 8, W)),
        ] + r_in,
        out_specs=[
            pl.BlockSpec((tm, S5_W), lambda i: (i, 0)),
            pl.BlockSpec((tm, S5_W), lambda i: (i, 0)),
            pl.BlockSpec((tm, W), lambda i: (i, 0)),
            pl.BlockSpec((tm, W), lambda i: (i, 0)),
        ] + r_out,
        out_shape=[
            jax.ShapeDtypeStruct((T, S5_W), BF16),
            jax.ShapeDtypeStruct((T, S5_W), F32),
            jax.ShapeDtypeStruct((T, W), F32),
            jax.ShapeDtypeStruct((T, W), F32),
        ] + r_shapes,
        scratch_shapes=[pltpu.VMEM((tm + 8, W), F32), pltpu.VMEM((tm + 8, W), F32)] + r_sems,
        compiler_params=_cparams(1),
    )(z, z, p["b4re"], p["b4im"], p["c4re"], p["c4im"], p["dvec"], p["wglu"], p["bglu"], p["coef_f"], *r_args)


def _s5_bwd(dy, z, yraw, sre, sim, p, rider=None):
    T = z.shape[0]
    tm = min(512, T)
    nt = T // tm
    nblk = tm // 8
    W = STATE_W
    rev = lambda i: nt - 1 - i

    def body(dya_ref, xa_ref, ga_ref, yraw_ref, sre_ref, sim_ref, hre_ref, him_ref,
             bre_t_ref, bim_t_ref, cre_t_ref, cim_t_ref, dv_ref, wg_ref, wgt_ref, bg_ref, coef_ref,
             dz_ref, dbre_ref, dbim_ref, dcre_ref, dcim_ref, dd_ref, dwg_ref, dbg_ref, da_ref,
             wre, wim, dyr_ref, xa_stage, ga_stage, xa_sem, ga_sem):
        i = pl.program_id(0)
        xa_out = _ColumnWriter(xa_stage, xa_sem, dz_ref, 0, i, nt)
        ga_out = _ColumnWriter(ga_stage, ga_sem, dz_ref, 6 * 512, i, nt)
        dxa_ref, dga_ref = xa_out.slot(), ga_out.slot()

        @pl.when(i == 0)
        def _():
            wre[tm:tm + 8, :] = jnp.zeros((8, W), F32)
            wim[tm:tm + 8, :] = jnp.zeros((8, W), F32)
            for ref in (dbre_ref, dbim_ref, dcre_ref, dcim_ref, dd_ref, dwg_ref, dbg_ref, da_ref):
                ref[...] = jnp.zeros_like(ref)

        xa = xa_ref[...].astype(F32)
        dya = dya_ref[...].astype(F32)
        yg, dgelu = _gelu_and_grad(yraw_ref[...])
        ygb = yg.astype(BF16)
        q = jnp.dot(ygb, wg_ref[...], preferred_element_type=F32) + bg_ref[...]
        sq = jax.nn.sigmoid(q)
        sga, dsga = _silu_and_grad(ga_ref[...].astype(F32))
        dga_ref[...] = (dya * (yg * sq) * dsga).astype(BF16)
        dya0 = dya * sga
        dq = dya0 * yg * sq * (1.0 - sq)
        dqb = dq.astype(BF16)
        dyg = dya0 * sq + jnp.dot(dqb, wgt_ref[...], preferred_element_type=F32)
        dwg_ref[...] += _dot_tn(ygb, dqb)
        dbg_ref[...] += jnp.sum(dq, axis=0, keepdims=True)
        dyraw = dyg * dgelu
        dd_ref[...] += jnp.sum(dyraw * xa, axis=0, keepdims=True)
        dyr_ref[...] = dyraw.astype(BF16)

        for j in range(SUPER):
            dj = dyr_ref[:, j * 128:(j + 1) * 128]
            wre[0:tm, j * 512:(j + 1) * 512] = jnp.dot(dj, cre_t_ref[j], preferred_element_type=F32)
            wim[0:tm, j * 512:(j + 1) * 512] = jnp.dot(dj, cim_t_ref[j], preferred_element_type=F32)

        row0 = lax.broadcasted_iota(jnp.int32, (8, SCAN_COLS), 0) == 0
        head_on = (i < nt - 1).astype(F32)

        def one_block(base, first):
            for cc in range(W // SCAN_COLS):
                cs = pl.ds(cc * SCAN_COLS, SCAN_COLS)
                r, im = _scan_block(wre[pl.ds(base, 8), cs], wim[pl.ds(base, 8), cs], coef_ref, cs, True)
                cr = wre[pl.ds(base + 8, 1), cs]
                ci = wim[pl.ds(base + 8, 1), cs]
                pr = coef_ref[6, :, cs]
                pi = coef_ref[7, :, cs]
                r, im = r + pr * cr - pi * ci, im + pr * ci + pi * cr
                wre[pl.ds(base, 8), cs] = r
                wim[pl.ds(base, 8), cs] = im
                if first:
                    pre = hre_ref[7:8, cs] * head_on
                    pim = him_ref[7:8, cs] * head_on
                else:
                    pre = sre_ref[pl.ds(base - 1, 1), cs]
                    pim = sim_ref[pl.ds(base - 1, 1), cs]
                spr = jnp.where(row0, pre, pltpu.roll(sre_ref[pl.ds(base, 8), cs], 1, 0))
                spi = jnp.where(row0, pim, pltpu.roll(sim_ref[pl.ds(base, 8), cs], 1, 0))
                da_ref[0, :, cs] += r * spr + im * spi
                da_ref[1, :, cs] += im * spr - r * spi

        def blk(b, carry):
            one_block(pl.multiple_of((nblk - 1 - b) * 8, 8), False)
            return carry

        lax.fori_loop(0, nblk - 1, blk, 0)
        one_block(0, True)
        wre[tm:tm + 8, :] = wre[0:8, :]
        wim[tm:tm + 8, :] = wim[0:8, :]

        xab = xa.astype(BF16)
        for j in range(SUPER):
            cols = slice(j * 512, (j + 1) * 512)
            gre = wre[0:tm, cols].astype(BF16)
            gim = wim[0:tm, cols].astype(BF16)
            xj = xab[:, j * 128:(j + 1) * 128]
            dj = dyr_ref[:, j * 128:(j + 1) * 128]
            dbre_ref[j] += _dot_tn(xj, gre)
            dbim_ref[j] += _dot_tn(xj, gim)
            dcre_ref[j] += _dot_tn(sre_ref[:, cols], dj)
            dcim_ref[j] += _dot_tn(sim_ref[:, cols], dj)
            dxj = jnp.dot(gre, bre_t_ref[j], preferred_element_type=F32)
            dxj += jnp.dot(gim, bim_t_ref[j], preferred_element_type=F32)
            dxj += dyraw[:, j * 128:(j + 1) * 128] * dv_ref[:, j * 128:(j + 1) * 128]
            dxa_ref[:, j * 128:(j + 1) * 128] = dxj.astype(BF16)
        xa_out.send(rev(i) * tm)
        ga_out.send(rev(i) * tm)

    acc = lambda shape: _full(shape)
    hb = tm // 8
    r_in, r_out, r_shapes, r_sems, r_args = _rider_specs(rider)
    stages, stage_sems = _stage_scratch(tm, (S5_W, S5_W))
    return pl.pallas_call(
        _ride(body, 17, 9, rider, lambda: pl.program_id(0) == 0, lambda: pl.program_id(0) == nt - 1),
        name="s5_bwd" + ("" if rider is None else "_ride"),
        grid=(nt,),
        in_specs=[
            pl.BlockSpec((tm, S5_W), lambda i: (rev(i), 0)),
            pl.BlockSpec((tm, S5_W), lambda i: (rev(i), 0)),
            pl.BlockSpec((tm, S5_W), lambda i: (rev(i), 6)),
            pl.BlockSpec((tm, S5_W), lambda i: (rev(i), 0)),
            pl.BlockSpec((tm, W), lambda i: (rev(i), 0)),
            pl.BlockSpec((tm, W), lambda i: (rev(i), 0)),
            pl.BlockSpec((8, W), lambda i: (jnp.maximum(rev(i) * hb - 1, 0), 0)),
            pl.BlockSpec((8, W), lambda i: (jnp.maximum(rev(i) * hb - 1, 0), 0)),
            _full((SUPER, 512, 128)), _full((SUPER, 512, 128)),
            _full((SUPER, 128, 512)), _full((SUPER, 128, 512)),
            _full((1, S5_W)), _full((S5_W, S5_W)), _full((S5_W, S5_W)), _full((1, S5_W)),
            _full((8, 8, W)),
        ] + r_in,
        out_specs=[
            pl.BlockSpec(memory_space=pl.ANY),
            acc((SUPER, 128, 512)), acc((SUPER, 128, 512)),
            acc((SUPER, 512, 128)), acc((SUPER, 512, 128)),
            acc((1, S5_W)), acc((S5_W, S5_W)), acc((1, S5_W)), acc((2, 8, W)),
        ] + r_out,
        out_shape=[
            jax.ShapeDtypeStruct((T, IN_COLS), BF16),
            jax.ShapeDtypeStruct((SUPER, 128, 512), F32), jax.ShapeDtypeStruct((SUPER, 128, 512), F32),
            jax.ShapeDtypeStruct((SUPER, 512, 128), F32), jax.ShapeDtypeStruct((SUPER, 512, 128), F32),
            jax.ShapeDtypeStruct((1, S5_W), F32), jax.ShapeDtypeStruct((S5_W, S5_W), F32),
            jax.ShapeDtypeStruct((1, S5_W), F32), jax.ShapeDtypeStruct((2, 8, W), F32),
        ] + r_shapes,
        scratch_shapes=[pltpu.VMEM((tm + 8, W), F32), pltpu.VMEM((tm + 8, W), F32), pltpu.VMEM((tm, S5_W), BF16)]
        + stages + stage_sems + r_sems,
        compiler_params=_cparams(1),
    )(dy, z, z, yraw, sre, sim, sre, sim,
      p["b4re_t"], p["b4im_t"], p["c4re_t"], p["c4im_t"], p["dvec"], p["wglu"], p["wglu_t"], p["bglu"], p["coef_r"],
      *r_args)


def _ln_fwd(vf, lng, lnb):
    mu = jnp.mean(vf, axis=-1, keepdims=True)
    d = vf - mu
    rstd = lax.rsqrt(jnp.mean(d * d, axis=-1, keepdims=True) + LN_EPS)
    xh = d * rstd
    return xh, rstd, xh * lng + lnb


def _col_block(tm, b):
    return pl.BlockSpec((tm, 512), lambda i: (i, b))


def _ln_halves(vf0, vf1):
    mu = (jnp.sum(vf0, axis=-1, keepdims=True) + jnp.sum(vf1, axis=-1, keepdims=True)) * (1.0 / SGU_W)
    d0, d1 = vf0 - mu, vf1 - mu
    var = (jnp.sum(d0 * d0, axis=-1, keepdims=True) + jnp.sum(d1 * d1, axis=-1, keepdims=True)) * (1.0 / SGU_W)
    rstd = lax.rsqrt(var + LN_EPS)
    return d0 * rstd, d1 * rstd, rstd


def _sgu_fwd(z, ws, bsf, lng, lnb):
    T = z.shape[0]
    tm = min(512, T)

    def body(u0, u1, v0, v1, g0, g1, ws_ref, bs_ref, lng_ref, lnb_ref, yb_ref, vn_ref):
        for c in range(tm // CHUNK):
            rows = slice(c * CHUNK, (c + 1) * CHUNK)
            xh0, xh1, _ = _ln_halves(_gelu(v0[rows, :].astype(F32)), _gelu(v1[rows, :].astype(F32)))
            vn_ref[:, 0:512] = (xh0 * lng_ref[:, 0:512] + lnb_ref[:, 0:512]).astype(BF16)
            vn_ref[:, 512:1024] = (xh1 * lng_ref[:, 512:1024] + lnb_ref[:, 512:1024]).astype(BF16)
            for half, (u_ref, g_ref) in enumerate(((u0, g0), (u1, g1))):
                sg, _ = _silu_and_grad(g_ref[rows, :].astype(F32))
                m = _gelu(u_ref[rows, :].astype(F32)) * sg
                for hh in range(SGU_HEADS // 2):
                    h = half * (SGU_HEADS // 2) + hh
                    cols = slice(h * 128, (h + 1) * 128)
                    s = jnp.dot(ws_ref[h], vn_ref[:, cols], preferred_element_type=F32) + bs_ref[:, cols]
                    yb_ref[rows, cols] = (m[:, hh * 128:(hh + 1) * 128] * s).astype(BF16)

    return pl.pallas_call(
        body,
        name="sgu_fwd",
        grid=(T // tm,),
        in_specs=[_col_block(tm, b) for b in (1, 2, 3, 4, 7, 8)] + [
            _full((SGU_HEADS, CHUNK, CHUNK)), _full((CHUNK, SGU_W)), _full((1, SGU_W)), _full((1, SGU_W)),
        ],
        out_specs=pl.BlockSpec((tm, SGU_W), lambda i: (i, 0)),
        out_shape=jax.ShapeDtypeStruct((T, SGU_W), BF16),
        scratch_shapes=[pltpu.VMEM((CHUNK, SGU_W), BF16)],
        compiler_params=_cparams(1),
    )(z, z, z, z, z, z, ws, bsf, lng, lnb)


def _sgu_bwd(dy, z, dz, ws, ws_t, bsf, lng, lnb, rider=None):
    T = z.shape[0]
    tm = min(512, T)
    HH = SGU_HEADS // 2

    def body(u0, u1, v0, v1, g0, g1, dy0, dy1, ws_ref, wst_ref, bs_ref, lng_ref, lnb_ref, dz_in,
             dz_ref, dws_ref, dbs_ref, dlng_ref, dlnb_ref, vn_ref, dvn_ref, *stage):
        step = pl.program_id(0)
        outs = [_ColumnWriter(stage[k], stage[3 + k], dz_ref, col, step, T // tm)
                for k, col in enumerate((512, 1536, 3584))]
        du_ref, dv_ref, dgb_ref = (o.slot() for o in outs)

        @pl.when(step == 0)
        def _():
            for ref in (dws_ref, dbs_ref, dlng_ref, dlnb_ref):
                ref[...] = jnp.zeros_like(ref)

        for c in range(tm // CHUNK):
            rows = slice(c * CHUNK, (c + 1) * CHUNK)
            vf0, dgv0 = _gelu_and_grad(v0[rows, :].astype(F32))
            vf1, dgv1 = _gelu_and_grad(v1[rows, :].astype(F32))
            xh0, xh1, rstd = _ln_halves(vf0, vf1)
            vn_ref[:, 0:512] = (xh0 * lng_ref[:, 0:512] + lnb_ref[:, 0:512]).astype(BF16)
            vn_ref[:, 512:1024] = (xh1 * lng_ref[:, 512:1024] + lnb_ref[:, 512:1024]).astype(BF16)
            for half, (u_ref, g_ref, dy_ref) in enumerate(((u0, g0, dy0), (u1, g1, dy1))):
                ug, dgu = _gelu_and_grad(u_ref[rows, :].astype(F32))
                sg, dsg = _silu_and_grad(g_ref[rows, :].astype(F32))
                dyb = dy_ref[rows, :].astype(F32)
                dyb0 = dyb * sg
                ds_half = dyb0 * ug
                du_scale = dyb0 * dgu
                dg_scale = dyb * ug * dsg
                for hh in range(HH):
                    h = half * HH + hh
                    cols = slice(h * 128, (h + 1) * 128)
                    lc = slice(hh * 128, (hh + 1) * 128)
                    s = jnp.dot(ws_ref[h], vn_ref[:, cols], preferred_element_type=F32) + bs_ref[:, cols]
                    du_ref[rows, cols] = (du_scale[:, lc] * s).astype(BF16)
                    dgb_ref[rows, cols] = (dg_scale[:, lc] * s).astype(BF16)
                    ds = ds_half[:, lc]
                    dbs_ref[:, cols] += ds
                    dsb = ds.astype(BF16)
                    dws_ref[h] += _dot_nt(dsb, vn_ref[:, cols])
                    dvn_ref[:, cols] = jnp.dot(wst_ref[h], dsb, preferred_element_type=F32)
            dvn0 = dvn_ref[:, 0:512]
            dvn1 = dvn_ref[:, 512:1024]
            dlnb_ref[:, 0:512] += jnp.sum(dvn0, axis=0, keepdims=True)
            dlnb_ref[:, 512:1024] += jnp.sum(dvn1, axis=0, keepdims=True)
            dlng_ref[:, 0:512] += jnp.sum(dvn0 * xh0, axis=0, keepdims=True)
            dlng_ref[:, 512:1024] += jnp.sum(dvn1 * xh1, axis=0, keepdims=True)
            dxh0 = dvn0 * lng_ref[:, 0:512]
            dxh1 = dvn1 * lng_ref[:, 512:1024]
            m1 = (jnp.sum(dxh0, axis=-1, keepdims=True) + jnp.sum(dxh1, axis=-1, keepdims=True)) * (1.0 / SGU_W)
            m2 = (jnp.sum(dxh0 * xh0, axis=-1, keepdims=True) + jnp.sum(dxh1 * xh1, axis=-1, keepdims=True)) * (1.0 / SGU_W)
            dv_ref[rows, 0:512] = (rstd * (dxh0 - m1 - xh0 * m2) * dgv0).astype(BF16)
            dv_ref[rows, 512:1024] = (rstd * (dxh1 - m1 - xh1 * m2) * dgv1).astype(BF16)
        for o in outs:
            o.send(step * tm)

    anyspec = pl.BlockSpec(memory_space=pl.ANY)
    r_in, r_out, r_shapes, r_sems, r_args = _rider_specs(rider)
    stages, stage_sems = _stage_scratch(tm, (SGU_W, SGU_W, SGU_W))
    return pl.pallas_call(
        _ride(body, 14, 5, rider, lambda: pl.program_id(0) == 0, lambda: pl.program_id(0) == T // tm - 1),
        name="sgu_bwd" + ("" if rider is None else "_ride"),
        grid=(T // tm,),
        in_specs=[_col_block(tm, b) for b in (1, 2, 3, 4, 7, 8)] + [_col_block(tm, 1), _col_block(tm, 2)] + [
            _full((SGU_HEADS, CHUNK, CHUNK)), _full((SGU_HEADS, CHUNK, CHUNK)),
            _full((CHUNK, SGU_W)), _full((1, SGU_W)), _full((1, SGU_W)), anyspec,
        ] + r_in,
        out_specs=[anyspec,
                   _full((SGU_HEADS, CHUNK, CHUNK)), _full((CHUNK, SGU_W)), _full((1, SGU_W)), _full((1, SGU_W))] + r_out,
        input_output_aliases={13: 0},
        out_shape=[
            jax.ShapeDtypeStruct((T, IN_COLS), BF16),
            jax.ShapeDtypeStruct((SGU_HEADS, CHUNK, CHUNK), F32), jax.ShapeDtypeStruct((CHUNK, SGU_W), F32),
            jax.ShapeDtypeStruct((1, SGU_W), F32), jax.ShapeDtypeStruct((1, SGU_W), F32),
        ] + r_shapes,
        scratch_shapes=[pltpu.VMEM((CHUNK, SGU_W), BF16), pltpu.VMEM((CHUNK, SGU_W), F32)] + stages + stage_sems + r_sems,
        compiler_params=_cparams(1),
    )(z, z, z, z, z, z, dy, dy, ws, ws_t, bsf, lng, lnb, dz, *r_args)


def _pool_den(first_row, n):
    return (lax.broadcasted_iota(jnp.int32, (n, 1), 0) + first_row + 1).astype(F32)


def _pool_p(ext, xc, pos, tm):
    w2 = ext + pltpu.roll(ext, 1, 0)
    w4 = w2 + pltpu.roll(w2, 2, 0)
    w8 = w4 + pltpu.roll(w4, 4, 0)
    w16 = w8 + pltpu.roll(w8, 8, 0)
    out = []
    for g, (w, ws) in enumerate(zip(POOL_WINDOWS, (w2, w4, w8, w16))):
        cols = slice(g * 128, (g + 1) * 128)
        mean = ws[POOL_HALO:POOL_HALO + tm, cols] / jnp.minimum(pos, float(w))
        out.append(mean - xc[:, cols])
    return out


def _pool_fwd(z, wp, scale):
    T = z.shape[0]
    tm = min(512, T)
    hb = tm // POOL_HALO

    def body(xc_ref, hx_ref, gc_ref, wp_ref, sc_ref, yc_ref):
        i = pl.program_id(0)
        xc = xc_ref[...].astype(F32)
        halo = hx_ref[...].astype(F32) * (i > 0).astype(F32)
        ext = jnp.concatenate([halo, xc], axis=0)
        ps = _pool_p(ext, xc, _pool_den(i * tm, tm), tm)
        sg, _ = _silu_and_grad(gc_ref[...].astype(F32))
        for g in range(4):
            cols = slice(g * 128, (g + 1) * 128)
            pw = _dot(ps[g], wp_ref[g])
            yc_ref[:, cols] = (pw * sc_ref[:, cols] * sg[:, cols]).astype(BF16)

    return pl.pallas_call(
        body,
        name="pool_fwd",
        grid=(T // tm,),
        in_specs=[
            _col_block(tm, 5),
            pl.BlockSpec((POOL_HALO, 512), lambda i: (jnp.maximum(i * hb - 1, 0), 5)),
            _col_block(tm, 9),
            _full((4, 128, 128)), _full((1, POOL_W)),
        ],
        out_specs=pl.BlockSpec((tm, POOL_W), lambda i: (i, 0)),
        out_shape=jax.ShapeDtypeStruct((T, POOL_W), BF16),
        compiler_params=_cparams(1),
    )(z, z, z, wp, scale)


def _pool_bwd(dy, z, dz, wp, wp_t, scale):
    T = z.shape[0]
    tm = min(512, T)
    nt = T // tm
    hb = tm // POOL_HALO
    last_hb = T // POOL_HALO - 1
    L = tm + POOL_HALO

    def body(xc_ref, hx_ref, gc_ref, gn_ref, dyc_ref, dyn_ref, wp_ref, wpt_ref, sc_ref, dz_in,
             dz_ref, dwp_ref, dsc_ref, xc_stage, gc_stage, xc_sem, gc_sem):
        i = pl.program_id(0)
        xc_out = _ColumnWriter(xc_stage, xc_sem, dz_ref, 5 * 512, i, nt)
        gc_out = _ColumnWriter(gc_stage, gc_sem, dz_ref, 9 * 512, i, nt)
        dxc_ref, dgc_ref = xc_out.slot(), gc_out.slot()

        @pl.when(i == 0)
        def _():
            dwp_ref[...] = jnp.zeros_like(dwp_ref)
            dsc_ref[...] = jnp.zeros_like(dsc_ref)

        xc = xc_ref[...].astype(F32)
        halo = hx_ref[...].astype(F32) * (i > 0).astype(F32)
        pos = _pool_den(i * tm, tm)
        ps = _pool_p(jnp.concatenate([halo, xc], axis=0), xc, pos, tm)
        sg, dsg = _silu_and_grad(gc_ref[...].astype(F32))
        dyc = dyc_ref[...].astype(F32)
        dyc0 = dyc * sg
        dpw = dyc0 * sc_ref[...]
        sgn, _ = _silu_and_grad(gn_ref[...].astype(F32))
        dpwn = dyn_ref[...].astype(F32) * sgn * sc_ref[...] * (i < nt - 1).astype(F32)
        posn = _pool_den((i + 1) * tm, POOL_HALO)
        dps, qs = [], []
        for g, w in enumerate(POOL_WINDOWS):
            cols = slice(g * 128, (g + 1) * 128)
            pw = _dot(ps[g], wp_ref[g])
            dgc_ref[:, cols] = (dyc[:, cols] * pw * sc_ref[:, cols] * dsg[:, cols]).astype(BF16)
            dsc_ref[:, cols] += jnp.sum(dyc0[:, cols] * pw, axis=0, keepdims=True)
            dwp_ref[g] += _dot_tn(ps[g], dpw[:, cols])
            dp = _dot(dpw[:, cols], wpt_ref[g])
            dpn = _dot(dpwn[:, cols], wpt_ref[g])
            dps.append(dp)
            qs.append(jnp.concatenate([dp / jnp.minimum(pos, float(w)), dpn / jnp.minimum(posn, float(w))], axis=0))
        ext = jnp.concatenate(qs, axis=1)
        f2 = ext + pltpu.roll(ext, L - 1, 0)
        f4 = f2 + pltpu.roll(f2, L - 2, 0)
        f8 = f4 + pltpu.roll(f4, L - 4, 0)
        f16 = f8 + pltpu.roll(f8, L - 8, 0)
        for g, f in enumerate((f2, f4, f8, f16)):
            cols = slice(g * 128, (g + 1) * 128)
            dxc_ref[:, cols] = (f[0:tm, cols] - dps[g]).astype(BF16)
        xc_out.send(i * tm)
        gc_out.send(i * tm)

    nxt = lambda i: jnp.minimum((i + 1) * hb, last_hb)
    anyspec = pl.BlockSpec(memory_space=pl.ANY)
    stages, stage_sems = _stage_scratch(tm, (POOL_W, POOL_W))
    return pl.pallas_call(
        body,
        name="pool_bwd",
        grid=(nt,),
        in_specs=[
            _col_block(tm, 5),
            pl.BlockSpec((POOL_HALO, 512), lambda i: (jnp.maximum(i * hb - 1, 0), 5)),
            _col_block(tm, 9),
            pl.BlockSpec((POOL_HALO, 512), lambda i: (nxt(i), 9)),
            _col_block(tm, 3),
            pl.BlockSpec((POOL_HALO, 512), lambda i: (nxt(i), 3)),
            _full((4, 128, 128)), _full((4, 128, 128)), _full((1, POOL_W)), anyspec,
        ],
        out_specs=[anyspec, _full((4, 128, 128)), _full((1, POOL_W))],
        input_output_aliases={9: 0},
        out_shape=[
            jax.ShapeDtypeStruct((T, IN_COLS), BF16),
            jax.ShapeDtypeStruct((4, 128, 128), F32), jax.ShapeDtypeStruct((1, POOL_W), F32),
        ],
        scratch_shapes=stages + stage_sems,
        compiler_params=_cparams(1),
    )(z, z, z, z, dy, dy, wp, wp_t, scale, dz)


def _row_tile(rows, cols):
    tr = 8
    while tr * 2 * cols * 4 <= 2 * 1024 * 1024 and rows % (tr * 2) == 0:
        tr *= 2
    return tr


def _add_own_half(part, recv, cidx):
    _, _, R2, C = part.shape
    tr = _row_tile(R2, C)

    def body(c_ref, a_ref, r_ref, o_ref):
        o_ref[...] = (a_ref[...] + r_ref[...]).astype(BF16)

    return pl.pallas_call(
        body,
        name="add_own_half",
        grid_spec=pltpu.PrefetchScalarGridSpec(
            num_scalar_prefetch=1,
            grid=(N_CHIPS, R2 // tr),
            in_specs=[
                pl.BlockSpec((None, None, tr, C), lambda j, i, c: (j, c[0], i, 0)),
                pl.BlockSpec((None, tr, C), lambda j, i, c: (j, i, 0)),
            ],
            out_specs=pl.BlockSpec((None, tr, C), lambda j, i, c: (j, i, 0)),
        ),
        out_shape=jax.ShapeDtypeStruct((N_CHIPS, R2, C), BF16),
        compiler_params=_cparams(2),
    )(cidx, part, recv)


def _add2(a, b):
    R, C = a.shape
    tr = _row_tile(R, C)

    def body(a_ref, b_ref, o_ref):
        o_ref[...] = a_ref[...] + b_ref[...]

    spec = pl.BlockSpec((tr, C), lambda i: (i, 0))
    return pl.pallas_call(
        body, name="add2", grid=(R // tr,), in_specs=[spec, spec], out_specs=spec,
        out_shape=jax.ShapeDtypeStruct((R, C), F32), compiler_params=_cparams(1),
    )(a, b)


def _sum_chips(parts):
    _, R, C = parts.shape
    tr = _row_tile(R, N_CHIPS * C)

    def body(p_ref, o_ref):
        p = [p_ref[j].astype(F32) for j in range(N_CHIPS)]
        o_ref[...] = ((p[0] + p[1]) + p[2]) + p[3]

    return pl.pallas_call(
        body, name="sum_chips", grid=(R // tr,),
        in_specs=[pl.BlockSpec((N_CHIPS, tr, C), lambda i: (0, i, 0))],
        out_specs=pl.BlockSpec((tr, C), lambda i: (i, 0)),
        out_shape=jax.ShapeDtypeStruct((R, C), F32), compiler_params=_cparams(1),
    )(parts)


def _adamw_math(w, g, m, v):
    m = ADAM_B1 * m + (1.0 - ADAM_B1) * g
    v = ADAM_B2 * v + (1.0 - ADAM_B2) * (g * g)
    m_hat = m / (1.0 - ADAM_B1 ** ADAM_STEP)
    v_hat = v / (1.0 - ADAM_B2 ** ADAM_STEP)
    delta = -ADAM_LR * (m_hat / (jnp.sqrt(v_hat) + ADAM_EPS) + ADAM_WD * w)
    return delta, m, v


def _adamw(w, g, m, v):
    R, C = w.shape
    tr = _row_tile(R, C)

    def body(w_ref, g_ref, m_ref, v_ref, d_ref, mo_ref, vo_ref):
        d_ref[...], mo_ref[...], vo_ref[...] = _adamw_math(w_ref[...], g_ref[...], m_ref[...], v_ref[...])

    spec = pl.BlockSpec((tr, C), lambda i: (i, 0))
    shp = jax.ShapeDtypeStruct((R, C), F32)
    return pl.pallas_call(
        body, name="adamw", grid=(R // tr,), in_specs=[spec] * 4, out_specs=[spec] * 3,
        out_shape=[shp] * 3, compiler_params=_cparams(1),
    )(w, g, m, v)


def _adamw_halves(w, mine, theirs, m, v, cidx, rider=None):
    _, _, R2, C = w.shape
    tr = _row_tile(R2, C)
    nr = R2 // tr

    def body(c_ref, w_ref, a0_ref, b0_ref, a1_ref, b1_ref, m_ref, v_ref, g_ref, d_ref, mo_ref, vo_ref):
        own = pl.program_id(1) == c_ref[0]
        g0 = jnp.where(own, a0_ref[...], b0_ref[...])
        g1 = jnp.where(own, a1_ref[...], b1_ref[...])
        g = jnp.where(pl.program_id(0) == 0, g0, g1)
        g_ref[...] = g
        d_ref[...], mo_ref[...], vo_ref[...] = _adamw_math(w_ref[...], g, m_ref[...], v_ref[...])

    full = pl.BlockSpec((None, None, tr, C), lambda l, h, i, c: (l, h, i, 0))

    def pick(layer, mine_side):
        def index(l, h, i, c):
            used = (l == layer) & ((h == c[0]) == mine_side)
            return (jnp.where(used, i, 0), 0)
        return pl.BlockSpec((tr, C), index)

    shp = jax.ShapeDtypeStruct(w.shape, F32)
    r_in, r_out, r_shapes, r_sems, r_args = _rider_specs(rider)
    last = lambda: (pl.program_id(0) == 1) & (pl.program_id(1) == 1) & (pl.program_id(2) == nr - 1)
    first = lambda: (pl.program_id(0) == 0) & (pl.program_id(1) == 0) & (pl.program_id(2) == 0)
    return pl.pallas_call(
        _ride(body, 8, 4, rider, first, last),
        name="adamw_halves" + ("" if rider is None else "_ride"),
        grid_spec=pltpu.PrefetchScalarGridSpec(
            num_scalar_prefetch=1, grid=(2, 2, nr),
            in_specs=[full, pick(0, True), pick(0, False), pick(1, True), pick(1, False), full, full] + r_in,
            out_specs=[full] * 4 + r_out,
            scratch_shapes=r_sems,
        ),
        out_shape=[shp] * 4 + r_shapes,
        compiler_params=_cparams(3),
    )(cidx, w, mine[0], theirs[0], mine[1], theirs[1], m, v, *r_args)


_ANY = pl.BlockSpec(memory_space=pl.ANY)


def _mesh_pos():
    return lax.axis_index("x"), lax.axis_index("y"), lax.axis_index("c")


def _other_chips(x, y):
    return [(2 * x + (1 - y), x, 1 - y), (2 * (1 - x) + y, 1 - x, y), (2 * (1 - x) + (1 - y), 1 - x, 1 - y)]


def _gathered_shapes(shards):
    return [jax.ShapeDtypeStruct((2, N_CHIPS) + s.shape[1:], s.dtype) for s in shards]


def _gather_sems(n):
    return [pltpu.SemaphoreType.DMA((2 * n,)), pltpu.SemaphoreType.DMA((6 * n,)), pltpu.SemaphoreType.DMA((6 * n,))]


def _gather_steps(ins, outs, lsem, ssem, rsem):
    n = len(ins)
    x, y, c = _mesh_pos()
    me = 2 * x + y
    sib = (x, y, 1 - c)
    chips = _other_chips(x, y)

    def ici(k, d):
        return pltpu.make_async_remote_copy(
            ins[k].at[c], outs[k].at[c, me], ssem.at[6 * k + d], rsem.at[6 * k + d],
            device_id=(chips[d][1], chips[d][2], c), device_id_type=MESH_ID)

    def landed(k, d):
        return pltpu.make_async_remote_copy(
            ins[k].at[c], outs[k].at[c, chips[d][0]], ssem.at[6 * k + d], rsem.at[6 * k + d],
            device_id=sib, device_id_type=MESH_ID)

    def fwd(k, d, half):
        return pltpu.make_async_remote_copy(
            outs[k].at[half, chips[d][0]], outs[k].at[half, chips[d][0]], ssem.at[6 * k + 3 + d],
            rsem.at[6 * k + 3 + d], device_id=sib, device_id_type=MESH_ID)

    def local(k, h):
        return pltpu.make_async_copy(ins[k].at[h], outs[k].at[h, me], lsem.at[2 * k + h])

    def start():
        for k in range(n):
            for h in range(2):
                local(k, h).start()
            for d in range(3):
                ici(k, d).start()

    def mid():
        for d in range(3):
            for k in range(n):
                landed(k, d).wait_recv()
                fwd(k, d, c).start()

    def end():
        for d in range(3):
            for k in range(n):
                fwd(k, d, 1 - c).wait_recv()
        for k in range(n):
            for d in range(3):
                ici(k, d).wait_send()
                fwd(k, d, c).wait_send()
            for h in range(2):
                local(k, h).wait()

    return start, mid, end


def _gather_rider(shards):
    return _Rider(shards, _gathered_shapes(shards), _gather_sems(len(shards)), _gather_steps)


def _pair_rider(arrs, other_half):
    n = len(arrs)

    def steps(ins, outs, ssem, rsem):
        x, y, c = _mesh_pos()

        def copy(k):
            return pltpu.make_async_remote_copy(ins[k].at[:, 1 - c] if other_half else ins[k], outs[k], ssem.at[k],
                                                rsem.at[k], device_id=(x, y, 1 - c), device_id_type=MESH_ID)

        def start():
            for k in range(n):
                copy(k).start()

        def end():
            for k in range(n):
                copy(k).wait()

        return start, end

    shapes = [jax.ShapeDtypeStruct(a.shape[:1] + a.shape[2:] if other_half else a.shape, a.dtype) for a in arrs]
    return _Rider(arrs, shapes, [pltpu.SemaphoreType.DMA((n,)), pltpu.SemaphoreType.DMA((n,))], steps)


def _chip_rider(arrs, broadcast):
    n = len(arrs)

    def steps(ins, outs, lsem, ssem, rsem):
        x, y, c = _mesh_pos()
        me = 2 * x + y

        def copies():
            cps = [pltpu.make_async_copy(ins[k] if broadcast else ins[k].at[me], outs[k].at[me], lsem.at[k])
                   for k in range(n)]
            for k in range(n):
                for d, (j, tx, ty) in enumerate(_other_chips(x, y)):
                    cps.append(pltpu.make_async_remote_copy(
                        ins[k] if broadcast else ins[k].at[j], outs[k].at[me], ssem.at[3 * k + d], rsem.at[3 * k + d],
                        device_id=(tx, ty, c), device_id_type=MESH_ID))
            return cps

        def start():
            for cp in copies():
                cp.start()

        def end():
            for cp in copies():
                cp.wait()

        return start, end

    shapes = [jax.ShapeDtypeStruct(((N_CHIPS,) + a.shape) if broadcast else a.shape, a.dtype) for a in arrs]
    sems = [pltpu.SemaphoreType.DMA((n,)), pltpu.SemaphoreType.DMA((3 * n,)), pltpu.SemaphoreType.DMA((3 * n,))]
    return _Rider(arrs, shapes, sems, steps)


def _run_rider(name, rider):
    n, m = len(rider.arrs), len(rider.out_shapes)

    def body(*refs):
        for step in rider.steps(refs[:n], refs[n:n + m], *refs[n + m:]):
            step()

    return pl.pallas_call(
        body, name=name, in_specs=[_ANY] * n, out_specs=[_ANY] * m, out_shape=rider.out_shapes,
        scratch_shapes=rider.sems,
    )(*rider.arrs)


SMALL = ("norm_g", "lam_re", "lam_im", "b_re", "b_im", "c_re", "c_im", "d_skip", "log_dt", "b_glu", "ln_g", "ln_b",
         "w_s", "b_s", "w_pool", "pool_scale", "final_g")
BIG = ("w_in", "w_glu", "w_out")
WEIGHTS = ("norm_g", "w_in", "lam_re", "lam_im", "b_re", "b_im", "c_re", "c_im", "d_skip", "log_dt", "w_glu", "b_glu",
           "ln_g", "ln_b", "w_s", "b_s", "w_pool", "pool_scale", "w_out", "final_g")
PACK_UNIT = 8 * 128
PACK_ROWS = 1024


def _pack(arrs):
    parts, total = [], 0
    for a in arrs:
        f = a.reshape(-1).astype(F32)
        pad = (-f.shape[0]) % PACK_UNIT
        parts.append(jnp.pad(f, (0, pad)) if pad else f)
        total += f.shape[0] + pad
    tail = (-total) % (PACK_ROWS * 128)
    if tail:
        parts.append(jnp.zeros((tail,), F32))
    return jnp.concatenate(parts).reshape(-1, 128)


def _unpack(buf, like):
    flat = buf.reshape(-1)
    out, off = [], 0
    for a in like:
        n = math.prod(a.shape)
        out.append(flat[off:off + n].reshape(a.shape))
        off += n + ((-n) % PACK_UNIT)
    return out


def _layer_params(l, wt, g_glu):
    a_re, a_im, bb_re, bb_im = _s5_prep(wt["lam_re"][l], wt["lam_im"][l], wt["b_re"][l], wt["b_im"][l], wt["log_dt"][l])
    b4re, b4im = _block_diag_in(bb_re), _block_diag_in(bb_im)
    c4re, c4im = _block_diag_out(wt["c_re"][l]), _block_diag_out(-wt["c_im"][l])
    tr = lambda m: jnp.swapaxes(m, 1, 2).astype(BF16)
    causal = jnp.tril(jnp.ones((CHUNK, CHUNK), dtype=bool))
    ws = jnp.where(causal[None], wt["w_s"][l], 0.0)
    wglu = g_glu[l].reshape(S5_W, S5_W)
    return dict(
        b4re=b4re.astype(BF16), b4im=b4im.astype(BF16), c4re=c4re.astype(BF16), c4im=c4im.astype(BF16),
        b4re_t=tr(b4re), b4im_t=tr(b4im), c4re_t=tr(c4re), c4im_t=tr(c4im),
        dvec=wt["d_skip"][l].reshape(1, S5_W), wglu=wglu, wglu_t=wglu.T, bglu=wt["b_glu"][l].reshape(1, S5_W),
        coef_f=_scan_coefs(a_re, a_im, False), coef_r=_scan_coefs(a_re, a_im, True),
        ws=ws.astype(BF16), ws_t=tr(ws),
        bsf=jnp.broadcast_to(wt["b_s"][l][:, None, :], (SGU_HEADS, CHUNK, CHUNK)).transpose(2, 0, 1).reshape(CHUNK, SGU_W),
        lng=wt["ln_g"][l].reshape(1, SGU_W), lnb=wt["ln_b"][l].reshape(1, SGU_W),
        wp=wt["w_pool"][l].astype(BF16), wp_t=tr(wt["w_pool"][l]), scale=wt["pool_scale"][l].reshape(1, POOL_W),
        norm_g=wt["norm_g"][l].reshape(1, D_MODEL),
    )


def _local_step(x0, tgt, wt, g_in0, rest, rest_gathered, cidx=None, order=None):
    dist = cidx is not None
    xs, saved, params = [x0], [], []
    for l in range(DEPTH):
        norm_g = wt["norm_g"][l].reshape(1, D_MODEL)
        out_rider = None
        if l == 0 and not rest_gathered:
            w_in1, w_glu_b, w_out_b = rest
            h = _rms_h(xs[-1], norm_g)
            z, g_in0, g_glu, g_out = _inproj_first(h, g_in0, order, [w_glu_b, w_out_b])
            out_rider = _gather_rider([w_in1])
        elif l == 0:
            g_in1, g_glu, g_out = rest
            z, h = _inproj(xs[-1], norm_g, g_in0)
        else:
            z, h = _inproj(xs[-1], norm_g, g_in1)
        p = _layer_params(l, wt, g_glu)
        params.append(p)
        ya, yraw, sre, sim, *gathered = _s5_fwd(z, p, out_rider)
        if gathered:
            (g_in1,) = gathered
        yb = _sgu_fwd(z, p["ws"], p["bsf"], p["lng"], p["lnb"])
        yc = _pool_fwd(z, p["wp"], p["scale"])
        w_out = g_out[l].reshape(D_MODEL, D_MODEL)
        if l < DEPTH - 1:
            xn, y = _outproj(ya, yb, yc, w_out, xs[-1])
            xs.append(xn)
        else:
            dx, loss, dfg, y = _outproj_loss(ya, yb, yc, w_out, xs[-1], wt["final_g"].reshape(1, D_MODEL), tgt)
        saved.append((z, h, yraw, sre, sim, y))
    g_in = (g_in0, g_in1)

    gr = {k: [None] * DEPTH for k in WEIGHTS if k != "final_g"}
    mine, theirs, chip_sum = [None] * DEPTH, [None] * DEPTH, None
    halves = lambda a, rows: a.reshape(N_CHIPS, 2, rows // 2, a.shape[-1])
    for l in reversed(range(DEPTH)):
        p = params[l]
        z, h, yraw, sre, sim, y = saved[l]
        w_out = g_out[l].reshape(D_MODEL, D_MODEL)
        dy = _outproj_bwd_dy(dx, w_out)
        gr["w_out"][l] = _outproj_bwd_dw(y, dx)
        ride_c = _chip_rider(chip_sum, False) if dist and l == 0 else None
        dz, dbre, dbim, dcre, dcim, dd, dwg, dbg, da, *landed = _s5_bwd(dy, z, yraw, sre, sim, p, ride_c)
        if ride_c:
            mine[1] = [_sum_chips(r) for r in landed]
        ride_e = _pair_rider(mine[1], False) if dist and l == 0 else None
        dz, dws, dbsf, dlng, dlnb, *got = _sgu_bwd(dy, z, dz, p["ws"], p["ws_t"], p["bsf"], p["lng"], p["lnb"], ride_e)
        if ride_e:
            theirs[1] = got
        dz, dwp, dsc = _pool_bwd(dy, z, dz, p["wp"], p["wp_t"], p["scale"])
        gr["w_in"][l] = _inproj_bwd_dw(h, dz)
        if not dist:
            dx, dng = _inproj_bwd_dx(dz, g_in[l], xs[l], p["norm_g"], dx)
        else:
            part = [halves(gr["w_in"][l], D_MODEL), halves(dwg, S5_W // N_CHIPS),
                    halves(gr["w_out"][l], D_MODEL // N_CHIPS)]
            ride_a = _pair_rider(part, True)
            if l == 1:
                dx, dng, *from_sib = _inproj_bwd_dx(dz, g_in[l], xs[l], p["norm_g"], dx, ride_a)
                chip_sum = [_add_own_half(a, r, cidx) for a, r in zip(part, from_sib)]
            else:
                nt = x0.shape[0] // _dx_tile(x0.shape[0])
                n_top = max(nt // 4, 1)
                dx_top, dng_top, *from_sib = _inproj_bwd_dx(dz, g_in[l], xs[l], p["norm_g"], dx, ride_a,
                                                            tiles=(0, n_top))
                chip_sum0 = [_add_own_half(a, r, cidx) for a, r in zip(part, from_sib)]
                dx, dng_rest, *landed = _inproj_bwd_dx(dz, g_in[l], xs[l], p["norm_g"], dx,
                                                       _chip_rider(chip_sum0, False), tiles=(n_top, nt - n_top),
                                                       prev=dx_top)
                dng = dng_top + dng_rest
                mine[0] = [_sum_chips(r) for r in landed]

        raw = (wt["lam_re"][l], wt["lam_im"][l], wt["b_re"][l], wt["b_im"][l], wt["log_dt"][l])
        _, vjp = jax.vjp(_s5_prep, *raw)
        da = jnp.sum(da, axis=1)
        cot = (da[0].reshape(S5_GROUPS, S5_STATE), da[1].reshape(S5_GROUPS, S5_STATE),
               _block_diag_in_grad(dbre), _block_diag_in_grad(dbim))
        gr["lam_re"][l], gr["lam_im"][l], gr["b_re"][l], gr["b_im"][l], gr["log_dt"][l] = vjp(cot)
        gr["c_re"][l] = _block_diag_out_grad(dcre)
        gr["c_im"][l] = -_block_diag_out_grad(dcim)
        gr["d_skip"][l] = dd.reshape(S5_GROUPS, S5_CH)
        gr["w_glu"][l] = dwg
        gr["b_glu"][l] = dbg.reshape(S5_W)
        causal = jnp.tril(jnp.ones((CHUNK, CHUNK), dtype=bool))
        gr["w_s"][l] = jnp.where(causal[None], dws, 0.0)
        gr["b_s"][l] = dbsf.reshape(CHUNK, SGU_HEADS, CHUNK).sum(-1).T
        gr["ln_g"][l] = dlng.reshape(SGU_W)
        gr["ln_b"][l] = dlnb.reshape(SGU_W)
        gr["w_pool"][l] = dwp
        gr["pool_scale"][l] = dsc.reshape(POOL_W)
        gr["norm_g"][l] = dng.reshape(D_MODEL)

    grads = {k: (v if k in BIG else jnp.stack(v)) for k, v in gr.items()}
    grads["final_g"] = dfg.reshape(D_MODEL)
    if dist:
        for i, k in enumerate(BIG):
            grads[k] = ([mine[l][i] for l in range(DEPTH)], [None, theirs[1][i]])
    return loss, dx, grads


def kernel(x, norm_g, w_in, lam_re, lam_im, b_re, b_im, c_re, c_im, d_skip, log_dt, w_glu, b_glu, ln_g, ln_b, w_s, b_s, w_pool, pool_scale, w_out, final_g, loss_target, m_norm_g, m_w_in, m_lam_re, m_lam_im, m_b_re, m_b_im, m_c_re, m_c_im, m_d_skip, m_log_dt, m_w_glu, m_b_glu, m_ln_g, m_ln_b, m_w_s, m_b_s, m_w_pool, m_pool_scale, m_w_out, m_final_g, v_norm_g, v_w_in, v_lam_re, v_lam_im, v_b_re, v_b_im, v_c_re, v_c_im, v_d_skip, v_log_dt, v_w_glu, v_b_glu, v_ln_g, v_ln_b, v_w_s, v_b_s, v_w_pool, v_pool_scale, v_w_out, v_final_g):
    wt = dict(norm_g=norm_g, w_in=w_in, lam_re=lam_re, lam_im=lam_im, b_re=b_re, b_im=b_im, c_re=c_re, c_im=c_im,
              d_skip=d_skip, log_dt=log_dt, w_glu=w_glu, b_glu=b_glu, ln_g=ln_g, ln_b=ln_b, w_s=w_s, b_s=b_s,
              w_pool=w_pool, pool_scale=pool_scale, w_out=w_out, final_g=final_g)
    mom = dict(norm_g=m_norm_g, w_in=m_w_in, lam_re=m_lam_re, lam_im=m_lam_im, b_re=m_b_re, b_im=m_b_im, c_re=m_c_re,
               c_im=m_c_im, d_skip=m_d_skip, log_dt=m_log_dt, w_glu=m_w_glu, b_glu=m_b_glu, ln_g=m_ln_g, ln_b=m_ln_b,
               w_s=m_w_s, b_s=m_b_s, w_pool=m_w_pool, pool_scale=m_pool_scale, w_out=m_w_out, final_g=m_final_g)
    vel = dict(norm_g=v_norm_g, w_in=v_w_in, lam_re=v_lam_re, lam_im=v_lam_im, b_re=v_b_re, b_im=v_b_im, c_re=v_c_re,
               c_im=v_c_im, d_skip=v_d_skip, log_dt=v_log_dt, w_glu=v_w_glu, b_glu=v_b_glu, ln_g=v_ln_g, ln_b=v_ln_b,
               w_s=v_w_s, b_s=v_b_s, w_pool=v_w_pool, pool_scale=v_pool_scale, w_out=v_w_out, final_g=v_final_g)
    T = x.shape[1]
    cidx = lax.axis_index("c").astype(jnp.int32).reshape(1)

    w_in_b = w_in.astype(BF16)
    w0 = w_in_b[0].reshape(2, HALF_D, SHARD_COLS)
    rest = (w_in_b[1].reshape(2, HALF_D, SHARD_COLS), w_glu.astype(BF16), w_out.astype(BF16))
    mx, my = lax.axis_index("x"), lax.axis_index("y")
    order = jnp.stack([2 * mx + my] + [j for j, _, _ in _other_chips(mx, my)]).astype(jnp.int32)
    loss, grad_x, grads = _local_step(x.reshape(T, D_MODEL), loss_target.reshape(T, D_MODEL), wt, w0, rest, False,
                                      cidx, order)

    packed = _pack([grads[k] for k in SMALL] + [loss[0, 0:1]])
    sib_packed, *theirs0 = _run_rider("pair_exchange", _pair_rider([packed] + [grads[k][0][0] for k in BIG], False))
    for k, t in zip(BIG, theirs0):
        grads[k][1][0] = t
    chip_packed = _add2(packed, sib_packed)
    half_rows = chip_packed.shape[0] // 2
    my_half = lax.dynamic_index_in_dim(chip_packed.reshape(2, half_rows, 128), cidx[0], 0, keepdims=False)
    small_ride = _chip_rider([my_half], True)

    out_g, out_d, out_m, out_v = {}, {}, {}, {}
    all_half = None
    for k in BIG:
        shape = wt[k].shape
        quad = lambda t: t.reshape(2, 2, shape[1] // 2, shape[2])
        g, d, m, v, *landed = _adamw_halves(quad(wt[k]), grads[k][0], grads[k][1], quad(mom[k]), quad(vel[k]), cidx,
                                            small_ride if k == BIG[0] else None)
        if landed:
            (all_half,) = landed
        out_g[k], out_d[k], out_m[k], out_v[k] = (t.reshape(shape) for t in (g, d, m, v))

    mine_half = _sum_chips(all_half)
    (their_half,) = _run_rider("small_result_exchange", _pair_rider([mine_half], False))
    total = jnp.where(cidx[0] == 0, jnp.concatenate([mine_half, their_half]), jnp.concatenate([their_half, mine_half]))
    like = [wt[k] for k in SMALL]
    small_g = _unpack(total, like + [loss[0, 0:1]])
    loss_out = small_g[-1].reshape(())
    w_p, m_p, v_p = _pack(like), _pack([mom[k] for k in SMALL]), _pack([vel[k] for k in SMALL])
    d_p, mo_p, vo_p = _adamw(w_p, total, m_p, v_p)
    for k, g, d, m, v in zip(SMALL, small_g[:-1], _unpack(d_p, like), _unpack(mo_p, like), _unpack(vo_p, like)):
        out_g[k], out_d[k], out_m[k], out_v[k] = g, d, m, v

    return (loss_out, grad_x.reshape(x.shape), *[out_g[k] for k in WEIGHTS], *[out_d[k] for k in WEIGHTS],
            *[out_m[k] for k in WEIGHTS], *[out_v[k] for k in WEIGHTS])
```
